```python
import math
import jax, jax.numpy as jnp
from jax import lax
import numpy as np

D_MODEL = 1024
BATCH = 16
SEQ = 2048
DEPTH = 4

D_FF = 2816
NORM_EPS = 1e-6
F_MIN = 1e-6
CHUNK = 64
HG_HEADS = 4
HG_DK = 128
HG_DV = 128
HG_QK = HG_HEADS * HG_DK
HG_WIDTH = HG_HEADS * HG_DV
S5_WIDTH = D_MODEL - HG_WIDTH
S5_GROUP = 16
S5_GROUPS = S5_WIDTH // S5_GROUP
S5_STATE = 64
EV_IN = 2 * HG_QK + 2 * HG_WIDTH + S5_WIDTH
GDN_HEADS = 8
GDN_DK = 128
GDN_DV = 128
GDN_QK = GDN_HEADS * GDN_DK
GDN_V = GDN_HEADS * GDN_DV
CONV_W = 4
OD_IN = 2 * GDN_QK + 2 * GDN_V + 2 * GDN_HEADS
N_EVEN = (DEPTH + 1) // 2
N_ODD = DEPTH // 2

kernel_name = 'hybrid_hgrn2_s5_gdn_macaron'


def rmsnorm(x, w):
    xf = x.astype(jnp.float32)
    y = xf * lax.rsqrt(jnp.mean(xf * xf, axis=-1, keepdims=True) + NORM_EPS)
    return (y * w.astype(jnp.float32)).astype(x.dtype)


def swiglu(h, w_gate, w_up, w_down):
    return (jax.nn.silu(h @ w_gate) * (h @ w_up)) @ w_down


def split_chunks(t, n_heads):
    b, l, _ = t.shape
    t = t.reshape(b, l // CHUNK, CHUNK, n_heads, -1)
    return t.transpose(1, 0, 3, 2, 4)


def merge_chunks(t):
    nc, b, h, c, d = t.shape
    return t.transpose(1, 0, 3, 2, 4).reshape(b, nc * c, h, d)


def masked_exp(mask, z):
    return jnp.where(mask, jnp.exp(jnp.where(mask, z, 0.0)), 0.0)


def gated_head_norm(o, gate, w):
    o = o * lax.rsqrt(jnp.mean(o * o, axis=-1, keepdims=True) + NORM_EPS) * w.astype(jnp.float32)
    o = o * jax.nn.silu(gate.astype(jnp.float32)).reshape(o.shape)
    return o.reshape(o.shape[0], o.shape[1], -1)


def l2norm(t):
    return t * lax.rsqrt(jnp.sum(t * t, axis=-1, keepdims=True) + NORM_EPS)


def hgrn2_mix(q_lin, f_lin, i_val, g_lin, lb, norm_w):
    f32 = jnp.float32
    q = jax.nn.silu(q_lin.astype(f32))
    f = lb + (1.0 - lb) * jax.nn.sigmoid(f_lin.astype(f32))
    log_f = jnp.log(jnp.maximum(f, F_MIN))
    k = 1.0 - f
    v = i_val.astype(f32)
    qc = split_chunks(q, HG_HEADS)
    kc = split_chunks(k, HG_HEADS)
    vc = split_chunks(v, HG_HEADS)
    lfc = split_chunks(log_f, HG_HEADS)
    causal = jnp.tril(jnp.ones((CHUNK, CHUNK), dtype=bool))[:, :, None]
    s0 = jnp.zeros((q.shape[0], HG_HEADS, HG_DK, HG_DV), f32)

    def step(S, inp):
        q_c, k_c, v_c, lf_c = inp
        b = jnp.cumsum(lf_c, axis=2)
        diff = b[:, :, :, None, :] - b[:, :, None, :, :]
        decay = masked_exp(causal, diff)
        attn = jnp.einsum('bhtk,bhtsk,bhsk->bhts', q_c, decay, k_c)
        o = attn @ v_c + jnp.einsum('bhtk,bhkv->bhtv', q_c * jnp.exp(b), S)
        b_last = b[:, :, -1:, :]
        S = jnp.exp(b_last[:, :, 0, :, None]) * S + jnp.einsum(
            'bhsk,bhsv->bhkv', k_c * jnp.exp(b_last - b), v_c)
        return S, o

    _, o = lax.scan(step, s0, (qc, kc, vc, lfc))
    return gated_head_norm(merge_chunks(o), g_lin, norm_w)


def s5_mix(u, a_re, a_im, b_re, b_im, c_re, c_im, d, log_dt, w_glu):
    f32 = jnp.float32
    bsz, l, _ = u.shape
    uf = u.astype(f32).reshape(bsz, l, S5_GROUPS, S5_GROUP)
    a_re = a_re.astype(f32)
    a_im = a_im.astype(f32)
    dt = jnp.exp(log_dt.astype(f32))[:, None]
    mag = jnp.exp(dt * a_re)
    ang = dt * a_im
    abar_re = mag * jnp.cos(ang)
    abar_im = mag * jnp.sin(ang)
    den = a_re * a_re + a_im * a_im
    zr = abar_re - 1.0
    zi = abar_im
    coef_re = ((zr * a_re + zi * a_im) / den)[..., None]
    coef_im = ((zi * a_re - zr * a_im) / den)[..., None]
    b_re = b_re.astype(f32)
    b_im = b_im.astype(f32)
    bb_re = coef_re * b_re - coef_im * b_im
    bb_im = coef_re * b_im + coef_im * b_re
    bu_re = jnp.einsum('blgp,gnp->blgn', uf, bb_re)
    bu_im = jnp.einsum('blgp,gnp->blgn', uf, bb_im)
    ar = jnp.broadcast_to(abar_re, (1, l, S5_GROUPS, S5_STATE))
    ai = jnp.broadcast_to(abar_im, (1, l, S5_GROUPS, S5_STATE))

    def combine(e1, e2):
        ar1, ai1, br1, bi1 = e1
        ar2, ai2, br2, bi2 = e2
        return (ar2 * ar1 - ai2 * ai1,
                ar2 * ai1 + ai2 * ar1,
                ar2 * br1 - ai2 * bi1 + br2,
                ar2 * bi1 + ai2 * br1 + bi2)

    _, _, h_re, h_im = lax.associative_scan(combine, (ar, ai, bu_re, bu_im), axis=1)
    y = (jnp.einsum('gpn,blgn->blgp', c_re.astype(f32), h_re)
         - jnp.einsum('gpn,blgn->blgp', c_im.astype(f32), h_im)
         + d.astype(f32).reshape(S5_GROUPS, S5_GROUP) * uf)
    y = jax.nn.gelu(y.reshape(bsz, l, S5_WIDTH))
    return y * jax.nn.sigmoid(y @ w_glu.astype(f32))


def causal_dwconv(x, w):
    return lax.conv_general_dilated(
        x, w[:, None, :].astype(x.dtype), window_strides=(1,), padding=[(CONV_W - 1, 0)],
        dimension_numbers=('NWC', 'WIO', 'NWC'), feature_group_count=x.shape[-1])


def gdn_mix(proj, conv_w, a_log, dt_bias, norm_w):
    f32 = jnp.float32
    n_qkv = 2 * GDN_QK + GDN_V
    qkv = jax.nn.silu(causal_dwconv(proj[..., :n_qkv], conv_w)).astype(f32)
    gate = proj[..., n_qkv:n_qkv + GDN_V]
    beta = jax.nn.sigmoid(proj[..., n_qkv + GDN_V:n_qkv + GDN_V + GDN_HEADS].astype(f32))
    a_lin = proj[..., n_qkv + GDN_V + GDN_HEADS:].astype(f32)
    log_alpha = -jnp.exp(a_log.astype(f32)) * jax.nn.softplus(a_lin + dt_bias.astype(f32))
    q = split_chunks(qkv[..., :GDN_QK], GDN_HEADS)
    k = split_chunks(qkv[..., GDN_QK:2 * GDN_QK], GDN_HEADS)
    v = split_chunks(qkv[..., 2 * GDN_QK:], GDN_HEADS)
    q = l2norm(q) * (GDN_DK ** -0.5)
    k = l2norm(k)
    beta = split_chunks(beta, GDN_HEADS)[..., 0]
    g = jnp.cumsum(split_chunks(log_alpha, GDN_HEADS)[..., 0], axis=-1)
    lower = jnp.tril(jnp.ones((CHUNK, CHUNK), dtype=bool))
    strict = jnp.tril(jnp.ones((CHUNK, CHUNK), dtype=bool), k=-1)
    l_mask = masked_exp(lower, g[..., :, None] - g[..., None, :])
    kb = k * beta[..., None]
    vb = v * beta[..., None]
    m = jnp.where(strict, jnp.einsum('...ik,...jk->...ij', kb, k) * l_mask, 0.0)
    eye = jnp.eye(CHUNK, dtype=f32)
    t_inv = lax.linalg.triangular_solve(eye + m, jnp.broadcast_to(eye, m.shape),
                                        left_side=True, lower=True, unit_diagonal=True)
    u_c = t_inv @ vb
    w_c = t_inv @ (kb * jnp.exp(g)[..., None])
    attn = jnp.einsum('...ik,...jk->...ij', q, k) * l_mask
    s0 = jnp.zeros((proj.shape[0], GDN_HEADS, GDN_DK, GDN_DV), f32)

    def step(S, inp):
        q_c, k_c, uu, ww, at, g_c = inp
        v_new = uu - ww @ S
        o = (q_c * jnp.exp(g_c)[..., None]) @ S + at @ v_new
        g_last = g_c[..., -1:]
        S = S * jnp.exp(g_last)[..., None] + jnp.einsum(
            'bhsk,bhsv->bhkv', k_c * jnp.exp(g_last - g_c)[..., None], v_new)
        return S, o

    _, o = lax.scan(step, s0, (q, k, u_c, w_c, attn, g))
    return gated_head_norm(merge_chunks(o), gate, norm_w)


def _fwd_setup_inputs(seed: int = 0) -> dict:
    key = jax.random.key(seed)
    ks = jax.random.split(key, 32)
    f32 = jnp.float32

    def nrm(k, shape, scale):
        return jax.random.normal(k, shape, f32) * scale

    def gain(k, shape):
        return 1.0 + 0.02 * jax.random.normal(k, shape, f32)

    n_idx = jnp.arange(S5_STATE, dtype=f32)
    gdn_dt = jnp.exp(jax.random.uniform(ks[27], (N_ODD, GDN_HEADS), f32, math.log(1e-3), math.log(1e-1)))
    return {
        'x': nrm(ks[0], (BATCH, SEQ, D_MODEL), 1.0),
        'ffn1_norm': gain(ks[1], (DEPTH, D_MODEL)),
        'ffn1_w_gate': nrm(ks[2], (DEPTH, D_MODEL, D_FF), D_MODEL ** -0.5),
        'ffn1_w_up': nrm(ks[3], (DEPTH, D_MODEL, D_FF), D_MODEL ** -0.5),
        'ffn1_w_down': nrm(ks[4], (DEPTH, D_FF, D_MODEL), D_FF ** -0.5),
        'mix_norm': gain(ks[5], (DEPTH, D_MODEL)),
        'ffn2_norm': gain(ks[6], (DEPTH, D_MODEL)),
        'ffn2_w_gate': nrm(ks[7], (DEPTH, D_MODEL, D_FF), D_MODEL ** -0.5),
        'ffn2_w_up': nrm(ks[8], (DEPTH, D_MODEL, D_FF), D_MODEL ** -0.5),
        'ffn2_w_down': nrm(ks[9], (DEPTH, D_FF, D_MODEL), D_FF ** -0.5),
        'ev_w_in': nrm(ks[10], (N_EVEN, D_MODEL, EV_IN), D_MODEL ** -0.5),
        'hg_lb_logits': nrm(ks[11], (N_EVEN, HG_QK), 0.1),
        'hg_norm_w': gain(ks[12], (N_EVEN, HG_DV)),
        's5_a_re': -0.5 + nrm(ks[13], (N_EVEN, S5_GROUPS, S5_STATE), 0.01),
        's5_a_im': math.pi * n_idx + nrm(ks[14], (N_EVEN, S5_GROUPS, S5_STATE), 0.01),
        's5_b_re': nrm(ks[15], (N_EVEN, S5_GROUPS, S5_STATE, S5_GROUP), (2 * S5_GROUP) ** -0.5),
        's5_b_im': nrm(ks[16], (N_EVEN, S5_GROUPS, S5_STATE, S5_GROUP), (2 * S5_GROUP) ** -0.5),
        's5_c_re': nrm(ks[17], (N_EVEN, S5_GROUPS, S5_GROUP, S5_STATE), (2 * S5_STATE) ** -0.5),
        's5_c_im': nrm(ks[18], (N_EVEN, S5_GROUPS, S5_GROUP, S5_STATE), (2 * S5_STATE) ** -0.5),
        's5_d': nrm(ks[19], (N_EVEN, S5_WIDTH), 1.0),
        's5_log_dt': jax.random.uniform(ks[20], (N_EVEN, S5_GROUPS), f32, math.log(1e-3), math.log(1e-1)),
        's5_w_glu': nrm(ks[21], (N_EVEN, S5_WIDTH, S5_WIDTH), S5_WIDTH ** -0.5),
        'ev_w_out': nrm(ks[22], (N_EVEN, HG_WIDTH + S5_WIDTH, D_MODEL), (HG_WIDTH + S5_WIDTH) ** -0.5),
        'od_w_in': nrm(ks[23], (N_ODD, D_MODEL, OD_IN), D_MODEL ** -0.5),
        'gdn_conv_w': nrm(ks[24], (N_ODD, CONV_W, 2 * GDN_QK + GDN_V), CONV_W ** -0.5),
        'gdn_a_log': jnp.log(jax.random.uniform(ks[25], (N_ODD, GDN_HEADS), f32, 1.0, 16.0)),
        'gdn_dt_bias': gdn_dt + jnp.log(-jnp.expm1(-gdn_dt)),
        'gdn_norm_w': gain(ks[26], (N_ODD, GDN_DV)),
        'od_w_out': nrm(ks[28], (N_ODD, GDN_V, D_MODEL), GDN_V ** -0.5),
        'final_norm': gain(ks[29], (D_MODEL,)),
    }


def _fwd_reference(x, ffn1_norm, ffn1_w_gate, ffn1_w_up, ffn1_w_down, mix_norm,
              ffn2_norm, ffn2_w_gate, ffn2_w_up, ffn2_w_down,
              ev_w_in, hg_lb_logits, hg_norm_w, s5_a_re, s5_a_im, s5_b_re, s5_b_im,
              s5_c_re, s5_c_im, s5_d, s5_log_dt, s5_w_glu, ev_w_out,
              od_w_in, gdn_conv_w, gdn_a_log, gdn_dt_bias, gdn_norm_w, od_w_out,
              final_norm):
    p = jax.nn.softmax(hg_lb_logits.astype(jnp.float32), axis=0)
    lbs = jnp.cumsum(p, axis=0) - p[0]
    for layer in range(DEPTH):
        x = x + 0.5 * swiglu(rmsnorm(x, ffn1_norm[layer]), ffn1_w_gate[layer],
                             ffn1_w_up[layer], ffn1_w_down[layer])
        h = rmsnorm(x, mix_norm[layer])
        j = layer // 2
        if layer % 2 == 0:
            proj = h @ ev_w_in[j]
            y_a = hgrn2_mix(proj[..., :HG_QK],
                            proj[..., HG_QK:2 * HG_QK],
                            proj[..., 2 * HG_QK:2 * HG_QK + HG_WIDTH],
                            proj[..., 2 * HG_QK + HG_WIDTH:2 * HG_QK + 2 * HG_WIDTH],
                            lbs[j], hg_norm_w[j])
            y_b = s5_mix(proj[..., 2 * HG_QK + 2 * HG_WIDTH:], s5_a_re[j], s5_a_im[j],
                         s5_b_re[j], s5_b_im[j], s5_c_re[j], s5_c_im[j], s5_d[j],
                         s5_log_dt[j], s5_w_glu[j])
            y = (jnp.concatenate([y_a, y_b], axis=-1) @ ev_w_out[j]).astype(x.dtype)
        else:
            proj = h @ od_w_in[j]
            y = (gdn_mix(proj, gdn_conv_w[j], gdn_a_log[j], gdn_dt_bias[j], gdn_norm_w[j])
                 @ od_w_out[j]).astype(x.dtype)
        x = x + y
        x = x + 0.5 * swiglu(rmsnorm(x, ffn2_norm[layer]), ffn2_w_gate[layer],
                             ffn2_w_up[layer], ffn2_w_down[layer])
    return rmsnorm(x, final_norm)


import jax as _jax
import jax.numpy as _jnp

TWIN_FORMAT = 'train_step'
FWD_PARAMS = ['x', 'ffn1_norm', 'ffn1_w_gate', 'ffn1_w_up', 'ffn1_w_down', 'mix_norm', 'ffn2_norm', 'ffn2_w_gate', 'ffn2_w_up', 'ffn2_w_down', 'ev_w_in', 'hg_lb_logits', 'hg_norm_w', 's5_a_re', 's5_a_im', 's5_b_re', 's5_b_im', 's5_c_re', 's5_c_im', 's5_d', 's5_log_dt', 's5_w_glu', 'ev_w_out', 'od_w_in', 'gdn_conv_w', 'gdn_a_log', 'gdn_dt_bias', 'gdn_norm_w', 'od_w_out', 'final_norm']
TWIN_WEIGHTS = ['ffn1_norm', 'ffn1_w_gate', 'ffn1_w_up', 'ffn1_w_down', 'mix_norm', 'ffn2_norm', 'ffn2_w_gate', 'ffn2_w_up', 'ffn2_w_down', 'ev_w_in', 'hg_lb_logits', 'hg_norm_w', 's5_a_re', 's5_a_im', 's5_b_re', 's5_b_im', 's5_c_re', 's5_c_im', 's5_d', 's5_log_dt', 's5_w_glu', 'ev_w_out', 'od_w_in', 'gdn_conv_w', 'gdn_a_log', 'gdn_dt_bias', 'gdn_norm_w', 'od_w_out', 'final_norm']
TWIN_DIFF_INPUT = 'x'
TWIN_INPUTS = ['x', 'ffn1_norm', 'ffn1_w_gate', 'ffn1_w_up', 'ffn1_w_down', 'mix_norm', 'ffn2_norm', 'ffn2_w_gate', 'ffn2_w_up', 'ffn2_w_down', 'ev_w_in', 'hg_lb_logits', 'hg_norm_w', 's5_a_re', 's5_a_im', 's5_b_re', 's5_b_im', 's5_c_re', 's5_c_im', 's5_d', 's5_log_dt', 's5_w_glu', 'ev_w_out', 'od_w_in', 'gdn_conv_w', 'gdn_a_log', 'gdn_dt_bias', 'gdn_norm_w', 'od_w_out', 'final_norm', 'loss_target', 'm_ffn1_norm', 'm_ffn1_w_gate', 'm_ffn1_w_up', 'm_ffn1_w_down', 'm_mix_norm', 'm_ffn2_norm', 'm_ffn2_w_gate', 'm_ffn2_w_up', 'm_ffn2_w_down', 'm_ev_w_in', 'm_hg_lb_logits', 'm_hg_norm_w', 'm_s5_a_re', 'm_s5_a_im', 'm_s5_b_re', 'm_s5_b_im', 'm_s5_c_re', 'm_s5_c_im', 'm_s5_d', 'm_s5_log_dt', 'm_s5_w_glu', 'm_ev_w_out', 'm_od_w_in', 'm_gdn_conv_w', 'm_gdn_a_log', 'm_gdn_dt_bias', 'm_gdn_norm_w', 'm_od_w_out', 'm_final_norm', 'v_ffn1_norm', 'v_ffn1_w_gate', 'v_ffn1_w_up', 'v_ffn1_w_down', 'v_mix_norm', 'v_ffn2_norm', 'v_ffn2_w_gate', 'v_ffn2_w_up', 'v_ffn2_w_down', 'v_ev_w_in', 'v_hg_lb_logits', 'v_hg_norm_w', 'v_s5_a_re', 'v_s5_a_im', 'v_s5_b_re', 'v_s5_b_im', 'v_s5_c_re', 'v_s5_c_im', 'v_s5_d', 'v_s5_log_dt', 'v_s5_w_glu', 'v_ev_w_out', 'v_od_w_in', 'v_gdn_conv_w', 'v_gdn_a_log', 'v_gdn_dt_bias', 'v_gdn_norm_w', 'v_od_w_out', 'v_final_norm']
TWIN_OUTPUTS = ['loss', 'grad_x', 'grad_ffn1_norm', 'grad_ffn1_w_gate', 'grad_ffn1_w_up', 'grad_ffn1_w_down', 'grad_mix_norm', 'grad_ffn2_norm', 'grad_ffn2_w_gate', 'grad_ffn2_w_up', 'grad_ffn2_w_down', 'grad_ev_w_in', 'grad_hg_lb_logits', 'grad_hg_norm_w', 'grad_s5_a_re', 'grad_s5_a_im', 'grad_s5_b_re', 'grad_s5_b_im', 'grad_s5_c_re', 'grad_s5_c_im', 'grad_s5_d', 'grad_s5_log_dt', 'grad_s5_w_glu', 'grad_ev_w_out', 'grad_od_w_in', 'grad_gdn_conv_w', 'grad_gdn_a_log', 'grad_gdn_dt_bias', 'grad_gdn_norm_w', 'grad_od_w_out', 'grad_final_norm', 'delta_ffn1_norm', 'delta_ffn1_w_gate', 'delta_ffn1_w_up', 'delta_ffn1_w_down', 'delta_mix_norm', 'delta_ffn2_norm', 'delta_ffn2_w_gate', 'delta_ffn2_w_up', 'delta_ffn2_w_down', 'delta_ev_w_in', 'delta_hg_lb_logits', 'delta_hg_norm_w', 'delta_s5_a_re', 'delta_s5_a_im', 'delta_s5_b_re', 'delta_s5_b_im', 'delta_s5_c_re', 'delta_s5_c_im', 'delta_s5_d', 'delta_s5_log_dt', 'delta_s5_w_glu', 'delta_ev_w_out', 'delta_od_w_in', 'delta_gdn_conv_w', 'delta_gdn_a_log', 'delta_gdn_dt_bias', 'delta_gdn_norm_w', 'delta_od_w_out', 'delta_final_norm', 'new_m_ffn1_norm', 'new_m_ffn1_w_gate', 'new_m_ffn1_w_up', 'new_m_ffn1_w_down', 'new_m_mix_norm', 'new_m_ffn2_norm', 'new_m_ffn2_w_gate', 'new_m_ffn2_w_up', 'new_m_ffn2_w_down', 'new_m_ev_w_in', 'new_m_hg_lb_logits', 'new_m_hg_norm_w', 'new_m_s5_a_re', 'new_m_s5_a_im', 'new_m_s5_b_re', 'new_m_s5_b_im', 'new_m_s5_c_re', 'new_m_s5_c_im', 'new_m_s5_d', 'new_m_s5_log_dt', 'new_m_s5_w_glu', 'new_m_ev_w_out', 'new_m_od_w_in', 'new_m_gdn_conv_w', 'new_m_gdn_a_log', 'new_m_gdn_dt_bias', 'new_m_gdn_norm_w', 'new_m_od_w_out', 'new_m_final_norm', 'new_v_ffn1_norm', 'new_v_ffn1_w_gate', 'new_v_ffn1_w_up', 'new_v_ffn1_w_down', 'new_v_mix_norm', 'new_v_ffn2_norm', 'new_v_ffn2_w_gate', 'new_v_ffn2_w_up', 'new_v_ffn2_w_down', 'new_v_ev_w_in', 'new_v_hg_lb_logits', 'new_v_hg_norm_w', 'new_v_s5_a_re', 'new_v_s5_a_im', 'new_v_s5_b_re', 'new_v_s5_b_im', 'new_v_s5_c_re', 'new_v_s5_c_im', 'new_v_s5_d', 'new_v_s5_log_dt', 'new_v_s5_w_glu', 'new_v_ev_w_out', 'new_v_od_w_in', 'new_v_gdn_conv_w', 'new_v_gdn_a_log', 'new_v_gdn_dt_bias', 'new_v_gdn_norm_w', 'new_v_od_w_out', 'new_v_final_norm']
TWIN_LEAF_KINDS = {'loss': 'loss', 'grad_x': 'grad_x', 'grad_ffn1_norm': 'grad_w', 'grad_ffn1_w_gate': 'grad_w', 'grad_ffn1_w_up': 'grad_w', 'grad_ffn1_w_down': 'grad_w', 'grad_mix_norm': 'grad_w', 'grad_ffn2_norm': 'grad_w', 'grad_ffn2_w_gate': 'grad_w', 'grad_ffn2_w_up': 'grad_w', 'grad_ffn2_w_down': 'grad_w', 'grad_ev_w_in': 'grad_w', 'grad_hg_lb_logits': 'grad_w', 'grad_hg_norm_w': 'grad_w', 'grad_s5_a_re': 'grad_w', 'grad_s5_a_im': 'grad_w', 'grad_s5_b_re': 'grad_w', 'grad_s5_b_im': 'grad_w', 'grad_s5_c_re': 'grad_w', 'grad_s5_c_im': 'grad_w', 'grad_s5_d': 'grad_w', 'grad_s5_log_dt': 'grad_w', 'grad_s5_w_glu': 'grad_w', 'grad_ev_w_out': 'grad_w', 'grad_od_w_in': 'grad_w', 'grad_gdn_conv_w': 'grad_w', 'grad_gdn_a_log': 'grad_w', 'grad_gdn_dt_bias': 'grad_w', 'grad_gdn_norm_w': 'grad_w', 'grad_od_w_out': 'grad_w', 'grad_final_norm': 'grad_w', 'delta_ffn1_norm': 'delta_w', 'delta_ffn1_w_gate': 'delta_w', 'delta_ffn1_w_up': 'delta_w', 'delta_ffn1_w_down': 'delta_w', 'delta_mix_norm': 'delta_w', 'delta_ffn2_norm': 'delta_w', 'delta_ffn2_w_gate': 'delta_w', 'delta_ffn2_w_up': 'delta_w', 'delta_ffn2_w_down': 'delta_w', 'delta_ev_w_in': 'delta_w', 'delta_hg_lb_logits': 'delta_w', 'delta_hg_norm_w': 'delta_w', 'delta_s5_a_re': 'delta_w', 'delta_s5_a_im': 'delta_w', 'delta_s5_b_re': 'delta_w', 'delta_s5_b_im': 'delta_w', 'delta_s5_c_re': 'delta_w', 'delta_s5_c_im': 'delta_w', 'delta_s5_d': 'delta_w', 'delta_s5_log_dt': 'delta_w', 'delta_s5_w_glu': 'delta_w', 'delta_ev_w_out': 'delta_w', 'delta_od_w_in': 'delta_w', 'delta_gdn_conv_w': 'delta_w', 'delta_gdn_a_log': 'delta_w', 'delta_gdn_dt_bias': 'delta_w', 'delta_gdn_norm_w': 'delta_w', 'delta_od_w_out': 'delta_w', 'delta_final_norm': 'delta_w', 'new_m_ffn1_norm': 'new_m', 'new_m_ffn1_w_gate': 'new_m', 'new_m_ffn1_w_up': 'new_m', 'new_m_ffn1_w_down': 'new_m', 'new_m_mix_norm': 'new_m', 'new_m_ffn2_norm': 'new_m', 'new_m_ffn2_w_gate': 'new_m', 'new_m_ffn2_w_up': 'new_m', 'new_m_ffn2_w_down': 'new_m', 'new_m_ev_w_in': 'new_m', 'new_m_hg_lb_logits': 'new_m', 'new_m_hg_norm_w': 'new_m', 'new_m_s5_a_re': 'new_m', 'new_m_s5_a_im': 'new_m', 'new_m_s5_b_re': 'new_m', 'new_m_s5_b_im': 'new_m', 'new_m_s5_c_re': 'new_m', 'new_m_s5_c_im': 'new_m', 'new_m_s5_d': 'new_m', 'new_m_s5_log_dt': 'new_m', 'new_m_s5_w_glu': 'new_m', 'new_m_ev_w_out': 'new_m', 'new_m_od_w_in': 'new_m', 'new_m_gdn_conv_w': 'new_m', 'new_m_gdn_a_log': 'new_m', 'new_m_gdn_dt_bias': 'new_m', 'new_m_gdn_norm_w': 'new_m', 'new_m_od_w_out': 'new_m', 'new_m_final_norm': 'new_m', 'new_v_ffn1_norm': 'new_v', 'new_v_ffn1_w_gate': 'new_v', 'new_v_ffn1_w_up': 'new_v', 'new_v_ffn1_w_down': 'new_v', 'new_v_mix_norm': 'new_v', 'new_v_ffn2_norm': 'new_v', 'new_v_ffn2_w_gate': 'new_v', 'new_v_ffn2_w_up': 'new_v', 'new_v_ffn2_w_down': 'new_v', 'new_v_ev_w_in': 'new_v', 'new_v_hg_lb_logits': 'new_v', 'new_v_hg_norm_w': 'new_v', 'new_v_s5_a_re': 'new_v', 'new_v_s5_a_im': 'new_v', 'new_v_s5_b_re': 'new_v', 'new_v_s5_b_im': 'new_v', 'new_v_s5_c_re': 'new_v', 'new_v_s5_c_im': 'new_v', 'new_v_s5_d': 'new_v', 'new_v_s5_log_dt': 'new_v', 'new_v_s5_w_glu': 'new_v', 'new_v_ev_w_out': 'new_v', 'new_v_od_w_in': 'new_v', 'new_v_gdn_conv_w': 'new_v', 'new_v_gdn_a_log': 'new_v', 'new_v_gdn_dt_bias': 'new_v', 'new_v_gdn_norm_w': 'new_v', 'new_v_od_w_out': 'new_v', 'new_v_final_norm': 'new_v'}


def _forward(args):
    return _fwd_reference(*[args[k] for k in FWD_PARAMS])


def _output_shape():
    out = _jax.eval_shape(lambda: _forward(_fwd_setup_inputs(0)))
    return out.shape, out.dtype

N_MICROBATCH = 1
ADAM_LR = 0.001
ADAM_B1 = 0.9
ADAM_B2 = 0.999
ADAM_EPS = 1e-08
ADAM_WD = 0.01
ADAM_STEP = 10
PER_EXAMPLE_BATCH_AXIS = {'x': 0, 'loss_target': 0}
SHARED_INPUTS = []
_WEIGHT_DTYPES = {'ffn1_norm': _jnp.float32, 'ffn1_w_gate': _jnp.float32, 'ffn1_w_up': _jnp.float32, 'ffn1_w_down': _jnp.float32, 'mix_norm': _jnp.float32, 'ffn2_norm': _jnp.float32, 'ffn2_w_gate': _jnp.float32, 'ffn2_w_up': _jnp.float32, 'ffn2_w_down': _jnp.float32, 'ev_w_in': _jnp.float32, 'hg_lb_logits': _jnp.float32, 'hg_norm_w': _jnp.float32, 's5_a_re': _jnp.float32, 's5_a_im': _jnp.float32, 's5_b_re': _jnp.float32, 's5_b_im': _jnp.float32, 's5_c_re': _jnp.float32, 's5_c_im': _jnp.float32, 's5_d': _jnp.float32, 's5_log_dt': _jnp.float32, 's5_w_glu': _jnp.float32, 'ev_w_out': _jnp.float32, 'od_w_in': _jnp.float32, 'gdn_conv_w': _jnp.float32, 'gdn_a_log': _jnp.float32, 'gdn_dt_bias': _jnp.float32, 'gdn_norm_w': _jnp.float32, 'od_w_out': _jnp.float32, 'final_norm': _jnp.float32}
MOMENT_SCALE = {'ffn1_norm': 8.760298e-02, 'ffn1_w_gate': 3.815931e-02, 'ffn1_w_up': 3.689837e-02, 'ffn1_w_down': 6.124465e-02, 'mix_norm': 1.353543e-01, 'ffn2_norm': 7.060866e-02, 'ffn2_w_gate': 3.021020e-02, 'ffn2_w_up': 2.927218e-02, 'ffn2_w_down': 4.850048e-02, 'ev_w_in': 8.375972e-02, 'hg_lb_logits': 7.410187e-03, 'hg_norm_w': 2.347997e-01, 's5_a_re': 4.203620e-03, 's5_a_im': 4.652079e-03, 's5_b_re': 2.631195e-03, 's5_b_im': 2.607352e-03, 's5_c_re': 5.272265e-03, 's5_c_im': 5.221609e-03, 's5_d': 7.969724e-02, 's5_log_dt': 2.270474e+00, 's5_w_glu': 2.064816e-02, 'ev_w_out': 9.742888e-02, 'od_w_in': 6.986497e-02, 'gdn_conv_w': 6.473997e-02, 'gdn_a_log': 3.972569e-01, 'gdn_dt_bias': 3.839914e-01, 'gdn_norm_w': 2.465035e-01, 'od_w_out': 8.193124e-02, 'final_norm': 3.202531e+01}


def _to_microbatches(a, axis):
    t = _jnp.moveaxis(a, axis, 0)
    t = t.reshape((N_MICROBATCH, t.shape[0] // N_MICROBATCH) + t.shape[1:])
    return _jnp.moveaxis(t, 1, axis + 1)


def setup_inputs(seed: int = 0) -> dict:
    inp = _fwd_setup_inputs(seed)
    key = _jax.random.fold_in(_jax.random.key(seed), 7919)
    shape, _ = _output_shape()
    out = dict(inp)
    out["loss_target"] = _jax.random.normal(_jax.random.fold_in(key, 0), shape, _jnp.float32)
    for i, name in enumerate(TWIN_WEIGHTS):
        w = inp[name].astype(_jnp.float32)
        if MOMENT_SCALE is None:
            s = _jnp.sqrt(_jnp.mean(_jnp.square(w)) + 1e-30)
        else:
            s = MOMENT_SCALE[name]
        km, kv = _jax.random.split(_jax.random.fold_in(key, i + 1))
        out[name] = w
        out["m_" + name] = s * _jax.random.normal(km, w.shape, _jnp.float32)
        out["v_" + name] = (s * s) * _jax.random.uniform(kv, w.shape, _jnp.float32, 0.5, 1.5)
    if N_MICROBATCH > 1:
        for name, axis in PER_EXAMPLE_BATCH_AXIS.items():
            out[name] = _to_microbatches(out[name], axis)
    return {'x': out['x'], 'ffn1_norm': out['ffn1_norm'], 'ffn1_w_gate': out['ffn1_w_gate'], 'ffn1_w_up': out['ffn1_w_up'], 'ffn1_w_down': out['ffn1_w_down'], 'mix_norm': out['mix_norm'], 'ffn2_norm': out['ffn2_norm'], 'ffn2_w_gate': out['ffn2_w_gate'], 'ffn2_w_up': out['ffn2_w_up'], 'ffn2_w_down': out['ffn2_w_down'], 'ev_w_in': out['ev_w_in'], 'hg_lb_logits': out['hg_lb_logits'], 'hg_norm_w': out['hg_norm_w'], 's5_a_re': out['s5_a_re'], 's5_a_im': out['s5_a_im'], 's5_b_re': out['s5_b_re'], 's5_b_im': out['s5_b_im'], 's5_c_re': out['s5_c_re'], 's5_c_im': out['s5_c_im'], 's5_d': out['s5_d'], 's5_log_dt': out['s5_log_dt'], 's5_w_glu': out['s5_w_glu'], 'ev_w_out': out['ev_w_out'], 'od_w_in': out['od_w_in'], 'gdn_conv_w': out['gdn_conv_w'], 'gdn_a_log': out['gdn_a_log'], 'gdn_dt_bias': out['gdn_dt_bias'], 'gdn_norm_w': out['gdn_norm_w'], 'od_w_out': out['od_w_out'], 'final_norm': out['final_norm'], 'loss_target': out['loss_target'], 'm_ffn1_norm': out['m_ffn1_norm'], 'm_ffn1_w_gate': out['m_ffn1_w_gate'], 'm_ffn1_w_up': out['m_ffn1_w_up'], 'm_ffn1_w_down': out['m_ffn1_w_down'], 'm_mix_norm': out['m_mix_norm'], 'm_ffn2_norm': out['m_ffn2_norm'], 'm_ffn2_w_gate': out['m_ffn2_w_gate'], 'm_ffn2_w_up': out['m_ffn2_w_up'], 'm_ffn2_w_down': out['m_ffn2_w_down'], 'm_ev_w_in': out['m_ev_w_in'], 'm_hg_lb_logits': out['m_hg_lb_logits'], 'm_hg_norm_w': out['m_hg_norm_w'], 'm_s5_a_re': out['m_s5_a_re'], 'm_s5_a_im': out['m_s5_a_im'], 'm_s5_b_re': out['m_s5_b_re'], 'm_s5_b_im': out['m_s5_b_im'], 'm_s5_c_re': out['m_s5_c_re'], 'm_s5_c_im': out['m_s5_c_im'], 'm_s5_d': out['m_s5_d'], 'm_s5_log_dt': out['m_s5_log_dt'], 'm_s5_w_glu': out['m_s5_w_glu'], 'm_ev_w_out': out['m_ev_w_out'], 'm_od_w_in': out['m_od_w_in'], 'm_gdn_conv_w': out['m_gdn_conv_w'], 'm_gdn_a_log': out['m_gdn_a_log'], 'm_gdn_dt_bias': out['m_gdn_dt_bias'], 'm_gdn_norm_w': out['m_gdn_norm_w'], 'm_od_w_out': out['m_od_w_out'], 'm_final_norm': out['m_final_norm'], 'v_ffn1_norm': out['v_ffn1_norm'], 'v_ffn1_w_gate': out['v_ffn1_w_gate'], 'v_ffn1_w_up': out['v_ffn1_w_up'], 'v_ffn1_w_down': out['v_ffn1_w_down'], 'v_mix_norm': out['v_mix_norm'], 'v_ffn2_norm': out['v_ffn2_norm'], 'v_ffn2_w_gate': out['v_ffn2_w_gate'], 'v_ffn2_w_up': out['v_ffn2_w_up'], 'v_ffn2_w_down': out['v_ffn2_w_down'], 'v_ev_w_in': out['v_ev_w_in'], 'v_hg_lb_logits': out['v_hg_lb_logits'], 'v_hg_norm_w': out['v_hg_norm_w'], 'v_s5_a_re': out['v_s5_a_re'], 'v_s5_a_im': out['v_s5_a_im'], 'v_s5_b_re': out['v_s5_b_re'], 'v_s5_b_im': out['v_s5_b_im'], 'v_s5_c_re': out['v_s5_c_re'], 'v_s5_c_im': out['v_s5_c_im'], 'v_s5_d': out['v_s5_d'], 'v_s5_log_dt': out['v_s5_log_dt'], 'v_s5_w_glu': out['v_s5_w_glu'], 'v_ev_w_out': out['v_ev_w_out'], 'v_od_w_in': out['v_od_w_in'], 'v_gdn_conv_w': out['v_gdn_conv_w'], 'v_gdn_a_log': out['v_gdn_a_log'], 'v_gdn_dt_bias': out['v_gdn_dt_bias'], 'v_gdn_norm_w': out['v_gdn_norm_w'], 'v_od_w_out': out['v_od_w_out'], 'v_final_norm': out['v_final_norm']}


def _loss(weights, diff, rest, loss_target):
    with _jax.named_scope("forward"):
        args = {**rest, TWIN_DIFF_INPUT: diff, **{k: w.astype(_WEIGHT_DTYPES[k]) for k, w in weights.items()}}
        y = _forward(args)
    with _jax.named_scope("loss_head"):
        err = _jnp.square(y.astype(_jnp.float32) - loss_target)
        return 0.5 * _jnp.sum(_jnp.mean(err, axis=-1)) if err.ndim else 0.5 * err


def _adamw(w, g, m, v):
    m = ADAM_B1 * m + (1.0 - ADAM_B1) * g
    v = ADAM_B2 * v + (1.0 - ADAM_B2) * _jnp.square(g)
    m_hat = m / (1.0 - ADAM_B1 ** ADAM_STEP)
    v_hat = v / (1.0 - ADAM_B2 ** ADAM_STEP)
    delta = -ADAM_LR * (m_hat / (_jnp.sqrt(v_hat) + ADAM_EPS) + ADAM_WD * w)
    return delta, m, v


def reference(x, ffn1_norm, ffn1_w_gate, ffn1_w_up, ffn1_w_down, mix_norm, ffn2_norm, ffn2_w_gate, ffn2_w_up, ffn2_w_down, ev_w_in, hg_lb_logits, hg_norm_w, s5_a_re, s5_a_im, s5_b_re, s5_b_im, s5_c_re, s5_c_im, s5_d, s5_log_dt, s5_w_glu, ev_w_out, od_w_in, gdn_conv_w, gdn_a_log, gdn_dt_bias, gdn_norm_w, od_w_out, final_norm, loss_target, m_ffn1_norm, m_ffn1_w_gate, m_ffn1_w_up, m_ffn1_w_down, m_mix_norm, m_ffn2_norm, m_ffn2_w_gate, m_ffn2_w_up, m_ffn2_w_down, m_ev_w_in, m_hg_lb_logits, m_hg_norm_w, m_s5_a_re, m_s5_a_im, m_s5_b_re, m_s5_b_im, m_s5_c_re, m_s5_c_im, m_s5_d, m_s5_log_dt, m_s5_w_glu, m_ev_w_out, m_od_w_in, m_gdn_conv_w, m_gdn_a_log, m_gdn_dt_bias, m_gdn_norm_w, m_od_w_out, m_final_norm, v_ffn1_norm, v_ffn1_w_gate, v_ffn1_w_up, v_ffn1_w_down, v_mix_norm, v_ffn2_norm, v_ffn2_w_gate, v_ffn2_w_up, v_ffn2_w_down, v_ev_w_in, v_hg_lb_logits, v_hg_norm_w, v_s5_a_re, v_s5_a_im, v_s5_b_re, v_s5_b_im, v_s5_c_re, v_s5_c_im, v_s5_d, v_s5_log_dt, v_s5_w_glu, v_ev_w_out, v_od_w_in, v_gdn_conv_w, v_gdn_a_log, v_gdn_dt_bias, v_gdn_norm_w, v_od_w_out, v_final_norm):
    given = dict(x=x, ffn1_norm=ffn1_norm, ffn1_w_gate=ffn1_w_gate, ffn1_w_up=ffn1_w_up, ffn1_w_down=ffn1_w_down, mix_norm=mix_norm, ffn2_norm=ffn2_norm, ffn2_w_gate=ffn2_w_gate, ffn2_w_up=ffn2_w_up, ffn2_w_down=ffn2_w_down, ev_w_in=ev_w_in, hg_lb_logits=hg_lb_logits, hg_norm_w=hg_norm_w, s5_a_re=s5_a_re, s5_a_im=s5_a_im, s5_b_re=s5_b_re, s5_b_im=s5_b_im, s5_c_re=s5_c_re, s5_c_im=s5_c_im, s5_d=s5_d, s5_log_dt=s5_log_dt, s5_w_glu=s5_w_glu, ev_w_out=ev_w_out, od_w_in=od_w_in, gdn_conv_w=gdn_conv_w, gdn_a_log=gdn_a_log, gdn_dt_bias=gdn_dt_bias, gdn_norm_w=gdn_norm_w, od_w_out=od_w_out, final_norm=final_norm, loss_target=loss_target, m_ffn1_norm=m_ffn1_norm, m_ffn1_w_gate=m_ffn1_w_gate, m_ffn1_w_up=m_ffn1_w_up, m_ffn1_w_down=m_ffn1_w_down, m_mix_norm=m_mix_norm, m_ffn2_norm=m_ffn2_norm, m_ffn2_w_gate=m_ffn2_w_gate, m_ffn2_w_up=m_ffn2_w_up, m_ffn2_w_down=m_ffn2_w_down, m_ev_w_in=m_ev_w_in, m_hg_lb_logits=m_hg_lb_logits, m_hg_norm_w=m_hg_norm_w, m_s5_a_re=m_s5_a_re, m_s5_a_im=m_s5_a_im, m_s5_b_re=m_s5_b_re, m_s5_b_im=m_s5_b_im, m_s5_c_re=m_s5_c_re, m_s5_c_im=m_s5_c_im, m_s5_d=m_s5_d, m_s5_log_dt=m_s5_log_dt, m_s5_w_glu=m_s5_w_glu, m_ev_w_out=m_ev_w_out, m_od_w_in=m_od_w_in, m_gdn_conv_w=m_gdn_conv_w, m_gdn_a_log=m_gdn_a_log, m_gdn_dt_bias=m_gdn_dt_bias, m_gdn_norm_w=m_gdn_norm_w, m_od_w_out=m_od_w_out, m_final_norm=m_final_norm, v_ffn1_norm=v_ffn1_norm, v_ffn1_w_gate=v_ffn1_w_gate, v_ffn1_w_up=v_ffn1_w_up, v_ffn1_w_down=v_ffn1_w_down, v_mix_norm=v_mix_norm, v_ffn2_norm=v_ffn2_norm, v_ffn2_w_gate=v_ffn2_w_gate, v_ffn2_w_up=v_ffn2_w_up, v_ffn2_w_down=v_ffn2_w_down, v_ev_w_in=v_ev_w_in, v_hg_lb_logits=v_hg_lb_logits, v_hg_norm_w=v_hg_norm_w, v_s5_a_re=v_s5_a_re, v_s5_a_im=v_s5_a_im, v_s5_b_re=v_s5_b_re, v_s5_b_im=v_s5_b_im, v_s5_c_re=v_s5_c_re, v_s5_c_im=v_s5_c_im, v_s5_d=v_s5_d, v_s5_log_dt=v_s5_log_dt, v_s5_w_glu=v_s5_w_glu, v_ev_w_out=v_ev_w_out, v_od_w_in=v_od_w_in, v_gdn_conv_w=v_gdn_conv_w, v_gdn_a_log=v_gdn_a_log, v_gdn_dt_bias=v_gdn_dt_bias, v_gdn_norm_w=v_gdn_norm_w, v_od_w_out=v_od_w_out, v_final_norm=v_final_norm)
    weights = {n: given[n] for n in TWIN_WEIGHTS}
    shared = {n: given[n] for n in SHARED_INPUTS}
    per_example = {n: given[n] for n in ['x']}
    grad_fn = _jax.value_and_grad(_loss, argnums=(0, 1))

    def one_microbatch(ex, loss_target):
        ex = dict(ex)
        diff = ex.pop(TWIN_DIFF_INPUT)
        return grad_fn(weights, diff, {**shared, **ex}, loss_target)

    if N_MICROBATCH == 1:
        loss, (grad_w, grad_x) = one_microbatch(per_example, given["loss_target"])
    else:
        def body(carry, xs):
            loss_sum, grad_sum = carry
            l_k, (gw_k, gx_k) = one_microbatch(xs[0], xs[1])
            with _jax.named_scope("update"):
                return (loss_sum + l_k, _jax.tree.map(_jnp.add, grad_sum, gw_k)), gx_k

        init = (_jnp.zeros((), _jnp.float32), _jax.tree.map(_jnp.zeros_like, weights))
        (loss, grad_w), grad_x = _jax.lax.scan(body, init, (per_example, given["loss_target"]))
    with _jax.named_scope("update"):
        delta_w, new_m, new_v = {}, {}, {}
        for n in TWIN_WEIGHTS:
            delta_w[n], new_m[n], new_v[n] = _adamw(weights[n], grad_w[n], given["m_" + n], given["v_" + n])
    return (loss, grad_x, *[grad_w[n] for n in TWIN_WEIGHTS], *[delta_w[n] for n in TWIN_WEIGHTS],
            *[new_m[n] for n in TWIN_WEIGHTS], *[new_v[n] for n in TWIN_WEIGHTS])
```

```python
import functools
import math

import jax
import jax.numpy as jnp
from jax import lax
from jax.experimental import pallas as pl
from jax.experimental.pallas import tpu as pltpu

F32 = jnp.float32
BF16 = jnp.bfloat16
HI = lax.Precision.HIGHEST

D_MODEL = 1024
DEPTH = 4
D_FF = 2816
NORM_EPS = 1e-6
F_MIN = 1e-6
CHUNK = 64
HG_HEADS = 4
HEAD_DIM = 128
HG_QK = HG_HEADS * HEAD_DIM
S5_WIDTH = 512
S5_GROUP = 16
S5_GROUPS = 32
S5_STATE = 64
S5_NSTATE = S5_GROUPS * S5_STATE
EV_IN = 2560
GDN_HEADS = 8
GDN_QKV = 3 * GDN_HEADS * HEAD_DIM
CONV_W = 4
OD_IN = 4112
OD_IN_PAD = 4224
N_DEV = 8
LANES = 128
ADAM_LR, ADAM_B1, ADAM_B2, ADAM_EPS, ADAM_WD, ADAM_STEP = 0.001, 0.9, 0.999, 1e-08, 0.01, 10
VMEM_LIMIT = 56 * 1024 * 1024

MESH = pl.DeviceIdType.MESH


def _params(sem=None, **kw):
    return pltpu.CompilerParams(dimension_semantics=sem, vmem_limit_bytes=VMEM_LIMIT, **kw)


def _tile(n, cands):
    for c in cands:
        if n % c == 0:
            return c
    return n


def _dot(a, b):
    return jnp.dot(a.astype(BF16), b.astype(BF16), preferred_element_type=F32)


def _dot_nt(a, b):
    return lax.dot_general(a.astype(BF16), b.astype(BF16), (((1,), (1,)), ((), ())), preferred_element_type=F32)


def _dot_tn(a, b):
    return lax.dot_general(a.astype(BF16), b.astype(BF16), (((0,), (0,)), ((), ())), preferred_element_type=F32)


def _dot_hi(a, b):
    return jnp.dot(a, b, precision=HI, preferred_element_type=F32)


_TN_CANDS = (1408, 1280, 1024, 512, 384, 256, 128)
_B_TILE_BYTES = 6 * 1024 * 1024


def _mm(a, b, *, name, nt=False, out_dtype=F32, res=None, scale=1.0):
    m, k = a.shape
    n = b.shape[0] if nt else b.shape[1]
    tm = _tile(m, (512, 256, 128))
    tn = _tile(n, [c for c in _TN_CANDS if c * k * b.dtype.itemsize <= _B_TILE_BYTES])

    def body(*refs):
        if res is None:
            a_ref, b_ref, o_ref = refs
        else:
            a_ref, b_ref, r_ref, o_ref = refs
        acc = (_dot_nt if nt else _dot)(a_ref[...], b_ref[...])
        if scale != 1.0:
            acc = scale * acc
        if res is not None:
            acc = r_ref[...] + acc
        o_ref[...] = acc.astype(out_dtype)

    b_spec = pl.BlockSpec((tn, k), lambda j, i: (j, 0)) if nt else pl.BlockSpec((k, tn), lambda j, i: (0, j))
    in_specs = [pl.BlockSpec((tm, k), lambda j, i: (i, 0)), b_spec]
    args = [a, b]
    if res is not None:
        in_specs.append(pl.BlockSpec((tm, tn), lambda j, i: (i, j)))
        args.append(res)
    return pl.pallas_call(
        body, name=name, grid=(n // tn, m // tm), in_specs=in_specs,
        out_specs=pl.BlockSpec((tm, tn), lambda j, i: (i, j)),
        out_shape=jax.ShapeDtypeStruct((m, n), out_dtype),
        compiler_params=_params(("parallel", "parallel")),
    )(*args)


def _mm_tn(a, b, *, name, scale=1.0):
    t, m = a.shape
    n = b.shape[1]
    tm = _tile(m, (1024, 512, 256, 128))
    tn = _tile(n, _TN_CANDS)
    tk = _tile(t, (512, 256, 128))
    nk = t // tk

    def body(a_ref, b_ref, o_ref, acc_ref):
        kk = pl.program_id(2)

        @pl.when(kk == 0)
        def _():
            acc_ref[...] = jnp.zeros_like(acc_ref)

        acc_ref[...] += _dot_tn(a_ref[...], b_ref[...])

        @pl.when(kk == nk - 1)
        def _():
            o_ref[...] = acc_ref[...] * scale if scale != 1.0 else acc_ref[...]

    return pl.pallas_call(
        body, name=name, grid=(m // tm, n // tn, nk),
        in_specs=[pl.BlockSpec((tk, tm), lambda i, j, kk: (kk, i)), pl.BlockSpec((tk, tn), lambda i, j, kk: (kk, j))],
        out_specs=pl.BlockSpec((tm, tn), lambda i, j, kk: (i, j)),
        out_shape=jax.ShapeDtypeStruct((m, n), F32),
        scratch_shapes=[pltpu.VMEM((tm, tn), F32)],
        compiler_params=_params(("parallel", "parallel", "arbitrary")),
    )(a, b)


def _sigmoid(x):
    return jax.nn.sigmoid(x)


def _head_norm_gate(o, gate, nw):
    r = lax.rsqrt(jnp.mean(o * o, axis=-1, keepdims=True) + NORM_EPS)
    return o * r * nw * (gate * _sigmoid(gate))


def _hg_chunk(st, ql, fl, iv, gl, lb, nw):
    c = ql.shape[0]
    q = ql * _sigmoid(ql)
    f = lb + (1.0 - lb) * _sigmoid(fl)
    lf = jnp.log(jnp.maximum(f, F_MIN))
    k = 1.0 - f
    ri = lax.broadcasted_iota(jnp.int32, (c, c), 0)
    ci = lax.broadcasted_iota(jnp.int32, (c, c), 1)
    rl = lax.broadcasted_iota(jnp.int32, (c, LANES), 0)
    b = _dot_hi(jnp.where(ci <= ri, 1.0, 0.0), lf)
    attn = jnp.where(ri == ci, _dot_nt(q, k), 0.0)
    sh = 0
    while (1 << sh) < c:
        m = 1 << sh
        ref = ((ri >> (sh + 1)) << (sh + 1)) + (m - 1)
        low_r = ((ri >> sh) & 1) == 1
        low_c = ((ci >> sh) & 1) == 1
        w_low = low_r & (ci > ref) & (ci <= ri)
        w_up = jnp.logical_not(low_r) & (ci > ri) & (ci <= ref)
        w = jnp.where(w_low | w_up, 1.0, 0.0)
        e = jnp.exp(_dot_hi(w, lf))
        low_l = ((rl >> sh) & 1) == 1
        qs = jnp.where(low_l, q * e, 0.0)
        ks = jnp.where(low_l, 0.0, k * e)
        pair = ((ri >> (sh + 1)) == (ci >> (sh + 1))) & low_r & jnp.logical_not(low_c)
        attn = attn + jnp.where(pair, _dot_nt(qs, ks), 0.0)
        sh += 1
    bl = jnp.sum(lf, axis=0, keepdims=True)
    o = _dot(attn, iv) + _dot_nt(q * jnp.exp(b), st)
    st_new = st * jnp.exp(bl) + _dot_tn(iv, k * jnp.exp(bl - b))
    return _head_norm_gate(o, gl, nw), st_new


def _hg_specs(nb, l, n_c, rev):
    def cidx(cc):
        return (n_c - 1 - cc) if rev else cc

    def col(off):
        return pl.BlockSpec((None, CHUNK, LANES), lambda bb, hh, cc: (bb, cidx(cc), hh + off))

    vec = pl.BlockSpec((None, 1, LANES), lambda bb, hh, cc: (hh, 0, 0))
    nw = pl.BlockSpec((1, LANES), lambda bb, hh, cc: (0, 0))
    st = pl.BlockSpec((None, None, None, LANES, LANES), lambda bb, hh, cc: (bb, hh, cidx(cc), 0, 0))
    acc = pl.BlockSpec((None, None, 1, LANES), lambda bb, hh, cc: (bb, hh, 0, 0))
    return col, vec, nw, st, acc


def _hgrn2_fwd(proj, lb, nw):
    nb, l, _ = proj.shape
    n_c = l // CHUNK
    col, vec, nws, st, _ = _hg_specs(nb, l, n_c, False)

    def body(ql_ref, fl_ref, iv_ref, gl_ref, lb_ref, nw_ref, y_ref, st_ref, s_scr):
        @pl.when(pl.program_id(2) == 0)
        def _():
            s_scr[...] = jnp.zeros_like(s_scr)

        s_in = s_scr[...]
        st_ref[...] = s_in
        y, s_new = _hg_chunk(s_in, ql_ref[...], fl_ref[...], iv_ref[...], gl_ref[...], lb_ref[...], nw_ref[...])
        y_ref[...] = y.astype(y_ref.dtype)
        s_scr[...] = s_new

    return pl.pallas_call(
        body, name="hgrn2_fwd", grid=(nb, HG_HEADS, n_c),
        in_specs=[col(0), col(4), col(8), col(12), vec, nws],
        out_specs=[col(0), st],
        out_shape=[jax.ShapeDtypeStruct((nb, l, HG_QK), BF16),
                   jax.ShapeDtypeStruct((nb, HG_HEADS, n_c, LANES, LANES), F32)],
        scratch_shapes=[pltpu.VMEM((LANES, LANES), F32)],
        compiler_params=_params(("parallel", "parallel", "arbitrary")),
    )(proj, proj, proj, proj, lb, nw)


def _hgrn2_bwd(proj, lb, nw, states, dy):
    nb, l, _ = proj.shape
    n_c = l // CHUNK
    col, vec, nws, st, acc = _hg_specs(nb, l, n_c, True)

    def body(ql_ref, fl_ref, iv_ref, gl_ref, lb_ref, nw_ref, st_ref, dy_ref,
             dql_ref, dfl_ref, div_ref, dgl_ref, dlb_ref, dnw_ref, ds_scr):
        @pl.when(pl.program_id(2) == 0)
        def _():
            ds_scr[...] = jnp.zeros_like(ds_scr)
            dlb_ref[...] = jnp.zeros_like(dlb_ref)
            dnw_ref[...] = jnp.zeros_like(dnw_ref)

        _, vjp = jax.vjp(_hg_chunk, st_ref[...], ql_ref[...], fl_ref[...], iv_ref[...], gl_ref[...],
                         lb_ref[...], nw_ref[...])
        ds, dql, dfl, div, dgl, dlb, dnw = vjp((dy_ref[...].astype(F32), ds_scr[...]))
        ds_scr[...] = ds
        dql_ref[...] = dql
        dfl_ref[...] = dfl
        div_ref[...] = div
        dgl_ref[...] = dgl
        dlb_ref[...] += dlb
        dnw_ref[...] += dnw

    outs = pl.pallas_call(
        body, name="hgrn2_bwd", grid=(nb, HG_HEADS, n_c),
        in_specs=[col(0), col(4), col(8), col(12), vec, nws, st, col(0)],
        out_specs=[col(0), col(0), col(0), col(0), acc, acc],
        out_shape=[jax.ShapeDtypeStruct((nb, l, HG_QK), F32)] * 4
        + [jax.ShapeDtypeStruct((nb, HG_HEADS, 1, LANES), F32)] * 2,
        scratch_shapes=[pltpu.VMEM((LANES, LANES), F32)],
        compiler_params=_params(("parallel", "parallel", "arbitrary")),
    )(proj, proj, proj, proj, lb, nw, states, dy)
    return outs


def _shift_down(x, s):
    if s == 0:
        return x
    rows = lax.broadcasted_iota(jnp.int32, x.shape, 0)
    return jnp.where(rows >= s, pltpu.roll(x, s, 0), 0.0)


def _shift_up(x, s):
    if s == 0:
        return x
    n = x.shape[0]
    rows = lax.broadcasted_iota(jnp.int32, x.shape, 0)
    return jnp.where(rows < n - s, pltpu.roll(x, n - s, 0), 0.0)


def _gdn_pre_fwd(proj, conv_w):
    nb, l, _ = proj.shape
    n_blk = GDN_QKV // LANES

    def body(x_ref, w_ref, o_ref):
        j = pl.program_id(1)
        x = x_ref[...]
        w = w_ref[...]
        c = w[3:4] * x
        for t in range(CONV_W - 1):
            c = c + w[t:t + 1] * _shift_down(x, CONV_W - 1 - t)
        s = c * _sigmoid(c)
        r = lax.rsqrt(jnp.sum(s * s, axis=-1, keepdims=True) + NORM_EPS)
        scale = jnp.where(j < GDN_HEADS, HEAD_DIM ** -0.5, 1.0)
        o_ref[...] = jnp.where(j < 2 * GDN_HEADS, s * r * scale, s)

    return pl.pallas_call(
        body, name="gdn_pre_fwd", grid=(nb, n_blk),
        in_specs=[pl.BlockSpec((None, l, LANES), lambda b, j: (b, 0, j)), pl.BlockSpec((CONV_W, LANES), lambda b, j: (0, j))],
        out_specs=pl.BlockSpec((None, l, LANES), lambda b, j: (b, 0, j)),
        out_shape=jax.ShapeDtypeStruct((nb, l, GDN_QKV), F32),
        compiler_params=_params(("parallel", "parallel")),
    )(proj, conv_w)


def _gdn_pre_bwd(proj, conv_w, dout):
    nb, l, _ = proj.shape
    n_blk = GDN_QKV // LANES

    def body(x_ref, w_ref, d_ref, dx_ref, dw_ref):
        j = pl.program_id(1)
        x = x_ref[...]
        w = w_ref[...]
        xs = [_shift_down(x, CONV_W - 1 - t) for t in range(CONV_W)]
        c = w[0:1] * xs[0]
        for t in range(1, CONV_W):
            c = c + w[t:t + 1] * xs[t]
        sg = _sigmoid(c)
        s = c * sg
        d = d_ref[...]
        r = lax.rsqrt(jnp.sum(s * s, axis=-1, keepdims=True) + NORM_EPS)
        scale = jnp.where(j < GDN_HEADS, HEAD_DIM ** -0.5, 1.0)
        ds_norm = scale * r * (d - s * (r * r) * jnp.sum(d * s, axis=-1, keepdims=True))
        ds = jnp.where(j < 2 * GDN_HEADS, ds_norm, d)
        dc = ds * (sg * (1.0 + c * (1.0 - sg)))
        dx = w[3:4] * dc
        for t in range(CONV_W - 1):
            dx = dx + w[t:t + 1] * _shift_up(dc, CONV_W - 1 - t)
        dx_ref[...] = dx
        for t in range(CONV_W):
            dw_ref[t:t + 1, :] = jnp.sum(dc * xs[t], axis=0, keepdims=True)

    return pl.pallas_call(
        body, name="gdn_pre_bwd", grid=(nb, n_blk),
        in_specs=[pl.BlockSpec((None, l, LANES), lambda b, j: (b, 0, j)), pl.BlockSpec((CONV_W, LANES), lambda b, j: (0, j)),
                  pl.BlockSpec((None, l, LANES), lambda b, j: (b, 0, j))],
        out_specs=[pl.BlockSpec((None, l, LANES), lambda b, j: (b, 0, j)), pl.BlockSpec((None, CONV_W, LANES), lambda b, j: (b, 0, j))],
        out_shape=[jax.ShapeDtypeStruct((nb, l, GDN_QKV), F32), jax.ShapeDtypeStruct((nb, CONV_W, GDN_QKV), F32)],
        compiler_params=_params(("parallel", "parallel")),
    )(proj, conv_w, dout)


def _gdn_chunk(st, q, k, v, gate, bl, al, alog, dtb, nw):
    c = q.shape[0]
    beta = _sigmoid(bl)
    x = al + dtb
    softplus = jnp.maximum(x, 0.0) + jnp.log(1.0 + jnp.exp(-jnp.abs(x)))
    la = -jnp.exp(alog) * softplus
    ri = lax.broadcasted_iota(jnp.int32, (c, c), 0)
    ci = lax.broadcasted_iota(jnp.int32, (c, c), 1)
    lower = ci <= ri
    strict = ci < ri
    ltri = jnp.where(lower, 1.0, 0.0)
    la_l = la + jnp.zeros((1, LANES), F32)
    g = _dot_hi(ltri, la_l)
    delta = _dot_hi(ltri, jnp.where(strict, la + jnp.zeros((1, c), F32), 0.0))
    gam = jnp.where(lower, jnp.exp(jnp.where(lower, delta, 0.0)), 0.0)
    kb = k * beta
    vb = v * beta
    m = jnp.where(strict, _dot_nt(kb, k) * gam, 0.0)
    pw = -m
    t_inv = jnp.where(ri == ci, 1.0, 0.0) + pw
    steps = int(math.log2(c)) - 1
    for _ in range(steps):
        pw = _dot_hi(pw, pw)
        t_inv = t_inv + _dot_hi(t_inv, pw)
    eg = jnp.exp(g)
    u = _dot(t_inv, vb)
    w = _dot(t_inv, kb * eg)
    attn = jnp.where(lower, _dot_nt(q, k) * gam, 0.0)
    v_new = u - _dot_nt(w, st)
    o = _dot_nt(q * eg, st) + _dot(attn, v_new)
    g_last = jnp.sum(la_l, axis=0, keepdims=True)
    st_new = st * jnp.exp(g_last) + _dot_tn(v_new, k * jnp.exp(g_last - g))
    return _head_norm_gate(o, gate, nw), st_new


def _gdn_specs(n_c, rev):
    def cidx(cc):
        return (n_c - 1 - cc) if rev else cc

    def col(off):
        return pl.BlockSpec((None, CHUNK, LANES), lambda bb, hh, cc: (bb, cidx(cc), hh + off))

    tok = pl.BlockSpec((None, None, CHUNK, 1), lambda bb, hh, cc: (bb, hh, cidx(cc), 0))
    scal = pl.BlockSpec((None, 1, 1), lambda bb, hh, cc: (hh, 0, 0))
    nw = pl.BlockSpec((1, LANES), lambda bb, hh, cc: (0, 0))
    st = pl.BlockSpec((None, None, None, LANES, LANES), lambda bb, hh, cc: (bb, hh, cidx(cc), 0, 0))
    acc_v = pl.BlockSpec((None, None, 1, LANES), lambda bb, hh, cc: (bb, hh, 0, 0))
    acc_s = pl.BlockSpec((None, None, 1, 1), lambda bb, hh, cc: (bb, hh, 0, 0))
    return col, tok, scal, nw, st, acc_v, acc_s


def _gdn_fwd(qkv, proj, bl, al, alog, dtb, nw):
    nb, l, _ = qkv.shape
    n_c = l // CHUNK
    col, tok, scal, nws, st, _, _ = _gdn_specs(n_c, False)

    def body(q_ref, k_ref, v_ref, g_ref, bl_ref, al_ref, alog_ref, dtb_ref, nw_ref, y_ref, st_ref, s_scr):
        @pl.when(pl.program_id(2) == 0)
        def _():
            s_scr[...] = jnp.zeros_like(s_scr)

        s_in = s_scr[...]
        st_ref[...] = s_in
        y, s_new = _gdn_chunk(s_in, q_ref[...], k_ref[...], v_ref[...], g_ref[...], bl_ref[...], al_ref[...],
                              alog_ref[...], dtb_ref[...], nw_ref[...])
        y_ref[...] = y.astype(y_ref.dtype)
        s_scr[...] = s_new

    h = GDN_HEADS
    return pl.pallas_call(
        body, name="gdn_fwd", grid=(nb, h, n_c),
        in_specs=[col(0), col(h), col(2 * h), col(3 * h), tok, tok, scal, scal, nws],
        out_specs=[col(0), st],
        out_shape=[jax.ShapeDtypeStruct((nb, l, h * LANES), BF16),
                   jax.ShapeDtypeStruct((nb, h, n_c, LANES, LANES), F32)],
        scratch_shapes=[pltpu.VMEM((LANES, LANES), F32)],
        compiler_params=_params(("parallel", "parallel", "arbitrary")),
    )(qkv, qkv, qkv, proj, bl, al, alog, dtb, nw)


def _gdn_bwd(qkv, proj, bl, al, alog, dtb, nw, states, dy):
    nb, l, _ = qkv.shape
    n_c = l // CHUNK
    col, tok, scal, nws, st, acc_v, acc_s = _gdn_specs(n_c, True)

    def body(q_ref, k_ref, v_ref, g_ref, bl_ref, al_ref, alog_ref, dtb_ref, nw_ref, st_ref, dy_ref,
             dq_ref, dk_ref, dv_ref, dg_ref, dbl_ref, dal_ref, dalog_ref, ddtb_ref, dnw_ref, ds_scr):
        @pl.when(pl.program_id(2) == 0)
        def _():
            ds_scr[...] = jnp.zeros_like(ds_scr)
            dalog_ref[...] = jnp.zeros_like(dalog_ref)
            ddtb_ref[...] = jnp.zeros_like(ddtb_ref)
            dnw_ref[...] = jnp.zeros_like(dnw_ref)

        _, vjp = jax.vjp(_gdn_chunk, st_ref[...], q_ref[...], k_ref[...], v_ref[...], g_ref[...], bl_ref[...],
                         al_ref[...], alog_ref[...], dtb_ref[...], nw_ref[...])
        ds, dq, dk, dv, dg, dbl, dal, dalog, ddtb, dnw = vjp((dy_ref[...].astype(F32), ds_scr[...]))
        ds_scr[...] = ds
        dq_ref[...] = dq
        dk_ref[...] = dk
        dv_ref[...] = dv
        dg_ref[...] = dg
        dbl_ref[...] = dbl
        dal_ref[...] = dal
        dalog_ref[...] += dalog
        ddtb_ref[...] += ddtb
        dnw_ref[...] += dnw

    h = GDN_HEADS
    wide = jax.ShapeDtypeStruct((nb, l, h * LANES), F32)
    return pl.pallas_call(
        body, name="gdn_bwd", grid=(nb, h, n_c),
        in_specs=[col(0), col(h), col(2 * h), col(3 * h), tok, tok, scal, scal, nws, st, col(0)],
        out_specs=[col(0), col(0), col(0), col(0), tok, tok, acc_s, acc_s, acc_v],
        out_shape=[wide, wide, wide, wide,
                   jax.ShapeDtypeStruct((nb, h, l, 1), F32), jax.ShapeDtypeStruct((nb, h, l, 1), F32),
                   jax.ShapeDtypeStruct((nb, h, 1, 1), F32), jax.ShapeDtypeStruct((nb, h, 1, 1), F32),
                   jax.ShapeDtypeStruct((nb, h, 1, LANES), F32)],
        scratch_shapes=[pltpu.VMEM((LANES, LANES), F32)],
        compiler_params=_params(("parallel", "parallel", "arbitrary")),
    )(qkv, qkv, qkv, proj, bl, al, alog, dtb, nw, states, dy)


S5_ROWS = S5_NSTATE // LANES
S5_TB = 256


def _s5_scan_fwd(bu, abar):
    nb, l = bu.shape[:2]
    tb = min(S5_TB, l)

    def body(bu_ref, a_ref, h_ref, c_scr):
        @pl.when(pl.program_id(1) == 0)
        def _():
            c_scr[...] = jnp.zeros_like(c_scr)

        ar = a_ref[0]
        ai = a_ref[1]

        def step(t, carry):
            hr, hi = carry
            nr = ar * hr - ai * hi + bu_ref[t, 0]
            ni = ar * hi + ai * hr + bu_ref[t, 1]
            h_ref[t, 0] = nr
            h_ref[t, 1] = ni
            return nr, ni

        hr, hi = lax.fori_loop(0, tb, step, (c_scr[0], c_scr[1]), unroll=8)
        c_scr[0] = hr
        c_scr[1] = hi

    blk = pl.BlockSpec((None, tb, 2, S5_ROWS, LANES), lambda b, i: (b, i, 0, 0, 0))
    return pl.pallas_call(
        body, name="s5_scan_fwd", grid=(nb, l // tb),
        in_specs=[blk, pl.BlockSpec((2, S5_ROWS, LANES), lambda b, i: (0, 0, 0))],
        out_specs=blk, out_shape=jax.ShapeDtypeStruct(bu.shape, F32),
        scratch_shapes=[pltpu.VMEM((2, S5_ROWS, LANES), F32)],
        compiler_params=_params(("parallel", "arbitrary")),
    )(bu, abar)


def _s5_scan_bwd(dh, h, abar):
    nb, l = dh.shape[:2]
    tb = min(S5_TB, l)
    n_blk = l // tb

    def body(dh_ref, h_ref, a_ref, g_ref, da_ref, c_scr):
        @pl.when(pl.program_id(1) == 0)
        def _():
            c_scr[...] = jnp.zeros_like(c_scr)
            da_ref[...] = jnp.zeros_like(da_ref)

        ar = a_ref[0]
        ai = a_ref[1]

        def step(s, carry):
            gr, gi, dar, dai = carry
            t = tb - 1 - s
            hr = h_ref[t, 0]
            hi = h_ref[t, 1]
            dar = dar + gr * hr + gi * hi
            dai = dai + gi * hr - gr * hi
            nr = ar * gr + ai * gi + dh_ref[t, 0]
            ni = ar * gi - ai * gr + dh_ref[t, 1]
            g_ref[t, 0] = nr
            g_ref[t, 1] = ni
            return nr, ni, dar, dai

        z = jnp.zeros((S5_ROWS, LANES), F32)
        gr, gi, dar, dai = lax.fori_loop(0, tb, step, (c_scr[0], c_scr[1], z, z), unroll=8)
        c_scr[0] = gr
        c_scr[1] = gi
        da_ref[0] += dar
        da_ref[1] += dai

    blk = pl.BlockSpec((None, tb, 2, S5_ROWS, LANES), lambda b, i: (b, n_blk - 1 - i, 0, 0, 0))
    return pl.pallas_call(
        body, name="s5_scan_bwd", grid=(nb, n_blk),
        in_specs=[blk, blk, pl.BlockSpec((2, S5_ROWS, LANES), lambda b, i: (0, 0, 0))],
        out_specs=[blk, pl.BlockSpec((None, 2, S5_ROWS, LANES), lambda b, i: (b, 0, 0, 0))],
        out_shape=[jax.ShapeDtypeStruct(dh.shape, F32), jax.ShapeDtypeStruct((nb, 2, S5_ROWS, LANES), F32)],
        scratch_shapes=[pltpu.VMEM((2, S5_ROWS, LANES), F32)],
        compiler_params=_params(("parallel", "arbitrary")),
    )(dh, h, abar)


_GELU_C = math.sqrt(2.0 / math.pi)


def _gelu(x):
    return 0.5 * x * (1.0 + jnp.tanh(_GELU_C * (x + 0.044715 * x * x * x)))


def _gelu_grad(x):
    t = jnp.tanh(_GELU_C * (x + 0.044715 * x * x * x))
    return 0.5 * (1.0 + t) + 0.5 * x * (1.0 - t * t) * _GELU_C * (1.0 + 3.0 * 0.044715 * x * x)


def _s5_post_fwd(yc, u, d, w_glu):
    t, w = yc.shape
    tt = _tile(t, (512, 256, 128))

    def body(yc_ref, u_ref, d_ref, w_ref, o_ref, y0_ref):
        y0 = yc_ref[...] + d_ref[...] * u_ref[...]
        y = _gelu(y0)
        z = _dot(y, w_ref[...])
        o_ref[...] = (y * _sigmoid(z)).astype(o_ref.dtype)
        y0_ref[...] = y0

    tok = pl.BlockSpec((tt, w), lambda i: (i, 0))
    return pl.pallas_call(
        body, name="s5_post_fwd", grid=(t // tt,),
        in_specs=[tok, tok, pl.BlockSpec((1, w), lambda i: (0, 0)), pl.BlockSpec((w, w), lambda i: (0, 0))],
        out_specs=[tok, tok],
        out_shape=[jax.ShapeDtypeStruct((t, w), BF16), jax.ShapeDtypeStruct((t, w), F32)],
        compiler_params=_params(("parallel",)),
    )(yc, u, d, w_glu)


def _s5_post_bwd(y0, u, d, w_glu, dout):
    t, w = y0.shape
    tt = _tile(t, (512, 256, 128))

    def body(y0_ref, u_ref, d_ref, w_ref, do_ref, dy0_ref, y_ref, dz_ref, du_ref, dd_ref):
        @pl.when(pl.program_id(0) == 0)
        def _():
            dd_ref[...] = jnp.zeros_like(dd_ref)

        y0 = y0_ref[...]
        y = _gelu(y0)
        s = _sigmoid(_dot(y, w_ref[...]))
        do = do_ref[...]
        dz = do * y * s * (1.0 - s)
        dy = do * s + _dot_nt(dz, w_ref[...])
        dy0 = dy * _gelu_grad(y0)
        dy0_ref[...] = dy0.astype(dy0_ref.dtype)
        y_ref[...] = y.astype(y_ref.dtype)
        dz_ref[...] = dz.astype(dz_ref.dtype)
        du_ref[...] = dy0 * d_ref[...]
        dd_ref[...] += jnp.sum(dy0 * u_ref[...], axis=0, keepdims=True)

    tok = pl.BlockSpec((tt, w), lambda i: (i, 0))
    vec = pl.BlockSpec((1, w), lambda i: (0, 0))
    return pl.pallas_call(
        body, name="s5_post_bwd", grid=(t // tt,),
        in_specs=[tok, tok, vec, pl.BlockSpec((w, w), lambda i: (0, 0)), tok],
        out_specs=[tok, tok, tok, tok, vec],
        out_shape=[jax.ShapeDtypeStruct((t, w), BF16), jax.ShapeDtypeStruct((t, w), BF16),
                   jax.ShapeDtypeStruct((t, w), BF16), jax.ShapeDtypeStruct((t, w), F32),
                   jax.ShapeDtypeStruct((1, w), F32)],
        compiler_params=_params(("arbitrary",)),
    )(y0, u, d, w_glu, dout)


def _s5_params(a_re, a_im, b_re, b_im, c_re, c_im, log_dt):
    dt = jnp.exp(log_dt)[:, None]
    mag = jnp.exp(dt * a_re)
    ang = dt * a_im
    abar_re = mag * jnp.cos(ang)
    abar_im = mag * jnp.sin(ang)
    den = a_re * a_re + a_im * a_im
    zr = abar_re - 1.0
    zi = abar_im
    coef_re = ((zr * a_re + zi * a_im) / den)[..., None]
    coef_im = ((zi * a_re - zr * a_im) / den)[..., None]
    bb_re = coef_re * b_re - coef_im * b_im
    bb_im = coef_re * b_im + coef_im * b_re
    eye = jnp.eye(S5_GROUPS, dtype=F32)

    def dense_in(bb):
        return jnp.einsum('gnp,gh->gphn', bb, eye).reshape(S5_WIDTH, S5_NSTATE)

    def dense_out(cc):
        return jnp.einsum('gpn,gh->gnhp', cc, eye).reshape(S5_NSTATE, S5_WIDTH)

    abar = jnp.stack([abar_re.reshape(S5_ROWS, LANES), abar_im.reshape(S5_ROWS, LANES)])
    bbd = jnp.concatenate([dense_in(bb_re), dense_in(bb_im)], axis=1)
    ccd = jnp.concatenate([dense_out(c_re), dense_out(-c_im)], axis=0)
    return abar, bbd, ccd


def _tok_tile(t):
    return _tile(t, (256, 128))


def _rms_fwd(x, w):
    t, d = x.shape
    tt = _tok_tile(t)

    def body(x_ref, w_ref, o_ref):
        xv = x_ref[...]
        r = lax.rsqrt(jnp.mean(xv * xv, axis=-1, keepdims=True) + NORM_EPS)
        o_ref[...] = (xv * r * w_ref[...]).astype(o_ref.dtype)

    tok = pl.BlockSpec((tt, d), lambda i: (i, 0))
    return pl.pallas_call(
        body, name="rms_fwd", grid=(t // tt,), in_specs=[tok, pl.BlockSpec((1, d), lambda i: (0, 0))],
        out_specs=tok, out_shape=jax.ShapeDtypeStruct((t, d), BF16), compiler_params=_params(("parallel",)),
    )(x, w)


def _rms_bwd_math(xv, wv, dy):
    r = lax.rsqrt(jnp.mean(xv * xv, axis=-1, keepdims=True) + NORM_EPS)
    xh = xv * r
    dxh = dy * wv
    dx = r * (dxh - xh * jnp.mean(dxh * xh, axis=-1, keepdims=True))
    return dx, jnp.sum(dy * xh, axis=0, keepdims=True)


def _rms_bwd(x, w, dy, dres):
    t, d = x.shape
    tt = _tok_tile(t)

    def body(x_ref, w_ref, dy_ref, dr_ref, dx_ref, dw_ref):
        @pl.when(pl.program_id(0) == 0)
        def _():
            dw_ref[...] = jnp.zeros_like(dw_ref)

        dx, dw = _rms_bwd_math(x_ref[...], w_ref[...], dy_ref[...])
        dx_ref[...] = dr_ref[...] + dx
        dw_ref[...] += dw

    tok = pl.BlockSpec((tt, d), lambda i: (i, 0))
    vec = pl.BlockSpec((1, d), lambda i: (0, 0))
    return pl.pallas_call(
        body, name="rms_bwd", grid=(t // tt,), in_specs=[tok, vec, tok, tok], out_specs=[tok, vec],
        out_shape=[jax.ShapeDtypeStruct((t, d), F32), jax.ShapeDtypeStruct((1, d), F32)],
        compiler_params=_params(("arbitrary",)),
    )(x, w, dy, dres)


def _loss_head(x, w, target):
    t, d = x.shape
    tt = _tok_tile(t)

    def body(x_ref, w_ref, t_ref, l_ref, dx_ref, dw_ref):
        @pl.when(pl.program_id(0) == 0)
        def _():
            l_ref[...] = jnp.zeros_like(l_ref)
            dw_ref[...] = jnp.zeros_like(dw_ref)

        xv = x_ref[...]
        wv = w_ref[...]
        r = lax.rsqrt(jnp.mean(xv * xv, axis=-1, keepdims=True) + NORM_EPS)
        err = xv * r * wv - t_ref[...]
        row = jnp.sum(err * err, axis=-1, keepdims=True)
        l_ref[...] += (0.5 / d) * jnp.sum(row, axis=0, keepdims=True)
        dx, dw = _rms_bwd_math(xv, wv, err * (1.0 / d))
        dx_ref[...] = dx
        dw_ref[...] += dw

    tok = pl.BlockSpec((tt, d), lambda i: (i, 0))
    vec = pl.BlockSpec((1, d), lambda i: (0, 0))
    return pl.pallas_call(
        body, name="loss_head", grid=(t // tt,), in_specs=[tok, vec, tok],
        out_specs=[pl.BlockSpec((1, 1), lambda i: (0, 0)), tok, vec],
        out_shape=[jax.ShapeDtypeStruct((1, 1), F32), jax.ShapeDtypeStruct((t, d), F32), jax.ShapeDtypeStruct((1, d), F32)],
        compiler_params=_params(("arbitrary",)),
    )(x, w, target)


def _swiglu_fwd(gu):
    t = gu.shape[0]
    tt = _tok_tile(t)

    def body(g_ref, u_ref, o_ref):
        g = g_ref[...]
        o_ref[...] = (g * _sigmoid(g) * u_ref[...]).astype(o_ref.dtype)

    return pl.pallas_call(
        body, name="swiglu_fwd", grid=(t // tt,),
        in_specs=[pl.BlockSpec((tt, D_FF), lambda i: (i, 0)), pl.BlockSpec((tt, D_FF), lambda i: (i, 1))],
        out_specs=pl.BlockSpec((tt, D_FF), lambda i: (i, 0)),
        out_shape=jax.ShapeDtypeStruct((t, D_FF), BF16), compiler_params=_params(("parallel",)),
    )(gu, gu)


def _swiglu_bwd(gu, da):
    t = gu.shape[0]
    tt = _tok_tile(t)

    def body(g_ref, u_ref, da_ref, o_ref):
        g = g_ref[...]
        s = _sigmoid(g)
        da_v = da_ref[...]
        o_ref[:, :D_FF] = (da_v * u_ref[...] * (s * (1.0 + g * (1.0 - s)))).astype(o_ref.dtype)
        o_ref[:, D_FF:] = (da_v * g * s).astype(o_ref.dtype)

    lo = pl.BlockSpec((tt, D_FF), lambda i: (i, 0))
    hi = pl.BlockSpec((tt, D_FF), lambda i: (i, 1))
    return pl.pallas_call(
        body, name="swiglu_bwd", grid=(t // tt,), in_specs=[lo, hi, lo],
        out_specs=pl.BlockSpec((tt, 2 * D_FF), lambda i: (i, 0)),
        out_shape=jax.ShapeDtypeStruct((t, 2 * D_FF), BF16), compiler_params=_params(("parallel",)),
    )(gu, gu, da)


def _adamw(w, g, m, v):
    r, c = w.shape
    tr = _tile(r, (512, 256, 128, 64, 32, 16, 8))
    c1 = 1.0 - ADAM_B1 ** ADAM_STEP
    c2 = 1.0 - ADAM_B2 ** ADAM_STEP

    def body(w_ref, g_ref, m_ref, v_ref, d_ref, nm_ref, nv_ref):
        gv = g_ref[...]
        nm = ADAM_B1 * m_ref[...] + (1.0 - ADAM_B1) * gv
        nv = ADAM_B2 * v_ref[...] + (1.0 - ADAM_B2) * (gv * gv)
        d_ref[...] = -ADAM_LR * ((nm / c1) / (jnp.sqrt(nv / c2) + ADAM_EPS) + ADAM_WD * w_ref[...])
        nm_ref[...] = nm
        nv_ref[...] = nv

    blk = pl.BlockSpec((tr, c), lambda i: (i, 0))
    return pl.pallas_call(
        body, name="adamw", grid=(r // tr,), in_specs=[blk] * 4, out_specs=[blk] * 3,
        out_shape=[jax.ShapeDtypeStruct((r, c), F32)] * 3, compiler_params=_params(("parallel",)),
    )(w, g, m, v)


def _ffn_fwd(x, nw, wgu, wd):
    h = _rms_fwd(x, nw)
    gu = _mm(h, wgu, name="ffn_in")
    a = _swiglu_fwd(gu)
    return _mm(a, wd, name="ffn_out", res=x, scale=0.5), (x, h, gu, a)


def _ffn_bwd(dxo, res, nw, wgu, wd):
    x, h, gu, a = res
    da = _mm(dxo, wd, name="ffn_out_dx", nt=True, scale=0.5)
    dwd = _mm_tn(a, dxo, name="ffn_out_dw", scale=0.5)
    dgu = _swiglu_bwd(gu, da)
    dwgu = _mm_tn(h, dgu, name="ffn_in_dw")
    dh = _mm(dgu, wgu, name="ffn_in_dx", nt=True)
    dx, dnw = _rms_bwd(x, nw, dh, dxo)
    return dx, dnw, dwgu, dwd


def _hg_lower_bounds(logits):
    p = jax.nn.softmax(logits, axis=0)
    return jnp.cumsum(p, axis=0) - p[0]


def _even_fwd(x1, p, nb, l):
    t = nb * l
    h = _rms_fwd(x1, p["mix_norm"])
    proj = _mm(h, p["w_in"], name="ev_in")
    proj3 = proj.reshape(nb, l, EV_IN)
    ya, hst = _hgrn2_fwd(proj3, p["lb"], p["hg_nw"])
    u = proj[:, 4 * HG_QK:]
    bu = _mm(u, p["bbd"], name="s5_in")
    hs = _s5_scan_fwd(bu.reshape(nb, l, 2, S5_ROWS, LANES), p["abar"])
    yc = _mm(hs.reshape(t, 2 * S5_NSTATE), p["ccd"], name="s5_out")
    yb, y0 = _s5_post_fwd(yc, u, p["s5_d"], p["w_glu"])
    ycat = jnp.concatenate([ya.reshape(t, HG_QK), yb], axis=1)
    x2 = _mm(ycat, p["w_out"], name="mix_out", res=x1)
    return x2, (x1, h, proj3, hst, u, hs, y0, ycat)


def _even_bwd(dx2, res, p, nb, l):
    x1, h, proj3, hst, u, hs, y0, ycat = res
    t = nb * l
    dycat = _mm(dx2, p["w_out"], name="mix_out_dx", nt=True)
    dw_out = _mm_tn(ycat, dx2, name="mix_out_dw")
    dq, df, di, dg, dlb, dhg_nw = _hgrn2_bwd(proj3, p["lb"], p["hg_nw"], hst, dycat.reshape(nb, l, D_MODEL))
    dy0, y, dz, du_d, dd = _s5_post_bwd(y0, u, p["s5_d"], p["w_glu"], dycat[:, HG_QK:])
    dw_glu = _mm_tn(y, dz, name="s5_glu_dw")
    dhs = _mm(dy0, p["ccd"], name="s5_out_dx", nt=True)
    dccd = _mm_tn(hs.reshape(t, 2 * S5_NSTATE), dy0, name="s5_out_dw")
    g5, dabar = _s5_scan_bwd(dhs.reshape(nb, l, 2, S5_ROWS, LANES), hs, p["abar"])
    g2 = g5.reshape(t, 2 * S5_NSTATE)
    du = _mm(g2, p["bbd"], name="s5_in_dx", nt=True, res=du_d)
    dbbd = _mm_tn(u, g2, name="s5_in_dw")
    dproj = jnp.concatenate([d.reshape(t, HG_QK).astype(BF16) for d in (dq, df, di, dg)] + [du.astype(BF16)], axis=1)
    dw_in = _mm_tn(h, dproj, name="ev_in_dw")
    dh = _mm(dproj, p["w_in"], name="ev_in_dx", nt=True)
    dx1, dmix = _rms_bwd(x1, p["mix_norm"], dh, dx2)
    small = dict(mix_norm=dmix, lb=dlb.sum(0).reshape(HG_QK), hg_nw=dhg_nw.sum((0, 1)).reshape(HEAD_DIM),
                 abar=dabar.sum(0), bbd=dbbd, ccd=dccd, s5_d=dd.reshape(S5_WIDTH))
    return dx1, dict(w_in=dw_in, w_glu=dw_glu, w_out=dw_out), small


def _tok_major(xs):
    return jnp.transpose(xs, (0, 2, 1))[..., None]


def _odd_fwd(x1, p, nb, l):
    t = nb * l
    h = _rms_fwd(x1, p["mix_norm"])
    proj3 = _mm(h, p["w_in"], name="od_in").reshape(nb, l, OD_IN_PAD)
    qkv = _gdn_pre_fwd(proj3, p["conv_w"])
    n_wide = GDN_QKV + GDN_HEADS * HEAD_DIM
    bl = _tok_major(proj3[..., n_wide:n_wide + GDN_HEADS])
    al = _tok_major(proj3[..., n_wide + GDN_HEADS:n_wide + 2 * GDN_HEADS])
    y, st = _gdn_fwd(qkv, proj3, bl, al, p["a_log"], p["dt_bias"], p["gdn_nw"])
    y2 = y.reshape(t, D_MODEL)
    x2 = _mm(y2, p["w_out"], name="mix_out", res=x1)
    return x2, (x1, h, proj3, qkv, bl, al, st, y2)


def _odd_bwd(dx2, res, p, nb, l):
    x1, h, proj3, qkv, bl, al, st, y2 = res
    t = nb * l
    dy = _mm(dx2, p["w_out"], name="mix_out_dx", nt=True)
    dw_out = _mm_tn(y2, dx2, name="mix_out_dw")
    dq, dk, dv, dgate, dbl, dal, dalog, ddtb, dnw = _gdn_bwd(
        qkv, proj3, bl, al, p["a_log"], p["dt_bias"], p["gdn_nw"], st, dy.reshape(nb, l, D_MODEL))
    dx_qkv, dcw = _gdn_pre_bwd(proj3, p["conv_w"], jnp.concatenate([dq, dk, dv], axis=-1))
    dsmall = jnp.concatenate([jnp.transpose(d[..., 0], (0, 2, 1)) for d in (dbl, dal)], axis=-1)
    pad = jnp.zeros((nb, l, OD_IN_PAD - OD_IN), BF16)
    dproj = jnp.concatenate([dx_qkv.astype(BF16), dgate.astype(BF16), dsmall.astype(BF16), pad], axis=-1)
    dproj = dproj.reshape(t, OD_IN_PAD)
    dw_in = _mm_tn(h, dproj, name="od_in_dw")[:, :OD_IN]
    dh = _mm(dproj, p["w_in"], name="od_in_dx", nt=True)
    dx1, dmix = _rms_bwd(x1, p["mix_norm"], dh, dx2)
    small = dict(mix_norm=dmix, a_log=dalog.sum(0).reshape(GDN_HEADS), dt_bias=ddtb.sum(0).reshape(GDN_HEADS),
                 gdn_nw=dnw.sum((0, 1)).reshape(HEAD_DIM))
    return dx1, dict(w_in=dw_in, conv_w=dcw.sum(0), w_out=dw_out), small


def _local_step(x, target, big, small):
    nb, l, _ = x.shape
    t = nb * l
    lbs, lbs_vjp = jax.vjp(_hg_lower_bounds, small["hg_lb_logits"])
    row = lambda v: v.reshape(1, -1)
    layers = []
    s5_vjps = []
    for layer in range(DEPTH):
        j = layer // 2
        p = dict(mix_norm=row(small["mix_norm"][layer]))
        if layer % 2 == 0:
            (abar, bbd, ccd), s5_vjp = jax.vjp(
                _s5_params, small["s5_a_re"][j], small["s5_a_im"][j], small["s5_b_re"][j], small["s5_b_im"][j],
                small["s5_c_re"][j], small["s5_c_im"][j], small["s5_log_dt"][j])
            s5_vjps.append(s5_vjp)
            p.update(w_in=big["ev_w_in"][j], w_glu=big["s5_w_glu"][j], w_out=big["ev_w_out"][j],
                     lb=lbs[j].reshape(HG_HEADS, 1, HEAD_DIM), hg_nw=row(small["hg_norm_w"][j]),
                     abar=abar, bbd=bbd.astype(BF16), ccd=ccd.astype(BF16), s5_d=row(small["s5_d"][j]))
        else:
            p.update(w_in=big["od_w_in"][j], conv_w=big["gdn_conv_w"][j], w_out=big["od_w_out"][j],
                     a_log=small["gdn_a_log"][j].reshape(GDN_HEADS, 1, 1),
                     dt_bias=small["gdn_dt_bias"][j].reshape(GDN_HEADS, 1, 1), gdn_nw=row(small["gdn_norm_w"][j]))
        layers.append(p)

    xs = x.reshape(t, D_MODEL)
    saved = []
    for layer in range(DEPTH):
        p = layers[layer]
        xs, r1 = _ffn_fwd(xs, row(small["ffn1_norm"][layer]), big["ffn1_wgu"][layer], big["ffn1_wd"][layer])
        xs, r2 = (_even_fwd if layer % 2 == 0 else _odd_fwd)(xs, p, nb, l)
        xs, r3 = _ffn_fwd(xs, row(small["ffn2_norm"][layer]), big["ffn2_wgu"][layer], big["ffn2_wd"][layer])
        saved.append((r1, r2, r3))
    loss, dx, dfinal = _loss_head(xs, row(small["final_norm"]), target.reshape(t, D_MODEL))

    gb = {k: [None] * DEPTH for k in ("ffn1_w_gate", "ffn1_w_up", "ffn1_w_down", "ffn2_w_gate", "ffn2_w_up", "ffn2_w_down")}
    gb.update({k: [None] * 2 for k in ("ev_w_in", "s5_w_glu", "ev_w_out", "od_w_in", "gdn_conv_w", "od_w_out")})
    gs = {k: [None] * DEPTH for k in ("ffn1_norm", "mix_norm", "ffn2_norm")}
    gs.update({k: [None] * 2 for k in ("hg_norm_w", "s5_a_re", "s5_a_im", "s5_b_re", "s5_b_im", "s5_c_re", "s5_c_im",
                                       "s5_d", "s5_log_dt", "gdn_a_log", "gdn_dt_bias", "gdn_norm_w")})
    dlbs = [None] * 2
    for layer in reversed(range(DEPTH)):
        j = layer // 2
        p = layers[layer]
        r1, r2, r3 = saved[layer]
        dx, dnw, dwgu, dwd = _ffn_bwd(dx, r3, row(small["ffn2_norm"][layer]), big["ffn2_wgu"][layer], big["ffn2_wd"][layer])
        gs["ffn2_norm"][layer] = dnw[0]
        gb["ffn2_w_gate"][layer], gb["ffn2_w_up"][layer], gb["ffn2_w_down"][layer] = dwgu[:, :D_FF], dwgu[:, D_FF:], dwd
        if layer % 2 == 0:
            dx, gw, sm = _even_bwd(dx, r2, p, nb, l)
            gb["ev_w_in"][j], gb["s5_w_glu"][j], gb["ev_w_out"][j] = gw["w_in"], gw["w_glu"], gw["w_out"]
            dlbs[j] = sm["lb"]
            gs["hg_norm_w"][j] = sm["hg_nw"]
            gs["s5_d"][j] = sm["s5_d"]
            (gs["s5_a_re"][j], gs["s5_a_im"][j], gs["s5_b_re"][j], gs["s5_b_im"][j], gs["s5_c_re"][j],
             gs["s5_c_im"][j], gs["s5_log_dt"][j]) = s5_vjps[j]((sm["abar"], sm["bbd"], sm["ccd"]))
        else:
            dx, gw, sm = _odd_bwd(dx, r2, p, nb, l)
            gb["od_w_in"][j], gb["gdn_conv_w"][j], gb["od_w_out"][j] = gw["w_in"], gw["conv_w"], gw["w_out"]
            gs["gdn_a_log"][j], gs["gdn_dt_bias"][j], gs["gdn_norm_w"][j] = sm["a_log"], sm["dt_bias"], sm["gdn_nw"]
        gs["mix_norm"][layer] = sm["mix_norm"][0]
        dx, dnw, dwgu, dwd = _ffn_bwd(dx, r1, row(small["ffn1_norm"][layer]), big["ffn1_wgu"][layer], big["ffn1_wd"][layer])
        gs["ffn1_norm"][layer] = dnw[0]
        gb["ffn1_w_gate"][layer], gb["ffn1_w_up"][layer], gb["ffn1_w_down"][layer] = dwgu[:, :D_FF], dwgu[:, D_FF:], dwd
    gbig = {k: jnp.stack(v) for k, v in gb.items()}
    gsmall = {k: jnp.stack(v) for k, v in gs.items()}
    gsmall["hg_lb_logits"] = lbs_vjp(jnp.stack(dlbs))[0]
    gsmall["final_norm"] = dfinal[0]
    return loss[0, 0], dx.reshape(nb, l, D_MODEL), gbig, gsmall


def _here():
    return lax.axis_index("x"), lax.axis_index("y"), lax.axis_index("c")


def _all_gather(block, name):
    r, c = block.shape

    def body(x_ref, out_ref, send_sems, recv_sems, local_sem):
        x, y, cc = _here()
        me, sibling = (x, y, cc), (x, y, 1 - cc)
        chips = [(1 - x, y), (x, 1 - y), (1 - x, 1 - y)]

        def slot(px, py, pc):
            return out_ref.at[4 * px + 2 * py + pc]

        def copy(k, blk, to, src=None):
            return pltpu.make_async_remote_copy(
                src_ref=slot(*blk) if src is None else src, dst_ref=slot(*blk),
                send_sem=send_sems.at[k], recv_sem=recv_sems.at[k], device_id=to, device_id_type=MESH)

        mine = pltpu.make_async_copy(x_ref, slot(*me), local_sem)
        mine.start()
        first = [copy(0, me, sibling, src=x_ref)]
        first += [copy(1 + j, me, (*chip, cc), src=x_ref) for j, chip in enumerate(chips)]
        for cp in first:
            cp.start()
        passed = [copy(4 + j, (*chip, cc), sibling) for j, chip in enumerate(chips)]
        for j, chip in enumerate(chips):
            copy(1 + j, (*chip, cc), me).wait_recv()
            passed[j].start()
        copy(0, sibling, me).wait_recv()
        for j, chip in enumerate(chips):
            copy(4 + j, (*chip, 1 - cc), me).wait_recv()
        for cp in first + passed:
            cp.wait_send()
        mine.wait()

    return pl.pallas_call(
        body, name=name,
        in_specs=[pl.BlockSpec(memory_space=pl.ANY)], out_specs=pl.BlockSpec(memory_space=pl.ANY),
        out_shape=jax.ShapeDtypeStruct((N_DEV, r, c), block.dtype),
        scratch_shapes=[pltpu.SemaphoreType.DMA((7,)), pltpu.SemaphoreType.DMA((7,)), pltpu.SemaphoreType.DMA],
    )(block)


def _rs_core_exchange(gb):
    _, n_chip, r, c = gb.shape

    def body(g_ref, out_ref, send_sem, recv_sem):
        x, y, cc = _here()
        cp = pltpu.make_async_remote_copy(
            src_ref=g_ref.at[1 - cc], dst_ref=out_ref, send_sem=send_sem, recv_sem=recv_sem,
            device_id=(x, y, 1 - cc), device_id_type=MESH)
        cp.start()
        cp.wait()

    return pl.pallas_call(
        body, name="rs_core_exchange",
        in_specs=[pl.BlockSpec(memory_space=pl.ANY)], out_specs=pl.BlockSpec(memory_space=pl.ANY),
        out_shape=jax.ShapeDtypeStruct((n_chip, r, c), gb.dtype),
        scratch_shapes=[pltpu.SemaphoreType.DMA, pltpu.SemaphoreType.DMA],
    )(gb)


def _rs_chip_sum(g, recv, core):
    _, n_chip, r, c = g.shape
    tr = _tile(r, (512, 256, 128, 64, 32, 16))

    def body(core_ref, g_ref, r_ref, o_ref, ob_ref):
        s = g_ref[...] + r_ref[...].astype(F32)
        o_ref[...] = s
        ob_ref[...] = s.astype(ob_ref.dtype)

    blk = pl.BlockSpec((None, tr, c), lambda j, i, core_ref: (j, i, 0))
    return pl.pallas_call(
        body, name="rs_chip_sum",
        grid_spec=pltpu.PrefetchScalarGridSpec(
            num_scalar_prefetch=1, grid=(n_chip, r // tr),
            in_specs=[pl.BlockSpec((None, None, tr, c), lambda j, i, core_ref: (core_ref[0], j, i, 0)), blk],
            out_specs=[blk, blk]),
        out_shape=[jax.ShapeDtypeStruct((n_chip, r, c), F32), jax.ShapeDtypeStruct((n_chip, r, c), BF16)],
        compiler_params=_params(("parallel", "parallel")),
    )(core, g, recv)


def _rs_chip_exchange(pb):
    _, r, c = pb.shape

    def body(p_ref, out_ref, send_sems, recv_sems):
        x, y, cc = _here()
        peers = [(1 - x, y), (x, 1 - y), (1 - x, 1 - y)]
        cps = [pltpu.make_async_remote_copy(
            src_ref=p_ref.at[2 * px + py], dst_ref=out_ref.at[k], send_sem=send_sems.at[k], recv_sem=recv_sems.at[k],
            device_id=(px, py, cc), device_id_type=MESH) for k, (px, py) in enumerate(peers)]
        for cp in cps:
            cp.start()
        for cp in cps:
            cp.wait()

    return pl.pallas_call(
        body, name="rs_chip_exchange",
        in_specs=[pl.BlockSpec(memory_space=pl.ANY)], out_specs=pl.BlockSpec(memory_space=pl.ANY),
        out_shape=jax.ShapeDtypeStruct((3, r, c), pb.dtype),
        scratch_shapes=[pltpu.SemaphoreType.DMA((3,)), pltpu.SemaphoreType.DMA((3,))],
    )(pb)


def _rs_final_sum(p, recv, chip):
    _, r, c = p.shape
    tr = _tile(r, (512, 256, 128, 64, 32, 16))

    def body(chip_ref, p_ref, r_ref, o_ref):
        s = p_ref[...]
        for k in range(3):
            s = s + r_ref[k].astype(F32)
        o_ref[...] = s

    return pl.pallas_call(
        body, name="rs_final_sum",
        grid_spec=pltpu.PrefetchScalarGridSpec(
            num_scalar_prefetch=1, grid=(r // tr,),
            in_specs=[pl.BlockSpec((None, tr, c), lambda i, chip_ref: (chip_ref[0], i, 0)),
                      pl.BlockSpec((3, tr, c), lambda i, chip_ref: (0, i, 0))],
            out_specs=pl.BlockSpec((tr, c), lambda i, chip_ref: (i, 0))),
        out_shape=jax.ShapeDtypeStruct((r, c), F32),
        compiler_params=_params(("parallel",)),
    )(chip, p, recv)


def _sum_slots(g):
    n, r, c = g.shape
    tr = _tile(r, (280, 256, 128, 64, 32, 16, 8))

    def body(g_ref, o_ref):
        s = g_ref[0]
        for k in range(1, n):
            s = s + g_ref[k]
        o_ref[...] = s

    return pl.pallas_call(
        body, name="sum_slots", grid=(r // tr,),
        in_specs=[pl.BlockSpec((n, tr, c), lambda i: (0, i, 0))], out_specs=pl.BlockSpec((tr, c), lambda i: (i, 0)),
        out_shape=jax.ShapeDtypeStruct((r, c), F32), compiler_params=_params(("parallel",)),
    )(g)


_WEIGHTS = ['ffn1_norm', 'ffn1_w_gate', 'ffn1_w_up', 'ffn1_w_down', 'mix_norm', 'ffn2_norm', 'ffn2_w_gate', 'ffn2_w_up',
            'ffn2_w_down', 'ev_w_in', 'hg_lb_logits', 'hg_norm_w', 's5_a_re', 's5_a_im', 's5_b_re', 's5_b_im', 's5_c_re',
            's5_c_im', 's5_d', 's5_log_dt', 's5_w_glu', 'ev_w_out', 'od_w_in', 'gdn_conv_w', 'gdn_a_log', 'gdn_dt_bias',
            'gdn_norm_w', 'od_w_out', 'final_norm']
_SHARDED = {'ffn1_w_gate': 2, 'ffn1_w_up': 2, 'ffn1_w_down': 1, 'ffn2_w_gate': 2, 'ffn2_w_up': 2, 'ffn2_w_down': 1,
            'ev_w_in': 2, 's5_w_glu': 1, 'ev_w_out': 1, 'od_w_in': 2, 'gdn_conv_w': 2, 'od_w_out': 1}
_REPLICATED = [n for n in _WEIGHTS if n not in _SHARDED]
_ROW = 1024
_ROW_ALIGN = 16


def _to_rows(flat, align=_ROW_ALIGN):
    n = flat.shape[-1]
    rows = -(-n // (_ROW * align)) * align
    pad = [(0, 0)] * (flat.ndim - 1) + [(0, rows * _ROW - n)]
    return jnp.pad(flat, pad).reshape(flat.shape[:-1] + (rows, _ROW))


def _unshard(stacked, axis):
    moved = jnp.moveaxis(stacked, 0, axis)
    shape = list(stacked.shape[1:])
    shape[axis] *= N_DEV
    return moved.reshape(shape)


def _shard_major(whole, axis):
    shape = list(whole.shape)
    split = shape[:axis] + [N_DEV, shape[axis] // N_DEV] + shape[axis + 1:]
    return jnp.moveaxis(whole.reshape(split), axis, 0).reshape(N_DEV, -1)


def _gather_weights(w):
    parts = []
    for n in _SHARDED:
        if n == 'gdn_conv_w':
            hi = w[n].astype(BF16)
            mid = (w[n] - hi.astype(F32)).astype(BF16)
            lo = (w[n] - hi.astype(F32) - mid.astype(F32)).astype(BF16)
            parts.append(jnp.stack([hi, mid, lo], axis=-1).reshape(-1))
        else:
            parts.append(w[n].astype(BF16).reshape(-1))
    gathered = _all_gather(_to_rows(jnp.concatenate(parts)), "gather_weights").reshape(N_DEV, -1)
    whole, off = {}, 0
    for n, axis in _SHARDED.items():
        shp = w[n].shape
        size = math.prod(shp) * (3 if n == 'gdn_conv_w' else 1)
        piece = gathered[:, off:off + size]
        off += size
        if n == 'gdn_conv_w':
            piece = piece.reshape((N_DEV,) + shp + (3,)).astype(F32)
            piece = (piece[..., 0] + piece[..., 1]) + piece[..., 2]
        else:
            piece = piece.reshape((N_DEV,) + shp)
        whole[n] = _unshard(piece, axis)
    big = {}
    for f in ('ffn1', 'ffn2'):
        big[f + '_wgu'] = jnp.concatenate([whole[f + '_w_gate'], whole[f + '_w_up']], axis=-1)
        big[f + '_wd'] = whole[f + '_w_down']
    for n in ('ev_w_in', 's5_w_glu', 'ev_w_out', 'gdn_conv_w', 'od_w_out'):
        big[n] = whole[n]
    big['od_w_in'] = jnp.pad(whole['od_w_in'], ((0, 0), (0, 0), (0, OD_IN_PAD - OD_IN)))
    return big


def _scatter_grads(gbig, w):
    x, y, c = _here()
    flat = jnp.concatenate([_shard_major(gbig[n], axis) for n, axis in _SHARDED.items()], axis=1)
    rows = _to_rows(flat)
    g = jnp.swapaxes(rows.reshape((4, 2) + rows.shape[1:]), 0, 1)
    from_sibling = _rs_core_exchange(g.astype(BF16))
    chip_sum, chip_sum_bf = _rs_chip_sum(g, from_sibling, jnp.reshape(c, (1,)).astype(jnp.int32))
    from_chips = _rs_chip_exchange(chip_sum_bf)
    mine = _rs_final_sum(chip_sum, from_chips, jnp.reshape(2 * x + y, (1,)).astype(jnp.int32)).reshape(-1)
    out, off = {}, 0
    for n in _SHARDED:
        size = math.prod(w[n].shape)
        out[n] = mine[off:off + size].reshape(w[n].shape)
        off += size
    return out


def _all_reduce_small(gsmall, loss):
    flat = jnp.concatenate([gsmall[n].reshape(-1) for n in _REPLICATED] + [loss.reshape(1)])
    total = _sum_slots(_all_gather(_to_rows(flat, 8), "gather_small")).reshape(-1)
    out, off = {}, 0
    for n in _REPLICATED:
        size = math.prod(gsmall[n].shape)
        out[n] = total[off:off + size].reshape(gsmall[n].shape)
        off += size
    return out, total[off]


def _as_2d(a):
    return a.reshape(-1, a.shape[-1])


def kernel(x, ffn1_norm, ffn1_w_gate, ffn1_w_up, ffn1_w_down, mix_norm, ffn2_norm, ffn2_w_gate, ffn2_w_up, ffn2_w_down, ev_w_in, hg_lb_logits, hg_norm_w, s5_a_re, s5_a_im, s5_b_re, s5_b_im, s5_c_re, s5_c_im, s5_d, s5_log_dt, s5_w_glu, ev_w_out, od_w_in, gdn_conv_w, gdn_a_log, gdn_dt_bias, gdn_norm_w, od_w_out, final_norm, loss_target, m_ffn1_norm, m_ffn1_w_gate, m_ffn1_w_up, m_ffn1_w_down, m_mix_norm, m_ffn2_norm, m_ffn2_w_gate, m_ffn2_w_up, m_ffn2_w_down, m_ev_w_in, m_hg_lb_logits, m_hg_norm_w, m_s5_a_re, m_s5_a_im, m_s5_b_re, m_s5_b_im, m_s5_c_re, m_s5_c_im, m_s5_d, m_s5_log_dt, m_s5_w_glu, m_ev_w_out, m_od_w_in, m_gdn_conv_w, m_gdn_a_log, m_gdn_dt_bias, m_gdn_norm_w, m_od_w_out, m_final_norm, v_ffn1_norm, v_ffn1_w_gate, v_ffn1_w_up, v_ffn1_w_down, v_mix_norm, v_ffn2_norm, v_ffn2_w_gate, v_ffn2_w_up, v_ffn2_w_down, v_ev_w_in, v_hg_lb_logits, v_hg_norm_w, v_s5_a_re, v_s5_a_im, v_s5_b_re, v_s5_b_im, v_s5_c_re, v_s5_c_im, v_s5_d, v_s5_log_dt, v_s5_w_glu, v_ev_w_out, v_od_w_in, v_gdn_conv_w, v_gdn_a_log, v_gdn_dt_bias, v_gdn_norm_w, v_od_w_out, v_final_norm):
    given = dict(locals())
    w = {n: given[n] for n in _WEIGHTS}
    big = _gather_weights(w)
    small = {n: w[n] for n in _REPLICATED}
    loss_part, grad_x, gbig, gsmall = _local_step(given['x'], given['loss_target'], big, small)
    grads = _scatter_grads(gbig, w)
    gsum, loss = _all_reduce_small(gsmall, loss_part)
    grads.update(gsum)

    delta, new_m, new_v = {}, {}, {}
    for n in _SHARDED:
        d, nm, nv = _adamw(_as_2d(w[n]), _as_2d(grads[n]), _as_2d(given['m_' + n]), _as_2d(given['v_' + n]))
        delta[n], new_m[n], new_v[n] = (a.reshape(w[n].shape) for a in (d, nm, nv))
    cat = lambda pre, src: _to_rows(jnp.concatenate([src[pre + n].reshape(-1) for n in _REPLICATED]), 8)
    d, nm, nv = _adamw(cat('', w), cat('', grads), cat('m_', given), cat('v_', given))
    off = 0
    for n in _REPLICATED:
        size = math.prod(w[n].shape)
        delta[n], new_m[n], new_v[n] = (a.reshape(-1)[off:off + size].reshape(w[n].shape) for a in (d, nm, nv))
        off += size
    return (loss, grad_x, *[grads[n] for n in _WEIGHTS], *[delta[n] for n in _WEIGHTS],
            *[new_m[n] for n in _WEIGHTS], *[new_v[n] for n in _WEIGHTS])
```

```python
import math

import jax
import jax.numpy as jnp
from jax import lax
from jax.experimental import pallas as pl
from jax.experimental.pallas import tpu as pltpu

F32 = jnp.float32
BF16 = jnp.bfloat16
HI = lax.Precision.HIGHEST

D_MODEL = 1024
DEPTH = 4
D_FF = 2816
NORM_EPS = 1e-6
F_MIN = 1e-6
CHUNK = 64
HG_HEADS = 4
HEAD_DIM = 128
HG_QK = HG_HEADS * HEAD_DIM
S5_WIDTH = 512
S5_GROUP = 16
S5_GROUPS = 32
S5_STATE = 64
S5_NSTATE = S5_GROUPS * S5_STATE
EV_IN = 2560
GDN_HEADS = 8
GDN_QKV = 3 * GDN_HEADS * HEAD_DIM
CONV_W = 4
OD_IN = 4112
OD_IN_PAD = 4224
N_DEV = 8
LANES = 128
ADAM_LR, ADAM_B1, ADAM_B2, ADAM_EPS, ADAM_WD, ADAM_STEP = 0.001, 0.9, 0.999, 1e-08, 0.01, 10
VMEM_LIMIT = 56 * 1024 * 1024

MESH = pl.DeviceIdType.MESH


def _params(sem=None, **kw):
    return pltpu.CompilerParams(dimension_semantics=sem, vmem_limit_bytes=VMEM_LIMIT, **kw)


def _tile(n, cands):
    for c in cands:
        if n % c == 0:
            return c
    return n


def _dot(a, b):
    return jnp.dot(a.astype(BF16), b.astype(BF16), preferred_element_type=F32)


def _dot_nt(a, b):
    return lax.dot_general(a.astype(BF16), b.astype(BF16), (((1,), (1,)), ((), ())), preferred_element_type=F32)


def _dot_tn(a, b):
    return lax.dot_general(a.astype(BF16), b.astype(BF16), (((0,), (0,)), ((), ())), preferred_element_type=F32)


def _dot_hi(a, b):
    return jnp.dot(a, b, precision=HI, preferred_element_type=F32)


_TN_CANDS = (1408, 1280, 1024, 512, 384, 256, 128)
_B_TILE_BYTES = 6 * 1024 * 1024


def _mm(a, b, *, name, nt=False, out_dtype=F32, res=None, scale=1.0):
    m, k = a.shape
    n = b.shape[0] if nt else b.shape[1]
    tm = _tile(m, (512, 256, 128))
    tn = _tile(n, [c for c in _TN_CANDS if c * k * b.dtype.itemsize <= _B_TILE_BYTES])

    def body(*refs):
        if res is None:
            a_ref, b_ref, o_ref = refs
        else:
            a_ref, b_ref, r_ref, o_ref = refs
        acc = (_dot_nt if nt else _dot)(a_ref[...], b_ref[...])
        if scale != 1.0:
            acc = scale * acc
        if res is not None:
            acc = r_ref[...] + acc
        o_ref[...] = acc.astype(out_dtype)

    b_spec = pl.BlockSpec((tn, k), lambda j, i: (j, 0)) if nt else pl.BlockSpec((k, tn), lambda j, i: (0, j))
    in_specs = [pl.BlockSpec((tm, k), lambda j, i: (i, 0)), b_spec]
    args = [a, b]
    if res is not None:
        in_specs.append(pl.BlockSpec((tm, tn), lambda j, i: (i, j)))
        args.append(res)
    return pl.pallas_call(
        body, name=name, grid=(n // tn, m // tm), in_specs=in_specs,
        out_specs=pl.BlockSpec((tm, tn), lambda j, i: (i, j)),
        out_shape=jax.ShapeDtypeStruct((m, n), out_dtype),
        compiler_params=_params(("parallel", "parallel")),
    )(*args)


def _mm_tn(a, b, *, name, scale=1.0, out_dtype=F32, into=None, slot=0):
    t, m = a.shape
    n = b.shape[1]
    tm = _tile(m, (1024, 512, 256, 128))
    tn = _tile(n, _TN_CANDS)
    tk = _tile(t, (512, 256, 128))
    nk = t // tk

    def body(*refs):
        a_ref, b_ref = refs[:2]
        o_ref, acc_ref = refs[-2:]
        kk = pl.program_id(2)

        @pl.when(kk == 0)
        def _():
            acc_ref[...] = jnp.zeros_like(acc_ref)

        acc_ref[...] += _dot_tn(a_ref[...], b_ref[...])

        @pl.when(kk == nk - 1)
        def _():
            o_ref[...] = (acc_ref[...] * scale if scale != 1.0 else acc_ref[...]).astype(o_ref.dtype)

    in_specs = [pl.BlockSpec((tk, tm), lambda i, j, kk: (kk, i)), pl.BlockSpec((tk, tn), lambda i, j, kk: (kk, j))]
    args = [a, b]
    if into is None:
        out_spec = pl.BlockSpec((tm, tn), lambda i, j, kk: (i, j))
        out_shape = jax.ShapeDtypeStruct((m, n), out_dtype)
        alias = {}
    else:
        in_specs.append(pl.BlockSpec(memory_space=pl.ANY))
        args.append(into)
        out_spec = pl.BlockSpec((None, tm, tn), lambda i, j, kk: (slot, i, j))
        out_shape = jax.ShapeDtypeStruct(into.shape, into.dtype)
        alias = {2: 0}
    return pl.pallas_call(
        body, name=name, grid=(m // tm, n // tn, nk), in_specs=in_specs, out_specs=out_spec, out_shape=out_shape,
        scratch_shapes=[pltpu.VMEM((tm, tn), F32)], input_output_aliases=alias,
        compiler_params=_params(("parallel", "parallel", "arbitrary")),
    )(*args)


def _sigmoid(x):
    return jax.nn.sigmoid(x)


def _head_norm_gate(o, gate, nw):
    r = lax.rsqrt(jnp.mean(o * o, axis=-1, keepdims=True) + NORM_EPS)
    return o * r * nw * (gate * _sigmoid(gate))


def _hg_chunk(st, ql, fl, iv, gl, lb, nw):
    c = ql.shape[0]
    q = ql * _sigmoid(ql)
    f = lb + (1.0 - lb) * _sigmoid(fl)
    lf = jnp.log(jnp.maximum(f, F_MIN))
    k = 1.0 - f
    ri = lax.broadcasted_iota(jnp.int32, (c, c), 0)
    ci = lax.broadcasted_iota(jnp.int32, (c, c), 1)
    rl = lax.broadcasted_iota(jnp.int32, (c, LANES), 0)
    b = _dot_hi(jnp.where(ci <= ri, 1.0, 0.0), lf)
    attn = jnp.where(ri == ci, _dot_nt(q, k), 0.0)
    sh = 0
    while (1 << sh) < c:
        m = 1 << sh
        ref = ((ri >> (sh + 1)) << (sh + 1)) + (m - 1)
        low_r = ((ri >> sh) & 1) == 1
        low_c = ((ci >> sh) & 1) == 1
        w_low = low_r & (ci > ref) & (ci <= ri)
        w_up = jnp.logical_not(low_r) & (ci > ri) & (ci <= ref)
        w = jnp.where(w_low | w_up, 1.0, 0.0)
        e = jnp.exp(_dot_hi(w, lf))
        low_l = ((rl >> sh) & 1) == 1
        qs = jnp.where(low_l, q * e, 0.0)
        ks = jnp.where(low_l, 0.0, k * e)
        pair = ((ri >> (sh + 1)) == (ci >> (sh + 1))) & low_r & jnp.logical_not(low_c)
        attn = attn + jnp.where(pair, _dot_nt(qs, ks), 0.0)
        sh += 1
    bl = jnp.sum(lf, axis=0, keepdims=True)
    o = _dot(attn, iv) + _dot_nt(q * jnp.exp(b), st)
    st_new = st * jnp.exp(bl) + _dot_tn(iv, k * jnp.exp(bl - b))
    return _head_norm_gate(o, gl, nw), st_new


def _hg_specs(nb, l, n_c, rev):
    def cidx(cc):
        return (n_c - 1 - cc) if rev else cc

    def col(off):
        return pl.BlockSpec((None, CHUNK, LANES), lambda bb, hh, cc: (bb, cidx(cc), hh + off))

    vec = pl.BlockSpec((None, 1, LANES), lambda bb, hh, cc: (hh, 0, 0))
    nw = pl.BlockSpec((1, LANES), lambda bb, hh, cc: (0, 0))
    st = pl.BlockSpec((None, None, None, LANES, LANES), lambda bb, hh, cc: (bb, hh, cidx(cc), 0, 0))
    acc = pl.BlockSpec((None, None, 1, LANES), lambda bb, hh, cc: (bb, hh, 0, 0))
    return col, vec, nw, st, acc


def _hgrn2_fwd(proj, lb, nw):
    nb, l, _ = proj.shape
    n_c = l // CHUNK
    col, vec, nws, st, _ = _hg_specs(nb, l, n_c, False)

    def body(ql_ref, fl_ref, iv_ref, gl_ref, lb_ref, nw_ref, y_ref, st_ref, s_scr):
        @pl.when(pl.program_id(2) == 0)
        def _():
            s_scr[...] = jnp.zeros_like(s_scr)

        s_in = s_scr[...]
        st_ref[...] = s_in
        y, s_new = _hg_chunk(s_in, ql_ref[...], fl_ref[...], iv_ref[...], gl_ref[...], lb_ref[...], nw_ref[...])
        y_ref[...] = y.astype(y_ref.dtype)
        s_scr[...] = s_new

    return pl.pallas_call(
        body, name="hgrn2_fwd", grid=(nb, HG_HEADS, n_c),
        in_specs=[col(0), col(4), col(8), col(12), vec, nws],
        out_specs=[col(0), st],
        out_shape=[jax.ShapeDtypeStruct((nb, l, HG_QK), BF16),
                   jax.ShapeDtypeStruct((nb, HG_HEADS, n_c, LANES, LANES), F32)],
        scratch_shapes=[pltpu.VMEM((LANES, LANES), F32)],
        compiler_params=_params(("parallel", "parallel", "arbitrary")),
    )(proj, proj, proj, proj, lb, nw)


def _hgrn2_bwd(proj, lb, nw, states, dy):
    nb, l, _ = proj.shape
    n_c = l // CHUNK
    col, vec, nws, st, acc = _hg_specs(nb, l, n_c, True)

    def body(ql_ref, fl_ref, iv_ref, gl_ref, lb_ref, nw_ref, st_ref, dy_ref,
             dql_ref, dfl_ref, div_ref, dgl_ref, dlb_ref, dnw_ref, ds_scr):
        @pl.when(pl.program_id(2) == 0)
        def _():
            ds_scr[...] = jnp.zeros_like(ds_scr)
            dlb_ref[...] = jnp.zeros_like(dlb_ref)
            dnw_ref[...] = jnp.zeros_like(dnw_ref)

        _, vjp = jax.vjp(_hg_chunk, st_ref[...], ql_ref[...], fl_ref[...], iv_ref[...], gl_ref[...],
                         lb_ref[...], nw_ref[...])
        ds, dql, dfl, div, dgl, dlb, dnw = vjp((dy_ref[...].astype(F32), ds_scr[...]))
        ds_scr[...] = ds
        dql_ref[...] = dql
        dfl_ref[...] = dfl
        div_ref[...] = div
        dgl_ref[...] = dgl
        dlb_ref[...] += dlb
        dnw_ref[...] += dnw

    return pl.pallas_call(
        body, name="hgrn2_bwd", grid=(nb, HG_HEADS, n_c),
        in_specs=[col(0), col(4), col(8), col(12), vec, nws, st, col(0)],
        out_specs=[col(0), col(0), col(0), col(0), acc, acc],
        out_shape=[jax.ShapeDtypeStruct((nb, l, HG_QK), F32)] * 4
        + [jax.ShapeDtypeStruct((nb, HG_HEADS, 1, LANES), F32)] * 2,
        scratch_shapes=[pltpu.VMEM((LANES, LANES), F32)],
        compiler_params=_params(("parallel", "parallel", "arbitrary")),
    )(proj, proj, proj, proj, lb, nw, states, dy)


def _shift_down(x, s):
    if s == 0:
        return x
    rows = lax.broadcasted_iota(jnp.int32, x.shape, 0)
    return jnp.where(rows >= s, pltpu.roll(x, s, 0), 0.0)


def _shift_up(x, s):
    if s == 0:
        return x
    n = x.shape[0]
    rows = lax.broadcasted_iota(jnp.int32, x.shape, 0)
    return jnp.where(rows < n - s, pltpu.roll(x, n - s, 0), 0.0)


def _gdn_pre_fwd(proj, conv_w):
    nb, l, _ = proj.shape
    n_blk = GDN_QKV // LANES

    def body(x_ref, w_ref, o_ref):
        j = pl.program_id(1)
        x = x_ref[...]
        w = w_ref[...]
        c = w[3:4] * x
        for t in range(CONV_W - 1):
            c = c + w[t:t + 1] * _shift_down(x, CONV_W - 1 - t)
        s = c * _sigmoid(c)
        r = lax.rsqrt(jnp.sum(s * s, axis=-1, keepdims=True) + NORM_EPS)
        scale = jnp.where(j < GDN_HEADS, HEAD_DIM ** -0.5, 1.0)
        o_ref[...] = jnp.where(j < 2 * GDN_HEADS, s * r * scale, s)

    return pl.pallas_call(
        body, name="gdn_pre_fwd", grid=(nb, n_blk),
        in_specs=[pl.BlockSpec((None, l, LANES), lambda b, j: (b, 0, j)), pl.BlockSpec((CONV_W, LANES), lambda b, j: (0, j))],
        out_specs=pl.BlockSpec((None, l, LANES), lambda b, j: (b, 0, j)),
        out_shape=jax.ShapeDtypeStruct((nb, l, GDN_QKV), F32),
        compiler_params=_params(("parallel", "parallel")),
    )(proj, conv_w)


def _gdn_pre_bwd(proj, conv_w, dout):
    nb, l, _ = proj.shape
    n_blk = GDN_QKV // LANES

    def body(x_ref, w_ref, d_ref, dx_ref, dw_ref):
        j = pl.program_id(1)
        x = x_ref[...]
        w = w_ref[...]
        xs = [_shift_down(x, CONV_W - 1 - t) for t in range(CONV_W)]
        c = w[0:1] * xs[0]
        for t in range(1, CONV_W):
            c = c + w[t:t + 1] * xs[t]
        sg = _sigmoid(c)
        s = c * sg
        d = d_ref[...]
        r = lax.rsqrt(jnp.sum(s * s, axis=-1, keepdims=True) + NORM_EPS)
        scale = jnp.where(j < GDN_HEADS, HEAD_DIM ** -0.5, 1.0)
        ds_norm = scale * r * (d - s * (r * r) * jnp.sum(d * s, axis=-1, keepdims=True))
        ds = jnp.where(j < 2 * GDN_HEADS, ds_norm, d)
        dc = ds * (sg * (1.0 + c * (1.0 - sg)))
        dx = w[3:4] * dc
        for t in range(CONV_W - 1):
            dx = dx + w[t:t + 1] * _shift_up(dc, CONV_W - 1 - t)
        dx_ref[...] = dx
        for t in range(CONV_W):
            dw_ref[t:t + 1, :] = jnp.sum(dc * xs[t], axis=0, keepdims=True)

    return pl.pallas_call(
        body, name="gdn_pre_bwd", grid=(nb, n_blk),
        in_specs=[pl.BlockSpec((None, l, LANES), lambda b, j: (b, 0, j)), pl.BlockSpec((CONV_W, LANES), lambda b, j: (0, j)),
                  pl.BlockSpec((None, l, LANES), lambda b, j: (b, 0, j))],
        out_specs=[pl.BlockSpec((None, l, LANES), lambda b, j: (b, 0, j)), pl.BlockSpec((None, CONV_W, LANES), lambda b, j: (b, 0, j))],
        out_shape=[jax.ShapeDtypeStruct((nb, l, GDN_QKV), F32), jax.ShapeDtypeStruct((nb, CONV_W, GDN_QKV), F32)],
        compiler_params=_params(("parallel", "parallel")),
    )(proj, conv_w, dout)


def _gdn_chunk(st, q, k, v, gate, bl, al, alog, dtb, nw):
    c = q.shape[0]
    beta = _sigmoid(bl)
    x = al + dtb
    softplus = jnp.maximum(x, 0.0) + jnp.log(1.0 + jnp.exp(-jnp.abs(x)))
    la = -jnp.exp(alog) * softplus
    ri = lax.broadcasted_iota(jnp.int32, (c, c), 0)
    ci = lax.broadcasted_iota(jnp.int32, (c, c), 1)
    lower = ci <= ri
    strict = ci < ri
    ltri = jnp.where(lower, 1.0, 0.0)
    la_l = la + jnp.zeros((1, LANES), F32)
    g = _dot_hi(ltri, la_l)
    delta = _dot_hi(ltri, jnp.where(strict, la + jnp.zeros((1, c), F32), 0.0))
    gam = jnp.where(lower, jnp.exp(jnp.where(lower, delta, 0.0)), 0.0)
    kb = k * beta
    vb = v * beta
    m = jnp.where(strict, _dot_nt(kb, k) * gam, 0.0)
    pw = -m
    t_inv = jnp.where(ri == ci, 1.0, 0.0) + pw
    steps = int(math.log2(c)) - 1
    for _ in range(steps):
        pw = _dot_hi(pw, pw)
        t_inv = t_inv + _dot_hi(t_inv, pw)
    eg = jnp.exp(g)
    u = _dot(t_inv, vb)
    w = _dot(t_inv, kb * eg)
    attn = jnp.where(lower, _dot_nt(q, k) * gam, 0.0)
    v_new = u - _dot_nt(w, st)
    o = _dot_nt(q * eg, st) + _dot(attn, v_new)
    g_last = jnp.sum(la_l, axis=0, keepdims=True)
    st_new = st * jnp.exp(g_last) + _dot_tn(v_new, k * jnp.exp(g_last - g))
    return _head_norm_gate(o, gate, nw), st_new


def _gdn_specs(n_c, rev):
    def cidx(cc):
        return (n_c - 1 - cc) if rev else cc

    def col(off):
        return pl.BlockSpec((None, CHUNK, LANES), lambda bb, hh, cc: (bb, cidx(cc), hh + off))

    tok = pl.BlockSpec((None, None, CHUNK, 1), lambda bb, hh, cc: (bb, hh, cidx(cc), 0))
    scal = pl.BlockSpec((None, 1, 1), lambda bb, hh, cc: (hh, 0, 0))
    nw = pl.BlockSpec((1, LANES), lambda bb, hh, cc: (0, 0))
    st = pl.BlockSpec((None, None, None, LANES, LANES), lambda bb, hh, cc: (bb, hh, cidx(cc), 0, 0))
    acc_v = pl.BlockSpec((None, None, 1, LANES), lambda bb, hh, cc: (bb, hh, 0, 0))
    acc_s = pl.BlockSpec((None, None, 1, 1), lambda bb, hh, cc: (bb, hh, 0, 0))
    return col, tok, scal, nw, st, acc_v, acc_s


def _gdn_fwd(qkv, proj, bl, al, alog, dtb, nw):
    nb, l, _ = qkv.shape
    n_c = l // CHUNK
    col, tok, scal, nws, st, _, _ = _gdn_specs(n_c, False)

    def body(q_ref, k_ref, v_ref, g_ref, bl_ref, al_ref, alog_ref, dtb_ref, nw_ref, y_ref, st_ref, s_scr):
        @pl.when(pl.program_id(2) == 0)
        def _():
            s_scr[...] = jnp.zeros_like(s_scr)

        s_in = s_scr[...]
        st_ref[...] = s_in
        y, s_new = _gdn_chunk(s_in, q_ref[...], k_ref[...], v_ref[...], g_ref[...], bl_ref[...], al_ref[...],
                              alog_ref[...], dtb_ref[...], nw_ref[...])
        y_ref[...] = y.astype(y_ref.dtype)
        s_scr[...] = s_new

    h = GDN_HEADS
    return pl.pallas_call(
        body, name="gdn_fwd", grid=(nb, h, n_c),
        in_specs=[col(0), col(h), col(2 * h), col(3 * h), tok, tok, scal, scal, nws],
        out_specs=[col(0), st],
        out_shape=[jax.ShapeDtypeStruct((nb, l, h * LANES), BF16),
                   jax.ShapeDtypeStruct((nb, h, n_c, LANES, LANES), F32)],
        scratch_shapes=[pltpu.VMEM((LANES, LANES), F32)],
        compiler_params=_params(("parallel", "parallel", "arbitrary")),
    )(qkv, qkv, qkv, proj, bl, al, alog, dtb, nw)


def _gdn_bwd(qkv, proj, bl, al, alog, dtb, nw, states, dy):
    nb, l, _ = qkv.shape
    n_c = l // CHUNK
    col, tok, scal, nws, st, acc_v, acc_s = _gdn_specs(n_c, True)

    def body(q_ref, k_ref, v_ref, g_ref, bl_ref, al_ref, alog_ref, dtb_ref, nw_ref, st_ref, dy_ref,
             dq_ref, dk_ref, dv_ref, dg_ref, dbl_ref, dal_ref, dalog_ref, ddtb_ref, dnw_ref, ds_scr):
        @pl.when(pl.program_id(2) == 0)
        def _():
            ds_scr[...] = jnp.zeros_like(ds_scr)
            dalog_ref[...] = jnp.zeros_like(dalog_ref)
            ddtb_ref[...] = jnp.zeros_like(ddtb_ref)
            dnw_ref[...] = jnp.zeros_like(dnw_ref)

        _, vjp = jax.vjp(_gdn_chunk, st_ref[...], q_ref[...], k_ref[...], v_ref[...], g_ref[...], bl_ref[...],
                         al_ref[...], alog_ref[...], dtb_ref[...], nw_ref[...])
        ds, dq, dk, dv, dg, dbl, dal, dalog, ddtb, dnw = vjp((dy_ref[...].astype(F32), ds_scr[...]))
        ds_scr[...] = ds
        dq_ref[...] = dq
        dk_ref[...] = dk
        dv_ref[...] = dv
        dg_ref[...] = dg
        dbl_ref[...] = dbl
        dal_ref[...] = dal
        dalog_ref[...] += dalog
        ddtb_ref[...] += ddtb
        dnw_ref[...] += dnw

    h = GDN_HEADS
    wide = jax.ShapeDtypeStruct((nb, l, h * LANES), F32)
    return pl.pallas_call(
        body, name="gdn_bwd", grid=(nb, h, n_c),
        in_specs=[col(0), col(h), col(2 * h), col(3 * h), tok, tok, scal, scal, nws, st, col(0)],
        out_specs=[col(0), col(0), col(0), col(0), tok, tok, acc_s, acc_s, acc_v],
        out_shape=[wide, wide, wide, wide,
                   jax.ShapeDtypeStruct((nb, h, l, 1), F32), jax.ShapeDtypeStruct((nb, h, l, 1), F32),
                   jax.ShapeDtypeStruct((nb, h, 1, 1), F32), jax.ShapeDtypeStruct((nb, h, 1, 1), F32),
                   jax.ShapeDtypeStruct((nb, h, 1, LANES), F32)],
        scratch_shapes=[pltpu.VMEM((LANES, LANES), F32)],
        compiler_params=_params(("parallel", "parallel", "arbitrary")),
    )(qkv, qkv, qkv, proj, bl, al, alog, dtb, nw, states, dy)


S5_ROWS = S5_NSTATE // LANES
S5_TB = 256


def _s5_scan_fwd(bu, abar):
    nb, l = bu.shape[:2]
    tb = min(S5_TB, l)

    def body(bu_ref, a_ref, h_ref, c_scr):
        @pl.when(pl.program_id(1) == 0)
        def _():
            c_scr[...] = jnp.zeros_like(c_scr)

        ar = a_ref[0]
        ai = a_ref[1]

        def step(t, carry):
            hr, hi = carry
            nr = ar * hr - ai * hi + bu_ref[t, 0]
            ni = ar * hi + ai * hr + bu_ref[t, 1]
            h_ref[t, 0] = nr
            h_ref[t, 1] = ni
            return nr, ni

        hr, hi = lax.fori_loop(0, tb, step, (c_scr[0], c_scr[1]), unroll=8)
        c_scr[0] = hr
        c_scr[1] = hi

    blk = pl.BlockSpec((None, tb, 2, S5_ROWS, LANES), lambda b, i: (b, i, 0, 0, 0))
    return pl.pallas_call(
        body, name="s5_scan_fwd", grid=(nb, l // tb),
        in_specs=[blk, pl.BlockSpec((2, S5_ROWS, LANES), lambda b, i: (0, 0, 0))],
        out_specs=blk, out_shape=jax.ShapeDtypeStruct(bu.shape, F32),
        scratch_shapes=[pltpu.VMEM((2, S5_ROWS, LANES), F32)],
        compiler_params=_params(("parallel", "arbitrary")),
    )(bu, abar)


def _s5_scan_bwd(dh, h, abar):
    nb, l = dh.shape[:2]
    tb = min(S5_TB, l)
    n_blk = l // tb

    def body(dh_ref, h_ref, a_ref, g_ref, da_ref, c_scr):
        @pl.when(pl.program_id(1) == 0)
        def _():
            c_scr[...] = jnp.zeros_like(c_scr)
            da_ref[...] = jnp.zeros_like(da_ref)

        ar = a_ref[0]
        ai = a_ref[1]

        def step(s, carry):
            gr, gi, dar, dai = carry
            t = tb - 1 - s
            hr = h_ref[t, 0]
            hi = h_ref[t, 1]
            dar = dar + gr * hr + gi * hi
            dai = dai + gi * hr - gr * hi
            nr = ar * gr + ai * gi + dh_ref[t, 0]
            ni = ar * gi - ai * gr + dh_ref[t, 1]
            g_ref[t, 0] = nr
            g_ref[t, 1] = ni
            return nr, ni, dar, dai

        z = jnp.zeros((S5_ROWS, LANES), F32)
        gr, gi, dar, dai = lax.fori_loop(0, tb, step, (c_scr[0], c_scr[1], z, z), unroll=8)
        c_scr[0] = gr
        c_scr[1] = gi
        da_ref[0] += dar
        da_ref[1] += dai

    blk = pl.BlockSpec((None, tb, 2, S5_ROWS, LANES), lambda b, i: (b, n_blk - 1 - i, 0, 0, 0))
    return pl.pallas_call(
        body, name="s5_scan_bwd", grid=(nb, n_blk),
        in_specs=[blk, blk, pl.BlockSpec((2, S5_ROWS, LANES), lambda b, i: (0, 0, 0))],
        out_specs=[blk, pl.BlockSpec((None, 2, S5_ROWS, LANES), lambda b, i: (b, 0, 0, 0))],
        out_shape=[jax.ShapeDtypeStruct(dh.shape, F32), jax.ShapeDtypeStruct((nb, 2, S5_ROWS, LANES), F32)],
        scratch_shapes=[pltpu.VMEM((2, S5_ROWS, LANES), F32)],
        compiler_params=_params(("parallel", "arbitrary")),
    )(dh, h, abar)


_GELU_C = math.sqrt(2.0 / math.pi)


def _gelu(x):
    return 0.5 * x * (1.0 + jnp.tanh(_GELU_C * (x + 0.044715 * x * x * x)))


def _gelu_grad(x):
    t = jnp.tanh(_GELU_C * (x + 0.044715 * x * x * x))
    return 0.5 * (1.0 + t) + 0.5 * x * (1.0 - t * t) * _GELU_C * (1.0 + 3.0 * 0.044715 * x * x)


def _s5_post_fwd(yc, u, d, w_glu):
    t, w = yc.shape
    tt = _tile(t, (512, 256, 128))

    def body(yc_ref, u_ref, d_ref, w_ref, o_ref, y0_ref):
        y0 = yc_ref[...] + d_ref[...] * u_ref[...]
        y = _gelu(y0)
        z = _dot(y, w_ref[...])
        o_ref[...] = (y * _sigmoid(z)).astype(o_ref.dtype)
        y0_ref[...] = y0

    tok = pl.BlockSpec((tt, w), lambda i: (i, 0))
    return pl.pallas_call(
        body, name="s5_post_fwd", grid=(t // tt,),
        in_specs=[tok, tok, pl.BlockSpec((1, w), lambda i: (0, 0)), pl.BlockSpec((w, w), lambda i: (0, 0))],
        out_specs=[tok, tok],
        out_shape=[jax.ShapeDtypeStruct((t, w), BF16), jax.ShapeDtypeStruct((t, w), F32)],
        compiler_params=_params(("parallel",)),
    )(yc, u, d, w_glu)


def _s5_post_bwd(y0, u, d, w_glu, dout):
    t, w = y0.shape
    tt = _tile(t, (512, 256, 128))

    def body(y0_ref, u_ref, d_ref, w_ref, do_ref, dy0_ref, y_ref, dz_ref, du_ref, dd_ref):
        @pl.when(pl.program_id(0) == 0)
        def _():
            dd_ref[...] = jnp.zeros_like(dd_ref)

        y0 = y0_ref[...]
        y = _gelu(y0)
        s = _sigmoid(_dot(y, w_ref[...]))
        do = do_ref[...]
        dz = do * y * s * (1.0 - s)
        dy = do * s + _dot_nt(dz, w_ref[...])
        dy0 = dy * _gelu_grad(y0)
        dy0_ref[...] = dy0.astype(dy0_ref.dtype)
        y_ref[...] = y.astype(y_ref.dtype)
        dz_ref[...] = dz.astype(dz_ref.dtype)
        du_ref[...] = dy0 * d_ref[...]
        dd_ref[...] += jnp.sum(dy0 * u_ref[...], axis=0, keepdims=True)

    tok = pl.BlockSpec((tt, w), lambda i: (i, 0))
    vec = pl.BlockSpec((1, w), lambda i: (0, 0))
    return pl.pallas_call(
        body, name="s5_post_bwd", grid=(t // tt,),
        in_specs=[tok, tok, vec, pl.BlockSpec((w, w), lambda i: (0, 0)), tok],
        out_specs=[tok, tok, tok, tok, vec],
        out_shape=[jax.ShapeDtypeStruct((t, w), BF16), jax.ShapeDtypeStruct((t, w), BF16),
                   jax.ShapeDtypeStruct((t, w), BF16), jax.ShapeDtypeStruct((t, w), F32),
                   jax.ShapeDtypeStruct((1, w), F32)],
        compiler_params=_params(("arbitrary",)),
    )(y0, u, d, w_glu, dout)


def _s5_params(a_re, a_im, b_re, b_im, c_re, c_im, log_dt):
    dt = jnp.exp(log_dt)[:, None]
    mag = jnp.exp(dt * a_re)
    ang = dt * a_im
    abar_re = mag * jnp.cos(ang)
    abar_im = mag * jnp.sin(ang)
    den = a_re * a_re + a_im * a_im
    zr = abar_re - 1.0
    zi = abar_im
    coef_re = ((zr * a_re + zi * a_im) / den)[..., None]
    coef_im = ((zi * a_re - zr * a_im) / den)[..., None]
    bb_re = coef_re * b_re - coef_im * b_im
    bb_im = coef_re * b_im + coef_im * b_re
    eye = jnp.eye(S5_GROUPS, dtype=F32)

    def dense_in(bb):
        return jnp.einsum('gnp,gh->gphn', bb, eye).reshape(S5_WIDTH, S5_NSTATE)

    def dense_out(cc):
        return jnp.einsum('gpn,gh->gnhp', cc, eye).reshape(S5_NSTATE, S5_WIDTH)

    abar = jnp.stack([abar_re.reshape(S5_ROWS, LANES), abar_im.reshape(S5_ROWS, LANES)])
    bbd = jnp.concatenate([dense_in(bb_re), dense_in(bb_im)], axis=1)
    ccd = jnp.concatenate([dense_out(c_re), dense_out(-c_im)], axis=0)
    return abar, bbd, ccd


def _tok_tile(t):
    return _tile(t, (256, 128))


def _rms_fwd(x, w):
    t, d = x.shape
    tt = _tok_tile(t)

    def body(x_ref, w_ref, o_ref):
        xv = x_ref[...]
        r = lax.rsqrt(jnp.mean(xv * xv, axis=-1, keepdims=True) + NORM_EPS)
        o_ref[...] = (xv * r * w_ref[...]).astype(o_ref.dtype)

    tok = pl.BlockSpec((tt, d), lambda i: (i, 0))
    return pl.pallas_call(
        body, name="rms_fwd", grid=(t // tt,), in_specs=[tok, pl.BlockSpec((1, d), lambda i: (0, 0))],
        out_specs=tok, out_shape=jax.ShapeDtypeStruct((t, d), BF16), compiler_params=_params(("parallel",)),
    )(x, w)


def _rms_bwd_math(xv, wv, dy):
    r = lax.rsqrt(jnp.mean(xv * xv, axis=-1, keepdims=True) + NORM_EPS)
    xh = xv * r
    dxh = dy * wv
    dx = r * (dxh - xh * jnp.mean(dxh * xh, axis=-1, keepdims=True))
    return dx, jnp.sum(dy * xh, axis=0, keepdims=True)


def _rms_bwd(x, w, dy, dres):
    t, d = x.shape
    tt = _tok_tile(t)

    def body(x_ref, w_ref, dy_ref, dr_ref, dx_ref, dw_ref):
        @pl.when(pl.program_id(0) == 0)
        def _():
            dw_ref[...] = jnp.zeros_like(dw_ref)

        dx, dw = _rms_bwd_math(x_ref[...], w_ref[...], dy_ref[...])
        dx_ref[...] = dr_ref[...] + dx
        dw_ref[...] += dw

    tok = pl.BlockSpec((tt, d), lambda i: (i, 0))
    vec = pl.BlockSpec((1, d), lambda i: (0, 0))
    return pl.pallas_call(
        body, name="rms_bwd", grid=(t // tt,), in_specs=[tok, vec, tok, tok], out_specs=[tok, vec],
        out_shape=[jax.ShapeDtypeStruct((t, d), F32), jax.ShapeDtypeStruct((1, d), F32)],
        compiler_params=_params(("arbitrary",)),
    )(x, w, dy, dres)


def _loss_head(x, w, target):
    t, d = x.shape
    tt = _tok_tile(t)

    def body(x_ref, w_ref, t_ref, l_ref, dx_ref, dw_ref):
        @pl.when(pl.program_id(0) == 0)
        def _():
            l_ref[...] = jnp.zeros_like(l_ref)
            dw_ref[...] = jnp.zeros_like(dw_ref)

        xv = x_ref[...]
        wv = w_ref[...]
        r = lax.rsqrt(jnp.mean(xv * xv, axis=-1, keepdims=True) + NORM_EPS)
        err = xv * r * wv - t_ref[...]
        row = jnp.sum(err * err, axis=-1, keepdims=True)
        l_ref[...] += (0.5 / d) * jnp.sum(row, axis=0, keepdims=True)
        dx, dw = _rms_bwd_math(xv, wv, err * (1.0 / d))
        dx_ref[...] = dx
        dw_ref[...] += dw

    tok = pl.BlockSpec((tt, d), lambda i: (i, 0))
    vec = pl.BlockSpec((1, d), lambda i: (0, 0))
    return pl.pallas_call(
        body, name="loss_head", grid=(t // tt,), in_specs=[tok, vec, tok],
        out_specs=[pl.BlockSpec((1, 1), lambda i: (0, 0)), tok, vec],
        out_shape=[jax.ShapeDtypeStruct((1, 1), F32), jax.ShapeDtypeStruct((t, d), F32), jax.ShapeDtypeStruct((1, d), F32)],
        compiler_params=_params(("arbitrary",)),
    )(x, w, target)


FF_PAD = 3072
FF_SHARD = D_FF // N_DEV
FF_SHARD_PAD = FF_PAD // N_DEV


def _ffn_in(x, nw, wg_t, wu_t):
    t, d = x.shape
    tm = _tile(t, (512, 256, 128))
    tn = 1024

    def body(x_ref, nw_ref, wg_ref, wu_ref, h_ref, g_ref, u_ref, a_ref):
        @pl.when(pl.program_id(1) == 0)
        def _():
            xv = x_ref[...]
            r = lax.rsqrt(jnp.mean(xv * xv, axis=-1, keepdims=True) + NORM_EPS)
            h_ref[...] = (xv * r * nw_ref[...]).astype(h_ref.dtype)

        h = h_ref[...]
        g = _dot_nt(h, wg_ref[...])
        u = _dot_nt(h, wu_ref[...])
        g_ref[...] = g.astype(g_ref.dtype)
        u_ref[...] = u.astype(u_ref.dtype)
        a_ref[...] = (g * _sigmoid(g) * u).astype(a_ref.dtype)

    tok = pl.BlockSpec((tm, d), lambda i, j: (i, 0))
    wsp = pl.BlockSpec((tn, d), lambda i, j: (j, 0))
    wide = pl.BlockSpec((tm, tn), lambda i, j: (i, j))
    return pl.pallas_call(
        body, name="ffn_in", grid=(t // tm, FF_PAD // tn),
        in_specs=[tok, pl.BlockSpec((1, d), lambda i, j: (0, 0)), wsp, wsp], out_specs=[tok, wide, wide, wide],
        out_shape=[jax.ShapeDtypeStruct((t, d), BF16)] + [jax.ShapeDtypeStruct((t, FF_PAD), BF16)] * 3,
        compiler_params=_params(("parallel", "arbitrary")),
    )(x, nw, wg_t, wu_t)


def _ffn_out_dx(dxo, wd, g, u):
    t, d = dxo.shape
    tm = _tile(t, (512, 256, 128))
    tn = 1024

    def body(dx_ref, wd_ref, g_ref, u_ref, dg_ref, du_ref):
        da = 0.5 * _dot_nt(dx_ref[...], wd_ref[...])
        g = g_ref[...].astype(F32)
        s = _sigmoid(g)
        dg_ref[...] = (da * u_ref[...].astype(F32) * (s * (1.0 + g * (1.0 - s)))).astype(dg_ref.dtype)
        du_ref[...] = (da * g * s).astype(du_ref.dtype)

    wide = pl.BlockSpec((tm, tn), lambda j, i: (i, j))
    return pl.pallas_call(
        body, name="ffn_out_dx", grid=(FF_PAD // tn, t // tm),
        in_specs=[pl.BlockSpec((tm, d), lambda j, i: (i, 0)), pl.BlockSpec((tn, d), lambda j, i: (j, 0)), wide, wide],
        out_specs=[wide, wide], out_shape=[jax.ShapeDtypeStruct((t, FF_PAD), BF16)] * 2,
        compiler_params=_params(("parallel", "parallel")),
    )(dxo, wd, g, u)


def _ffn_in_dx(dg, du, wg_t, wu_t, x, nw, dxo):
    t, d = x.shape
    tm = _tile(t, (256, 128))

    def body(dg_ref, du_ref, wg_ref, wu_ref, x_ref, nw_ref, dr_ref, dx_ref, dw_ref):
        @pl.when(pl.program_id(0) == 0)
        def _():
            dw_ref[...] = jnp.zeros_like(dw_ref)

        dh = _dot(dg_ref[...], wg_ref[...]) + _dot(du_ref[...], wu_ref[...])
        dx, dw = _rms_bwd_math(x_ref[...], nw_ref[...], dh)
        dx_ref[...] = dr_ref[...] + dx
        dw_ref[...] += dw

    wide = pl.BlockSpec((tm, FF_PAD), lambda i: (i, 0))
    wsp = pl.BlockSpec((FF_PAD, d), lambda i: (0, 0))
    tok = pl.BlockSpec((tm, d), lambda i: (i, 0))
    vec = pl.BlockSpec((1, d), lambda i: (0, 0))
    return pl.pallas_call(
        body, name="ffn_in_dx", grid=(t // tm,), in_specs=[wide, wide, wsp, wsp, tok, vec, tok], out_specs=[tok, vec],
        out_shape=[jax.ShapeDtypeStruct((t, d), F32), jax.ShapeDtypeStruct((1, d), F32)],
        compiler_params=_params(("arbitrary",)),
    )(dg, du, wg_t, wu_t, x, nw, dxo)


def _adamw(w, g, m, v):
    r, c = w.shape
    tr = _tile(r, (512, 256, 128, 64, 32, 16, 8))
    c1 = 1.0 - ADAM_B1 ** ADAM_STEP
    c2 = 1.0 - ADAM_B2 ** ADAM_STEP

    def body(w_ref, g_ref, m_ref, v_ref, d_ref, nm_ref, nv_ref):
        gv = g_ref[...]
        nm = ADAM_B1 * m_ref[...] + (1.0 - ADAM_B1) * gv
        nv = ADAM_B2 * v_ref[...] + (1.0 - ADAM_B2) * (gv * gv)
        d_ref[...] = -ADAM_LR * ((nm / c1) / (jnp.sqrt(nv / c2) + ADAM_EPS) + ADAM_WD * w_ref[...])
        nm_ref[...] = nm
        nv_ref[...] = nv

    blk = pl.BlockSpec((tr, c), lambda i: (i, 0))
    return pl.pallas_call(
        body, name="adamw", grid=(r // tr,), in_specs=[blk] * 4, out_specs=[blk] * 3,
        out_shape=[jax.ShapeDtypeStruct((r, c), F32)] * 3, compiler_params=_params(("parallel",)),
    )(w, g, m, v)


def _ffn_slot(which, kind, layer):
    return (which * 3 + kind) * DEPTH + layer


def _ffn_fwd(x, nw, w_ffn, which, layer):
    wg, wu, wd = (w_ffn[_ffn_slot(which, kind, layer)] for kind in range(3))
    h, g, u, a = _ffn_in(x, nw, wg, wu)
    return _mm(a, wd, name="ffn_out", res=x, scale=0.5), (x, h, g, u, a)


def _ffn_bwd(dxo, res, nw, w_ffn, g_ffn, which, layer):
    x, h, g, u, a = res
    wg, wu, wd = (w_ffn[_ffn_slot(which, kind, layer)] for kind in range(3))
    dg, du = _ffn_out_dx(dxo, wd, g, u)
    g_ffn = _mm_tn(a, dxo, name="ffn_out_dw", scale=0.5, into=g_ffn, slot=_ffn_slot(which, 2, layer))
    g_ffn = _mm_tn(dg, h, name="ffn_in_dw", into=g_ffn, slot=_ffn_slot(which, 0, layer))
    g_ffn = _mm_tn(du, h, name="ffn_in_dw", into=g_ffn, slot=_ffn_slot(which, 1, layer))
    dx, dnw = _ffn_in_dx(dg, du, wg, wu, x, nw, dxo)
    return dx, dnw, g_ffn


def _hg_lower_bounds(logits):
    p = jax.nn.softmax(logits, axis=0)
    return jnp.cumsum(p, axis=0) - p[0]


def _even_fwd(x1, p, nb, l):
    t = nb * l
    h = _rms_fwd(x1, p["mix_norm"])
    proj = _mm(h, p["w_in_t"], name="ev_in", nt=True)
    proj3 = proj.reshape(nb, l, EV_IN)
    ya, hst = _hgrn2_fwd(proj3, p["lb"], p["hg_nw"])
    u = proj[:, 4 * HG_QK:]
    bu = _mm(u, p["bbd"], name="s5_in")
    hs = _s5_scan_fwd(bu.reshape(nb, l, 2, S5_ROWS, LANES), p["abar"])
    yc = _mm(hs.reshape(t, 2 * S5_NSTATE), p["ccd"], name="s5_out")
    yb, y0 = _s5_post_fwd(yc, u, p["s5_d"], p["w_glu"])
    ycat = jnp.concatenate([ya.reshape(t, HG_QK), yb], axis=1)
    x2 = _mm(ycat, p["w_out"], name="mix_out", res=x1)
    return x2, (x1, h, proj3, hst, u, hs, y0, ycat)


def _even_bwd(dx2, res, p, nb, l):
    x1, h, proj3, hst, u, hs, y0, ycat = res
    t = nb * l
    dycat = _mm(dx2, p["w_out"], name="mix_out_dx", nt=True)
    dw_out = _mm_tn(ycat, dx2, name="mix_out_dw", out_dtype=BF16)
    dq, df, di, dg, dlb, dhg_nw = _hgrn2_bwd(proj3, p["lb"], p["hg_nw"], hst, dycat.reshape(nb, l, D_MODEL))
    dy0, y, dz, du_d, dd = _s5_post_bwd(y0, u, p["s5_d"], p["w_glu"], dycat[:, HG_QK:])
    dw_glu = _mm_tn(y, dz, name="s5_glu_dw", out_dtype=BF16)
    dhs = _mm(dy0, p["ccd"], name="s5_out_dx", nt=True)
    dccd = _mm_tn(hs.reshape(t, 2 * S5_NSTATE), dy0, name="s5_out_dw")
    g5, dabar = _s5_scan_bwd(dhs.reshape(nb, l, 2, S5_ROWS, LANES), hs, p["abar"])
    g2 = g5.reshape(t, 2 * S5_NSTATE)
    du = _mm(g2, p["bbd"], name="s5_in_dx", nt=True, res=du_d)
    dbbd = _mm_tn(u, g2, name="s5_in_dw")
    dproj = jnp.concatenate([d.reshape(t, HG_QK).astype(BF16) for d in (dq, df, di, dg)] + [du.astype(BF16)], axis=1)
    dw_in_t = _mm_tn(dproj, h, name="ev_in_dw", out_dtype=BF16)
    dh = _mm(dproj, p["w_in_t"], name="ev_in_dx")
    dx1, dmix = _rms_bwd(x1, p["mix_norm"], dh, dx2)
    small = dict(mix_norm=dmix, lb=dlb.sum(0).reshape(HG_QK), hg_nw=dhg_nw.sum((0, 1)).reshape(HEAD_DIM),
                 abar=dabar.sum(0), bbd=dbbd, ccd=dccd, s5_d=dd.reshape(S5_WIDTH))
    return dx1, dict(w_in_t=dw_in_t, w_glu=dw_glu, w_out=dw_out), small


def _tok_major(xs):
    return jnp.transpose(xs, (0, 2, 1))[..., None]


def _odd_fwd(x1, p, nb, l):
    t = nb * l
    h = _rms_fwd(x1, p["mix_norm"])
    proj3 = _mm(h, p["w_in"], name="od_in").reshape(nb, l, OD_IN_PAD)
    qkv = _gdn_pre_fwd(proj3, p["conv_w"])
    n_wide = GDN_QKV + GDN_HEADS * HEAD_DIM
    bl = _tok_major(proj3[..., n_wide:n_wide + GDN_HEADS])
    al = _tok_major(proj3[..., n_wide + GDN_HEADS:n_wide + 2 * GDN_HEADS])
    y, st = _gdn_fwd(qkv, proj3, bl, al, p["a_log"], p["dt_bias"], p["gdn_nw"])
    y2 = y.reshape(t, D_MODEL)
    x2 = _mm(y2, p["w_out"], name="mix_out", res=x1)
    return x2, (x1, h, proj3, qkv, bl, al, st, y2)


def _odd_bwd(dx2, res, p, nb, l):
    x1, h, proj3, qkv, bl, al, st, y2 = res
    t = nb * l
    dy = _mm(dx2, p["w_out"], name="mix_out_dx", nt=True)
    dw_out = _mm_tn(y2, dx2, name="mix_out_dw", out_dtype=BF16)
    dq, dk, dv, dgate, dbl, dal, dalog, ddtb, dnw = _gdn_bwd(
        qkv, proj3, bl, al, p["a_log"], p["dt_bias"], p["gdn_nw"], st, dy.reshape(nb, l, D_MODEL))
    dx_qkv, dcw = _gdn_pre_bwd(proj3, p["conv_w"], jnp.concatenate([dq, dk, dv], axis=-1))
    dsmall = jnp.concatenate([jnp.transpose(d[..., 0], (0, 2, 1)) for d in (dbl, dal)], axis=-1)
    pad = jnp.zeros((nb, l, OD_IN_PAD - OD_IN), BF16)
    dproj = jnp.concatenate([dx_qkv.astype(BF16), dgate.astype(BF16), dsmall.astype(BF16), pad], axis=-1)
    dproj = dproj.reshape(t, OD_IN_PAD)
    dw_in = _mm_tn(h, dproj, name="od_in_dw", out_dtype=BF16)[:, :OD_IN]
    dh = _mm(dproj, p["w_in"], name="od_in_dx", nt=True)
    dx1, dmix = _rms_bwd(x1, p["mix_norm"], dh, dx2)
    small = dict(mix_norm=dmix, a_log=dalog.sum(0).reshape(GDN_HEADS), dt_bias=ddtb.sum(0).reshape(GDN_HEADS),
                 gdn_nw=dnw.sum((0, 1)).reshape(HEAD_DIM))
    return dx1, dict(w_in=dw_in, conv_w=dcw.sum(0), w_out=dw_out), small


def _local_step(x, target, big, small):
    nb, l, _ = x.shape
    t = nb * l
    lbs, lbs_vjp = jax.vjp(_hg_lower_bounds, small["hg_lb_logits"])
    row = lambda v: v.reshape(1, -1)
    layers = []
    s5_vjps = []
    for layer in range(DEPTH):
        j = layer // 2
        p = dict(mix_norm=row(small["mix_norm"][layer]))
        if layer % 2 == 0:
            (abar, bbd, ccd), s5_vjp = jax.vjp(
                _s5_params, small["s5_a_re"][j], small["s5_a_im"][j], small["s5_b_re"][j], small["s5_b_im"][j],
                small["s5_c_re"][j], small["s5_c_im"][j], small["s5_log_dt"][j])
            s5_vjps.append(s5_vjp)
            p.update(w_in_t=big["ev_w_in_t"][j], w_glu=big["s5_w_glu"][j], w_out=big["w_out"][j],
                     lb=lbs[j].reshape(HG_HEADS, 1, HEAD_DIM), hg_nw=row(small["hg_norm_w"][j]),
                     abar=abar, bbd=bbd.astype(BF16), ccd=ccd.astype(BF16), s5_d=row(small["s5_d"][j]))
        else:
            p.update(w_in=big["od_w_in"][j], conv_w=big["gdn_conv_w"][j], w_out=big["w_out"][2 + j],
                     a_log=small["gdn_a_log"][j].reshape(GDN_HEADS, 1, 1),
                     dt_bias=small["gdn_dt_bias"][j].reshape(GDN_HEADS, 1, 1), gdn_nw=row(small["gdn_norm_w"][j]))
        layers.append(p)

    xs = x.reshape(t, D_MODEL)
    saved = []
    for layer in range(DEPTH):
        xs, r1 = _ffn_fwd(xs, row(small["ffn1_norm"][layer]), big["ffn"], 0, layer)
        xs, r2 = (_even_fwd if layer % 2 == 0 else _odd_fwd)(xs, layers[layer], nb, l)
        xs, r3 = _ffn_fwd(xs, row(small["ffn2_norm"][layer]), big["ffn"], 1, layer)
        saved.append((r1, r2, r3))
    loss, dx, dfinal = _loss_head(xs, row(small["final_norm"]), target.reshape(t, D_MODEL))

    g_ffn = lax.empty(big["ffn"].shape, BF16)
    gb = {k: [None] * 2 for k in ("ev_w_in_t", "s5_w_glu", "ev_w_out", "od_w_in", "gdn_conv_w", "od_w_out")}
    gs = {k: [None] * DEPTH for k in ("ffn1_norm", "mix_norm", "ffn2_norm")}
    gs.update({k: [None] * 2 for k in ("hg_norm_w", "s5_a_re", "s5_a_im", "s5_b_re", "s5_b_im", "s5_c_re", "s5_c_im",
                                       "s5_d", "s5_log_dt", "gdn_a_log", "gdn_dt_bias", "gdn_norm_w")})
    dlbs = [None] * 2
    for layer in reversed(range(DEPTH)):
        j = layer // 2
        p = layers[layer]
        r1, r2, r3 = saved[layer]
        dx, dnw, g_ffn = _ffn_bwd(dx, r3, row(small["ffn2_norm"][layer]), big["ffn"], g_ffn, 1, layer)
        gs["ffn2_norm"][layer] = dnw[0]
        if layer % 2 == 0:
            dx, gw, sm = _even_bwd(dx, r2, p, nb, l)
            gb["ev_w_in_t"][j], gb["s5_w_glu"][j], gb["ev_w_out"][j] = gw["w_in_t"], gw["w_glu"], gw["w_out"]
            dlbs[j] = sm["lb"]
            gs["hg_norm_w"][j] = sm["hg_nw"]
            gs["s5_d"][j] = sm["s5_d"]
            (gs["s5_a_re"][j], gs["s5_a_im"][j], gs["s5_b_re"][j], gs["s5_b_im"][j], gs["s5_c_re"][j],
             gs["s5_c_im"][j], gs["s5_log_dt"][j]) = s5_vjps[j]((sm["abar"], sm["bbd"], sm["ccd"]))
        else:
            dx, gw, sm = _odd_bwd(dx, r2, p, nb, l)
            gb["od_w_in"][j], gb["gdn_conv_w"][j], gb["od_w_out"][j] = gw["w_in"], gw["conv_w"], gw["w_out"]
            gs["gdn_a_log"][j], gs["gdn_dt_bias"][j], gs["gdn_norm_w"][j] = sm["a_log"], sm["dt_bias"], sm["gdn_nw"]
        gs["mix_norm"][layer] = sm["mix_norm"][0]
        dx, dnw, g_ffn = _ffn_bwd(dx, r1, row(small["ffn1_norm"][layer]), big["ffn"], g_ffn, 0, layer)
        gs["ffn1_norm"][layer] = dnw[0]
    gbig = dict(ffn=g_ffn, ev_w_in_t=jnp.stack(gb["ev_w_in_t"]), s5_w_glu=jnp.stack(gb["s5_w_glu"]),
                w_out=jnp.stack(gb["ev_w_out"] + gb["od_w_out"]), od_w_in=jnp.stack(gb["od_w_in"]),
                gdn_conv_w=jnp.stack(gb["gdn_conv_w"]))
    gsmall = {k: jnp.stack(v) for k, v in gs.items()}
    gsmall["hg_lb_logits"] = lbs_vjp(jnp.stack(dlbs))[0]
    gsmall["final_norm"] = dfinal[0]
    return loss[0, 0], dx.reshape(nb, l, D_MODEL), gbig, gsmall


def _here():
    return lax.axis_index("x"), lax.axis_index("y"), lax.axis_index("c")


def _other_chips(x, y):
    return [(1 - x, y), (x, 1 - y), (1 - x, 1 - y)]


def _all_gather(blocks, name):
    n = len(blocks)

    def body(*refs):
        x_refs, o_refs = refs[:n], refs[n:2 * n]
        send_sems, recv_sems, local_sems = refs[2 * n:]
        x, y, cc = _here()
        me, sibling = (x, y, cc), (x, y, 1 - cc)
        chips = _other_chips(x, y)

        def win(i, p):
            return o_refs[i].at[:, 4 * p[0] + 2 * p[1] + p[2]]

        def copy(i, k, blk, to, src=None):
            return pltpu.make_async_remote_copy(
                src_ref=win(i, blk) if src is None else src, dst_ref=win(i, blk),
                send_sem=send_sems.at[i, k], recv_sem=recv_sems.at[i, k], device_id=to, device_id_type=MESH)

        mine = [pltpu.make_async_copy(x_refs[i], win(i, me), local_sems.at[i]) for i in range(n)]
        for cp in mine:
            cp.start()
        first = []
        for i in range(n):
            first.append(copy(i, 0, me, sibling, src=x_refs[i]))
            first += [copy(i, 1 + j, me, (*chip, cc), src=x_refs[i]) for j, chip in enumerate(chips)]
        for cp in first:
            cp.start()
        passed = []
        for j, chip in enumerate(chips):
            for i in range(n):
                copy(i, 1 + j, (*chip, cc), me).wait_recv()
                fwd = copy(i, 4 + j, (*chip, cc), sibling)
                fwd.start()
                passed.append(fwd)
        for i in range(n):
            copy(i, 0, sibling, me).wait_recv()
            for j, chip in enumerate(chips):
                copy(i, 4 + j, (*chip, 1 - cc), me).wait_recv()
        for cp in first + passed:
            cp.wait_send()
        for cp in mine:
            cp.wait()

    hbm = pl.BlockSpec(memory_space=pl.ANY)
    return pl.pallas_call(
        body, name=name, in_specs=[hbm] * n, out_specs=[hbm] * n,
        out_shape=[jax.ShapeDtypeStruct((b.shape[0], N_DEV) + b.shape[1:], b.dtype) for b in blocks],
        scratch_shapes=[pltpu.SemaphoreType.DMA((n, 7)), pltpu.SemaphoreType.DMA((n, 7)), pltpu.SemaphoreType.DMA((n,))],
    )(*blocks)


def _rs_core_exchange(gs):
    n = len(gs)

    def body(*refs):
        g_refs, o_refs = refs[:n], refs[n:2 * n]
        send_sems, recv_sems = refs[2 * n:]
        x, y, cc = _here()
        cps = [pltpu.make_async_remote_copy(
            src_ref=g_refs[i].at[:, 2 * j + (1 - cc)], dst_ref=o_refs[i].at[j],
            send_sem=send_sems.at[i, j], recv_sem=recv_sems.at[i, j], device_id=(x, y, 1 - cc), device_id_type=MESH)
            for i in range(n) for j in range(4)]
        for cp in cps:
            cp.start()
        for cp in cps:
            cp.wait()

    hbm = pl.BlockSpec(memory_space=pl.ANY)
    return pl.pallas_call(
        body, name="rs_core_exchange", in_specs=[hbm] * n, out_specs=[hbm] * n,
        out_shape=[jax.ShapeDtypeStruct((4, g.shape[0]) + g.shape[2:], g.dtype) for g in gs],
        scratch_shapes=[pltpu.SemaphoreType.DMA((n, 4)), pltpu.SemaphoreType.DMA((n, 4))],
    )(*gs)


def _rs_chip_exchange(ps):
    n = len(ps)

    def body(*refs):
        p_refs, o_refs = refs[:n], refs[n:2 * n]
        send_sems, recv_sems = refs[2 * n:]
        x, y, cc = _here()
        cps = [pltpu.make_async_remote_copy(
            src_ref=p_refs[i].at[2 * px + py], dst_ref=o_refs[i].at[k],
            send_sem=send_sems.at[i, k], recv_sem=recv_sems.at[i, k], device_id=(px, py, cc), device_id_type=MESH)
            for i in range(n) for k, (px, py) in enumerate(_other_chips(x, y))]
        for cp in cps:
            cp.start()
        for cp in cps:
            cp.wait()

    hbm = pl.BlockSpec(memory_space=pl.ANY)
    return pl.pallas_call(
        body, name="rs_chip_exchange", in_specs=[hbm] * n, out_specs=[hbm] * n,
        out_shape=[jax.ShapeDtypeStruct((3,) + p.shape[1:], p.dtype) for p in ps],
        scratch_shapes=[pltpu.SemaphoreType.DMA((n, 3)), pltpu.SemaphoreType.DMA((n, 3))],
    )(*ps)


def _row_tile(w):
    return w if w <= 512 else _tile(w, (512, 256, 128))


def _rs_chip_sum(g, r1, core):
    a, _, w, c = g.shape
    tr = _row_tile(w)

    def body(core_ref, g_ref, r_ref, o_ref):
        o_ref[...] = (g_ref[...].astype(F32) + r_ref[...].astype(F32)).astype(o_ref.dtype)

    blk = pl.BlockSpec((None, None, tr, c), lambda j, aa, i, core_ref: (j, aa, i, 0))
    return pl.pallas_call(
        body, name="rs_chip_sum",
        grid_spec=pltpu.PrefetchScalarGridSpec(
            num_scalar_prefetch=1, grid=(4, a, w // tr),
            in_specs=[pl.BlockSpec((None, None, tr, c), lambda j, aa, i, core_ref: (aa, 2 * j + core_ref[0], i, 0)), blk],
            out_specs=blk),
        out_shape=jax.ShapeDtypeStruct((4, a, w, c), g.dtype),
        compiler_params=_params(("parallel", "parallel", "parallel")),
    )(core, g, r1)


def _rs_final_sum(g, r1, r2, place):
    a, _, w, c = g.shape
    tr = _row_tile(w)

    def body(place_ref, g_ref, r1_ref, r2_ref, o_ref):
        s = g_ref[...].astype(F32) + r1_ref[...].astype(F32)
        for k in range(3):
            s = s + r2_ref[k].astype(F32)
        o_ref[...] = s

    return pl.pallas_call(
        body, name="rs_final_sum",
        grid_spec=pltpu.PrefetchScalarGridSpec(
            num_scalar_prefetch=1, grid=(a, w // tr),
            in_specs=[pl.BlockSpec((None, None, tr, c), lambda aa, i, place_ref: (aa, place_ref[0], i, 0)),
                      pl.BlockSpec((None, None, tr, c), lambda aa, i, place_ref: (place_ref[1], aa, i, 0)),
                      pl.BlockSpec((3, None, tr, c), lambda aa, i, place_ref: (0, aa, i, 0))],
            out_specs=pl.BlockSpec((None, tr, c), lambda aa, i, place_ref: (aa, i, 0))),
        out_shape=jax.ShapeDtypeStruct((a, w, c), F32),
        compiler_params=_params(("parallel", "parallel")),
    )(place, g, r1, r2)


def _sum_slots(g):
    _, n, r, c = g.shape

    def body(g_ref, o_ref):
        s = g_ref[0, 0]
        for k in range(1, n):
            s = s + g_ref[0, k]
        o_ref[...] = s

    return pl.pallas_call(
        body, name="sum_slots", grid=(1,),
        in_specs=[pl.BlockSpec((1, n, r, c), lambda i: (0, 0, 0, 0))], out_specs=pl.BlockSpec((r, c), lambda i: (0, 0)),
        out_shape=jax.ShapeDtypeStruct((r, c), F32), compiler_params=_params(("arbitrary",)),
    )(g)


_WEIGHTS = ['ffn1_norm', 'ffn1_w_gate', 'ffn1_w_up', 'ffn1_w_down', 'mix_norm', 'ffn2_norm', 'ffn2_w_gate', 'ffn2_w_up',
            'ffn2_w_down', 'ev_w_in', 'hg_lb_logits', 'hg_norm_w', 's5_a_re', 's5_a_im', 's5_b_re', 's5_b_im', 's5_c_re',
            's5_c_im', 's5_d', 's5_log_dt', 's5_w_glu', 'ev_w_out', 'od_w_in', 'gdn_conv_w', 'gdn_a_log', 'gdn_dt_bias',
            'gdn_norm_w', 'od_w_out', 'final_norm']
_SHARDED = ['ffn1_w_gate', 'ffn1_w_up', 'ffn1_w_down', 'ffn2_w_gate', 'ffn2_w_up', 'ffn2_w_down', 'ev_w_in', 's5_w_glu',
            'ev_w_out', 'od_w_in', 'gdn_conv_w', 'od_w_out']
_REPLICATED = [n for n in _WEIGHTS if n not in _SHARDED]
_SMALL_COLS = 1024


def _pad_rows(a, rows):
    return jnp.pad(a, ((0, 0), (0, rows - a.shape[1]), (0, 0)))


def _gather_weights(w):
    t_ = lambda a: jnp.swapaxes(a, 1, 2)
    ffn = []
    for f in ('ffn1', 'ffn2'):
        ffn += [_pad_rows(t_(w[f + '_w_gate']), FF_SHARD_PAD), _pad_rows(t_(w[f + '_w_up']), FF_SHARD_PAD),
                _pad_rows(w[f + '_w_down'], FF_SHARD_PAD)]
    blocks = [jnp.concatenate(ffn).astype(BF16),
              t_(w['ev_w_in']).astype(BF16),
              w['s5_w_glu'].astype(BF16),
              jnp.concatenate([w['ev_w_out'], w['od_w_out']]).astype(BF16),
              w['od_w_in'].astype(BF16).reshape(1, 2 * D_MODEL, OD_IN // N_DEV),
              w['gdn_conv_w'].reshape(1, 2 * CONV_W, GDN_QKV // N_DEV)]
    ffn_w, in_t, glu, w_out, od_in, conv = _all_gather(blocks, "gather_weights")
    whole = lambda g: g.reshape(g.shape[0], N_DEV * g.shape[2], g.shape[3])
    od_in = jnp.moveaxis(od_in.reshape(N_DEV, 2, D_MODEL, OD_IN // N_DEV), 0, 2).reshape(2, D_MODEL, OD_IN)
    conv = jnp.moveaxis(conv.reshape(N_DEV, 2, CONV_W, GDN_QKV // N_DEV), 0, 2).reshape(2, CONV_W, GDN_QKV)
    return dict(ffn=whole(ffn_w), ev_w_in_t=whole(in_t), s5_w_glu=whole(glu), w_out=whole(w_out),
                od_w_in=jnp.pad(od_in, ((0, 0), (0, 0), (0, OD_IN_PAD - OD_IN))), gdn_conv_w=conv)


def _scatter_grads(gbig):
    x, y, c = _here()
    windows = lambda g, w: g.reshape(g.shape[0], N_DEV, w, g.shape[2])
    od_in = jnp.moveaxis(gbig['od_w_in'].reshape(2, D_MODEL, N_DEV, OD_IN // N_DEV), 2, 0)
    conv = jnp.moveaxis(gbig['gdn_conv_w'].reshape(2, CONV_W, N_DEV, GDN_QKV // N_DEV), 2, 0)
    gs = [windows(gbig['ffn'], FF_SHARD_PAD), windows(gbig['ev_w_in_t'], EV_IN // N_DEV),
          windows(gbig['s5_w_glu'], S5_WIDTH // N_DEV), windows(gbig['w_out'], D_MODEL // N_DEV),
          od_in.reshape(1, N_DEV, 2 * D_MODEL, OD_IN // N_DEV), conv.reshape(1, N_DEV, 2 * CONV_W, GDN_QKV // N_DEV)]
    core = jnp.reshape(c, (1,)).astype(jnp.int32)
    place = jnp.stack([4 * x + 2 * y + c, 2 * x + y]).astype(jnp.int32)
    r1 = _rs_core_exchange(gs)
    ps = [_rs_chip_sum(g, r, core) for g, r in zip(gs, r1)]
    r2 = _rs_chip_exchange(ps)
    ffn, in_t, glu, w_out, od_in, conv = (_rs_final_sum(g, ra, rb, place) for g, ra, rb in zip(gs, r1, r2))
    t_ = lambda a: jnp.swapaxes(a, 1, 2)
    ffn = ffn[:, :FF_SHARD].reshape(2, 3, DEPTH, FF_SHARD, D_MODEL)
    out = {}
    for i, f in enumerate(('ffn1', 'ffn2')):
        out[f + '_w_gate'], out[f + '_w_up'], out[f + '_w_down'] = t_(ffn[i, 0]), t_(ffn[i, 1]), ffn[i, 2]
    out['ev_w_in'] = t_(in_t)
    out['s5_w_glu'] = glu
    out['ev_w_out'], out['od_w_out'] = w_out[:2], w_out[2:]
    out['od_w_in'] = od_in.reshape(2, D_MODEL, OD_IN // N_DEV)
    out['gdn_conv_w'] = conv.reshape(2, CONV_W, GDN_QKV // N_DEV)
    return out


def _flat_small(src, prefix=''):
    flat = jnp.concatenate([src[prefix + n].reshape(-1) for n in _REPLICATED])
    rows = -(-(flat.shape[0] + 1) // (_SMALL_COLS * 8)) * 8
    return jnp.pad(flat, (0, rows * _SMALL_COLS - flat.shape[0])).reshape(rows, _SMALL_COLS)


def _split_small(flat2d, like):
    flat, out, off = flat2d.reshape(-1), {}, 0
    for n in _REPLICATED:
        size = math.prod(like[n].shape)
        out[n] = flat[off:off + size].reshape(like[n].shape)
        off += size
    return out, flat[off]


def _all_reduce_small(gsmall, loss):
    flat = _flat_small(gsmall)
    n_used = sum(math.prod(gsmall[n].shape) for n in _REPLICATED)
    flat = flat.reshape(-1).at[n_used].set(loss).reshape(flat.shape)
    (gathered,) = _all_gather([flat[None]], "gather_small")
    return _split_small(_sum_slots(gathered), gsmall)


def _as_2d(a):
    return a.reshape(-1, a.shape[-1])


def kernel(x, ffn1_norm, ffn1_w_gate, ffn1_w_up, ffn1_w_down, mix_norm, ffn2_norm, ffn2_w_gate, ffn2_w_up, ffn2_w_down, ev_w_in, hg_lb_logits, hg_norm_w, s5_a_re, s5_a_im, s5_b_re, s5_b_im, s5_c_re, s5_c_im, s5_d, s5_log_dt, s5_w_glu, ev_w_out, od_w_in, gdn_conv_w, gdn_a_log, gdn_dt_bias, gdn_norm_w, od_w_out, final_norm, loss_target, m_ffn1_norm, m_ffn1_w_gate, m_ffn1_w_up, m_ffn1_w_down, m_mix_norm, m_ffn2_norm, m_ffn2_w_gate, m_ffn2_w_up, m_ffn2_w_down, m_ev_w_in, m_hg_lb_logits, m_hg_norm_w, m_s5_a_re, m_s5_a_im, m_s5_b_re, m_s5_b_im, m_s5_c_re, m_s5_c_im, m_s5_d, m_s5_log_dt, m_s5_w_glu, m_ev_w_out, m_od_w_in, m_gdn_conv_w, m_gdn_a_log, m_gdn_dt_bias, m_gdn_norm_w, m_od_w_out, m_final_norm, v_ffn1_norm, v_ffn1_w_gate, v_ffn1_w_up, v_ffn1_w_down, v_mix_norm, v_ffn2_norm, v_ffn2_w_gate, v_ffn2_w_up, v_ffn2_w_down, v_ev_w_in, v_hg_lb_logits, v_hg_norm_w, v_s5_a_re, v_s5_a_im, v_s5_b_re, v_s5_b_im, v_s5_c_re, v_s5_c_im, v_s5_d, v_s5_log_dt, v_s5_w_glu, v_ev_w_out, v_od_w_in, v_gdn_conv_w, v_gdn_a_log, v_gdn_dt_bias, v_gdn_norm_w, v_od_w_out, v_final_norm):
    given = dict(locals())
    w = {n: given[n] for n in _WEIGHTS}
    big = _gather_weights(w)
    small = {n: w[n] for n in _REPLICATED}
    loss_part, grad_x, gbig, gsmall = _local_step(given['x'], given['loss_target'], big, small)
    grads = _scatter_grads(gbig)
    gsum, loss = _all_reduce_small(gsmall, loss_part)
    grads.update(gsum)

    delta, new_m, new_v = {}, {}, {}
    for n in _SHARDED:
        d, nm, nv = _adamw(_as_2d(w[n]), _as_2d(grads[n]), _as_2d(given['m_' + n]), _as_2d(given['v_' + n]))
        delta[n], new_m[n], new_v[n] = (a.reshape(w[n].shape) for a in (d, nm, nv))
    d, nm, nv = _adamw(_flat_small(w), _flat_small(grads), _flat_small(given, 'm_'), _flat_small(given, 'v_'))
    (delta_s, _), (new_m_s, _), (new_v_s, _) = (_split_small(a, small) for a in (d, nm, nv))
    delta.update(delta_s)
    new_m.update(new_m_s)
    new_v.update(new_v_s)
    return (loss, grad_x, *[grads[n] for n in _WEIGHTS], *[delta[n] for n in _WEIGHTS],
            *[new_m[n] for n in _WEIGHTS], *[new_v[n] for n in _WEIGHTS])
```

```python
import math

import jax
import jax.numpy as jnp
from jax import lax
from jax.experimental import pallas as pl
from jax.experimental.pallas import tpu as pltpu

F32 = jnp.float32
BF16 = jnp.bfloat16
HI = lax.Precision.HIGH

D_MODEL = 1024
DEPTH = 4
D_FF = 2816
NORM_EPS = 1e-6
F_MIN = 1e-6
CHUNK = 64
HG_HEADS = 4
HEAD_DIM = 128
HG_QK = HG_HEADS * HEAD_DIM
S5_WIDTH = 512
S5_GROUP = 16
S5_GROUPS = 32
S5_STATE = 64
S5_NSTATE = S5_GROUPS * S5_STATE
EV_IN = 2560
GDN_HEADS = 8
GDN_QKV = 3 * GDN_HEADS * HEAD_DIM
CONV_W = 4
OD_IN = 4112
OD_IN_PAD = 4224
N_DEV = 8
LANES = 128
ADAM_LR, ADAM_B1, ADAM_B2, ADAM_EPS, ADAM_WD, ADAM_STEP = 0.001, 0.9, 0.999, 1e-08, 0.01, 10
VMEM_LIMIT = 56 * 1024 * 1024

MESH = pl.DeviceIdType.MESH


def _params(sem=None, **kw):
    return pltpu.CompilerParams(dimension_semantics=sem, vmem_limit_bytes=VMEM_LIMIT, **kw)


def _tile(n, cands):
    for c in cands:
        if n % c == 0:
            return c
    return n


def _dot(a, b):
    return jnp.dot(a.astype(BF16), b.astype(BF16), preferred_element_type=F32)


def _dot_nt(a, b):
    return lax.dot_general(a.astype(BF16), b.astype(BF16), (((1,), (1,)), ((), ())), preferred_element_type=F32)


def _dot_tn(a, b):
    return lax.dot_general(a.astype(BF16), b.astype(BF16), (((0,), (0,)), ((), ())), preferred_element_type=F32)


def _dot_hi(a, b):
    return jnp.dot(a, b, precision=HI, preferred_element_type=F32)


def _bmm(a, b):
    return jnp.einsum('gmk,gkn->gmn', a.astype(BF16), b.astype(BF16), preferred_element_type=F32)


def _bmm_nt(a, b):
    return jnp.einsum('gmk,gnk->gmn', a.astype(BF16), b.astype(BF16), preferred_element_type=F32)


def _bmm_tn(a, b):
    return jnp.einsum('gkm,gkn->gmn', a.astype(BF16), b.astype(BF16), preferred_element_type=F32)


def _bmm_hi(a, b):
    return jnp.einsum('gmk,gkn->gmn', a, b, precision=HI, preferred_element_type=F32)


_TN_CANDS = (1408, 1280, 1024, 512, 384, 256, 128)
_B_TILE_BYTES = 6 * 1024 * 1024


def _mm(a, b, *, name, nt=False, out_dtype=F32, res=None, scale=1.0):
    m, k = a.shape
    n = b.shape[0] if nt else b.shape[1]
    tm = _tile(m, (512, 256, 128))
    tn = _tile(n, [c for c in _TN_CANDS if c * k * b.dtype.itemsize <= _B_TILE_BYTES])

    def body(*refs):
        if res is None:
            a_ref, b_ref, o_ref = refs
        else:
            a_ref, b_ref, r_ref, o_ref = refs
        acc = (_dot_nt if nt else _dot)(a_ref[...], b_ref[...])
        if scale != 1.0:
            acc = scale * acc
        if res is not None:
            acc = r_ref[...] + acc
        o_ref[...] = acc.astype(out_dtype)

    b_spec = pl.BlockSpec((tn, k), lambda j, i: (j, 0)) if nt else pl.BlockSpec((k, tn), lambda j, i: (0, j))
    in_specs = [pl.BlockSpec((tm, k), lambda j, i: (i, 0)), b_spec]
    args = [a, b]
    if res is not None:
        in_specs.append(pl.BlockSpec((tm, tn), lambda j, i: (i, j)))
        args.append(res)
    return pl.pallas_call(
        body, name=name, grid=(n // tn, m // tm), in_specs=in_specs,
        out_specs=pl.BlockSpec((tm, tn), lambda j, i: (i, j)),
        out_shape=jax.ShapeDtypeStruct((m, n), out_dtype),
        compiler_params=_params(("parallel", "parallel")),
    )(*args)


def _mm_tn(a, b, *, name, scale=1.0, out_dtype=F32, into=None, slot=0):
    t, m = a.shape
    n = b.shape[1]
    tm = _tile(m, (1024, 512, 256, 128))
    tn = _tile(n, _TN_CANDS)
    tk = _tile(t, (512, 256, 128))
    nk = t // tk

    def body(*refs):
        a_ref, b_ref = refs[:2]
        o_ref, acc_ref = refs[-2:]
        kk = pl.program_id(2)

        @pl.when(kk == 0)
        def _():
            acc_ref[...] = jnp.zeros_like(acc_ref)

        acc_ref[...] += _dot_tn(a_ref[...], b_ref[...])

        @pl.when(kk == nk - 1)
        def _():
            o_ref[...] = (acc_ref[...] * scale if scale != 1.0 else acc_ref[...]).astype(o_ref.dtype)

    in_specs = [pl.BlockSpec((tk, tm), lambda i, j, kk: (kk, i)), pl.BlockSpec((tk, tn), lambda i, j, kk: (kk, j))]
    args = [a, b]
    if into is None:
        out_spec = pl.BlockSpec((tm, tn), lambda i, j, kk: (i, j))
        out_shape = jax.ShapeDtypeStruct((m, n), out_dtype)
        alias = {}
    else:
        in_specs.append(pl.BlockSpec(memory_space=pl.ANY))
        args.append(into)
        out_spec = pl.BlockSpec((None, tm, tn), lambda i, j, kk: (slot, i, j))
        out_shape = jax.ShapeDtypeStruct(into.shape, into.dtype)
        alias = {2: 0}
    return pl.pallas_call(
        body, name=name, grid=(m // tm, n // tn, nk), in_specs=in_specs, out_specs=out_spec, out_shape=out_shape,
        scratch_shapes=[pltpu.VMEM((tm, tn), F32)], input_output_aliases=alias,
        compiler_params=_params(("parallel", "parallel", "arbitrary")),
    )(*args)


def _sigmoid(x):
    return jax.nn.sigmoid(x)


def _head_norm_gate(o, gate, nw):
    r = lax.rsqrt(jnp.mean(o * o, axis=-1, keepdims=True) + NORM_EPS)
    return o * r * nw * (gate * _sigmoid(gate))


def _hg_chunk(st, ql, fl, iv, gl, lb, nw):
    n_g, c, _ = ql.shape
    q = ql * _sigmoid(ql)
    f = lb + (1.0 - lb) * _sigmoid(fl)
    lf = jnp.log(jnp.maximum(f, F_MIN))
    k = 1.0 - f
    ri = lax.broadcasted_iota(jnp.int32, (c, c), 0)
    ci = lax.broadcasted_iota(jnp.int32, (c, c), 1)
    rl = lax.broadcasted_iota(jnp.int32, (c, LANES), 0)
    each = lambda m2: jnp.broadcast_to(m2[None], (n_g,) + m2.shape)
    b = _bmm_hi(each(jnp.where(ci <= ri, 1.0, 0.0)), lf)
    attn = jnp.where((ri == ci)[None], _bmm_nt(q, k), 0.0)
    sh = 0
    while (1 << sh) < c:
        m = 1 << sh
        ref = ((ri >> (sh + 1)) << (sh + 1)) + (m - 1)
        low_r = ((ri >> sh) & 1) == 1
        low_c = ((ci >> sh) & 1) == 1
        w_low = low_r & (ci > ref) & (ci <= ri)
        w_up = jnp.logical_not(low_r) & (ci > ri) & (ci <= ref)
        w = jnp.where(w_low | w_up, 1.0, 0.0)
        e = jnp.exp(_bmm_hi(each(w), lf))
        low_l = (((rl >> sh) & 1) == 1)[None]
        qs = jnp.where(low_l, q * e, 0.0)
        ks = jnp.where(low_l, 0.0, k * e)
        pair = ((ri >> (sh + 1)) == (ci >> (sh + 1))) & low_r & jnp.logical_not(low_c)
        attn = attn + jnp.where(pair[None], _bmm_nt(qs, ks), 0.0)
        sh += 1
    bl = jnp.sum(lf, axis=1, keepdims=True)
    o = _bmm(attn, iv) + _bmm_nt(q * jnp.exp(b), st)
    st_new = st * jnp.exp(bl) + _bmm_tn(iv, k * jnp.exp(bl - b))
    return _head_norm_gate(o, gl, nw), st_new


def _hg_specs(nb, n_c, rev):
    def cidx(cc):
        return (n_c - 1 - cc) if rev else cc

    def col(off):
        return pl.BlockSpec((nb, CHUNK, HG_QK), lambda cc: (0, cidx(cc), off))

    vec = pl.BlockSpec((HG_HEADS, 1, LANES), lambda cc: (0, 0, 0))
    nw = pl.BlockSpec((1, LANES), lambda cc: (0, 0))
    st = pl.BlockSpec((nb, HG_HEADS, None, LANES, LANES), lambda cc: (0, 0, cidx(cc), 0, 0))
    acc = pl.BlockSpec((nb, HG_HEADS, 1, LANES), lambda cc: (0, 0, 0, 0))
    return col, vec, nw, st, acc


def _head(j):
    return slice(j * LANES, (j + 1) * LANES)


def _seq_heads(ref, nb, n):
    return jnp.stack([ref[b, :, _head(j)] for b in range(nb) for j in range(n)])


def _put_seq_heads(ref, val, nb, n):
    for b in range(nb):
        for j in range(n):
            ref[b, :, _head(j)] = val[b * n + j].astype(ref.dtype)


def _hgrn2_fwd(proj, lb, nw):
    nb, l, _ = proj.shape
    n_c = l // CHUNK
    n_g = nb * HG_HEADS
    col, vec, nws, st, _ = _hg_specs(nb, n_c, False)

    def body(ql_ref, fl_ref, iv_ref, gl_ref, lb_ref, nw_ref, y_ref, st_ref, s_scr):
        @pl.when(pl.program_id(0) == 0)
        def _():
            s_scr[...] = jnp.zeros_like(s_scr)

        s_in = s_scr[...]
        st_ref[...] = s_in.reshape(st_ref.shape)
        args = [_seq_heads(r, nb, HG_HEADS) for r in (ql_ref, fl_ref, iv_ref, gl_ref)]
        y, s_new = _hg_chunk(s_in, *args, jnp.concatenate([lb_ref[...]] * nb), nw_ref[...])
        _put_seq_heads(y_ref, y, nb, HG_HEADS)
        s_scr[...] = s_new

    return pl.pallas_call(
        body, name="hgrn2_fwd", grid=(n_c,),
        in_specs=[col(0), col(1), col(2), col(3), vec, nws],
        out_specs=[col(0), st],
        out_shape=[jax.ShapeDtypeStruct((nb, l, HG_QK), BF16),
                   jax.ShapeDtypeStruct((nb, HG_HEADS, n_c, LANES, LANES), F32)],
        scratch_shapes=[pltpu.VMEM((n_g, LANES, LANES), F32)],
        compiler_params=_params(("arbitrary",)),
    )(proj, proj, proj, proj, lb, nw)


def _hgrn2_bwd(proj, lb, nw, states, dy):
    nb, l, _ = proj.shape
    n_c = l // CHUNK
    n_g = nb * HG_HEADS
    col, vec, nws, st, acc = _hg_specs(nb, n_c, True)

    def body(ql_ref, fl_ref, iv_ref, gl_ref, lb_ref, nw_ref, st_ref, dy_ref,
             dql_ref, dfl_ref, div_ref, dgl_ref, dlb_ref, dnw_ref, ds_scr):
        @pl.when(pl.program_id(0) == 0)
        def _():
            ds_scr[...] = jnp.zeros_like(ds_scr)
            dlb_ref[...] = jnp.zeros_like(dlb_ref)
            dnw_ref[...] = jnp.zeros_like(dnw_ref)

        args = [_seq_heads(r, nb, HG_HEADS) for r in (ql_ref, fl_ref, iv_ref, gl_ref)]
        _, vjp = jax.vjp(_hg_chunk, st_ref[...].reshape(n_g, LANES, LANES), *args,
                         jnp.concatenate([lb_ref[...]] * nb), nw_ref[...])
        ds, dql, dfl, div, dgl, dlb, dnw = vjp((_seq_heads(dy_ref, nb, HG_HEADS).astype(F32), ds_scr[...]))
        ds_scr[...] = ds
        for ref, val in ((dql_ref, dql), (dfl_ref, dfl), (div_ref, div), (dgl_ref, dgl)):
            _put_seq_heads(ref, val, nb, HG_HEADS)
        dlb_ref[...] += dlb.reshape(dlb_ref.shape)
        dnw_ref[...] += dnw

    return pl.pallas_call(
        body, name="hgrn2_bwd", grid=(n_c,),
        in_specs=[col(0), col(1), col(2), col(3), vec, nws, st, col(0)],
        out_specs=[col(0), col(0), col(0), col(0), acc, nws],
        out_shape=[jax.ShapeDtypeStruct((nb, l, HG_QK), F32)] * 4
        + [jax.ShapeDtypeStruct((nb, HG_HEADS, 1, LANES), F32), jax.ShapeDtypeStruct((1, LANES), F32)],
        scratch_shapes=[pltpu.VMEM((n_g, LANES, LANES), F32)],
        compiler_params=_params(("arbitrary",)),
    )(proj, proj, proj, proj, lb, nw, states, dy)


def _shift_down(x, s):
    if s == 0:
        return x
    rows = lax.broadcasted_iota(jnp.int32, x.shape, 0)
    return jnp.where(rows >= s, pltpu.roll(x, s, 0), 0.0)


def _shift_up(x, s):
    if s == 0:
        return x
    n = x.shape[0]
    rows = lax.broadcasted_iota(jnp.int32, x.shape, 0)
    return jnp.where(rows < n - s, pltpu.roll(x, n - s, 0), 0.0)


def _gdn_pre_fwd(proj, conv_w):
    nb, l, _ = proj.shape
    n_blk = GDN_QKV // LANES

    def body(x_ref, w_ref, o_ref):
        j = pl.program_id(1)
        x = x_ref[...]
        w = w_ref[...]
        c = w[3:4] * x
        for t in range(CONV_W - 1):
            c = c + w[t:t + 1] * _shift_down(x, CONV_W - 1 - t)
        s = c * _sigmoid(c)
        r = lax.rsqrt(jnp.sum(s * s, axis=-1, keepdims=True) + NORM_EPS)
        scale = jnp.where(j < GDN_HEADS, HEAD_DIM ** -0.5, 1.0)
        o_ref[...] = jnp.where(j < 2 * GDN_HEADS, s * r * scale, s)

    return pl.pallas_call(
        body, name="gdn_pre_fwd", grid=(nb, n_blk),
        in_specs=[pl.BlockSpec((None, l, LANES), lambda b, j: (b, 0, j)), pl.BlockSpec((CONV_W, LANES), lambda b, j: (0, j))],
        out_specs=pl.BlockSpec((None, l, LANES), lambda b, j: (b, 0, j)),
        out_shape=jax.ShapeDtypeStruct((nb, l, GDN_QKV), F32),
        compiler_params=_params(("parallel", "parallel")),
    )(proj, conv_w)


def _gdn_pre_bwd(proj, conv_w, dout):
    nb, l, _ = proj.shape
    n_blk = GDN_QKV // LANES

    def body(x_ref, w_ref, d_ref, dx_ref, dw_ref):
        j = pl.program_id(1)
        x = x_ref[...]
        w = w_ref[...]
        xs = [_shift_down(x, CONV_W - 1 - t) for t in range(CONV_W)]
        c = w[0:1] * xs[0]
        for t in range(1, CONV_W):
            c = c + w[t:t + 1] * xs[t]
        sg = _sigmoid(c)
        s = c * sg
        d = d_ref[...]
        r = lax.rsqrt(jnp.sum(s * s, axis=-1, keepdims=True) + NORM_EPS)
        scale = jnp.where(j < GDN_HEADS, HEAD_DIM ** -0.5, 1.0)
        ds_norm = scale * r * (d - s * (r * r) * jnp.sum(d * s, axis=-1, keepdims=True))
        ds = jnp.where(j < 2 * GDN_HEADS, ds_norm, d)
        dc = ds * (sg * (1.0 + c * (1.0 - sg)))
        dx = w[3:4] * dc
        for t in range(CONV_W - 1):
            dx = dx + w[t:t + 1] * _shift_up(dc, CONV_W - 1 - t)
        dx_ref[...] = dx
        for t in range(CONV_W):
            dw_ref[t:t + 1, :] = jnp.sum(dc * xs[t], axis=0, keepdims=True)

    return pl.pallas_call(
        body, name="gdn_pre_bwd", grid=(nb, n_blk),
        in_specs=[pl.BlockSpec((None, l, LANES), lambda b, j: (b, 0, j)), pl.BlockSpec((CONV_W, LANES), lambda b, j: (0, j)),
                  pl.BlockSpec((None, l, LANES), lambda b, j: (b, 0, j))],
        out_specs=[pl.BlockSpec((None, l, LANES), lambda b, j: (b, 0, j)), pl.BlockSpec((None, CONV_W, LANES), lambda b, j: (b, 0, j))],
        out_shape=[jax.ShapeDtypeStruct((nb, l, GDN_QKV), F32), jax.ShapeDtypeStruct((nb, CONV_W, GDN_QKV), F32)],
        compiler_params=_params(("parallel", "parallel")),
    )(proj, conv_w, dout)


def _gdn_chunk(st, q, k, v, gate, bl, al, alog, dtb, nw):
    n_g, c, _ = q.shape
    beta = _sigmoid(bl)
    x = al + dtb
    softplus = jnp.maximum(x, 0.0) + jnp.log(1.0 + jnp.exp(-jnp.abs(x)))
    la = -jnp.exp(alog) * softplus
    ri = lax.broadcasted_iota(jnp.int32, (c, c), 0)
    ci = lax.broadcasted_iota(jnp.int32, (c, c), 1)
    lower = (ci <= ri)[None]
    strict = (ci < ri)[None]
    ltri = jnp.broadcast_to(jnp.where(lower, 1.0, 0.0), (n_g, c, c))
    la_l = la + jnp.zeros((1, 1, LANES), F32)
    g = _bmm_hi(ltri, la_l)
    delta = _bmm_hi(ltri, jnp.where(strict, la + jnp.zeros((1, 1, c), F32), 0.0))
    gam = jnp.where(lower, jnp.exp(jnp.where(lower, delta, 0.0)), 0.0)
    kb = k * beta
    vb = v * beta
    m = jnp.where(strict, _bmm_nt(kb, k) * gam, 0.0)
    pw = -m
    t_inv = jnp.where((ri == ci)[None], 1.0, 0.0) + pw
    steps = int(math.log2(c)) - 1
    for _ in range(steps):
        pw = _bmm_hi(pw, pw)
        t_inv = t_inv + _bmm_hi(t_inv, pw)
    eg = jnp.exp(g)
    u = _bmm(t_inv, vb)
    w = _bmm(t_inv, kb * eg)
    attn = jnp.where(lower, _bmm_nt(q, k) * gam, 0.0)
    v_new = u - _bmm_nt(w, st)
    o = _bmm_nt(q * eg, st) + _bmm(attn, v_new)
    g_last = jnp.sum(la_l, axis=1, keepdims=True)
    st_new = st * jnp.exp(g_last) + _bmm_tn(v_new, k * jnp.exp(g_last - g))
    return _head_norm_gate(o, gate, nw), st_new


GDN_WIDE = GDN_HEADS * LANES


def _gdn_specs(nb, n_c, rev):
    def cidx(cc):
        return (n_c - 1 - cc) if rev else cc

    def col(off):
        return pl.BlockSpec((nb, CHUNK, GDN_WIDE), lambda cc: (0, cidx(cc), off))

    tok = pl.BlockSpec((nb, GDN_HEADS, CHUNK, 1), lambda cc: (0, 0, cidx(cc), 0))
    scal = pl.BlockSpec((GDN_HEADS, 1, 1), lambda cc: (0, 0, 0))
    nw = pl.BlockSpec((1, LANES), lambda cc: (0, 0))
    st = pl.BlockSpec((nb, GDN_HEADS, None, LANES, LANES), lambda cc: (0, 0, cidx(cc), 0, 0))
    acc_s = pl.BlockSpec((nb, GDN_HEADS, 1, 1), lambda cc: (0, 0, 0, 0))
    return col, tok, scal, nw, st, acc_s


def _gdn_fwd(qkv, proj, bl, al, alog, dtb, nw):
    nb, l, _ = qkv.shape
    n_c = l // CHUNK
    n_g = nb * GDN_HEADS
    col, tok, scal, nws, st, _ = _gdn_specs(nb, n_c, False)

    def body(q_ref, k_ref, v_ref, g_ref, bl_ref, al_ref, alog_ref, dtb_ref, nw_ref, y_ref, st_ref, s_scr):
        @pl.when(pl.program_id(0) == 0)
        def _():
            s_scr[...] = jnp.zeros_like(s_scr)

        s_in = s_scr[...]
        st_ref[...] = s_in.reshape(st_ref.shape)
        args = [_seq_heads(r, nb, GDN_HEADS) for r in (q_ref, k_ref, v_ref, g_ref)]
        y, s_new = _gdn_chunk(s_in, *args, bl_ref[...].reshape(n_g, CHUNK, 1), al_ref[...].reshape(n_g, CHUNK, 1),
                              jnp.concatenate([alog_ref[...]] * nb), jnp.concatenate([dtb_ref[...]] * nb), nw_ref[...])
        _put_seq_heads(y_ref, y, nb, GDN_HEADS)
        s_scr[...] = s_new

    return pl.pallas_call(
        body, name="gdn_fwd", grid=(n_c,),
        in_specs=[col(0), col(1), col(2), col(3), tok, tok, scal, scal, nws],
        out_specs=[col(0), st],
        out_shape=[jax.ShapeDtypeStruct((nb, l, GDN_WIDE), BF16),
                   jax.ShapeDtypeStruct((nb, GDN_HEADS, n_c, LANES, LANES), F32)],
        scratch_shapes=[pltpu.VMEM((n_g, LANES, LANES), F32)],
        compiler_params=_params(("arbitrary",)),
    )(qkv, qkv, qkv, proj, bl, al, alog, dtb, nw)


def _gdn_bwd(qkv, proj, bl, al, alog, dtb, nw, states, dy):
    nb, l, _ = qkv.shape
    n_c = l // CHUNK
    n_g = nb * GDN_HEADS
    col, tok, scal, nws, st, acc_s = _gdn_specs(nb, n_c, True)

    def body(q_ref, k_ref, v_ref, g_ref, bl_ref, al_ref, alog_ref, dtb_ref, nw_ref, st_ref, dy_ref,
             dq_ref, dk_ref, dv_ref, dg_ref, dbl_ref, dal_ref, dalog_ref, ddtb_ref, dnw_ref, ds_scr):
        @pl.when(pl.program_id(0) == 0)
        def _():
            ds_scr[...] = jnp.zeros_like(ds_scr)
            dalog_ref[...] = jnp.zeros_like(dalog_ref)
            ddtb_ref[...] = jnp.zeros_like(ddtb_ref)
            dnw_ref[...] = jnp.zeros_like(dnw_ref)

        args = [_seq_heads(r, nb, GDN_HEADS) for r in (q_ref, k_ref, v_ref, g_ref)]
        _, vjp = jax.vjp(_gdn_chunk, st_ref[...].reshape(n_g, LANES, LANES), *args,
                         bl_ref[...].reshape(n_g, CHUNK, 1), al_ref[...].reshape(n_g, CHUNK, 1),
                         jnp.concatenate([alog_ref[...]] * nb), jnp.concatenate([dtb_ref[...]] * nb), nw_ref[...])
        ds, dq, dk, dv, dg, dbl, dal, dalog, ddtb, dnw = vjp((_seq_heads(dy_ref, nb, GDN_HEADS).astype(F32), ds_scr[...]))
        ds_scr[...] = ds
        for ref, val in ((dq_ref, dq), (dk_ref, dk), (dv_ref, dv), (dg_ref, dg)):
            _put_seq_heads(ref, val, nb, GDN_HEADS)
        dbl_ref[...] = dbl.reshape(dbl_ref.shape)
        dal_ref[...] = dal.reshape(dal_ref.shape)
        dalog_ref[...] += dalog.reshape(dalog_ref.shape)
        ddtb_ref[...] += ddtb.reshape(ddtb_ref.shape)
        dnw_ref[...] += dnw

    wide = jax.ShapeDtypeStruct((nb, l, GDN_WIDE), F32)
    h = GDN_HEADS
    return pl.pallas_call(
        body, name="gdn_bwd", grid=(n_c,),
        in_specs=[col(0), col(1), col(2), col(3), tok, tok, scal, scal, nws, st, col(0)],
        out_specs=[col(0), col(0), col(0), col(0), tok, tok, acc_s, acc_s, nws],
        out_shape=[wide, wide, wide, wide,
                   jax.ShapeDtypeStruct((nb, h, l, 1), F32), jax.ShapeDtypeStruct((nb, h, l, 1), F32),
                   jax.ShapeDtypeStruct((nb, h, 1, 1), F32), jax.ShapeDtypeStruct((nb, h, 1, 1), F32),
                   jax.ShapeDtypeStruct((1, LANES), F32)],
        scratch_shapes=[pltpu.VMEM((n_g, LANES, LANES), F32)],
        compiler_params=_params(("arbitrary",)),
    )(qkv, qkv, qkv, proj, bl, al, alog, dtb, nw, states, dy)


S5_ROWS = S5_NSTATE // LANES
S5_TB = 256


def _s5_scan_fwd(bu, abar):
    nb, l = bu.shape[:2]
    tb = min(S5_TB, l)

    def body(bu_ref, a_ref, h_ref, c_scr):
        @pl.when(pl.program_id(1) == 0)
        def _():
            c_scr[...] = jnp.zeros_like(c_scr)

        ar = a_ref[0]
        ai = a_ref[1]

        def step(t, carry):
            hr, hi = carry
            nr = ar * hr - ai * hi + bu_ref[t, 0]
            ni = ar * hi + ai * hr + bu_ref[t, 1]
            h_ref[t, 0] = nr
            h_ref[t, 1] = ni
            return nr, ni

        hr, hi = lax.fori_loop(0, tb, step, (c_scr[0], c_scr[1]), unroll=8)
        c_scr[0] = hr
        c_scr[1] = hi

    blk = pl.BlockSpec((None, tb, 2, S5_ROWS, LANES), lambda b, i: (b, i, 0, 0, 0))
    return pl.pallas_call(
        body, name="s5_scan_fwd", grid=(nb, l // tb),
        in_specs=[blk, pl.BlockSpec((2, S5_ROWS, LANES), lambda b, i: (0, 0, 0))],
        out_specs=blk, out_shape=jax.ShapeDtypeStruct(bu.shape, F32),
        scratch_shapes=[pltpu.VMEM((2, S5_ROWS, LANES), F32)],
        compiler_params=_params(("parallel", "arbitrary")),
    )(bu, abar)


def _s5_scan_bwd(dh, h, abar):
    nb, l = dh.shape[:2]
    tb = min(S5_TB, l)
    n_blk = l // tb

    def body(dh_ref, h_ref, a_ref, g_ref, da_ref, c_scr):
        @pl.when(pl.program_id(1) == 0)
        def _():
            c_scr[...] = jnp.zeros_like(c_scr)
            da_ref[...] = jnp.zeros_like(da_ref)

        ar = a_ref[0]
        ai = a_ref[1]

        def step(s, carry):
            gr, gi, dar, dai = carry
            t = tb - 1 - s
            hr = h_ref[t, 0]
            hi = h_ref[t, 1]
            dar = dar + gr * hr + gi * hi
            dai = dai + gi * hr - gr * hi
            nr = ar * gr + ai * gi + dh_ref[t, 0]
            ni = ar * gi - ai * gr + dh_ref[t, 1]
            g_ref[t, 0] = nr
            g_ref[t, 1] = ni
            return nr, ni, dar, dai

        z = jnp.zeros((S5_ROWS, LANES), F32)
        gr, gi, dar, dai = lax.fori_loop(0, tb, step, (c_scr[0], c_scr[1], z, z), unroll=8)
        c_scr[0] = gr
        c_scr[1] = gi
        da_ref[0] += dar
        da_ref[1] += dai

    blk = pl.BlockSpec((None, tb, 2, S5_ROWS, LANES), lambda b, i: (b, n_blk - 1 - i, 0, 0, 0))
    return pl.pallas_call(
        body, name="s5_scan_bwd", grid=(nb, n_blk),
        in_specs=[blk, blk, pl.BlockSpec((2, S5_ROWS, LANES), lambda b, i: (0, 0, 0))],
        out_specs=[blk, pl.BlockSpec((None, 2, S5_ROWS, LANES), lambda b, i: (b, 0, 0, 0))],
        out_shape=[jax.ShapeDtypeStruct(dh.shape, F32), jax.ShapeDtypeStruct((nb, 2, S5_ROWS, LANES), F32)],
        scratch_shapes=[pltpu.VMEM((2, S5_ROWS, LANES), F32)],
        compiler_params=_params(("parallel", "arbitrary")),
    )(dh, h, abar)


_GELU_C = math.sqrt(2.0 / math.pi)


def _gelu(x):
    return 0.5 * x * (1.0 + jnp.tanh(_GELU_C * (x + 0.044715 * x * x * x)))


def _gelu_grad(x):
    t = jnp.tanh(_GELU_C * (x + 0.044715 * x * x * x))
    return 0.5 * (1.0 + t) + 0.5 * x * (1.0 - t * t) * _GELU_C * (1.0 + 3.0 * 0.044715 * x * x)


def _s5_post_fwd(yc, u, d, w_glu):
    t, w = yc.shape
    tt = _tile(t, (512, 256, 128))

    def body(yc_ref, u_ref, d_ref, w_ref, o_ref, y0_ref):
        y0 = yc_ref[...] + d_ref[...] * u_ref[...]
        y = _gelu(y0)
        z = _dot(y, w_ref[...])
        o_ref[...] = (y * _sigmoid(z)).astype(o_ref.dtype)
        y0_ref[...] = y0

    tok = pl.BlockSpec((tt, w), lambda i: (i, 0))
    return pl.pallas_call(
        body, name="s5_post_fwd", grid=(t // tt,),
        in_specs=[tok, tok, pl.BlockSpec((1, w), lambda i: (0, 0)), pl.BlockSpec((w, w), lambda i: (0, 0))],
        out_specs=[tok, tok],
        out_shape=[jax.ShapeDtypeStruct((t, w), BF16), jax.ShapeDtypeStruct((t, w), F32)],
        compiler_params=_params(("parallel",)),
    )(yc, u, d, w_glu)


def _s5_post_bwd(y0, u, d, w_glu, dout):
    t, w = y0.shape
    tt = _tile(t, (512, 256, 128))

    def body(y0_ref, u_ref, d_ref, w_ref, do_ref, dy0_ref, y_ref, dz_ref, du_ref, dd_ref):
        @pl.when(pl.program_id(0) == 0)
        def _():
            dd_ref[...] = jnp.zeros_like(dd_ref)

        y0 = y0_ref[...]
        y = _gelu(y0)
        s = _sigmoid(_dot(y, w_ref[...]))
        do = do_ref[...]
        dz = do * y * s * (1.0 - s)
        dy = do * s + _dot_nt(dz, w_ref[...])
        dy0 = dy * _gelu_grad(y0)
        dy0_ref[...] = dy0.astype(dy0_ref.dtype)
        y_ref[...] = y.astype(y_ref.dtype)
        dz_ref[...] = dz.astype(dz_ref.dtype)
        du_ref[...] = dy0 * d_ref[...]
        dd_ref[...] += jnp.sum(dy0 * u_ref[...], axis=0, keepdims=True)

    tok = pl.BlockSpec((tt, w), lambda i: (i, 0))
    vec = pl.BlockSpec((1, w), lambda i: (0, 0))
    return pl.pallas_call(
        body, name="s5_post_bwd", grid=(t // tt,),
        in_specs=[tok, tok, vec, pl.BlockSpec((w, w), lambda i: (0, 0)), tok],
        out_specs=[tok, tok, tok, tok, vec],
        out_shape=[jax.ShapeDtypeStruct((t, w), BF16), jax.ShapeDtypeStruct((t, w), BF16),
                   jax.ShapeDtypeStruct((t, w), BF16), jax.ShapeDtypeStruct((t, w), F32),
                   jax.ShapeDtypeStruct((1, w), F32)],
        compiler_params=_params(("arbitrary",)),
    )(y0, u, d, w_glu, dout)


def _s5_params(a_re, a_im, b_re, b_im, c_re, c_im, log_dt):
    dt = jnp.exp(log_dt)[:, None]
    mag = jnp.exp(dt * a_re)
    ang = dt * a_im
    abar_re = mag * jnp.cos(ang)
    abar_im = mag * jnp.sin(ang)
    den = a_re * a_re + a_im * a_im
    zr = abar_re - 1.0
    zi = abar_im
    coef_re = ((zr * a_re + zi * a_im) / den)[..., None]
    coef_im = ((zi * a_re - zr * a_im) / den)[..., None]
    bb_re = coef_re * b_re - coef_im * b_im
    bb_im = coef_re * b_im + coef_im * b_re
    eye = jnp.eye(S5_GROUPS, dtype=F32)

    def dense_in(bb):
        return jnp.einsum('gnp,gh->gphn', bb, eye).reshape(S5_WIDTH, S5_NSTATE)

    def dense_out(cc):
        return jnp.einsum('gpn,gh->gnhp', cc, eye).reshape(S5_NSTATE, S5_WIDTH)

    abar = jnp.stack([abar_re.reshape(S5_ROWS, LANES), abar_im.reshape(S5_ROWS, LANES)])
    bbd = jnp.concatenate([dense_in(bb_re), dense_in(bb_im)], axis=1)
    ccd = jnp.concatenate([dense_out(c_re), dense_out(-c_im)], axis=0)
    return abar, bbd, ccd


def _tok_tile(t):
    return _tile(t, (256, 128))


def _rms_fwd(x, w):
    t, d = x.shape
    tt = _tok_tile(t)

    def body(x_ref, w_ref, o_ref):
        xv = x_ref[...]
        r = lax.rsqrt(jnp.mean(xv * xv, axis=-1, keepdims=True) + NORM_EPS)
        o_ref[...] = (xv * r * w_ref[...]).astype(o_ref.dtype)

    tok = pl.BlockSpec((tt, d), lambda i: (i, 0))
    return pl.pallas_call(
        body, name="rms_fwd", grid=(t // tt,), in_specs=[tok, pl.BlockSpec((1, d), lambda i: (0, 0))],
        out_specs=tok, out_shape=jax.ShapeDtypeStruct((t, d), BF16), compiler_params=_params(("parallel",)),
    )(x, w)


def _rms_bwd_math(xv, wv, dy):
    r = lax.rsqrt(jnp.mean(xv * xv, axis=-1, keepdims=True) + NORM_EPS)
    xh = xv * r
    dxh = dy * wv
    dx = r * (dxh - xh * jnp.mean(dxh * xh, axis=-1, keepdims=True))
    return dx, jnp.sum(dy * xh, axis=0, keepdims=True)


def _rms_bwd(x, w, dy, dres):
    t, d = x.shape
    tt = _tok_tile(t)

    def body(x_ref, w_ref, dy_ref, dr_ref, dx_ref, dw_ref):
        @pl.when(pl.program_id(0) == 0)
        def _():
            dw_ref[...] = jnp.zeros_like(dw_ref)

        dx, dw = _rms_bwd_math(x_ref[...], w_ref[...], dy_ref[...])
        dx_ref[...] = dr_ref[...] + dx
        dw_ref[...] += dw

    tok = pl.BlockSpec((tt, d), lambda i: (i, 0))
    vec = pl.BlockSpec((1, d), lambda i: (0, 0))
    return pl.pallas_call(
        body, name="rms_bwd", grid=(t // tt,), in_specs=[tok, vec, tok, tok], out_specs=[tok, vec],
        out_shape=[jax.ShapeDtypeStruct((t, d), F32), jax.ShapeDtypeStruct((1, d), F32)],
        compiler_params=_params(("arbitrary",)),
    )(x, w, dy, dres)


def _loss_head(x, w, target):
    t, d = x.shape
    tt = _tok_tile(t)

    def body(x_ref, w_ref, t_ref, l_ref, dx_ref, dw_ref):
        @pl.when(pl.program_id(0) == 0)
        def _():
            l_ref[...] = jnp.zeros_like(l_ref)
            dw_ref[...] = jnp.zeros_like(dw_ref)

        xv = x_ref[...]
        wv = w_ref[...]
        r = lax.rsqrt(jnp.mean(xv * xv, axis=-1, keepdims=True) + NORM_EPS)
        err = xv * r * wv - t_ref[...]
        row = jnp.sum(err * err, axis=-1, keepdims=True)
        l_ref[...] += (0.5 / d) * jnp.sum(row, axis=0, keepdims=True)
        dx, dw = _rms_bwd_math(xv, wv, err * (1.0 / d))
        dx_ref[...] = dx
        dw_ref[...] += dw

    tok = pl.BlockSpec((tt, d), lambda i: (i, 0))
    vec = pl.BlockSpec((1, d), lambda i: (0, 0))
    return pl.pallas_call(
        body, name="loss_head", grid=(t // tt,), in_specs=[tok, vec, tok],
        out_specs=[pl.BlockSpec((1, 1), lambda i: (0, 0)), tok, vec],
        out_shape=[jax.ShapeDtypeStruct((1, 1), F32), jax.ShapeDtypeStruct((t, d), F32), jax.ShapeDtypeStruct((1, d), F32)],
        compiler_params=_params(("arbitrary",)),
    )(x, w, target)


FF_PAD = 3072
FF_SHARD = D_FF // N_DEV
FF_SHARD_PAD = FF_PAD // N_DEV


def _ffn_in(x, nw, wg_t, wu_t):
    t, d = x.shape
    tm = _tile(t, (512, 256, 128))
    tn = 1024

    def body(x_ref, nw_ref, wg_ref, wu_ref, h_ref, g_ref, u_ref, a_ref):
        @pl.when(pl.program_id(1) == 0)
        def _():
            xv = x_ref[...]
            r = lax.rsqrt(jnp.mean(xv * xv, axis=-1, keepdims=True) + NORM_EPS)
            h_ref[...] = (xv * r * nw_ref[...]).astype(h_ref.dtype)

        h = h_ref[...]
        g = _dot_nt(h, wg_ref[...])
        u = _dot_nt(h, wu_ref[...])
        g_ref[...] = g.astype(g_ref.dtype)
        u_ref[...] = u.astype(u_ref.dtype)
        a_ref[...] = (g * _sigmoid(g) * u).astype(a_ref.dtype)

    tok = pl.BlockSpec((tm, d), lambda i, j: (i, 0))
    wsp = pl.BlockSpec((tn, d), lambda i, j: (j, 0))
    wide = pl.BlockSpec((tm, tn), lambda i, j: (i, j))
    return pl.pallas_call(
        body, name="ffn_in", grid=(t // tm, FF_PAD // tn),
        in_specs=[tok, pl.BlockSpec((1, d), lambda i, j: (0, 0)), wsp, wsp], out_specs=[tok, wide, wide, wide],
        out_shape=[jax.ShapeDtypeStruct((t, d), BF16)] + [jax.ShapeDtypeStruct((t, FF_PAD), BF16)] * 3,
        compiler_params=_params(("parallel", "arbitrary")),
    )(x, nw, wg_t, wu_t)


def _ffn_out_dx(dxo, wd, g, u):
    t, d = dxo.shape
    tm = _tile(t, (512, 256, 128))
    tn = 1024

    def body(dx_ref, wd_ref, g_ref, u_ref, dg_ref, du_ref):
        da = 0.5 * _dot_nt(dx_ref[...], wd_ref[...])
        g = g_ref[...].astype(F32)
        s = _sigmoid(g)
        dg_ref[...] = (da * u_ref[...].astype(F32) * (s * (1.0 + g * (1.0 - s)))).astype(dg_ref.dtype)
        du_ref[...] = (da * g * s).astype(du_ref.dtype)

    wide = pl.BlockSpec((tm, tn), lambda j, i: (i, j))
    return pl.pallas_call(
        body, name="ffn_out_dx", grid=(FF_PAD // tn, t // tm),
        in_specs=[pl.BlockSpec((tm, d), lambda j, i: (i, 0)), pl.BlockSpec((tn, d), lambda j, i: (j, 0)), wide, wide],
        out_specs=[wide, wide], out_shape=[jax.ShapeDtypeStruct((t, FF_PAD), BF16)] * 2,
        compiler_params=_params(("parallel", "parallel")),
    )(dxo, wd, g, u)


def _ffn_in_dx(dg, du, wg_t, wu_t, x, nw, dxo):
    t, d = x.shape
    tm = _tile(t, (256, 128))

    def body(dg_ref, du_ref, wg_ref, wu_ref, x_ref, nw_ref, dr_ref, dx_ref, dw_ref):
        @pl.when(pl.program_id(0) == 0)
        def _():
            dw_ref[...] = jnp.zeros_like(dw_ref)

        dh = _dot(dg_ref[...], wg_ref[...]) + _dot(du_ref[...], wu_ref[...])
        dx, dw = _rms_bwd_math(x_ref[...], nw_ref[...], dh)
        dx_ref[...] = dr_ref[...] + dx
        dw_ref[...] += dw

    wide = pl.BlockSpec((tm, FF_PAD), lambda i: (i, 0))
    wsp = pl.BlockSpec((FF_PAD, d), lambda i: (0, 0))
    tok = pl.BlockSpec((tm, d), lambda i: (i, 0))
    vec = pl.BlockSpec((1, d), lambda i: (0, 0))
    return pl.pallas_call(
        body, name="ffn_in_dx", grid=(t // tm,), in_specs=[wide, wide, wsp, wsp, tok, vec, tok], out_specs=[tok, vec],
        out_shape=[jax.ShapeDtypeStruct((t, d), F32), jax.ShapeDtypeStruct((1, d), F32)],
        compiler_params=_params(("arbitrary",)),
    )(dg, du, wg_t, wu_t, x, nw, dxo)


def _adamw(w, g, m, v):
    r, c = w.shape
    tr = _tile(r, (512, 256, 128, 64, 32, 16, 8))
    c1 = 1.0 - ADAM_B1 ** ADAM_STEP
    c2 = 1.0 - ADAM_B2 ** ADAM_STEP

    def body(w_ref, g_ref, m_ref, v_ref, d_ref, nm_ref, nv_ref):
        gv = g_ref[...]
        nm = ADAM_B1 * m_ref[...] + (1.0 - ADAM_B1) * gv
        nv = ADAM_B2 * v_ref[...] + (1.0 - ADAM_B2) * (gv * gv)
        d_ref[...] = -ADAM_LR * ((nm / c1) / (jnp.sqrt(nv / c2) + ADAM_EPS) + ADAM_WD * w_ref[...])
        nm_ref[...] = nm
        nv_ref[...] = nv

    blk = pl.BlockSpec((tr, c), lambda i: (i, 0))
    return pl.pallas_call(
        body, name="adamw", grid=(r // tr,), in_specs=[blk] * 4, out_specs=[blk] * 3,
        out_shape=[jax.ShapeDtypeStruct((r, c), F32)] * 3, compiler_params=_params(("parallel",)),
    )(w, g, m, v)


def _ffn_slot(which, kind, layer):
    return (which * 3 + kind) * DEPTH + layer


def _ffn_fwd(x, nw, w_ffn, which, layer):
    wg, wu, wd = (w_ffn[_ffn_slot(which, kind, layer)] for kind in range(3))
    h, g, u, a = _ffn_in(x, nw, wg, wu)
    return _mm(a, wd, name="ffn_out", res=x, scale=0.5), (x, h, g, u, a)


def _ffn_bwd(dxo, res, nw, w_ffn, g_ffn, which, layer):
    x, h, g, u, a = res
    wg, wu, wd = (w_ffn[_ffn_slot(which, kind, layer)] for kind in range(3))
    dg, du = _ffn_out_dx(dxo, wd, g, u)
    g_ffn = _mm_tn(a, dxo, name="ffn_out_dw", scale=0.5, into=g_ffn, slot=_ffn_slot(which, 2, layer))
    g_ffn = _mm_tn(dg, h, name="ffn_in_dw", into=g_ffn, slot=_ffn_slot(which, 0, layer))
    g_ffn = _mm_tn(du, h, name="ffn_in_dw", into=g_ffn, slot=_ffn_slot(which, 1, layer))
    dx, dnw = _ffn_in_dx(dg, du, wg, wu, x, nw, dxo)
    return dx, dnw, g_ffn


def _hg_lower_bounds(logits):
    p = jax.nn.softmax(logits, axis=0)
    return jnp.cumsum(p, axis=0) - p[0]


def _even_fwd(x1, p, nb, l):
    t = nb * l
    h = _rms_fwd(x1, p["mix_norm"])
    proj = _mm(h, p["w_in_t"], name="ev_in", nt=True)
    proj3 = proj.reshape(nb, l, EV_IN)
    ya, hst = _hgrn2_fwd(proj3, p["lb"], p["hg_nw"])
    u = proj[:, 4 * HG_QK:]
    bu = _mm(u, p["bbd"], name="s5_in")
    hs = _s5_scan_fwd(bu.reshape(nb, l, 2, S5_ROWS, LANES), p["abar"])
    yc = _mm(hs.reshape(t, 2 * S5_NSTATE), p["ccd"], name="s5_out")
    yb, y0 = _s5_post_fwd(yc, u, p["s5_d"], p["w_glu"])
    ycat = jnp.concatenate([ya.reshape(t, HG_QK), yb], axis=1)
    x2 = _mm(ycat, p["w_out"], name="mix_out", res=x1)
    return x2, (x1, h, proj3, hst, u, hs, y0, ycat)


def _even_bwd(dx2, res, p, nb, l):
    x1, h, proj3, hst, u, hs, y0, ycat = res
    t = nb * l
    dycat = _mm(dx2, p["w_out"], name="mix_out_dx", nt=True)
    dw_out = _mm_tn(ycat, dx2, name="mix_out_dw", out_dtype=BF16)
    dq, df, di, dg, dlb, dhg_nw = _hgrn2_bwd(proj3, p["lb"], p["hg_nw"], hst, dycat.reshape(nb, l, D_MODEL))
    dy0, y, dz, du_d, dd = _s5_post_bwd(y0, u, p["s5_d"], p["w_glu"], dycat[:, HG_QK:])
    dw_glu = _mm_tn(y, dz, name="s5_glu_dw", out_dtype=BF16)
    dhs = _mm(dy0, p["ccd"], name="s5_out_dx", nt=True)
    dccd = _mm_tn(hs.reshape(t, 2 * S5_NSTATE), dy0, name="s5_out_dw")
    g5, dabar = _s5_scan_bwd(dhs.reshape(nb, l, 2, S5_ROWS, LANES), hs, p["abar"])
    g2 = g5.reshape(t, 2 * S5_NSTATE)
    du = _mm(g2, p["bbd"], name="s5_in_dx", nt=True, res=du_d)
    dbbd = _mm_tn(u, g2, name="s5_in_dw")
    dproj = jnp.concatenate([d.reshape(t, HG_QK).astype(BF16) for d in (dq, df, di, dg)] + [du.astype(BF16)], axis=1)
    dw_in_t = _mm_tn(dproj, h, name="ev_in_dw", out_dtype=BF16)
    dh = _mm(dproj, p["w_in_t"], name="ev_in_dx")
    dx1, dmix = _rms_bwd(x1, p["mix_norm"], dh, dx2)
    small = dict(mix_norm=dmix, lb=dlb.sum(0).reshape(HG_QK), hg_nw=dhg_nw.reshape(HEAD_DIM),
                 abar=dabar.sum(0), bbd=dbbd, ccd=dccd, s5_d=dd.reshape(S5_WIDTH))
    return dx1, dict(w_in_t=dw_in_t, w_glu=dw_glu, w_out=dw_out), small


def _tok_major(xs):
    return jnp.transpose(xs, (0, 2, 1))[..., None]


def _odd_fwd(x1, p, nb, l):
    t = nb * l
    h = _rms_fwd(x1, p["mix_norm"])
    proj3 = _mm(h, p["w_in"], name="od_in").reshape(nb, l, OD_IN_PAD)
    qkv = _gdn_pre_fwd(proj3, p["conv_w"])
    n_wide = GDN_QKV + GDN_HEADS * HEAD_DIM
    bl = _tok_major(proj3[..., n_wide:n_wide + GDN_HEADS])
    al = _tok_major(proj3[..., n_wide + GDN_HEADS:n_wide + 2 * GDN_HEADS])
    y, st = _gdn_fwd(qkv, proj3, bl, al, p["a_log"], p["dt_bias"], p["gdn_nw"])
    y2 = y.reshape(t, D_MODEL)
    x2 = _mm(y2, p["w_out"], name="mix_out", res=x1)
    return x2, (x1, h, proj3, qkv, bl, al, st, y2)


def _odd_bwd(dx2, res, p, nb, l):
    x1, h, proj3, qkv, bl, al, st, y2 = res
    t = nb * l
    dy = _mm(dx2, p["w_out"], name="mix_out_dx", nt=True)
    dw_out = _mm_tn(y2, dx2, name="mix_out_dw", out_dtype=BF16)
    dq, dk, dv, dgate, dbl, dal, dalog, ddtb, dnw = _gdn_bwd(
        qkv, proj3, bl, al, p["a_log"], p["dt_bias"], p["gdn_nw"], st, dy.reshape(nb, l, D_MODEL))
    dx_qkv, dcw = _gdn_pre_bwd(proj3, p["conv_w"], jnp.concatenate([dq, dk, dv], axis=-1))
    dsmall = jnp.concatenate([jnp.transpose(d[..., 0], (0, 2, 1)) for d in (dbl, dal)], axis=-1)
    pad = jnp.zeros((nb, l, OD_IN_PAD - OD_IN), BF16)
    dproj = jnp.concatenate([dx_qkv.astype(BF16), dgate.astype(BF16), dsmall.astype(BF16), pad], axis=-1)
    dproj = dproj.reshape(t, OD_IN_PAD)
    dw_in = _mm_tn(h, dproj, name="od_in_dw", out_dtype=BF16)[:, :OD_IN]
    dh = _mm(dproj, p["w_in"], name="od_in_dx", nt=True)
    dx1, dmix = _rms_bwd(x1, p["mix_norm"], dh, dx2)
    small = dict(mix_norm=dmix, a_log=dalog.sum(0).reshape(GDN_HEADS), dt_bias=ddtb.sum(0).reshape(GDN_HEADS),
                 gdn_nw=dnw.reshape(HEAD_DIM))
    return dx1, dict(w_in=dw_in, conv_w=dcw.sum(0), w_out=dw_out), small


def _local_step(x, target, big, small):
    nb, l, _ = x.shape
    t = nb * l
    lbs, lbs_vjp = jax.vjp(_hg_lower_bounds, small["hg_lb_logits"])
    row = lambda v: v.reshape(1, -1)
    layers = []
    s5_vjps = []
    for layer in range(DEPTH):
        j = layer // 2
        p = dict(mix_norm=row(small["mix_norm"][layer]))
        if layer % 2 == 0:
            (abar, bbd, ccd), s5_vjp = jax.vjp(
                _s5_params, small["s5_a_re"][j], small["s5_a_im"][j], small["s5_b_re"][j], small["s5_b_im"][j],
                small["s5_c_re"][j], small["s5_c_im"][j], small["s5_log_dt"][j])
            s5_vjps.append(s5_vjp)
            p.update(w_in_t=big["ev_w_in_t"][j], w_glu=big["s5_w_glu"][j], w_out=big["w_out"][j],
                     lb=lbs[j].reshape(HG_HEADS, 1, HEAD_DIM), hg_nw=row(small["hg_norm_w"][j]),
                     abar=abar, bbd=bbd.astype(BF16), ccd=ccd.astype(BF16), s5_d=row(small["s5_d"][j]))
        else:
            p.update(w_in=big["od_w_in"][j], conv_w=big["gdn_conv_w"][j], w_out=big["w_out"][2 + j],
                     a_log=small["gdn_a_log"][j].reshape(GDN_HEADS, 1, 1),
                     dt_bias=small["gdn_dt_bias"][j].reshape(GDN_HEADS, 1, 1), gdn_nw=row(small["gdn_norm_w"][j]))
        layers.append(p)

    xs = x.reshape(t, D_MODEL)
    saved = []
    for layer in range(DEPTH):
        xs, r1 = _ffn_fwd(xs, row(small["ffn1_norm"][layer]), big["ffn"], 0, layer)
        xs, r2 = (_even_fwd if layer % 2 == 0 else _odd_fwd)(xs, layers[layer], nb, l)
        xs, r3 = _ffn_fwd(xs, row(small["ffn2_norm"][layer]), big["ffn"], 1, layer)
        saved.append((r1, r2, r3))
    loss, dx, dfinal = _loss_head(xs, row(small["final_norm"]), target.reshape(t, D_MODEL))

    g_ffn = lax.empty(big["ffn"].shape, BF16)
    gb = {k: [None] * 2 for k in ("ev_w_in_t", "s5_w_glu", "ev_w_out", "od_w_in", "gdn_conv_w", "od_w_out")}
    gs = {k: [None] * DEPTH for k in ("ffn1_norm", "mix_norm", "ffn2_norm")}
    gs.update({k: [None] * 2 for k in ("hg_norm_w", "s5_a_re", "s5_a_im", "s5_b_re", "s5_b_im", "s5_c_re", "s5_c_im",
                                       "s5_d", "s5_log_dt", "gdn_a_log", "gdn_dt_bias", "gdn_norm_w")})
    dlbs = [None] * 2
    for layer in reversed(range(DEPTH)):
        j = layer // 2
        p = layers[layer]
        r1, r2, r3 = saved[layer]
        dx, dnw, g_ffn = _ffn_bwd(dx, r3, row(small["ffn2_norm"][layer]), big["ffn"], g_ffn, 1, layer)
        gs["ffn2_norm"][layer] = dnw[0]
        if layer % 2 == 0:
            dx, gw, sm = _even_bwd(dx, r2, p, nb, l)
            gb["ev_w_in_t"][j], gb["s5_w_glu"][j], gb["ev_w_out"][j] = gw["w_in_t"], gw["w_glu"], gw["w_out"]
            dlbs[j] = sm["lb"]
            gs["hg_norm_w"][j] = sm["hg_nw"]
            gs["s5_d"][j] = sm["s5_d"]
            (gs["s5_a_re"][j], gs["s5_a_im"][j], gs["s5_b_re"][j], gs["s5_b_im"][j], gs["s5_c_re"][j],
             gs["s5_c_im"][j], gs["s5_log_dt"][j]) = s5_vjps[j]((sm["abar"], sm["bbd"], sm["ccd"]))
        else:
            dx, gw, sm = _odd_bwd(dx, r2, p, nb, l)
            gb["od_w_in"][j], gb["gdn_conv_w"][j], gb["od_w_out"][j] = gw["w_in"], gw["conv_w"], gw["w_out"]
            gs["gdn_a_log"][j], gs["gdn_dt_bias"][j], gs["gdn_norm_w"][j] = sm["a_log"], sm["dt_bias"], sm["gdn_nw"]
        gs["mix_norm"][layer] = sm["mix_norm"][0]
        dx, dnw, g_ffn = _ffn_bwd(dx, r1, row(small["ffn1_norm"][layer]), big["ffn"], g_ffn, 0, layer)
        gs["ffn1_norm"][layer] = dnw[0]
    gbig = dict(ffn=g_ffn, ev_w_in_t=jnp.stack(gb["ev_w_in_t"]), s5_w_glu=jnp.stack(gb["s5_w_glu"]),
                w_out=jnp.stack(gb["ev_w_out"] + gb["od_w_out"]), od_w_in=jnp.stack(gb["od_w_in"]),
                gdn_conv_w=jnp.stack(gb["gdn_conv_w"]))
    gsmall = {k: jnp.stack(v) for k, v in gs.items()}
    gsmall["hg_lb_logits"] = lbs_vjp(jnp.stack(dlbs))[0]
    gsmall["final_norm"] = dfinal[0]
    return loss[0, 0], dx.reshape(nb, l, D_MODEL), gbig, gsmall


def _here():
    return lax.axis_index("x"), lax.axis_index("y"), lax.axis_index("c")


def _other_chips(x, y):
    return [(1 - x, y), (x, 1 - y), (1 - x, 1 - y)]


def _all_gather(blocks, name):
    n = len(blocks)

    def body(*refs):
        x_refs, o_refs = refs[:n], refs[n:2 * n]
        send_sems, recv_sems, local_sems = refs[2 * n:]
        x, y, cc = _here()
        me, sibling = (x, y, cc), (x, y, 1 - cc)
        chips = _other_chips(x, y)

        def win(i, p):
            return o_refs[i].at[:, 4 * p[0] + 2 * p[1] + p[2]]

        def copy(i, k, blk, to, src=None):
            return pltpu.make_async_remote_copy(
                src_ref=win(i, blk) if src is None else src, dst_ref=win(i, blk),
                send_sem=send_sems.at[i, k], recv_sem=recv_sems.at[i, k], device_id=to, device_id_type=MESH)

        mine = [pltpu.make_async_copy(x_refs[i], win(i, me), local_sems.at[i]) for i in range(n)]
        for cp in mine:
            cp.start()
        first = []
        for i in range(n):
            first.append(copy(i, 0, me, sibling, src=x_refs[i]))
            first += [copy(i, 1 + j, me, (*chip, cc), src=x_refs[i]) for j, chip in enumerate(chips)]
        for cp in first:
            cp.start()
        passed = []
        for j, chip in enumerate(chips):
            for i in range(n):
                copy(i, 1 + j, (*chip, cc), me).wait_recv()
                fwd = copy(i, 4 + j, (*chip, cc), sibling)
                fwd.start()
                passed.append(fwd)
        for i in range(n):
            copy(i, 0, sibling, me).wait_recv()
            for j, chip in enumerate(chips):
                copy(i, 4 + j, (*chip, 1 - cc), me).wait_recv()
        for cp in first + passed:
            cp.wait_send()
        for cp in mine:
            cp.wait()

    hbm = pl.BlockSpec(memory_space=pl.ANY)
    return pl.pallas_call(
        body, name=name, in_specs=[hbm] * n, out_specs=[hbm] * n,
        out_shape=[jax.ShapeDtypeStruct((b.shape[0], N_DEV) + b.shape[1:], b.dtype) for b in blocks],
        scratch_shapes=[pltpu.SemaphoreType.DMA((n, 7)), pltpu.SemaphoreType.DMA((n, 7)), pltpu.SemaphoreType.DMA((n,))],
    )(*blocks)


def _rs_core_exchange(gs):
    n = len(gs)

    def body(*refs):
        g_refs, o_refs = refs[:n], refs[n:2 * n]
        send_sems, recv_sems = refs[2 * n:]
        x, y, cc = _here()
        cps = [pltpu.make_async_remote_copy(
            src_ref=g_refs[i].at[:, 2 * j + (1 - cc)], dst_ref=o_refs[i].at[j],
            send_sem=send_sems.at[i, j], recv_sem=recv_sems.at[i, j], device_id=(x, y, 1 - cc), device_id_type=MESH)
            for i in range(n) for j in range(4)]
        for cp in cps:
            cp.start()
        for cp in cps:
            cp.wait()

    hbm = pl.BlockSpec(memory_space=pl.ANY)
    return pl.pallas_call(
        body, name="rs_core_exchange", in_specs=[hbm] * n, out_specs=[hbm] * n,
        out_shape=[jax.ShapeDtypeStruct((4, g.shape[0]) + g.shape[2:], g.dtype) for g in gs],
        scratch_shapes=[pltpu.SemaphoreType.DMA((n, 4)), pltpu.SemaphoreType.DMA((n, 4))],
    )(*gs)


def _rs_chip_exchange(ps):
    n = len(ps)

    def body(*refs):
        p_refs, o_refs = refs[:n], refs[n:2 * n]
        send_sems, recv_sems = refs[2 * n:]
        x, y, cc = _here()
        cps = [pltpu.make_async_remote_copy(
            src_ref=p_refs[i].at[2 * px + py], dst_ref=o_refs[i].at[k],
            send_sem=send_sems.at[i, k], recv_sem=recv_sems.at[i, k], device_id=(px, py, cc), device_id_type=MESH)
            for i in range(n) for k, (px, py) in enumerate(_other_chips(x, y))]
        for cp in cps:
            cp.start()
        for cp in cps:
            cp.wait()

    hbm = pl.BlockSpec(memory_space=pl.ANY)
    return pl.pallas_call(
        body, name="rs_chip_exchange", in_specs=[hbm] * n, out_specs=[hbm] * n,
        out_shape=[jax.ShapeDtypeStruct((3,) + p.shape[1:], p.dtype) for p in ps],
        scratch_shapes=[pltpu.SemaphoreType.DMA((n, 3)), pltpu.SemaphoreType.DMA((n, 3))],
    )(*ps)


def _row_tile(w):
    return w if w <= 512 else _tile(w, (512, 256, 128))


def _rs_chip_sum(g, r1, core):
    a, _, w, c = g.shape
    tr = _row_tile(w)

    def body(core_ref, g_ref, r_ref, o_ref):
        o_ref[...] = (g_ref[...].astype(F32) + r_ref[...].astype(F32)).astype(o_ref.dtype)

    blk = pl.BlockSpec((None, None, tr, c), lambda j, aa, i, core_ref: (j, aa, i, 0))
    return pl.pallas_call(
        body, name="rs_chip_sum",
        grid_spec=pltpu.PrefetchScalarGridSpec(
            num_scalar_prefetch=1, grid=(4, a, w // tr),
            in_specs=[pl.BlockSpec((None, None, tr, c), lambda j, aa, i, core_ref: (aa, 2 * j + core_ref[0], i, 0)), blk],
            out_specs=blk),
        out_shape=jax.ShapeDtypeStruct((4, a, w, c), g.dtype),
        compiler_params=_params(("parallel", "parallel", "parallel")),
    )(core, g, r1)


def _rs_final_sum(g, r1, r2, place):
    a, _, w, c = g.shape
    tr = _row_tile(w)

    def body(place_ref, g_ref, r1_ref, r2_ref, o_ref):
        s = g_ref[...].astype(F32) + r1_ref[...].astype(F32)
        for k in range(3):
            s = s + r2_ref[k].astype(F32)
        o_ref[...] = s

    return pl.pallas_call(
        body, name="rs_final_sum",
        grid_spec=pltpu.PrefetchScalarGridSpec(
            num_scalar_prefetch=1, grid=(a, w // tr),
            in_specs=[pl.BlockSpec((None, None, tr, c), lambda aa, i, place_ref: (aa, place_ref[0], i, 0)),
                      pl.BlockSpec((None, None, tr, c), lambda aa, i, place_ref: (place_ref[1], aa, i, 0)),
                      pl.BlockSpec((3, None, tr, c), lambda aa, i, place_ref: (0, aa, i, 0))],
            out_specs=pl.BlockSpec((None, tr, c), lambda aa, i, place_ref: (aa, i, 0))),
        out_shape=jax.ShapeDtypeStruct((a, w, c), F32),
        compiler_params=_params(("parallel", "parallel")),
    )(place, g, r1, r2)


def _sum_slots(g):
    _, n, r, c = g.shape

    def body(g_ref, o_ref):
        s = g_ref[0, 0]
        for k in range(1, n):
            s = s + g_ref[0, k]
        o_ref[...] = s

    return pl.pallas_call(
        body, name="sum_slots", grid=(1,),
        in_specs=[pl.BlockSpec((1, n, r, c), lambda i: (0, 0, 0, 0))], out_specs=pl.BlockSpec((r, c), lambda i: (0, 0)),
        out_shape=jax.ShapeDtypeStruct((r, c), F32), compiler_params=_params(("arbitrary",)),
    )(g)


_WEIGHTS = ['ffn1_norm', 'ffn1_w_gate', 'ffn1_w_up', 'ffn1_w_down', 'mix_norm', 'ffn2_norm', 'ffn2_w_gate', 'ffn2_w_up',
            'ffn2_w_down', 'ev_w_in', 'hg_lb_logits', 'hg_norm_w', 's5_a_re', 's5_a_im', 's5_b_re', 's5_b_im', 's5_c_re',
            's5_c_im', 's5_d', 's5_log_dt', 's5_w_glu', 'ev_w_out', 'od_w_in', 'gdn_conv_w', 'gdn_a_log', 'gdn_dt_bias',
            'gdn_norm_w', 'od_w_out', 'final_norm']
_SHARDED = ['ffn1_w_gate', 'ffn1_w_up', 'ffn1_w_down', 'ffn2_w_gate', 'ffn2_w_up', 'ffn2_w_down', 'ev_w_in', 's5_w_glu',
            'ev_w_out', 'od_w_in', 'gdn_conv_w', 'od_w_out']
_REPLICATED = [n for n in _WEIGHTS if n not in _SHARDED]
_SMALL_COLS = 1024


def _pad_rows(a, rows):
    return jnp.pad(a, ((0, 0), (0, rows - a.shape[1]), (0, 0)))


def _gather_weights(w):
    t_ = lambda a: jnp.swapaxes(a, 1, 2)
    ffn = []
    for f in ('ffn1', 'ffn2'):
        ffn += [_pad_rows(t_(w[f + '_w_gate']), FF_SHARD_PAD), _pad_rows(t_(w[f + '_w_up']), FF_SHARD_PAD),
                _pad_rows(w[f + '_w_down'], FF_SHARD_PAD)]
    blocks = [jnp.concatenate(ffn).astype(BF16),
              t_(w['ev_w_in']).astype(BF16),
              w['s5_w_glu'].astype(BF16),
              jnp.concatenate([w['ev_w_out'], w['od_w_out']]).astype(BF16),
              w['od_w_in'].astype(BF16).reshape(1, 2 * D_MODEL, OD_IN // N_DEV),
              w['gdn_conv_w'].reshape(1, 2 * CONV_W, GDN_QKV // N_DEV)]
    ffn_w, in_t, glu, w_out, od_in, conv = _all_gather(blocks, "gather_weights")
    whole = lambda g: g.reshape(g.shape[0], N_DEV * g.shape[2], g.shape[3])
    od_in = jnp.moveaxis(od_in.reshape(N_DEV, 2, D_MODEL, OD_IN // N_DEV), 0, 2).reshape(2, D_MODEL, OD_IN)
    conv = jnp.moveaxis(conv.reshape(N_DEV, 2, CONV_W, GDN_QKV // N_DEV), 0, 2).reshape(2, CONV_W, GDN_QKV)
    return dict(ffn=whole(ffn_w), ev_w_in_t=whole(in_t), s5_w_glu=whole(glu), w_out=whole(w_out),
                od_w_in=jnp.pad(od_in, ((0, 0), (0, 0), (0, OD_IN_PAD - OD_IN))), gdn_conv_w=conv)


def _scatter_grads(gbig):
    x, y, c = _here()
    windows = lambda g, w: g.reshape(g.shape[0], N_DEV, w, g.shape[2])
    od_in = jnp.moveaxis(gbig['od_w_in'].reshape(2, D_MODEL, N_DEV, OD_IN // N_DEV), 2, 0)
    conv = jnp.moveaxis(gbig['gdn_conv_w'].reshape(2, CONV_W, N_DEV, GDN_QKV // N_DEV), 2, 0)
    gs = [windows(gbig['ffn'], FF_SHARD_PAD), windows(gbig['ev_w_in_t'], EV_IN // N_DEV),
          windows(gbig['s5_w_glu'], S5_WIDTH // N_DEV), windows(gbig['w_out'], D_MODEL // N_DEV),
          od_in.reshape(1, N_DEV, 2 * D_MODEL, OD_IN // N_DEV), conv.reshape(1, N_DEV, 2 * CONV_W, GDN_QKV // N_DEV)]
    core = jnp.reshape(c, (1,)).astype(jnp.int32)
    place = jnp.stack([4 * x + 2 * y + c, 2 * x + y]).astype(jnp.int32)
    r1 = _rs_core_exchange(gs)
    ps = [_rs_chip_sum(g, r, core) for g, r in zip(gs, r1)]
    r2 = _rs_chip_exchange(ps)
    ffn, in_t, glu, w_out, od_in, conv = (_rs_final_sum(g, ra, rb, place) for g, ra, rb in zip(gs, r1, r2))
    t_ = lambda a: jnp.swapaxes(a, 1, 2)
    ffn = ffn[:, :FF_SHARD].reshape(2, 3, DEPTH, FF_SHARD, D_MODEL)
    out = {}
    for i, f in enumerate(('ffn1', 'ffn2')):
        out[f + '_w_gate'], out[f + '_w_up'], out[f + '_w_down'] = t_(ffn[i, 0]), t_(ffn[i, 1]), ffn[i, 2]
    out['ev_w_in'] = t_(in_t)
    out['s5_w_glu'] = glu
    out['ev_w_out'], out['od_w_out'] = w_out[:2], w_out[2:]
    out['od_w_in'] = od_in.reshape(2, D_MODEL, OD_IN // N_DEV)
    out['gdn_conv_w'] = conv.reshape(2, CONV_W, GDN_QKV // N_DEV)
    return out


def _flat_small(src, prefix=''):
    flat = jnp.concatenate([src[prefix + n].reshape(-1) for n in _REPLICATED])
    rows = -(-(flat.shape[0] + 1) // (_SMALL_COLS * 8)) * 8
    return jnp.pad(flat, (0, rows * _SMALL_COLS - flat.shape[0])).reshape(rows, _SMALL_COLS)


def _split_small(flat2d, like):
    flat, out, off = flat2d.reshape(-1), {}, 0
    for n in _REPLICATED:
        size = math.prod(like[n].shape)
        out[n] = flat[off:off + size].reshape(like[n].shape)
        off += size
    return out, flat[off]


def _all_reduce_small(gsmall, loss):
    flat = _flat_small(gsmall)
    n_used = sum(math.prod(gsmall[n].shape) for n in _REPLICATED)
    flat = flat.reshape(-1).at[n_used].set(loss).reshape(flat.shape)
    (gathered,) = _all_gather([flat[None]], "gather_small")
    return _split_small(_sum_slots(gathered), gsmall)


def _as_2d(a):
    return a.reshape(-1, a.shape[-1])


def kernel(x, ffn1_norm, ffn1_w_gate, ffn1_w_up, ffn1_w_down, mix_norm, ffn2_norm, ffn2_w_gate, ffn2_w_up, ffn2_w_down, ev_w_in, hg_lb_logits, hg_norm_w, s5_a_re, s5_a_im, s5_b_re, s5_b_im, s5_c_re, s5_c_im, s5_d, s5_log_dt, s5_w_glu, ev_w_out, od_w_in, gdn_conv_w, gdn_a_log, gdn_dt_bias, gdn_norm_w, od_w_out, final_norm, loss_target, m_ffn1_norm, m_ffn1_w_gate, m_ffn1_w_up, m_ffn1_w_down, m_mix_norm, m_ffn2_norm, m_ffn2_w_gate, m_ffn2_w_up, m_ffn2_w_down, m_ev_w_in, m_hg_lb_logits, m_hg_norm_w, m_s5_a_re, m_s5_a_im, m_s5_b_re, m_s5_b_im, m_s5_c_re, m_s5_c_im, m_s5_d, m_s5_log_dt, m_s5_w_glu, m_ev_w_out, m_od_w_in, m_gdn_conv_w, m_gdn_a_log, m_gdn_dt_bias, m_gdn_norm_w, m_od_w_out, m_final_norm, v_ffn1_norm, v_ffn1_w_gate, v_ffn1_w_up, v_ffn1_w_down, v_mix_norm, v_ffn2_norm, v_ffn2_w_gate, v_ffn2_w_up, v_ffn2_w_down, v_ev_w_in, v_hg_lb_logits, v_hg_norm_w, v_s5_a_re, v_s5_a_im, v_s5_b_re, v_s5_b_im, v_s5_c_re, v_s5_c_im, v_s5_d, v_s5_log_dt, v_s5_w_glu, v_ev_w_out, v_od_w_in, v_gdn_conv_w, v_gdn_a_log, v_gdn_dt_bias, v_gdn_norm_w, v_od_w_out, v_final_norm):
    given = dict(locals())
    w = {n: given[n] for n in _WEIGHTS}
    big = _gather_weights(w)
    small = {n: w[n] for n in _REPLICATED}
    loss_part, grad_x, gbig, gsmall = _local_step(given['x'], given['loss_target'], big, small)
    grads = _scatter_grads(gbig)
    gsum, loss = _all_reduce_small(gsmall, loss_part)
    grads.update(gsum)

    delta, new_m, new_v = {}, {}, {}
    for n in _SHARDED:
        d, nm, nv = _adamw(_as_2d(w[n]), _as_2d(grads[n]), _as_2d(given['m_' + n]), _as_2d(given['v_' + n]))
        delta[n], new_m[n], new_v[n] = (a.reshape(w[n].shape) for a in (d, nm, nv))
    d, nm, nv = _adamw(_flat_small(w), _flat_small(grads), _flat_small(given, 'm_'), _flat_small(given, 'v_'))
    (delta_s, _), (new_m_s, _), (new_v_s, _) = (_split_small(a, small) for a in (d, nm, nv))
    delta.update(delta_s)
    new_m.update(new_m_s)
    new_v.update(new_v_s)
    return (loss, grad_x, *[grads[n] for n in _WEIGHTS], *[delta[n] for n in _WEIGHTS],
            *[new_m[n] for n in _WEIGHTS], *[new_v[n] for n in _WEIGHTS])
```

```python
import math

import jax
import jax.numpy as jnp
from jax import lax
from jax.experimental import pallas as pl
from jax.experimental.pallas import tpu as pltpu

F32 = jnp.float32
BF16 = jnp.bfloat16
HI = lax.Precision.HIGH

D_MODEL = 1024
DEPTH = 4
D_FF = 2816
NORM_EPS = 1e-6
F_MIN = 1e-6
CHUNK = 64
HG_HEADS = 4
HEAD_DIM = 128
HG_QK = HG_HEADS * HEAD_DIM
S5_WIDTH = 512
S5_GROUP = 16
S5_GROUPS = 32
S5_STATE = 64
S5_NSTATE = S5_GROUPS * S5_STATE
EV_IN = 2560
GDN_HEADS = 8
GDN_QKV = 3 * GDN_HEADS * HEAD_DIM
CONV_W = 4
OD_IN = 4112
OD_IN_PAD = 4224
N_DEV = 8
LANES = 128
ADAM_LR, ADAM_B1, ADAM_B2, ADAM_EPS, ADAM_WD, ADAM_STEP = 0.001, 0.9, 0.999, 1e-08, 0.01, 10
VMEM_LIMIT = 56 * 1024 * 1024

MESH = pl.DeviceIdType.MESH


def _params(sem=None, **kw):
    return pltpu.CompilerParams(dimension_semantics=sem, vmem_limit_bytes=VMEM_LIMIT, **kw)


def _tile(n, cands):
    for c in cands:
        if n % c == 0:
            return c
    return n


def _dot(a, b):
    return jnp.dot(a.astype(BF16), b.astype(BF16), preferred_element_type=F32)


def _dot_nt(a, b):
    return lax.dot_general(a.astype(BF16), b.astype(BF16), (((1,), (1,)), ((), ())), preferred_element_type=F32)


def _dot_tn(a, b):
    return lax.dot_general(a.astype(BF16), b.astype(BF16), (((0,), (0,)), ((), ())), preferred_element_type=F32)


def _dot_hi(a, b):
    return jnp.dot(a, b, precision=HI, preferred_element_type=F32)


def _bmm(a, b):
    return jnp.einsum('gmk,gkn->gmn', a.astype(BF16), b.astype(BF16), preferred_element_type=F32)


def _bmm_nt(a, b):
    return jnp.einsum('gmk,gnk->gmn', a.astype(BF16), b.astype(BF16), preferred_element_type=F32)


def _bmm_tn(a, b):
    return jnp.einsum('gkm,gkn->gmn', a.astype(BF16), b.astype(BF16), preferred_element_type=F32)


def _bmm_hi(a, b):
    return jnp.einsum('gmk,gkn->gmn', a, b, precision=HI, preferred_element_type=F32)


_TN_CANDS = (1408, 1280, 1024, 512, 384, 256, 128)
_B_TILE_BYTES = 6 * 1024 * 1024


def _mm(a, b, *, name, nt=False, out_dtype=F32, res=None, scale=1.0):
    m, k = a.shape
    n = b.shape[0] if nt else b.shape[1]
    tm = _tile(m, (512, 256, 128))
    tn = _tile(n, [c for c in _TN_CANDS if c * k * b.dtype.itemsize <= _B_TILE_BYTES])

    def body(*refs):
        if res is None:
            a_ref, b_ref, o_ref = refs
        else:
            a_ref, b_ref, r_ref, o_ref = refs
        acc = (_dot_nt if nt else _dot)(a_ref[...], b_ref[...])
        if scale != 1.0:
            acc = scale * acc
        if res is not None:
            acc = r_ref[...] + acc
        o_ref[...] = acc.astype(out_dtype)

    b_spec = pl.BlockSpec((tn, k), lambda j, i: (j, 0)) if nt else pl.BlockSpec((k, tn), lambda j, i: (0, j))
    in_specs = [pl.BlockSpec((tm, k), lambda j, i: (i, 0)), b_spec]
    args = [a, b]
    if res is not None:
        in_specs.append(pl.BlockSpec((tm, tn), lambda j, i: (i, j)))
        args.append(res)
    return pl.pallas_call(
        body, name=name, grid=(n // tn, m // tm), in_specs=in_specs,
        out_specs=pl.BlockSpec((tm, tn), lambda j, i: (i, j)),
        out_shape=jax.ShapeDtypeStruct((m, n), out_dtype),
        compiler_params=_params(("parallel", "parallel")),
    )(*args)


def _mm_tn(a, b, *, name, scale=1.0, out_dtype=F32, into=None, slot=0):
    t, m = a.shape
    n = b.shape[1]
    tm = _tile(m, (1024, 512, 256, 128))
    tn = _tile(n, _TN_CANDS)
    tk = _tile(t, (2048, 1024, 512, 256, 128))
    nk = t // tk

    def body(*refs):
        a_ref, b_ref = refs[:2]
        o_ref, acc_ref = refs[-2:]
        kk = pl.program_id(2)

        @pl.when(kk == 0)
        def _():
            acc_ref[...] = jnp.zeros_like(acc_ref)

        acc_ref[...] += _dot_tn(a_ref[...], b_ref[...])

        @pl.when(kk == nk - 1)
        def _():
            o_ref[...] = (acc_ref[...] * scale if scale != 1.0 else acc_ref[...]).astype(o_ref.dtype)

    in_specs = [pl.BlockSpec((tk, tm), lambda i, j, kk: (kk, i)), pl.BlockSpec((tk, tn), lambda i, j, kk: (kk, j))]
    args = [a, b]
    if into is None:
        out_spec = pl.BlockSpec((tm, tn), lambda i, j, kk: (i, j))
        out_shape = jax.ShapeDtypeStruct((m, n), out_dtype)
        alias = {}
    else:
        in_specs.append(pl.BlockSpec(memory_space=pl.ANY))
        args.append(into)
        out_spec = pl.BlockSpec((None, tm, tn), lambda i, j, kk: (slot, i, j))
        out_shape = jax.ShapeDtypeStruct(into.shape, into.dtype)
        alias = {2: 0}
    return pl.pallas_call(
        body, name=name, grid=(m // tm, n // tn, nk), in_specs=in_specs, out_specs=out_spec, out_shape=out_shape,
        scratch_shapes=[pltpu.VMEM((tm, tn), F32)], input_output_aliases=alias,
        compiler_params=_params(("parallel", "parallel", "arbitrary")),
    )(*args)


def _sigmoid(x):
    return jax.nn.sigmoid(x)


def _head_norm_gate(o, gate, nw):
    r = lax.rsqrt(jnp.mean(o * o, axis=-1, keepdims=True) + NORM_EPS)
    return o * r * nw * (gate * _sigmoid(gate))


def _hg_chunk(st, ql, fl, iv, gl, lb, nw):
    n_g, c, _ = ql.shape
    q = ql * _sigmoid(ql)
    f = lb + (1.0 - lb) * _sigmoid(fl)
    lf = jnp.log(jnp.maximum(f, F_MIN))
    k = 1.0 - f
    ri = lax.broadcasted_iota(jnp.int32, (c, c), 0)
    ci = lax.broadcasted_iota(jnp.int32, (c, c), 1)
    rl = lax.broadcasted_iota(jnp.int32, (c, LANES), 0)
    each = lambda m2: jnp.broadcast_to(m2[None], (n_g,) + m2.shape)
    b = _bmm_hi(each(jnp.where(ci <= ri, 1.0, 0.0)), lf)
    attn = jnp.where((ri == ci)[None], _bmm_nt(q, k), 0.0)
    sh = 0
    while (1 << sh) < c:
        m = 1 << sh
        ref = ((ri >> (sh + 1)) << (sh + 1)) + (m - 1)
        low_r = ((ri >> sh) & 1) == 1
        low_c = ((ci >> sh) & 1) == 1
        w_low = low_r & (ci > ref) & (ci <= ri)
        w_up = jnp.logical_not(low_r) & (ci > ri) & (ci <= ref)
        w = jnp.where(w_low | w_up, 1.0, 0.0)
        e = jnp.exp(_bmm_hi(each(w), lf))
        low_l = (((rl >> sh) & 1) == 1)[None]
        qs = jnp.where(low_l, q * e, 0.0)
        ks = jnp.where(low_l, 0.0, k * e)
        pair = ((ri >> (sh + 1)) == (ci >> (sh + 1))) & low_r & jnp.logical_not(low_c)
        attn = attn + jnp.where(pair[None], _bmm_nt(qs, ks), 0.0)
        sh += 1
    bl = jnp.sum(lf, axis=1, keepdims=True)
    o = _bmm(attn, iv) + _bmm_nt(q * jnp.exp(b), st)
    st_new = st * jnp.exp(bl) + _bmm_tn(iv, k * jnp.exp(bl - b))
    return _head_norm_gate(o, gl, nw), st_new


def _hg_specs(nb, n_c, rev):
    def cidx(cc):
        return (n_c - 1 - cc) if rev else cc

    def col(width):
        return pl.BlockSpec((nb, CHUNK, width), lambda cc: (0, cidx(cc), 0))

    vec = pl.BlockSpec((HG_HEADS, 1, LANES), lambda cc: (0, 0, 0))
    nw = pl.BlockSpec((1, LANES), lambda cc: (0, 0))
    st = pl.BlockSpec((nb, HG_HEADS, None, LANES, LANES), lambda cc: (0, 0, cidx(cc), 0, 0))
    acc = pl.BlockSpec((nb, HG_HEADS, 1, LANES), lambda cc: (0, 0, 0, 0))
    return col, vec, nw, st, acc


def _head(j):
    return slice(j * LANES, (j + 1) * LANES)


def _seq_heads(ref, nb, n, first=0):
    return jnp.stack([ref[b, :, _head(first + j)] for b in range(nb) for j in range(n)])


def _put_seq_heads(ref, val, nb, n, first=0):
    for b in range(nb):
        for j in range(n):
            ref[b, :, _head(first + j)] = val[b * n + j].astype(ref.dtype)


def _hgrn2_fwd(proj, lb, nw):
    nb, l, _ = proj.shape
    n_c = l // CHUNK
    n_g = nb * HG_HEADS
    col, vec, nws, st, _ = _hg_specs(nb, n_c, False)

    def body(p_ref, lb_ref, nw_ref, y_ref, st_ref, s_scr):
        @pl.when(pl.program_id(0) == 0)
        def _():
            s_scr[...] = jnp.zeros_like(s_scr)

        s_in = s_scr[...]
        st_ref[...] = s_in.reshape(st_ref.shape)
        args = [_seq_heads(p_ref, nb, HG_HEADS, k * HG_HEADS) for k in range(4)]
        y, s_new = _hg_chunk(s_in, *args, jnp.concatenate([lb_ref[...]] * nb), nw_ref[...])
        _put_seq_heads(y_ref, y, nb, HG_HEADS)
        s_scr[...] = s_new

    return pl.pallas_call(
        body, name="hgrn2_fwd", grid=(n_c,),
        in_specs=[col(4 * HG_QK), vec, nws], out_specs=[col(HG_QK), st],
        out_shape=[jax.ShapeDtypeStruct((nb, l, HG_QK), BF16),
                   jax.ShapeDtypeStruct((nb, HG_HEADS, n_c, LANES, LANES), F32)],
        scratch_shapes=[pltpu.VMEM((n_g, LANES, LANES), F32)],
        compiler_params=_params(("arbitrary",)),
    )(proj, lb, nw)


def _hgrn2_bwd(proj, lb, nw, states, dy):
    nb, l, _ = proj.shape
    n_c = l // CHUNK
    n_g = nb * HG_HEADS
    col, vec, nws, st, acc = _hg_specs(nb, n_c, True)

    def body(p_ref, lb_ref, nw_ref, st_ref, dy_ref, dp_ref, dlb_ref, dnw_ref, ds_scr):
        @pl.when(pl.program_id(0) == 0)
        def _():
            ds_scr[...] = jnp.zeros_like(ds_scr)
            dlb_ref[...] = jnp.zeros_like(dlb_ref)
            dnw_ref[...] = jnp.zeros_like(dnw_ref)

        args = [_seq_heads(p_ref, nb, HG_HEADS, k * HG_HEADS) for k in range(4)]
        _, vjp = jax.vjp(_hg_chunk, st_ref[...].reshape(n_g, LANES, LANES), *args,
                         jnp.concatenate([lb_ref[...]] * nb), nw_ref[...])
        ds, dql, dfl, div, dgl, dlb, dnw = vjp((_seq_heads(dy_ref, nb, HG_HEADS).astype(F32), ds_scr[...]))
        ds_scr[...] = ds
        for k, val in enumerate((dql, dfl, div, dgl)):
            _put_seq_heads(dp_ref, val, nb, HG_HEADS, k * HG_HEADS)
        dlb_ref[...] += dlb.reshape(dlb_ref.shape)
        dnw_ref[...] += dnw

    return pl.pallas_call(
        body, name="hgrn2_bwd", grid=(n_c,),
        in_specs=[col(4 * HG_QK), vec, nws, st, col(HG_QK)], out_specs=[col(4 * HG_QK), acc, nws],
        out_shape=[jax.ShapeDtypeStruct((nb, l, 4 * HG_QK), BF16),
                   jax.ShapeDtypeStruct((nb, HG_HEADS, 1, LANES), F32), jax.ShapeDtypeStruct((1, LANES), F32)],
        scratch_shapes=[pltpu.VMEM((n_g, LANES, LANES), F32)],
        compiler_params=_params(("arbitrary",)),
    )(proj, lb, nw, states, dy)


def _shift_down(x, s):
    if s == 0:
        return x
    rows = lax.broadcasted_iota(jnp.int32, x.shape, 0)
    return jnp.where(rows >= s, pltpu.roll(x, s, 0), 0.0)


def _shift_up(x, s):
    if s == 0:
        return x
    n = x.shape[0]
    rows = lax.broadcasted_iota(jnp.int32, x.shape, 0)
    return jnp.where(rows < n - s, pltpu.roll(x, n - s, 0), 0.0)


def _gdn_pre_fwd(proj, conv_w):
    nb, l, _ = proj.shape
    n_blk = GDN_QKV // LANES

    def body(x_ref, w_ref, o_ref):
        j = pl.program_id(1)
        x = x_ref[...]
        w = w_ref[...]
        c = w[3:4] * x
        for t in range(CONV_W - 1):
            c = c + w[t:t + 1] * _shift_down(x, CONV_W - 1 - t)
        s = c * _sigmoid(c)
        r = lax.rsqrt(jnp.sum(s * s, axis=-1, keepdims=True) + NORM_EPS)
        scale = jnp.where(j < GDN_HEADS, HEAD_DIM ** -0.5, 1.0)
        o_ref[...] = jnp.where(j < 2 * GDN_HEADS, s * r * scale, s)

    return pl.pallas_call(
        body, name="gdn_pre_fwd", grid=(nb, n_blk),
        in_specs=[pl.BlockSpec((None, l, LANES), lambda b, j: (b, 0, j)), pl.BlockSpec((CONV_W, LANES), lambda b, j: (0, j))],
        out_specs=pl.BlockSpec((None, l, LANES), lambda b, j: (b, 0, j)),
        out_shape=jax.ShapeDtypeStruct((nb, l, GDN_QKV), F32),
        compiler_params=_params(("parallel", "parallel")),
    )(proj, conv_w)


def _gdn_pre_bwd(proj, conv_w, dout):
    nb, l, _ = proj.shape
    n_blk = GDN_QKV // LANES

    def body(x_ref, w_ref, d_ref, dx_ref, dw_ref):
        j = pl.program_id(1)
        x = x_ref[...]
        w = w_ref[...]
        xs = [_shift_down(x, CONV_W - 1 - t) for t in range(CONV_W)]
        c = w[0:1] * xs[0]
        for t in range(1, CONV_W):
            c = c + w[t:t + 1] * xs[t]
        sg = _sigmoid(c)
        s = c * sg
        d = d_ref[...]
        r = lax.rsqrt(jnp.sum(s * s, axis=-1, keepdims=True) + NORM_EPS)
        scale = jnp.where(j < GDN_HEADS, HEAD_DIM ** -0.5, 1.0)
        ds_norm = scale * r * (d - s * (r * r) * jnp.sum(d * s, axis=-1, keepdims=True))
        ds = jnp.where(j < 2 * GDN_HEADS, ds_norm, d)
        dc = ds * (sg * (1.0 + c * (1.0 - sg)))
        dx = w[3:4] * dc
        for t in range(CONV_W - 1):
            dx = dx + w[t:t + 1] * _shift_up(dc, CONV_W - 1 - t)
        dx_ref[...] = dx.astype(dx_ref.dtype)
        for t in range(CONV_W):
            dw_ref[t:t + 1, :] = jnp.sum(dc * xs[t], axis=0, keepdims=True)

    return pl.pallas_call(
        body, name="gdn_pre_bwd", grid=(nb, n_blk),
        in_specs=[pl.BlockSpec((None, l, LANES), lambda b, j: (b, 0, j)), pl.BlockSpec((CONV_W, LANES), lambda b, j: (0, j)),
                  pl.BlockSpec((None, l, LANES), lambda b, j: (b, 0, j))],
        out_specs=[pl.BlockSpec((None, l, LANES), lambda b, j: (b, 0, j)), pl.BlockSpec((None, CONV_W, LANES), lambda b, j: (b, 0, j))],
        out_shape=[jax.ShapeDtypeStruct((nb, l, GDN_QKV), BF16), jax.ShapeDtypeStruct((nb, CONV_W, GDN_QKV), F32)],
        compiler_params=_params(("parallel", "parallel")),
    )(proj, conv_w, dout)


def _gdn_chunk(st, q, k, v, gate, bl, al, alog, dtb, nw):
    n_g, c, _ = q.shape
    beta = _sigmoid(bl)
    x = al + dtb
    softplus = jnp.maximum(x, 0.0) + jnp.log(1.0 + jnp.exp(-jnp.abs(x)))
    la = -jnp.exp(alog) * softplus
    ri = lax.broadcasted_iota(jnp.int32, (c, c), 0)
    ci = lax.broadcasted_iota(jnp.int32, (c, c), 1)
    lower = (ci <= ri)[None]
    strict = (ci < ri)[None]
    ltri = jnp.broadcast_to(jnp.where(lower, 1.0, 0.0), (n_g, c, c))
    la_l = la + jnp.zeros((1, 1, LANES), F32)
    g = _bmm_hi(ltri, la_l)
    delta = _bmm_hi(ltri, jnp.where(strict, la + jnp.zeros((1, 1, c), F32), 0.0))
    gam = jnp.where(lower, jnp.exp(jnp.where(lower, delta, 0.0)), 0.0)
    kb = k * beta
    vb = v * beta
    m = jnp.where(strict, _bmm_nt(kb, k) * gam, 0.0)
    pw = -m
    t_inv = jnp.where((ri == ci)[None], 1.0, 0.0) + pw
    steps = int(math.log2(c)) - 1
    for _ in range(steps):
        pw = _bmm_hi(pw, pw)
        t_inv = t_inv + _bmm_hi(t_inv, pw)
    eg = jnp.exp(g)
    u = _bmm(t_inv, vb)
    w = _bmm(t_inv, kb * eg)
    attn = jnp.where(lower, _bmm_nt(q, k) * gam, 0.0)
    v_new = u - _bmm_nt(w, st)
    o = _bmm_nt(q * eg, st) + _bmm(attn, v_new)
    g_last = jnp.sum(la_l, axis=1, keepdims=True)
    st_new = st * jnp.exp(g_last) + _bmm_tn(v_new, k * jnp.exp(g_last - g))
    return _head_norm_gate(o, gate, nw), st_new


GDN_WIDE = GDN_HEADS * LANES
GDN_SMALL_BLOCK = (GDN_QKV + GDN_WIDE) // LANES


def _gdn_specs(nb, n_c, rev):
    def cidx(cc):
        return (n_c - 1 - cc) if rev else cc

    def col(width, blk=0):
        return pl.BlockSpec((nb, CHUNK, width), lambda cc: (0, cidx(cc), blk))

    scal = pl.BlockSpec((GDN_HEADS, 1, 1), lambda cc: (0, 0, 0))
    nw = pl.BlockSpec((1, LANES), lambda cc: (0, 0))
    st = pl.BlockSpec((nb, GDN_HEADS, None, LANES, LANES), lambda cc: (0, 0, cidx(cc), 0, 0))
    acc_s = pl.BlockSpec((nb, GDN_HEADS, 1, 1), lambda cc: (0, 0, 0, 0))
    return col, scal, nw, st, acc_s


def _tok_cols(xs, nb, first):
    return jnp.stack([xs[b][:, first + h:first + h + 1] for b in range(nb) for h in range(GDN_HEADS)])


def _gdn_inputs(qkv_ref, p_gate_ref, p_small_ref, alog_ref, dtb_ref, nw_ref, nb):
    q, k, v = (_seq_heads(qkv_ref, nb, GDN_HEADS, i * GDN_HEADS) for i in range(3))
    xs = p_small_ref[...]
    return (q, k, v, _seq_heads(p_gate_ref, nb, GDN_HEADS), _tok_cols(xs, nb, 0), _tok_cols(xs, nb, GDN_HEADS),
            jnp.concatenate([alog_ref[...]] * nb), jnp.concatenate([dtb_ref[...]] * nb), nw_ref[...])


def _gdn_fwd(qkv, proj, alog, dtb, nw):
    nb, l, _ = qkv.shape
    n_c = l // CHUNK
    n_g = nb * GDN_HEADS
    col, scal, nws, st, _ = _gdn_specs(nb, n_c, False)

    def body(qkv_ref, pg_ref, ps_ref, alog_ref, dtb_ref, nw_ref, y_ref, st_ref, s_scr):
        @pl.when(pl.program_id(0) == 0)
        def _():
            s_scr[...] = jnp.zeros_like(s_scr)

        s_in = s_scr[...]
        st_ref[...] = s_in.reshape(st_ref.shape)
        y, s_new = _gdn_chunk(s_in, *_gdn_inputs(qkv_ref, pg_ref, ps_ref, alog_ref, dtb_ref, nw_ref, nb))
        _put_seq_heads(y_ref, y, nb, GDN_HEADS)
        s_scr[...] = s_new

    return pl.pallas_call(
        body, name="gdn_fwd", grid=(n_c,),
        in_specs=[col(GDN_QKV), col(GDN_WIDE, GDN_QKV // GDN_WIDE), col(LANES, GDN_SMALL_BLOCK), scal, scal, nws],
        out_specs=[col(GDN_WIDE), st],
        out_shape=[jax.ShapeDtypeStruct((nb, l, GDN_WIDE), BF16),
                   jax.ShapeDtypeStruct((nb, GDN_HEADS, n_c, LANES, LANES), F32)],
        scratch_shapes=[pltpu.VMEM((n_g, LANES, LANES), F32)],
        compiler_params=_params(("arbitrary",)),
    )(qkv, proj, proj, alog, dtb, nw)


def _gdn_bwd(qkv, proj, alog, dtb, nw, states, dy):
    nb, l, _ = qkv.shape
    n_c = l // CHUNK
    n_g = nb * GDN_HEADS
    col, scal, nws, st, acc_s = _gdn_specs(nb, n_c, True)

    def body(qkv_ref, pg_ref, ps_ref, alog_ref, dtb_ref, nw_ref, st_ref, dy_ref,
             dqkv_ref, dg_ref, dsm_ref, dalog_ref, ddtb_ref, dnw_ref, ds_scr):
        @pl.when(pl.program_id(0) == 0)
        def _():
            ds_scr[...] = jnp.zeros_like(ds_scr)
            dalog_ref[...] = jnp.zeros_like(dalog_ref)
            ddtb_ref[...] = jnp.zeros_like(ddtb_ref)
            dnw_ref[...] = jnp.zeros_like(dnw_ref)

        _, vjp = jax.vjp(_gdn_chunk, st_ref[...].reshape(n_g, LANES, LANES),
                         *_gdn_inputs(qkv_ref, pg_ref, ps_ref, alog_ref, dtb_ref, nw_ref, nb))
        ds, dq, dk, dv, dg, dbl, dal, dalog, ddtb, dnw = vjp((_seq_heads(dy_ref, nb, GDN_HEADS).astype(F32), ds_scr[...]))
        ds_scr[...] = ds
        for i, val in enumerate((dq, dk, dv)):
            _put_seq_heads(dqkv_ref, val, nb, GDN_HEADS, i * GDN_HEADS)
        _put_seq_heads(dg_ref, dg, nb, GDN_HEADS)
        lane = lax.broadcasted_iota(jnp.int32, (CHUNK, LANES), 1)
        for b in range(nb):
            small = jnp.zeros((CHUNK, LANES), F32)
            for h in range(GDN_HEADS):
                small = small + jnp.where(lane == h, dbl[b * GDN_HEADS + h], 0.0)
                small = small + jnp.where(lane == GDN_HEADS + h, dal[b * GDN_HEADS + h], 0.0)
            dsm_ref[b] = small.astype(dsm_ref.dtype)
        dalog_ref[...] += dalog.reshape(dalog_ref.shape)
        ddtb_ref[...] += ddtb.reshape(ddtb_ref.shape)
        dnw_ref[...] += dnw

    h = GDN_HEADS
    return pl.pallas_call(
        body, name="gdn_bwd", grid=(n_c,),
        in_specs=[col(GDN_QKV), col(GDN_WIDE, GDN_QKV // GDN_WIDE), col(LANES, GDN_SMALL_BLOCK), scal, scal, nws, st,
                  col(GDN_WIDE)],
        out_specs=[col(GDN_QKV), col(GDN_WIDE), col(LANES), acc_s, acc_s, nws],
        out_shape=[jax.ShapeDtypeStruct((nb, l, GDN_QKV), F32), jax.ShapeDtypeStruct((nb, l, GDN_WIDE), BF16),
                   jax.ShapeDtypeStruct((nb, l, LANES), BF16),
                   jax.ShapeDtypeStruct((nb, h, 1, 1), F32), jax.ShapeDtypeStruct((nb, h, 1, 1), F32),
                   jax.ShapeDtypeStruct((1, LANES), F32)],
        scratch_shapes=[pltpu.VMEM((n_g, LANES, LANES), F32)],
        compiler_params=_params(("arbitrary",)),
    )(qkv, proj, proj, alog, dtb, nw, states, dy)


S5_ROWS = S5_NSTATE // LANES
S5_TB = 256


def _s5_scan_fwd(bu, abar):
    nb, l = bu.shape[:2]
    tb = min(S5_TB, l)

    def body(bu_ref, a_ref, h_ref, c_scr):
        @pl.when(pl.program_id(1) == 0)
        def _():
            c_scr[...] = jnp.zeros_like(c_scr)

        ar = a_ref[0]
        ai = a_ref[1]

        def step(t, carry):
            hr, hi = carry
            nr = ar * hr - ai * hi + bu_ref[t, 0].astype(F32)
            ni = ar * hi + ai * hr + bu_ref[t, 1].astype(F32)
            h_ref[t, 0] = nr.astype(h_ref.dtype)
            h_ref[t, 1] = ni.astype(h_ref.dtype)
            return nr, ni

        hr, hi = lax.fori_loop(0, tb, step, (c_scr[0], c_scr[1]), unroll=8)
        c_scr[0] = hr
        c_scr[1] = hi

    blk = pl.BlockSpec((None, tb, 2, S5_ROWS, LANES), lambda b, i: (b, i, 0, 0, 0))
    return pl.pallas_call(
        body, name="s5_scan_fwd", grid=(nb, l // tb),
        in_specs=[blk, pl.BlockSpec((2, S5_ROWS, LANES), lambda b, i: (0, 0, 0))],
        out_specs=blk, out_shape=jax.ShapeDtypeStruct(bu.shape, bu.dtype),
        scratch_shapes=[pltpu.VMEM((2, S5_ROWS, LANES), F32)],
        compiler_params=_params(("parallel", "arbitrary")),
    )(bu, abar)


def _s5_scan_bwd(dh, h, abar):
    nb, l = dh.shape[:2]
    tb = min(S5_TB, l)
    n_blk = l // tb

    def body(dh_ref, h_ref, a_ref, g_ref, da_ref, c_scr):
        @pl.when(pl.program_id(1) == 0)
        def _():
            c_scr[...] = jnp.zeros_like(c_scr)
            da_ref[...] = jnp.zeros_like(da_ref)

        ar = a_ref[0]
        ai = a_ref[1]

        def step(s, carry):
            gr, gi, dar, dai = carry
            t = tb - 1 - s
            hr = h_ref[t, 0].astype(F32)
            hi = h_ref[t, 1].astype(F32)
            dar = dar + gr * hr + gi * hi
            dai = dai + gi * hr - gr * hi
            nr = ar * gr + ai * gi + dh_ref[t, 0].astype(F32)
            ni = ar * gi - ai * gr + dh_ref[t, 1].astype(F32)
            g_ref[t, 0] = nr.astype(g_ref.dtype)
            g_ref[t, 1] = ni.astype(g_ref.dtype)
            return nr, ni, dar, dai

        z = jnp.zeros((S5_ROWS, LANES), F32)
        gr, gi, dar, dai = lax.fori_loop(0, tb, step, (c_scr[0], c_scr[1], z, z), unroll=8)
        c_scr[0] = gr
        c_scr[1] = gi
        da_ref[0] += dar
        da_ref[1] += dai

    blk = pl.BlockSpec((None, tb, 2, S5_ROWS, LANES), lambda b, i: (b, n_blk - 1 - i, 0, 0, 0))
    return pl.pallas_call(
        body, name="s5_scan_bwd", grid=(nb, n_blk),
        in_specs=[blk, blk, pl.BlockSpec((2, S5_ROWS, LANES), lambda b, i: (0, 0, 0))],
        out_specs=[blk, pl.BlockSpec((None, 2, S5_ROWS, LANES), lambda b, i: (b, 0, 0, 0))],
        out_shape=[jax.ShapeDtypeStruct(dh.shape, dh.dtype), jax.ShapeDtypeStruct((nb, 2, S5_ROWS, LANES), F32)],
        scratch_shapes=[pltpu.VMEM((2, S5_ROWS, LANES), F32)],
        compiler_params=_params(("parallel", "arbitrary")),
    )(dh, h, abar)


_GELU_C = math.sqrt(2.0 / math.pi)


def _gelu(x):
    return 0.5 * x * (1.0 + jnp.tanh(_GELU_C * (x + 0.044715 * x * x * x)))


def _gelu_grad(x):
    t = jnp.tanh(_GELU_C * (x + 0.044715 * x * x * x))
    return 0.5 * (1.0 + t) + 0.5 * x * (1.0 - t * t) * _GELU_C * (1.0 + 3.0 * 0.044715 * x * x)


def _s5_post_fwd(yc, u, d, w_glu):
    t, w = yc.shape
    tt = _tile(t, (512, 256, 128))

    def body(yc_ref, u_ref, d_ref, w_ref, o_ref, y0_ref):
        y0 = yc_ref[...] + d_ref[...] * u_ref[...]
        y = _gelu(y0)
        z = _dot(y, w_ref[...])
        o_ref[...] = (y * _sigmoid(z)).astype(o_ref.dtype)
        y0_ref[...] = y0

    tok = pl.BlockSpec((tt, w), lambda i: (i, 0))
    return pl.pallas_call(
        body, name="s5_post_fwd", grid=(t // tt,),
        in_specs=[tok, tok, pl.BlockSpec((1, w), lambda i: (0, 0)), pl.BlockSpec((w, w), lambda i: (0, 0))],
        out_specs=[tok, tok],
        out_shape=[jax.ShapeDtypeStruct((t, w), BF16), jax.ShapeDtypeStruct((t, w), F32)],
        compiler_params=_params(("parallel",)),
    )(yc, u, d, w_glu)


def _s5_post_bwd(y0, u, d, w_glu, dout):
    t, w = y0.shape
    tt = _tile(t, (512, 256, 128))

    def body(y0_ref, u_ref, d_ref, w_ref, do_ref, dy0_ref, y_ref, dz_ref, du_ref, dd_ref):
        @pl.when(pl.program_id(0) == 0)
        def _():
            dd_ref[...] = jnp.zeros_like(dd_ref)

        y0 = y0_ref[...]
        y = _gelu(y0)
        s = _sigmoid(_dot(y, w_ref[...]))
        do = do_ref[...]
        dz = do * y * s * (1.0 - s)
        dy = do * s + _dot_nt(dz, w_ref[...])
        dy0 = dy * _gelu_grad(y0)
        dy0_ref[...] = dy0.astype(dy0_ref.dtype)
        y_ref[...] = y.astype(y_ref.dtype)
        dz_ref[...] = dz.astype(dz_ref.dtype)
        du_ref[...] = dy0 * d_ref[...]
        dd_ref[...] += jnp.sum(dy0 * u_ref[...], axis=0, keepdims=True)

    tok = pl.BlockSpec((tt, w), lambda i: (i, 0))
    vec = pl.BlockSpec((1, w), lambda i: (0, 0))
    return pl.pallas_call(
        body, name="s5_post_bwd", grid=(t // tt,),
        in_specs=[tok, tok, vec, pl.BlockSpec((w, w), lambda i: (0, 0)), tok],
        out_specs=[tok, tok, tok, tok, vec],
        out_shape=[jax.ShapeDtypeStruct((t, w), BF16), jax.ShapeDtypeStruct((t, w), BF16),
                   jax.ShapeDtypeStruct((t, w), BF16), jax.ShapeDtypeStruct((t, w), F32),
                   jax.ShapeDtypeStruct((1, w), F32)],
        compiler_params=_params(("arbitrary",)),
    )(y0, u, d, w_glu, dout)


def _s5_params(a_re, a_im, b_re, b_im, c_re, c_im, log_dt):
    dt = jnp.exp(log_dt)[:, None]
    mag = jnp.exp(dt * a_re)
    ang = dt * a_im
    abar_re = mag * jnp.cos(ang)
    abar_im = mag * jnp.sin(ang)
    den = a_re * a_re + a_im * a_im
    zr = abar_re - 1.0
    zi = abar_im
    coef_re = ((zr * a_re + zi * a_im) / den)[..., None]
    coef_im = ((zi * a_re - zr * a_im) / den)[..., None]
    bb_re = coef_re * b_re - coef_im * b_im
    bb_im = coef_re * b_im + coef_im * b_re
    eye = jnp.eye(S5_GROUPS, dtype=F32)

    def dense_in(bb):
        return jnp.einsum('gnp,gh->gphn', bb, eye).reshape(S5_WIDTH, S5_NSTATE)

    def dense_out(cc):
        return jnp.einsum('gpn,gh->gnhp', cc, eye).reshape(S5_NSTATE, S5_WIDTH)

    abar = jnp.stack([abar_re.reshape(S5_ROWS, LANES), abar_im.reshape(S5_ROWS, LANES)])
    bbd = jnp.concatenate([dense_in(bb_re), dense_in(bb_im)], axis=1)
    ccd = jnp.concatenate([dense_out(c_re), dense_out(-c_im)], axis=0)
    return abar, bbd, ccd


def _tok_tile(t):
    return _tile(t, (256, 128))


def _rms_fwd(x, w):
    t, d = x.shape
    tt = _tok_tile(t)

    def body(x_ref, w_ref, o_ref):
        xv = x_ref[...]
        r = lax.rsqrt(jnp.mean(xv * xv, axis=-1, keepdims=True) + NORM_EPS)
        o_ref[...] = (xv * r * w_ref[...]).astype(o_ref.dtype)

    tok = pl.BlockSpec((tt, d), lambda i: (i, 0))
    return pl.pallas_call(
        body, name="rms_fwd", grid=(t // tt,), in_specs=[tok, pl.BlockSpec((1, d), lambda i: (0, 0))],
        out_specs=tok, out_shape=jax.ShapeDtypeStruct((t, d), BF16), compiler_params=_params(("parallel",)),
    )(x, w)


def _rms_bwd_math(xv, wv, dy):
    r = lax.rsqrt(jnp.mean(xv * xv, axis=-1, keepdims=True) + NORM_EPS)
    xh = xv * r
    dxh = dy * wv
    dx = r * (dxh - xh * jnp.mean(dxh * xh, axis=-1, keepdims=True))
    return dx, jnp.sum(dy * xh, axis=0, keepdims=True)


def _rms_bwd(x, w, dy, dres):
    t, d = x.shape
    tt = _tok_tile(t)

    def body(x_ref, w_ref, dy_ref, dr_ref, dx_ref, dw_ref):
        @pl.when(pl.program_id(0) == 0)
        def _():
            dw_ref[...] = jnp.zeros_like(dw_ref)

        dx, dw = _rms_bwd_math(x_ref[...], w_ref[...], dy_ref[...])
        dx_ref[...] = dr_ref[...] + dx
        dw_ref[...] += dw

    tok = pl.BlockSpec((tt, d), lambda i: (i, 0))
    vec = pl.BlockSpec((1, d), lambda i: (0, 0))
    return pl.pallas_call(
        body, name="rms_bwd", grid=(t // tt,), in_specs=[tok, vec, tok, tok], out_specs=[tok, vec],
        out_shape=[jax.ShapeDtypeStruct((t, d), F32), jax.ShapeDtypeStruct((1, d), F32)],
        compiler_params=_params(("arbitrary",)),
    )(x, w, dy, dres)


def _loss_head(x, w, target):
    t, d = x.shape
    tt = _tok_tile(t)

    def body(x_ref, w_ref, t_ref, l_ref, dx_ref, dw_ref):
        @pl.when(pl.program_id(0) == 0)
        def _():
            l_ref[...] = jnp.zeros_like(l_ref)
            dw_ref[...] = jnp.zeros_like(dw_ref)

        xv = x_ref[...]
        wv = w_ref[...]
        r = lax.rsqrt(jnp.mean(xv * xv, axis=-1, keepdims=True) + NORM_EPS)
        err = xv * r * wv - t_ref[...]
        row = jnp.sum(err * err, axis=-1, keepdims=True)
        l_ref[...] += (0.5 / d) * jnp.sum(row, axis=0, keepdims=True)
        dx, dw = _rms_bwd_math(xv, wv, err * (1.0 / d))
        dx_ref[...] = dx
        dw_ref[...] += dw

    tok = pl.BlockSpec((tt, d), lambda i: (i, 0))
    vec = pl.BlockSpec((1, d), lambda i: (0, 0))
    return pl.pallas_call(
        body, name="loss_head", grid=(t // tt,), in_specs=[tok, vec, tok],
        out_specs=[pl.BlockSpec((1, 1), lambda i: (0, 0)), tok, vec],
        out_shape=[jax.ShapeDtypeStruct((1, 1), F32), jax.ShapeDtypeStruct((t, d), F32), jax.ShapeDtypeStruct((1, d), F32)],
        compiler_params=_params(("arbitrary",)),
    )(x, w, target)


FF_PAD = 3072
FF_SHARD = D_FF // N_DEV
FF_SHARD_PAD = FF_PAD // N_DEV


def _ffn_in(x, nw, wg_t, wu_t):
    t, d = x.shape
    tm = _tile(t, (512, 256, 128))
    tn = 1024

    def body(x_ref, nw_ref, wg_ref, wu_ref, h_ref, g_ref, u_ref, a_ref):
        @pl.when(pl.program_id(1) == 0)
        def _():
            xv = x_ref[...]
            r = lax.rsqrt(jnp.mean(xv * xv, axis=-1, keepdims=True) + NORM_EPS)
            h_ref[...] = (xv * r * nw_ref[...]).astype(h_ref.dtype)

        h = h_ref[...]
        g = _dot_nt(h, wg_ref[...])
        u = _dot_nt(h, wu_ref[...])
        g_ref[...] = g.astype(g_ref.dtype)
        u_ref[...] = u.astype(u_ref.dtype)
        a_ref[...] = (g * _sigmoid(g) * u).astype(a_ref.dtype)

    tok = pl.BlockSpec((tm, d), lambda i, j: (i, 0))
    wsp = pl.BlockSpec((tn, d), lambda i, j: (j, 0))
    wide = pl.BlockSpec((tm, tn), lambda i, j: (i, j))
    return pl.pallas_call(
        body, name="ffn_in", grid=(t // tm, FF_PAD // tn),
        in_specs=[tok, pl.BlockSpec((1, d), lambda i, j: (0, 0)), wsp, wsp], out_specs=[tok, wide, wide, wide],
        out_shape=[jax.ShapeDtypeStruct((t, d), BF16)] + [jax.ShapeDtypeStruct((t, FF_PAD), BF16)] * 3,
        compiler_params=_params(("parallel", "arbitrary")),
    )(x, nw, wg_t, wu_t)


def _ffn_out_dx(dxo, wd, g, u):
    t, d = dxo.shape
    tm = _tile(t, (512, 256, 128))
    tn = 1024

    def body(dx_ref, wd_ref, g_ref, u_ref, dg_ref, du_ref):
        da = 0.5 * _dot_nt(dx_ref[...], wd_ref[...])
        g = g_ref[...].astype(F32)
        s = _sigmoid(g)
        dg_ref[...] = (da * u_ref[...].astype(F32) * (s * (1.0 + g * (1.0 - s)))).astype(dg_ref.dtype)
        du_ref[...] = (da * g * s).astype(du_ref.dtype)

    wide = pl.BlockSpec((tm, tn), lambda j, i: (i, j))
    return pl.pallas_call(
        body, name="ffn_out_dx", grid=(FF_PAD // tn, t // tm),
        in_specs=[pl.BlockSpec((tm, d), lambda j, i: (i, 0)), pl.BlockSpec((tn, d), lambda j, i: (j, 0)), wide, wide],
        out_specs=[wide, wide], out_shape=[jax.ShapeDtypeStruct((t, FF_PAD), BF16)] * 2,
        compiler_params=_params(("parallel", "parallel")),
    )(dxo, wd, g, u)


def _ffn_in_dx(dg, du, wg_t, wu_t, x, nw, dxo):
    t, d = x.shape
    tm = _tile(t, (256, 128))

    def body(dg_ref, du_ref, wg_ref, wu_ref, x_ref, nw_ref, dr_ref, dx_ref, dw_ref):
        @pl.when(pl.program_id(0) == 0)
        def _():
            dw_ref[...] = jnp.zeros_like(dw_ref)

        dh = _dot(dg_ref[...], wg_ref[...]) + _dot(du_ref[...], wu_ref[...])
        dx, dw = _rms_bwd_math(x_ref[...], nw_ref[...], dh)
        dx_ref[...] = dr_ref[...] + dx
        dw_ref[...] += dw

    wide = pl.BlockSpec((tm, FF_PAD), lambda i: (i, 0))
    wsp = pl.BlockSpec((FF_PAD, d), lambda i: (0, 0))
    tok = pl.BlockSpec((tm, d), lambda i: (i, 0))
    vec = pl.BlockSpec((1, d), lambda i: (0, 0))
    return pl.pallas_call(
        body, name="ffn_in_dx", grid=(t // tm,), in_specs=[wide, wide, wsp, wsp, tok, vec, tok], out_specs=[tok, vec],
        out_shape=[jax.ShapeDtypeStruct((t, d), F32), jax.ShapeDtypeStruct((1, d), F32)],
        compiler_params=_params(("arbitrary",)),
    )(dg, du, wg_t, wu_t, x, nw, dxo)


def _adamw(w, g, m, v):
    r, c = w.shape
    tr = _tile(r, (512, 256, 128, 64, 32, 16, 8))
    c1 = 1.0 - ADAM_B1 ** ADAM_STEP
    c2 = 1.0 - ADAM_B2 ** ADAM_STEP

    def body(w_ref, g_ref, m_ref, v_ref, d_ref, nm_ref, nv_ref):
        gv = g_ref[...]
        nm = ADAM_B1 * m_ref[...] + (1.0 - ADAM_B1) * gv
        nv = ADAM_B2 * v_ref[...] + (1.0 - ADAM_B2) * (gv * gv)
        d_ref[...] = -ADAM_LR * ((nm / c1) / (jnp.sqrt(nv / c2) + ADAM_EPS) + ADAM_WD * w_ref[...])
        nm_ref[...] = nm
        nv_ref[...] = nv

    blk = pl.BlockSpec((tr, c), lambda i: (i, 0))
    return pl.pallas_call(
        body, name="adamw", grid=(r // tr,), in_specs=[blk] * 4, out_specs=[blk] * 3,
        out_shape=[jax.ShapeDtypeStruct((r, c), F32)] * 3, compiler_params=_params(("parallel",)),
    )(w, g, m, v)


def _ffn_slot(which, kind, layer):
    return (which * 3 + kind) * DEPTH + layer


def _ffn_fwd(x, nw, w_ffn, which, layer):
    wg, wu, wd = (w_ffn[_ffn_slot(which, kind, layer)] for kind in range(3))
    h, g, u, a = _ffn_in(x, nw, wg, wu)
    return _mm(a, wd, name="ffn_out", res=x, scale=0.5), (x, h, g, u, a)


def _ffn_bwd(dxo, res, nw, w_ffn, g_ffn, which, layer):
    x, h, g, u, a = res
    wg, wu, wd = (w_ffn[_ffn_slot(which, kind, layer)] for kind in range(3))
    dg, du = _ffn_out_dx(dxo, wd, g, u)
    g_ffn = _mm_tn(a, dxo, name="ffn_out_dw", scale=0.5, into=g_ffn, slot=_ffn_slot(which, 2, layer))
    g_ffn = _mm_tn(dg, h, name="ffn_in_dw", into=g_ffn, slot=_ffn_slot(which, 0, layer))
    g_ffn = _mm_tn(du, h, name="ffn_in_dw", into=g_ffn, slot=_ffn_slot(which, 1, layer))
    dx, dnw = _ffn_in_dx(dg, du, wg, wu, x, nw, dxo)
    return dx, dnw, g_ffn


def _hg_lower_bounds(logits):
    p = jax.nn.softmax(logits, axis=0)
    return jnp.cumsum(p, axis=0) - p[0]


def _even_fwd(x1, p, nb, l):
    t = nb * l
    h = _rms_fwd(x1, p["mix_norm"])
    proj = _mm(h, p["w_in_t"], name="ev_in", nt=True)
    proj3 = proj.reshape(nb, l, EV_IN)
    ya, hst = _hgrn2_fwd(proj3, p["lb"], p["hg_nw"])
    u = proj[:, 4 * HG_QK:]
    bu = _mm(u, p["bbd"], name="s5_in", out_dtype=BF16)
    hs = _s5_scan_fwd(bu.reshape(nb, l, 2, S5_ROWS, LANES), p["abar"])
    yc = _mm(hs.reshape(t, 2 * S5_NSTATE), p["ccd"], name="s5_out")
    yb, y0 = _s5_post_fwd(yc, u, p["s5_d"], p["w_glu"])
    ycat = jnp.concatenate([ya.reshape(t, HG_QK), yb], axis=1)
    x2 = _mm(ycat, p["w_out"], name="mix_out", res=x1)
    return x2, (x1, h, proj3, hst, u, hs, y0, ycat)


def _even_bwd(dx2, res, p, nb, l):
    x1, h, proj3, hst, u, hs, y0, ycat = res
    t = nb * l
    dycat = _mm(dx2, p["w_out"], name="mix_out_dx", nt=True)
    dw_out = _mm_tn(ycat, dx2, name="mix_out_dw", out_dtype=BF16)
    dp_hg, dlb, dhg_nw = _hgrn2_bwd(proj3, p["lb"], p["hg_nw"], hst, dycat.reshape(nb, l, D_MODEL))
    dy0, y, dz, du_d, dd = _s5_post_bwd(y0, u, p["s5_d"], p["w_glu"], dycat[:, HG_QK:])
    dw_glu = _mm_tn(y, dz, name="s5_glu_dw", out_dtype=BF16)
    dhs = _mm(dy0, p["ccd"], name="s5_out_dx", nt=True, out_dtype=BF16)
    dccd = _mm_tn(hs.reshape(t, 2 * S5_NSTATE), dy0, name="s5_out_dw")
    g5, dabar = _s5_scan_bwd(dhs.reshape(nb, l, 2, S5_ROWS, LANES), hs, p["abar"])
    g2 = g5.reshape(t, 2 * S5_NSTATE)
    du = _mm(g2, p["bbd"], name="s5_in_dx", nt=True, res=du_d, out_dtype=BF16)
    dbbd = _mm_tn(u, g2, name="s5_in_dw")
    dproj = jnp.concatenate([dp_hg.reshape(t, 4 * HG_QK), du], axis=1)
    dw_in_t = _mm_tn(dproj, h, name="ev_in_dw", out_dtype=BF16)
    dh = _mm(dproj, p["w_in_t"], name="ev_in_dx")
    dx1, dmix = _rms_bwd(x1, p["mix_norm"], dh, dx2)
    small = dict(mix_norm=dmix, lb=dlb.sum(0).reshape(HG_QK), hg_nw=dhg_nw.reshape(HEAD_DIM),
                 abar=dabar.sum(0), bbd=dbbd, ccd=dccd, s5_d=dd.reshape(S5_WIDTH))
    return dx1, dict(w_in_t=dw_in_t, w_glu=dw_glu, w_out=dw_out), small


def _odd_fwd(x1, p, nb, l):
    t = nb * l
    h = _rms_fwd(x1, p["mix_norm"])
    proj3 = _mm(h, p["w_in"], name="od_in").reshape(nb, l, OD_IN_PAD)
    qkv = _gdn_pre_fwd(proj3, p["conv_w"])
    y, st = _gdn_fwd(qkv, proj3, p["a_log"], p["dt_bias"], p["gdn_nw"])
    y2 = y.reshape(t, D_MODEL)
    x2 = _mm(y2, p["w_out"], name="mix_out", res=x1)
    return x2, (x1, h, proj3, qkv, st, y2)


def _odd_bwd(dx2, res, p, nb, l):
    x1, h, proj3, qkv, st, y2 = res
    t = nb * l
    dy = _mm(dx2, p["w_out"], name="mix_out_dx", nt=True)
    dw_out = _mm_tn(y2, dx2, name="mix_out_dw", out_dtype=BF16)
    dqkv, dgate, dsmall, dalog, ddtb, dnw = _gdn_bwd(
        qkv, proj3, p["a_log"], p["dt_bias"], p["gdn_nw"], st, dy.reshape(nb, l, D_MODEL))
    dx_qkv, dcw = _gdn_pre_bwd(proj3, p["conv_w"], dqkv)
    dproj = jnp.concatenate([dx_qkv, dgate, dsmall], axis=-1).reshape(t, OD_IN_PAD)
    dw_in = _mm_tn(h, dproj, name="od_in_dw", out_dtype=BF16)[:, :OD_IN]
    dh = _mm(dproj, p["w_in"], name="od_in_dx", nt=True)
    dx1, dmix = _rms_bwd(x1, p["mix_norm"], dh, dx2)
    small = dict(mix_norm=dmix, a_log=dalog.sum(0).reshape(GDN_HEADS), dt_bias=ddtb.sum(0).reshape(GDN_HEADS),
                 gdn_nw=dnw.reshape(HEAD_DIM))
    return dx1, dict(w_in=dw_in, conv_w=dcw.sum(0), w_out=dw_out), small


def _local_step(x, target, big, small):
    nb, l, _ = x.shape
    t = nb * l
    lbs, lbs_vjp = jax.vjp(_hg_lower_bounds, small["hg_lb_logits"])
    row = lambda v: v.reshape(1, -1)
    layers = []
    s5_vjps = []
    for layer in range(DEPTH):
        j = layer // 2
        p = dict(mix_norm=row(small["mix_norm"][layer]))
        if layer % 2 == 0:
            (abar, bbd, ccd), s5_vjp = jax.vjp(
                _s5_params, small["s5_a_re"][j], small["s5_a_im"][j], small["s5_b_re"][j], small["s5_b_im"][j],
                small["s5_c_re"][j], small["s5_c_im"][j], small["s5_log_dt"][j])
            s5_vjps.append(s5_vjp)
            p.update(w_in_t=big["ev_w_in_t"][j], w_glu=big["s5_w_glu"][j], w_out=big["w_out"][j],
                     lb=lbs[j].reshape(HG_HEADS, 1, HEAD_DIM), hg_nw=row(small["hg_norm_w"][j]),
                     abar=abar, bbd=bbd.astype(BF16), ccd=ccd.astype(BF16), s5_d=row(small["s5_d"][j]))
        else:
            p.update(w_in=big["od_w_in"][j], conv_w=big["gdn_conv_w"][j], w_out=big["w_out"][2 + j],
                     a_log=small["gdn_a_log"][j].reshape(GDN_HEADS, 1, 1),
                     dt_bias=small["gdn_dt_bias"][j].reshape(GDN_HEADS, 1, 1), gdn_nw=row(small["gdn_norm_w"][j]))
        layers.append(p)

    xs = x.reshape(t, D_MODEL)
    saved = []
    for layer in range(DEPTH):
        xs, r1 = _ffn_fwd(xs, row(small["ffn1_norm"][layer]), big["ffn"], 0, layer)
        xs, r2 = (_even_fwd if layer % 2 == 0 else _odd_fwd)(xs, layers[layer], nb, l)
        xs, r3 = _ffn_fwd(xs, row(small["ffn2_norm"][layer]), big["ffn"], 1, layer)
        saved.append((r1, r2, r3))
    loss, dx, dfinal = _loss_head(xs, row(small["final_norm"]), target.reshape(t, D_MODEL))

    g_ffn = lax.empty(big["ffn"].shape, BF16)
    gb = {k: [None] * 2 for k in ("ev_w_in_t", "s5_w_glu", "ev_w_out", "od_w_in", "gdn_conv_w", "od_w_out")}
    gs = {k: [None] * DEPTH for k in ("ffn1_norm", "mix_norm", "ffn2_norm")}
    gs.update({k: [None] * 2 for k in ("hg_norm_w", "s5_a_re", "s5_a_im", "s5_b_re", "s5_b_im", "s5_c_re", "s5_c_im",
                                       "s5_d", "s5_log_dt", "gdn_a_log", "gdn_dt_bias", "gdn_norm_w")})
    dlbs = [None] * 2
    for layer in reversed(range(DEPTH)):
        j = layer // 2
        p = layers[layer]
        r1, r2, r3 = saved[layer]
        dx, dnw, g_ffn = _ffn_bwd(dx, r3, row(small["ffn2_norm"][layer]), big["ffn"], g_ffn, 1, layer)
        gs["ffn2_norm"][layer] = dnw[0]
        if layer % 2 == 0:
            dx, gw, sm = _even_bwd(dx, r2, p, nb, l)
            gb["ev_w_in_t"][j], gb["s5_w_glu"][j], gb["ev_w_out"][j] = gw["w_in_t"], gw["w_glu"], gw["w_out"]
            dlbs[j] = sm["lb"]
            gs["hg_norm_w"][j] = sm["hg_nw"]
            gs["s5_d"][j] = sm["s5_d"]
            (gs["s5_a_re"][j], gs["s5_a_im"][j], gs["s5_b_re"][j], gs["s5_b_im"][j], gs["s5_c_re"][j],
             gs["s5_c_im"][j], gs["s5_log_dt"][j]) = s5_vjps[j]((sm["abar"], sm["bbd"], sm["ccd"]))
        else:
            dx, gw, sm = _odd_bwd(dx, r2, p, nb, l)
            gb["od_w_in"][j], gb["gdn_conv_w"][j], gb["od_w_out"][j] = gw["w_in"], gw["conv_w"], gw["w_out"]
            gs["gdn_a_log"][j], gs["gdn_dt_bias"][j], gs["gdn_norm_w"][j] = sm["a_log"], sm["dt_bias"], sm["gdn_nw"]
        gs["mix_norm"][layer] = sm["mix_norm"][0]
        dx, dnw, g_ffn = _ffn_bwd(dx, r1, row(small["ffn1_norm"][layer]), big["ffn"], g_ffn, 0, layer)
        gs["ffn1_norm"][layer] = dnw[0]
    gbig = dict(ffn=g_ffn, ev_w_in_t=jnp.stack(gb["ev_w_in_t"]), s5_w_glu=jnp.stack(gb["s5_w_glu"]),
                w_out=jnp.stack(gb["ev_w_out"] + gb["od_w_out"]), od_w_in=jnp.stack(gb["od_w_in"]),
                gdn_conv_w=jnp.stack(gb["gdn_conv_w"]))
    gsmall = {k: jnp.stack(v) for k, v in gs.items()}
    gsmall["hg_lb_logits"] = lbs_vjp(jnp.stack(dlbs))[0]
    gsmall["final_norm"] = dfinal[0]
    return loss[0, 0], dx.reshape(nb, l, D_MODEL), gbig, gsmall


def _here():
    return lax.axis_index("x"), lax.axis_index("y"), lax.axis_index("c")


def _other_chips(x, y):
    return [(1 - x, y), (x, 1 - y), (1 - x, 1 - y)]


def _all_gather(blocks, name):
    n = len(blocks)

    def body(*refs):
        x_refs, o_refs = refs[:n], refs[n:2 * n]
        send_sems, recv_sems, local_sems = refs[2 * n:]
        x, y, cc = _here()
        me, sibling = (x, y, cc), (x, y, 1 - cc)
        chips = _other_chips(x, y)

        def win(i, p):
            return o_refs[i].at[:, 4 * p[0] + 2 * p[1] + p[2]]

        def copy(i, k, blk, to, src=None):
            return pltpu.make_async_remote_copy(
                src_ref=win(i, blk) if src is None else src, dst_ref=win(i, blk),
                send_sem=send_sems.at[i, k], recv_sem=recv_sems.at[i, k], device_id=to, device_id_type=MESH)

        mine = [pltpu.make_async_copy(x_refs[i], win(i, me), local_sems.at[i]) for i in range(n)]
        for cp in mine:
            cp.start()
        first = []
        for i in range(n):
            first.append(copy(i, 0, me, sibling, src=x_refs[i]))
            first += [copy(i, 1 + j, me, (*chip, cc), src=x_refs[i]) for j, chip in enumerate(chips)]
        for cp in first:
            cp.start()
        passed = []
        for j, chip in enumerate(chips):
            for i in range(n):
                copy(i, 1 + j, (*chip, cc), me).wait_recv()
                fwd = copy(i, 4 + j, (*chip, cc), sibling)
                fwd.start()
                passed.append(fwd)
        for i in range(n):
            copy(i, 0, sibling, me).wait_recv()
            for j, chip in enumerate(chips):
                copy(i, 4 + j, (*chip, 1 - cc), me).wait_recv()
        for cp in first + passed:
            cp.wait_send()
        for cp in mine:
            cp.wait()

    hbm = pl.BlockSpec(memory_space=pl.ANY)
    return pl.pallas_call(
        body, name=name, in_specs=[hbm] * n, out_specs=[hbm] * n,
        out_shape=[jax.ShapeDtypeStruct((b.shape[0], N_DEV) + b.shape[1:], b.dtype) for b in blocks],
        scratch_shapes=[pltpu.SemaphoreType.DMA((n, 7)), pltpu.SemaphoreType.DMA((n, 7)), pltpu.SemaphoreType.DMA((n,))],
    )(*blocks)


def _rs_core_exchange(gs):
    n = len(gs)

    def body(*refs):
        g_refs, o_refs = refs[:n], refs[n:2 * n]
        send_sems, recv_sems = refs[2 * n:]
        x, y, cc = _here()
        cps = [pltpu.make_async_remote_copy(
            src_ref=g_refs[i].at[:, 2 * j + (1 - cc)], dst_ref=o_refs[i].at[j],
            send_sem=send_sems.at[i, j], recv_sem=recv_sems.at[i, j], device_id=(x, y, 1 - cc), device_id_type=MESH)
            for i in range(n) for j in range(4)]
        for cp in cps:
            cp.start()
        for cp in cps:
            cp.wait()

    hbm = pl.BlockSpec(memory_space=pl.ANY)
    return pl.pallas_call(
        body, name="rs_core_exchange", in_specs=[hbm] * n, out_specs=[hbm] * n,
        out_shape=[jax.ShapeDtypeStruct((4, g.shape[0]) + g.shape[2:], g.dtype) for g in gs],
        scratch_shapes=[pltpu.SemaphoreType.DMA((n, 4)), pltpu.SemaphoreType.DMA((n, 4))],
    )(*gs)


def _rs_chip_exchange(ps):
    n = len(ps)

    def body(*refs):
        p_refs, o_refs = refs[:n], refs[n:2 * n]
        send_sems, recv_sems = refs[2 * n:]
        x, y, cc = _here()
        cps = [pltpu.make_async_remote_copy(
            src_ref=p_refs[i].at[2 * px + py], dst_ref=o_refs[i].at[k],
            send_sem=send_sems.at[i, k], recv_sem=recv_sems.at[i, k], device_id=(px, py, cc), device_id_type=MESH)
            for i in range(n) for k, (px, py) in enumerate(_other_chips(x, y))]
        for cp in cps:
            cp.start()
        for cp in cps:
            cp.wait()

    hbm = pl.BlockSpec(memory_space=pl.ANY)
    return pl.pallas_call(
        body, name="rs_chip_exchange", in_specs=[hbm] * n, out_specs=[hbm] * n,
        out_shape=[jax.ShapeDtypeStruct((3,) + p.shape[1:], p.dtype) for p in ps],
        scratch_shapes=[pltpu.SemaphoreType.DMA((n, 3)), pltpu.SemaphoreType.DMA((n, 3))],
    )(*ps)


def _row_tile(w):
    return w if w <= 512 else _tile(w, (512, 256, 128))


def _rs_chip_sum(g, r1):
    a, _, w, c = g.shape
    tr = _row_tile(w)

    def body(g_ref, r_ref, o_ref):
        o_ref[...] = (g_ref[...].astype(F32) + r_ref[...].astype(F32)).astype(o_ref.dtype)

    blk = pl.BlockSpec((None, None, tr, c), lambda j, aa, i: (j, aa, i, 0))
    return pl.pallas_call(
        body, name="rs_chip_sum", grid=(4, a, w // tr),
        in_specs=[pl.BlockSpec((None, None, tr, c), lambda j, aa, i: (aa, 2 * j + lax.axis_index("c"), i, 0)), blk],
        out_specs=blk, out_shape=jax.ShapeDtypeStruct((4, a, w, c), g.dtype),
        compiler_params=_params(("parallel", "parallel", "parallel")),
    )(g, r1)


def _rs_final_sum(g, r1, r2):
    a, _, w, c = g.shape
    tr = _row_tile(w)

    def body(g_ref, r1_ref, r2_ref, o_ref):
        s = g_ref[...].astype(F32) + r1_ref[...].astype(F32)
        for k in range(3):
            s = s + r2_ref[k].astype(F32)
        o_ref[...] = s

    def device():
        return 4 * lax.axis_index("x") + 2 * lax.axis_index("y") + lax.axis_index("c")

    def chip():
        return 2 * lax.axis_index("x") + lax.axis_index("y")

    return pl.pallas_call(
        body, name="rs_final_sum", grid=(a, w // tr),
        in_specs=[pl.BlockSpec((None, None, tr, c), lambda aa, i: (aa, device(), i, 0)),
                  pl.BlockSpec((None, None, tr, c), lambda aa, i: (chip(), aa, i, 0)),
                  pl.BlockSpec((3, None, tr, c), lambda aa, i: (0, aa, i, 0))],
        out_specs=pl.BlockSpec((None, tr, c), lambda aa, i: (aa, i, 0)),
        out_shape=jax.ShapeDtypeStruct((a, w, c), F32),
        compiler_params=_params(("parallel", "parallel")),
    )(g, r1, r2)


def _sum_slots(g):
    _, n, r, c = g.shape

    def body(g_ref, o_ref):
        s = g_ref[0, 0]
        for k in range(1, n):
            s = s + g_ref[0, k]
        o_ref[...] = s

    return pl.pallas_call(
        body, name="sum_slots", grid=(1,),
        in_specs=[pl.BlockSpec((1, n, r, c), lambda i: (0, 0, 0, 0))], out_specs=pl.BlockSpec((r, c), lambda i: (0, 0)),
        out_shape=jax.ShapeDtypeStruct((r, c), F32), compiler_params=_params(("arbitrary",)),
    )(g)


_WEIGHTS = ['ffn1_norm', 'ffn1_w_gate', 'ffn1_w_up', 'ffn1_w_down', 'mix_norm', 'ffn2_norm', 'ffn2_w_gate', 'ffn2_w_up',
            'ffn2_w_down', 'ev_w_in', 'hg_lb_logits', 'hg_norm_w', 's5_a_re', 's5_a_im', 's5_b_re', 's5_b_im', 's5_c_re',
            's5_c_im', 's5_d', 's5_log_dt', 's5_w_glu', 'ev_w_out', 'od_w_in', 'gdn_conv_w', 'gdn_a_log', 'gdn_dt_bias',
            'gdn_norm_w', 'od_w_out', 'final_norm']
_SHARDED = ['ffn1_w_gate', 'ffn1_w_up', 'ffn1_w_down', 'ffn2_w_gate', 'ffn2_w_up', 'ffn2_w_down', 'ev_w_in', 's5_w_glu',
            'ev_w_out', 'od_w_in', 'gdn_conv_w', 'od_w_out']
_REPLICATED = [n for n in _WEIGHTS if n not in _SHARDED]
_SMALL_COLS = 1024


def _pad_rows(a, rows):
    return jnp.pad(a, ((0, 0), (0, rows - a.shape[1]), (0, 0)))


def _gather_weights(w):
    t_ = lambda a: jnp.swapaxes(a, 1, 2)
    ffn = []
    for f in ('ffn1', 'ffn2'):
        ffn += [_pad_rows(t_(w[f + '_w_gate']), FF_SHARD_PAD), _pad_rows(t_(w[f + '_w_up']), FF_SHARD_PAD),
                _pad_rows(w[f + '_w_down'], FF_SHARD_PAD)]
    blocks = [jnp.concatenate(ffn).astype(BF16),
              t_(w['ev_w_in']).astype(BF16),
              w['s5_w_glu'].astype(BF16),
              jnp.concatenate([w['ev_w_out'], w['od_w_out']]).astype(BF16),
              w['od_w_in'].astype(BF16).reshape(1, 2 * D_MODEL, OD_IN // N_DEV),
              w['gdn_conv_w'].reshape(1, 2 * CONV_W, GDN_QKV // N_DEV)]
    ffn_w, in_t, glu, w_out, od_in, conv = _all_gather(blocks, "gather_weights")
    whole = lambda g: g.reshape(g.shape[0], N_DEV * g.shape[2], g.shape[3])
    od_in = jnp.moveaxis(od_in.reshape(N_DEV, 2, D_MODEL, OD_IN // N_DEV), 0, 2).reshape(2, D_MODEL, OD_IN)
    conv = jnp.moveaxis(conv.reshape(N_DEV, 2, CONV_W, GDN_QKV // N_DEV), 0, 2).reshape(2, CONV_W, GDN_QKV)
    return dict(ffn=whole(ffn_w), ev_w_in_t=whole(in_t), s5_w_glu=whole(glu), w_out=whole(w_out),
                od_w_in=jnp.pad(od_in, ((0, 0), (0, 0), (0, OD_IN_PAD - OD_IN))), gdn_conv_w=conv)


def _scatter_grads(gbig):
    windows = lambda g, w: g.reshape(g.shape[0], N_DEV, w, g.shape[2])
    od_in = jnp.moveaxis(gbig['od_w_in'].reshape(2, D_MODEL, N_DEV, OD_IN // N_DEV), 2, 0)
    conv = jnp.moveaxis(gbig['gdn_conv_w'].reshape(2, CONV_W, N_DEV, GDN_QKV // N_DEV), 2, 0)
    gs = [windows(gbig['ffn'], FF_SHARD_PAD), windows(gbig['ev_w_in_t'], EV_IN // N_DEV),
          windows(gbig['s5_w_glu'], S5_WIDTH // N_DEV), windows(gbig['w_out'], D_MODEL // N_DEV),
          od_in.reshape(1, N_DEV, 2 * D_MODEL, OD_IN // N_DEV), conv.reshape(1, N_DEV, 2 * CONV_W, GDN_QKV // N_DEV)]
    r1 = _rs_core_exchange(gs)
    ps = [_rs_chip_sum(g, r) for g, r in zip(gs, r1)]
    r2 = _rs_chip_exchange(ps)
    ffn, in_t, glu, w_out, od_in, conv = (_rs_final_sum(g, ra, rb) for g, ra, rb in zip(gs, r1, r2))
    t_ = lambda a: jnp.swapaxes(a, 1, 2)
    ffn = ffn[:, :FF_SHARD].reshape(2, 3, DEPTH, FF_SHARD, D_MODEL)
    out = {}
    for i, f in enumerate(('ffn1', 'ffn2')):
        out[f + '_w_gate'], out[f + '_w_up'], out[f + '_w_down'] = t_(ffn[i, 0]), t_(ffn[i, 1]), ffn[i, 2]
    out['ev_w_in'] = t_(in_t)
    out['s5_w_glu'] = glu
    out['ev_w_out'], out['od_w_out'] = w_out[:2], w_out[2:]
    out['od_w_in'] = od_in.reshape(2, D_MODEL, OD_IN // N_DEV)
    out['gdn_conv_w'] = conv.reshape(2, CONV_W, GDN_QKV // N_DEV)
    return out


def _flat_small(src, prefix=''):
    flat = jnp.concatenate([src[prefix + n].reshape(-1) for n in _REPLICATED])
    rows = -(-(flat.shape[0] + 1) // (_SMALL_COLS * 8)) * 8
    return jnp.pad(flat, (0, rows * _SMALL_COLS - flat.shape[0])).reshape(rows, _SMALL_COLS)


def _split_small(flat2d, like):
    flat, out, off = flat2d.reshape(-1), {}, 0
    for n in _REPLICATED:
        size = math.prod(like[n].shape)
        out[n] = flat[off:off + size].reshape(like[n].shape)
        off += size
    return out, flat[off]


def _all_reduce_small(gsmall, loss):
    flat = _flat_small(gsmall)
    n_used = sum(math.prod(gsmall[n].shape) for n in _REPLICATED)
    flat = flat.reshape(-1).at[n_used].set(loss).reshape(flat.shape)
    (gathered,) = _all_gather([flat[None]], "gather_small")
    return _split_small(_sum_slots(gathered), gsmall)


def _as_2d(a):
    return a.reshape(-1, a.shape[-1])


def kernel(x, ffn1_norm, ffn1_w_gate, ffn1_w_up, ffn1_w_down, mix_norm, ffn2_norm, ffn2_w_gate, ffn2_w_up, ffn2_w_down, ev_w_in, hg_lb_logits, hg_norm_w, s5_a_re, s5_a_im, s5_b_re, s5_b_im, s5_c_re, s5_c_im, s5_d, s5_log_dt, s5_w_glu, ev_w_out, od_w_in, gdn_conv_w, gdn_a_log, gdn_dt_bias, gdn_norm_w, od_w_out, final_norm, loss_target, m_ffn1_norm, m_ffn1_w_gate, m_ffn1_w_up, m_ffn1_w_down, m_mix_norm, m_ffn2_norm, m_ffn2_w_gate, m_ffn2_w_up, m_ffn2_w_down, m_ev_w_in, m_hg_lb_logits, m_hg_norm_w, m_s5_a_re, m_s5_a_im, m_s5_b_re, m_s5_b_im, m_s5_c_re, m_s5_c_im, m_s5_d, m_s5_log_dt, m_s5_w_glu, m_ev_w_out, m_od_w_in, m_gdn_conv_w, m_gdn_a_log, m_gdn_dt_bias, m_gdn_norm_w, m_od_w_out, m_final_norm, v_ffn1_norm, v_ffn1_w_gate, v_ffn1_w_up, v_ffn1_w_down, v_mix_norm, v_ffn2_norm, v_ffn2_w_gate, v_ffn2_w_up, v_ffn2_w_down, v_ev_w_in, v_hg_lb_logits, v_hg_norm_w, v_s5_a_re, v_s5_a_im, v_s5_b_re, v_s5_b_im, v_s5_c_re, v_s5_c_im, v_s5_d, v_s5_log_dt, v_s5_w_glu, v_ev_w_out, v_od_w_in, v_gdn_conv_w, v_gdn_a_log, v_gdn_dt_bias, v_gdn_norm_w, v_od_w_out, v_final_norm):
    given = dict(locals())
    w = {n: given[n] for n in _WEIGHTS}
    big = _gather_weights(w)
    small = {n: w[n] for n in _REPLICATED}
    loss_part, grad_x, gbig, gsmall = _local_step(given['x'], given['loss_target'], big, small)
    grads = _scatter_grads(gbig)
    gsum, loss = _all_reduce_small(gsmall, loss_part)
    grads.update(gsum)

    delta, new_m, new_v = {}, {}, {}
    for n in _SHARDED:
        d, nm, nv = _adamw(_as_2d(w[n]), _as_2d(grads[n]), _as_2d(given['m_' + n]), _as_2d(given['v_' + n]))
        delta[n], new_m[n], new_v[n] = (a.reshape(w[n].shape) for a in (d, nm, nv))
    d, nm, nv = _adamw(_flat_small(w), _flat_small(grads), _flat_small(given, 'm_'), _flat_small(given, 'v_'))
    (delta_s, _), (new_m_s, _), (new_v_s, _) = (_split_small(a, small) for a in (d, nm, nv))
    delta.update(delta_s)
    new_m.update(new_m_s)
    new_v.update(new_v_s)
    return (loss, grad_x, *[grads[n] for n in _WEIGHTS], *[delta[n] for n in _WEIGHTS],
            *[new_m[n] for n in _WEIGHTS], *[new_v[n] for n in _WEIGHTS])
```

```python
import math

import jax
import jax.numpy as jnp
from jax import lax
from jax.experimental import pallas as pl
from jax.experimental.pallas import tpu as pltpu

F32 = jnp.float32
BF16 = jnp.bfloat16
HI = lax.Precision.HIGH

D_MODEL = 1024
DEPTH = 4
D_FF = 2816
NORM_EPS = 1e-6
F_MIN = 1e-6
CHUNK = 64
HG_HEADS = 4
HEAD_DIM = 128
HG_QK = HG_HEADS * HEAD_DIM
S5_WIDTH = 512
S5_GROUP = 16
S5_GROUPS = 32
S5_STATE = 64
S5_NSTATE = S5_GROUPS * S5_STATE
EV_IN = 2560
GDN_HEADS = 8
GDN_QKV = 3 * GDN_HEADS * HEAD_DIM
CONV_W = 4
OD_IN = 4112
OD_IN_PAD = 4224
N_DEV = 8
LANES = 128
ADAM_LR, ADAM_B1, ADAM_B2, ADAM_EPS, ADAM_WD, ADAM_STEP = 0.001, 0.9, 0.999, 1e-08, 0.01, 10
VMEM_LIMIT = 56 * 1024 * 1024

MESH = pl.DeviceIdType.MESH


def _params(sem=None, **kw):
    return pltpu.CompilerParams(dimension_semantics=sem, vmem_limit_bytes=VMEM_LIMIT, **kw)


def _tile(n, cands):
    for c in cands:
        if n % c == 0:
            return c
    return n


def _dot(a, b):
    return jnp.dot(a.astype(BF16), b.astype(BF16), preferred_element_type=F32)


def _dot_nt(a, b):
    return lax.dot_general(a.astype(BF16), b.astype(BF16), (((1,), (1,)), ((), ())), preferred_element_type=F32)


def _dot_tn(a, b):
    return lax.dot_general(a.astype(BF16), b.astype(BF16), (((0,), (0,)), ((), ())), preferred_element_type=F32)


def _dot_hi(a, b):
    return jnp.dot(a, b, precision=HI, preferred_element_type=F32)


def _bmm(a, b):
    return jnp.einsum('gmk,gkn->gmn', a.astype(BF16), b.astype(BF16), preferred_element_type=F32)


def _bmm_nt(a, b):
    return jnp.einsum('gmk,gnk->gmn', a.astype(BF16), b.astype(BF16), preferred_element_type=F32)


def _bmm_tn(a, b):
    return jnp.einsum('gkm,gkn->gmn', a.astype(BF16), b.astype(BF16), preferred_element_type=F32)


def _bmm_hi(a, b):
    return jnp.einsum('gmk,gkn->gmn', a, b, precision=HI, preferred_element_type=F32)


_TN_CANDS = (1408, 1280, 1024, 512, 384, 256, 128)
_B_TILE_BYTES = 6 * 1024 * 1024


def _mm(a, b, *, name, nt=False, out_dtype=F32, res=None, scale=1.0):
    m, k = a.shape
    n = b.shape[0] if nt else b.shape[1]
    tm = _tile(m, (512, 256, 128))
    tn = _tile(n, [c for c in _TN_CANDS if c * k * b.dtype.itemsize <= _B_TILE_BYTES])

    def body(*refs):
        if res is None:
            a_ref, b_ref, o_ref = refs
        else:
            a_ref, b_ref, r_ref, o_ref = refs
        acc = (_dot_nt if nt else _dot)(a_ref[...], b_ref[...])
        if scale != 1.0:
            acc = scale * acc
        if res is not None:
            acc = r_ref[...] + acc
        o_ref[...] = acc.astype(out_dtype)

    b_spec = pl.BlockSpec((tn, k), lambda j, i: (j, 0)) if nt else pl.BlockSpec((k, tn), lambda j, i: (0, j))
    in_specs = [pl.BlockSpec((tm, k), lambda j, i: (i, 0)), b_spec]
    args = [a, b]
    if res is not None:
        in_specs.append(pl.BlockSpec((tm, tn), lambda j, i: (i, j)))
        args.append(res)
    return pl.pallas_call(
        body, name=name, grid=(n // tn, m // tm), in_specs=in_specs,
        out_specs=pl.BlockSpec((tm, tn), lambda j, i: (i, j)),
        out_shape=jax.ShapeDtypeStruct((m, n), out_dtype),
        compiler_params=_params(("parallel", "parallel")),
    )(*args)


def _mm_tn(a, b, *, name, scale=1.0, out_dtype=F32, into=None, slot=0):
    t, m = a.shape
    n = b.shape[1]
    tm = _tile(m, (1024, 512, 256, 128))
    tn = _tile(n, _TN_CANDS)
    tk = _tile(t, (2048, 1024, 512, 256, 128))
    nk = t // tk

    def body(*refs):
        a_ref, b_ref = refs[:2]
        o_ref, acc_ref = refs[-2:]
        kk = pl.program_id(2)

        @pl.when(kk == 0)
        def _():
            acc_ref[...] = jnp.zeros_like(acc_ref)

        acc_ref[...] += _dot_tn(a_ref[...], b_ref[...])

        @pl.when(kk == nk - 1)
        def _():
            o_ref[...] = (acc_ref[...] * scale if scale != 1.0 else acc_ref[...]).astype(o_ref.dtype)

    in_specs = [pl.BlockSpec((tk, tm), lambda i, j, kk: (kk, i)), pl.BlockSpec((tk, tn), lambda i, j, kk: (kk, j))]
    args = [a, b]
    if into is None:
        out_spec = pl.BlockSpec((tm, tn), lambda i, j, kk: (i, j))
        out_shape = jax.ShapeDtypeStruct((m, n), out_dtype)
        alias = {}
    else:
        in_specs.append(pl.BlockSpec(memory_space=pl.ANY))
        args.append(into)
        out_spec = pl.BlockSpec((None, tm, tn), lambda i, j, kk: (slot, i, j))
        out_shape = jax.ShapeDtypeStruct(into.shape, into.dtype)
        alias = {2: 0}
    return pl.pallas_call(
        body, name=name, grid=(m // tm, n // tn, nk), in_specs=in_specs, out_specs=out_spec, out_shape=out_shape,
        scratch_shapes=[pltpu.VMEM((tm, tn), F32)], input_output_aliases=alias,
        compiler_params=_params(("parallel", "parallel", "arbitrary")),
    )(*args)


def _sigmoid(x):
    return jax.nn.sigmoid(x)


def _head_norm_gate(o, gate, nw):
    r = lax.rsqrt(jnp.mean(o * o, axis=-1, keepdims=True) + NORM_EPS)
    return o * r * nw * (gate * _sigmoid(gate))


def _hg_chunk(st, ql, fl, iv, gl, lb, nw):
    n_g, c, _ = ql.shape
    q = ql * _sigmoid(ql)
    f = lb + (1.0 - lb) * _sigmoid(fl)
    lf = jnp.log(jnp.maximum(f, F_MIN))
    k = 1.0 - f
    ri = lax.broadcasted_iota(jnp.int32, (c, c), 0)
    ci = lax.broadcasted_iota(jnp.int32, (c, c), 1)
    rl = lax.broadcasted_iota(jnp.int32, (c, LANES), 0)
    each = lambda m2: jnp.broadcast_to(m2[None], (n_g,) + m2.shape)
    b = _bmm_hi(each(jnp.where(ci <= ri, 1.0, 0.0)), lf)
    attn = jnp.where((ri == ci)[None], _bmm_nt(q, k), 0.0)
    sh = 0
    while (1 << sh) < c:
        m = 1 << sh
        ref = ((ri >> (sh + 1)) << (sh + 1)) + (m - 1)
        low_r = ((ri >> sh) & 1) == 1
        low_c = ((ci >> sh) & 1) == 1
        w_low = low_r & (ci > ref) & (ci <= ri)
        w_up = jnp.logical_not(low_r) & (ci > ri) & (ci <= ref)
        w = jnp.where(w_low | w_up, 1.0, 0.0)
        e = jnp.exp(_bmm_hi(each(w), lf))
        low_l = (((rl >> sh) & 1) == 1)[None]
        qs = jnp.where(low_l, q * e, 0.0)
        ks = jnp.where(low_l, 0.0, k * e)
        pair = ((ri >> (sh + 1)) == (ci >> (sh + 1))) & low_r & jnp.logical_not(low_c)
        attn = attn + jnp.where(pair[None], _bmm_nt(qs, ks), 0.0)
        sh += 1
    bl = jnp.sum(lf, axis=1, keepdims=True)
    o = _bmm(attn, iv) + _bmm_nt(q * jnp.exp(b), st)
    st_new = st * jnp.exp(bl) + _bmm_tn(iv, k * jnp.exp(bl - b))
    return _head_norm_gate(o, gl, nw), st_new


def _hg_specs(nb, n_c, rev):
    def cidx(cc):
        return (n_c - 1 - cc) if rev else cc

    def col(width):
        return pl.BlockSpec((nb, CHUNK, width), lambda cc: (0, cidx(cc), 0))

    vec = pl.BlockSpec((HG_HEADS, 1, LANES), lambda cc: (0, 0, 0))
    nw = pl.BlockSpec((1, LANES), lambda cc: (0, 0))
    st = pl.BlockSpec((nb, HG_HEADS, None, LANES, LANES), lambda cc: (0, 0, cidx(cc), 0, 0))
    acc = pl.BlockSpec((nb, HG_HEADS, 1, LANES), lambda cc: (0, 0, 0, 0))
    return col, vec, nw, st, acc


def _head(j):
    return slice(j * LANES, (j + 1) * LANES)


def _seq_heads(ref, nb, n, first=0):
    return jnp.stack([ref[b, :, _head(first + j)] for b in range(nb) for j in range(n)])


def _put_seq_heads(ref, val, nb, n, first=0):
    for b in range(nb):
        for j in range(n):
            ref[b, :, _head(first + j)] = val[b * n + j].astype(ref.dtype)


def _hgrn2_fwd(proj, lb, nw):
    nb, l, _ = proj.shape
    n_c = l // CHUNK
    n_g = nb * HG_HEADS
    col, vec, nws, st, _ = _hg_specs(nb, n_c, False)

    def body(p_ref, lb_ref, nw_ref, y_ref, st_ref, s_scr):
        @pl.when(pl.program_id(0) == 0)
        def _():
            s_scr[...] = jnp.zeros_like(s_scr)

        s_in = s_scr[...]
        st_ref[...] = s_in.reshape(st_ref.shape)
        args = [_seq_heads(p_ref, nb, HG_HEADS, k * HG_HEADS) for k in range(4)]
        y, s_new = _hg_chunk(s_in, *args, jnp.concatenate([lb_ref[...]] * nb), nw_ref[...])
        _put_seq_heads(y_ref, y, nb, HG_HEADS)
        s_scr[...] = s_new

    return pl.pallas_call(
        body, name="hgrn2_fwd", grid=(n_c,),
        in_specs=[col(4 * HG_QK), vec, nws], out_specs=[col(HG_QK), st],
        out_shape=[jax.ShapeDtypeStruct((nb, l, HG_QK), BF16),
                   jax.ShapeDtypeStruct((nb, HG_HEADS, n_c, LANES, LANES), F32)],
        scratch_shapes=[pltpu.VMEM((n_g, LANES, LANES), F32)],
        compiler_params=_params(("arbitrary",)),
    )(proj, lb, nw)


def _hgrn2_bwd(proj, lb, nw, states, dy):
    nb, l, _ = proj.shape
    n_c = l // CHUNK
    n_g = nb * HG_HEADS
    col, vec, nws, st, acc = _hg_specs(nb, n_c, True)

    def body(p_ref, lb_ref, nw_ref, st_ref, dy_ref, dp_ref, dlb_ref, dnw_ref, ds_scr):
        @pl.when(pl.program_id(0) == 0)
        def _():
            ds_scr[...] = jnp.zeros_like(ds_scr)
            dlb_ref[...] = jnp.zeros_like(dlb_ref)
            dnw_ref[...] = jnp.zeros_like(dnw_ref)

        args = [_seq_heads(p_ref, nb, HG_HEADS, k * HG_HEADS) for k in range(4)]
        _, vjp = jax.vjp(_hg_chunk, st_ref[...].reshape(n_g, LANES, LANES), *args,
                         jnp.concatenate([lb_ref[...]] * nb), nw_ref[...])
        ds, dql, dfl, div, dgl, dlb, dnw = vjp((_seq_heads(dy_ref, nb, HG_HEADS).astype(F32), ds_scr[...]))
        ds_scr[...] = ds
        for k, val in enumerate((dql, dfl, div, dgl)):
            _put_seq_heads(dp_ref, val, nb, HG_HEADS, k * HG_HEADS)
        dlb_ref[...] += dlb.reshape(dlb_ref.shape)
        dnw_ref[...] += dnw

    return pl.pallas_call(
        body, name="hgrn2_bwd", grid=(n_c,),
        in_specs=[col(4 * HG_QK), vec, nws, st, col(HG_QK)], out_specs=[col(4 * HG_QK), acc, nws],
        out_shape=[jax.ShapeDtypeStruct((nb, l, 4 * HG_QK), BF16),
                   jax.ShapeDtypeStruct((nb, HG_HEADS, 1, LANES), F32), jax.ShapeDtypeStruct((1, LANES), F32)],
        scratch_shapes=[pltpu.VMEM((n_g, LANES, LANES), F32)],
        compiler_params=_params(("arbitrary",)),
    )(proj, lb, nw, states, dy)


def _shift_down(x, s):
    if s == 0:
        return x
    rows = lax.broadcasted_iota(jnp.int32, x.shape, 0)
    return jnp.where(rows >= s, pltpu.roll(x, s, 0), 0.0)


def _shift_up(x, s):
    if s == 0:
        return x
    n = x.shape[0]
    rows = lax.broadcasted_iota(jnp.int32, x.shape, 0)
    return jnp.where(rows < n - s, pltpu.roll(x, n - s, 0), 0.0)


def _gdn_pre_fwd(proj, conv_w):
    nb, l, _ = proj.shape
    n_blk = GDN_QKV // LANES

    def body(x_ref, w_ref, o_ref):
        j = pl.program_id(1)
        x = x_ref[...]
        w = w_ref[...]
        c = w[3:4] * x
        for t in range(CONV_W - 1):
            c = c + w[t:t + 1] * _shift_down(x, CONV_W - 1 - t)
        s = c * _sigmoid(c)
        r = lax.rsqrt(jnp.sum(s * s, axis=-1, keepdims=True) + NORM_EPS)
        scale = jnp.where(j < GDN_HEADS, HEAD_DIM ** -0.5, 1.0)
        o_ref[...] = jnp.where(j < 2 * GDN_HEADS, s * r * scale, s)

    return pl.pallas_call(
        body, name="gdn_pre_fwd", grid=(nb, n_blk),
        in_specs=[pl.BlockSpec((None, l, LANES), lambda b, j: (b, 0, j)), pl.BlockSpec((CONV_W, LANES), lambda b, j: (0, j))],
        out_specs=pl.BlockSpec((None, l, LANES), lambda b, j: (b, 0, j)),
        out_shape=jax.ShapeDtypeStruct((nb, l, GDN_QKV), F32),
        compiler_params=_params(("parallel", "parallel")),
    )(proj, conv_w)


def _gdn_pre_bwd(proj, conv_w, dout):
    nb, l, _ = proj.shape
    n_blk = GDN_QKV // LANES

    def body(x_ref, w_ref, d_ref, dx_ref, dw_ref):
        j = pl.program_id(1)
        x = x_ref[...]
        w = w_ref[...]
        xs = [_shift_down(x, CONV_W - 1 - t) for t in range(CONV_W)]
        c = w[0:1] * xs[0]
        for t in range(1, CONV_W):
            c = c + w[t:t + 1] * xs[t]
        sg = _sigmoid(c)
        s = c * sg
        d = d_ref[...]
        r = lax.rsqrt(jnp.sum(s * s, axis=-1, keepdims=True) + NORM_EPS)
        scale = jnp.where(j < GDN_HEADS, HEAD_DIM ** -0.5, 1.0)
        ds_norm = scale * r * (d - s * (r * r) * jnp.sum(d * s, axis=-1, keepdims=True))
        ds = jnp.where(j < 2 * GDN_HEADS, ds_norm, d)
        dc = ds * (sg * (1.0 + c * (1.0 - sg)))
        dx = w[3:4] * dc
        for t in range(CONV_W - 1):
            dx = dx + w[t:t + 1] * _shift_up(dc, CONV_W - 1 - t)
        dx_ref[...] = dx.astype(dx_ref.dtype)
        for t in range(CONV_W):
            dw_ref[t:t + 1, :] = jnp.sum(dc * xs[t], axis=0, keepdims=True)

    return pl.pallas_call(
        body, name="gdn_pre_bwd", grid=(nb, n_blk),
        in_specs=[pl.BlockSpec((None, l, LANES), lambda b, j: (b, 0, j)), pl.BlockSpec((CONV_W, LANES), lambda b, j: (0, j)),
                  pl.BlockSpec((None, l, LANES), lambda b, j: (b, 0, j))],
        out_specs=[pl.BlockSpec((None, l, LANES), lambda b, j: (b, 0, j)), pl.BlockSpec((None, CONV_W, LANES), lambda b, j: (b, 0, j))],
        out_shape=[jax.ShapeDtypeStruct((nb, l, GDN_QKV), BF16), jax.ShapeDtypeStruct((nb, CONV_W, GDN_QKV), F32)],
        compiler_params=_params(("parallel", "parallel")),
    )(proj, conv_w, dout)


def _gdn_chunk(st, q, k, v, gate, bl, al, alog, dtb, nw):
    n_g, c, _ = q.shape
    beta = _sigmoid(bl)
    x = al + dtb
    softplus = jnp.maximum(x, 0.0) + jnp.log(1.0 + jnp.exp(-jnp.abs(x)))
    la = -jnp.exp(alog) * softplus
    ri = lax.broadcasted_iota(jnp.int32, (c, c), 0)
    ci = lax.broadcasted_iota(jnp.int32, (c, c), 1)
    lower = (ci <= ri)[None]
    strict = (ci < ri)[None]
    ltri = jnp.broadcast_to(jnp.where(lower, 1.0, 0.0), (n_g, c, c))
    la_l = la + jnp.zeros((1, 1, LANES), F32)
    g = _bmm_hi(ltri, la_l)
    delta = _bmm_hi(ltri, jnp.where(strict, la + jnp.zeros((1, 1, c), F32), 0.0))
    gam = jnp.where(lower, jnp.exp(jnp.where(lower, delta, 0.0)), 0.0)
    kb = k * beta
    vb = v * beta
    m = jnp.where(strict, _bmm_nt(kb, k) * gam, 0.0)
    pw = -m
    t_inv = jnp.where((ri == ci)[None], 1.0, 0.0) + pw
    steps = int(math.log2(c)) - 1
    for _ in range(steps):
        pw = _bmm_hi(pw, pw)
        t_inv = t_inv + _bmm_hi(t_inv, pw)
    eg = jnp.exp(g)
    u = _bmm(t_inv, vb)
    w = _bmm(t_inv, kb * eg)
    attn = jnp.where(lower, _bmm_nt(q, k) * gam, 0.0)
    v_new = u - _bmm_nt(w, st)
    o = _bmm_nt(q * eg, st) + _bmm(attn, v_new)
    g_last = jnp.sum(la_l, axis=1, keepdims=True)
    st_new = st * jnp.exp(g_last) + _bmm_tn(v_new, k * jnp.exp(g_last - g))
    return _head_norm_gate(o, gate, nw), st_new


GDN_WIDE = GDN_HEADS * LANES
GDN_SMALL_BLOCK = (GDN_QKV + GDN_WIDE) // LANES


def _gdn_specs(nb, n_c, rev):
    def cidx(cc):
        return (n_c - 1 - cc) if rev else cc

    def col(width, blk=0):
        return pl.BlockSpec((nb, CHUNK, width), lambda cc: (0, cidx(cc), blk))

    scal = pl.BlockSpec((GDN_HEADS, 1, 1), lambda cc: (0, 0, 0))
    nw = pl.BlockSpec((1, LANES), lambda cc: (0, 0))
    st = pl.BlockSpec((nb, GDN_HEADS, None, LANES, LANES), lambda cc: (0, 0, cidx(cc), 0, 0))
    acc_s = pl.BlockSpec((nb, GDN_HEADS, 1, 1), lambda cc: (0, 0, 0, 0))
    return col, scal, nw, st, acc_s


def _tok_cols(xs, nb, first):
    return jnp.stack([xs[b][:, first + h:first + h + 1] for b in range(nb) for h in range(GDN_HEADS)])


def _gdn_inputs(qkv_ref, p_gate_ref, p_small_ref, alog_ref, dtb_ref, nw_ref, nb):
    q, k, v = (_seq_heads(qkv_ref, nb, GDN_HEADS, i * GDN_HEADS) for i in range(3))
    xs = p_small_ref[...]
    return (q, k, v, _seq_heads(p_gate_ref, nb, GDN_HEADS), _tok_cols(xs, nb, 0), _tok_cols(xs, nb, GDN_HEADS),
            jnp.concatenate([alog_ref[...]] * nb), jnp.concatenate([dtb_ref[...]] * nb), nw_ref[...])


def _gdn_fwd(qkv, proj, alog, dtb, nw):
    nb, l, _ = qkv.shape
    n_c = l // CHUNK
    n_g = nb * GDN_HEADS
    col, scal, nws, st, _ = _gdn_specs(nb, n_c, False)

    def body(qkv_ref, pg_ref, ps_ref, alog_ref, dtb_ref, nw_ref, y_ref, st_ref, s_scr):
        @pl.when(pl.program_id(0) == 0)
        def _():
            s_scr[...] = jnp.zeros_like(s_scr)

        s_in = s_scr[...]
        st_ref[...] = s_in.reshape(st_ref.shape)
        y, s_new = _gdn_chunk(s_in, *_gdn_inputs(qkv_ref, pg_ref, ps_ref, alog_ref, dtb_ref, nw_ref, nb))
        _put_seq_heads(y_ref, y, nb, GDN_HEADS)
        s_scr[...] = s_new

    return pl.pallas_call(
        body, name="gdn_fwd", grid=(n_c,),
        in_specs=[col(GDN_QKV), col(GDN_WIDE, GDN_QKV // GDN_WIDE), col(LANES, GDN_SMALL_BLOCK), scal, scal, nws],
        out_specs=[col(GDN_WIDE), st],
        out_shape=[jax.ShapeDtypeStruct((nb, l, GDN_WIDE), BF16),
                   jax.ShapeDtypeStruct((nb, GDN_HEADS, n_c, LANES, LANES), F32)],
        scratch_shapes=[pltpu.VMEM((n_g, LANES, LANES), F32)],
        compiler_params=_params(("arbitrary",)),
    )(qkv, proj, proj, alog, dtb, nw)


def _gdn_bwd(qkv, proj, alog, dtb, nw, states, dy):
    nb, l, _ = qkv.shape
    n_c = l // CHUNK
    n_g = nb * GDN_HEADS
    col, scal, nws, st, acc_s = _gdn_specs(nb, n_c, True)

    def body(qkv_ref, pg_ref, ps_ref, alog_ref, dtb_ref, nw_ref, st_ref, dy_ref,
             dqkv_ref, dg_ref, dsm_ref, dalog_ref, ddtb_ref, dnw_ref, ds_scr):
        @pl.when(pl.program_id(0) == 0)
        def _():
            ds_scr[...] = jnp.zeros_like(ds_scr)
            dalog_ref[...] = jnp.zeros_like(dalog_ref)
            ddtb_ref[...] = jnp.zeros_like(ddtb_ref)
            dnw_ref[...] = jnp.zeros_like(dnw_ref)

        _, vjp = jax.vjp(_gdn_chunk, st_ref[...].reshape(n_g, LANES, LANES),
                         *_gdn_inputs(qkv_ref, pg_ref, ps_ref, alog_ref, dtb_ref, nw_ref, nb))
        ds, dq, dk, dv, dg, dbl, dal, dalog, ddtb, dnw = vjp((_seq_heads(dy_ref, nb, GDN_HEADS).astype(F32), ds_scr[...]))
        ds_scr[...] = ds
        for i, val in enumerate((dq, dk, dv)):
            _put_seq_heads(dqkv_ref, val, nb, GDN_HEADS, i * GDN_HEADS)
        _put_seq_heads(dg_ref, dg, nb, GDN_HEADS)
        lane = lax.broadcasted_iota(jnp.int32, (CHUNK, LANES), 1)
        for b in range(nb):
            small = jnp.zeros((CHUNK, LANES), F32)
            for h in range(GDN_HEADS):
                small = small + jnp.where(lane == h, dbl[b * GDN_HEADS + h], 0.0)
                small = small + jnp.where(lane == GDN_HEADS + h, dal[b * GDN_HEADS + h], 0.0)
            dsm_ref[b] = small.astype(dsm_ref.dtype)
        dalog_ref[...] += dalog.reshape(dalog_ref.shape)
        ddtb_ref[...] += ddtb.reshape(ddtb_ref.shape)
        dnw_ref[...] += dnw

    h = GDN_HEADS
    return pl.pallas_call(
        body, name="gdn_bwd", grid=(n_c,),
        in_specs=[col(GDN_QKV), col(GDN_WIDE, GDN_QKV // GDN_WIDE), col(LANES, GDN_SMALL_BLOCK), scal, scal, nws, st,
                  col(GDN_WIDE)],
        out_specs=[col(GDN_QKV), col(GDN_WIDE), col(LANES), acc_s, acc_s, nws],
        out_shape=[jax.ShapeDtypeStruct((nb, l, GDN_QKV), F32), jax.ShapeDtypeStruct((nb, l, GDN_WIDE), BF16),
                   jax.ShapeDtypeStruct((nb, l, LANES), BF16),
                   jax.ShapeDtypeStruct((nb, h, 1, 1), F32), jax.ShapeDtypeStruct((nb, h, 1, 1), F32),
                   jax.ShapeDtypeStruct((1, LANES), F32)],
        scratch_shapes=[pltpu.VMEM((n_g, LANES, LANES), F32)],
        compiler_params=_params(("arbitrary",)),
    )(qkv, proj, proj, alog, dtb, nw, states, dy)


S5_ROWS = S5_NSTATE // LANES
S5_TB = 256


def _s5_scan_fwd(bu, abar):
    nb, l = bu.shape[:2]
    tb = min(S5_TB, l)

    def body(bu_ref, a_ref, h_ref, c_scr):
        @pl.when(pl.program_id(1) == 0)
        def _():
            c_scr[...] = jnp.zeros_like(c_scr)

        ar = a_ref[0]
        ai = a_ref[1]

        def step(t, carry):
            hr, hi = carry
            nr = ar * hr - ai * hi + bu_ref[t, 0].astype(F32)
            ni = ar * hi + ai * hr + bu_ref[t, 1].astype(F32)
            h_ref[t, 0] = nr.astype(h_ref.dtype)
            h_ref[t, 1] = ni.astype(h_ref.dtype)
            return nr, ni

        hr, hi = lax.fori_loop(0, tb, step, (c_scr[0], c_scr[1]), unroll=8)
        c_scr[0] = hr
        c_scr[1] = hi

    blk = pl.BlockSpec((None, tb, 2, S5_ROWS, LANES), lambda b, i: (b, i, 0, 0, 0))
    return pl.pallas_call(
        body, name="s5_scan_fwd", grid=(nb, l // tb),
        in_specs=[blk, pl.BlockSpec((2, S5_ROWS, LANES), lambda b, i: (0, 0, 0))],
        out_specs=blk, out_shape=jax.ShapeDtypeStruct(bu.shape, bu.dtype),
        scratch_shapes=[pltpu.VMEM((2, S5_ROWS, LANES), F32)],
        compiler_params=_params(("parallel", "arbitrary")),
    )(bu, abar)


def _s5_scan_bwd(dh, h, abar):
    nb, l = dh.shape[:2]
    tb = min(S5_TB, l)
    n_blk = l // tb

    def body(dh_ref, h_ref, a_ref, g_ref, da_ref, c_scr):
        @pl.when(pl.program_id(1) == 0)
        def _():
            c_scr[...] = jnp.zeros_like(c_scr)
            da_ref[...] = jnp.zeros_like(da_ref)

        ar = a_ref[0]
        ai = a_ref[1]

        def step(s, carry):
            gr, gi, dar, dai = carry
            t = tb - 1 - s
            hr = h_ref[t, 0].astype(F32)
            hi = h_ref[t, 1].astype(F32)
            dar = dar + gr * hr + gi * hi
            dai = dai + gi * hr - gr * hi
            nr = ar * gr + ai * gi + dh_ref[t, 0].astype(F32)
            ni = ar * gi - ai * gr + dh_ref[t, 1].astype(F32)
            g_ref[t, 0] = nr.astype(g_ref.dtype)
            g_ref[t, 1] = ni.astype(g_ref.dtype)
            return nr, ni, dar, dai

        z = jnp.zeros((S5_ROWS, LANES), F32)
        gr, gi, dar, dai = lax.fori_loop(0, tb, step, (c_scr[0], c_scr[1], z, z), unroll=8)
        c_scr[0] = gr
        c_scr[1] = gi
        da_ref[0] += dar
        da_ref[1] += dai

    blk = pl.BlockSpec((None, tb, 2, S5_ROWS, LANES), lambda b, i: (b, n_blk - 1 - i, 0, 0, 0))
    return pl.pallas_call(
        body, name="s5_scan_bwd", grid=(nb, n_blk),
        in_specs=[blk, blk, pl.BlockSpec((2, S5_ROWS, LANES), lambda b, i: (0, 0, 0))],
        out_specs=[blk, pl.BlockSpec((None, 2, S5_ROWS, LANES), lambda b, i: (b, 0, 0, 0))],
        out_shape=[jax.ShapeDtypeStruct(dh.shape, dh.dtype), jax.ShapeDtypeStruct((nb, 2, S5_ROWS, LANES), F32)],
        scratch_shapes=[pltpu.VMEM((2, S5_ROWS, LANES), F32)],
        compiler_params=_params(("parallel", "arbitrary")),
    )(dh, h, abar)


_GELU_C = math.sqrt(2.0 / math.pi)


def _gelu(x):
    return 0.5 * x * (1.0 + jnp.tanh(_GELU_C * (x + 0.044715 * x * x * x)))


def _gelu_grad(x):
    t = jnp.tanh(_GELU_C * (x + 0.044715 * x * x * x))
    return 0.5 * (1.0 + t) + 0.5 * x * (1.0 - t * t) * _GELU_C * (1.0 + 3.0 * 0.044715 * x * x)


def _s5_post_fwd(yc, u, d, w_glu):
    t, w = yc.shape
    tt = _tile(t, (512, 256, 128))

    def body(yc_ref, u_ref, d_ref, w_ref, o_ref, y0_ref):
        y0 = yc_ref[...] + d_ref[...] * u_ref[...]
        y = _gelu(y0)
        z = _dot(y, w_ref[...])
        o_ref[...] = (y * _sigmoid(z)).astype(o_ref.dtype)
        y0_ref[...] = y0

    tok = pl.BlockSpec((tt, w), lambda i: (i, 0))
    return pl.pallas_call(
        body, name="s5_post_fwd", grid=(t // tt,),
        in_specs=[tok, tok, pl.BlockSpec((1, w), lambda i: (0, 0)), pl.BlockSpec((w, w), lambda i: (0, 0))],
        out_specs=[tok, tok],
        out_shape=[jax.ShapeDtypeStruct((t, w), BF16), jax.ShapeDtypeStruct((t, w), F32)],
        compiler_params=_params(("parallel",)),
    )(yc, u, d, w_glu)


def _s5_post_bwd(y0, u, d, w_glu, dout):
    t, w = y0.shape
    tt = _tile(t, (512, 256, 128))

    def body(y0_ref, u_ref, d_ref, w_ref, do_ref, dy0_ref, y_ref, dz_ref, du_ref, dd_ref):
        @pl.when(pl.program_id(0) == 0)
        def _():
            dd_ref[...] = jnp.zeros_like(dd_ref)

        y0 = y0_ref[...]
        y = _gelu(y0)
        s = _sigmoid(_dot(y, w_ref[...]))
        do = do_ref[...]
        dz = do * y * s * (1.0 - s)
        dy = do * s + _dot_nt(dz, w_ref[...])
        dy0 = dy * _gelu_grad(y0)
        dy0_ref[...] = dy0.astype(dy0_ref.dtype)
        y_ref[...] = y.astype(y_ref.dtype)
        dz_ref[...] = dz.astype(dz_ref.dtype)
        du_ref[...] = dy0 * d_ref[...]
        dd_ref[...] += jnp.sum(dy0 * u_ref[...], axis=0, keepdims=True)

    tok = pl.BlockSpec((tt, w), lambda i: (i, 0))
    vec = pl.BlockSpec((1, w), lambda i: (0, 0))
    return pl.pallas_call(
        body, name="s5_post_bwd", grid=(t // tt,),
        in_specs=[tok, tok, vec, pl.BlockSpec((w, w), lambda i: (0, 0)), tok],
        out_specs=[tok, tok, tok, tok, vec],
        out_shape=[jax.ShapeDtypeStruct((t, w), BF16), jax.ShapeDtypeStruct((t, w), BF16),
                   jax.ShapeDtypeStruct((t, w), BF16), jax.ShapeDtypeStruct((t, w), F32),
                   jax.ShapeDtypeStruct((1, w), F32)],
        compiler_params=_params(("arbitrary",)),
    )(y0, u, d, w_glu, dout)


def _s5_params(a_re, a_im, b_re, b_im, c_re, c_im, log_dt):
    dt = jnp.exp(log_dt)[:, None]
    mag = jnp.exp(dt * a_re)
    ang = dt * a_im
    abar_re = mag * jnp.cos(ang)
    abar_im = mag * jnp.sin(ang)
    den = a_re * a_re + a_im * a_im
    zr = abar_re - 1.0
    zi = abar_im
    coef_re = ((zr * a_re + zi * a_im) / den)[..., None]
    coef_im = ((zi * a_re - zr * a_im) / den)[..., None]
    bb_re = coef_re * b_re - coef_im * b_im
    bb_im = coef_re * b_im + coef_im * b_re
    eye = jnp.eye(S5_GROUPS, dtype=F32)

    def dense_in(bb):
        return jnp.einsum('gnp,gh->gphn', bb, eye).reshape(S5_WIDTH, S5_NSTATE)

    def dense_out(cc):
        return jnp.einsum('gpn,gh->gnhp', cc, eye).reshape(S5_NSTATE, S5_WIDTH)

    abar = jnp.stack([abar_re.reshape(S5_ROWS, LANES), abar_im.reshape(S5_ROWS, LANES)])
    bbd = jnp.concatenate([dense_in(bb_re), dense_in(bb_im)], axis=1)
    ccd = jnp.concatenate([dense_out(c_re), dense_out(-c_im)], axis=0)
    return abar, bbd, ccd


def _tok_tile(t):
    return _tile(t, (256, 128))


def _rms_fwd(x, w):
    t, d = x.shape
    tt = _tok_tile(t)

    def body(x_ref, w_ref, o_ref):
        xv = x_ref[...]
        r = lax.rsqrt(jnp.mean(xv * xv, axis=-1, keepdims=True) + NORM_EPS)
        o_ref[...] = (xv * r * w_ref[...]).astype(o_ref.dtype)

    tok = pl.BlockSpec((tt, d), lambda i: (i, 0))
    return pl.pallas_call(
        body, name="rms_fwd", grid=(t // tt,), in_specs=[tok, pl.BlockSpec((1, d), lambda i: (0, 0))],
        out_specs=tok, out_shape=jax.ShapeDtypeStruct((t, d), BF16), compiler_params=_params(("parallel",)),
    )(x, w)


def _rms_bwd_math(xv, wv, dy):
    r = lax.rsqrt(jnp.mean(xv * xv, axis=-1, keepdims=True) + NORM_EPS)
    xh = xv * r
    dxh = dy * wv
    dx = r * (dxh - xh * jnp.mean(dxh * xh, axis=-1, keepdims=True))
    return dx, jnp.sum(dy * xh, axis=0, keepdims=True)


def _rms_bwd(x, w, dy, dres):
    t, d = x.shape
    tt = _tok_tile(t)

    def body(x_ref, w_ref, dy_ref, dr_ref, dx_ref, dw_ref):
        @pl.when(pl.program_id(0) == 0)
        def _():
            dw_ref[...] = jnp.zeros_like(dw_ref)

        dx, dw = _rms_bwd_math(x_ref[...], w_ref[...], dy_ref[...])
        dx_ref[...] = dr_ref[...] + dx
        dw_ref[...] += dw

    tok = pl.BlockSpec((tt, d), lambda i: (i, 0))
    vec = pl.BlockSpec((1, d), lambda i: (0, 0))
    return pl.pallas_call(
        body, name="rms_bwd", grid=(t // tt,), in_specs=[tok, vec, tok, tok], out_specs=[tok, vec],
        out_shape=[jax.ShapeDtypeStruct((t, d), F32), jax.ShapeDtypeStruct((1, d), F32)],
        compiler_params=_params(("arbitrary",)),
    )(x, w, dy, dres)


def _loss_head(x, w, target):
    t, d = x.shape
    tt = _tok_tile(t)

    def body(x_ref, w_ref, t_ref, l_ref, dx_ref, dw_ref):
        @pl.when(pl.program_id(0) == 0)
        def _():
            l_ref[...] = jnp.zeros_like(l_ref)
            dw_ref[...] = jnp.zeros_like(dw_ref)

        xv = x_ref[...]
        wv = w_ref[...]
        r = lax.rsqrt(jnp.mean(xv * xv, axis=-1, keepdims=True) + NORM_EPS)
        err = xv * r * wv - t_ref[...]
        row = jnp.sum(err * err, axis=-1, keepdims=True)
        l_ref[...] += (0.5 / d) * jnp.sum(row, axis=0, keepdims=True)
        dx, dw = _rms_bwd_math(xv, wv, err * (1.0 / d))
        dx_ref[...] = dx
        dw_ref[...] += dw

    tok = pl.BlockSpec((tt, d), lambda i: (i, 0))
    vec = pl.BlockSpec((1, d), lambda i: (0, 0))
    return pl.pallas_call(
        body, name="loss_head", grid=(t // tt,), in_specs=[tok, vec, tok],
        out_specs=[pl.BlockSpec((1, 1), lambda i: (0, 0)), tok, vec],
        out_shape=[jax.ShapeDtypeStruct((1, 1), F32), jax.ShapeDtypeStruct((t, d), F32), jax.ShapeDtypeStruct((1, d), F32)],
        compiler_params=_params(("arbitrary",)),
    )(x, w, target)


FF_PAD = 3072
FF_SHARD = D_FF // N_DEV
FF_SHARD_PAD = FF_PAD // N_DEV


def _ffn_in(x, nw, wg_t, wu_t):
    t, d = x.shape
    tm = _tile(t, (512, 256, 128))
    tn = 1024

    def body(x_ref, nw_ref, wg_ref, wu_ref, h_ref, g_ref, u_ref, a_ref):
        @pl.when(pl.program_id(1) == 0)
        def _():
            xv = x_ref[...]
            r = lax.rsqrt(jnp.mean(xv * xv, axis=-1, keepdims=True) + NORM_EPS)
            h_ref[...] = (xv * r * nw_ref[...]).astype(h_ref.dtype)

        h = h_ref[...]
        g = _dot_nt(h, wg_ref[...])
        u = _dot_nt(h, wu_ref[...])
        g_ref[...] = g.astype(g_ref.dtype)
        u_ref[...] = u.astype(u_ref.dtype)
        a_ref[...] = (g * _sigmoid(g) * u).astype(a_ref.dtype)

    tok = pl.BlockSpec((tm, d), lambda i, j: (i, 0))
    wsp = pl.BlockSpec((tn, d), lambda i, j: (j, 0))
    wide = pl.BlockSpec((tm, tn), lambda i, j: (i, j))
    return pl.pallas_call(
        body, name="ffn_in", grid=(t // tm, FF_PAD // tn),
        in_specs=[tok, pl.BlockSpec((1, d), lambda i, j: (0, 0)), wsp, wsp], out_specs=[tok, wide, wide, wide],
        out_shape=[jax.ShapeDtypeStruct((t, d), BF16)] + [jax.ShapeDtypeStruct((t, FF_PAD), BF16)] * 3,
        compiler_params=_params(("parallel", "arbitrary")),
    )(x, nw, wg_t, wu_t)


def _ffn_out_dx(dxo, wd, g, u):
    t, d = dxo.shape
    tm = _tile(t, (512, 256, 128))
    tn = 1024

    def body(dx_ref, wd_ref, g_ref, u_ref, dg_ref, du_ref):
        da = 0.5 * _dot_nt(dx_ref[...], wd_ref[...])
        g = g_ref[...].astype(F32)
        s = _sigmoid(g)
        dg_ref[...] = (da * u_ref[...].astype(F32) * (s * (1.0 + g * (1.0 - s)))).astype(dg_ref.dtype)
        du_ref[...] = (da * g * s).astype(du_ref.dtype)

    wide = pl.BlockSpec((tm, tn), lambda j, i: (i, j))
    return pl.pallas_call(
        body, name="ffn_out_dx", grid=(FF_PAD // tn, t // tm),
        in_specs=[pl.BlockSpec((tm, d), lambda j, i: (i, 0)), pl.BlockSpec((tn, d), lambda j, i: (j, 0)), wide, wide],
        out_specs=[wide, wide], out_shape=[jax.ShapeDtypeStruct((t, FF_PAD), BF16)] * 2,
        compiler_params=_params(("parallel", "parallel")),
    )(dxo, wd, g, u)


def _ffn_in_dx(dg, du, wg_t, wu_t, x, nw, dxo):
    t, d = x.shape
    tm = _tile(t, (256, 128))

    def body(dg_ref, du_ref, wg_ref, wu_ref, x_ref, nw_ref, dr_ref, dx_ref, dw_ref):
        @pl.when(pl.program_id(0) == 0)
        def _():
            dw_ref[...] = jnp.zeros_like(dw_ref)

        dh = _dot(dg_ref[...], wg_ref[...]) + _dot(du_ref[...], wu_ref[...])
        dx, dw = _rms_bwd_math(x_ref[...], nw_ref[...], dh)
        dx_ref[...] = dr_ref[...] + dx
        dw_ref[...] += dw

    wide = pl.BlockSpec((tm, FF_PAD), lambda i: (i, 0))
    wsp = pl.BlockSpec((FF_PAD, d), lambda i: (0, 0))
    tok = pl.BlockSpec((tm, d), lambda i: (i, 0))
    vec = pl.BlockSpec((1, d), lambda i: (0, 0))
    return pl.pallas_call(
        body, name="ffn_in_dx", grid=(t // tm,), in_specs=[wide, wide, wsp, wsp, tok, vec, tok], out_specs=[tok, vec],
        out_shape=[jax.ShapeDtypeStruct((t, d), F32), jax.ShapeDtypeStruct((1, d), F32)],
        compiler_params=_params(("arbitrary",)),
    )(dg, du, wg_t, wu_t, x, nw, dxo)


def _adamw(w, g, m, v):
    r, c = w.shape
    tr = _tile(r, (512, 256, 128, 64, 32, 16, 8))
    c1 = 1.0 - ADAM_B1 ** ADAM_STEP
    c2 = 1.0 - ADAM_B2 ** ADAM_STEP

    def body(w_ref, g_ref, m_ref, v_ref, d_ref, nm_ref, nv_ref):
        gv = g_ref[...]
        nm = ADAM_B1 * m_ref[...] + (1.0 - ADAM_B1) * gv
        nv = ADAM_B2 * v_ref[...] + (1.0 - ADAM_B2) * (gv * gv)
        d_ref[...] = -ADAM_LR * ((nm / c1) / (jnp.sqrt(nv / c2) + ADAM_EPS) + ADAM_WD * w_ref[...])
        nm_ref[...] = nm
        nv_ref[...] = nv

    blk = pl.BlockSpec((tr, c), lambda i: (i, 0))
    return pl.pallas_call(
        body, name="adamw", grid=(r // tr,), in_specs=[blk] * 4, out_specs=[blk] * 3,
        out_shape=[jax.ShapeDtypeStruct((r, c), F32)] * 3, compiler_params=_params(("parallel",)),
    )(w, g, m, v)


def _ffn_slot(which, kind):
    return which * 3 + kind


def _ffn_fwd(x, nw, w_ffn, which):
    wg, wu, wd = (w_ffn[_ffn_slot(which, kind)] for kind in range(3))
    h, g, u, a = _ffn_in(x, nw, wg, wu)
    return _mm(a, wd, name="ffn_out", res=x, scale=0.5), (x, h, g, u, a)


def _ffn_bwd(dxo, res, nw, w_ffn, g_ffn, which):
    x, h, g, u, a = res
    wg, wu, wd = (w_ffn[_ffn_slot(which, kind)] for kind in range(3))
    dg, du = _ffn_out_dx(dxo, wd, g, u)
    g_ffn = _mm_tn(a, dxo, name="ffn_out_dw", scale=0.5, into=g_ffn, slot=_ffn_slot(which, 2))
    g_ffn = _mm_tn(dg, h, name="ffn_in_dw", into=g_ffn, slot=_ffn_slot(which, 0))
    g_ffn = _mm_tn(du, h, name="ffn_in_dw", into=g_ffn, slot=_ffn_slot(which, 1))
    dx, dnw = _ffn_in_dx(dg, du, wg, wu, x, nw, dxo)
    return dx, dnw, g_ffn


def _hg_lower_bounds(logits):
    p = jax.nn.softmax(logits, axis=0)
    return jnp.cumsum(p, axis=0) - p[0]


def _even_fwd(x1, p, nb, l):
    t = nb * l
    h = _rms_fwd(x1, p["mix_norm"])
    proj = _mm(h, p["w_in_t"], name="ev_in", nt=True)
    proj3 = proj.reshape(nb, l, EV_IN)
    ya, hst = _hgrn2_fwd(proj3, p["lb"], p["hg_nw"])
    u = proj[:, 4 * HG_QK:]
    bu = _mm(u, p["bbd"], name="s5_in", out_dtype=BF16)
    hs = _s5_scan_fwd(bu.reshape(nb, l, 2, S5_ROWS, LANES), p["abar"])
    yc = _mm(hs.reshape(t, 2 * S5_NSTATE), p["ccd"], name="s5_out")
    yb, y0 = _s5_post_fwd(yc, u, p["s5_d"], p["w_glu"])
    ycat = jnp.concatenate([ya.reshape(t, HG_QK), yb], axis=1)
    x2 = _mm(ycat, p["w_out"], name="mix_out", res=x1)
    return x2, (x1, h, proj3, hst, u, hs, y0, ycat)


def _even_bwd(dx2, res, p, nb, l):
    x1, h, proj3, hst, u, hs, y0, ycat = res
    t = nb * l
    dycat = _mm(dx2, p["w_out"], name="mix_out_dx", nt=True)
    dw_out = _mm_tn(ycat, dx2, name="mix_out_dw", out_dtype=BF16)
    dp_hg, dlb, dhg_nw = _hgrn2_bwd(proj3, p["lb"], p["hg_nw"], hst, dycat.reshape(nb, l, D_MODEL))
    dy0, y, dz, du_d, dd = _s5_post_bwd(y0, u, p["s5_d"], p["w_glu"], dycat[:, HG_QK:])
    dw_glu = _mm_tn(y, dz, name="s5_glu_dw", out_dtype=BF16)
    dhs = _mm(dy0, p["ccd"], name="s5_out_dx", nt=True, out_dtype=BF16)
    dccd = _mm_tn(hs.reshape(t, 2 * S5_NSTATE), dy0, name="s5_out_dw")
    g5, dabar = _s5_scan_bwd(dhs.reshape(nb, l, 2, S5_ROWS, LANES), hs, p["abar"])
    g2 = g5.reshape(t, 2 * S5_NSTATE)
    du = _mm(g2, p["bbd"], name="s5_in_dx", nt=True, res=du_d, out_dtype=BF16)
    dbbd = _mm_tn(u, g2, name="s5_in_dw")
    dproj = jnp.concatenate([dp_hg.reshape(t, 4 * HG_QK), du], axis=1)
    dw_in_t = _mm_tn(dproj, h, name="ev_in_dw", out_dtype=BF16)
    dh = _mm(dproj, p["w_in_t"], name="ev_in_dx")
    dx1, dmix = _rms_bwd(x1, p["mix_norm"], dh, dx2)
    small = dict(mix_norm=dmix, lb=dlb.sum(0).reshape(HG_QK), hg_nw=dhg_nw.reshape(HEAD_DIM),
                 abar=dabar.sum(0), bbd=dbbd, ccd=dccd, s5_d=dd.reshape(S5_WIDTH))
    return dx1, dict(w_in_t=dw_in_t, w_glu=dw_glu, w_out=dw_out), small


def _odd_fwd(x1, p, nb, l):
    t = nb * l
    h = _rms_fwd(x1, p["mix_norm"])
    proj3 = _mm(h, p["w_in"], name="od_in").reshape(nb, l, OD_IN_PAD)
    qkv = _gdn_pre_fwd(proj3, p["conv_w"])
    y, st = _gdn_fwd(qkv, proj3, p["a_log"], p["dt_bias"], p["gdn_nw"])
    y2 = y.reshape(t, D_MODEL)
    x2 = _mm(y2, p["w_out"], name="mix_out", res=x1)
    return x2, (x1, h, proj3, qkv, st, y2)


def _odd_bwd(dx2, res, p, nb, l):
    x1, h, proj3, qkv, st, y2 = res
    t = nb * l
    dy = _mm(dx2, p["w_out"], name="mix_out_dx", nt=True)
    dw_out = _mm_tn(y2, dx2, name="mix_out_dw", out_dtype=BF16)
    dqkv, dgate, dsmall, dalog, ddtb, dnw = _gdn_bwd(
        qkv, proj3, p["a_log"], p["dt_bias"], p["gdn_nw"], st, dy.reshape(nb, l, D_MODEL))
    dx_qkv, dcw = _gdn_pre_bwd(proj3, p["conv_w"], dqkv)
    dproj = jnp.concatenate([dx_qkv, dgate, dsmall], axis=-1).reshape(t, OD_IN_PAD)
    dw_in = _mm_tn(h, dproj, name="od_in_dw", out_dtype=BF16)[:, :OD_IN]
    dh = _mm(dproj, p["w_in"], name="od_in_dx", nt=True)
    dx1, dmix = _rms_bwd(x1, p["mix_norm"], dh, dx2)
    small = dict(mix_norm=dmix, a_log=dalog.sum(0).reshape(GDN_HEADS), dt_bias=ddtb.sum(0).reshape(GDN_HEADS),
                 gdn_nw=dnw.reshape(HEAD_DIM))
    return dx1, dict(w_in=dw_in, conv_w=dcw.sum(0), w_out=dw_out), small


def _local_step(x, target, layer_weights, small, layer_done):
    nb, l, _ = x.shape
    t = nb * l
    lbs, lbs_vjp = jax.vjp(_hg_lower_bounds, small["hg_lb_logits"])
    row = lambda v: v.reshape(1, -1)
    s5_vjps = {}

    def layer_params(layer, big):
        j = layer // 2
        p = dict(mix_norm=row(small["mix_norm"][layer]), w_out=big["w_out"])
        if layer % 2 == 0:
            (abar, bbd, ccd), s5_vjps[j] = jax.vjp(
                _s5_params, small["s5_a_re"][j], small["s5_a_im"][j], small["s5_b_re"][j], small["s5_b_im"][j],
                small["s5_c_re"][j], small["s5_c_im"][j], small["s5_log_dt"][j])
            p.update(w_in_t=big["w_in_t"], w_glu=big["w_glu"],
                     lb=lbs[j].reshape(HG_HEADS, 1, HEAD_DIM), hg_nw=row(small["hg_norm_w"][j]),
                     abar=abar, bbd=bbd.astype(BF16), ccd=ccd.astype(BF16), s5_d=row(small["s5_d"][j]))
        else:
            p.update(w_in=big["w_in"], conv_w=big["conv_w"],
                     a_log=small["gdn_a_log"][j].reshape(GDN_HEADS, 1, 1),
                     dt_bias=small["gdn_dt_bias"][j].reshape(GDN_HEADS, 1, 1), gdn_nw=row(small["gdn_norm_w"][j]))
        return p

    xs = x.reshape(t, D_MODEL)
    saved, layers, ffn_w = [], [], []
    for layer in range(DEPTH):
        big = layer_weights(layer, xs)
        p = layer_params(layer, big)
        xs, r1 = _ffn_fwd(xs, row(small["ffn1_norm"][layer]), big["ffn"], 0)
        xs, r2 = (_even_fwd if layer % 2 == 0 else _odd_fwd)(xs, p, nb, l)
        xs, r3 = _ffn_fwd(xs, row(small["ffn2_norm"][layer]), big["ffn"], 1)
        saved.append((r1, r2, r3))
        layers.append(p)
        ffn_w.append(big["ffn"])
    loss, dx, dfinal = _loss_head(xs, row(small["final_norm"]), target.reshape(t, D_MODEL))

    gs = {k: [None] * DEPTH for k in ("ffn1_norm", "mix_norm", "ffn2_norm")}
    gs.update({k: [None] * 2 for k in ("hg_norm_w", "s5_a_re", "s5_a_im", "s5_b_re", "s5_b_im", "s5_c_re", "s5_c_im",
                                       "s5_d", "s5_log_dt", "gdn_a_log", "gdn_dt_bias", "gdn_norm_w")})
    dlbs = [None] * 2
    for layer in reversed(range(DEPTH)):
        j = layer // 2
        p = layers[layer]
        r1, r2, r3 = saved[layer]
        g_ffn = lax.empty(ffn_w[layer].shape, BF16)
        dx, dnw, g_ffn = _ffn_bwd(dx, r3, row(small["ffn2_norm"][layer]), ffn_w[layer], g_ffn, 1)
        gs["ffn2_norm"][layer] = dnw[0]
        if layer % 2 == 0:
            dx, gw, sm = _even_bwd(dx, r2, p, nb, l)
            dlbs[j] = sm["lb"]
            gs["hg_norm_w"][j] = sm["hg_nw"]
            gs["s5_d"][j] = sm["s5_d"]
            (gs["s5_a_re"][j], gs["s5_a_im"][j], gs["s5_b_re"][j], gs["s5_b_im"][j], gs["s5_c_re"][j],
             gs["s5_c_im"][j], gs["s5_log_dt"][j]) = s5_vjps[j]((sm["abar"], sm["bbd"], sm["ccd"]))
        else:
            dx, gw, sm = _odd_bwd(dx, r2, p, nb, l)
            gs["gdn_a_log"][j], gs["gdn_dt_bias"][j], gs["gdn_norm_w"][j] = sm["a_log"], sm["dt_bias"], sm["gdn_nw"]
        gs["mix_norm"][layer] = sm["mix_norm"][0]
        dx, dnw, g_ffn = _ffn_bwd(dx, r1, row(small["ffn1_norm"][layer]), ffn_w[layer], g_ffn, 0)
        gs["ffn1_norm"][layer] = dnw[0]
        layer_done(layer, dict(gw, ffn=g_ffn))
    gsmall = {k: jnp.stack(v) for k, v in gs.items()}
    gsmall["hg_lb_logits"] = lbs_vjp(jnp.stack(dlbs))[0]
    gsmall["final_norm"] = dfinal[0]
    return loss[0, 0], dx.reshape(nb, l, D_MODEL), gsmall


def _here():
    return lax.axis_index("x"), lax.axis_index("y"), lax.axis_index("c")


def _other_chips(x, y):
    return [(1 - x, y), (x, 1 - y), (1 - x, 1 - y)]


def _all_gather(blocks, name):
    n = len(blocks)

    def body(*refs):
        x_refs, o_refs = refs[:n], refs[n:2 * n]
        send_sems, recv_sems, local_sems = refs[2 * n:]
        x, y, cc = _here()
        me, sibling = (x, y, cc), (x, y, 1 - cc)
        chips = _other_chips(x, y)

        def win(i, p):
            return o_refs[i].at[:, 4 * p[0] + 2 * p[1] + p[2]]

        def copy(i, k, blk, to, src=None):
            return pltpu.make_async_remote_copy(
                src_ref=win(i, blk) if src is None else src, dst_ref=win(i, blk),
                send_sem=send_sems.at[i, k], recv_sem=recv_sems.at[i, k], device_id=to, device_id_type=MESH)

        mine = [pltpu.make_async_copy(x_refs[i], win(i, me), local_sems.at[i]) for i in range(n)]
        for cp in mine:
            cp.start()
        first = []
        for i in range(n):
            first.append(copy(i, 0, me, sibling, src=x_refs[i]))
            first += [copy(i, 1 + j, me, (*chip, cc), src=x_refs[i]) for j, chip in enumerate(chips)]
        for cp in first:
            cp.start()
        passed = []
        for j, chip in enumerate(chips):
            for i in range(n):
                copy(i, 1 + j, (*chip, cc), me).wait_recv()
                fwd = copy(i, 4 + j, (*chip, cc), sibling)
                fwd.start()
                passed.append(fwd)
        for i in range(n):
            copy(i, 0, sibling, me).wait_recv()
            for j, chip in enumerate(chips):
                copy(i, 4 + j, (*chip, 1 - cc), me).wait_recv()
        for cp in first + passed:
            cp.wait_send()
        for cp in mine:
            cp.wait()

    hbm = pl.BlockSpec(memory_space=pl.ANY)
    return pl.pallas_call(
        body, name=name, in_specs=[hbm] * n, out_specs=[hbm] * n,
        out_shape=[jax.ShapeDtypeStruct((b.shape[0], N_DEV) + b.shape[1:], b.dtype) for b in blocks],
        scratch_shapes=[pltpu.SemaphoreType.DMA((n, 7)), pltpu.SemaphoreType.DMA((n, 7)), pltpu.SemaphoreType.DMA((n,))],
    )(*blocks)


_HBM = pl.BlockSpec(memory_space=pltpu.HBM)
_SEM = pl.BlockSpec(memory_space=pltpu.SEMAPHORE)
_DATAFLOW = pltpu.SideEffectType.DATAFLOW_SIDE_EFFECTING


def _peer(k):
    x, y, c = _here()
    return (x ^ ((k >> 2) & 1), y ^ ((k >> 1) & 1), c ^ (k & 1))


def _device_of(p):
    return 4 * p[0] + 2 * p[1] + p[2]


def _exchange_start(srcs, lands, copies, name):
    n = len(srcs)

    def body(*refs):
        s_refs, l_refs, sems = refs[:n], refs[n:2 * n], refs[2 * n:4 * n]
        for i in range(n):
            for k in range(1, N_DEV):
                src, dst = copies(s_refs[i], l_refs[i], k)
                pltpu.make_async_remote_copy(
                    src_ref=src, dst_ref=dst, send_sem=sems[2 * i], recv_sem=sems[2 * i + 1],
                    device_id=_peer(k), device_id_type=MESH).start()

    both = list(srcs) + list(lands)
    outs = pl.pallas_call(
        body, name=name,
        out_shape=tuple([pltpu.SemaphoreType.DMA(())] * (2 * n)) + tuple(pltpu.HBM(a.shape, a.dtype) for a in both),
        in_specs=[_HBM] * (2 * n), out_specs=tuple([_SEM] * (2 * n)) + tuple([_HBM] * (2 * n)),
        input_output_aliases={i: 2 * n + i for i in range(2 * n)},
        compiler_params=pltpu.CompilerParams(has_side_effects=_DATAFLOW),
    )(*[pltpu.with_memory_space_constraint(a, pltpu.HBM) for a in both])
    return list(outs[:2 * n]), list(outs[2 * n:3 * n]), list(outs[3 * n:])


def _exchange_wait(handle, seven, after, name):
    sems, srcs, lands = handle
    n = len(srcs)

    def body(*refs):
        l_refs, sem_refs = refs[n:2 * n], refs[2 * n:4 * n]
        for i in range(n):
            all_seven = seven(l_refs[i])
            cp = pltpu.make_async_remote_copy(
                src_ref=all_seven, dst_ref=all_seven, send_sem=sem_refs[2 * i], recv_sem=sem_refs[2 * i + 1],
                device_id=_peer(1), device_id_type=MESH)
            cp.wait_send()
            cp.wait_recv()

    both = list(srcs) + list(lands)
    outs = pl.pallas_call(
        body, name=name, out_shape=tuple(pltpu.HBM(a.shape, a.dtype) for a in both),
        in_specs=[_HBM] * (2 * n) + [_SEM] * (2 * n) + [pl.BlockSpec(memory_space=pl.ANY)],
        out_specs=tuple([_HBM] * (2 * n)), input_output_aliases={i: i for i in range(2 * n)},
        compiler_params=pltpu.CompilerParams(has_side_effects=_DATAFLOW),
    )(*both, *sems, after)
    return list(outs[:n]), list(outs[n:])


def _gather_copies(src, land, k):
    del k
    return src, land.at[:, _device_of(_here())]


def _gather_seven(land):
    return land.at[:, pl.ds(0, N_DEV - 1)]


def _scatter_copies(src, land, k):
    return src.at[:, _device_of(_peer(k))], land.at[k - 1]


def _scatter_seven(land):
    return land


def _row_tile(w):
    return w if w <= 512 else _tile(w, (512, 256, 128))


def _rs_sum(g, land):
    a, _, w, c = g.shape
    tr = _row_tile(w)

    def body(g_ref, l_ref, o_ref):
        s = g_ref[...].astype(F32)
        for k in range(N_DEV - 1):
            s = s + l_ref[k].astype(F32)
        o_ref[...] = s

    def device():
        return _device_of(_here())

    return pl.pallas_call(
        body, name="rs_sum", grid=(a, w // tr),
        in_specs=[pl.BlockSpec((None, None, tr, c), lambda aa, i: (aa, device(), i, 0)),
                  pl.BlockSpec((N_DEV - 1, None, tr, c), lambda aa, i: (0, aa, i, 0))],
        out_specs=pl.BlockSpec((None, tr, c), lambda aa, i: (aa, i, 0)),
        out_shape=jax.ShapeDtypeStruct((a, w, c), F32),
        compiler_params=_params(("parallel", "parallel")),
    )(g, land)


def _sum_slots(g):
    _, n, r, c = g.shape

    def body(g_ref, o_ref):
        s = g_ref[0, 0]
        for k in range(1, n):
            s = s + g_ref[0, k]
        o_ref[...] = s

    return pl.pallas_call(
        body, name="sum_slots", grid=(1,),
        in_specs=[pl.BlockSpec((1, n, r, c), lambda i: (0, 0, 0, 0))], out_specs=pl.BlockSpec((r, c), lambda i: (0, 0)),
        out_shape=jax.ShapeDtypeStruct((r, c), F32), compiler_params=_params(("arbitrary",)),
    )(g)


_WEIGHTS = ['ffn1_norm', 'ffn1_w_gate', 'ffn1_w_up', 'ffn1_w_down', 'mix_norm', 'ffn2_norm', 'ffn2_w_gate', 'ffn2_w_up',
            'ffn2_w_down', 'ev_w_in', 'hg_lb_logits', 'hg_norm_w', 's5_a_re', 's5_a_im', 's5_b_re', 's5_b_im', 's5_c_re',
            's5_c_im', 's5_d', 's5_log_dt', 's5_w_glu', 'ev_w_out', 'od_w_in', 'gdn_conv_w', 'gdn_a_log', 'gdn_dt_bias',
            'gdn_norm_w', 'od_w_out', 'final_norm']
_SHARDED = ['ffn1_w_gate', 'ffn1_w_up', 'ffn1_w_down', 'ffn2_w_gate', 'ffn2_w_up', 'ffn2_w_down', 'ev_w_in', 's5_w_glu',
            'ev_w_out', 'od_w_in', 'gdn_conv_w', 'od_w_out']
_REPLICATED = [n for n in _WEIGHTS if n not in _SHARDED]
_SMALL_COLS = 1024


def _pad_rows(a, rows):
    return jnp.pad(a, ((0, rows - a.shape[0]), (0, 0)))


def _layer_blocks(w, layer):
    j = layer // 2
    ffn = []
    for f in ('ffn1', 'ffn2'):
        ffn += [_pad_rows(w[f + '_w_gate'][layer].T, FF_SHARD_PAD), _pad_rows(w[f + '_w_up'][layer].T, FF_SHARD_PAD),
                _pad_rows(w[f + '_w_down'][layer], FF_SHARD_PAD)]
    ffn = jnp.stack(ffn).astype(BF16)
    if layer % 2 == 0:
        return [ffn, w['ev_w_in'][j].T.astype(BF16)[None], w['s5_w_glu'][j].astype(BF16)[None],
                w['ev_w_out'][j].astype(BF16)[None]]
    return [ffn, w['od_w_in'][j].astype(BF16)[None], w['gdn_conv_w'][j][None], w['od_w_out'][j].astype(BF16)[None]]


def _layer_whole(gathered, layer):
    whole = lambda g: g.reshape(g.shape[0], N_DEV * g.shape[2], g.shape[3])
    ffn, a1, a2, w_out = gathered
    if layer % 2 == 0:
        return dict(ffn=whole(ffn), w_in_t=whole(a1)[0], w_glu=whole(a2)[0], w_out=whole(w_out)[0])
    w_in = jnp.moveaxis(a1[0], 0, 1).reshape(D_MODEL, OD_IN)
    conv = jnp.moveaxis(a2[0], 0, 1).reshape(CONV_W, GDN_QKV)
    return dict(ffn=whole(ffn), w_in=jnp.pad(w_in, ((0, 0), (0, OD_IN_PAD - OD_IN))), conv_w=conv, w_out=whole(w_out)[0])


def _layer_grad_windows(g, layer):
    windows = lambda a, w: a.reshape(-1, N_DEV, w, a.shape[-1])
    ffn, w_out = windows(g['ffn'], FF_SHARD_PAD), windows(g['w_out'], D_MODEL // N_DEV)
    if layer % 2 == 0:
        return [ffn, windows(g['w_in_t'], EV_IN // N_DEV), windows(g['w_glu'], S5_WIDTH // N_DEV), w_out]
    w_in = jnp.moveaxis(g['w_in'].reshape(D_MODEL, N_DEV, OD_IN // N_DEV), 1, 0)[None]
    conv = jnp.moveaxis(g['conv_w'].reshape(CONV_W, N_DEV, GDN_QKV // N_DEV), 1, 0)[None]
    return [ffn, w_in, conv, w_out]


def _flat_small(src, prefix=''):
    flat = jnp.concatenate([src[prefix + n].reshape(-1) for n in _REPLICATED])
    rows = -(-(flat.shape[0] + 1) // (_SMALL_COLS * 8)) * 8
    return jnp.pad(flat, (0, rows * _SMALL_COLS - flat.shape[0])).reshape(rows, _SMALL_COLS)


def _split_small(flat2d, like):
    flat, out, off = flat2d.reshape(-1), {}, 0
    for n in _REPLICATED:
        size = math.prod(like[n].shape)
        out[n] = flat[off:off + size].reshape(like[n].shape)
        off += size
    return out, flat[off]


def _all_reduce_small(gsmall, loss):
    flat = _flat_small(gsmall)
    n_used = sum(math.prod(gsmall[n].shape) for n in _REPLICATED)
    flat = flat.reshape(-1).at[n_used].set(loss).reshape(flat.shape)
    (gathered,) = _all_gather([flat[None]], "gather_small")
    return _split_small(_sum_slots(gathered), gsmall)


def _as_2d(a):
    return a.reshape(-1, a.shape[-1])


def kernel(x, ffn1_norm, ffn1_w_gate, ffn1_w_up, ffn1_w_down, mix_norm, ffn2_norm, ffn2_w_gate, ffn2_w_up, ffn2_w_down, ev_w_in, hg_lb_logits, hg_norm_w, s5_a_re, s5_a_im, s5_b_re, s5_b_im, s5_c_re, s5_c_im, s5_d, s5_log_dt, s5_w_glu, ev_w_out, od_w_in, gdn_conv_w, gdn_a_log, gdn_dt_bias, gdn_norm_w, od_w_out, final_norm, loss_target, m_ffn1_norm, m_ffn1_w_gate, m_ffn1_w_up, m_ffn1_w_down, m_mix_norm, m_ffn2_norm, m_ffn2_w_gate, m_ffn2_w_up, m_ffn2_w_down, m_ev_w_in, m_hg_lb_logits, m_hg_norm_w, m_s5_a_re, m_s5_a_im, m_s5_b_re, m_s5_b_im, m_s5_c_re, m_s5_c_im, m_s5_d, m_s5_log_dt, m_s5_w_glu, m_ev_w_out, m_od_w_in, m_gdn_conv_w, m_gdn_a_log, m_gdn_dt_bias, m_gdn_norm_w, m_od_w_out, m_final_norm, v_ffn1_norm, v_ffn1_w_gate, v_ffn1_w_up, v_ffn1_w_down, v_mix_norm, v_ffn2_norm, v_ffn2_w_gate, v_ffn2_w_up, v_ffn2_w_down, v_ev_w_in, v_hg_lb_logits, v_hg_norm_w, v_s5_a_re, v_s5_a_im, v_s5_b_re, v_s5_b_im, v_s5_c_re, v_s5_c_im, v_s5_d, v_s5_log_dt, v_s5_w_glu, v_ev_w_out, v_od_w_in, v_gdn_conv_w, v_gdn_a_log, v_gdn_dt_bias, v_gdn_norm_w, v_od_w_out, v_final_norm):
    given = dict(locals())
    w = {n: given[n] for n in _WEIGHTS}
    small = {n: w[n] for n in _REPLICATED}
    me = _device_of(_here())

    blocks = [_layer_blocks(w, layer) for layer in range(DEPTH)]
    first = _all_gather(blocks[0], "gather_weights")
    pending = {}
    for layer in range(1, DEPTH):
        lands = [lax.dynamic_update_slice(lax.empty((b.shape[0], N_DEV) + b.shape[1:], b.dtype), b[:, None], (0, me, 0, 0))
                 for b in blocks[layer]]
        pending[layer] = _exchange_start(blocks[layer], lands, _gather_copies, "gather_start_%d" % layer)

    def layer_weights(layer, after):
        if layer == 0:
            return _layer_whole(first, 0)
        return _layer_whole(_exchange_wait(pending[layer], _gather_seven, after, "gather_wait_%d" % layer)[1], layer)

    sent = {}

    def layer_done(layer, grads):
        gs = _layer_grad_windows(grads, layer)
        lands = [lax.empty((N_DEV - 1, g.shape[0]) + g.shape[2:], g.dtype) for g in gs]
        sent[layer] = _exchange_start(gs, lands, _scatter_copies, "scatter_start_%d" % layer)

    loss_part, grad_x, gsmall = _local_step(given['x'], given['loss_target'], layer_weights, small, layer_done)
    gsum, loss = _all_reduce_small(gsmall, loss_part)
    summed = {}
    for layer in reversed(range(DEPTH)):
        partials, lands = _exchange_wait(sent[layer], _scatter_seven, grad_x, "scatter_wait_%d" % layer)
        summed[layer] = [_rs_sum(g, land) for g, land in zip(partials, lands)]

    grads = dict(gsum)
    ffn = jnp.stack([summed[layer][0][:, :FF_SHARD] for layer in range(DEPTH)], axis=1)
    t_ = lambda a: jnp.swapaxes(a, 1, 2)
    for i, f in enumerate(('ffn1', 'ffn2')):
        grads[f + '_w_gate'], grads[f + '_w_up'], grads[f + '_w_down'] = t_(ffn[3 * i]), t_(ffn[3 * i + 1]), ffn[3 * i + 2]
    even, odd = [summed[0], summed[2]], [summed[1], summed[3]]
    grads['ev_w_in'] = jnp.stack([r[1][0].T for r in even])
    grads['s5_w_glu'] = jnp.stack([r[2][0] for r in even])
    grads['ev_w_out'] = jnp.stack([r[3][0] for r in even])
    grads['od_w_in'] = jnp.stack([r[1][0] for r in odd])
    grads['gdn_conv_w'] = jnp.stack([r[2][0] for r in odd])
    grads['od_w_out'] = jnp.stack([r[3][0] for r in odd])

    delta, new_m, new_v = {}, {}, {}
    for n in _SHARDED:
        d, nm, nv = _adamw(_as_2d(w[n]), _as_2d(grads[n]), _as_2d(given['m_' + n]), _as_2d(given['v_' + n]))
        delta[n], new_m[n], new_v[n] = (a.reshape(w[n].shape) for a in (d, nm, nv))
    d, nm, nv = _adamw(_flat_small(w), _flat_small(grads), _flat_small(given, 'm_'), _flat_small(given, 'v_'))
    (delta_s, _), (new_m_s, _), (new_v_s, _) = (_split_small(a, small) for a in (d, nm, nv))
    delta.update(delta_s)
    new_m.update(new_m_s)
    new_v.update(new_v_s)
    return (loss, grad_x, *[grads[n] for n in _WEIGHTS], *[delta[n] for n in _WEIGHTS],
            *[new_m[n] for n in _WEIGHTS], *[new_v[n] for n in _WEIGHTS])
```

```python
import math

import jax
import jax.numpy as jnp
from jax import lax
from jax.experimental import pallas as pl
from jax.experimental.pallas import tpu as pltpu

F32 = jnp.float32
BF16 = jnp.bfloat16
HI = lax.Precision.HIGH

D_MODEL = 1024
DEPTH = 4
D_FF = 2816
NORM_EPS = 1e-6
F_MIN = 1e-6
CHUNK = 64
HG_HEADS = 4
HEAD_DIM = 128
HG_QK = HG_HEADS * HEAD_DIM
S5_WIDTH = 512
S5_GROUP = 16
S5_GROUPS = 32
S5_STATE = 64
S5_NSTATE = S5_GROUPS * S5_STATE
EV_IN = 2560
GDN_HEADS = 8
GDN_QKV = 3 * GDN_HEADS * HEAD_DIM
CONV_W = 4
OD_IN = 4112
OD_IN_PAD = 4224
N_DEV = 8
LANES = 128
ADAM_LR, ADAM_B1, ADAM_B2, ADAM_EPS, ADAM_WD, ADAM_STEP = 0.001, 0.9, 0.999, 1e-08, 0.01, 10
VMEM_LIMIT = 56 * 1024 * 1024

MESH = pl.DeviceIdType.MESH


def _params(sem=None, **kw):
    return pltpu.CompilerParams(dimension_semantics=sem, vmem_limit_bytes=VMEM_LIMIT, **kw)


def _tile(n, cands):
    for c in cands:
        if n % c == 0:
            return c
    return n


def _dot(a, b):
    return jnp.dot(a.astype(BF16), b.astype(BF16), preferred_element_type=F32)


def _dot_nt(a, b):
    return lax.dot_general(a.astype(BF16), b.astype(BF16), (((1,), (1,)), ((), ())), preferred_element_type=F32)


def _dot_tn(a, b):
    return lax.dot_general(a.astype(BF16), b.astype(BF16), (((0,), (0,)), ((), ())), preferred_element_type=F32)


def _dot_hi(a, b):
    return jnp.dot(a, b, precision=HI, preferred_element_type=F32)


def _bmm(a, b):
    return jnp.einsum('gmk,gkn->gmn', a.astype(BF16), b.astype(BF16), preferred_element_type=F32)


def _bmm_nt(a, b):
    return jnp.einsum('gmk,gnk->gmn', a.astype(BF16), b.astype(BF16), preferred_element_type=F32)


def _bmm_tn(a, b):
    return jnp.einsum('gkm,gkn->gmn', a.astype(BF16), b.astype(BF16), preferred_element_type=F32)


def _bmm_hi(a, b):
    return jnp.einsum('gmk,gkn->gmn', a, b, precision=HI, preferred_element_type=F32)


_TN_CANDS = (1408, 1280, 1024, 512, 384, 256, 128)
_B_TILE_BYTES = 6 * 1024 * 1024


def _mm(a, b, *, name, nt=False, out_dtype=F32, res=None, scale=1.0):
    m, k = a.shape
    n = b.shape[0] if nt else b.shape[1]
    tm = _tile(m, (512, 256, 128))
    tn = _tile(n, [c for c in _TN_CANDS if c * k * b.dtype.itemsize <= _B_TILE_BYTES])

    def body(*refs):
        if res is None:
            a_ref, b_ref, o_ref = refs
        else:
            a_ref, b_ref, r_ref, o_ref = refs
        acc = (_dot_nt if nt else _dot)(a_ref[...], b_ref[...])
        if scale != 1.0:
            acc = scale * acc
        if res is not None:
            acc = r_ref[...] + acc
        o_ref[...] = acc.astype(out_dtype)

    b_spec = pl.BlockSpec((tn, k), lambda j, i: (j, 0)) if nt else pl.BlockSpec((k, tn), lambda j, i: (0, j))
    in_specs = [pl.BlockSpec((tm, k), lambda j, i: (i, 0)), b_spec]
    args = [a, b]
    if res is not None:
        in_specs.append(pl.BlockSpec((tm, tn), lambda j, i: (i, j)))
        args.append(res)
    return pl.pallas_call(
        body, name=name, grid=(n // tn, m // tm), in_specs=in_specs,
        out_specs=pl.BlockSpec((tm, tn), lambda j, i: (i, j)),
        out_shape=jax.ShapeDtypeStruct((m, n), out_dtype),
        compiler_params=_params(("parallel", "parallel")),
    )(*args)


def _mm_tn(a, b, *, name, scale=1.0, out_dtype=F32, into=None, slot=0):
    t, m = a.shape
    n = b.shape[1]
    tm = _tile(m, (1024, 512, 256, 128))
    tn = _tile(n, _TN_CANDS)
    tk = _tile(t, (2048, 1024, 512, 256, 128))
    nk = t // tk

    def body(*refs):
        a_ref, b_ref = refs[:2]
        o_ref, acc_ref = refs[-2:]
        kk = pl.program_id(2)

        @pl.when(kk == 0)
        def _():
            acc_ref[...] = jnp.zeros_like(acc_ref)

        acc_ref[...] += _dot_tn(a_ref[...], b_ref[...])

        @pl.when(kk == nk - 1)
        def _():
            o_ref[...] = (acc_ref[...] * scale if scale != 1.0 else acc_ref[...]).astype(o_ref.dtype)

    in_specs = [pl.BlockSpec((tk, tm), lambda i, j, kk: (kk, i)), pl.BlockSpec((tk, tn), lambda i, j, kk: (kk, j))]
    args = [a, b]
    if into is None:
        out_spec = pl.BlockSpec((tm, tn), lambda i, j, kk: (i, j))
        out_shape = jax.ShapeDtypeStruct((m, n), out_dtype)
        alias = {}
    else:
        in_specs.append(pl.BlockSpec(memory_space=pl.ANY))
        args.append(into)
        out_spec = pl.BlockSpec((None, tm, tn), lambda i, j, kk: (slot, i, j))
        out_shape = jax.ShapeDtypeStruct(into.shape, into.dtype)
        alias = {2: 0}
    return pl.pallas_call(
        body, name=name, grid=(m // tm, n // tn, nk), in_specs=in_specs, out_specs=out_spec, out_shape=out_shape,
        scratch_shapes=[pltpu.VMEM((tm, tn), F32)], input_output_aliases=alias,
        compiler_params=_params(("parallel", "parallel", "arbitrary")),
    )(*args)


def _sigmoid(x):
    return jax.nn.sigmoid(x)


def _head_norm_gate(o, gate, nw):
    r = lax.rsqrt(jnp.mean(o * o, axis=-1, keepdims=True) + NORM_EPS)
    return o * r * nw * (gate * _sigmoid(gate))


def _hg_chunk(st, ql, fl, iv, gl, lb, nw):
    n_g, c, _ = ql.shape
    q = ql * _sigmoid(ql)
    f = lb + (1.0 - lb) * _sigmoid(fl)
    lf = jnp.log(jnp.maximum(f, F_MIN))
    k = 1.0 - f
    ri = lax.broadcasted_iota(jnp.int32, (c, c), 0)
    ci = lax.broadcasted_iota(jnp.int32, (c, c), 1)
    rl = lax.broadcasted_iota(jnp.int32, (c, LANES), 0)
    each = lambda m2: jnp.broadcast_to(m2[None], (n_g,) + m2.shape)
    b = _bmm_hi(each(jnp.where(ci <= ri, 1.0, 0.0)), lf)
    attn = jnp.where((ri == ci)[None], _bmm_nt(q, k), 0.0)
    sh = 0
    while (1 << sh) < c:
        m = 1 << sh
        ref = ((ri >> (sh + 1)) << (sh + 1)) + (m - 1)
        low_r = ((ri >> sh) & 1) == 1
        low_c = ((ci >> sh) & 1) == 1
        w_low = low_r & (ci > ref) & (ci <= ri)
        w_up = jnp.logical_not(low_r) & (ci > ri) & (ci <= ref)
        w = jnp.where(w_low | w_up, 1.0, 0.0)
        e = jnp.exp(_bmm_hi(each(w), lf))
        low_l = (((rl >> sh) & 1) == 1)[None]
        qs = jnp.where(low_l, q * e, 0.0)
        ks = jnp.where(low_l, 0.0, k * e)
        pair = ((ri >> (sh + 1)) == (ci >> (sh + 1))) & low_r & jnp.logical_not(low_c)
        attn = attn + jnp.where(pair[None], _bmm_nt(qs, ks), 0.0)
        sh += 1
    bl = jnp.sum(lf, axis=1, keepdims=True)
    o = _bmm(attn, iv) + _bmm_nt(q * jnp.exp(b), st)
    st_new = st * jnp.exp(bl) + _bmm_tn(iv, k * jnp.exp(bl - b))
    return _head_norm_gate(o, gl, nw), st_new


def _hg_specs(nb, n_c, rev):
    def cidx(cc):
        return (n_c - 1 - cc) if rev else cc

    def col(width):
        return pl.BlockSpec((nb, CHUNK, width), lambda cc: (0, cidx(cc), 0))

    vec = pl.BlockSpec((HG_HEADS, 1, LANES), lambda cc: (0, 0, 0))
    nw = pl.BlockSpec((1, LANES), lambda cc: (0, 0))
    st = pl.BlockSpec((nb, HG_HEADS, None, LANES, LANES), lambda cc: (0, 0, cidx(cc), 0, 0))
    acc = pl.BlockSpec((nb, HG_HEADS, 1, LANES), lambda cc: (0, 0, 0, 0))
    return col, vec, nw, st, acc


def _head(j):
    return slice(j * LANES, (j + 1) * LANES)


def _seq_heads(ref, nb, n, first=0):
    return jnp.stack([ref[b, :, _head(first + j)] for b in range(nb) for j in range(n)])


def _put_seq_heads(ref, val, nb, n, first=0):
    for b in range(nb):
        for j in range(n):
            ref[b, :, _head(first + j)] = val[b * n + j].astype(ref.dtype)


def _hgrn2_fwd(proj, lb, nw):
    nb, l, _ = proj.shape
    n_c = l // CHUNK
    n_g = nb * HG_HEADS
    col, vec, nws, st, _ = _hg_specs(nb, n_c, False)

    def body(p_ref, lb_ref, nw_ref, y_ref, st_ref, s_scr):
        @pl.when(pl.program_id(0) == 0)
        def _():
            s_scr[...] = jnp.zeros_like(s_scr)

        s_in = s_scr[...]
        st_ref[...] = s_in.reshape(st_ref.shape)
        args = [_seq_heads(p_ref, nb, HG_HEADS, k * HG_HEADS) for k in range(4)]
        y, s_new = _hg_chunk(s_in, *args, jnp.concatenate([lb_ref[...]] * nb), nw_ref[...])
        _put_seq_heads(y_ref, y, nb, HG_HEADS)
        s_scr[...] = s_new

    return pl.pallas_call(
        body, name="hgrn2_fwd", grid=(n_c,),
        in_specs=[col(4 * HG_QK), vec, nws], out_specs=[col(HG_QK), st],
        out_shape=[jax.ShapeDtypeStruct((nb, l, HG_QK), BF16),
                   jax.ShapeDtypeStruct((nb, HG_HEADS, n_c, LANES, LANES), F32)],
        scratch_shapes=[pltpu.VMEM((n_g, LANES, LANES), F32)],
        compiler_params=_params(("arbitrary",)),
    )(proj, lb, nw)


def _hgrn2_bwd(proj, lb, nw, states, dy):
    nb, l, _ = proj.shape
    n_c = l // CHUNK
    n_g = nb * HG_HEADS
    col, vec, nws, st, acc = _hg_specs(nb, n_c, True)

    def body(p_ref, lb_ref, nw_ref, st_ref, dy_ref, dp_ref, dlb_ref, dnw_ref, ds_scr):
        @pl.when(pl.program_id(0) == 0)
        def _():
            ds_scr[...] = jnp.zeros_like(ds_scr)
            dlb_ref[...] = jnp.zeros_like(dlb_ref)
            dnw_ref[...] = jnp.zeros_like(dnw_ref)

        args = [_seq_heads(p_ref, nb, HG_HEADS, k * HG_HEADS) for k in range(4)]
        _, vjp = jax.vjp(_hg_chunk, st_ref[...].reshape(n_g, LANES, LANES), *args,
                         jnp.concatenate([lb_ref[...]] * nb), nw_ref[...])
        ds, dql, dfl, div, dgl, dlb, dnw = vjp((_seq_heads(dy_ref, nb, HG_HEADS).astype(F32), ds_scr[...]))
        ds_scr[...] = ds
        for k, val in enumerate((dql, dfl, div, dgl)):
            _put_seq_heads(dp_ref, val, nb, HG_HEADS, k * HG_HEADS)
        dlb_ref[...] += dlb.reshape(dlb_ref.shape)
        dnw_ref[...] += dnw

    return pl.pallas_call(
        body, name="hgrn2_bwd", grid=(n_c,),
        in_specs=[col(4 * HG_QK), vec, nws, st, col(HG_QK)], out_specs=[col(4 * HG_QK), acc, nws],
        out_shape=[jax.ShapeDtypeStruct((nb, l, 4 * HG_QK), BF16),
                   jax.ShapeDtypeStruct((nb, HG_HEADS, 1, LANES), F32), jax.ShapeDtypeStruct((1, LANES), F32)],
        scratch_shapes=[pltpu.VMEM((n_g, LANES, LANES), F32)],
        compiler_params=_params(("arbitrary",)),
    )(proj, lb, nw, states, dy)


def _shift_down(x, s):
    if s == 0:
        return x
    rows = lax.broadcasted_iota(jnp.int32, x.shape, 0)
    return jnp.where(rows >= s, pltpu.roll(x, s, 0), 0.0)


def _shift_up(x, s):
    if s == 0:
        return x
    n = x.shape[0]
    rows = lax.broadcasted_iota(jnp.int32, x.shape, 0)
    return jnp.where(rows < n - s, pltpu.roll(x, n - s, 0), 0.0)


def _gdn_pre_fwd(proj, conv_w):
    nb, l, _ = proj.shape
    n_blk = GDN_QKV // LANES

    def body(x_ref, w_ref, o_ref):
        j = pl.program_id(1)
        x = x_ref[...]
        w = w_ref[...]
        c = w[3:4] * x
        for t in range(CONV_W - 1):
            c = c + w[t:t + 1] * _shift_down(x, CONV_W - 1 - t)
        s = c * _sigmoid(c)
        r = lax.rsqrt(jnp.sum(s * s, axis=-1, keepdims=True) + NORM_EPS)
        scale = jnp.where(j < GDN_HEADS, HEAD_DIM ** -0.5, 1.0)
        o_ref[...] = jnp.where(j < 2 * GDN_HEADS, s * r * scale, s)

    return pl.pallas_call(
        body, name="gdn_pre_fwd", grid=(nb, n_blk),
        in_specs=[pl.BlockSpec((None, l, LANES), lambda b, j: (b, 0, j)), pl.BlockSpec((CONV_W, LANES), lambda b, j: (0, j))],
        out_specs=pl.BlockSpec((None, l, LANES), lambda b, j: (b, 0, j)),
        out_shape=jax.ShapeDtypeStruct((nb, l, GDN_QKV), F32),
        compiler_params=_params(("parallel", "parallel")),
    )(proj, conv_w)


def _gdn_pre_bwd(proj, conv_w, dout):
    nb, l, _ = proj.shape
    n_blk = GDN_QKV // LANES

    def body(x_ref, w_ref, d_ref, dx_ref, dw_ref):
        j = pl.program_id(1)
        x = x_ref[...]
        w = w_ref[...]
        xs = [_shift_down(x, CONV_W - 1 - t) for t in range(CONV_W)]
        c = w[0:1] * xs[0]
        for t in range(1, CONV_W):
            c = c + w[t:t + 1] * xs[t]
        sg = _sigmoid(c)
        s = c * sg
        d = d_ref[...]
        r = lax.rsqrt(jnp.sum(s * s, axis=-1, keepdims=True) + NORM_EPS)
        scale = jnp.where(j < GDN_HEADS, HEAD_DIM ** -0.5, 1.0)
        ds_norm = scale * r * (d - s * (r * r) * jnp.sum(d * s, axis=-1, keepdims=True))
        ds = jnp.where(j < 2 * GDN_HEADS, ds_norm, d)
        dc = ds * (sg * (1.0 + c * (1.0 - sg)))
        dx = w[3:4] * dc
        for t in range(CONV_W - 1):
            dx = dx + w[t:t + 1] * _shift_up(dc, CONV_W - 1 - t)
        dx_ref[...] = dx.astype(dx_ref.dtype)
        for t in range(CONV_W):
            dw_ref[t:t + 1, :] = jnp.sum(dc * xs[t], axis=0, keepdims=True)

    return pl.pallas_call(
        body, name="gdn_pre_bwd", grid=(nb, n_blk),
        in_specs=[pl.BlockSpec((None, l, LANES), lambda b, j: (b, 0, j)), pl.BlockSpec((CONV_W, LANES), lambda b, j: (0, j)),
                  pl.BlockSpec((None, l, LANES), lambda b, j: (b, 0, j))],
        out_specs=[pl.BlockSpec((None, l, LANES), lambda b, j: (b, 0, j)), pl.BlockSpec((None, CONV_W, LANES), lambda b, j: (b, 0, j))],
        out_shape=[jax.ShapeDtypeStruct((nb, l, GDN_QKV), BF16), jax.ShapeDtypeStruct((nb, CONV_W, GDN_QKV), F32)],
        compiler_params=_params(("parallel", "parallel")),
    )(proj, conv_w, dout)


def _gdn_chunk(st, q, k, v, gate, bl, al, alog, dtb, nw):
    n_g, c, _ = q.shape
    beta = _sigmoid(bl)
    x = al + dtb
    softplus = jnp.maximum(x, 0.0) + jnp.log(1.0 + jnp.exp(-jnp.abs(x)))
    la = -jnp.exp(alog) * softplus
    ri = lax.broadcasted_iota(jnp.int32, (c, c), 0)
    ci = lax.broadcasted_iota(jnp.int32, (c, c), 1)
    lower = (ci <= ri)[None]
    strict = (ci < ri)[None]
    ltri = jnp.broadcast_to(jnp.where(lower, 1.0, 0.0), (n_g, c, c))
    la_l = la + jnp.zeros((1, 1, LANES), F32)
    g = _bmm_hi(ltri, la_l)
    delta = _bmm_hi(ltri, jnp.where(strict, la + jnp.zeros((1, 1, c), F32), 0.0))
    gam = jnp.where(lower, jnp.exp(jnp.where(lower, delta, 0.0)), 0.0)
    kb = k * beta
    vb = v * beta
    m = jnp.where(strict, _bmm_nt(kb, k) * gam, 0.0)
    pw = -m
    t_inv = jnp.where((ri == ci)[None], 1.0, 0.0) + pw
    steps = int(math.log2(c)) - 1
    for _ in range(steps):
        pw = _bmm_hi(pw, pw)
        t_inv = t_inv + _bmm_hi(t_inv, pw)
    eg = jnp.exp(g)
    u = _bmm(t_inv, vb)
    w = _bmm(t_inv, kb * eg)
    attn = jnp.where(lower, _bmm_nt(q, k) * gam, 0.0)
    v_new = u - _bmm_nt(w, st)
    o = _bmm_nt(q * eg, st) + _bmm(attn, v_new)
    g_last = jnp.sum(la_l, axis=1, keepdims=True)
    st_new = st * jnp.exp(g_last) + _bmm_tn(v_new, k * jnp.exp(g_last - g))
    return _head_norm_gate(o, gate, nw), st_new


GDN_WIDE = GDN_HEADS * LANES
GDN_SMALL_BLOCK = (GDN_QKV + GDN_WIDE) // LANES


def _gdn_specs(nb, n_c, rev):
    def cidx(cc):
        return (n_c - 1 - cc) if rev else cc

    def col(width, blk=0):
        return pl.BlockSpec((nb, CHUNK, width), lambda cc: (0, cidx(cc), blk))

    scal = pl.BlockSpec((GDN_HEADS, 1, 1), lambda cc: (0, 0, 0))
    nw = pl.BlockSpec((1, LANES), lambda cc: (0, 0))
    st = pl.BlockSpec((nb, GDN_HEADS, None, LANES, LANES), lambda cc: (0, 0, cidx(cc), 0, 0))
    acc_s = pl.BlockSpec((nb, GDN_HEADS, 1, 1), lambda cc: (0, 0, 0, 0))
    return col, scal, nw, st, acc_s


def _tok_cols(xs, nb, first):
    return jnp.stack([xs[b][:, first + h:first + h + 1] for b in range(nb) for h in range(GDN_HEADS)])


def _gdn_inputs(qkv_ref, p_gate_ref, p_small_ref, alog_ref, dtb_ref, nw_ref, nb):
    q, k, v = (_seq_heads(qkv_ref, nb, GDN_HEADS, i * GDN_HEADS) for i in range(3))
    xs = p_small_ref[...]
    return (q, k, v, _seq_heads(p_gate_ref, nb, GDN_HEADS), _tok_cols(xs, nb, 0), _tok_cols(xs, nb, GDN_HEADS),
            jnp.concatenate([alog_ref[...]] * nb), jnp.concatenate([dtb_ref[...]] * nb), nw_ref[...])


def _gdn_fwd(qkv, proj, alog, dtb, nw):
    nb, l, _ = qkv.shape
    n_c = l // CHUNK
    n_g = nb * GDN_HEADS
    col, scal, nws, st, _ = _gdn_specs(nb, n_c, False)

    def body(qkv_ref, pg_ref, ps_ref, alog_ref, dtb_ref, nw_ref, y_ref, st_ref, s_scr):
        @pl.when(pl.program_id(0) == 0)
        def _():
            s_scr[...] = jnp.zeros_like(s_scr)

        s_in = s_scr[...]
        st_ref[...] = s_in.reshape(st_ref.shape)
        y, s_new = _gdn_chunk(s_in, *_gdn_inputs(qkv_ref, pg_ref, ps_ref, alog_ref, dtb_ref, nw_ref, nb))
        _put_seq_heads(y_ref, y, nb, GDN_HEADS)
        s_scr[...] = s_new

    return pl.pallas_call(
        body, name="gdn_fwd", grid=(n_c,),
        in_specs=[col(GDN_QKV), col(GDN_WIDE, GDN_QKV // GDN_WIDE), col(LANES, GDN_SMALL_BLOCK), scal, scal, nws],
        out_specs=[col(GDN_WIDE), st],
        out_shape=[jax.ShapeDtypeStruct((nb, l, GDN_WIDE), BF16),
                   jax.ShapeDtypeStruct((nb, GDN_HEADS, n_c, LANES, LANES), F32)],
        scratch_shapes=[pltpu.VMEM((n_g, LANES, LANES), F32)],
        compiler_params=_params(("arbitrary",)),
    )(qkv, proj, proj, alog, dtb, nw)


def _gdn_bwd(qkv, proj, alog, dtb, nw, states, dy):
    nb, l, _ = qkv.shape
    n_c = l // CHUNK
    n_g = nb * GDN_HEADS
    col, scal, nws, st, acc_s = _gdn_specs(nb, n_c, True)

    def body(qkv_ref, pg_ref, ps_ref, alog_ref, dtb_ref, nw_ref, st_ref, dy_ref,
             dqkv_ref, dg_ref, dsm_ref, dalog_ref, ddtb_ref, dnw_ref, ds_scr):
        @pl.when(pl.program_id(0) == 0)
        def _():
            ds_scr[...] = jnp.zeros_like(ds_scr)
            dalog_ref[...] = jnp.zeros_like(dalog_ref)
            ddtb_ref[...] = jnp.zeros_like(ddtb_ref)
            dnw_ref[...] = jnp.zeros_like(dnw_ref)

        _, vjp = jax.vjp(_gdn_chunk, st_ref[...].reshape(n_g, LANES, LANES),
                         *_gdn_inputs(qkv_ref, pg_ref, ps_ref, alog_ref, dtb_ref, nw_ref, nb))
        ds, dq, dk, dv, dg, dbl, dal, dalog, ddtb, dnw = vjp((_seq_heads(dy_ref, nb, GDN_HEADS).astype(F32), ds_scr[...]))
        ds_scr[...] = ds
        for i, val in enumerate((dq, dk, dv)):
            _put_seq_heads(dqkv_ref, val, nb, GDN_HEADS, i * GDN_HEADS)
        _put_seq_heads(dg_ref, dg, nb, GDN_HEADS)
        lane = lax.broadcasted_iota(jnp.int32, (CHUNK, LANES), 1)
        for b in range(nb):
            small = jnp.zeros((CHUNK, LANES), F32)
            for h in range(GDN_HEADS):
                small = small + jnp.where(lane == h, dbl[b * GDN_HEADS + h], 0.0)
                small = small + jnp.where(lane == GDN_HEADS + h, dal[b * GDN_HEADS + h], 0.0)
            dsm_ref[b] = small.astype(dsm_ref.dtype)
        dalog_ref[...] += dalog.reshape(dalog_ref.shape)
        ddtb_ref[...] += ddtb.reshape(ddtb_ref.shape)
        dnw_ref[...] += dnw

    h = GDN_HEADS
    return pl.pallas_call(
        body, name="gdn_bwd", grid=(n_c,),
        in_specs=[col(GDN_QKV), col(GDN_WIDE, GDN_QKV // GDN_WIDE), col(LANES, GDN_SMALL_BLOCK), scal, scal, nws, st,
                  col(GDN_WIDE)],
        out_specs=[col(GDN_QKV), col(GDN_WIDE), col(LANES), acc_s, acc_s, nws],
        out_shape=[jax.ShapeDtypeStruct((nb, l, GDN_QKV), F32), jax.ShapeDtypeStruct((nb, l, GDN_WIDE), BF16),
                   jax.ShapeDtypeStruct((nb, l, LANES), BF16),
                   jax.ShapeDtypeStruct((nb, h, 1, 1), F32), jax.ShapeDtypeStruct((nb, h, 1, 1), F32),
                   jax.ShapeDtypeStruct((1, LANES), F32)],
        scratch_shapes=[pltpu.VMEM((n_g, LANES, LANES), F32)],
        compiler_params=_params(("arbitrary",)),
    )(qkv, proj, proj, alog, dtb, nw, states, dy)


S5_ROWS = S5_NSTATE // LANES
S5_TB = 256


def _s5_scan_fwd(bu, abar):
    nb, l = bu.shape[:2]
    tb = min(S5_TB, l)

    def body(bu_ref, a_ref, h_ref, c_scr):
        @pl.when(pl.program_id(1) == 0)
        def _():
            c_scr[...] = jnp.zeros_like(c_scr)

        ar = a_ref[0]
        ai = a_ref[1]

        def step(t, carry):
            hr, hi = carry
            nr = ar * hr - ai * hi + bu_ref[t, 0].astype(F32)
            ni = ar * hi + ai * hr + bu_ref[t, 1].astype(F32)
            h_ref[t, 0] = nr.astype(h_ref.dtype)
            h_ref[t, 1] = ni.astype(h_ref.dtype)
            return nr, ni

        hr, hi = lax.fori_loop(0, tb, step, (c_scr[0], c_scr[1]), unroll=8)
        c_scr[0] = hr
        c_scr[1] = hi

    blk = pl.BlockSpec((None, tb, 2, S5_ROWS, LANES), lambda b, i: (b, i, 0, 0, 0))
    return pl.pallas_call(
        body, name="s5_scan_fwd", grid=(nb, l // tb),
        in_specs=[blk, pl.BlockSpec((2, S5_ROWS, LANES), lambda b, i: (0, 0, 0))],
        out_specs=blk, out_shape=jax.ShapeDtypeStruct(bu.shape, bu.dtype),
        scratch_shapes=[pltpu.VMEM((2, S5_ROWS, LANES), F32)],
        compiler_params=_params(("parallel", "arbitrary")),
    )(bu, abar)


def _s5_scan_bwd(dh, h, abar):
    nb, l = dh.shape[:2]
    tb = min(S5_TB, l)
    n_blk = l // tb

    def body(dh_ref, h_ref, a_ref, g_ref, da_ref, c_scr):
        @pl.when(pl.program_id(1) == 0)
        def _():
            c_scr[...] = jnp.zeros_like(c_scr)
            da_ref[...] = jnp.zeros_like(da_ref)

        ar = a_ref[0]
        ai = a_ref[1]

        def step(s, carry):
            gr, gi, dar, dai = carry
            t = tb - 1 - s
            hr = h_ref[t, 0].astype(F32)
            hi = h_ref[t, 1].astype(F32)
            dar = dar + gr * hr + gi * hi
            dai = dai + gi * hr - gr * hi
            nr = ar * gr + ai * gi + dh_ref[t, 0].astype(F32)
            ni = ar * gi - ai * gr + dh_ref[t, 1].astype(F32)
            g_ref[t, 0] = nr.astype(g_ref.dtype)
            g_ref[t, 1] = ni.astype(g_ref.dtype)
            return nr, ni, dar, dai

        z = jnp.zeros((S5_ROWS, LANES), F32)
        gr, gi, dar, dai = lax.fori_loop(0, tb, step, (c_scr[0], c_scr[1], z, z), unroll=8)
        c_scr[0] = gr
        c_scr[1] = gi
        da_ref[0] += dar
        da_ref[1] += dai

    blk = pl.BlockSpec((None, tb, 2, S5_ROWS, LANES), lambda b, i: (b, n_blk - 1 - i, 0, 0, 0))
    return pl.pallas_call(
        body, name="s5_scan_bwd", grid=(nb, n_blk),
        in_specs=[blk, blk, pl.BlockSpec((2, S5_ROWS, LANES), lambda b, i: (0, 0, 0))],
        out_specs=[blk, pl.BlockSpec((None, 2, S5_ROWS, LANES), lambda b, i: (b, 0, 0, 0))],
        out_shape=[jax.ShapeDtypeStruct(dh.shape, dh.dtype), jax.ShapeDtypeStruct((nb, 2, S5_ROWS, LANES), F32)],
        scratch_shapes=[pltpu.VMEM((2, S5_ROWS, LANES), F32)],
        compiler_params=_params(("parallel", "arbitrary")),
    )(dh, h, abar)


_GELU_C = math.sqrt(2.0 / math.pi)


def _gelu(x):
    return 0.5 * x * (1.0 + jnp.tanh(_GELU_C * (x + 0.044715 * x * x * x)))


def _gelu_grad(x):
    t = jnp.tanh(_GELU_C * (x + 0.044715 * x * x * x))
    return 0.5 * (1.0 + t) + 0.5 * x * (1.0 - t * t) * _GELU_C * (1.0 + 3.0 * 0.044715 * x * x)


def _s5_post_fwd(yc, u, d, w_glu):
    t, w = yc.shape
    tt = _tile(t, (512, 256, 128))

    def body(yc_ref, u_ref, d_ref, w_ref, o_ref, y0_ref):
        y0 = yc_ref[...] + d_ref[...] * u_ref[...]
        y = _gelu(y0)
        z = _dot(y, w_ref[...])
        o_ref[...] = (y * _sigmoid(z)).astype(o_ref.dtype)
        y0_ref[...] = y0

    tok = pl.BlockSpec((tt, w), lambda i: (i, 0))
    return pl.pallas_call(
        body, name="s5_post_fwd", grid=(t // tt,),
        in_specs=[tok, tok, pl.BlockSpec((1, w), lambda i: (0, 0)), pl.BlockSpec((w, w), lambda i: (0, 0))],
        out_specs=[tok, tok],
        out_shape=[jax.ShapeDtypeStruct((t, w), BF16), jax.ShapeDtypeStruct((t, w), F32)],
        compiler_params=_params(("parallel",)),
    )(yc, u, d, w_glu)


def _s5_post_bwd(y0, u, d, w_glu, dout):
    t, w = y0.shape
    tt = _tile(t, (512, 256, 128))

    def body(y0_ref, u_ref, d_ref, w_ref, do_ref, dy0_ref, y_ref, dz_ref, du_ref, dd_ref):
        @pl.when(pl.program_id(0) == 0)
        def _():
            dd_ref[...] = jnp.zeros_like(dd_ref)

        y0 = y0_ref[...]
        y = _gelu(y0)
        s = _sigmoid(_dot(y, w_ref[...]))
        do = do_ref[...]
        dz = do * y * s * (1.0 - s)
        dy = do * s + _dot_nt(dz, w_ref[...])
        dy0 = dy * _gelu_grad(y0)
        dy0_ref[...] = dy0.astype(dy0_ref.dtype)
        y_ref[...] = y.astype(y_ref.dtype)
        dz_ref[...] = dz.astype(dz_ref.dtype)
        du_ref[...] = dy0 * d_ref[...]
        dd_ref[...] += jnp.sum(dy0 * u_ref[...], axis=0, keepdims=True)

    tok = pl.BlockSpec((tt, w), lambda i: (i, 0))
    vec = pl.BlockSpec((1, w), lambda i: (0, 0))
    return pl.pallas_call(
        body, name="s5_post_bwd", grid=(t // tt,),
        in_specs=[tok, tok, vec, pl.BlockSpec((w, w), lambda i: (0, 0)), tok],
        out_specs=[tok, tok, tok, tok, vec],
        out_shape=[jax.ShapeDtypeStruct((t, w), BF16), jax.ShapeDtypeStruct((t, w), BF16),
                   jax.ShapeDtypeStruct((t, w), BF16), jax.ShapeDtypeStruct((t, w), F32),
                   jax.ShapeDtypeStruct((1, w), F32)],
        compiler_params=_params(("arbitrary",)),
    )(y0, u, d, w_glu, dout)


def _s5_params(a_re, a_im, b_re, b_im, c_re, c_im, log_dt):
    dt = jnp.exp(log_dt)[:, None]
    mag = jnp.exp(dt * a_re)
    ang = dt * a_im
    abar_re = mag * jnp.cos(ang)
    abar_im = mag * jnp.sin(ang)
    den = a_re * a_re + a_im * a_im
    zr = abar_re - 1.0
    zi = abar_im
    coef_re = ((zr * a_re + zi * a_im) / den)[..., None]
    coef_im = ((zi * a_re - zr * a_im) / den)[..., None]
    bb_re = coef_re * b_re - coef_im * b_im
    bb_im = coef_re * b_im + coef_im * b_re
    eye = jnp.eye(S5_GROUPS, dtype=F32)

    def dense_in(bb):
        return jnp.einsum('gnp,gh->gphn', bb, eye).reshape(S5_WIDTH, S5_NSTATE)

    def dense_out(cc):
        return jnp.einsum('gpn,gh->gnhp', cc, eye).reshape(S5_NSTATE, S5_WIDTH)

    abar = jnp.stack([abar_re.reshape(S5_ROWS, LANES), abar_im.reshape(S5_ROWS, LANES)])
    bbd = jnp.concatenate([dense_in(bb_re), dense_in(bb_im)], axis=1)
    ccd = jnp.concatenate([dense_out(c_re), dense_out(-c_im)], axis=0)
    return abar, bbd, ccd


def _tok_tile(t):
    return _tile(t, (256, 128))


def _rms_fwd(x, w):
    t, d = x.shape
    tt = _tok_tile(t)

    def body(x_ref, w_ref, o_ref):
        xv = x_ref[...]
        r = lax.rsqrt(jnp.mean(xv * xv, axis=-1, keepdims=True) + NORM_EPS)
        o_ref[...] = (xv * r * w_ref[...]).astype(o_ref.dtype)

    tok = pl.BlockSpec((tt, d), lambda i: (i, 0))
    return pl.pallas_call(
        body, name="rms_fwd", grid=(t // tt,), in_specs=[tok, pl.BlockSpec((1, d), lambda i: (0, 0))],
        out_specs=tok, out_shape=jax.ShapeDtypeStruct((t, d), BF16), compiler_params=_params(("parallel",)),
    )(x, w)


def _rms_bwd_math(xv, wv, dy):
    r = lax.rsqrt(jnp.mean(xv * xv, axis=-1, keepdims=True) + NORM_EPS)
    xh = xv * r
    dxh = dy * wv
    dx = r * (dxh - xh * jnp.mean(dxh * xh, axis=-1, keepdims=True))
    return dx, jnp.sum(dy * xh, axis=0, keepdims=True)


def _rms_bwd(x, w, dy, dres):
    t, d = x.shape
    tt = _tok_tile(t)

    def body(x_ref, w_ref, dy_ref, dr_ref, dx_ref, dw_ref):
        @pl.when(pl.program_id(0) == 0)
        def _():
            dw_ref[...] = jnp.zeros_like(dw_ref)

        dx, dw = _rms_bwd_math(x_ref[...], w_ref[...], dy_ref[...])
        dx_ref[...] = dr_ref[...] + dx
        dw_ref[...] += dw

    tok = pl.BlockSpec((tt, d), lambda i: (i, 0))
    vec = pl.BlockSpec((1, d), lambda i: (0, 0))
    return pl.pallas_call(
        body, name="rms_bwd", grid=(t // tt,), in_specs=[tok, vec, tok, tok], out_specs=[tok, vec],
        out_shape=[jax.ShapeDtypeStruct((t, d), F32), jax.ShapeDtypeStruct((1, d), F32)],
        compiler_params=_params(("arbitrary",)),
    )(x, w, dy, dres)


def _loss_head(x, w, target):
    t, d = x.shape
    tt = _tok_tile(t)

    def body(x_ref, w_ref, t_ref, l_ref, dx_ref, dw_ref):
        @pl.when(pl.program_id(0) == 0)
        def _():
            l_ref[...] = jnp.zeros_like(l_ref)
            dw_ref[...] = jnp.zeros_like(dw_ref)

        xv = x_ref[...]
        wv = w_ref[...]
        r = lax.rsqrt(jnp.mean(xv * xv, axis=-1, keepdims=True) + NORM_EPS)
        err = xv * r * wv - t_ref[...]
        row = jnp.sum(err * err, axis=-1, keepdims=True)
        l_ref[...] += (0.5 / d) * jnp.sum(row, axis=0, keepdims=True)
        dx, dw = _rms_bwd_math(xv, wv, err * (1.0 / d))
        dx_ref[...] = dx
        dw_ref[...] += dw

    tok = pl.BlockSpec((tt, d), lambda i: (i, 0))
    vec = pl.BlockSpec((1, d), lambda i: (0, 0))
    return pl.pallas_call(
        body, name="loss_head", grid=(t // tt,), in_specs=[tok, vec, tok],
        out_specs=[pl.BlockSpec((1, 1), lambda i: (0, 0)), tok, vec],
        out_shape=[jax.ShapeDtypeStruct((1, 1), F32), jax.ShapeDtypeStruct((t, d), F32), jax.ShapeDtypeStruct((1, d), F32)],
        compiler_params=_params(("arbitrary",)),
    )(x, w, target)


FF_PAD = 3072
FF_SHARD = D_FF // N_DEV
FF_SHARD_PAD = FF_PAD // N_DEV


def _ffn_in(x, nw, wg_t, wu_t):
    t, d = x.shape
    tm = _tile(t, (512, 256, 128))
    tn = 1024

    def body(x_ref, nw_ref, wg_ref, wu_ref, h_ref, g_ref, u_ref, a_ref):
        @pl.when(pl.program_id(1) == 0)
        def _():
            xv = x_ref[...]
            r = lax.rsqrt(jnp.mean(xv * xv, axis=-1, keepdims=True) + NORM_EPS)
            h_ref[...] = (xv * r * nw_ref[...]).astype(h_ref.dtype)

        h = h_ref[...]
        g = _dot_nt(h, wg_ref[...])
        u = _dot_nt(h, wu_ref[...])
        g_ref[...] = g.astype(g_ref.dtype)
        u_ref[...] = u.astype(u_ref.dtype)
        a_ref[...] = (g * _sigmoid(g) * u).astype(a_ref.dtype)

    tok = pl.BlockSpec((tm, d), lambda i, j: (i, 0))
    wsp = pl.BlockSpec((tn, d), lambda i, j: (j, 0))
    wide = pl.BlockSpec((tm, tn), lambda i, j: (i, j))
    return pl.pallas_call(
        body, name="ffn_in", grid=(t // tm, FF_PAD // tn),
        in_specs=[tok, pl.BlockSpec((1, d), lambda i, j: (0, 0)), wsp, wsp], out_specs=[tok, wide, wide, wide],
        out_shape=[jax.ShapeDtypeStruct((t, d), BF16)] + [jax.ShapeDtypeStruct((t, FF_PAD), BF16)] * 3,
        compiler_params=_params(("parallel", "arbitrary")),
    )(x, nw, wg_t, wu_t)


def _ffn_out_dx(dxo, wd, g, u):
    t, d = dxo.shape
    tm = _tile(t, (512, 256, 128))
    tn = 1024

    def body(dx_ref, wd_ref, g_ref, u_ref, dg_ref, du_ref):
        da = 0.5 * _dot_nt(dx_ref[...], wd_ref[...])
        g = g_ref[...].astype(F32)
        s = _sigmoid(g)
        dg_ref[...] = (da * u_ref[...].astype(F32) * (s * (1.0 + g * (1.0 - s)))).astype(dg_ref.dtype)
        du_ref[...] = (da * g * s).astype(du_ref.dtype)

    wide = pl.BlockSpec((tm, tn), lambda j, i: (i, j))
    return pl.pallas_call(
        body, name="ffn_out_dx", grid=(FF_PAD // tn, t // tm),
        in_specs=[pl.BlockSpec((tm, d), lambda j, i: (i, 0)), pl.BlockSpec((tn, d), lambda j, i: (j, 0)), wide, wide],
        out_specs=[wide, wide], out_shape=[jax.ShapeDtypeStruct((t, FF_PAD), BF16)] * 2,
        compiler_params=_params(("parallel", "parallel")),
    )(dxo, wd, g, u)


def _ffn_in_dx(dg, du, wg_t, wu_t, x, nw, dxo):
    t, d = x.shape
    tm = _tile(t, (256, 128))

    def body(dg_ref, du_ref, wg_ref, wu_ref, x_ref, nw_ref, dr_ref, dx_ref, dw_ref):
        @pl.when(pl.program_id(0) == 0)
        def _():
            dw_ref[...] = jnp.zeros_like(dw_ref)

        dh = _dot(dg_ref[...], wg_ref[...]) + _dot(du_ref[...], wu_ref[...])
        dx, dw = _rms_bwd_math(x_ref[...], nw_ref[...], dh)
        dx_ref[...] = dr_ref[...] + dx
        dw_ref[...] += dw

    wide = pl.BlockSpec((tm, FF_PAD), lambda i: (i, 0))
    wsp = pl.BlockSpec((FF_PAD, d), lambda i: (0, 0))
    tok = pl.BlockSpec((tm, d), lambda i: (i, 0))
    vec = pl.BlockSpec((1, d), lambda i: (0, 0))
    return pl.pallas_call(
        body, name="ffn_in_dx", grid=(t // tm,), in_specs=[wide, wide, wsp, wsp, tok, vec, tok], out_specs=[tok, vec],
        out_shape=[jax.ShapeDtypeStruct((t, d), F32), jax.ShapeDtypeStruct((1, d), F32)],
        compiler_params=_params(("arbitrary",)),
    )(dg, du, wg_t, wu_t, x, nw, dxo)


def _adamw(w, g, m, v):
    r, c = w.shape
    tr = _tile(r, (512, 256, 128, 64, 32, 16, 8))
    c1 = 1.0 - ADAM_B1 ** ADAM_STEP
    c2 = 1.0 - ADAM_B2 ** ADAM_STEP

    def body(w_ref, g_ref, m_ref, v_ref, d_ref, nm_ref, nv_ref):
        gv = g_ref[...]
        nm = ADAM_B1 * m_ref[...] + (1.0 - ADAM_B1) * gv
        nv = ADAM_B2 * v_ref[...] + (1.0 - ADAM_B2) * (gv * gv)
        d_ref[...] = -ADAM_LR * ((nm / c1) / (jnp.sqrt(nv / c2) + ADAM_EPS) + ADAM_WD * w_ref[...])
        nm_ref[...] = nm
        nv_ref[...] = nv

    blk = pl.BlockSpec((tr, c), lambda i: (i, 0))
    return pl.pallas_call(
        body, name="adamw", grid=(r // tr,), in_specs=[blk] * 4, out_specs=[blk] * 3,
        out_shape=[jax.ShapeDtypeStruct((r, c), F32)] * 3, compiler_params=_params(("parallel",)),
    )(w, g, m, v)


N_KINDS = 3


def _ffn_slot(which, kind):
    return which * N_KINDS + kind


def _ffn_fwd(x, nw, w_ffn, which):
    wg, wu, wd = (w_ffn[_ffn_slot(which, kind)] for kind in range(3))
    h, g, u, a = _ffn_in(x, nw, wg, wu)
    return _mm(a, wd, name="ffn_out", res=x, scale=0.5), (x, h, g, u, a)


def _ffn_bwd(dxo, res, nw, w_ffn, which):
    x, h, g, u, a = res
    wg, wu, wd = (w_ffn[_ffn_slot(which, kind)] for kind in range(3))
    dg, du = _ffn_out_dx(dxo, wd, g, u)
    g_ffn = lax.empty((N_KINDS,) + wd.shape, BF16)
    g_ffn = _mm_tn(a, dxo, name="ffn_out_dw", scale=0.5, into=g_ffn, slot=2)
    g_ffn = _mm_tn(dg, h, name="ffn_in_dw", into=g_ffn, slot=0)
    g_ffn = _mm_tn(du, h, name="ffn_in_dw", into=g_ffn, slot=1)
    dx, dnw = _ffn_in_dx(dg, du, wg, wu, x, nw, dxo)
    return dx, dnw, g_ffn


def _hg_lower_bounds(logits):
    p = jax.nn.softmax(logits, axis=0)
    return jnp.cumsum(p, axis=0) - p[0]


def _even_fwd(x1, p, nb, l):
    t = nb * l
    h = _rms_fwd(x1, p["mix_norm"])
    proj = _mm(h, p["w_in_t"], name="ev_in", nt=True)
    proj3 = proj.reshape(nb, l, EV_IN)
    ya, hst = _hgrn2_fwd(proj3, p["lb"], p["hg_nw"])
    u = proj[:, 4 * HG_QK:]
    bu = _mm(u, p["bbd"], name="s5_in", out_dtype=BF16)
    hs = _s5_scan_fwd(bu.reshape(nb, l, 2, S5_ROWS, LANES), p["abar"])
    yc = _mm(hs.reshape(t, 2 * S5_NSTATE), p["ccd"], name="s5_out")
    yb, y0 = _s5_post_fwd(yc, u, p["s5_d"], p["w_glu"])
    ycat = jnp.concatenate([ya.reshape(t, HG_QK), yb], axis=1)
    x2 = _mm(ycat, p["w_out"], name="mix_out", res=x1)
    return x2, (x1, h, proj3, hst, u, hs, y0, ycat)


def _even_bwd(dx2, res, p, nb, l):
    x1, h, proj3, hst, u, hs, y0, ycat = res
    t = nb * l
    dycat = _mm(dx2, p["w_out"], name="mix_out_dx", nt=True)
    dw_out = _mm_tn(ycat, dx2, name="mix_out_dw", out_dtype=BF16)
    dp_hg, dlb, dhg_nw = _hgrn2_bwd(proj3, p["lb"], p["hg_nw"], hst, dycat.reshape(nb, l, D_MODEL))
    dy0, y, dz, du_d, dd = _s5_post_bwd(y0, u, p["s5_d"], p["w_glu"], dycat[:, HG_QK:])
    dw_glu = _mm_tn(y, dz, name="s5_glu_dw", out_dtype=BF16)
    dhs = _mm(dy0, p["ccd"], name="s5_out_dx", nt=True, out_dtype=BF16)
    dccd = _mm_tn(hs.reshape(t, 2 * S5_NSTATE), dy0, name="s5_out_dw")
    g5, dabar = _s5_scan_bwd(dhs.reshape(nb, l, 2, S5_ROWS, LANES), hs, p["abar"])
    g2 = g5.reshape(t, 2 * S5_NSTATE)
    du = _mm(g2, p["bbd"], name="s5_in_dx", nt=True, res=du_d, out_dtype=BF16)
    dbbd = _mm_tn(u, g2, name="s5_in_dw")
    dproj = jnp.concatenate([dp_hg.reshape(t, 4 * HG_QK), du], axis=1)
    dw_in_t = _mm_tn(dproj, h, name="ev_in_dw", out_dtype=BF16)
    dh = _mm(dproj, p["w_in_t"], name="ev_in_dx")
    dx1, dmix = _rms_bwd(x1, p["mix_norm"], dh, dx2)
    small = dict(mix_norm=dmix, lb=dlb.sum(0).reshape(HG_QK), hg_nw=dhg_nw.reshape(HEAD_DIM),
                 abar=dabar.sum(0), bbd=dbbd, ccd=dccd, s5_d=dd.reshape(S5_WIDTH))
    return dx1, dict(w_in_t=dw_in_t, w_glu=dw_glu, w_out=dw_out), small


def _odd_fwd(x1, p, nb, l):
    t = nb * l
    h = _rms_fwd(x1, p["mix_norm"])
    proj3 = _mm(h, p["w_in"], name="od_in").reshape(nb, l, OD_IN_PAD)
    qkv = _gdn_pre_fwd(proj3, p["conv_w"])
    y, st = _gdn_fwd(qkv, proj3, p["a_log"], p["dt_bias"], p["gdn_nw"])
    y2 = y.reshape(t, D_MODEL)
    x2 = _mm(y2, p["w_out"], name="mix_out", res=x1)
    return x2, (x1, h, proj3, qkv, st, y2)


def _odd_bwd(dx2, res, p, nb, l):
    x1, h, proj3, qkv, st, y2 = res
    t = nb * l
    dy = _mm(dx2, p["w_out"], name="mix_out_dx", nt=True)
    dw_out = _mm_tn(y2, dx2, name="mix_out_dw", out_dtype=BF16)
    dqkv, dgate, dsmall, dalog, ddtb, dnw = _gdn_bwd(
        qkv, proj3, p["a_log"], p["dt_bias"], p["gdn_nw"], st, dy.reshape(nb, l, D_MODEL))
    dx_qkv, dcw = _gdn_pre_bwd(proj3, p["conv_w"], dqkv)
    dproj = jnp.concatenate([dx_qkv, dgate, dsmall], axis=-1).reshape(t, OD_IN_PAD)
    dw_in = _mm_tn(h, dproj, name="od_in_dw", out_dtype=BF16)[:, :OD_IN]
    dh = _mm(dproj, p["w_in"], name="od_in_dx", nt=True)
    dx1, dmix = _rms_bwd(x1, p["mix_norm"], dh, dx2)
    small = dict(mix_norm=dmix, a_log=dalog.sum(0).reshape(GDN_HEADS), dt_bias=ddtb.sum(0).reshape(GDN_HEADS),
                 gdn_nw=dnw.reshape(HEAD_DIM))
    return dx1, dict(w_in=dw_in, conv_w=dcw.sum(0), w_out=dw_out), small


def _local_step(x, target, layer_weights, small, layer_done):
    nb, l, _ = x.shape
    t = nb * l
    lbs, lbs_vjp = jax.vjp(_hg_lower_bounds, small["hg_lb_logits"])
    row = lambda v: v.reshape(1, -1)
    s5_vjps = {}

    def layer_params(layer, big):
        j = layer // 2
        p = dict(mix_norm=row(small["mix_norm"][layer]), w_out=big["w_out"])
        if layer % 2 == 0:
            (abar, bbd, ccd), s5_vjps[j] = jax.vjp(
                _s5_params, small["s5_a_re"][j], small["s5_a_im"][j], small["s5_b_re"][j], small["s5_b_im"][j],
                small["s5_c_re"][j], small["s5_c_im"][j], small["s5_log_dt"][j])
            p.update(w_in_t=big["w_in_t"], w_glu=big["w_glu"],
                     lb=lbs[j].reshape(HG_HEADS, 1, HEAD_DIM), hg_nw=row(small["hg_norm_w"][j]),
                     abar=abar, bbd=bbd.astype(BF16), ccd=ccd.astype(BF16), s5_d=row(small["s5_d"][j]))
        else:
            p.update(w_in=big["w_in"], conv_w=big["conv_w"],
                     a_log=small["gdn_a_log"][j].reshape(GDN_HEADS, 1, 1),
                     dt_bias=small["gdn_dt_bias"][j].reshape(GDN_HEADS, 1, 1), gdn_nw=row(small["gdn_norm_w"][j]))
        return p

    xs = x.reshape(t, D_MODEL)
    saved, layers, ffn_w = [], [], []
    for layer in range(DEPTH):
        big = layer_weights(layer, xs)
        p = layer_params(layer, big)
        xs, r1 = _ffn_fwd(xs, row(small["ffn1_norm"][layer]), big["ffn"], 0)
        xs, r2 = (_even_fwd if layer % 2 == 0 else _odd_fwd)(xs, p, nb, l)
        xs, r3 = _ffn_fwd(xs, row(small["ffn2_norm"][layer]), big["ffn"], 1)
        saved.append((r1, r2, r3))
        layers.append(p)
        ffn_w.append(big["ffn"])
    loss, dx, dfinal = _loss_head(xs, row(small["final_norm"]), target.reshape(t, D_MODEL))

    gs = {k: [None] * DEPTH for k in ("ffn1_norm", "mix_norm", "ffn2_norm")}
    gs.update({k: [None] * 2 for k in ("hg_norm_w", "s5_a_re", "s5_a_im", "s5_b_re", "s5_b_im", "s5_c_re", "s5_c_im",
                                       "s5_d", "s5_log_dt", "gdn_a_log", "gdn_dt_bias", "gdn_norm_w")})
    dlbs = [None] * 2
    for layer in reversed(range(DEPTH)):
        j = layer // 2
        p = layers[layer]
        r1, r2, r3 = saved[layer]
        dx, dnw, g_ffn = _ffn_bwd(dx, r3, row(small["ffn2_norm"][layer]), ffn_w[layer], 1)
        gs["ffn2_norm"][layer] = dnw[0]
        if layer % 2 == 0:
            dx, gw, sm = _even_bwd(dx, r2, p, nb, l)
            dlbs[j] = sm["lb"]
            gs["hg_norm_w"][j] = sm["hg_nw"]
            gs["s5_d"][j] = sm["s5_d"]
            (gs["s5_a_re"][j], gs["s5_a_im"][j], gs["s5_b_re"][j], gs["s5_b_im"][j], gs["s5_c_re"][j],
             gs["s5_c_im"][j], gs["s5_log_dt"][j]) = s5_vjps[j]((sm["abar"], sm["bbd"], sm["ccd"]))
        else:
            dx, gw, sm = _odd_bwd(dx, r2, p, nb, l)
            gs["gdn_a_log"][j], gs["gdn_dt_bias"][j], gs["gdn_norm_w"][j] = sm["a_log"], sm["dt_bias"], sm["gdn_nw"]
        gs["mix_norm"][layer] = sm["mix_norm"][0]
        layer_done(layer, 0, dict(gw, ffn=g_ffn))
        dx, dnw, g_ffn = _ffn_bwd(dx, r1, row(small["ffn1_norm"][layer]), ffn_w[layer], 0)
        gs["ffn1_norm"][layer] = dnw[0]
        layer_done(layer, 1, dict(ffn=g_ffn))
    gsmall = {k: jnp.stack(v) for k, v in gs.items()}
    gsmall["hg_lb_logits"] = lbs_vjp(jnp.stack(dlbs))[0]
    gsmall["final_norm"] = dfinal[0]
    return loss[0, 0], dx.reshape(nb, l, D_MODEL), gsmall


def _here():
    return lax.axis_index("x"), lax.axis_index("y"), lax.axis_index("c")


def _other_chips(x, y):
    return [(1 - x, y), (x, 1 - y), (1 - x, 1 - y)]


def _all_gather(blocks, name):
    n = len(blocks)

    def body(*refs):
        x_refs, o_refs = refs[:n], refs[n:2 * n]
        send_sems, recv_sems, local_sems = refs[2 * n:]
        x, y, cc = _here()
        me, sibling = (x, y, cc), (x, y, 1 - cc)
        chips = _other_chips(x, y)

        def win(i, p):
            return o_refs[i].at[:, 4 * p[0] + 2 * p[1] + p[2]]

        def copy(i, k, blk, to, src=None):
            return pltpu.make_async_remote_copy(
                src_ref=win(i, blk) if src is None else src, dst_ref=win(i, blk),
                send_sem=send_sems.at[i, k], recv_sem=recv_sems.at[i, k], device_id=to, device_id_type=MESH)

        mine = [pltpu.make_async_copy(x_refs[i], win(i, me), local_sems.at[i]) for i in range(n)]
        for cp in mine:
            cp.start()
        first = []
        for i in range(n):
            first.append(copy(i, 0, me, sibling, src=x_refs[i]))
            first += [copy(i, 1 + j, me, (*chip, cc), src=x_refs[i]) for j, chip in enumerate(chips)]
        for cp in first:
            cp.start()
        passed = []
        for j, chip in enumerate(chips):
            for i in range(n):
                copy(i, 1 + j, (*chip, cc), me).wait_recv()
                fwd = copy(i, 4 + j, (*chip, cc), sibling)
                fwd.start()
                passed.append(fwd)
        for i in range(n):
            copy(i, 0, sibling, me).wait_recv()
            for j, chip in enumerate(chips):
                copy(i, 4 + j, (*chip, 1 - cc), me).wait_recv()
        for cp in first + passed:
            cp.wait_send()
        for cp in mine:
            cp.wait()

    hbm = pl.BlockSpec(memory_space=pl.ANY)
    return pl.pallas_call(
        body, name=name, in_specs=[hbm] * n, out_specs=[hbm] * n,
        out_shape=[jax.ShapeDtypeStruct((b.shape[0], N_DEV) + b.shape[1:], b.dtype) for b in blocks],
        scratch_shapes=[pltpu.SemaphoreType.DMA((n, 7)), pltpu.SemaphoreType.DMA((n, 7)), pltpu.SemaphoreType.DMA((n,))],
    )(*blocks)


_HBM = pl.BlockSpec(memory_space=pltpu.HBM)
_SEM = pl.BlockSpec(memory_space=pltpu.SEMAPHORE)
_DATAFLOW = pltpu.SideEffectType.DATAFLOW_SIDE_EFFECTING


def _peer(k):
    x, y, c = _here()
    return (x ^ ((k >> 2) & 1), y ^ ((k >> 1) & 1), c ^ (k & 1))


def _device_of(p):
    return 4 * p[0] + 2 * p[1] + p[2]


def _exchange_start(srcs, lands, copies, name, after=None):
    n = len(srcs)

    n_in = 2 * n + (after is not None)

    def body(*refs):
        s_refs, l_refs, sems = refs[:n], refs[n:2 * n], refs[n_in:n_in + 2 * n]
        for i in range(n):
            for k in range(1, N_DEV):
                src, dst = copies(s_refs[i], l_refs[i], k)
                pltpu.make_async_remote_copy(
                    src_ref=src, dst_ref=dst, send_sem=sems[2 * i], recv_sem=sems[2 * i + 1],
                    device_id=_peer(k), device_id_type=MESH).start()

    both = list(srcs) + list(lands)
    outs = pl.pallas_call(
        body, name=name,
        out_shape=tuple([pltpu.SemaphoreType.DMA(())] * (2 * n)) + tuple(pltpu.HBM(a.shape, a.dtype) for a in both),
        in_specs=[_HBM] * (2 * n) + [pl.BlockSpec(memory_space=pl.ANY)] * (after is not None),
        out_specs=tuple([_SEM] * (2 * n)) + tuple([_HBM] * (2 * n)),
        input_output_aliases={i: 2 * n + i for i in range(2 * n)},
        compiler_params=pltpu.CompilerParams(has_side_effects=_DATAFLOW),
    )(*[pltpu.with_memory_space_constraint(a, pltpu.HBM) for a in both], *([after] if after is not None else []))
    return list(outs[:2 * n]), list(outs[2 * n:3 * n]), list(outs[3 * n:])


def _exchange_wait(handle, seven, after, name):
    sems, srcs, lands = handle
    n = len(srcs)

    def body(*refs):
        l_refs, sem_refs = refs[n:2 * n], refs[2 * n:4 * n]
        for i in range(n):
            all_seven = seven(l_refs[i])
            cp = pltpu.make_async_remote_copy(
                src_ref=all_seven, dst_ref=all_seven, send_sem=sem_refs[2 * i], recv_sem=sem_refs[2 * i + 1],
                device_id=_peer(1), device_id_type=MESH)
            cp.wait_send()
            cp.wait_recv()

    both = list(srcs) + list(lands)
    outs = pl.pallas_call(
        body, name=name, out_shape=tuple(pltpu.HBM(a.shape, a.dtype) for a in both),
        in_specs=[_HBM] * (2 * n) + [_SEM] * (2 * n) + [pl.BlockSpec(memory_space=pl.ANY)],
        out_specs=tuple([_HBM] * (2 * n)), input_output_aliases={i: i for i in range(2 * n)},
        compiler_params=pltpu.CompilerParams(has_side_effects=_DATAFLOW),
    )(*both, *sems, after)
    return list(outs[:n]), list(outs[n:])


def _gather_copies(src, land, k):
    del k
    return src, land.at[:, _device_of(_here())]


def _gather_seven(land):
    return land.at[:, pl.ds(0, N_DEV - 1)]


def _scatter_copies(src, land, k):
    return src.at[:, _device_of(_peer(k))], land.at[k - 1]


def _scatter_seven(land):
    return land


def _row_tile(w):
    return w if w <= 512 else _tile(w, (512, 256, 128))


def _rs_sum(g, land):
    a, _, w, c = g.shape
    tr = _row_tile(w)

    def body(g_ref, l_ref, o_ref):
        s = g_ref[...].astype(F32)
        for k in range(N_DEV - 1):
            s = s + l_ref[k].astype(F32)
        o_ref[...] = s

    def device():
        return _device_of(_here())

    return pl.pallas_call(
        body, name="rs_sum", grid=(a, w // tr),
        in_specs=[pl.BlockSpec((None, None, tr, c), lambda aa, i: (aa, device(), i, 0)),
                  pl.BlockSpec((N_DEV - 1, None, tr, c), lambda aa, i: (0, aa, i, 0))],
        out_specs=pl.BlockSpec((None, tr, c), lambda aa, i: (aa, i, 0)),
        out_shape=jax.ShapeDtypeStruct((a, w, c), F32),
        compiler_params=_params(("parallel", "parallel")),
    )(g, land)


def _sum_slots(g):
    _, n, r, c = g.shape

    def body(g_ref, o_ref):
        s = g_ref[0, 0]
        for k in range(1, n):
            s = s + g_ref[0, k]
        o_ref[...] = s

    return pl.pallas_call(
        body, name="sum_slots", grid=(1,),
        in_specs=[pl.BlockSpec((1, n, r, c), lambda i: (0, 0, 0, 0))], out_specs=pl.BlockSpec((r, c), lambda i: (0, 0)),
        out_shape=jax.ShapeDtypeStruct((r, c), F32), compiler_params=_params(("arbitrary",)),
    )(g)


_WEIGHTS = ['ffn1_norm', 'ffn1_w_gate', 'ffn1_w_up', 'ffn1_w_down', 'mix_norm', 'ffn2_norm', 'ffn2_w_gate', 'ffn2_w_up',
            'ffn2_w_down', 'ev_w_in', 'hg_lb_logits', 'hg_norm_w', 's5_a_re', 's5_a_im', 's5_b_re', 's5_b_im', 's5_c_re',
            's5_c_im', 's5_d', 's5_log_dt', 's5_w_glu', 'ev_w_out', 'od_w_in', 'gdn_conv_w', 'gdn_a_log', 'gdn_dt_bias',
            'gdn_norm_w', 'od_w_out', 'final_norm']
_SHARDED = ['ffn1_w_gate', 'ffn1_w_up', 'ffn1_w_down', 'ffn2_w_gate', 'ffn2_w_up', 'ffn2_w_down', 'ev_w_in', 's5_w_glu',
            'ev_w_out', 'od_w_in', 'gdn_conv_w', 'od_w_out']
_REPLICATED = [n for n in _WEIGHTS if n not in _SHARDED]
_SMALL_COLS = 1024


def _pad_rows(a, rows):
    return jnp.pad(a, ((0, rows - a.shape[0]), (0, 0)))


def _layer_blocks(w, layer):
    j = layer // 2
    ffn = []
    for f in ('ffn1', 'ffn2'):
        ffn += [_pad_rows(w[f + '_w_gate'][layer].T, FF_SHARD_PAD), _pad_rows(w[f + '_w_up'][layer].T, FF_SHARD_PAD),
                _pad_rows(w[f + '_w_down'][layer], FF_SHARD_PAD)]
    ffn = jnp.stack(ffn).astype(BF16)
    if layer % 2 == 0:
        return [ffn, w['ev_w_in'][j].T.astype(BF16)[None], w['s5_w_glu'][j].astype(BF16)[None],
                w['ev_w_out'][j].astype(BF16)[None]]
    return [ffn, w['od_w_in'][j].astype(BF16)[None], w['gdn_conv_w'][j][None], w['od_w_out'][j].astype(BF16)[None]]


def _layer_whole(gathered, layer):
    whole = lambda g: g.reshape(g.shape[0], N_DEV * g.shape[2], g.shape[3])
    ffn, a1, a2, w_out = gathered
    if layer % 2 == 0:
        return dict(ffn=whole(ffn), w_in_t=whole(a1)[0], w_glu=whole(a2)[0], w_out=whole(w_out)[0])
    w_in = jnp.moveaxis(a1[0], 0, 1).reshape(D_MODEL, OD_IN)
    conv = jnp.moveaxis(a2[0], 0, 1).reshape(CONV_W, GDN_QKV)
    return dict(ffn=whole(ffn), w_in=jnp.pad(w_in, ((0, 0), (0, OD_IN_PAD - OD_IN))), conv_w=conv, w_out=whole(w_out)[0])


def _layer_grad_windows(g, layer):
    windows = lambda a, w: a.reshape(-1, N_DEV, w, a.shape[-1])
    ffn = windows(g['ffn'], FF_SHARD_PAD)
    if 'w_out' not in g:
        return [ffn]
    w_out = windows(g['w_out'], D_MODEL // N_DEV)
    if layer % 2 == 0:
        return [ffn, windows(g['w_in_t'], EV_IN // N_DEV), windows(g['w_glu'], S5_WIDTH // N_DEV), w_out]
    w_in = jnp.moveaxis(g['w_in'].reshape(D_MODEL, N_DEV, OD_IN // N_DEV), 1, 0)[None]
    conv = jnp.moveaxis(g['conv_w'].reshape(CONV_W, N_DEV, GDN_QKV // N_DEV), 1, 0)[None]
    return [ffn, w_in, conv, w_out]


def _flat_small(src, prefix=''):
    flat = jnp.concatenate([src[prefix + n].reshape(-1) for n in _REPLICATED])
    rows = -(-(flat.shape[0] + 1) // (_SMALL_COLS * 8)) * 8
    return jnp.pad(flat, (0, rows * _SMALL_COLS - flat.shape[0])).reshape(rows, _SMALL_COLS)


def _split_small(flat2d, like):
    flat, out, off = flat2d.reshape(-1), {}, 0
    for n in _REPLICATED:
        size = math.prod(like[n].shape)
        out[n] = flat[off:off + size].reshape(like[n].shape)
        off += size
    return out, flat[off]


def _all_reduce_small(gsmall, loss):
    flat = _flat_small(gsmall)
    n_used = sum(math.prod(gsmall[n].shape) for n in _REPLICATED)
    flat = flat.reshape(-1).at[n_used].set(loss).reshape(flat.shape)
    (gathered,) = _all_gather([flat[None]], "gather_small")
    return _split_small(_sum_slots(gathered), gsmall)


def _as_2d(a):
    return a.reshape(-1, a.shape[-1])


def kernel(x, ffn1_norm, ffn1_w_gate, ffn1_w_up, ffn1_w_down, mix_norm, ffn2_norm, ffn2_w_gate, ffn2_w_up, ffn2_w_down, ev_w_in, hg_lb_logits, hg_norm_w, s5_a_re, s5_a_im, s5_b_re, s5_b_im, s5_c_re, s5_c_im, s5_d, s5_log_dt, s5_w_glu, ev_w_out, od_w_in, gdn_conv_w, gdn_a_log, gdn_dt_bias, gdn_norm_w, od_w_out, final_norm, loss_target, m_ffn1_norm, m_ffn1_w_gate, m_ffn1_w_up, m_ffn1_w_down, m_mix_norm, m_ffn2_norm, m_ffn2_w_gate, m_ffn2_w_up, m_ffn2_w_down, m_ev_w_in, m_hg_lb_logits, m_hg_norm_w, m_s5_a_re, m_s5_a_im, m_s5_b_re, m_s5_b_im, m_s5_c_re, m_s5_c_im, m_s5_d, m_s5_log_dt, m_s5_w_glu, m_ev_w_out, m_od_w_in, m_gdn_conv_w, m_gdn_a_log, m_gdn_dt_bias, m_gdn_norm_w, m_od_w_out, m_final_norm, v_ffn1_norm, v_ffn1_w_gate, v_ffn1_w_up, v_ffn1_w_down, v_mix_norm, v_ffn2_norm, v_ffn2_w_gate, v_ffn2_w_up, v_ffn2_w_down, v_ev_w_in, v_hg_lb_logits, v_hg_norm_w, v_s5_a_re, v_s5_a_im, v_s5_b_re, v_s5_b_im, v_s5_c_re, v_s5_c_im, v_s5_d, v_s5_log_dt, v_s5_w_glu, v_ev_w_out, v_od_w_in, v_gdn_conv_w, v_gdn_a_log, v_gdn_dt_bias, v_gdn_norm_w, v_od_w_out, v_final_norm):
    given = dict(locals())
    w = {n: given[n] for n in _WEIGHTS}
    small = {n: w[n] for n in _REPLICATED}
    me = _device_of(_here())

    blocks = [_layer_blocks(w, layer) for layer in range(DEPTH)]
    first = _all_gather(blocks[0], "gather_weights")
    pending = {}

    def start_gather(layer, after=None):
        lands = [lax.dynamic_update_slice(lax.empty((b.shape[0], N_DEV) + b.shape[1:], b.dtype), b[:, None], (0, me, 0, 0))
                 for b in blocks[layer]]
        pending[layer] = _exchange_start(blocks[layer], lands, _gather_copies, "gather_start_%d" % layer, after)

    start_gather(1)

    def layer_weights(layer, after):
        if layer == 0:
            return _layer_whole(first, 0)
        got = _exchange_wait(pending[layer], _gather_seven, after, "gather_wait_%d" % layer)[1]
        if layer + 1 < DEPTH:
            start_gather(layer + 1, after)
        return _layer_whole(got, layer)

    sent = {}

    def layer_done(layer, part, grads):
        gs = _layer_grad_windows(grads, layer)
        lands = [lax.empty((N_DEV - 1, g.shape[0]) + g.shape[2:], g.dtype) for g in gs]
        sent[layer, part] = _exchange_start(gs, lands, _scatter_copies, "scatter_start_%d_%d" % (layer, part))

    loss_part, grad_x, gsmall = _local_step(given['x'], given['loss_target'], layer_weights, small, layer_done)
    gsum, loss = _all_reduce_small(gsmall, loss_part)
    summed = {}
    for layer in reversed(range(DEPTH)):
        for part in range(2):
            partials, lands = _exchange_wait(sent[layer, part], _scatter_seven, grad_x, "scatter_wait_%d_%d" % (layer, part))
            summed[layer, part] = [_rs_sum(g, land) for g, land in zip(partials, lands)]

    grads = dict(gsum)
    t_ = lambda a: jnp.swapaxes(a, 1, 2)
    for i, f in enumerate(('ffn1', 'ffn2')):
        ffn = jnp.stack([summed[layer, 1 - i][0][:, :FF_SHARD] for layer in range(DEPTH)], axis=1)
        grads[f + '_w_gate'], grads[f + '_w_up'], grads[f + '_w_down'] = t_(ffn[0]), t_(ffn[1]), ffn[2]
    even, odd = [summed[0, 0], summed[2, 0]], [summed[1, 0], summed[3, 0]]
    grads['ev_w_in'] = jnp.stack([r[1][0].T for r in even])
    grads['s5_w_glu'] = jnp.stack([r[2][0] for r in even])
    grads['ev_w_out'] = jnp.stack([r[3][0] for r in even])
    grads['od_w_in'] = jnp.stack([r[1][0] for r in odd])
    grads['gdn_conv_w'] = jnp.stack([r[2][0] for r in odd])
    grads['od_w_out'] = jnp.stack([r[3][0] for r in odd])

    delta, new_m, new_v = {}, {}, {}
    for n in _SHARDED:
        d, nm, nv = _adamw(_as_2d(w[n]), _as_2d(grads[n]), _as_2d(given['m_' + n]), _as_2d(given['v_' + n]))
        delta[n], new_m[n], new_v[n] = (a.reshape(w[n].shape) for a in (d, nm, nv))
    d, nm, nv = _adamw(_flat_small(w), _flat_small(grads), _flat_small(given, 'm_'), _flat_small(given, 'v_'))
    (delta_s, _), (new_m_s, _), (new_v_s, _) = (_split_small(a, small) for a in (d, nm, nv))
    delta.update(delta_s)
    new_m.update(new_m_s)
    new_v.update(new_v_s)
    return (loss, grad_x, *[grads[n] for n in _WEIGHTS], *[delta[n] for n in _WEIGHTS],
            *[new_m[n] for n in _WEIGHTS], *[new_v[n] for n in _WEIGHTS])
```

```python
import math

import jax
import jax.numpy as jnp
from jax import lax
from jax.experimental import pallas as pl
from jax.experimental.pallas import tpu as pltpu

F32 = jnp.float32
BF16 = jnp.bfloat16
HI = lax.Precision.HIGH

D_MODEL = 1024
DEPTH = 4
D_FF = 2816
NORM_EPS = 1e-6
F_MIN = 1e-6
CHUNK = 64
HG_HEADS = 4
HEAD_DIM = 128
HG_QK = HG_HEADS * HEAD_DIM
S5_WIDTH = 512
S5_GROUP = 16
S5_GROUPS = 32
S5_STATE = 64
S5_NSTATE = S5_GROUPS * S5_STATE
EV_IN = 2560
GDN_HEADS = 8
GDN_QKV = 3 * GDN_HEADS * HEAD_DIM
CONV_W = 4
OD_IN = 4112
OD_IN_PAD = 4224
N_DEV = 8
LANES = 128
ADAM_LR, ADAM_B1, ADAM_B2, ADAM_EPS, ADAM_WD, ADAM_STEP = 0.001, 0.9, 0.999, 1e-08, 0.01, 10
VMEM_LIMIT = 56 * 1024 * 1024

MESH = pl.DeviceIdType.MESH


def _params(sem=None, **kw):
    return pltpu.CompilerParams(dimension_semantics=sem, vmem_limit_bytes=VMEM_LIMIT, **kw)


def _tile(n, cands):
    for c in cands:
        if n % c == 0:
            return c
    return n


def _dot(a, b):
    return jnp.dot(a.astype(BF16), b.astype(BF16), preferred_element_type=F32)


def _dot_nt(a, b):
    return lax.dot_general(a.astype(BF16), b.astype(BF16), (((1,), (1,)), ((), ())), preferred_element_type=F32)


def _dot_tn(a, b):
    return lax.dot_general(a.astype(BF16), b.astype(BF16), (((0,), (0,)), ((), ())), preferred_element_type=F32)


def _dot_hi(a, b):
    return jnp.dot(a, b, precision=HI, preferred_element_type=F32)


def _bmm(a, b):
    return jnp.einsum('gmk,gkn->gmn', a.astype(BF16), b.astype(BF16), preferred_element_type=F32)


def _bmm_nt(a, b):
    return jnp.einsum('gmk,gnk->gmn', a.astype(BF16), b.astype(BF16), preferred_element_type=F32)


def _bmm_tn(a, b):
    return jnp.einsum('gkm,gkn->gmn', a.astype(BF16), b.astype(BF16), preferred_element_type=F32)


def _bmm_hi(a, b):
    return jnp.einsum('gmk,gkn->gmn', a, b, precision=HI, preferred_element_type=F32)


_TN_CANDS = (1408, 1280, 1024, 512, 384, 256, 128)
_B_TILE_BYTES = 6 * 1024 * 1024


def _mm(a, b, *, name, nt=False, out_dtype=F32, res=None, scale=1.0):
    m, k = a.shape
    n = b.shape[0] if nt else b.shape[1]
    tm = _tile(m, (512, 256, 128))
    tn = _tile(n, [c for c in _TN_CANDS if c * k * b.dtype.itemsize <= _B_TILE_BYTES])

    def body(*refs):
        if res is None:
            a_ref, b_ref, o_ref = refs
        else:
            a_ref, b_ref, r_ref, o_ref = refs
        acc = (_dot_nt if nt else _dot)(a_ref[...], b_ref[...])
        if scale != 1.0:
            acc = scale * acc
        if res is not None:
            acc = r_ref[...] + acc
        o_ref[...] = acc.astype(out_dtype)

    b_spec = pl.BlockSpec((tn, k), lambda j, i: (j, 0)) if nt else pl.BlockSpec((k, tn), lambda j, i: (0, j))
    in_specs = [pl.BlockSpec((tm, k), lambda j, i: (i, 0)), b_spec]
    args = [a, b]
    if res is not None:
        in_specs.append(pl.BlockSpec((tm, tn), lambda j, i: (i, j)))
        args.append(res)
    return pl.pallas_call(
        body, name=name, grid=(n // tn, m // tm), in_specs=in_specs,
        out_specs=pl.BlockSpec((tm, tn), lambda j, i: (i, j)),
        out_shape=jax.ShapeDtypeStruct((m, n), out_dtype),
        compiler_params=_params(("parallel", "parallel")),
    )(*args)


def _mm_tn(a, b, *, name, scale=1.0, out_dtype=F32, into=None, slot=0):
    t, m = a.shape
    n = b.shape[1]
    tm = _tile(m, (1024, 512, 256, 128))
    tn = _tile(n, _TN_CANDS)
    tk = _tile(t, (2048, 1024, 512, 256, 128))
    nk = t // tk

    def body(*refs):
        a_ref, b_ref = refs[:2]
        o_ref, acc_ref = refs[-2:]
        kk = pl.program_id(2)

        @pl.when(kk == 0)
        def _():
            acc_ref[...] = jnp.zeros_like(acc_ref)

        acc_ref[...] += _dot_tn(a_ref[...], b_ref[...])

        @pl.when(kk == nk - 1)
        def _():
            o_ref[...] = (acc_ref[...] * scale if scale != 1.0 else acc_ref[...]).astype(o_ref.dtype)

    in_specs = [pl.BlockSpec((tk, tm), lambda i, j, kk: (kk, i)), pl.BlockSpec((tk, tn), lambda i, j, kk: (kk, j))]
    args = [a, b]
    if into is None:
        out_spec = pl.BlockSpec((tm, tn), lambda i, j, kk: (i, j))
        out_shape = jax.ShapeDtypeStruct((m, n), out_dtype)
        alias = {}
    else:
        in_specs.append(pl.BlockSpec(memory_space=pl.ANY))
        args.append(into)
        out_spec = pl.BlockSpec((None, tm, tn), lambda i, j, kk: (slot, i, j))
        out_shape = jax.ShapeDtypeStruct(into.shape, into.dtype)
        alias = {2: 0}
    return pl.pallas_call(
        body, name=name, grid=(m // tm, n // tn, nk), in_specs=in_specs, out_specs=out_spec, out_shape=out_shape,
        scratch_shapes=[pltpu.VMEM((tm, tn), F32)], input_output_aliases=alias,
        compiler_params=_params(("parallel", "parallel", "arbitrary")),
    )(*args)


def _sigmoid(x):
    return jax.nn.sigmoid(x)


def _head_norm_gate(o, gate, nw):
    r = lax.rsqrt(jnp.mean(o * o, axis=-1, keepdims=True) + NORM_EPS)
    return o * r * nw * (gate * _sigmoid(gate))


def _hg_chunk(st, ql, fl, iv, gl, lb, nw):
    n_g, c, _ = ql.shape
    q = ql * _sigmoid(ql)
    f = lb + (1.0 - lb) * _sigmoid(fl)
    lf = jnp.log(jnp.maximum(f, F_MIN))
    k = 1.0 - f
    ri = lax.broadcasted_iota(jnp.int32, (c, c), 0)
    ci = lax.broadcasted_iota(jnp.int32, (c, c), 1)
    rl = lax.broadcasted_iota(jnp.int32, (c, LANES), 0)
    each = lambda m2: jnp.broadcast_to(m2[None], (n_g,) + m2.shape)
    b = _bmm_hi(each(jnp.where(ci <= ri, 1.0, 0.0)), lf)
    attn = jnp.where((ri == ci)[None], _bmm_nt(q, k), 0.0)
    sh = 0
    while (1 << sh) < c:
        m = 1 << sh
        ref = ((ri >> (sh + 1)) << (sh + 1)) + (m - 1)
        low_r = ((ri >> sh) & 1) == 1
        low_c = ((ci >> sh) & 1) == 1
        w_low = low_r & (ci > ref) & (ci <= ri)
        w_up = jnp.logical_not(low_r) & (ci > ri) & (ci <= ref)
        w = jnp.where(w_low | w_up, 1.0, 0.0)
        e = jnp.exp(_bmm_hi(each(w), lf))
        low_l = (((rl >> sh) & 1) == 1)[None]
        qs = jnp.where(low_l, q * e, 0.0)
        ks = jnp.where(low_l, 0.0, k * e)
        pair = ((ri >> (sh + 1)) == (ci >> (sh + 1))) & low_r & jnp.logical_not(low_c)
        attn = attn + jnp.where(pair[None], _bmm_nt(qs, ks), 0.0)
        sh += 1
    bl = jnp.sum(lf, axis=1, keepdims=True)
    o = _bmm(attn, iv) + _bmm_nt(q * jnp.exp(b), st)
    st_new = st * jnp.exp(bl) + _bmm_tn(iv, k * jnp.exp(bl - b))
    return _head_norm_gate(o, gl, nw), st_new


def _hg_specs(nb, n_c, rev):
    def cidx(cc):
        return (n_c - 1 - cc) if rev else cc

    def col(width):
        return pl.BlockSpec((nb, CHUNK, width), lambda cc: (0, cidx(cc), 0))

    vec = pl.BlockSpec((HG_HEADS, 1, LANES), lambda cc: (0, 0, 0))
    nw = pl.BlockSpec((1, LANES), lambda cc: (0, 0))
    st = pl.BlockSpec((nb, HG_HEADS, None, LANES, LANES), lambda cc: (0, 0, cidx(cc), 0, 0))
    acc = pl.BlockSpec((nb, HG_HEADS, 1, LANES), lambda cc: (0, 0, 0, 0))
    return col, vec, nw, st, acc


def _head(j):
    return slice(j * LANES, (j + 1) * LANES)


def _seq_heads(ref, nb, n, first=0):
    return jnp.stack([ref[b, :, _head(first + j)] for b in range(nb) for j in range(n)])


def _put_seq_heads(ref, val, nb, n, first=0):
    for b in range(nb):
        for j in range(n):
            ref[b, :, _head(first + j)] = val[b * n + j].astype(ref.dtype)


def _hgrn2_fwd(proj, lb, nw):
    nb, l, _ = proj.shape
    n_c = l // CHUNK
    n_g = nb * HG_HEADS
    col, vec, nws, st, _ = _hg_specs(nb, n_c, False)

    def body(p_ref, lb_ref, nw_ref, y_ref, st_ref, s_scr):
        @pl.when(pl.program_id(0) == 0)
        def _():
            s_scr[...] = jnp.zeros_like(s_scr)

        s_in = s_scr[...]
        st_ref[...] = s_in.reshape(st_ref.shape)
        args = [_seq_heads(p_ref, nb, HG_HEADS, k * HG_HEADS) for k in range(4)]
        y, s_new = _hg_chunk(s_in, *args, jnp.concatenate([lb_ref[...]] * nb), nw_ref[...])
        _put_seq_heads(y_ref, y, nb, HG_HEADS)
        s_scr[...] = s_new

    return pl.pallas_call(
        body, name="hgrn2_fwd", grid=(n_c,),
        in_specs=[col(4 * HG_QK), vec, nws], out_specs=[col(HG_QK), st],
        out_shape=[jax.ShapeDtypeStruct((nb, l, HG_QK), BF16),
                   jax.ShapeDtypeStruct((nb, HG_HEADS, n_c, LANES, LANES), F32)],
        scratch_shapes=[pltpu.VMEM((n_g, LANES, LANES), F32)],
        compiler_params=_params(("arbitrary",)),
    )(proj, lb, nw)


def _hgrn2_bwd(proj, lb, nw, states, dy):
    nb, l, _ = proj.shape
    n_c = l // CHUNK
    n_g = nb * HG_HEADS
    col, vec, nws, st, acc = _hg_specs(nb, n_c, True)

    def body(p_ref, lb_ref, nw_ref, st_ref, dy_ref, dp_ref, dlb_ref, dnw_ref, ds_scr):
        @pl.when(pl.program_id(0) == 0)
        def _():
            ds_scr[...] = jnp.zeros_like(ds_scr)
            dlb_ref[...] = jnp.zeros_like(dlb_ref)
            dnw_ref[...] = jnp.zeros_like(dnw_ref)

        args = [_seq_heads(p_ref, nb, HG_HEADS, k * HG_HEADS) for k in range(4)]
        _, vjp = jax.vjp(_hg_chunk, st_ref[...].reshape(n_g, LANES, LANES), *args,
                         jnp.concatenate([lb_ref[...]] * nb), nw_ref[...])
        ds, dql, dfl, div, dgl, dlb, dnw = vjp((_seq_heads(dy_ref, nb, HG_HEADS).astype(F32), ds_scr[...]))
        ds_scr[...] = ds
        for k, val in enumerate((dql, dfl, div, dgl)):
            _put_seq_heads(dp_ref, val, nb, HG_HEADS, k * HG_HEADS)
        dlb_ref[...] += dlb.reshape(dlb_ref.shape)
        dnw_ref[...] += dnw

    return pl.pallas_call(
        body, name="hgrn2_bwd", grid=(n_c,),
        in_specs=[col(4 * HG_QK), vec, nws, st, col(HG_QK)], out_specs=[col(4 * HG_QK), acc, nws],
        out_shape=[jax.ShapeDtypeStruct((nb, l, 4 * HG_QK), BF16),
                   jax.ShapeDtypeStruct((nb, HG_HEADS, 1, LANES), F32), jax.ShapeDtypeStruct((1, LANES), F32)],
        scratch_shapes=[pltpu.VMEM((n_g, LANES, LANES), F32)],
        compiler_params=_params(("arbitrary",)),
    )(proj, lb, nw, states, dy)


def _shift_down(x, s):
    if s == 0:
        return x
    rows = lax.broadcasted_iota(jnp.int32, x.shape, 0)
    return jnp.where(rows >= s, pltpu.roll(x, s, 0), 0.0)


def _shift_up(x, s):
    if s == 0:
        return x
    n = x.shape[0]
    rows = lax.broadcasted_iota(jnp.int32, x.shape, 0)
    return jnp.where(rows < n - s, pltpu.roll(x, n - s, 0), 0.0)


def _gdn_pre_fwd(proj, conv_w):
    nb, l, _ = proj.shape
    n_blk = GDN_QKV // LANES

    def body(x_ref, w_ref, o_ref):
        j = pl.program_id(1)
        x = x_ref[...]
        w = w_ref[...]
        c = w[3:4] * x
        for t in range(CONV_W - 1):
            c = c + w[t:t + 1] * _shift_down(x, CONV_W - 1 - t)
        s = c * _sigmoid(c)
        r = lax.rsqrt(jnp.sum(s * s, axis=-1, keepdims=True) + NORM_EPS)
        scale = jnp.where(j < GDN_HEADS, HEAD_DIM ** -0.5, 1.0)
        o_ref[...] = jnp.where(j < 2 * GDN_HEADS, s * r * scale, s)

    return pl.pallas_call(
        body, name="gdn_pre_fwd", grid=(nb, n_blk),
        in_specs=[pl.BlockSpec((None, l, LANES), lambda b, j: (b, 0, j)), pl.BlockSpec((CONV_W, LANES), lambda b, j: (0, j))],
        out_specs=pl.BlockSpec((None, l, LANES), lambda b, j: (b, 0, j)),
        out_shape=jax.ShapeDtypeStruct((nb, l, GDN_QKV), F32),
        compiler_params=_params(("parallel", "parallel")),
    )(proj, conv_w)


def _gdn_pre_bwd(proj, conv_w, dout):
    nb, l, _ = proj.shape
    n_blk = GDN_QKV // LANES

    def body(x_ref, w_ref, d_ref, dx_ref, dw_ref):
        j = pl.program_id(1)
        x = x_ref[...]
        w = w_ref[...]
        xs = [_shift_down(x, CONV_W - 1 - t) for t in range(CONV_W)]
        c = w[0:1] * xs[0]
        for t in range(1, CONV_W):
            c = c + w[t:t + 1] * xs[t]
        sg = _sigmoid(c)
        s = c * sg
        d = d_ref[...]
        r = lax.rsqrt(jnp.sum(s * s, axis=-1, keepdims=True) + NORM_EPS)
        scale = jnp.where(j < GDN_HEADS, HEAD_DIM ** -0.5, 1.0)
        ds_norm = scale * r * (d - s * (r * r) * jnp.sum(d * s, axis=-1, keepdims=True))
        ds = jnp.where(j < 2 * GDN_HEADS, ds_norm, d)
        dc = ds * (sg * (1.0 + c * (1.0 - sg)))
        dx = w[3:4] * dc
        for t in range(CONV_W - 1):
            dx = dx + w[t:t + 1] * _shift_up(dc, CONV_W - 1 - t)
        dx_ref[...] = dx.astype(dx_ref.dtype)
        for t in range(CONV_W):
            dw_ref[t:t + 1, :] = jnp.sum(dc * xs[t], axis=0, keepdims=True)

    return pl.pallas_call(
        body, name="gdn_pre_bwd", grid=(nb, n_blk),
        in_specs=[pl.BlockSpec((None, l, LANES), lambda b, j: (b, 0, j)), pl.BlockSpec((CONV_W, LANES), lambda b, j: (0, j)),
                  pl.BlockSpec((None, l, LANES), lambda b, j: (b, 0, j))],
        out_specs=[pl.BlockSpec((None, l, LANES), lambda b, j: (b, 0, j)), pl.BlockSpec((None, CONV_W, LANES), lambda b, j: (b, 0, j))],
        out_shape=[jax.ShapeDtypeStruct((nb, l, GDN_QKV), BF16), jax.ShapeDtypeStruct((nb, CONV_W, GDN_QKV), F32)],
        compiler_params=_params(("parallel", "parallel")),
    )(proj, conv_w, dout)


def _gdn_chunk(st, q, k, v, gate, bl, al, alog, dtb, nw):
    n_g, c, _ = q.shape
    beta = _sigmoid(bl)
    x = al + dtb
    softplus = jnp.maximum(x, 0.0) + jnp.log(1.0 + jnp.exp(-jnp.abs(x)))
    la = -jnp.exp(alog) * softplus
    ri = lax.broadcasted_iota(jnp.int32, (c, c), 0)
    ci = lax.broadcasted_iota(jnp.int32, (c, c), 1)
    lower = (ci <= ri)[None]
    strict = (ci < ri)[None]
    ltri = jnp.broadcast_to(jnp.where(lower, 1.0, 0.0), (n_g, c, c))
    la_l = la + jnp.zeros((1, 1, LANES), F32)
    g = _bmm_hi(ltri, la_l)
    delta = _bmm_hi(ltri, jnp.where(strict, la + jnp.zeros((1, 1, c), F32), 0.0))
    gam = jnp.where(lower, jnp.exp(jnp.where(lower, delta, 0.0)), 0.0)
    kb = k * beta
    vb = v * beta
    m = jnp.where(strict, _bmm_nt(kb, k) * gam, 0.0)
    pw = -m
    t_inv = jnp.where((ri == ci)[None], 1.0, 0.0) + pw
    steps = int(math.log2(c)) - 1
    for _ in range(steps):
        pw = _bmm_hi(pw, pw)
        t_inv = t_inv + _bmm_hi(t_inv, pw)
    eg = jnp.exp(g)
    u = _bmm(t_inv, vb)
    w = _bmm(t_inv, kb * eg)
    attn = jnp.where(lower, _bmm_nt(q, k) * gam, 0.0)
    v_new = u - _bmm_nt(w, st)
    o = _bmm_nt(q * eg, st) + _bmm(attn, v_new)
    g_last = jnp.sum(la_l, axis=1, keepdims=True)
    st_new = st * jnp.exp(g_last) + _bmm_tn(v_new, k * jnp.exp(g_last - g))
    return _head_norm_gate(o, gate, nw), st_new


GDN_WIDE = GDN_HEADS * LANES
GDN_SMALL_BLOCK = (GDN_QKV + GDN_WIDE) // LANES


def _gdn_specs(nb, n_c, rev):
    def cidx(cc):
        return (n_c - 1 - cc) if rev else cc

    def col(width, blk=0):
        return pl.BlockSpec((nb, CHUNK, width), lambda cc: (0, cidx(cc), blk))

    scal = pl.BlockSpec((GDN_HEADS, 1, 1), lambda cc: (0, 0, 0))
    nw = pl.BlockSpec((1, LANES), lambda cc: (0, 0))
    st = pl.BlockSpec((nb, GDN_HEADS, None, LANES, LANES), lambda cc: (0, 0, cidx(cc), 0, 0))
    acc_s = pl.BlockSpec((nb, GDN_HEADS, 1, 1), lambda cc: (0, 0, 0, 0))
    return col, scal, nw, st, acc_s


def _tok_cols(xs, nb, first):
    return jnp.stack([xs[b][:, first + h:first + h + 1] for b in range(nb) for h in range(GDN_HEADS)])


def _gdn_inputs(qkv_ref, p_gate_ref, p_small_ref, alog_ref, dtb_ref, nw_ref, nb):
    q, k, v = (_seq_heads(qkv_ref, nb, GDN_HEADS, i * GDN_HEADS) for i in range(3))
    xs = p_small_ref[...]
    return (q, k, v, _seq_heads(p_gate_ref, nb, GDN_HEADS), _tok_cols(xs, nb, 0), _tok_cols(xs, nb, GDN_HEADS),
            jnp.concatenate([alog_ref[...]] * nb), jnp.concatenate([dtb_ref[...]] * nb), nw_ref[...])


def _gdn_fwd(qkv, proj, alog, dtb, nw):
    nb, l, _ = qkv.shape
    n_c = l // CHUNK
    n_g = nb * GDN_HEADS
    col, scal, nws, st, _ = _gdn_specs(nb, n_c, False)

    def body(qkv_ref, pg_ref, ps_ref, alog_ref, dtb_ref, nw_ref, y_ref, st_ref, s_scr):
        @pl.when(pl.program_id(0) == 0)
        def _():
            s_scr[...] = jnp.zeros_like(s_scr)

        s_in = s_scr[...]
        st_ref[...] = s_in.reshape(st_ref.shape)
        y, s_new = _gdn_chunk(s_in, *_gdn_inputs(qkv_ref, pg_ref, ps_ref, alog_ref, dtb_ref, nw_ref, nb))
        _put_seq_heads(y_ref, y, nb, GDN_HEADS)
        s_scr[...] = s_new

    return pl.pallas_call(
        body, name="gdn_fwd", grid=(n_c,),
        in_specs=[col(GDN_QKV), col(GDN_WIDE, GDN_QKV // GDN_WIDE), col(LANES, GDN_SMALL_BLOCK), scal, scal, nws],
        out_specs=[col(GDN_WIDE), st],
        out_shape=[jax.ShapeDtypeStruct((nb, l, GDN_WIDE), BF16),
                   jax.ShapeDtypeStruct((nb, GDN_HEADS, n_c, LANES, LANES), F32)],
        scratch_shapes=[pltpu.VMEM((n_g, LANES, LANES), F32)],
        compiler_params=_params(("arbitrary",)),
    )(qkv, proj, proj, alog, dtb, nw)


def _gdn_bwd(qkv, proj, alog, dtb, nw, states, dy):
    nb, l, _ = qkv.shape
    n_c = l // CHUNK
    n_g = nb * GDN_HEADS
    col, scal, nws, st, acc_s = _gdn_specs(nb, n_c, True)

    def body(qkv_ref, pg_ref, ps_ref, alog_ref, dtb_ref, nw_ref, st_ref, dy_ref,
             dqkv_ref, dg_ref, dsm_ref, dalog_ref, ddtb_ref, dnw_ref, ds_scr):
        @pl.when(pl.program_id(0) == 0)
        def _():
            ds_scr[...] = jnp.zeros_like(ds_scr)
            dalog_ref[...] = jnp.zeros_like(dalog_ref)
            ddtb_ref[...] = jnp.zeros_like(ddtb_ref)
            dnw_ref[...] = jnp.zeros_like(dnw_ref)

        _, vjp = jax.vjp(_gdn_chunk, st_ref[...].reshape(n_g, LANES, LANES),
                         *_gdn_inputs(qkv_ref, pg_ref, ps_ref, alog_ref, dtb_ref, nw_ref, nb))
        ds, dq, dk, dv, dg, dbl, dal, dalog, ddtb, dnw = vjp((_seq_heads(dy_ref, nb, GDN_HEADS).astype(F32), ds_scr[...]))
        ds_scr[...] = ds
        for i, val in enumerate((dq, dk, dv)):
            _put_seq_heads(dqkv_ref, val, nb, GDN_HEADS, i * GDN_HEADS)
        _put_seq_heads(dg_ref, dg, nb, GDN_HEADS)
        lane = lax.broadcasted_iota(jnp.int32, (CHUNK, LANES), 1)
        for b in range(nb):
            small = jnp.zeros((CHUNK, LANES), F32)
            for h in range(GDN_HEADS):
                small = small + jnp.where(lane == h, dbl[b * GDN_HEADS + h], 0.0)
                small = small + jnp.where(lane == GDN_HEADS + h, dal[b * GDN_HEADS + h], 0.0)
            dsm_ref[b] = small.astype(dsm_ref.dtype)
        dalog_ref[...] += dalog.reshape(dalog_ref.shape)
        ddtb_ref[...] += ddtb.reshape(ddtb_ref.shape)
        dnw_ref[...] += dnw

    h = GDN_HEADS
    return pl.pallas_call(
        body, name="gdn_bwd", grid=(n_c,),
        in_specs=[col(GDN_QKV), col(GDN_WIDE, GDN_QKV // GDN_WIDE), col(LANES, GDN_SMALL_BLOCK), scal, scal, nws, st,
                  col(GDN_WIDE)],
        out_specs=[col(GDN_QKV), col(GDN_WIDE), col(LANES), acc_s, acc_s, nws],
        out_shape=[jax.ShapeDtypeStruct((nb, l, GDN_QKV), F32), jax.ShapeDtypeStruct((nb, l, GDN_WIDE), BF16),
                   jax.ShapeDtypeStruct((nb, l, LANES), BF16),
                   jax.ShapeDtypeStruct((nb, h, 1, 1), F32), jax.ShapeDtypeStruct((nb, h, 1, 1), F32),
                   jax.ShapeDtypeStruct((1, LANES), F32)],
        scratch_shapes=[pltpu.VMEM((n_g, LANES, LANES), F32)],
        compiler_params=_params(("arbitrary",)),
    )(qkv, proj, proj, alog, dtb, nw, states, dy)


S5_ROWS = S5_NSTATE // LANES
S5_TB = 256


def _s5_scan_fwd(bu, abar):
    nb, l = bu.shape[:2]
    tb = min(S5_TB, l)

    def body(bu_ref, a_ref, h_ref, c_scr):
        @pl.when(pl.program_id(1) == 0)
        def _():
            c_scr[...] = jnp.zeros_like(c_scr)

        ar = a_ref[0]
        ai = a_ref[1]

        def step(t, carry):
            hr, hi = carry
            nr = ar * hr - ai * hi + bu_ref[t, 0].astype(F32)
            ni = ar * hi + ai * hr + bu_ref[t, 1].astype(F32)
            h_ref[t, 0] = nr.astype(h_ref.dtype)
            h_ref[t, 1] = ni.astype(h_ref.dtype)
            return nr, ni

        hr, hi = lax.fori_loop(0, tb, step, (c_scr[0], c_scr[1]), unroll=8)
        c_scr[0] = hr
        c_scr[1] = hi

    blk = pl.BlockSpec((None, tb, 2, S5_ROWS, LANES), lambda b, i: (b, i, 0, 0, 0))
    return pl.pallas_call(
        body, name="s5_scan_fwd", grid=(nb, l // tb),
        in_specs=[blk, pl.BlockSpec((2, S5_ROWS, LANES), lambda b, i: (0, 0, 0))],
        out_specs=blk, out_shape=jax.ShapeDtypeStruct(bu.shape, bu.dtype),
        scratch_shapes=[pltpu.VMEM((2, S5_ROWS, LANES), F32)],
        compiler_params=_params(("parallel", "arbitrary")),
    )(bu, abar)


def _s5_scan_bwd(dh, h, abar):
    nb, l = dh.shape[:2]
    tb = min(S5_TB, l)
    n_blk = l // tb

    def body(dh_ref, h_ref, a_ref, g_ref, da_ref, c_scr):
        @pl.when(pl.program_id(1) == 0)
        def _():
            c_scr[...] = jnp.zeros_like(c_scr)
            da_ref[...] = jnp.zeros_like(da_ref)

        ar = a_ref[0]
        ai = a_ref[1]

        def step(s, carry):
            gr, gi, dar, dai = carry
            t = tb - 1 - s
            hr = h_ref[t, 0].astype(F32)
            hi = h_ref[t, 1].astype(F32)
            dar = dar + gr * hr + gi * hi
            dai = dai + gi * hr - gr * hi
            nr = ar * gr + ai * gi + dh_ref[t, 0].astype(F32)
            ni = ar * gi - ai * gr + dh_ref[t, 1].astype(F32)
            g_ref[t, 0] = nr.astype(g_ref.dtype)
            g_ref[t, 1] = ni.astype(g_ref.dtype)
            return nr, ni, dar, dai

        z = jnp.zeros((S5_ROWS, LANES), F32)
        gr, gi, dar, dai = lax.fori_loop(0, tb, step, (c_scr[0], c_scr[1], z, z), unroll=8)
        c_scr[0] = gr
        c_scr[1] = gi
        da_ref[0] += dar
        da_ref[1] += dai

    blk = pl.BlockSpec((None, tb, 2, S5_ROWS, LANES), lambda b, i: (b, n_blk - 1 - i, 0, 0, 0))
    return pl.pallas_call(
        body, name="s5_scan_bwd", grid=(nb, n_blk),
        in_specs=[blk, blk, pl.BlockSpec((2, S5_ROWS, LANES), lambda b, i: (0, 0, 0))],
        out_specs=[blk, pl.BlockSpec((None, 2, S5_ROWS, LANES), lambda b, i: (b, 0, 0, 0))],
        out_shape=[jax.ShapeDtypeStruct(dh.shape, dh.dtype), jax.ShapeDtypeStruct((nb, 2, S5_ROWS, LANES), F32)],
        scratch_shapes=[pltpu.VMEM((2, S5_ROWS, LANES), F32)],
        compiler_params=_params(("parallel", "arbitrary")),
    )(dh, h, abar)


_GELU_C = math.sqrt(2.0 / math.pi)


def _gelu(x):
    return 0.5 * x * (1.0 + jnp.tanh(_GELU_C * (x + 0.044715 * x * x * x)))


def _gelu_grad(x):
    t = jnp.tanh(_GELU_C * (x + 0.044715 * x * x * x))
    return 0.5 * (1.0 + t) + 0.5 * x * (1.0 - t * t) * _GELU_C * (1.0 + 3.0 * 0.044715 * x * x)


def _s5_post_fwd(yc, u, d, w_glu):
    t, w = yc.shape
    tt = _tile(t, (512, 256, 128))

    def body(yc_ref, u_ref, d_ref, w_ref, o_ref, y0_ref):
        y0 = yc_ref[...] + d_ref[...] * u_ref[...]
        y = _gelu(y0)
        z = _dot(y, w_ref[...])
        o_ref[...] = (y * _sigmoid(z)).astype(o_ref.dtype)
        y0_ref[...] = y0

    tok = pl.BlockSpec((tt, w), lambda i: (i, 0))
    return pl.pallas_call(
        body, name="s5_post_fwd", grid=(t // tt,),
        in_specs=[tok, tok, pl.BlockSpec((1, w), lambda i: (0, 0)), pl.BlockSpec((w, w), lambda i: (0, 0))],
        out_specs=[tok, tok],
        out_shape=[jax.ShapeDtypeStruct((t, w), BF16), jax.ShapeDtypeStruct((t, w), F32)],
        compiler_params=_params(("parallel",)),
    )(yc, u, d, w_glu)


def _s5_post_bwd(y0, u, d, w_glu, dout):
    t, w = y0.shape
    tt = _tile(t, (512, 256, 128))

    def body(y0_ref, u_ref, d_ref, w_ref, do_ref, dy0_ref, y_ref, dz_ref, du_ref, dd_ref):
        @pl.when(pl.program_id(0) == 0)
        def _():
            dd_ref[...] = jnp.zeros_like(dd_ref)

        y0 = y0_ref[...]
        y = _gelu(y0)
        s = _sigmoid(_dot(y, w_ref[...]))
        do = do_ref[...]
        dz = do * y * s * (1.0 - s)
        dy = do * s + _dot_nt(dz, w_ref[...])
        dy0 = dy * _gelu_grad(y0)
        dy0_ref[...] = dy0.astype(dy0_ref.dtype)
        y_ref[...] = y.astype(y_ref.dtype)
        dz_ref[...] = dz.astype(dz_ref.dtype)
        du_ref[...] = dy0 * d_ref[...]
        dd_ref[...] += jnp.sum(dy0 * u_ref[...], axis=0, keepdims=True)

    tok = pl.BlockSpec((tt, w), lambda i: (i, 0))
    vec = pl.BlockSpec((1, w), lambda i: (0, 0))
    return pl.pallas_call(
        body, name="s5_post_bwd", grid=(t // tt,),
        in_specs=[tok, tok, vec, pl.BlockSpec((w, w), lambda i: (0, 0)), tok],
        out_specs=[tok, tok, tok, tok, vec],
        out_shape=[jax.ShapeDtypeStruct((t, w), BF16), jax.ShapeDtypeStruct((t, w), BF16),
                   jax.ShapeDtypeStruct((t, w), BF16), jax.ShapeDtypeStruct((t, w), F32),
                   jax.ShapeDtypeStruct((1, w), F32)],
        compiler_params=_params(("arbitrary",)),
    )(y0, u, d, w_glu, dout)


def _s5_params(a_re, a_im, b_re, b_im, c_re, c_im, log_dt):
    dt = jnp.exp(log_dt)[:, None]
    mag = jnp.exp(dt * a_re)
    ang = dt * a_im
    abar_re = mag * jnp.cos(ang)
    abar_im = mag * jnp.sin(ang)
    den = a_re * a_re + a_im * a_im
    zr = abar_re - 1.0
    zi = abar_im
    coef_re = ((zr * a_re + zi * a_im) / den)[..., None]
    coef_im = ((zi * a_re - zr * a_im) / den)[..., None]
    bb_re = coef_re * b_re - coef_im * b_im
    bb_im = coef_re * b_im + coef_im * b_re
    eye = jnp.eye(S5_GROUPS, dtype=F32)

    def dense_in(bb):
        return jnp.einsum('gnp,gh->gphn', bb, eye).reshape(S5_WIDTH, S5_NSTATE)

    def dense_out(cc):
        return jnp.einsum('gpn,gh->gnhp', cc, eye).reshape(S5_NSTATE, S5_WIDTH)

    abar = jnp.stack([abar_re.reshape(S5_ROWS, LANES), abar_im.reshape(S5_ROWS, LANES)])
    bbd = jnp.concatenate([dense_in(bb_re), dense_in(bb_im)], axis=1)
    ccd = jnp.concatenate([dense_out(c_re), dense_out(-c_im)], axis=0)
    return abar, bbd, ccd


def _tok_tile(t):
    return _tile(t, (256, 128))


def _rms_fwd(x, w):
    t, d = x.shape
    tt = _tok_tile(t)

    def body(x_ref, w_ref, o_ref):
        xv = x_ref[...]
        r = lax.rsqrt(jnp.mean(xv * xv, axis=-1, keepdims=True) + NORM_EPS)
        o_ref[...] = (xv * r * w_ref[...]).astype(o_ref.dtype)

    tok = pl.BlockSpec((tt, d), lambda i: (i, 0))
    return pl.pallas_call(
        body, name="rms_fwd", grid=(t // tt,), in_specs=[tok, pl.BlockSpec((1, d), lambda i: (0, 0))],
        out_specs=tok, out_shape=jax.ShapeDtypeStruct((t, d), BF16), compiler_params=_params(("parallel",)),
    )(x, w)


def _rms_bwd_math(xv, wv, dy):
    r = lax.rsqrt(jnp.mean(xv * xv, axis=-1, keepdims=True) + NORM_EPS)
    xh = xv * r
    dxh = dy * wv
    dx = r * (dxh - xh * jnp.mean(dxh * xh, axis=-1, keepdims=True))
    return dx, jnp.sum(dy * xh, axis=0, keepdims=True)


def _rms_bwd(x, w, dy, dres):
    t, d = x.shape
    tt = _tok_tile(t)

    def body(x_ref, w_ref, dy_ref, dr_ref, dx_ref, dw_ref):
        @pl.when(pl.program_id(0) == 0)
        def _():
            dw_ref[...] = jnp.zeros_like(dw_ref)

        dx, dw = _rms_bwd_math(x_ref[...], w_ref[...], dy_ref[...])
        dx_ref[...] = dr_ref[...] + dx
        dw_ref[...] += dw

    tok = pl.BlockSpec((tt, d), lambda i: (i, 0))
    vec = pl.BlockSpec((1, d), lambda i: (0, 0))
    return pl.pallas_call(
        body, name="rms_bwd", grid=(t // tt,), in_specs=[tok, vec, tok, tok], out_specs=[tok, vec],
        out_shape=[jax.ShapeDtypeStruct((t, d), F32), jax.ShapeDtypeStruct((1, d), F32)],
        compiler_params=_params(("arbitrary",)),
    )(x, w, dy, dres)


def _loss_head(x, w, target):
    t, d = x.shape
    tt = _tok_tile(t)

    def body(x_ref, w_ref, t_ref, l_ref, dx_ref, dw_ref):
        @pl.when(pl.program_id(0) == 0)
        def _():
            l_ref[...] = jnp.zeros_like(l_ref)
            dw_ref[...] = jnp.zeros_like(dw_ref)

        xv = x_ref[...]
        wv = w_ref[...]
        r = lax.rsqrt(jnp.mean(xv * xv, axis=-1, keepdims=True) + NORM_EPS)
        err = xv * r * wv - t_ref[...]
        row = jnp.sum(err * err, axis=-1, keepdims=True)
        l_ref[...] += (0.5 / d) * jnp.sum(row, axis=0, keepdims=True)
        dx, dw = _rms_bwd_math(xv, wv, err * (1.0 / d))
        dx_ref[...] = dx
        dw_ref[...] += dw

    tok = pl.BlockSpec((tt, d), lambda i: (i, 0))
    vec = pl.BlockSpec((1, d), lambda i: (0, 0))
    return pl.pallas_call(
        body, name="loss_head", grid=(t // tt,), in_specs=[tok, vec, tok],
        out_specs=[pl.BlockSpec((1, 1), lambda i: (0, 0)), tok, vec],
        out_shape=[jax.ShapeDtypeStruct((1, 1), F32), jax.ShapeDtypeStruct((t, d), F32), jax.ShapeDtypeStruct((1, d), F32)],
        compiler_params=_params(("arbitrary",)),
    )(x, w, target)


FF_PAD = 3072
FF_SHARD = D_FF // N_DEV
FF_SHARD_PAD = FF_PAD // N_DEV


def _ffn_in(x, nw, wg_t, wu_t):
    t, d = x.shape
    tm = _tile(t, (512, 256, 128))
    tn = 1024

    def body(x_ref, nw_ref, wg_ref, wu_ref, h_ref, g_ref, u_ref, a_ref):
        @pl.when(pl.program_id(1) == 0)
        def _():
            xv = x_ref[...]
            r = lax.rsqrt(jnp.mean(xv * xv, axis=-1, keepdims=True) + NORM_EPS)
            h_ref[...] = (xv * r * nw_ref[...]).astype(h_ref.dtype)

        h = h_ref[...]
        g = _dot_nt(h, wg_ref[...])
        u = _dot_nt(h, wu_ref[...])
        g_ref[...] = g.astype(g_ref.dtype)
        u_ref[...] = u.astype(u_ref.dtype)
        a_ref[...] = (g * _sigmoid(g) * u).astype(a_ref.dtype)

    tok = pl.BlockSpec((tm, d), lambda i, j: (i, 0))
    wsp = pl.BlockSpec((tn, d), lambda i, j: (j, 0))
    wide = pl.BlockSpec((tm, tn), lambda i, j: (i, j))
    return pl.pallas_call(
        body, name="ffn_in", grid=(t // tm, FF_PAD // tn),
        in_specs=[tok, pl.BlockSpec((1, d), lambda i, j: (0, 0)), wsp, wsp], out_specs=[tok, wide, wide, wide],
        out_shape=[jax.ShapeDtypeStruct((t, d), BF16)] + [jax.ShapeDtypeStruct((t, FF_PAD), BF16)] * 3,
        compiler_params=_params(("parallel", "arbitrary")),
    )(x, nw, wg_t, wu_t)


def _ffn_out_dx(dxo, wd, g, u):
    t, d = dxo.shape
    tm = _tile(t, (512, 256, 128))
    tn = 1024

    def body(dx_ref, wd_ref, g_ref, u_ref, dg_ref, du_ref):
        da = 0.5 * _dot_nt(dx_ref[...], wd_ref[...])
        g = g_ref[...].astype(F32)
        s = _sigmoid(g)
        dg_ref[...] = (da * u_ref[...].astype(F32) * (s * (1.0 + g * (1.0 - s)))).astype(dg_ref.dtype)
        du_ref[...] = (da * g * s).astype(du_ref.dtype)

    wide = pl.BlockSpec((tm, tn), lambda j, i: (i, j))
    return pl.pallas_call(
        body, name="ffn_out_dx", grid=(FF_PAD // tn, t // tm),
        in_specs=[pl.BlockSpec((tm, d), lambda j, i: (i, 0)), pl.BlockSpec((tn, d), lambda j, i: (j, 0)), wide, wide],
        out_specs=[wide, wide], out_shape=[jax.ShapeDtypeStruct((t, FF_PAD), BF16)] * 2,
        compiler_params=_params(("parallel", "parallel")),
    )(dxo, wd, g, u)


def _ffn_in_dx(dg, du, wg_t, wu_t, x, nw, dxo):
    t, d = x.shape
    tm = _tile(t, (256, 128))

    def body(dg_ref, du_ref, wg_ref, wu_ref, x_ref, nw_ref, dr_ref, dx_ref, dw_ref):
        @pl.when(pl.program_id(0) == 0)
        def _():
            dw_ref[...] = jnp.zeros_like(dw_ref)

        dh = _dot(dg_ref[...], wg_ref[...]) + _dot(du_ref[...], wu_ref[...])
        dx, dw = _rms_bwd_math(x_ref[...], nw_ref[...], dh)
        dx_ref[...] = dr_ref[...] + dx
        dw_ref[...] += dw

    wide = pl.BlockSpec((tm, FF_PAD), lambda i: (i, 0))
    wsp = pl.BlockSpec((FF_PAD, d), lambda i: (0, 0))
    tok = pl.BlockSpec((tm, d), lambda i: (i, 0))
    vec = pl.BlockSpec((1, d), lambda i: (0, 0))
    return pl.pallas_call(
        body, name="ffn_in_dx", grid=(t // tm,), in_specs=[wide, wide, wsp, wsp, tok, vec, tok], out_specs=[tok, vec],
        out_shape=[jax.ShapeDtypeStruct((t, d), F32), jax.ShapeDtypeStruct((1, d), F32)],
        compiler_params=_params(("arbitrary",)),
    )(dg, du, wg_t, wu_t, x, nw, dxo)


def _adamw(w, g, m, v):
    r, c = w.shape
    tr = _tile(r, (512, 256, 128, 64, 32, 16, 8))
    c1 = 1.0 - ADAM_B1 ** ADAM_STEP
    c2 = 1.0 - ADAM_B2 ** ADAM_STEP

    def body(w_ref, g_ref, m_ref, v_ref, d_ref, nm_ref, nv_ref):
        gv = g_ref[...]
        nm = ADAM_B1 * m_ref[...] + (1.0 - ADAM_B1) * gv
        nv = ADAM_B2 * v_ref[...] + (1.0 - ADAM_B2) * (gv * gv)
        d_ref[...] = -ADAM_LR * ((nm / c1) / (jnp.sqrt(nv / c2) + ADAM_EPS) + ADAM_WD * w_ref[...])
        nm_ref[...] = nm
        nv_ref[...] = nv

    blk = pl.BlockSpec((tr, c), lambda i: (i, 0))
    return pl.pallas_call(
        body, name="adamw", grid=(r // tr,), in_specs=[blk] * 4, out_specs=[blk] * 3,
        out_shape=[jax.ShapeDtypeStruct((r, c), F32)] * 3, compiler_params=_params(("parallel",)),
    )(w, g, m, v)


N_KINDS = 3


def _ffn_slot(which, kind):
    return which * N_KINDS + kind


def _ffn_fwd(x, nw, w_ffn, which):
    wg, wu, wd = (w_ffn[_ffn_slot(which, kind)] for kind in range(3))
    h, g, u, a = _ffn_in(x, nw, wg, wu)
    return _mm(a, wd, name="ffn_out", res=x, scale=0.5), (x, h, g, u, a)


def _ffn_bwd(dxo, res, nw, w_ffn, which):
    x, h, g, u, a = res
    wg, wu, wd = (w_ffn[_ffn_slot(which, kind)] for kind in range(3))
    dg, du = _ffn_out_dx(dxo, wd, g, u)
    g_ffn = lax.empty((N_KINDS,) + wd.shape, BF16)
    g_ffn = _mm_tn(a, dxo, name="ffn_out_dw", scale=0.5, into=g_ffn, slot=2)
    g_ffn = _mm_tn(dg, h, name="ffn_in_dw", into=g_ffn, slot=0)
    g_ffn = _mm_tn(du, h, name="ffn_in_dw", into=g_ffn, slot=1)
    dx, dnw = _ffn_in_dx(dg, du, wg, wu, x, nw, dxo)
    return dx, dnw, g_ffn


def _hg_lower_bounds(logits):
    p = jax.nn.softmax(logits, axis=0)
    return jnp.cumsum(p, axis=0) - p[0]


def _even_fwd(x1, p, nb, l):
    t = nb * l
    h = _rms_fwd(x1, p["mix_norm"])
    proj = _mm(h, p["w_in_t"], name="ev_in", nt=True)
    proj3 = proj.reshape(nb, l, EV_IN)
    ya, hst = _hgrn2_fwd(proj3, p["lb"], p["hg_nw"])
    u = proj[:, 4 * HG_QK:]
    bu = _mm(u, p["bbd"], name="s5_in", out_dtype=BF16)
    hs = _s5_scan_fwd(bu.reshape(nb, l, 2, S5_ROWS, LANES), p["abar"])
    yc = _mm(hs.reshape(t, 2 * S5_NSTATE), p["ccd"], name="s5_out")
    yb, y0 = _s5_post_fwd(yc, u, p["s5_d"], p["w_glu"])
    ycat = jnp.concatenate([ya.reshape(t, HG_QK), yb], axis=1)
    x2 = _mm(ycat, p["w_out"], name="mix_out", res=x1)
    return x2, (x1, h, proj3, hst, u, hs, y0, ycat)


def _even_bwd(dx2, res, p, nb, l):
    x1, h, proj3, hst, u, hs, y0, ycat = res
    t = nb * l
    dycat = _mm(dx2, p["w_out"], name="mix_out_dx", nt=True)
    dw_out = _mm_tn(ycat, dx2, name="mix_out_dw", out_dtype=BF16)
    dp_hg, dlb, dhg_nw = _hgrn2_bwd(proj3, p["lb"], p["hg_nw"], hst, dycat.reshape(nb, l, D_MODEL))
    dy0, y, dz, du_d, dd = _s5_post_bwd(y0, u, p["s5_d"], p["w_glu"], dycat[:, HG_QK:])
    dw_glu = _mm_tn(y, dz, name="s5_glu_dw", out_dtype=BF16)
    dhs = _mm(dy0, p["ccd"], name="s5_out_dx", nt=True, out_dtype=BF16)
    dccd = _mm_tn(hs.reshape(t, 2 * S5_NSTATE), dy0, name="s5_out_dw")
    g5, dabar = _s5_scan_bwd(dhs.reshape(nb, l, 2, S5_ROWS, LANES), hs, p["abar"])
    g2 = g5.reshape(t, 2 * S5_NSTATE)
    du = _mm(g2, p["bbd"], name="s5_in_dx", nt=True, res=du_d, out_dtype=BF16)
    dbbd = _mm_tn(u, g2, name="s5_in_dw")
    dproj = jnp.concatenate([dp_hg.reshape(t, 4 * HG_QK), du], axis=1)
    dw_in_t = _mm_tn(dproj, h, name="ev_in_dw", out_dtype=BF16)
    dh = _mm(dproj, p["w_in_t"], name="ev_in_dx")
    dx1, dmix = _rms_bwd(x1, p["mix_norm"], dh, dx2)
    small = dict(mix_norm=dmix, lb=dlb.sum(0).reshape(HG_QK), hg_nw=dhg_nw.reshape(HEAD_DIM),
                 abar=dabar.sum(0), bbd=dbbd, ccd=dccd, s5_d=dd.reshape(S5_WIDTH))
    return dx1, dict(w_in_t=dw_in_t, w_glu=dw_glu, w_out=dw_out), small


def _odd_fwd(x1, p, nb, l):
    t = nb * l
    h = _rms_fwd(x1, p["mix_norm"])
    proj3 = _mm(h, p["w_in"], name="od_in").reshape(nb, l, OD_IN_PAD)
    qkv = _gdn_pre_fwd(proj3, p["conv_w"])
    y, st = _gdn_fwd(qkv, proj3, p["a_log"], p["dt_bias"], p["gdn_nw"])
    y2 = y.reshape(t, D_MODEL)
    x2 = _mm(y2, p["w_out"], name="mix_out", res=x1)
    return x2, (x1, h, proj3, qkv, st, y2)


def _odd_bwd(dx2, res, p, nb, l):
    x1, h, proj3, qkv, st, y2 = res
    t = nb * l
    dy = _mm(dx2, p["w_out"], name="mix_out_dx", nt=True)
    dw_out = _mm_tn(y2, dx2, name="mix_out_dw", out_dtype=BF16)
    dqkv, dgate, dsmall, dalog, ddtb, dnw = _gdn_bwd(
        qkv, proj3, p["a_log"], p["dt_bias"], p["gdn_nw"], st, dy.reshape(nb, l, D_MODEL))
    dx_qkv, dcw = _gdn_pre_bwd(proj3, p["conv_w"], dqkv)
    dproj = jnp.concatenate([dx_qkv, dgate, dsmall], axis=-1).reshape(t, OD_IN_PAD)
    dw_in = _mm_tn(h, dproj, name="od_in_dw", out_dtype=BF16)[:, :OD_IN]
    dh = _mm(dproj, p["w_in"], name="od_in_dx", nt=True)
    dx1, dmix = _rms_bwd(x1, p["mix_norm"], dh, dx2)
    small = dict(mix_norm=dmix, a_log=dalog.sum(0).reshape(GDN_HEADS), dt_bias=ddtb.sum(0).reshape(GDN_HEADS),
                 gdn_nw=dnw.reshape(HEAD_DIM))
    return dx1, dict(w_in=dw_in, conv_w=dcw.sum(0), w_out=dw_out), small


def _local_step(x, target, layer_weights, small, layer_done):
    nb, l, _ = x.shape
    t = nb * l
    lbs, lbs_vjp = jax.vjp(_hg_lower_bounds, small["hg_lb_logits"])
    row = lambda v: v.reshape(1, -1)
    s5_vjps = {}

    def layer_params(layer, big):
        j = layer // 2
        p = dict(mix_norm=row(small["mix_norm"][layer]), w_out=big["w_out"])
        if layer % 2 == 0:
            (abar, bbd, ccd), s5_vjps[j] = jax.vjp(
                _s5_params, small["s5_a_re"][j], small["s5_a_im"][j], small["s5_b_re"][j], small["s5_b_im"][j],
                small["s5_c_re"][j], small["s5_c_im"][j], small["s5_log_dt"][j])
            p.update(w_in_t=big["w_in_t"], w_glu=big["w_glu"],
                     lb=lbs[j].reshape(HG_HEADS, 1, HEAD_DIM), hg_nw=row(small["hg_norm_w"][j]),
                     abar=abar, bbd=bbd.astype(BF16), ccd=ccd.astype(BF16), s5_d=row(small["s5_d"][j]))
        else:
            p.update(w_in=big["w_in"], conv_w=big["conv_w"],
                     a_log=small["gdn_a_log"][j].reshape(GDN_HEADS, 1, 1),
                     dt_bias=small["gdn_dt_bias"][j].reshape(GDN_HEADS, 1, 1), gdn_nw=row(small["gdn_norm_w"][j]))
        return p

    xs = x.reshape(t, D_MODEL)
    saved, layers, ffn_w = [], [], []
    for layer in range(DEPTH):
        big = layer_weights(layer, xs)
        p = layer_params(layer, big)
        xs, r1 = _ffn_fwd(xs, row(small["ffn1_norm"][layer]), big["ffn"], 0)
        xs, r2 = (_even_fwd if layer % 2 == 0 else _odd_fwd)(xs, p, nb, l)
        xs, r3 = _ffn_fwd(xs, row(small["ffn2_norm"][layer]), big["ffn"], 1)
        saved.append((r1, r2, r3))
        layers.append(p)
        ffn_w.append(big["ffn"])
    loss, dx, dfinal = _loss_head(xs, row(small["final_norm"]), target.reshape(t, D_MODEL))

    gs = {k: [None] * DEPTH for k in ("ffn1_norm", "mix_norm", "ffn2_norm")}
    gs.update({k: [None] * 2 for k in ("hg_norm_w", "s5_a_re", "s5_a_im", "s5_b_re", "s5_b_im", "s5_c_re", "s5_c_im",
                                       "s5_d", "s5_log_dt", "gdn_a_log", "gdn_dt_bias", "gdn_norm_w")})
    dlbs = [None] * 2
    for layer in reversed(range(DEPTH)):
        j = layer // 2
        p = layers[layer]
        r1, r2, r3 = saved[layer]
        dx, dnw, g_ffn = _ffn_bwd(dx, r3, row(small["ffn2_norm"][layer]), ffn_w[layer], 1)
        gs["ffn2_norm"][layer] = dnw[0]
        if layer % 2 == 0:
            dx, gw, sm = _even_bwd(dx, r2, p, nb, l)
            dlbs[j] = sm["lb"]
            gs["hg_norm_w"][j] = sm["hg_nw"]
            gs["s5_d"][j] = sm["s5_d"]
            (gs["s5_a_re"][j], gs["s5_a_im"][j], gs["s5_b_re"][j], gs["s5_b_im"][j], gs["s5_c_re"][j],
             gs["s5_c_im"][j], gs["s5_log_dt"][j]) = s5_vjps[j]((sm["abar"], sm["bbd"], sm["ccd"]))
        else:
            dx, gw, sm = _odd_bwd(dx, r2, p, nb, l)
            gs["gdn_a_log"][j], gs["gdn_dt_bias"][j], gs["gdn_norm_w"][j] = sm["a_log"], sm["dt_bias"], sm["gdn_nw"]
        gs["mix_norm"][layer] = sm["mix_norm"][0]
        layer_done(layer, 0, dict(gw, ffn=g_ffn))
        dx, dnw, g_ffn = _ffn_bwd(dx, r1, row(small["ffn1_norm"][layer]), ffn_w[layer], 0)
        gs["ffn1_norm"][layer] = dnw[0]
        layer_done(layer, 1, dict(ffn=g_ffn))
    gsmall = {k: jnp.stack(v) for k, v in gs.items()}
    gsmall["hg_lb_logits"] = lbs_vjp(jnp.stack(dlbs))[0]
    gsmall["final_norm"] = dfinal[0]
    return loss[0, 0], dx.reshape(nb, l, D_MODEL), gsmall


def _here():
    return lax.axis_index("x"), lax.axis_index("y"), lax.axis_index("c")


def _other_chips(x, y):
    return [(1 - x, y), (x, 1 - y), (1 - x, 1 - y)]


def _all_gather(blocks, name):
    n = len(blocks)

    def body(*refs):
        x_refs, o_refs = refs[:n], refs[n:2 * n]
        send_sems, recv_sems, local_sems = refs[2 * n:]
        x, y, cc = _here()
        me, sibling = (x, y, cc), (x, y, 1 - cc)
        chips = _other_chips(x, y)

        def win(i, p):
            return o_refs[i].at[:, 4 * p[0] + 2 * p[1] + p[2]]

        def copy(i, k, blk, to, src=None):
            return pltpu.make_async_remote_copy(
                src_ref=win(i, blk) if src is None else src, dst_ref=win(i, blk),
                send_sem=send_sems.at[i, k], recv_sem=recv_sems.at[i, k], device_id=to, device_id_type=MESH)

        mine = [pltpu.make_async_copy(x_refs[i], win(i, me), local_sems.at[i]) for i in range(n)]
        for cp in mine:
            cp.start()
        first = []
        for i in range(n):
            first.append(copy(i, 0, me, sibling, src=x_refs[i]))
            first += [copy(i, 1 + j, me, (*chip, cc), src=x_refs[i]) for j, chip in enumerate(chips)]
        for cp in first:
            cp.start()
        passed = []
        for j, chip in enumerate(chips):
            for i in range(n):
                copy(i, 1 + j, (*chip, cc), me).wait_recv()
                fwd = copy(i, 4 + j, (*chip, cc), sibling)
                fwd.start()
                passed.append(fwd)
        for i in range(n):
            copy(i, 0, sibling, me).wait_recv()
            for j, chip in enumerate(chips):
                copy(i, 4 + j, (*chip, 1 - cc), me).wait_recv()
        for cp in first + passed:
            cp.wait_send()
        for cp in mine:
            cp.wait()

    hbm = pl.BlockSpec(memory_space=pl.ANY)
    return pl.pallas_call(
        body, name=name, in_specs=[hbm] * n, out_specs=[hbm] * n,
        out_shape=[jax.ShapeDtypeStruct((b.shape[0], N_DEV) + b.shape[1:], b.dtype) for b in blocks],
        scratch_shapes=[pltpu.SemaphoreType.DMA((n, 7)), pltpu.SemaphoreType.DMA((n, 7)), pltpu.SemaphoreType.DMA((n,))],
    )(*blocks)


_HBM = pl.BlockSpec(memory_space=pltpu.HBM)
_SEM = pl.BlockSpec(memory_space=pltpu.SEMAPHORE)
_DATAFLOW = pltpu.SideEffectType.DATAFLOW_SIDE_EFFECTING


def _peer(k):
    x, y, c = _here()
    return (x ^ ((k >> 2) & 1), y ^ ((k >> 1) & 1), c ^ (k & 1))


def _device_of(p):
    return 4 * p[0] + 2 * p[1] + p[2]


_OTHER_CHIPS = (4, 2, 6)


def _exchange_start(srcs, lands, copies, name, after=None):
    n = len(srcs)
    n_in = 2 * n + (after is not None)

    def body(*refs):
        s_refs, l_refs, sems = refs[:n], refs[n:2 * n], refs[n_in:n_in + 2 * n]
        for i in range(n):
            for slot, k in enumerate(_OTHER_CHIPS):
                src, dst = copies(s_refs[i], l_refs[i], k, slot)
                pltpu.make_async_remote_copy(
                    src_ref=src, dst_ref=dst, send_sem=sems[2 * i], recv_sem=sems[2 * i + 1],
                    device_id=_peer(k), device_id_type=MESH).start()

    both = list(srcs) + list(lands)
    outs = pl.pallas_call(
        body, name=name,
        out_shape=tuple([pltpu.SemaphoreType.DMA(())] * (2 * n)) + tuple(pltpu.HBM(a.shape, a.dtype) for a in both),
        in_specs=[_HBM] * (2 * n) + [pl.BlockSpec(memory_space=pl.ANY)] * (after is not None),
        out_specs=tuple([_SEM] * (2 * n)) + tuple([_HBM] * (2 * n)),
        input_output_aliases={i: 2 * n + i for i in range(2 * n)},
        compiler_params=pltpu.CompilerParams(has_side_effects=_DATAFLOW),
    )(*[pltpu.with_memory_space_constraint(a, pltpu.HBM) for a in both], *([after] if after is not None else []))
    return list(outs[:2 * n]), list(outs[2 * n:3 * n]), list(outs[3 * n:])


def _exchange_wait(handle, three, after, name):
    sems, srcs, lands = handle
    n = len(srcs)

    def body(*refs):
        l_refs, sem_refs = refs[n:2 * n], refs[2 * n:4 * n]
        for i in range(n):
            all_three = three(l_refs[i])
            cp = pltpu.make_async_remote_copy(
                src_ref=all_three, dst_ref=all_three, send_sem=sem_refs[2 * i], recv_sem=sem_refs[2 * i + 1],
                device_id=_peer(_OTHER_CHIPS[0]), device_id_type=MESH)
            cp.wait_send()
            cp.wait_recv()

    both = list(srcs) + list(lands)
    outs = pl.pallas_call(
        body, name=name, out_shape=tuple(pltpu.HBM(a.shape, a.dtype) for a in both),
        in_specs=[_HBM] * (2 * n) + [_SEM] * (2 * n) + [pl.BlockSpec(memory_space=pl.ANY)],
        out_specs=tuple([_HBM] * (2 * n)), input_output_aliases={i: i for i in range(2 * n)},
        compiler_params=pltpu.CompilerParams(has_side_effects=_DATAFLOW),
    )(*both, *sems, after)
    return list(outs[:n]), list(outs[n:])


def _gather_copies(src, land, k, slot):
    del k, slot
    return src, land.at[:, _device_of(_here())]


def _gather_three(land):
    return land.at[:, pl.ds(0, len(_OTHER_CHIPS))]


def _scatter_copies(src, land, k, slot):
    p = _peer(k)
    return src.at[2 * p[0] + p[1]], land.at[slot]


def _scatter_three(land):
    return land


def _core_forward(lands):
    n = len(lands)

    def body(*refs):
        l_refs = refs[:n]
        send_sems, recv_sems = refs[2 * n:]
        x, y, cc = _here()
        mine = [(x, y)] + [_peer(k)[:2] for k in _OTHER_CHIPS]
        cps = []
        for i in range(n):
            for j, (px, py) in enumerate(mine):
                give = l_refs[i].at[:, _device_of((px, py, cc))]
                take = l_refs[i].at[:, _device_of((px, py, 1 - cc))]
                cps.append((pltpu.make_async_remote_copy(
                    src_ref=give, dst_ref=give, send_sem=send_sems.at[i, j], recv_sem=recv_sems.at[i, j],
                    device_id=(x, y, 1 - cc), device_id_type=MESH), take))
        for cp, _ in cps:
            cp.start()
        for i in range(n):
            for j, (cp, take) in enumerate(cps[4 * i:4 * i + 4]):
                cp.wait_send()
                pltpu.make_async_remote_copy(
                    src_ref=take, dst_ref=take, send_sem=send_sems.at[i, j], recv_sem=recv_sems.at[i, j],
                    device_id=(x, y, 1 - cc), device_id_type=MESH).wait_recv()

    hbm = pl.BlockSpec(memory_space=pl.ANY)
    return pl.pallas_call(
        body, name="gather_core_forward", in_specs=[hbm] * n, out_specs=[hbm] * n,
        out_shape=[jax.ShapeDtypeStruct(a.shape, a.dtype) for a in lands],
        input_output_aliases={i: i for i in range(n)},
        scratch_shapes=[pltpu.SemaphoreType.DMA((n, 4)), pltpu.SemaphoreType.DMA((n, 4))],
    )(*lands)


def _rs_core_exchange(gs):
    n = len(gs)

    def body(*refs):
        g_refs, o_refs = refs[:n], refs[n:2 * n]
        send_sems, recv_sems = refs[2 * n:]
        x, y, cc = _here()
        cps = [pltpu.make_async_remote_copy(
            src_ref=g_refs[i].at[:, 2 * j + (1 - cc)], dst_ref=o_refs[i].at[j],
            send_sem=send_sems.at[i, j], recv_sem=recv_sems.at[i, j], device_id=(x, y, 1 - cc), device_id_type=MESH)
            for i in range(n) for j in range(4)]
        for cp in cps:
            cp.start()
        for cp in cps:
            cp.wait()

    hbm = pl.BlockSpec(memory_space=pl.ANY)
    return pl.pallas_call(
        body, name="rs_core_exchange", in_specs=[hbm] * n, out_specs=[hbm] * n,
        out_shape=[jax.ShapeDtypeStruct((4, g.shape[0]) + g.shape[2:], g.dtype) for g in gs],
        scratch_shapes=[pltpu.SemaphoreType.DMA((n, 4)), pltpu.SemaphoreType.DMA((n, 4))],
    )(*gs)


def _row_tile(w):
    return w if w <= 512 else _tile(w, (512, 256, 128))


def _rs_chip_sum(g, r1):
    a, _, w, c = g.shape
    tr = _row_tile(w)

    def body(g_ref, r_ref, o_ref):
        o_ref[...] = (g_ref[...].astype(F32) + r_ref[...].astype(F32)).astype(o_ref.dtype)

    blk = pl.BlockSpec((None, None, tr, c), lambda j, aa, i: (j, aa, i, 0))
    return pl.pallas_call(
        body, name="rs_chip_sum", grid=(4, a, w // tr),
        in_specs=[pl.BlockSpec((None, None, tr, c), lambda j, aa, i: (aa, 2 * j + lax.axis_index("c"), i, 0)), blk],
        out_specs=blk, out_shape=jax.ShapeDtypeStruct((4, a, w, c), g.dtype),
        compiler_params=_params(("parallel", "parallel", "parallel")),
    )(g, r1)


def _rs_final_sum(g, r1, r2):
    a, _, w, c = g.shape
    tr = _row_tile(w)

    def body(g_ref, r1_ref, r2_ref, o_ref):
        s = g_ref[...].astype(F32) + r1_ref[...].astype(F32)
        for k in range(3):
            s = s + r2_ref[k].astype(F32)
        o_ref[...] = s

    def device():
        return 4 * lax.axis_index("x") + 2 * lax.axis_index("y") + lax.axis_index("c")

    def chip():
        return 2 * lax.axis_index("x") + lax.axis_index("y")

    return pl.pallas_call(
        body, name="rs_final_sum", grid=(a, w // tr),
        in_specs=[pl.BlockSpec((None, None, tr, c), lambda aa, i: (aa, device(), i, 0)),
                  pl.BlockSpec((None, None, tr, c), lambda aa, i: (chip(), aa, i, 0)),
                  pl.BlockSpec((3, None, tr, c), lambda aa, i: (0, aa, i, 0))],
        out_specs=pl.BlockSpec((None, tr, c), lambda aa, i: (aa, i, 0)),
        out_shape=jax.ShapeDtypeStruct((a, w, c), F32),
        compiler_params=_params(("parallel", "parallel")),
    )(g, r1, r2)


def _sum_slots(g):
    _, n, r, c = g.shape

    def body(g_ref, o_ref):
        s = g_ref[0, 0]
        for k in range(1, n):
            s = s + g_ref[0, k]
        o_ref[...] = s

    return pl.pallas_call(
        body, name="sum_slots", grid=(1,),
        in_specs=[pl.BlockSpec((1, n, r, c), lambda i: (0, 0, 0, 0))], out_specs=pl.BlockSpec((r, c), lambda i: (0, 0)),
        out_shape=jax.ShapeDtypeStruct((r, c), F32), compiler_params=_params(("arbitrary",)),
    )(g)


_WEIGHTS = ['ffn1_norm', 'ffn1_w_gate', 'ffn1_w_up', 'ffn1_w_down', 'mix_norm', 'ffn2_norm', 'ffn2_w_gate', 'ffn2_w_up',
            'ffn2_w_down', 'ev_w_in', 'hg_lb_logits', 'hg_norm_w', 's5_a_re', 's5_a_im', 's5_b_re', 's5_b_im', 's5_c_re',
            's5_c_im', 's5_d', 's5_log_dt', 's5_w_glu', 'ev_w_out', 'od_w_in', 'gdn_conv_w', 'gdn_a_log', 'gdn_dt_bias',
            'gdn_norm_w', 'od_w_out', 'final_norm']
_SHARDED = ['ffn1_w_gate', 'ffn1_w_up', 'ffn1_w_down', 'ffn2_w_gate', 'ffn2_w_up', 'ffn2_w_down', 'ev_w_in', 's5_w_glu',
            'ev_w_out', 'od_w_in', 'gdn_conv_w', 'od_w_out']
_REPLICATED = [n for n in _WEIGHTS if n not in _SHARDED]
_SMALL_COLS = 1024


def _pad_rows(a, rows):
    return jnp.pad(a, ((0, rows - a.shape[0]), (0, 0)))


def _layer_blocks(w, layer):
    j = layer // 2
    ffn = []
    for f in ('ffn1', 'ffn2'):
        ffn += [_pad_rows(w[f + '_w_gate'][layer].T, FF_SHARD_PAD), _pad_rows(w[f + '_w_up'][layer].T, FF_SHARD_PAD),
                _pad_rows(w[f + '_w_down'][layer], FF_SHARD_PAD)]
    ffn = jnp.stack(ffn).astype(BF16)
    if layer % 2 == 0:
        return [ffn, w['ev_w_in'][j].T.astype(BF16)[None], w['s5_w_glu'][j].astype(BF16)[None],
                w['ev_w_out'][j].astype(BF16)[None]]
    return [ffn, w['od_w_in'][j].astype(BF16)[None], w['gdn_conv_w'][j][None], w['od_w_out'][j].astype(BF16)[None]]


def _layer_whole(gathered, layer):
    whole = lambda g: g.reshape(g.shape[0], N_DEV * g.shape[2], g.shape[3])
    ffn, a1, a2, w_out = gathered
    if layer % 2 == 0:
        return dict(ffn=whole(ffn), w_in_t=whole(a1)[0], w_glu=whole(a2)[0], w_out=whole(w_out)[0])
    w_in = jnp.moveaxis(a1[0], 0, 1).reshape(D_MODEL, OD_IN)
    conv = jnp.moveaxis(a2[0], 0, 1).reshape(CONV_W, GDN_QKV)
    return dict(ffn=whole(ffn), w_in=jnp.pad(w_in, ((0, 0), (0, OD_IN_PAD - OD_IN))), conv_w=conv, w_out=whole(w_out)[0])


def _layer_grad_windows(g, layer):
    windows = lambda a, w: a.reshape(-1, N_DEV, w, a.shape[-1])
    ffn = windows(g['ffn'], FF_SHARD_PAD)
    if 'w_out' not in g:
        return [ffn]
    w_out = windows(g['w_out'], D_MODEL // N_DEV)
    if layer % 2 == 0:
        return [ffn, windows(g['w_in_t'], EV_IN // N_DEV), windows(g['w_glu'], S5_WIDTH // N_DEV), w_out]
    w_in = jnp.moveaxis(g['w_in'].reshape(D_MODEL, N_DEV, OD_IN // N_DEV), 1, 0)[None]
    conv = jnp.moveaxis(g['conv_w'].reshape(CONV_W, N_DEV, GDN_QKV // N_DEV), 1, 0)[None]
    return [ffn, w_in, conv, w_out]


def _flat_small(src, prefix=''):
    flat = jnp.concatenate([src[prefix + n].reshape(-1) for n in _REPLICATED])
    rows = -(-(flat.shape[0] + 1) // (_SMALL_COLS * 8)) * 8
    return jnp.pad(flat, (0, rows * _SMALL_COLS - flat.shape[0])).reshape(rows, _SMALL_COLS)


def _split_small(flat2d, like):
    flat, out, off = flat2d.reshape(-1), {}, 0
    for n in _REPLICATED:
        size = math.prod(like[n].shape)
        out[n] = flat[off:off + size].reshape(like[n].shape)
        off += size
    return out, flat[off]


def _all_reduce_small(gsmall, loss):
    flat = _flat_small(gsmall)
    n_used = sum(math.prod(gsmall[n].shape) for n in _REPLICATED)
    flat = flat.reshape(-1).at[n_used].set(loss).reshape(flat.shape)
    (gathered,) = _all_gather([flat[None]], "gather_small")
    return _split_small(_sum_slots(gathered), gsmall)


def _as_2d(a):
    return a.reshape(-1, a.shape[-1])


def kernel(x, ffn1_norm, ffn1_w_gate, ffn1_w_up, ffn1_w_down, mix_norm, ffn2_norm, ffn2_w_gate, ffn2_w_up, ffn2_w_down, ev_w_in, hg_lb_logits, hg_norm_w, s5_a_re, s5_a_im, s5_b_re, s5_b_im, s5_c_re, s5_c_im, s5_d, s5_log_dt, s5_w_glu, ev_w_out, od_w_in, gdn_conv_w, gdn_a_log, gdn_dt_bias, gdn_norm_w, od_w_out, final_norm, loss_target, m_ffn1_norm, m_ffn1_w_gate, m_ffn1_w_up, m_ffn1_w_down, m_mix_norm, m_ffn2_norm, m_ffn2_w_gate, m_ffn2_w_up, m_ffn2_w_down, m_ev_w_in, m_hg_lb_logits, m_hg_norm_w, m_s5_a_re, m_s5_a_im, m_s5_b_re, m_s5_b_im, m_s5_c_re, m_s5_c_im, m_s5_d, m_s5_log_dt, m_s5_w_glu, m_ev_w_out, m_od_w_in, m_gdn_conv_w, m_gdn_a_log, m_gdn_dt_bias, m_gdn_norm_w, m_od_w_out, m_final_norm, v_ffn1_norm, v_ffn1_w_gate, v_ffn1_w_up, v_ffn1_w_down, v_mix_norm, v_ffn2_norm, v_ffn2_w_gate, v_ffn2_w_up, v_ffn2_w_down, v_ev_w_in, v_hg_lb_logits, v_hg_norm_w, v_s5_a_re, v_s5_a_im, v_s5_b_re, v_s5_b_im, v_s5_c_re, v_s5_c_im, v_s5_d, v_s5_log_dt, v_s5_w_glu, v_ev_w_out, v_od_w_in, v_gdn_conv_w, v_gdn_a_log, v_gdn_dt_bias, v_gdn_norm_w, v_od_w_out, v_final_norm):
    given = dict(locals())
    w = {n: given[n] for n in _WEIGHTS}
    small = {n: w[n] for n in _REPLICATED}
    me = _device_of(_here())

    blocks = [_layer_blocks(w, layer) for layer in range(DEPTH)]
    first = _all_gather(blocks[0], "gather_weights")
    pending = {}

    def start_gather(layer, after=None):
        lands = [lax.dynamic_update_slice(lax.empty((b.shape[0], N_DEV) + b.shape[1:], b.dtype), b[:, None], (0, me, 0, 0))
                 for b in blocks[layer]]
        pending[layer] = _exchange_start(blocks[layer], lands, _gather_copies, "gather_start_%d" % layer, after)

    start_gather(1)

    def layer_weights(layer, after):
        if layer == 0:
            return _layer_whole(first, 0)
        got = _exchange_wait(pending[layer], _gather_three, after, "gather_wait_%d" % layer)[1]
        if layer + 1 < DEPTH:
            start_gather(layer + 1, after)
        return _layer_whole(_core_forward(got), layer)

    sent = {}

    def layer_done(layer, part, grads):
        gs = _layer_grad_windows(grads, layer)
        from_sibling = _rs_core_exchange(gs)
        chip = [_rs_chip_sum(g, r) for g, r in zip(gs, from_sibling)]
        lands = [lax.empty((len(_OTHER_CHIPS),) + p.shape[1:], p.dtype) for p in chip]
        sent[layer, part] = (gs, from_sibling, _exchange_start(chip, lands, _scatter_copies,
                                                              "scatter_start_%d_%d" % (layer, part)))

    loss_part, grad_x, gsmall = _local_step(given['x'], given['loss_target'], layer_weights, small, layer_done)
    gsum, loss = _all_reduce_small(gsmall, loss_part)
    summed = {}
    for layer in reversed(range(DEPTH)):
        for part in range(2):
            gs, from_sibling, handle = sent[layer, part]
            _, lands = _exchange_wait(handle, _scatter_three, grad_x, "scatter_wait_%d_%d" % (layer, part))
            summed[layer, part] = [_rs_final_sum(g, r, land) for g, r, land in zip(gs, from_sibling, lands)]

    grads = dict(gsum)
    t_ = lambda a: jnp.swapaxes(a, 1, 2)
    for i, f in enumerate(('ffn1', 'ffn2')):
        ffn = jnp.stack([summed[layer, 1 - i][0][:, :FF_SHARD] for layer in range(DEPTH)], axis=1)
        grads[f + '_w_gate'], grads[f + '_w_up'], grads[f + '_w_down'] = t_(ffn[0]), t_(ffn[1]), ffn[2]
    even, odd = [summed[0, 0], summed[2, 0]], [summed[1, 0], summed[3, 0]]
    grads['ev_w_in'] = jnp.stack([r[1][0].T for r in even])
    grads['s5_w_glu'] = jnp.stack([r[2][0] for r in even])
    grads['ev_w_out'] = jnp.stack([r[3][0] for r in even])
    grads['od_w_in'] = jnp.stack([r[1][0] for r in odd])
    grads['gdn_conv_w'] = jnp.stack([r[2][0] for r in odd])
    grads['od_w_out'] = jnp.stack([r[3][0] for r in odd])

    delta, new_m, new_v = {}, {}, {}
    for n in _SHARDED:
        d, nm, nv = _adamw(_as_2d(w[n]), _as_2d(grads[n]), _as_2d(given['m_' + n]), _as_2d(given['v_' + n]))
        delta[n], new_m[n], new_v[n] = (a.reshape(w[n].shape) for a in (d, nm, nv))
    d, nm, nv = _adamw(_flat_small(w), _flat_small(grads), _flat_small(given, 'm_'), _flat_small(given, 'v_'))
    (delta_s, _), (new_m_s, _), (new_v_s, _) = (_split_small(a, small) for a in (d, nm, nv))
    delta.update(delta_s)
    new_m.update(new_m_s)
    new_v.update(new_v_s)
    return (loss, grad_x, *[grads[n] for n in _WEIGHTS], *[delta[n] for n in _WEIGHTS],
            *[new_m[n] for n in _WEIGHTS], *[new_v[n] for n in _WEIGHTS])
```

```python
import math

import jax
import jax.numpy as jnp
from jax import lax
from jax.experimental import pallas as pl
from jax.experimental.pallas import tpu as pltpu

F32 = jnp.float32
BF16 = jnp.bfloat16
HI = lax.Precision.HIGH

D_MODEL = 1024
DEPTH = 4
D_FF = 2816
NORM_EPS = 1e-6
F_MIN = 1e-6
CHUNK = 64
HG_HEADS = 4
HEAD_DIM = 128
HG_QK = HG_HEADS * HEAD_DIM
S5_WIDTH = 512
S5_GROUP = 16
S5_GROUPS = 32
S5_STATE = 64
S5_NSTATE = S5_GROUPS * S5_STATE
EV_IN = 2560
GDN_HEADS = 8
GDN_QKV = 3 * GDN_HEADS * HEAD_DIM
CONV_W = 4
OD_IN = 4112
OD_IN_PAD = 4224
N_DEV = 8
LANES = 128
ADAM_LR, ADAM_B1, ADAM_B2, ADAM_EPS, ADAM_WD, ADAM_STEP = 0.001, 0.9, 0.999, 1e-08, 0.01, 10
VMEM_LIMIT = 56 * 1024 * 1024

MESH = pl.DeviceIdType.MESH


def _params(sem=None, **kw):
    return pltpu.CompilerParams(dimension_semantics=sem, vmem_limit_bytes=VMEM_LIMIT, **kw)


def _tile(n, cands):
    for c in cands:
        if n % c == 0:
            return c
    return n


def _dot(a, b):
    return jnp.dot(a.astype(BF16), b.astype(BF16), preferred_element_type=F32)


def _dot_nt(a, b):
    return lax.dot_general(a.astype(BF16), b.astype(BF16), (((1,), (1,)), ((), ())), preferred_element_type=F32)


def _dot_tn(a, b):
    return lax.dot_general(a.astype(BF16), b.astype(BF16), (((0,), (0,)), ((), ())), preferred_element_type=F32)


def _dot_hi(a, b):
    return jnp.dot(a, b, precision=HI, preferred_element_type=F32)


def _bmm(a, b):
    return jnp.einsum('gmk,gkn->gmn', a.astype(BF16), b.astype(BF16), preferred_element_type=F32)


def _bmm_nt(a, b):
    return jnp.einsum('gmk,gnk->gmn', a.astype(BF16), b.astype(BF16), preferred_element_type=F32)


def _bmm_tn(a, b):
    return jnp.einsum('gkm,gkn->gmn', a.astype(BF16), b.astype(BF16), preferred_element_type=F32)


def _bmm_hi(a, b):
    return jnp.einsum('gmk,gkn->gmn', a, b, precision=HI, preferred_element_type=F32)


_TN_CANDS = (1408, 1280, 1024, 512, 384, 256, 128)
_B_TILE_BYTES = 6 * 1024 * 1024


def _mm(a, b, *, name, nt=False, out_dtype=F32, res=None, scale=1.0):
    m, k = a.shape
    n = b.shape[0] if nt else b.shape[1]
    tm = _tile(m, (512, 256, 128))
    tn = _tile(n, [c for c in _TN_CANDS if c * k * b.dtype.itemsize <= _B_TILE_BYTES])

    def body(*refs):
        if res is None:
            a_ref, b_ref, o_ref = refs
        else:
            a_ref, b_ref, r_ref, o_ref = refs
        acc = (_dot_nt if nt else _dot)(a_ref[...], b_ref[...])
        if scale != 1.0:
            acc = scale * acc
        if res is not None:
            acc = r_ref[...] + acc
        o_ref[...] = acc.astype(out_dtype)

    b_spec = pl.BlockSpec((tn, k), lambda j, i: (j, 0)) if nt else pl.BlockSpec((k, tn), lambda j, i: (0, j))
    in_specs = [pl.BlockSpec((tm, k), lambda j, i: (i, 0)), b_spec]
    args = [a, b]
    if res is not None:
        in_specs.append(pl.BlockSpec((tm, tn), lambda j, i: (i, j)))
        args.append(res)
    return pl.pallas_call(
        body, name=name, grid=(n // tn, m // tm), in_specs=in_specs,
        out_specs=pl.BlockSpec((tm, tn), lambda j, i: (i, j)),
        out_shape=jax.ShapeDtypeStruct((m, n), out_dtype),
        compiler_params=_params(("parallel", "parallel")),
    )(*args)


def _mm_tn(a, b, *, name, scale=1.0, out_dtype=F32, into=None, slot=0):
    t, m = a.shape
    n = b.shape[1]
    tm = _tile(m, (1024, 512, 256, 128))
    tn = _tile(n, _TN_CANDS)
    tk = _tile(t, (2048, 1024, 512, 256, 128))
    nk = t // tk

    def body(*refs):
        a_ref, b_ref = refs[:2]
        o_ref, acc_ref = refs[-2:]
        kk = pl.program_id(2)

        @pl.when(kk == 0)
        def _():
            acc_ref[...] = jnp.zeros_like(acc_ref)

        acc_ref[...] += _dot_tn(a_ref[...], b_ref[...])

        @pl.when(kk == nk - 1)
        def _():
            o_ref[...] = (acc_ref[...] * scale if scale != 1.0 else acc_ref[...]).astype(o_ref.dtype)

    in_specs = [pl.BlockSpec((tk, tm), lambda i, j, kk: (kk, i)), pl.BlockSpec((tk, tn), lambda i, j, kk: (kk, j))]
    args = [a, b]
    if into is None:
        out_spec = pl.BlockSpec((tm, tn), lambda i, j, kk: (i, j))
        out_shape = jax.ShapeDtypeStruct((m, n), out_dtype)
        alias = {}
    else:
        in_specs.append(pl.BlockSpec(memory_space=pl.ANY))
        args.append(into)
        out_spec = pl.BlockSpec((None, tm, tn), lambda i, j, kk: (slot, i, j))
        out_shape = jax.ShapeDtypeStruct(into.shape, into.dtype)
        alias = {2: 0}
    return pl.pallas_call(
        body, name=name, grid=(m // tm, n // tn, nk), in_specs=in_specs, out_specs=out_spec, out_shape=out_shape,
        scratch_shapes=[pltpu.VMEM((tm, tn), F32)], input_output_aliases=alias,
        compiler_params=_params(("parallel", "parallel", "arbitrary")),
    )(*args)


def _sigmoid(x):
    return jax.nn.sigmoid(x)


def _head_norm_gate(o, gate, nw):
    r = lax.rsqrt(jnp.mean(o * o, axis=-1, keepdims=True) + NORM_EPS)
    return o * r * nw * (gate * _sigmoid(gate))


def _hg_chunk(st, ql, fl, iv, gl, lb, nw):
    n_g, c, _ = ql.shape
    q = ql * _sigmoid(ql)
    f = lb + (1.0 - lb) * _sigmoid(fl)
    lf = jnp.log(jnp.maximum(f, F_MIN))
    k = 1.0 - f
    ri = lax.broadcasted_iota(jnp.int32, (c, c), 0)
    ci = lax.broadcasted_iota(jnp.int32, (c, c), 1)
    rl = lax.broadcasted_iota(jnp.int32, (c, LANES), 0)
    each = lambda m2: jnp.broadcast_to(m2[None], (n_g,) + m2.shape)
    b = _bmm_hi(each(jnp.where(ci <= ri, 1.0, 0.0)), lf)
    attn = jnp.where((ri == ci)[None], _bmm_nt(q, k), 0.0)
    sh = 0
    while (1 << sh) < c:
        m = 1 << sh
        ref = ((ri >> (sh + 1)) << (sh + 1)) + (m - 1)
        low_r = ((ri >> sh) & 1) == 1
        low_c = ((ci >> sh) & 1) == 1
        w_low = low_r & (ci > ref) & (ci <= ri)
        w_up = jnp.logical_not(low_r) & (ci > ri) & (ci <= ref)
        w = jnp.where(w_low | w_up, 1.0, 0.0)
        e = jnp.exp(_bmm_hi(each(w), lf))
        low_l = (((rl >> sh) & 1) == 1)[None]
        qs = jnp.where(low_l, q * e, 0.0)
        ks = jnp.where(low_l, 0.0, k * e)
        pair = ((ri >> (sh + 1)) == (ci >> (sh + 1))) & low_r & jnp.logical_not(low_c)
        attn = attn + jnp.where(pair[None], _bmm_nt(qs, ks), 0.0)
        sh += 1
    bl = jnp.sum(lf, axis=1, keepdims=True)
    o = _bmm(attn, iv) + _bmm_nt(q * jnp.exp(b), st)
    st_new = st * jnp.exp(bl) + _bmm_tn(iv, k * jnp.exp(bl - b))
    return _head_norm_gate(o, gl, nw), st_new


def _hg_specs(nb, n_c, rev):
    def cidx(cc):
        return (n_c - 1 - cc) if rev else cc

    def col(width):
        return pl.BlockSpec((nb, CHUNK, width), lambda cc: (0, cidx(cc), 0))

    vec = pl.BlockSpec((HG_HEADS, 1, LANES), lambda cc: (0, 0, 0))
    nw = pl.BlockSpec((1, LANES), lambda cc: (0, 0))
    st = pl.BlockSpec((nb, HG_HEADS, None, LANES, LANES), lambda cc: (0, 0, cidx(cc), 0, 0))
    acc = pl.BlockSpec((nb, HG_HEADS, 1, LANES), lambda cc: (0, 0, 0, 0))
    return col, vec, nw, st, acc


def _head(j):
    return slice(j * LANES, (j + 1) * LANES)


def _seq_heads(ref, nb, n, first=0):
    return jnp.stack([ref[b, :, _head(first + j)] for b in range(nb) for j in range(n)])


def _put_seq_heads(ref, val, nb, n, first=0):
    for b in range(nb):
        for j in range(n):
            ref[b, :, _head(first + j)] = val[b * n + j].astype(ref.dtype)


def _hgrn2_fwd(proj, lb, nw):
    nb, l, _ = proj.shape
    n_c = l // CHUNK
    n_g = nb * HG_HEADS
    col, vec, nws, st, _ = _hg_specs(nb, n_c, False)

    def body(p_ref, lb_ref, nw_ref, y_ref, st_ref, s_scr):
        @pl.when(pl.program_id(0) == 0)
        def _():
            s_scr[...] = jnp.zeros_like(s_scr)

        s_in = s_scr[...]
        st_ref[...] = s_in.reshape(st_ref.shape)
        args = [_seq_heads(p_ref, nb, HG_HEADS, k * HG_HEADS) for k in range(4)]
        y, s_new = _hg_chunk(s_in, *args, jnp.concatenate([lb_ref[...]] * nb), nw_ref[...])
        _put_seq_heads(y_ref, y, nb, HG_HEADS)
        s_scr[...] = s_new

    return pl.pallas_call(
        body, name="hgrn2_fwd", grid=(n_c,),
        in_specs=[col(4 * HG_QK), vec, nws], out_specs=[col(HG_QK), st],
        out_shape=[jax.ShapeDtypeStruct((nb, l, HG_QK), BF16),
                   jax.ShapeDtypeStruct((nb, HG_HEADS, n_c, LANES, LANES), F32)],
        scratch_shapes=[pltpu.VMEM((n_g, LANES, LANES), F32)],
        compiler_params=_params(("arbitrary",)),
    )(proj, lb, nw)


def _hgrn2_bwd(proj, lb, nw, states, dy):
    nb, l, _ = proj.shape
    n_c = l // CHUNK
    n_g = nb * HG_HEADS
    col, vec, nws, st, acc = _hg_specs(nb, n_c, True)

    def body(p_ref, lb_ref, nw_ref, st_ref, dy_ref, dp_ref, dlb_ref, dnw_ref, ds_scr):
        @pl.when(pl.program_id(0) == 0)
        def _():
            ds_scr[...] = jnp.zeros_like(ds_scr)
            dlb_ref[...] = jnp.zeros_like(dlb_ref)
            dnw_ref[...] = jnp.zeros_like(dnw_ref)

        args = [_seq_heads(p_ref, nb, HG_HEADS, k * HG_HEADS) for k in range(4)]
        _, vjp = jax.vjp(_hg_chunk, st_ref[...].reshape(n_g, LANES, LANES), *args,
                         jnp.concatenate([lb_ref[...]] * nb), nw_ref[...])
        ds, dql, dfl, div, dgl, dlb, dnw = vjp((_seq_heads(dy_ref, nb, HG_HEADS).astype(F32), ds_scr[...]))
        ds_scr[...] = ds
        for k, val in enumerate((dql, dfl, div, dgl)):
            _put_seq_heads(dp_ref, val, nb, HG_HEADS, k * HG_HEADS)
        dlb_ref[...] += dlb.reshape(dlb_ref.shape)
        dnw_ref[...] += dnw

    return pl.pallas_call(
        body, name="hgrn2_bwd", grid=(n_c,),
        in_specs=[col(4 * HG_QK), vec, nws, st, col(HG_QK)], out_specs=[col(4 * HG_QK), acc, nws],
        out_shape=[jax.ShapeDtypeStruct((nb, l, 4 * HG_QK), BF16),
                   jax.ShapeDtypeStruct((nb, HG_HEADS, 1, LANES), F32), jax.ShapeDtypeStruct((1, LANES), F32)],
        scratch_shapes=[pltpu.VMEM((n_g, LANES, LANES), F32)],
        compiler_params=_params(("arbitrary",)),
    )(proj, lb, nw, states, dy)


def _shift_down(x, s):
    if s == 0:
        return x
    rows = lax.broadcasted_iota(jnp.int32, x.shape, 0)
    return jnp.where(rows >= s, pltpu.roll(x, s, 0), 0.0)


def _shift_up(x, s):
    if s == 0:
        return x
    n = x.shape[0]
    rows = lax.broadcasted_iota(jnp.int32, x.shape, 0)
    return jnp.where(rows < n - s, pltpu.roll(x, n - s, 0), 0.0)


def _gdn_pre_fwd(proj, conv_w):
    nb, l, _ = proj.shape
    n_blk = GDN_QKV // LANES

    def body(x_ref, w_ref, o_ref):
        j = pl.program_id(1)
        x = x_ref[...]
        w = w_ref[...]
        c = w[3:4] * x
        for t in range(CONV_W - 1):
            c = c + w[t:t + 1] * _shift_down(x, CONV_W - 1 - t)
        s = c * _sigmoid(c)
        r = lax.rsqrt(jnp.sum(s * s, axis=-1, keepdims=True) + NORM_EPS)
        scale = jnp.where(j < GDN_HEADS, HEAD_DIM ** -0.5, 1.0)
        o_ref[...] = jnp.where(j < 2 * GDN_HEADS, s * r * scale, s)

    return pl.pallas_call(
        body, name="gdn_pre_fwd", grid=(nb, n_blk),
        in_specs=[pl.BlockSpec((None, l, LANES), lambda b, j: (b, 0, j)), pl.BlockSpec((CONV_W, LANES), lambda b, j: (0, j))],
        out_specs=pl.BlockSpec((None, l, LANES), lambda b, j: (b, 0, j)),
        out_shape=jax.ShapeDtypeStruct((nb, l, GDN_QKV), F32),
        compiler_params=_params(("parallel", "parallel")),
    )(proj, conv_w)


def _gdn_pre_bwd(proj, conv_w, dout):
    nb, l, _ = proj.shape
    n_blk = GDN_QKV // LANES

    def body(x_ref, w_ref, d_ref, dx_ref, dw_ref):
        j = pl.program_id(1)
        x = x_ref[...]
        w = w_ref[...]
        xs = [_shift_down(x, CONV_W - 1 - t) for t in range(CONV_W)]
        c = w[0:1] * xs[0]
        for t in range(1, CONV_W):
            c = c + w[t:t + 1] * xs[t]
        sg = _sigmoid(c)
        s = c * sg
        d = d_ref[...]
        r = lax.rsqrt(jnp.sum(s * s, axis=-1, keepdims=True) + NORM_EPS)
        scale = jnp.where(j < GDN_HEADS, HEAD_DIM ** -0.5, 1.0)
        ds_norm = scale * r * (d - s * (r * r) * jnp.sum(d * s, axis=-1, keepdims=True))
        ds = jnp.where(j < 2 * GDN_HEADS, ds_norm, d)
        dc = ds * (sg * (1.0 + c * (1.0 - sg)))
        dx = w[3:4] * dc
        for t in range(CONV_W - 1):
            dx = dx + w[t:t + 1] * _shift_up(dc, CONV_W - 1 - t)
        dx_ref[...] = dx.astype(dx_ref.dtype)
        for t in range(CONV_W):
            dw_ref[t:t + 1, :] = jnp.sum(dc * xs[t], axis=0, keepdims=True)

    return pl.pallas_call(
        body, name="gdn_pre_bwd", grid=(nb, n_blk),
        in_specs=[pl.BlockSpec((None, l, LANES), lambda b, j: (b, 0, j)), pl.BlockSpec((CONV_W, LANES), lambda b, j: (0, j)),
                  pl.BlockSpec((None, l, LANES), lambda b, j: (b, 0, j))],
        out_specs=[pl.BlockSpec((None, l, LANES), lambda b, j: (b, 0, j)), pl.BlockSpec((None, CONV_W, LANES), lambda b, j: (b, 0, j))],
        out_shape=[jax.ShapeDtypeStruct((nb, l, GDN_QKV), BF16), jax.ShapeDtypeStruct((nb, CONV_W, GDN_QKV), F32)],
        compiler_params=_params(("parallel", "parallel")),
    )(proj, conv_w, dout)


def _gdn_chunk(st, q, k, v, gate, bl, al, alog, dtb, nw):
    n_g, c, _ = q.shape
    beta = _sigmoid(bl)
    x = al + dtb
    softplus = jnp.maximum(x, 0.0) + jnp.log(1.0 + jnp.exp(-jnp.abs(x)))
    la = -jnp.exp(alog) * softplus
    ri = lax.broadcasted_iota(jnp.int32, (c, c), 0)
    ci = lax.broadcasted_iota(jnp.int32, (c, c), 1)
    lower = (ci <= ri)[None]
    strict = (ci < ri)[None]
    ltri = jnp.broadcast_to(jnp.where(lower, 1.0, 0.0), (n_g, c, c))
    la_l = la + jnp.zeros((1, 1, LANES), F32)
    g = _bmm_hi(ltri, la_l)
    delta = _bmm_hi(ltri, jnp.where(strict, la + jnp.zeros((1, 1, c), F32), 0.0))
    gam = jnp.where(lower, jnp.exp(jnp.where(lower, delta, 0.0)), 0.0)
    kb = k * beta
    vb = v * beta
    m = jnp.where(strict, _bmm_nt(kb, k) * gam, 0.0)
    pw = -m
    t_inv = jnp.where((ri == ci)[None], 1.0, 0.0) + pw
    steps = int(math.log2(c)) - 1
    for _ in range(steps):
        pw = _bmm_hi(pw, pw)
        t_inv = t_inv + _bmm_hi(t_inv, pw)
    eg = jnp.exp(g)
    u = _bmm(t_inv, vb)
    w = _bmm(t_inv, kb * eg)
    attn = jnp.where(lower, _bmm_nt(q, k) * gam, 0.0)
    v_new = u - _bmm_nt(w, st)
    o = _bmm_nt(q * eg, st) + _bmm(attn, v_new)
    g_last = jnp.sum(la_l, axis=1, keepdims=True)
    st_new = st * jnp.exp(g_last) + _bmm_tn(v_new, k * jnp.exp(g_last - g))
    return _head_norm_gate(o, gate, nw), st_new


GDN_WIDE = GDN_HEADS * LANES
GDN_SMALL_BLOCK = (GDN_QKV + GDN_WIDE) // LANES


def _gdn_specs(nb, n_c, rev):
    def cidx(cc):
        return (n_c - 1 - cc) if rev else cc

    def col(width, blk=0):
        return pl.BlockSpec((nb, CHUNK, width), lambda cc: (0, cidx(cc), blk))

    scal = pl.BlockSpec((GDN_HEADS, 1, 1), lambda cc: (0, 0, 0))
    nw = pl.BlockSpec((1, LANES), lambda cc: (0, 0))
    st = pl.BlockSpec((nb, GDN_HEADS, None, LANES, LANES), lambda cc: (0, 0, cidx(cc), 0, 0))
    acc_s = pl.BlockSpec((nb, GDN_HEADS, 1, 1), lambda cc: (0, 0, 0, 0))
    return col, scal, nw, st, acc_s


def _tok_cols(xs, nb, first):
    return jnp.stack([xs[b][:, first + h:first + h + 1] for b in range(nb) for h in range(GDN_HEADS)])


def _gdn_inputs(qkv_ref, p_gate_ref, p_small_ref, alog_ref, dtb_ref, nw_ref, nb):
    q, k, v = (_seq_heads(qkv_ref, nb, GDN_HEADS, i * GDN_HEADS) for i in range(3))
    xs = p_small_ref[...]
    return (q, k, v, _seq_heads(p_gate_ref, nb, GDN_HEADS), _tok_cols(xs, nb, 0), _tok_cols(xs, nb, GDN_HEADS),
            jnp.concatenate([alog_ref[...]] * nb), jnp.concatenate([dtb_ref[...]] * nb), nw_ref[...])


def _gdn_fwd(qkv, proj, alog, dtb, nw):
    nb, l, _ = qkv.shape
    n_c = l // CHUNK
    n_g = nb * GDN_HEADS
    col, scal, nws, st, _ = _gdn_specs(nb, n_c, False)

    def body(qkv_ref, pg_ref, ps_ref, alog_ref, dtb_ref, nw_ref, y_ref, st_ref, s_scr):
        @pl.when(pl.program_id(0) == 0)
        def _():
            s_scr[...] = jnp.zeros_like(s_scr)

        s_in = s_scr[...]
        st_ref[...] = s_in.reshape(st_ref.shape)
        y, s_new = _gdn_chunk(s_in, *_gdn_inputs(qkv_ref, pg_ref, ps_ref, alog_ref, dtb_ref, nw_ref, nb))
        _put_seq_heads(y_ref, y, nb, GDN_HEADS)
        s_scr[...] = s_new

    return pl.pallas_call(
        body, name="gdn_fwd", grid=(n_c,),
        in_specs=[col(GDN_QKV), col(GDN_WIDE, GDN_QKV // GDN_WIDE), col(LANES, GDN_SMALL_BLOCK), scal, scal, nws],
        out_specs=[col(GDN_WIDE), st],
        out_shape=[jax.ShapeDtypeStruct((nb, l, GDN_WIDE), BF16),
                   jax.ShapeDtypeStruct((nb, GDN_HEADS, n_c, LANES, LANES), F32)],
        scratch_shapes=[pltpu.VMEM((n_g, LANES, LANES), F32)],
        compiler_params=_params(("arbitrary",)),
    )(qkv, proj, proj, alog, dtb, nw)


def _gdn_bwd(qkv, proj, alog, dtb, nw, states, dy):
    nb, l, _ = qkv.shape
    n_c = l // CHUNK
    n_g = nb * GDN_HEADS
    col, scal, nws, st, acc_s = _gdn_specs(nb, n_c, True)

    def body(qkv_ref, pg_ref, ps_ref, alog_ref, dtb_ref, nw_ref, st_ref, dy_ref,
             dqkv_ref, dg_ref, dsm_ref, dalog_ref, ddtb_ref, dnw_ref, ds_scr):
        @pl.when(pl.program_id(0) == 0)
        def _():
            ds_scr[...] = jnp.zeros_like(ds_scr)
            dalog_ref[...] = jnp.zeros_like(dalog_ref)
            ddtb_ref[...] = jnp.zeros_like(ddtb_ref)
            dnw_ref[...] = jnp.zeros_like(dnw_ref)

        _, vjp = jax.vjp(_gdn_chunk, st_ref[...].reshape(n_g, LANES, LANES),
                         *_gdn_inputs(qkv_ref, pg_ref, ps_ref, alog_ref, dtb_ref, nw_ref, nb))
        ds, dq, dk, dv, dg, dbl, dal, dalog, ddtb, dnw = vjp((_seq_heads(dy_ref, nb, GDN_HEADS).astype(F32), ds_scr[...]))
        ds_scr[...] = ds
        for i, val in enumerate((dq, dk, dv)):
            _put_seq_heads(dqkv_ref, val, nb, GDN_HEADS, i * GDN_HEADS)
        _put_seq_heads(dg_ref, dg, nb, GDN_HEADS)
        lane = lax.broadcasted_iota(jnp.int32, (CHUNK, LANES), 1)
        for b in range(nb):
            small = jnp.zeros((CHUNK, LANES), F32)
            for h in range(GDN_HEADS):
                small = small + jnp.where(lane == h, dbl[b * GDN_HEADS + h], 0.0)
                small = small + jnp.where(lane == GDN_HEADS + h, dal[b * GDN_HEADS + h], 0.0)
            dsm_ref[b] = small.astype(dsm_ref.dtype)
        dalog_ref[...] += dalog.reshape(dalog_ref.shape)
        ddtb_ref[...] += ddtb.reshape(ddtb_ref.shape)
        dnw_ref[...] += dnw

    h = GDN_HEADS
    return pl.pallas_call(
        body, name="gdn_bwd", grid=(n_c,),
        in_specs=[col(GDN_QKV), col(GDN_WIDE, GDN_QKV // GDN_WIDE), col(LANES, GDN_SMALL_BLOCK), scal, scal, nws, st,
                  col(GDN_WIDE)],
        out_specs=[col(GDN_QKV), col(GDN_WIDE), col(LANES), acc_s, acc_s, nws],
        out_shape=[jax.ShapeDtypeStruct((nb, l, GDN_QKV), F32), jax.ShapeDtypeStruct((nb, l, GDN_WIDE), BF16),
                   jax.ShapeDtypeStruct((nb, l, LANES), BF16),
                   jax.ShapeDtypeStruct((nb, h, 1, 1), F32), jax.ShapeDtypeStruct((nb, h, 1, 1), F32),
                   jax.ShapeDtypeStruct((1, LANES), F32)],
        scratch_shapes=[pltpu.VMEM((n_g, LANES, LANES), F32)],
        compiler_params=_params(("arbitrary",)),
    )(qkv, proj, proj, alog, dtb, nw, states, dy)


S5_ROWS = S5_NSTATE // LANES
S5_TB = 256


def _s5_scan_fwd(bu, abar):
    nb, l = bu.shape[:2]
    tb = min(S5_TB, l)

    def body(bu_ref, a_ref, h_ref, c_scr):
        @pl.when(pl.program_id(1) == 0)
        def _():
            c_scr[...] = jnp.zeros_like(c_scr)

        ar = a_ref[0]
        ai = a_ref[1]

        def step(t, carry):
            hr, hi = carry
            nr = ar * hr - ai * hi + bu_ref[t, 0].astype(F32)
            ni = ar * hi + ai * hr + bu_ref[t, 1].astype(F32)
            h_ref[t, 0] = nr.astype(h_ref.dtype)
            h_ref[t, 1] = ni.astype(h_ref.dtype)
            return nr, ni

        hr, hi = lax.fori_loop(0, tb, step, (c_scr[0], c_scr[1]), unroll=8)
        c_scr[0] = hr
        c_scr[1] = hi

    blk = pl.BlockSpec((None, tb, 2, S5_ROWS, LANES), lambda b, i: (b, i, 0, 0, 0))
    return pl.pallas_call(
        body, name="s5_scan_fwd", grid=(nb, l // tb),
        in_specs=[blk, pl.BlockSpec((2, S5_ROWS, LANES), lambda b, i: (0, 0, 0))],
        out_specs=blk, out_shape=jax.ShapeDtypeStruct(bu.shape, bu.dtype),
        scratch_shapes=[pltpu.VMEM((2, S5_ROWS, LANES), F32)],
        compiler_params=_params(("parallel", "arbitrary")),
    )(bu, abar)


def _s5_scan_bwd(dh, h, abar):
    nb, l = dh.shape[:2]
    tb = min(S5_TB, l)
    n_blk = l // tb

    def body(dh_ref, h_ref, a_ref, g_ref, da_ref, c_scr):
        @pl.when(pl.program_id(1) == 0)
        def _():
            c_scr[...] = jnp.zeros_like(c_scr)
            da_ref[...] = jnp.zeros_like(da_ref)

        ar = a_ref[0]
        ai = a_ref[1]

        def step(s, carry):
            gr, gi, dar, dai = carry
            t = tb - 1 - s
            hr = h_ref[t, 0].astype(F32)
            hi = h_ref[t, 1].astype(F32)
            dar = dar + gr * hr + gi * hi
            dai = dai + gi * hr - gr * hi
            nr = ar * gr + ai * gi + dh_ref[t, 0].astype(F32)
            ni = ar * gi - ai * gr + dh_ref[t, 1].astype(F32)
            g_ref[t, 0] = nr.astype(g_ref.dtype)
            g_ref[t, 1] = ni.astype(g_ref.dtype)
            return nr, ni, dar, dai

        z = jnp.zeros((S5_ROWS, LANES), F32)
        gr, gi, dar, dai = lax.fori_loop(0, tb, step, (c_scr[0], c_scr[1], z, z), unroll=8)
        c_scr[0] = gr
        c_scr[1] = gi
        da_ref[0] += dar
        da_ref[1] += dai

    blk = pl.BlockSpec((None, tb, 2, S5_ROWS, LANES), lambda b, i: (b, n_blk - 1 - i, 0, 0, 0))
    return pl.pallas_call(
        body, name="s5_scan_bwd", grid=(nb, n_blk),
        in_specs=[blk, blk, pl.BlockSpec((2, S5_ROWS, LANES), lambda b, i: (0, 0, 0))],
        out_specs=[blk, pl.BlockSpec((None, 2, S5_ROWS, LANES), lambda b, i: (b, 0, 0, 0))],
        out_shape=[jax.ShapeDtypeStruct(dh.shape, dh.dtype), jax.ShapeDtypeStruct((nb, 2, S5_ROWS, LANES), F32)],
        scratch_shapes=[pltpu.VMEM((2, S5_ROWS, LANES), F32)],
        compiler_params=_params(("parallel", "arbitrary")),
    )(dh, h, abar)


_GELU_C = math.sqrt(2.0 / math.pi)


def _gelu(x):
    return 0.5 * x * (1.0 + jnp.tanh(_GELU_C * (x + 0.044715 * x * x * x)))


def _gelu_grad(x):
    t = jnp.tanh(_GELU_C * (x + 0.044715 * x * x * x))
    return 0.5 * (1.0 + t) + 0.5 * x * (1.0 - t * t) * _GELU_C * (1.0 + 3.0 * 0.044715 * x * x)


def _s5_post_fwd(yc, u, d, w_glu):
    t, w = yc.shape
    tt = _tile(t, (512, 256, 128))

    def body(yc_ref, u_ref, d_ref, w_ref, o_ref, y0_ref):
        y0 = yc_ref[...] + d_ref[...] * u_ref[...]
        y = _gelu(y0)
        z = _dot(y, w_ref[...])
        o_ref[...] = (y * _sigmoid(z)).astype(o_ref.dtype)
        y0_ref[...] = y0

    tok = pl.BlockSpec((tt, w), lambda i: (i, 0))
    return pl.pallas_call(
        body, name="s5_post_fwd", grid=(t // tt,),
        in_specs=[tok, tok, pl.BlockSpec((1, w), lambda i: (0, 0)), pl.BlockSpec((w, w), lambda i: (0, 0))],
        out_specs=[tok, tok],
        out_shape=[jax.ShapeDtypeStruct((t, w), BF16), jax.ShapeDtypeStruct((t, w), F32)],
        compiler_params=_params(("parallel",)),
    )(yc, u, d, w_glu)


def _s5_post_bwd(y0, u, d, w_glu, dout):
    t, w = y0.shape
    tt = _tile(t, (512, 256, 128))

    def body(y0_ref, u_ref, d_ref, w_ref, do_ref, dy0_ref, y_ref, dz_ref, du_ref, dd_ref):
        @pl.when(pl.program_id(0) == 0)
        def _():
            dd_ref[...] = jnp.zeros_like(dd_ref)

        y0 = y0_ref[...]
        y = _gelu(y0)
        s = _sigmoid(_dot(y, w_ref[...]))
        do = do_ref[...]
        dz = do * y * s * (1.0 - s)
        dy = do * s + _dot_nt(dz, w_ref[...])
        dy0 = dy * _gelu_grad(y0)
        dy0_ref[...] = dy0.astype(dy0_ref.dtype)
        y_ref[...] = y.astype(y_ref.dtype)
        dz_ref[...] = dz.astype(dz_ref.dtype)
        du_ref[...] = dy0 * d_ref[...]
        dd_ref[...] += jnp.sum(dy0 * u_ref[...], axis=0, keepdims=True)

    tok = pl.BlockSpec((tt, w), lambda i: (i, 0))
    vec = pl.BlockSpec((1, w), lambda i: (0, 0))
    return pl.pallas_call(
        body, name="s5_post_bwd", grid=(t // tt,),
        in_specs=[tok, tok, vec, pl.BlockSpec((w, w), lambda i: (0, 0)), tok],
        out_specs=[tok, tok, tok, tok, vec],
        out_shape=[jax.ShapeDtypeStruct((t, w), BF16), jax.ShapeDtypeStruct((t, w), BF16),
                   jax.ShapeDtypeStruct((t, w), BF16), jax.ShapeDtypeStruct((t, w), F32),
                   jax.ShapeDtypeStruct((1, w), F32)],
        compiler_params=_params(("arbitrary",)),
    )(y0, u, d, w_glu, dout)


def _s5_params(a_re, a_im, b_re, b_im, c_re, c_im, log_dt):
    dt = jnp.exp(log_dt)[:, None]
    mag = jnp.exp(dt * a_re)
    ang = dt * a_im
    abar_re = mag * jnp.cos(ang)
    abar_im = mag * jnp.sin(ang)
    den = a_re * a_re + a_im * a_im
    zr = abar_re - 1.0
    zi = abar_im
    coef_re = ((zr * a_re + zi * a_im) / den)[..., None]
    coef_im = ((zi * a_re - zr * a_im) / den)[..., None]
    bb_re = coef_re * b_re - coef_im * b_im
    bb_im = coef_re * b_im + coef_im * b_re
    eye = jnp.eye(S5_GROUPS, dtype=F32)

    def dense_in(bb):
        return jnp.einsum('gnp,gh->gphn', bb, eye).reshape(S5_WIDTH, S5_NSTATE)

    def dense_out(cc):
        return jnp.einsum('gpn,gh->gnhp', cc, eye).reshape(S5_NSTATE, S5_WIDTH)

    abar = jnp.stack([abar_re.reshape(S5_ROWS, LANES), abar_im.reshape(S5_ROWS, LANES)])
    bbd = jnp.concatenate([dense_in(bb_re), dense_in(bb_im)], axis=1)
    ccd = jnp.concatenate([dense_out(c_re), dense_out(-c_im)], axis=0)
    return abar, bbd, ccd


def _tok_tile(t):
    return _tile(t, (256, 128))


def _rms_fwd(x, w):
    t, d = x.shape
    tt = _tok_tile(t)

    def body(x_ref, w_ref, o_ref):
        xv = x_ref[...]
        r = lax.rsqrt(jnp.mean(xv * xv, axis=-1, keepdims=True) + NORM_EPS)
        o_ref[...] = (xv * r * w_ref[...]).astype(o_ref.dtype)

    tok = pl.BlockSpec((tt, d), lambda i: (i, 0))
    return pl.pallas_call(
        body, name="rms_fwd", grid=(t // tt,), in_specs=[tok, pl.BlockSpec((1, d), lambda i: (0, 0))],
        out_specs=tok, out_shape=jax.ShapeDtypeStruct((t, d), BF16), compiler_params=_params(("parallel",)),
    )(x, w)


def _rms_bwd_math(xv, wv, dy):
    r = lax.rsqrt(jnp.mean(xv * xv, axis=-1, keepdims=True) + NORM_EPS)
    xh = xv * r
    dxh = dy * wv
    dx = r * (dxh - xh * jnp.mean(dxh * xh, axis=-1, keepdims=True))
    return dx, jnp.sum(dy * xh, axis=0, keepdims=True)


def _rms_bwd(x, w, dy, dres):
    t, d = x.shape
    tt = _tok_tile(t)

    def body(x_ref, w_ref, dy_ref, dr_ref, dx_ref, dw_ref):
        @pl.when(pl.program_id(0) == 0)
        def _():
            dw_ref[...] = jnp.zeros_like(dw_ref)

        dx, dw = _rms_bwd_math(x_ref[...], w_ref[...], dy_ref[...])
        dx_ref[...] = dr_ref[...] + dx
        dw_ref[...] += dw

    tok = pl.BlockSpec((tt, d), lambda i: (i, 0))
    vec = pl.BlockSpec((1, d), lambda i: (0, 0))
    return pl.pallas_call(
        body, name="rms_bwd", grid=(t // tt,), in_specs=[tok, vec, tok, tok], out_specs=[tok, vec],
        out_shape=[jax.ShapeDtypeStruct((t, d), F32), jax.ShapeDtypeStruct((1, d), F32)],
        compiler_params=_params(("arbitrary",)),
    )(x, w, dy, dres)


def _loss_head(x, w, target):
    t, d = x.shape
    tt = _tok_tile(t)

    def body(x_ref, w_ref, t_ref, l_ref, dx_ref, dw_ref):
        @pl.when(pl.program_id(0) == 0)
        def _():
            l_ref[...] = jnp.zeros_like(l_ref)
            dw_ref[...] = jnp.zeros_like(dw_ref)

        xv = x_ref[...]
        wv = w_ref[...]
        r = lax.rsqrt(jnp.mean(xv * xv, axis=-1, keepdims=True) + NORM_EPS)
        err = xv * r * wv - t_ref[...]
        row = jnp.sum(err * err, axis=-1, keepdims=True)
        l_ref[...] += (0.5 / d) * jnp.sum(row, axis=0, keepdims=True)
        dx, dw = _rms_bwd_math(xv, wv, err * (1.0 / d))
        dx_ref[...] = dx
        dw_ref[...] += dw

    tok = pl.BlockSpec((tt, d), lambda i: (i, 0))
    vec = pl.BlockSpec((1, d), lambda i: (0, 0))
    return pl.pallas_call(
        body, name="loss_head", grid=(t // tt,), in_specs=[tok, vec, tok],
        out_specs=[pl.BlockSpec((1, 1), lambda i: (0, 0)), tok, vec],
        out_shape=[jax.ShapeDtypeStruct((1, 1), F32), jax.ShapeDtypeStruct((t, d), F32), jax.ShapeDtypeStruct((1, d), F32)],
        compiler_params=_params(("arbitrary",)),
    )(x, w, target)


FF_PAD = 3072
FF_SHARD = D_FF // N_DEV
FF_SHARD_PAD = FF_PAD // N_DEV


def _ffn_in(x, nw, wg_t, wu_t):
    t, d = x.shape
    tm = _tile(t, (512, 256, 128))
    tn = 1024

    def body(x_ref, nw_ref, wg_ref, wu_ref, h_ref, dadg_ref, dadu_ref, a_ref):
        @pl.when(pl.program_id(1) == 0)
        def _():
            xv = x_ref[...]
            r = lax.rsqrt(jnp.mean(xv * xv, axis=-1, keepdims=True) + NORM_EPS)
            h_ref[...] = (xv * r * nw_ref[...]).astype(h_ref.dtype)

        h = h_ref[...]
        g = _dot_nt(h, wg_ref[...])
        u = _dot_nt(h, wu_ref[...])
        s = _sigmoid(g)
        silu = g * s
        dadg_ref[...] = (u * (s + silu * (1.0 - s))).astype(dadg_ref.dtype)
        dadu_ref[...] = silu.astype(dadu_ref.dtype)
        a_ref[...] = (silu * u).astype(a_ref.dtype)

    tok = pl.BlockSpec((tm, d), lambda i, j: (i, 0))
    wsp = pl.BlockSpec((tn, d), lambda i, j: (j, 0))
    wide = pl.BlockSpec((tm, tn), lambda i, j: (i, j))
    return pl.pallas_call(
        body, name="ffn_in", grid=(t // tm, FF_PAD // tn),
        in_specs=[tok, pl.BlockSpec((1, d), lambda i, j: (0, 0)), wsp, wsp], out_specs=[tok, wide, wide, wide],
        out_shape=[jax.ShapeDtypeStruct((t, d), BF16)] + [jax.ShapeDtypeStruct((t, FF_PAD), BF16)] * 3,
        compiler_params=_params(("parallel", "arbitrary")),
    )(x, nw, wg_t, wu_t)


def _ffn_out_dx(dxo, wd, dadg, dadu, after=None):
    t, d = dxo.shape
    tm = _tile(t, (512, 256, 128))
    tn = 1024

    def body(dx_ref, wd_ref, dadg_ref, dadu_ref, *rest):
        dg_ref, du_ref = rest[-2:]
        da = 0.5 * _dot_nt(dx_ref[...], wd_ref[...])
        dg_ref[...] = (da * dadg_ref[...].astype(F32)).astype(dg_ref.dtype)
        du_ref[...] = (da * dadu_ref[...].astype(F32)).astype(du_ref.dtype)

    wide = pl.BlockSpec((tm, tn), lambda j, i: (i, j))
    return pl.pallas_call(
        body, name="ffn_out_dx", grid=(FF_PAD // tn, t // tm),
        in_specs=[pl.BlockSpec((tm, d), lambda j, i: (i, 0)), pl.BlockSpec((tn, d), lambda j, i: (j, 0)), wide, wide]
        + [pl.BlockSpec(memory_space=pl.ANY)] * (after is not None),
        out_specs=[wide, wide], out_shape=[jax.ShapeDtypeStruct((t, FF_PAD), BF16)] * 2,
        compiler_params=_params(("parallel", "parallel")),
    )(dxo, wd, dadg, dadu, *([after] if after is not None else []))


def _ffn_in_dx(dg, du, wg_t, wu_t, x, nw, dxo):
    t, d = x.shape
    tm = _tile(t, (256, 128))

    def body(dg_ref, du_ref, wg_ref, wu_ref, x_ref, nw_ref, dr_ref, dx_ref, dw_ref):
        @pl.when(pl.program_id(0) == 0)
        def _():
            dw_ref[...] = jnp.zeros_like(dw_ref)

        dh = _dot(dg_ref[...], wg_ref[...]) + _dot(du_ref[...], wu_ref[...])
        dx, dw = _rms_bwd_math(x_ref[...], nw_ref[...], dh)
        dx_ref[...] = dr_ref[...] + dx
        dw_ref[...] += dw

    wide = pl.BlockSpec((tm, FF_PAD), lambda i: (i, 0))
    wsp = pl.BlockSpec((FF_PAD, d), lambda i: (0, 0))
    tok = pl.BlockSpec((tm, d), lambda i: (i, 0))
    vec = pl.BlockSpec((1, d), lambda i: (0, 0))
    return pl.pallas_call(
        body, name="ffn_in_dx", grid=(t // tm,), in_specs=[wide, wide, wsp, wsp, tok, vec, tok], out_specs=[tok, vec],
        out_shape=[jax.ShapeDtypeStruct((t, d), F32), jax.ShapeDtypeStruct((1, d), F32)],
        compiler_params=_params(("arbitrary",)),
    )(dg, du, wg_t, wu_t, x, nw, dxo)


def _adamw(w, g, m, v):
    r, c = w.shape
    tr = _tile(r, (512, 256, 128, 64, 32, 16, 8))
    c1 = 1.0 - ADAM_B1 ** ADAM_STEP
    c2 = 1.0 - ADAM_B2 ** ADAM_STEP

    def body(w_ref, g_ref, m_ref, v_ref, d_ref, nm_ref, nv_ref):
        gv = g_ref[...]
        nm = ADAM_B1 * m_ref[...] + (1.0 - ADAM_B1) * gv
        nv = ADAM_B2 * v_ref[...] + (1.0 - ADAM_B2) * (gv * gv)
        d_ref[...] = -ADAM_LR * ((nm / c1) / (jnp.sqrt(nv / c2) + ADAM_EPS) + ADAM_WD * w_ref[...])
        nm_ref[...] = nm
        nv_ref[...] = nv

    blk = pl.BlockSpec((tr, c), lambda i: (i, 0))
    return pl.pallas_call(
        body, name="adamw", grid=(r // tr,), in_specs=[blk] * 4, out_specs=[blk] * 3,
        out_shape=[jax.ShapeDtypeStruct((r, c), F32)] * 3, compiler_params=_params(("parallel",)),
    )(w, g, m, v)


N_KINDS = 3


def _ffn_slot(which, kind):
    return which * N_KINDS + kind


def _ffn_fwd(x, nw, w_ffn, which):
    wg, wu, wd = (w_ffn[_ffn_slot(which, kind)] for kind in range(3))
    h, dadg, dadu, a = _ffn_in(x, nw, wg, wu)
    return _mm(a, wd, name="ffn_out", res=x, scale=0.5), (x, h, dadg, dadu, a)


def _ffn_bwd(dxo, res, nw, w_ffn, which, after=None):
    x, h, dadg, dadu, a = res
    wg, wu, wd = (w_ffn[_ffn_slot(which, kind)] for kind in range(3))
    dg, du = _ffn_out_dx(dxo, wd, dadg, dadu, after)
    g_ffn = lax.empty((N_KINDS,) + wd.shape, BF16)
    g_ffn = _mm_tn(a, dxo, name="ffn_out_dw", scale=0.5, into=g_ffn, slot=2)
    g_ffn = _mm_tn(dg, h, name="ffn_in_dw", into=g_ffn, slot=0)
    g_ffn = _mm_tn(du, h, name="ffn_in_dw", into=g_ffn, slot=1)
    dx, dnw = _ffn_in_dx(dg, du, wg, wu, x, nw, dxo)
    return dx, dnw, g_ffn


def _hg_lower_bounds(logits):
    p = jax.nn.softmax(logits, axis=0)
    return jnp.cumsum(p, axis=0) - p[0]


def _even_fwd(x1, p, nb, l):
    t = nb * l
    h = _rms_fwd(x1, p["mix_norm"])
    proj = _mm(h, p["w_in_t"], name="ev_in", nt=True)
    proj3 = proj.reshape(nb, l, EV_IN)
    ya, hst = _hgrn2_fwd(proj3, p["lb"], p["hg_nw"])
    u = proj[:, 4 * HG_QK:]
    bu = _mm(u, p["bbd"], name="s5_in", out_dtype=BF16)
    hs = _s5_scan_fwd(bu.reshape(nb, l, 2, S5_ROWS, LANES), p["abar"])
    yc = _mm(hs.reshape(t, 2 * S5_NSTATE), p["ccd"], name="s5_out")
    yb, y0 = _s5_post_fwd(yc, u, p["s5_d"], p["w_glu"])
    ycat = jnp.concatenate([ya.reshape(t, HG_QK), yb], axis=1)
    x2 = _mm(ycat, p["w_out"], name="mix_out", res=x1)
    return x2, (x1, h, proj3, hst, u, hs, y0, ycat)


def _even_bwd(dx2, res, p, nb, l):
    x1, h, proj3, hst, u, hs, y0, ycat = res
    t = nb * l
    dycat = _mm(dx2, p["w_out"], name="mix_out_dx", nt=True)
    dw_out = _mm_tn(ycat, dx2, name="mix_out_dw", out_dtype=BF16)
    dp_hg, dlb, dhg_nw = _hgrn2_bwd(proj3, p["lb"], p["hg_nw"], hst, dycat.reshape(nb, l, D_MODEL))
    dy0, y, dz, du_d, dd = _s5_post_bwd(y0, u, p["s5_d"], p["w_glu"], dycat[:, HG_QK:])
    dw_glu = _mm_tn(y, dz, name="s5_glu_dw", out_dtype=BF16)
    dhs = _mm(dy0, p["ccd"], name="s5_out_dx", nt=True, out_dtype=BF16)
    dccd = _mm_tn(hs.reshape(t, 2 * S5_NSTATE), dy0, name="s5_out_dw")
    g5, dabar = _s5_scan_bwd(dhs.reshape(nb, l, 2, S5_ROWS, LANES), hs, p["abar"])
    g2 = g5.reshape(t, 2 * S5_NSTATE)
    du = _mm(g2, p["bbd"], name="s5_in_dx", nt=True, res=du_d, out_dtype=BF16)
    dbbd = _mm_tn(u, g2, name="s5_in_dw")
    dproj = jnp.concatenate([dp_hg.reshape(t, 4 * HG_QK), du], axis=1)
    dw_in_t = _mm_tn(dproj, h, name="ev_in_dw", out_dtype=BF16)
    dh = _mm(dproj, p["w_in_t"], name="ev_in_dx")
    dx1, dmix = _rms_bwd(x1, p["mix_norm"], dh, dx2)
    small = dict(mix_norm=dmix, lb=dlb.sum(0).reshape(HG_QK), hg_nw=dhg_nw.reshape(HEAD_DIM),
                 abar=dabar.sum(0), bbd=dbbd, ccd=dccd, s5_d=dd.reshape(S5_WIDTH))
    return dx1, dict(w_in_t=dw_in_t, w_glu=dw_glu, w_out=dw_out), small


def _odd_fwd(x1, p, nb, l):
    t = nb * l
    h = _rms_fwd(x1, p["mix_norm"])
    proj3 = _mm(h, p["w_in"], name="od_in").reshape(nb, l, OD_IN_PAD)
    qkv = _gdn_pre_fwd(proj3, p["conv_w"])
    y, st = _gdn_fwd(qkv, proj3, p["a_log"], p["dt_bias"], p["gdn_nw"])
    y2 = y.reshape(t, D_MODEL)
    x2 = _mm(y2, p["w_out"], name="mix_out", res=x1)
    return x2, (x1, h, proj3, qkv, st, y2)


def _odd_bwd(dx2, res, p, nb, l):
    x1, h, proj3, qkv, st, y2 = res
    t = nb * l
    dy = _mm(dx2, p["w_out"], name="mix_out_dx", nt=True)
    dw_out = _mm_tn(y2, dx2, name="mix_out_dw", out_dtype=BF16)
    dqkv, dgate, dsmall, dalog, ddtb, dnw = _gdn_bwd(
        qkv, proj3, p["a_log"], p["dt_bias"], p["gdn_nw"], st, dy.reshape(nb, l, D_MODEL))
    dx_qkv, dcw = _gdn_pre_bwd(proj3, p["conv_w"], dqkv)
    dproj = jnp.concatenate([dx_qkv, dgate, dsmall], axis=-1).reshape(t, OD_IN_PAD)
    dw_in = _mm_tn(h, dproj, name="od_in_dw", out_dtype=BF16)[:, :OD_IN]
    dh = _mm(dproj, p["w_in"], name="od_in_dx", nt=True)
    dx1, dmix = _rms_bwd(x1, p["mix_norm"], dh, dx2)
    small = dict(mix_norm=dmix, a_log=dalog.sum(0).reshape(GDN_HEADS), dt_bias=ddtb.sum(0).reshape(GDN_HEADS),
                 gdn_nw=dnw.reshape(HEAD_DIM))
    return dx1, dict(w_in=dw_in, conv_w=dcw.sum(0), w_out=dw_out), small


def _local_step(x, target, layer_weights, small, layer_done):
    nb, l, _ = x.shape
    t = nb * l
    lbs, lbs_vjp = jax.vjp(_hg_lower_bounds, small["hg_lb_logits"])
    row = lambda v: v.reshape(1, -1)
    s5_vjps = {}

    def layer_params(layer, big):
        j = layer // 2
        p = dict(mix_norm=row(small["mix_norm"][layer]), w_out=big["w_out"])
        if layer % 2 == 0:
            (abar, bbd, ccd), s5_vjps[j] = jax.vjp(
                _s5_params, small["s5_a_re"][j], small["s5_a_im"][j], small["s5_b_re"][j], small["s5_b_im"][j],
                small["s5_c_re"][j], small["s5_c_im"][j], small["s5_log_dt"][j])
            p.update(w_in_t=big["w_in_t"], w_glu=big["w_glu"],
                     lb=lbs[j].reshape(HG_HEADS, 1, HEAD_DIM), hg_nw=row(small["hg_norm_w"][j]),
                     abar=abar, bbd=bbd.astype(BF16), ccd=ccd.astype(BF16), s5_d=row(small["s5_d"][j]))
        else:
            p.update(w_in=big["w_in"], conv_w=big["conv_w"],
                     a_log=small["gdn_a_log"][j].reshape(GDN_HEADS, 1, 1),
                     dt_bias=small["gdn_dt_bias"][j].reshape(GDN_HEADS, 1, 1), gdn_nw=row(small["gdn_norm_w"][j]))
        return p

    xs = x.reshape(t, D_MODEL)
    saved, layers, ffn_w = [], [], []
    for layer in range(DEPTH):
        big = layer_weights(layer, xs)
        p = layer_params(layer, big)
        xs, r1 = _ffn_fwd(xs, row(small["ffn1_norm"][layer]), big["ffn"], 0)
        xs, r2 = (_even_fwd if layer % 2 == 0 else _odd_fwd)(xs, p, nb, l)
        xs, r3 = _ffn_fwd(xs, row(small["ffn2_norm"][layer]), big["ffn"], 1)
        saved.append((r1, r2, r3))
        layers.append(p)
        ffn_w.append(big["ffn"])
    loss, dx, dfinal = _loss_head(xs, row(small["final_norm"]), target.reshape(t, D_MODEL))

    gs = {k: [None] * DEPTH for k in ("ffn1_norm", "mix_norm", "ffn2_norm")}
    gs.update({k: [None] * 2 for k in ("hg_norm_w", "s5_a_re", "s5_a_im", "s5_b_re", "s5_b_im", "s5_c_re", "s5_c_im",
                                       "s5_d", "s5_log_dt", "gdn_a_log", "gdn_dt_bias", "gdn_norm_w")})
    dlbs = [None] * 2
    token = None
    for layer in reversed(range(DEPTH)):
        j = layer // 2
        p = layers[layer]
        r1, r2, r3 = saved[layer]
        dx, dnw, g_ffn = _ffn_bwd(dx, r3, row(small["ffn2_norm"][layer]), ffn_w[layer], 1, token)
        gs["ffn2_norm"][layer] = dnw[0]
        if layer % 2 == 0:
            dx, gw, sm = _even_bwd(dx, r2, p, nb, l)
            dlbs[j] = sm["lb"]
            gs["hg_norm_w"][j] = sm["hg_nw"]
            gs["s5_d"][j] = sm["s5_d"]
            (gs["s5_a_re"][j], gs["s5_a_im"][j], gs["s5_b_re"][j], gs["s5_b_im"][j], gs["s5_c_re"][j],
             gs["s5_c_im"][j], gs["s5_log_dt"][j]) = s5_vjps[j]((sm["abar"], sm["bbd"], sm["ccd"]))
        else:
            dx, gw, sm = _odd_bwd(dx, r2, p, nb, l)
            gs["gdn_a_log"][j], gs["gdn_dt_bias"][j], gs["gdn_norm_w"][j] = sm["a_log"], sm["dt_bias"], sm["gdn_nw"]
        gs["mix_norm"][layer] = sm["mix_norm"][0]
        token = layer_done(layer, 0, dict(gw, ffn=g_ffn))
        dx, dnw, g_ffn = _ffn_bwd(dx, r1, row(small["ffn1_norm"][layer]), ffn_w[layer], 0, token)
        gs["ffn1_norm"][layer] = dnw[0]
        token = layer_done(layer, 1, dict(ffn=g_ffn))
    gsmall = {k: jnp.stack(v) for k, v in gs.items()}
    gsmall["hg_lb_logits"] = lbs_vjp(jnp.stack(dlbs))[0]
    gsmall["final_norm"] = dfinal[0]
    return loss[0, 0], dx.reshape(nb, l, D_MODEL), gsmall


def _here():
    return lax.axis_index("x"), lax.axis_index("y"), lax.axis_index("c")


def _other_chips(x, y):
    return [(1 - x, y), (x, 1 - y), (1 - x, 1 - y)]


def _all_gather(blocks, name):
    n = len(blocks)

    def body(*refs):
        x_refs, o_refs = refs[:n], refs[n:2 * n]
        send_sems, recv_sems, local_sems = refs[2 * n:]
        x, y, cc = _here()
        me, sibling = (x, y, cc), (x, y, 1 - cc)
        chips = _other_chips(x, y)

        def win(i, p):
            return o_refs[i].at[:, 4 * p[0] + 2 * p[1] + p[2]]

        def copy(i, k, blk, to, src=None):
            return pltpu.make_async_remote_copy(
                src_ref=win(i, blk) if src is None else src, dst_ref=win(i, blk),
                send_sem=send_sems.at[i, k], recv_sem=recv_sems.at[i, k], device_id=to, device_id_type=MESH)

        mine = [pltpu.make_async_copy(x_refs[i], win(i, me), local_sems.at[i]) for i in range(n)]
        for cp in mine:
            cp.start()
        first = []
        for i in range(n):
            first.append(copy(i, 0, me, sibling, src=x_refs[i]))
            first += [copy(i, 1 + j, me, (*chip, cc), src=x_refs[i]) for j, chip in enumerate(chips)]
        for cp in first:
            cp.start()
        passed = []
        for j, chip in enumerate(chips):
            for i in range(n):
                copy(i, 1 + j, (*chip, cc), me).wait_recv()
                fwd = copy(i, 4 + j, (*chip, cc), sibling)
                fwd.start()
                passed.append(fwd)
        for i in range(n):
            copy(i, 0, sibling, me).wait_recv()
            for j, chip in enumerate(chips):
                copy(i, 4 + j, (*chip, 1 - cc), me).wait_recv()
        for cp in first + passed:
            cp.wait_send()
        for cp in mine:
            cp.wait()

    hbm = pl.BlockSpec(memory_space=pl.ANY)
    return pl.pallas_call(
        body, name=name, in_specs=[hbm] * n, out_specs=[hbm] * n,
        out_shape=[jax.ShapeDtypeStruct((b.shape[0], N_DEV) + b.shape[1:], b.dtype) for b in blocks],
        scratch_shapes=[pltpu.SemaphoreType.DMA((n, 7)), pltpu.SemaphoreType.DMA((n, 7)), pltpu.SemaphoreType.DMA((n,))],
    )(*blocks)


_HBM = pl.BlockSpec(memory_space=pltpu.HBM)
_SEM = pl.BlockSpec(memory_space=pltpu.SEMAPHORE)
_DATAFLOW = pltpu.SideEffectType.DATAFLOW_SIDE_EFFECTING


def _peer(k):
    x, y, c = _here()
    return (x ^ ((k >> 2) & 1), y ^ ((k >> 1) & 1), c ^ (k & 1))


def _device_of(p):
    return 4 * p[0] + 2 * p[1] + p[2]


_OTHER_CHIPS = (4, 2, 6)


def _exchange_start(srcs, lands, copies, name, after=None):
    n = len(srcs)
    n_in = 2 * n + (after is not None)

    def body(*refs):
        s_refs, l_refs, sems = refs[:n], refs[n:2 * n], refs[n_in:n_in + 2 * n]
        for i in range(n):
            for slot, k in enumerate(_OTHER_CHIPS):
                src, dst = copies(s_refs[i], l_refs[i], k, slot)
                pltpu.make_async_remote_copy(
                    src_ref=src, dst_ref=dst, send_sem=sems[2 * i], recv_sem=sems[2 * i + 1],
                    device_id=_peer(k), device_id_type=MESH).start()
        refs[-1][...] = jnp.zeros_like(refs[-1])

    both = list(srcs) + list(lands)
    outs = pl.pallas_call(
        body, name=name,
        out_shape=tuple([pltpu.SemaphoreType.DMA(())] * (2 * n)) + tuple(pltpu.HBM(a.shape, a.dtype) for a in both)
        + (jax.ShapeDtypeStruct((8, LANES), F32),),
        in_specs=[_HBM] * (2 * n) + [pl.BlockSpec(memory_space=pl.ANY)] * (after is not None),
        out_specs=tuple([_SEM] * (2 * n)) + tuple([_HBM] * (2 * n)) + (pl.BlockSpec(memory_space=pltpu.VMEM),),
        input_output_aliases={i: 2 * n + i for i in range(2 * n)},
        compiler_params=pltpu.CompilerParams(has_side_effects=_DATAFLOW),
    )(*[pltpu.with_memory_space_constraint(a, pltpu.HBM) for a in both], *([after] if after is not None else []))
    return list(outs[:2 * n]), list(outs[2 * n:3 * n]), list(outs[3 * n:4 * n]), outs[4 * n]


def _exchange_wait(handle, three, after, name):
    sems, srcs, lands, _ = handle
    n = len(srcs)

    def body(*refs):
        l_refs, sem_refs = refs[n:2 * n], refs[2 * n:4 * n]
        for i in range(n):
            all_three = three(l_refs[i])
            cp = pltpu.make_async_remote_copy(
                src_ref=all_three, dst_ref=all_three, send_sem=sem_refs[2 * i], recv_sem=sem_refs[2 * i + 1],
                device_id=_peer(_OTHER_CHIPS[0]), device_id_type=MESH)
            cp.wait_send()
            cp.wait_recv()

    both = list(srcs) + list(lands)
    outs = pl.pallas_call(
        body, name=name, out_shape=tuple(pltpu.HBM(a.shape, a.dtype) for a in both),
        in_specs=[_HBM] * (2 * n) + [_SEM] * (2 * n) + [pl.BlockSpec(memory_space=pl.ANY)],
        out_specs=tuple([_HBM] * (2 * n)), input_output_aliases={i: i for i in range(2 * n)},
        compiler_params=pltpu.CompilerParams(has_side_effects=_DATAFLOW),
    )(*both, *sems, after)
    return list(outs[:n]), list(outs[n:])


def _gather_copies(src, land, k, slot):
    del k, slot
    return src, land.at[:, _device_of(_here())]


def _gather_three(land):
    return land.at[:, pl.ds(0, len(_OTHER_CHIPS))]


def _scatter_copies(src, land, k, slot):
    p = _peer(k)
    return src.at[2 * p[0] + p[1]], land.at[slot]


def _scatter_three(land):
    return land


def _core_forward(lands):
    n = len(lands)

    def body(*refs):
        l_refs = refs[:n]
        send_sems, recv_sems = refs[2 * n:]
        x, y, cc = _here()
        mine = [(x, y)] + [_peer(k)[:2] for k in _OTHER_CHIPS]
        cps = []
        for i in range(n):
            for j, (px, py) in enumerate(mine):
                give = l_refs[i].at[:, _device_of((px, py, cc))]
                take = l_refs[i].at[:, _device_of((px, py, 1 - cc))]
                cps.append((pltpu.make_async_remote_copy(
                    src_ref=give, dst_ref=give, send_sem=send_sems.at[i, j], recv_sem=recv_sems.at[i, j],
                    device_id=(x, y, 1 - cc), device_id_type=MESH), take))
        for cp, _ in cps:
            cp.start()
        for i in range(n):
            for j, (cp, take) in enumerate(cps[4 * i:4 * i + 4]):
                cp.wait_send()
                pltpu.make_async_remote_copy(
                    src_ref=take, dst_ref=take, send_sem=send_sems.at[i, j], recv_sem=recv_sems.at[i, j],
                    device_id=(x, y, 1 - cc), device_id_type=MESH).wait_recv()

    hbm = pl.BlockSpec(memory_space=pl.ANY)
    return pl.pallas_call(
        body, name="gather_core_forward", in_specs=[hbm] * n, out_specs=[hbm] * n,
        out_shape=[jax.ShapeDtypeStruct(a.shape, a.dtype) for a in lands],
        input_output_aliases={i: i for i in range(n)},
        scratch_shapes=[pltpu.SemaphoreType.DMA((n, 4)), pltpu.SemaphoreType.DMA((n, 4))],
    )(*lands)


def _rs_core_exchange(gs):
    n = len(gs)

    def body(*refs):
        g_refs, o_refs = refs[:n], refs[n:2 * n]
        send_sems, recv_sems = refs[2 * n:]
        x, y, cc = _here()
        cps = [pltpu.make_async_remote_copy(
            src_ref=g_refs[i].at[:, 2 * j + (1 - cc)], dst_ref=o_refs[i].at[j],
            send_sem=send_sems.at[i, j], recv_sem=recv_sems.at[i, j], device_id=(x, y, 1 - cc), device_id_type=MESH)
            for i in range(n) for j in range(4)]
        for cp in cps:
            cp.start()
        for cp in cps:
            cp.wait()

    hbm = pl.BlockSpec(memory_space=pl.ANY)
    return pl.pallas_call(
        body, name="rs_core_exchange", in_specs=[hbm] * n, out_specs=[hbm] * n,
        out_shape=[jax.ShapeDtypeStruct((4, g.shape[0]) + g.shape[2:], g.dtype) for g in gs],
        scratch_shapes=[pltpu.SemaphoreType.DMA((n, 4)), pltpu.SemaphoreType.DMA((n, 4))],
    )(*gs)


def _row_tile(w):
    return w if w <= 512 else _tile(w, (512, 256, 128))


def _rs_chip_sum(g, r1):
    a, _, w, c = g.shape
    tr = _row_tile(w)

    def body(g_ref, r_ref, o_ref):
        o_ref[...] = (g_ref[...].astype(F32) + r_ref[...].astype(F32)).astype(o_ref.dtype)

    blk = pl.BlockSpec((None, None, tr, c), lambda j, aa, i: (j, aa, i, 0))
    return pl.pallas_call(
        body, name="rs_chip_sum", grid=(4, a, w // tr),
        in_specs=[pl.BlockSpec((None, None, tr, c), lambda j, aa, i: (aa, 2 * j + lax.axis_index("c"), i, 0)), blk],
        out_specs=blk, out_shape=jax.ShapeDtypeStruct((4, a, w, c), g.dtype),
        compiler_params=_params(("parallel", "parallel", "parallel")),
    )(g, r1)


def _rs_final_sum(g, r1, r2):
    a, _, w, c = g.shape
    tr = _row_tile(w)

    def body(g_ref, r1_ref, r2_ref, o_ref):
        s = g_ref[...].astype(F32) + r1_ref[...].astype(F32)
        for k in range(3):
            s = s + r2_ref[k].astype(F32)
        o_ref[...] = s

    def device():
        return 4 * lax.axis_index("x") + 2 * lax.axis_index("y") + lax.axis_index("c")

    def chip():
        return 2 * lax.axis_index("x") + lax.axis_index("y")

    return pl.pallas_call(
        body, name="rs_final_sum", grid=(a, w // tr),
        in_specs=[pl.BlockSpec((None, None, tr, c), lambda aa, i: (aa, device(), i, 0)),
                  pl.BlockSpec((None, None, tr, c), lambda aa, i: (chip(), aa, i, 0)),
                  pl.BlockSpec((3, None, tr, c), lambda aa, i: (0, aa, i, 0))],
        out_specs=pl.BlockSpec((None, tr, c), lambda aa, i: (aa, i, 0)),
        out_shape=jax.ShapeDtypeStruct((a, w, c), F32),
        compiler_params=_params(("parallel", "parallel")),
    )(g, r1, r2)


def _sum_slots(g):
    _, n, r, c = g.shape

    def body(g_ref, o_ref):
        s = g_ref[0, 0]
        for k in range(1, n):
            s = s + g_ref[0, k]
        o_ref[...] = s

    return pl.pallas_call(
        body, name="sum_slots", grid=(1,),
        in_specs=[pl.BlockSpec((1, n, r, c), lambda i: (0, 0, 0, 0))], out_specs=pl.BlockSpec((r, c), lambda i: (0, 0)),
        out_shape=jax.ShapeDtypeStruct((r, c), F32), compiler_params=_params(("arbitrary",)),
    )(g)


_WEIGHTS = ['ffn1_norm', 'ffn1_w_gate', 'ffn1_w_up', 'ffn1_w_down', 'mix_norm', 'ffn2_norm', 'ffn2_w_gate', 'ffn2_w_up',
            'ffn2_w_down', 'ev_w_in', 'hg_lb_logits', 'hg_norm_w', 's5_a_re', 's5_a_im', 's5_b_re', 's5_b_im', 's5_c_re',
            's5_c_im', 's5_d', 's5_log_dt', 's5_w_glu', 'ev_w_out', 'od_w_in', 'gdn_conv_w', 'gdn_a_log', 'gdn_dt_bias',
            'gdn_norm_w', 'od_w_out', 'final_norm']
_SHARDED = ['ffn1_w_gate', 'ffn1_w_up', 'ffn1_w_down', 'ffn2_w_gate', 'ffn2_w_up', 'ffn2_w_down', 'ev_w_in', 's5_w_glu',
            'ev_w_out', 'od_w_in', 'gdn_conv_w', 'od_w_out']
_REPLICATED = [n for n in _WEIGHTS if n not in _SHARDED]
_SMALL_COLS = 1024


def _pad_rows(a, rows):
    return jnp.pad(a, ((0, rows - a.shape[0]), (0, 0)))


def _layer_blocks(w, layer):
    j = layer // 2
    ffn = []
    for f in ('ffn1', 'ffn2'):
        ffn += [_pad_rows(w[f + '_w_gate'][layer].T, FF_SHARD_PAD), _pad_rows(w[f + '_w_up'][layer].T, FF_SHARD_PAD),
                _pad_rows(w[f + '_w_down'][layer], FF_SHARD_PAD)]
    ffn = jnp.stack(ffn).astype(BF16)
    if layer % 2 == 0:
        return [ffn, w['ev_w_in'][j].T.astype(BF16)[None], w['s5_w_glu'][j].astype(BF16)[None],
                w['ev_w_out'][j].astype(BF16)[None]]
    return [ffn, w['od_w_in'][j].astype(BF16)[None], w['gdn_conv_w'][j][None], w['od_w_out'][j].astype(BF16)[None]]


def _layer_whole(gathered, layer):
    whole = lambda g: g.reshape(g.shape[0], N_DEV * g.shape[2], g.shape[3])
    ffn, a1, a2, w_out = gathered
    if layer % 2 == 0:
        return dict(ffn=whole(ffn), w_in_t=whole(a1)[0], w_glu=whole(a2)[0], w_out=whole(w_out)[0])
    w_in = jnp.moveaxis(a1[0], 0, 1).reshape(D_MODEL, OD_IN)
    conv = jnp.moveaxis(a2[0], 0, 1).reshape(CONV_W, GDN_QKV)
    return dict(ffn=whole(ffn), w_in=jnp.pad(w_in, ((0, 0), (0, OD_IN_PAD - OD_IN))), conv_w=conv, w_out=whole(w_out)[0])


def _layer_grad_windows(g, layer):
    windows = lambda a, w: a.reshape(-1, N_DEV, w, a.shape[-1])
    ffn = windows(g['ffn'], FF_SHARD_PAD)
    if 'w_out' not in g:
        return [ffn]
    w_out = windows(g['w_out'], D_MODEL // N_DEV)
    if layer % 2 == 0:
        return [ffn, windows(g['w_in_t'], EV_IN // N_DEV), windows(g['w_glu'], S5_WIDTH // N_DEV), w_out]
    w_in = jnp.moveaxis(g['w_in'].reshape(D_MODEL, N_DEV, OD_IN // N_DEV), 1, 0)[None]
    conv = jnp.moveaxis(g['conv_w'].reshape(CONV_W, N_DEV, GDN_QKV // N_DEV), 1, 0)[None]
    return [ffn, w_in, conv, w_out]


def _flat_small(src, prefix=''):
    flat = jnp.concatenate([src[prefix + n].reshape(-1) for n in _REPLICATED])
    rows = -(-(flat.shape[0] + 1) // (_SMALL_COLS * 8)) * 8
    return jnp.pad(flat, (0, rows * _SMALL_COLS - flat.shape[0])).reshape(rows, _SMALL_COLS)


def _split_small(flat2d, like):
    flat, out, off = flat2d.reshape(-1), {}, 0
    for n in _REPLICATED:
        size = math.prod(like[n].shape)
        out[n] = flat[off:off + size].reshape(like[n].shape)
        off += size
    return out, flat[off]


def _all_reduce_small(gsmall, loss):
    flat = _flat_small(gsmall)
    n_used = sum(math.prod(gsmall[n].shape) for n in _REPLICATED)
    flat = flat.reshape(-1).at[n_used].set(loss).reshape(flat.shape)
    (gathered,) = _all_gather([flat[None]], "gather_small")
    return _split_small(_sum_slots(gathered), gsmall)


def _as_2d(a):
    return a.reshape(-1, a.shape[-1])


def kernel(x, ffn1_norm, ffn1_w_gate, ffn1_w_up, ffn1_w_down, mix_norm, ffn2_norm, ffn2_w_gate, ffn2_w_up, ffn2_w_down, ev_w_in, hg_lb_logits, hg_norm_w, s5_a_re, s5_a_im, s5_b_re, s5_b_im, s5_c_re, s5_c_im, s5_d, s5_log_dt, s5_w_glu, ev_w_out, od_w_in, gdn_conv_w, gdn_a_log, gdn_dt_bias, gdn_norm_w, od_w_out, final_norm, loss_target, m_ffn1_norm, m_ffn1_w_gate, m_ffn1_w_up, m_ffn1_w_down, m_mix_norm, m_ffn2_norm, m_ffn2_w_gate, m_ffn2_w_up, m_ffn2_w_down, m_ev_w_in, m_hg_lb_logits, m_hg_norm_w, m_s5_a_re, m_s5_a_im, m_s5_b_re, m_s5_b_im, m_s5_c_re, m_s5_c_im, m_s5_d, m_s5_log_dt, m_s5_w_glu, m_ev_w_out, m_od_w_in, m_gdn_conv_w, m_gdn_a_log, m_gdn_dt_bias, m_gdn_norm_w, m_od_w_out, m_final_norm, v_ffn1_norm, v_ffn1_w_gate, v_ffn1_w_up, v_ffn1_w_down, v_mix_norm, v_ffn2_norm, v_ffn2_w_gate, v_ffn2_w_up, v_ffn2_w_down, v_ev_w_in, v_hg_lb_logits, v_hg_norm_w, v_s5_a_re, v_s5_a_im, v_s5_b_re, v_s5_b_im, v_s5_c_re, v_s5_c_im, v_s5_d, v_s5_log_dt, v_s5_w_glu, v_ev_w_out, v_od_w_in, v_gdn_conv_w, v_gdn_a_log, v_gdn_dt_bias, v_gdn_norm_w, v_od_w_out, v_final_norm):
    given = dict(locals())
    w = {n: given[n] for n in _WEIGHTS}
    small = {n: w[n] for n in _REPLICATED}
    me = _device_of(_here())

    blocks = [_layer_blocks(w, layer) for layer in range(DEPTH)]
    first = _all_gather(blocks[0], "gather_weights")
    pending = {}

    def start_gather(layer, after=None):
        lands = [lax.dynamic_update_slice(lax.empty((b.shape[0], N_DEV) + b.shape[1:], b.dtype), b[:, None], (0, me, 0, 0))
                 for b in blocks[layer]]
        pending[layer] = _exchange_start(blocks[layer], lands, _gather_copies, "gather_start_%d" % layer, after)

    start_gather(1)

    def layer_weights(layer, after):
        if layer == 0:
            return _layer_whole(first, 0)
        got = _exchange_wait(pending[layer], _gather_three, after, "gather_wait_%d" % layer)[1]
        if layer + 1 < DEPTH:
            start_gather(layer + 1, after)
        return _layer_whole(_core_forward(got), layer)

    sent = {}

    def layer_done(layer, part, grads):
        gs = _layer_grad_windows(grads, layer)
        from_sibling = _rs_core_exchange(gs)
        chip = [_rs_chip_sum(g, r) for g, r in zip(gs, from_sibling)]
        lands = [lax.empty((len(_OTHER_CHIPS),) + p.shape[1:], p.dtype) for p in chip]
        handle = _exchange_start(chip, lands, _scatter_copies, "scatter_start_%d_%d" % (layer, part))
        sent[layer, part] = (gs, from_sibling, handle)
        return handle[3]

    loss_part, grad_x, gsmall = _local_step(given['x'], given['loss_target'], layer_weights, small, layer_done)
    gsum, loss = _all_reduce_small(gsmall, loss_part)
    summed = {}
    for layer in reversed(range(DEPTH)):
        for part in range(2):
            gs, from_sibling, handle = sent[layer, part]
            _, lands = _exchange_wait(handle, _scatter_three, grad_x, "scatter_wait_%d_%d" % (layer, part))
            summed[layer, part] = [_rs_final_sum(g, r, land) for g, r, land in zip(gs, from_sibling, lands)]

    grads = dict(gsum)
    t_ = lambda a: jnp.swapaxes(a, 1, 2)
    for i, f in enumerate(('ffn1', 'ffn2')):
        ffn = jnp.stack([summed[layer, 1 - i][0][:, :FF_SHARD] for layer in range(DEPTH)], axis=1)
        grads[f + '_w_gate'], grads[f + '_w_up'], grads[f + '_w_down'] = t_(ffn[0]), t_(ffn[1]), ffn[2]
    even, odd = [summed[0, 0], summed[2, 0]], [summed[1, 0], summed[3, 0]]
    grads['ev_w_in'] = jnp.stack([r[1][0].T for r in even])
    grads['s5_w_glu'] = jnp.stack([r[2][0] for r in even])
    grads['ev_w_out'] = jnp.stack([r[3][0] for r in even])
    grads['od_w_in'] = jnp.stack([r[1][0] for r in odd])
    grads['gdn_conv_w'] = jnp.stack([r[2][0] for r in odd])
    grads['od_w_out'] = jnp.stack([r[3][0] for r in odd])

    delta, new_m, new_v = {}, {}, {}
    for n in _SHARDED:
        d, nm, nv = _adamw(_as_2d(w[n]), _as_2d(grads[n]), _as_2d(given['m_' + n]), _as_2d(given['v_' + n]))
        delta[n], new_m[n], new_v[n] = (a.reshape(w[n].shape) for a in (d, nm, nv))
    d, nm, nv = _adamw(_flat_small(w), _flat_small(grads), _flat_small(given, 'm_'), _flat_small(given, 'v_'))
    (delta_s, _), (new_m_s, _), (new_v_s, _) = (_split_small(a, small) for a in (d, nm, nv))
    delta.update(delta_s)
    new_m.update(new_m_s)
    new_v.update(new_v_s)
    return (loss, grad_x, *[grads[n] for n in _WEIGHTS], *[delta[n] for n in _WEIGHTS],
            *[new_m[n] for n in _WEIGHTS], *[new_v[n] for n in _WEIGHTS])
```

```python
import math

import jax
import jax.numpy as jnp
from jax import lax
from jax.experimental import pallas as pl
from jax.experimental.pallas import tpu as pltpu

F32 = jnp.float32
BF16 = jnp.bfloat16
HI = lax.Precision.HIGH

D_MODEL = 1024
DEPTH = 4
D_FF = 2816
NORM_EPS = 1e-6
F_MIN = 1e-6
CHUNK = 64
HG_HEADS = 4
HEAD_DIM = 128
HG_QK = HG_HEADS * HEAD_DIM
S5_WIDTH = 512
S5_GROUP = 16
S5_GROUPS = 32
S5_STATE = 64
S5_NSTATE = S5_GROUPS * S5_STATE
EV_IN = 2560
GDN_HEADS = 8
GDN_QKV = 3 * GDN_HEADS * HEAD_DIM
CONV_W = 4
OD_IN = 4112
OD_IN_PAD = 4224
N_DEV = 8
LANES = 128
ADAM_LR, ADAM_B1, ADAM_B2, ADAM_EPS, ADAM_WD, ADAM_STEP = 0.001, 0.9, 0.999, 1e-08, 0.01, 10
VMEM_LIMIT = 56 * 1024 * 1024

MESH = pl.DeviceIdType.MESH


def _params(sem=None, **kw):
    return pltpu.CompilerParams(dimension_semantics=sem, vmem_limit_bytes=VMEM_LIMIT, **kw)


def _tile(n, cands):
    for c in cands:
        if n % c == 0:
            return c
    return n


def _dot(a, b):
    return jnp.dot(a.astype(BF16), b.astype(BF16), preferred_element_type=F32)


def _dot_nt(a, b):
    return lax.dot_general(a.astype(BF16), b.astype(BF16), (((1,), (1,)), ((), ())), preferred_element_type=F32)


def _dot_tn(a, b):
    return lax.dot_general(a.astype(BF16), b.astype(BF16), (((0,), (0,)), ((), ())), preferred_element_type=F32)


def _dot_hi(a, b):
    return jnp.dot(a, b, precision=HI, preferred_element_type=F32)


def _bmm(a, b):
    return jnp.einsum('gmk,gkn->gmn', a.astype(BF16), b.astype(BF16), preferred_element_type=F32)


def _bmm_nt(a, b):
    return jnp.einsum('gmk,gnk->gmn', a.astype(BF16), b.astype(BF16), preferred_element_type=F32)


def _bmm_tn(a, b):
    return jnp.einsum('gkm,gkn->gmn', a.astype(BF16), b.astype(BF16), preferred_element_type=F32)


def _bmm_hi(a, b):
    return jnp.einsum('gmk,gkn->gmn', a, b, precision=HI, preferred_element_type=F32)


_TN_CANDS = (1408, 1280, 1024, 512, 384, 256, 128)
_B_TILE_BYTES = 6 * 1024 * 1024
_TOKEN_TILES = (1024, 512, 256, 128)


def _mm(a, b, *, name, nt=False, out_dtype=F32, res=None, scale=1.0):
    m, k = a.shape
    n = b.shape[0] if nt else b.shape[1]
    tm = _tile(m, _TOKEN_TILES)
    tn = _tile(n, [c for c in _TN_CANDS if c * k * b.dtype.itemsize <= _B_TILE_BYTES])

    def body(*refs):
        if res is None:
            a_ref, b_ref, o_ref = refs
        else:
            a_ref, b_ref, r_ref, o_ref = refs
        acc = (_dot_nt if nt else _dot)(a_ref[...], b_ref[...])
        if scale != 1.0:
            acc = scale * acc
        if res is not None:
            acc = r_ref[...] + acc
        o_ref[...] = acc.astype(out_dtype)

    b_spec = pl.BlockSpec((tn, k), lambda j, i: (j, 0)) if nt else pl.BlockSpec((k, tn), lambda j, i: (0, j))
    in_specs = [pl.BlockSpec((tm, k), lambda j, i: (i, 0)), b_spec]
    args = [a, b]
    if res is not None:
        in_specs.append(pl.BlockSpec((tm, tn), lambda j, i: (i, j)))
        args.append(res)
    return pl.pallas_call(
        body, name=name, grid=(n // tn, m // tm), in_specs=in_specs,
        out_specs=pl.BlockSpec((tm, tn), lambda j, i: (i, j)),
        out_shape=jax.ShapeDtypeStruct((m, n), out_dtype),
        compiler_params=_params(("parallel", "parallel")),
    )(*args)


def _mm_tn(a, b, *, name, scale=1.0, out_dtype=F32, into=None, slot=0):
    t, m = a.shape
    n = b.shape[1]
    tm = _tile(m, (1024, 512, 256, 128))
    tn = _tile(n, _TN_CANDS)
    tk = _tile(t, (2048, 1024, 512, 256, 128))
    nk = t // tk

    def body(*refs):
        a_ref, b_ref = refs[:2]
        o_ref, acc_ref = refs[-2:]
        kk = pl.program_id(2)

        @pl.when(kk == 0)
        def _():
            acc_ref[...] = jnp.zeros_like(acc_ref)

        acc_ref[...] += _dot_tn(a_ref[...], b_ref[...])

        @pl.when(kk == nk - 1)
        def _():
            o_ref[...] = (acc_ref[...] * scale if scale != 1.0 else acc_ref[...]).astype(o_ref.dtype)

    in_specs = [pl.BlockSpec((tk, tm), lambda i, j, kk: (kk, i)), pl.BlockSpec((tk, tn), lambda i, j, kk: (kk, j))]
    args = [a, b]
    if into is None:
        out_spec = pl.BlockSpec((tm, tn), lambda i, j, kk: (i, j))
        out_shape = jax.ShapeDtypeStruct((m, n), out_dtype)
        alias = {}
    else:
        in_specs.append(pl.BlockSpec(memory_space=pl.ANY))
        args.append(into)
        out_spec = pl.BlockSpec((None, tm, tn), lambda i, j, kk: (slot, i, j))
        out_shape = jax.ShapeDtypeStruct(into.shape, into.dtype)
        alias = {2: 0}
    return pl.pallas_call(
        body, name=name, grid=(m // tm, n // tn, nk), in_specs=in_specs, out_specs=out_spec, out_shape=out_shape,
        scratch_shapes=[pltpu.VMEM((tm, tn), F32)], input_output_aliases=alias,
        compiler_params=_params(("parallel", "parallel", "arbitrary")),
    )(*args)


def _sigmoid(x):
    return jax.nn.sigmoid(x)


def _head_norm_gate(o, gate, nw):
    r = lax.rsqrt(jnp.mean(o * o, axis=-1, keepdims=True) + NORM_EPS)
    return o * r * nw * (gate * _sigmoid(gate))


def _hg_chunk(st, ql, fl, iv, gl, lb, nw):
    n_g, c, _ = ql.shape
    q = ql * _sigmoid(ql)
    f = lb + (1.0 - lb) * _sigmoid(fl)
    lf = jnp.log(jnp.maximum(f, F_MIN))
    k = 1.0 - f
    ri = lax.broadcasted_iota(jnp.int32, (c, c), 0)
    ci = lax.broadcasted_iota(jnp.int32, (c, c), 1)
    rl = lax.broadcasted_iota(jnp.int32, (c, LANES), 0)
    each = lambda m2: jnp.broadcast_to(m2[None], (n_g,) + m2.shape)
    b = _bmm_hi(each(jnp.where(ci <= ri, 1.0, 0.0)), lf)
    attn = jnp.where((ri == ci)[None], _bmm_nt(q, k), 0.0)
    sh = 0
    while (1 << sh) < c:
        m = 1 << sh
        ref = ((ri >> (sh + 1)) << (sh + 1)) + (m - 1)
        low_r = ((ri >> sh) & 1) == 1
        low_c = ((ci >> sh) & 1) == 1
        w_low = low_r & (ci > ref) & (ci <= ri)
        w_up = jnp.logical_not(low_r) & (ci > ri) & (ci <= ref)
        w = jnp.where(w_low | w_up, 1.0, 0.0)
        e = jnp.exp(_bmm_hi(each(w), lf))
        low_l = (((rl >> sh) & 1) == 1)[None]
        qs = jnp.where(low_l, q * e, 0.0)
        ks = jnp.where(low_l, 0.0, k * e)
        pair = ((ri >> (sh + 1)) == (ci >> (sh + 1))) & low_r & jnp.logical_not(low_c)
        attn = attn + jnp.where(pair[None], _bmm_nt(qs, ks), 0.0)
        sh += 1
    bl = jnp.sum(lf, axis=1, keepdims=True)
    o = _bmm(attn, iv) + _bmm_nt(q * jnp.exp(b), st)
    st_new = st * jnp.exp(bl) + _bmm_tn(iv, k * jnp.exp(bl - b))
    return _head_norm_gate(o, gl, nw), st_new


def _hg_specs(nb, n_c, rev):
    def cidx(cc):
        return (n_c - 1 - cc) if rev else cc

    def col(width):
        return pl.BlockSpec((nb, CHUNK, width), lambda cc: (0, cidx(cc), 0))

    vec = pl.BlockSpec((HG_HEADS, 1, LANES), lambda cc: (0, 0, 0))
    nw = pl.BlockSpec((1, LANES), lambda cc: (0, 0))
    st = pl.BlockSpec((nb, HG_HEADS, None, LANES, LANES), lambda cc: (0, 0, cidx(cc), 0, 0))
    acc = pl.BlockSpec((nb, HG_HEADS, 1, LANES), lambda cc: (0, 0, 0, 0))
    return col, vec, nw, st, acc


def _head(j):
    return slice(j * LANES, (j + 1) * LANES)


def _seq_heads(ref, nb, n, first=0):
    return jnp.stack([ref[b, :, _head(first + j)] for b in range(nb) for j in range(n)])


def _put_seq_heads(ref, val, nb, n, first=0):
    for b in range(nb):
        for j in range(n):
            ref[b, :, _head(first + j)] = val[b * n + j].astype(ref.dtype)


def _hgrn2_fwd(proj, lb, nw):
    nb, l, _ = proj.shape
    n_c = l // CHUNK
    n_g = nb * HG_HEADS
    col, vec, nws, st, _ = _hg_specs(nb, n_c, False)

    def body(p_ref, lb_ref, nw_ref, y_ref, st_ref, s_scr):
        @pl.when(pl.program_id(0) == 0)
        def _():
            s_scr[...] = jnp.zeros_like(s_scr)

        s_in = s_scr[...]
        st_ref[...] = s_in.reshape(st_ref.shape)
        args = [_seq_heads(p_ref, nb, HG_HEADS, k * HG_HEADS) for k in range(4)]
        y, s_new = _hg_chunk(s_in, *args, jnp.concatenate([lb_ref[...]] * nb), nw_ref[...])
        _put_seq_heads(y_ref, y, nb, HG_HEADS)
        s_scr[...] = s_new

    return pl.pallas_call(
        body, name="hgrn2_fwd", grid=(n_c,),
        in_specs=[col(4 * HG_QK), vec, nws], out_specs=[col(HG_QK), st],
        out_shape=[jax.ShapeDtypeStruct((nb, l, HG_QK), BF16),
                   jax.ShapeDtypeStruct((nb, HG_HEADS, n_c, LANES, LANES), F32)],
        scratch_shapes=[pltpu.VMEM((n_g, LANES, LANES), F32)],
        compiler_params=_params(("arbitrary",)),
    )(proj, lb, nw)


def _hgrn2_bwd(proj, lb, nw, states, dy):
    nb, l, _ = proj.shape
    n_c = l // CHUNK
    n_g = nb * HG_HEADS
    col, vec, nws, st, acc = _hg_specs(nb, n_c, True)

    def body(p_ref, lb_ref, nw_ref, st_ref, dy_ref, dp_ref, dlb_ref, dnw_ref, ds_scr):
        @pl.when(pl.program_id(0) == 0)
        def _():
            ds_scr[...] = jnp.zeros_like(ds_scr)
            dlb_ref[...] = jnp.zeros_like(dlb_ref)
            dnw_ref[...] = jnp.zeros_like(dnw_ref)

        args = [_seq_heads(p_ref, nb, HG_HEADS, k * HG_HEADS) for k in range(4)]
        _, vjp = jax.vjp(_hg_chunk, st_ref[...].reshape(n_g, LANES, LANES), *args,
                         jnp.concatenate([lb_ref[...]] * nb), nw_ref[...])
        ds, dql, dfl, div, dgl, dlb, dnw = vjp((_seq_heads(dy_ref, nb, HG_HEADS).astype(F32), ds_scr[...]))
        ds_scr[...] = ds
        for k, val in enumerate((dql, dfl, div, dgl)):
            _put_seq_heads(dp_ref, val, nb, HG_HEADS, k * HG_HEADS)
        dlb_ref[...] += dlb.reshape(dlb_ref.shape)
        dnw_ref[...] += dnw

    return pl.pallas_call(
        body, name="hgrn2_bwd", grid=(n_c,),
        in_specs=[col(4 * HG_QK), vec, nws, st, col(HG_QK)], out_specs=[col(4 * HG_QK), acc, nws],
        out_shape=[jax.ShapeDtypeStruct((nb, l, 4 * HG_QK), BF16),
                   jax.ShapeDtypeStruct((nb, HG_HEADS, 1, LANES), F32), jax.ShapeDtypeStruct((1, LANES), F32)],
        scratch_shapes=[pltpu.VMEM((n_g, LANES, LANES), F32)],
        compiler_params=_params(("arbitrary",)),
    )(proj, lb, nw, states, dy)


def _shift_down(x, s):
    if s == 0:
        return x
    rows = lax.broadcasted_iota(jnp.int32, x.shape, 0)
    return jnp.where(rows >= s, pltpu.roll(x, s, 0), 0.0)


def _shift_up(x, s):
    if s == 0:
        return x
    n = x.shape[0]
    rows = lax.broadcasted_iota(jnp.int32, x.shape, 0)
    return jnp.where(rows < n - s, pltpu.roll(x, n - s, 0), 0.0)


def _gdn_pre_fwd(proj, conv_w):
    nb, l, _ = proj.shape
    n_blk = GDN_QKV // LANES

    def body(x_ref, w_ref, o_ref):
        j = pl.program_id(1)
        x = x_ref[...]
        w = w_ref[...]
        c = w[3:4] * x
        for t in range(CONV_W - 1):
            c = c + w[t:t + 1] * _shift_down(x, CONV_W - 1 - t)
        s = c * _sigmoid(c)
        r = lax.rsqrt(jnp.sum(s * s, axis=-1, keepdims=True) + NORM_EPS)
        scale = jnp.where(j < GDN_HEADS, HEAD_DIM ** -0.5, 1.0)
        o_ref[...] = jnp.where(j < 2 * GDN_HEADS, s * r * scale, s)

    return pl.pallas_call(
        body, name="gdn_pre_fwd", grid=(nb, n_blk),
        in_specs=[pl.BlockSpec((None, l, LANES), lambda b, j: (b, 0, j)), pl.BlockSpec((CONV_W, LANES), lambda b, j: (0, j))],
        out_specs=pl.BlockSpec((None, l, LANES), lambda b, j: (b, 0, j)),
        out_shape=jax.ShapeDtypeStruct((nb, l, GDN_QKV), F32),
        compiler_params=_params(("parallel", "parallel")),
    )(proj, conv_w)


def _gdn_pre_bwd(proj, conv_w, dout):
    nb, l, _ = proj.shape
    n_blk = GDN_QKV // LANES

    def body(x_ref, w_ref, d_ref, dx_ref, dw_ref):
        j = pl.program_id(1)
        x = x_ref[...]
        w = w_ref[...]
        xs = [_shift_down(x, CONV_W - 1 - t) for t in range(CONV_W)]
        c = w[0:1] * xs[0]
        for t in range(1, CONV_W):
            c = c + w[t:t + 1] * xs[t]
        sg = _sigmoid(c)
        s = c * sg
        d = d_ref[...]
        r = lax.rsqrt(jnp.sum(s * s, axis=-1, keepdims=True) + NORM_EPS)
        scale = jnp.where(j < GDN_HEADS, HEAD_DIM ** -0.5, 1.0)
        ds_norm = scale * r * (d - s * (r * r) * jnp.sum(d * s, axis=-1, keepdims=True))
        ds = jnp.where(j < 2 * GDN_HEADS, ds_norm, d)
        dc = ds * (sg * (1.0 + c * (1.0 - sg)))
        dx = w[3:4] * dc
        for t in range(CONV_W - 1):
            dx = dx + w[t:t + 1] * _shift_up(dc, CONV_W - 1 - t)
        dx_ref[...] = dx.astype(dx_ref.dtype)
        for t in range(CONV_W):
            dw_ref[t:t + 1, :] = jnp.sum(dc * xs[t], axis=0, keepdims=True)

    return pl.pallas_call(
        body, name="gdn_pre_bwd", grid=(nb, n_blk),
        in_specs=[pl.BlockSpec((None, l, LANES), lambda b, j: (b, 0, j)), pl.BlockSpec((CONV_W, LANES), lambda b, j: (0, j)),
                  pl.BlockSpec((None, l, LANES), lambda b, j: (b, 0, j))],
        out_specs=[pl.BlockSpec((None, l, LANES), lambda b, j: (b, 0, j)), pl.BlockSpec((None, CONV_W, LANES), lambda b, j: (b, 0, j))],
        out_shape=[jax.ShapeDtypeStruct((nb, l, GDN_QKV), BF16), jax.ShapeDtypeStruct((nb, CONV_W, GDN_QKV), F32)],
        compiler_params=_params(("parallel", "parallel")),
    )(proj, conv_w, dout)


def _gdn_chunk(st, q, k, v, gate, bl, al, alog, dtb, nw):
    n_g, c, _ = q.shape
    beta = _sigmoid(bl)
    x = al + dtb
    softplus = jnp.maximum(x, 0.0) + jnp.log(1.0 + jnp.exp(-jnp.abs(x)))
    la = -jnp.exp(alog) * softplus
    ri = lax.broadcasted_iota(jnp.int32, (c, c), 0)
    ci = lax.broadcasted_iota(jnp.int32, (c, c), 1)
    lower = (ci <= ri)[None]
    strict = (ci < ri)[None]
    ltri = jnp.broadcast_to(jnp.where(lower, 1.0, 0.0), (n_g, c, c))
    la_l = la + jnp.zeros((1, 1, LANES), F32)
    g = _bmm_hi(ltri, la_l)
    delta = _bmm_hi(ltri, jnp.where(strict, la + jnp.zeros((1, 1, c), F32), 0.0))
    gam = jnp.where(lower, jnp.exp(jnp.where(lower, delta, 0.0)), 0.0)
    kb = k * beta
    vb = v * beta
    m = jnp.where(strict, _bmm_nt(kb, k) * gam, 0.0)
    pw = -m
    t_inv = jnp.where((ri == ci)[None], 1.0, 0.0) + pw
    steps = int(math.log2(c)) - 1
    for _ in range(steps):
        pw = _bmm_hi(pw, pw)
        t_inv = t_inv + _bmm_hi(t_inv, pw)
    eg = jnp.exp(g)
    u = _bmm(t_inv, vb)
    w = _bmm(t_inv, kb * eg)
    attn = jnp.where(lower, _bmm_nt(q, k) * gam, 0.0)
    v_new = u - _bmm_nt(w, st)
    o = _bmm_nt(q * eg, st) + _bmm(attn, v_new)
    g_last = jnp.sum(la_l, axis=1, keepdims=True)
    st_new = st * jnp.exp(g_last) + _bmm_tn(v_new, k * jnp.exp(g_last - g))
    return _head_norm_gate(o, gate, nw), st_new


GDN_WIDE = GDN_HEADS * LANES
GDN_SMALL_BLOCK = (GDN_QKV + GDN_WIDE) // LANES


def _gdn_specs(nb, n_c, rev):
    def cidx(cc):
        return (n_c - 1 - cc) if rev else cc

    def col(width, blk=0):
        return pl.BlockSpec((nb, CHUNK, width), lambda cc: (0, cidx(cc), blk))

    scal = pl.BlockSpec((GDN_HEADS, 1, 1), lambda cc: (0, 0, 0))
    nw = pl.BlockSpec((1, LANES), lambda cc: (0, 0))
    st = pl.BlockSpec((nb, GDN_HEADS, None, LANES, LANES), lambda cc: (0, 0, cidx(cc), 0, 0))
    acc_s = pl.BlockSpec((nb, GDN_HEADS, 1, 1), lambda cc: (0, 0, 0, 0))
    return col, scal, nw, st, acc_s


def _tok_cols(xs, nb, first):
    return jnp.stack([xs[b][:, first + h:first + h + 1] for b in range(nb) for h in range(GDN_HEADS)])


def _gdn_inputs(qkv_ref, p_gate_ref, p_small_ref, alog_ref, dtb_ref, nw_ref, nb):
    q, k, v = (_seq_heads(qkv_ref, nb, GDN_HEADS, i * GDN_HEADS) for i in range(3))
    xs = p_small_ref[...]
    return (q, k, v, _seq_heads(p_gate_ref, nb, GDN_HEADS), _tok_cols(xs, nb, 0), _tok_cols(xs, nb, GDN_HEADS),
            jnp.concatenate([alog_ref[...]] * nb), jnp.concatenate([dtb_ref[...]] * nb), nw_ref[...])


def _gdn_fwd(qkv, proj, alog, dtb, nw):
    nb, l, _ = qkv.shape
    n_c = l // CHUNK
    n_g = nb * GDN_HEADS
    col, scal, nws, st, _ = _gdn_specs(nb, n_c, False)

    def body(qkv_ref, pg_ref, ps_ref, alog_ref, dtb_ref, nw_ref, y_ref, st_ref, s_scr):
        @pl.when(pl.program_id(0) == 0)
        def _():
            s_scr[...] = jnp.zeros_like(s_scr)

        s_in = s_scr[...]
        st_ref[...] = s_in.reshape(st_ref.shape)
        y, s_new = _gdn_chunk(s_in, *_gdn_inputs(qkv_ref, pg_ref, ps_ref, alog_ref, dtb_ref, nw_ref, nb))
        _put_seq_heads(y_ref, y, nb, GDN_HEADS)
        s_scr[...] = s_new

    return pl.pallas_call(
        body, name="gdn_fwd", grid=(n_c,),
        in_specs=[col(GDN_QKV), col(GDN_WIDE, GDN_QKV // GDN_WIDE), col(LANES, GDN_SMALL_BLOCK), scal, scal, nws],
        out_specs=[col(GDN_WIDE), st],
        out_shape=[jax.ShapeDtypeStruct((nb, l, GDN_WIDE), BF16),
                   jax.ShapeDtypeStruct((nb, GDN_HEADS, n_c, LANES, LANES), F32)],
        scratch_shapes=[pltpu.VMEM((n_g, LANES, LANES), F32)],
        compiler_params=_params(("arbitrary",)),
    )(qkv, proj, proj, alog, dtb, nw)


def _gdn_bwd(qkv, proj, alog, dtb, nw, states, dy):
    nb, l, _ = qkv.shape
    n_c = l // CHUNK
    n_g = nb * GDN_HEADS
    col, scal, nws, st, acc_s = _gdn_specs(nb, n_c, True)

    def body(qkv_ref, pg_ref, ps_ref, alog_ref, dtb_ref, nw_ref, st_ref, dy_ref,
             dqkv_ref, dg_ref, dsm_ref, dalog_ref, ddtb_ref, dnw_ref, ds_scr):
        @pl.when(pl.program_id(0) == 0)
        def _():
            ds_scr[...] = jnp.zeros_like(ds_scr)
            dalog_ref[...] = jnp.zeros_like(dalog_ref)
            ddtb_ref[...] = jnp.zeros_like(ddtb_ref)
            dnw_ref[...] = jnp.zeros_like(dnw_ref)

        _, vjp = jax.vjp(_gdn_chunk, st_ref[...].reshape(n_g, LANES, LANES),
                         *_gdn_inputs(qkv_ref, pg_ref, ps_ref, alog_ref, dtb_ref, nw_ref, nb))
        ds, dq, dk, dv, dg, dbl, dal, dalog, ddtb, dnw = vjp((_seq_heads(dy_ref, nb, GDN_HEADS).astype(F32), ds_scr[...]))
        ds_scr[...] = ds
        for i, val in enumerate((dq, dk, dv)):
            _put_seq_heads(dqkv_ref, val, nb, GDN_HEADS, i * GDN_HEADS)
        _put_seq_heads(dg_ref, dg, nb, GDN_HEADS)
        lane = lax.broadcasted_iota(jnp.int32, (CHUNK, LANES), 1)
        for b in range(nb):
            small = jnp.zeros((CHUNK, LANES), F32)
            for h in range(GDN_HEADS):
                small = small + jnp.where(lane == h, dbl[b * GDN_HEADS + h], 0.0)
                small = small + jnp.where(lane == GDN_HEADS + h, dal[b * GDN_HEADS + h], 0.0)
            dsm_ref[b] = small.astype(dsm_ref.dtype)
        dalog_ref[...] += dalog.reshape(dalog_ref.shape)
        ddtb_ref[...] += ddtb.reshape(ddtb_ref.shape)
        dnw_ref[...] += dnw

    h = GDN_HEADS
    return pl.pallas_call(
        body, name="gdn_bwd", grid=(n_c,),
        in_specs=[col(GDN_QKV), col(GDN_WIDE, GDN_QKV // GDN_WIDE), col(LANES, GDN_SMALL_BLOCK), scal, scal, nws, st,
                  col(GDN_WIDE)],
        out_specs=[col(GDN_QKV), col(GDN_WIDE), col(LANES), acc_s, acc_s, nws],
        out_shape=[jax.ShapeDtypeStruct((nb, l, GDN_QKV), F32), jax.ShapeDtypeStruct((nb, l, GDN_WIDE), BF16),
                   jax.ShapeDtypeStruct((nb, l, LANES), BF16),
                   jax.ShapeDtypeStruct((nb, h, 1, 1), F32), jax.ShapeDtypeStruct((nb, h, 1, 1), F32),
                   jax.ShapeDtypeStruct((1, LANES), F32)],
        scratch_shapes=[pltpu.VMEM((n_g, LANES, LANES), F32)],
        compiler_params=_params(("arbitrary",)),
    )(qkv, proj, proj, alog, dtb, nw, states, dy)


S5_ROWS = S5_NSTATE // LANES
S5_TB = 256


def _s5_scan_fwd(bu, abar):
    nb, l = bu.shape[:2]
    tb = min(S5_TB, l)

    def body(bu_ref, a_ref, h_ref, c_scr):
        @pl.when(pl.program_id(1) == 0)
        def _():
            c_scr[...] = jnp.zeros_like(c_scr)

        ar = a_ref[0]
        ai = a_ref[1]

        def step(t, carry):
            hr, hi = carry
            nr = ar * hr - ai * hi + bu_ref[t, 0].astype(F32)
            ni = ar * hi + ai * hr + bu_ref[t, 1].astype(F32)
            h_ref[t, 0] = nr.astype(h_ref.dtype)
            h_ref[t, 1] = ni.astype(h_ref.dtype)
            return nr, ni

        hr, hi = lax.fori_loop(0, tb, step, (c_scr[0], c_scr[1]), unroll=8)
        c_scr[0] = hr
        c_scr[1] = hi

    blk = pl.BlockSpec((None, tb, 2, S5_ROWS, LANES), lambda b, i: (b, i, 0, 0, 0))
    return pl.pallas_call(
        body, name="s5_scan_fwd", grid=(nb, l // tb),
        in_specs=[blk, pl.BlockSpec((2, S5_ROWS, LANES), lambda b, i: (0, 0, 0))],
        out_specs=blk, out_shape=jax.ShapeDtypeStruct(bu.shape, bu.dtype),
        scratch_shapes=[pltpu.VMEM((2, S5_ROWS, LANES), F32)],
        compiler_params=_params(("parallel", "arbitrary")),
    )(bu, abar)


def _s5_scan_bwd(dh, h, abar):
    nb, l = dh.shape[:2]
    tb = min(S5_TB, l)
    n_blk = l // tb

    def body(dh_ref, h_ref, a_ref, g_ref, da_ref, c_scr):
        @pl.when(pl.program_id(1) == 0)
        def _():
            c_scr[...] = jnp.zeros_like(c_scr)
            da_ref[...] = jnp.zeros_like(da_ref)

        ar = a_ref[0]
        ai = a_ref[1]

        def step(s, carry):
            gr, gi, dar, dai = carry
            t = tb - 1 - s
            hr = h_ref[t, 0].astype(F32)
            hi = h_ref[t, 1].astype(F32)
            dar = dar + gr * hr + gi * hi
            dai = dai + gi * hr - gr * hi
            nr = ar * gr + ai * gi + dh_ref[t, 0].astype(F32)
            ni = ar * gi - ai * gr + dh_ref[t, 1].astype(F32)
            g_ref[t, 0] = nr.astype(g_ref.dtype)
            g_ref[t, 1] = ni.astype(g_ref.dtype)
            return nr, ni, dar, dai

        z = jnp.zeros((S5_ROWS, LANES), F32)
        gr, gi, dar, dai = lax.fori_loop(0, tb, step, (c_scr[0], c_scr[1], z, z), unroll=8)
        c_scr[0] = gr
        c_scr[1] = gi
        da_ref[0] += dar
        da_ref[1] += dai

    blk = pl.BlockSpec((None, tb, 2, S5_ROWS, LANES), lambda b, i: (b, n_blk - 1 - i, 0, 0, 0))
    return pl.pallas_call(
        body, name="s5_scan_bwd", grid=(nb, n_blk),
        in_specs=[blk, blk, pl.BlockSpec((2, S5_ROWS, LANES), lambda b, i: (0, 0, 0))],
        out_specs=[blk, pl.BlockSpec((None, 2, S5_ROWS, LANES), lambda b, i: (b, 0, 0, 0))],
        out_shape=[jax.ShapeDtypeStruct(dh.shape, dh.dtype), jax.ShapeDtypeStruct((nb, 2, S5_ROWS, LANES), F32)],
        scratch_shapes=[pltpu.VMEM((2, S5_ROWS, LANES), F32)],
        compiler_params=_params(("parallel", "arbitrary")),
    )(dh, h, abar)


_GELU_C = math.sqrt(2.0 / math.pi)


def _gelu(x):
    return 0.5 * x * (1.0 + jnp.tanh(_GELU_C * (x + 0.044715 * x * x * x)))


def _gelu_grad(x):
    t = jnp.tanh(_GELU_C * (x + 0.044715 * x * x * x))
    return 0.5 * (1.0 + t) + 0.5 * x * (1.0 - t * t) * _GELU_C * (1.0 + 3.0 * 0.044715 * x * x)


def _s5_post_fwd(yc, u, d, w_glu):
    t, w = yc.shape
    tt = _tile(t, (512, 256, 128))

    def body(yc_ref, u_ref, d_ref, w_ref, o_ref, y0_ref):
        y0 = yc_ref[...] + d_ref[...] * u_ref[...]
        y = _gelu(y0)
        z = _dot(y, w_ref[...])
        o_ref[...] = (y * _sigmoid(z)).astype(o_ref.dtype)
        y0_ref[...] = y0

    tok = pl.BlockSpec((tt, w), lambda i: (i, 0))
    return pl.pallas_call(
        body, name="s5_post_fwd", grid=(t // tt,),
        in_specs=[tok, tok, pl.BlockSpec((1, w), lambda i: (0, 0)), pl.BlockSpec((w, w), lambda i: (0, 0))],
        out_specs=[tok, tok],
        out_shape=[jax.ShapeDtypeStruct((t, w), BF16), jax.ShapeDtypeStruct((t, w), F32)],
        compiler_params=_params(("parallel",)),
    )(yc, u, d, w_glu)


def _s5_post_bwd(y0, u, d, w_glu, dout):
    t, w = y0.shape
    tt = _tile(t, (512, 256, 128))

    def body(y0_ref, u_ref, d_ref, w_ref, do_ref, dy0_ref, y_ref, dz_ref, du_ref, dd_ref):
        @pl.when(pl.program_id(0) == 0)
        def _():
            dd_ref[...] = jnp.zeros_like(dd_ref)

        y0 = y0_ref[...]
        y = _gelu(y0)
        s = _sigmoid(_dot(y, w_ref[...]))
        do = do_ref[...]
        dz = do * y * s * (1.0 - s)
        dy = do * s + _dot_nt(dz, w_ref[...])
        dy0 = dy * _gelu_grad(y0)
        dy0_ref[...] = dy0.astype(dy0_ref.dtype)
        y_ref[...] = y.astype(y_ref.dtype)
        dz_ref[...] = dz.astype(dz_ref.dtype)
        du_ref[...] = dy0 * d_ref[...]
        dd_ref[...] += jnp.sum(dy0 * u_ref[...], axis=0, keepdims=True)

    tok = pl.BlockSpec((tt, w), lambda i: (i, 0))
    vec = pl.BlockSpec((1, w), lambda i: (0, 0))
    return pl.pallas_call(
        body, name="s5_post_bwd", grid=(t // tt,),
        in_specs=[tok, tok, vec, pl.BlockSpec((w, w), lambda i: (0, 0)), tok],
        out_specs=[tok, tok, tok, tok, vec],
        out_shape=[jax.ShapeDtypeStruct((t, w), BF16), jax.ShapeDtypeStruct((t, w), BF16),
                   jax.ShapeDtypeStruct((t, w), BF16), jax.ShapeDtypeStruct((t, w), F32),
                   jax.ShapeDtypeStruct((1, w), F32)],
        compiler_params=_params(("arbitrary",)),
    )(y0, u, d, w_glu, dout)


def _s5_params(a_re, a_im, b_re, b_im, c_re, c_im, log_dt):
    dt = jnp.exp(log_dt)[:, None]
    mag = jnp.exp(dt * a_re)
    ang = dt * a_im
    abar_re = mag * jnp.cos(ang)
    abar_im = mag * jnp.sin(ang)
    den = a_re * a_re + a_im * a_im
    zr = abar_re - 1.0
    zi = abar_im
    coef_re = ((zr * a_re + zi * a_im) / den)[..., None]
    coef_im = ((zi * a_re - zr * a_im) / den)[..., None]
    bb_re = coef_re * b_re - coef_im * b_im
    bb_im = coef_re * b_im + coef_im * b_re
    eye = jnp.eye(S5_GROUPS, dtype=F32)

    def dense_in(bb):
        return jnp.einsum('gnp,gh->gphn', bb, eye).reshape(S5_WIDTH, S5_NSTATE)

    def dense_out(cc):
        return jnp.einsum('gpn,gh->gnhp', cc, eye).reshape(S5_NSTATE, S5_WIDTH)

    abar = jnp.stack([abar_re.reshape(S5_ROWS, LANES), abar_im.reshape(S5_ROWS, LANES)])
    bbd = jnp.concatenate([dense_in(bb_re), dense_in(bb_im)], axis=1)
    ccd = jnp.concatenate([dense_out(c_re), dense_out(-c_im)], axis=0)
    return abar, bbd, ccd


def _tok_tile(t):
    return _tile(t, (256, 128))


def _rms_fwd(x, w):
    t, d = x.shape
    tt = _tok_tile(t)

    def body(x_ref, w_ref, o_ref):
        xv = x_ref[...]
        r = lax.rsqrt(jnp.mean(xv * xv, axis=-1, keepdims=True) + NORM_EPS)
        o_ref[...] = (xv * r * w_ref[...]).astype(o_ref.dtype)

    tok = pl.BlockSpec((tt, d), lambda i: (i, 0))
    return pl.pallas_call(
        body, name="rms_fwd", grid=(t // tt,), in_specs=[tok, pl.BlockSpec((1, d), lambda i: (0, 0))],
        out_specs=tok, out_shape=jax.ShapeDtypeStruct((t, d), BF16), compiler_params=_params(("parallel",)),
    )(x, w)


def _rms_bwd_math(xv, wv, dy):
    r = lax.rsqrt(jnp.mean(xv * xv, axis=-1, keepdims=True) + NORM_EPS)
    xh = xv * r
    dxh = dy * wv
    dx = r * (dxh - xh * jnp.mean(dxh * xh, axis=-1, keepdims=True))
    return dx, jnp.sum(dy * xh, axis=0, keepdims=True)


def _rms_bwd(x, w, dy, dres):
    t, d = x.shape
    tt = _tok_tile(t)

    def body(x_ref, w_ref, dy_ref, dr_ref, dx_ref, dw_ref):
        @pl.when(pl.program_id(0) == 0)
        def _():
            dw_ref[...] = jnp.zeros_like(dw_ref)

        dx, dw = _rms_bwd_math(x_ref[...], w_ref[...], dy_ref[...])
        dx_ref[...] = dr_ref[...] + dx
        dw_ref[...] += dw

    tok = pl.BlockSpec((tt, d), lambda i: (i, 0))
    vec = pl.BlockSpec((1, d), lambda i: (0, 0))
    return pl.pallas_call(
        body, name="rms_bwd", grid=(t // tt,), in_specs=[tok, vec, tok, tok], out_specs=[tok, vec],
        out_shape=[jax.ShapeDtypeStruct((t, d), F32), jax.ShapeDtypeStruct((1, d), F32)],
        compiler_params=_params(("arbitrary",)),
    )(x, w, dy, dres)


def _loss_head(x, w, target):
    t, d = x.shape
    tt = _tok_tile(t)

    def body(x_ref, w_ref, t_ref, l_ref, dx_ref, dw_ref):
        @pl.when(pl.program_id(0) == 0)
        def _():
            l_ref[...] = jnp.zeros_like(l_ref)
            dw_ref[...] = jnp.zeros_like(dw_ref)

        xv = x_ref[...]
        wv = w_ref[...]
        r = lax.rsqrt(jnp.mean(xv * xv, axis=-1, keepdims=True) + NORM_EPS)
        err = xv * r * wv - t_ref[...]
        row = jnp.sum(err * err, axis=-1, keepdims=True)
        l_ref[...] += (0.5 / d) * jnp.sum(row, axis=0, keepdims=True)
        dx, dw = _rms_bwd_math(xv, wv, err * (1.0 / d))
        dx_ref[...] = dx
        dw_ref[...] += dw

    tok = pl.BlockSpec((tt, d), lambda i: (i, 0))
    vec = pl.BlockSpec((1, d), lambda i: (0, 0))
    return pl.pallas_call(
        body, name="loss_head", grid=(t // tt,), in_specs=[tok, vec, tok],
        out_specs=[pl.BlockSpec((1, 1), lambda i: (0, 0)), tok, vec],
        out_shape=[jax.ShapeDtypeStruct((1, 1), F32), jax.ShapeDtypeStruct((t, d), F32), jax.ShapeDtypeStruct((1, d), F32)],
        compiler_params=_params(("arbitrary",)),
    )(x, w, target)


FF_PAD = 3072
FF_SHARD = D_FF // N_DEV
FF_SHARD_PAD = FF_PAD // N_DEV


def _ffn_in(x, nw, wg_t, wu_t):
    t, d = x.shape
    tm = _tile(t, _TOKEN_TILES)
    tn = 1024

    def body(x_ref, nw_ref, wg_ref, wu_ref, h_ref, dadg_ref, dadu_ref, a_ref):
        @pl.when(pl.program_id(1) == 0)
        def _():
            xv = x_ref[...]
            r = lax.rsqrt(jnp.mean(xv * xv, axis=-1, keepdims=True) + NORM_EPS)
            h_ref[...] = (xv * r * nw_ref[...]).astype(h_ref.dtype)

        h = h_ref[...]
        g = _dot_nt(h, wg_ref[...])
        u = _dot_nt(h, wu_ref[...])
        s = _sigmoid(g)
        silu = g * s
        dadg_ref[...] = (u * (s + silu * (1.0 - s))).astype(dadg_ref.dtype)
        dadu_ref[...] = silu.astype(dadu_ref.dtype)
        a_ref[...] = (silu * u).astype(a_ref.dtype)

    tok = pl.BlockSpec((tm, d), lambda i, j: (i, 0))
    wsp = pl.BlockSpec((tn, d), lambda i, j: (j, 0))
    wide = pl.BlockSpec((tm, tn), lambda i, j: (i, j))
    return pl.pallas_call(
        body, name="ffn_in", grid=(t // tm, FF_PAD // tn),
        in_specs=[tok, pl.BlockSpec((1, d), lambda i, j: (0, 0)), wsp, wsp], out_specs=[tok, wide, wide, wide],
        out_shape=[jax.ShapeDtypeStruct((t, d), BF16)] + [jax.ShapeDtypeStruct((t, FF_PAD), BF16)] * 3,
        compiler_params=_params(("parallel", "arbitrary")),
    )(x, nw, wg_t, wu_t)


def _ffn_out_dx(dxo, wd, dadg, dadu, after=None):
    t, d = dxo.shape
    tm = _tile(t, _TOKEN_TILES)
    tn = 1024

    def body(dx_ref, wd_ref, dadg_ref, dadu_ref, *rest):
        dg_ref, du_ref = rest[-2:]
        da = 0.5 * _dot_nt(dx_ref[...], wd_ref[...])
        dg_ref[...] = (da * dadg_ref[...].astype(F32)).astype(dg_ref.dtype)
        du_ref[...] = (da * dadu_ref[...].astype(F32)).astype(du_ref.dtype)

    wide = pl.BlockSpec((tm, tn), lambda j, i: (i, j))
    return pl.pallas_call(
        body, name="ffn_out_dx", grid=(FF_PAD // tn, t // tm),
        in_specs=[pl.BlockSpec((tm, d), lambda j, i: (i, 0)), pl.BlockSpec((tn, d), lambda j, i: (j, 0)), wide, wide]
        + [pl.BlockSpec(memory_space=pl.ANY)] * (after is not None),
        out_specs=[wide, wide], out_shape=[jax.ShapeDtypeStruct((t, FF_PAD), BF16)] * 2,
        compiler_params=_params(("parallel", "parallel")),
    )(dxo, wd, dadg, dadu, *([after] if after is not None else []))


def _ffn_in_dx(dg, du, wg_t, wu_t, x, nw, dxo):
    t, d = x.shape
    tm = _tile(t, (256, 128))

    def body(dg_ref, du_ref, wg_ref, wu_ref, x_ref, nw_ref, dr_ref, dx_ref, dw_ref):
        @pl.when(pl.program_id(0) == 0)
        def _():
            dw_ref[...] = jnp.zeros_like(dw_ref)

        dh = _dot(dg_ref[...], wg_ref[...]) + _dot(du_ref[...], wu_ref[...])
        dx, dw = _rms_bwd_math(x_ref[...], nw_ref[...], dh)
        dx_ref[...] = dr_ref[...] + dx
        dw_ref[...] += dw

    wide = pl.BlockSpec((tm, FF_PAD), lambda i: (i, 0))
    wsp = pl.BlockSpec((FF_PAD, d), lambda i: (0, 0))
    tok = pl.BlockSpec((tm, d), lambda i: (i, 0))
    vec = pl.BlockSpec((1, d), lambda i: (0, 0))
    return pl.pallas_call(
        body, name="ffn_in_dx", grid=(t // tm,), in_specs=[wide, wide, wsp, wsp, tok, vec, tok], out_specs=[tok, vec],
        out_shape=[jax.ShapeDtypeStruct((t, d), F32), jax.ShapeDtypeStruct((1, d), F32)],
        compiler_params=_params(("arbitrary",)),
    )(dg, du, wg_t, wu_t, x, nw, dxo)


def _adamw(w, g, m, v):
    r, c = w.shape
    tr = _tile(r, (512, 256, 128, 64, 32, 16, 8))
    c1 = 1.0 - ADAM_B1 ** ADAM_STEP
    c2 = 1.0 - ADAM_B2 ** ADAM_STEP

    def body(w_ref, g_ref, m_ref, v_ref, d_ref, nm_ref, nv_ref):
        gv = g_ref[...]
        nm = ADAM_B1 * m_ref[...] + (1.0 - ADAM_B1) * gv
        nv = ADAM_B2 * v_ref[...] + (1.0 - ADAM_B2) * (gv * gv)
        d_ref[...] = -ADAM_LR * ((nm / c1) / (jnp.sqrt(nv / c2) + ADAM_EPS) + ADAM_WD * w_ref[...])
        nm_ref[...] = nm
        nv_ref[...] = nv

    blk = pl.BlockSpec((tr, c), lambda i: (i, 0))
    return pl.pallas_call(
        body, name="adamw", grid=(r // tr,), in_specs=[blk] * 4, out_specs=[blk] * 3,
        out_shape=[jax.ShapeDtypeStruct((r, c), F32)] * 3, compiler_params=_params(("parallel",)),
    )(w, g, m, v)


N_KINDS = 3


def _ffn_slot(which, kind):
    return which * N_KINDS + kind


def _ffn_fwd(x, nw, w_ffn, which):
    wg, wu, wd = (w_ffn[_ffn_slot(which, kind)] for kind in range(3))
    h, dadg, dadu, a = _ffn_in(x, nw, wg, wu)
    return _mm(a, wd, name="ffn_out", res=x, scale=0.5), (x, h, dadg, dadu, a)


def _ffn_bwd(dxo, res, nw, w_ffn, which, after=None):
    x, h, dadg, dadu, a = res
    wg, wu, wd = (w_ffn[_ffn_slot(which, kind)] for kind in range(3))
    dg, du = _ffn_out_dx(dxo, wd, dadg, dadu, after)
    g_ffn = lax.empty((N_KINDS,) + wd.shape, BF16)
    g_ffn = _mm_tn(a, dxo, name="ffn_out_dw", scale=0.5, into=g_ffn, slot=2)
    g_ffn = _mm_tn(dg, h, name="ffn_in_dw", into=g_ffn, slot=0)
    g_ffn = _mm_tn(du, h, name="ffn_in_dw", into=g_ffn, slot=1)
    dx, dnw = _ffn_in_dx(dg, du, wg, wu, x, nw, dxo)
    return dx, dnw, g_ffn


def _hg_lower_bounds(logits):
    p = jax.nn.softmax(logits, axis=0)
    return jnp.cumsum(p, axis=0) - p[0]


def _even_fwd(x1, p, nb, l):
    t = nb * l
    h = _rms_fwd(x1, p["mix_norm"])
    proj = _mm(h, p["w_in_t"], name="ev_in", nt=True)
    proj3 = proj.reshape(nb, l, EV_IN)
    ya, hst = _hgrn2_fwd(proj3, p["lb"], p["hg_nw"])
    u = proj[:, 4 * HG_QK:]
    bu = _mm(u, p["bbd"], name="s5_in", out_dtype=BF16)
    hs = _s5_scan_fwd(bu.reshape(nb, l, 2, S5_ROWS, LANES), p["abar"])
    yc = _mm(hs.reshape(t, 2 * S5_NSTATE), p["ccd"], name="s5_out")
    yb, y0 = _s5_post_fwd(yc, u, p["s5_d"], p["w_glu"])
    ycat = jnp.concatenate([ya.reshape(t, HG_QK), yb], axis=1)
    x2 = _mm(ycat, p["w_out"], name="mix_out", res=x1)
    return x2, (x1, h, proj3, hst, u, hs, y0, ycat)


def _even_bwd(dx2, res, p, nb, l):
    x1, h, proj3, hst, u, hs, y0, ycat = res
    t = nb * l
    dycat = _mm(dx2, p["w_out"], name="mix_out_dx", nt=True)
    dw_out = _mm_tn(ycat, dx2, name="mix_out_dw", out_dtype=BF16)
    dp_hg, dlb, dhg_nw = _hgrn2_bwd(proj3, p["lb"], p["hg_nw"], hst, dycat.reshape(nb, l, D_MODEL))
    dy0, y, dz, du_d, dd = _s5_post_bwd(y0, u, p["s5_d"], p["w_glu"], dycat[:, HG_QK:])
    dw_glu = _mm_tn(y, dz, name="s5_glu_dw", out_dtype=BF16)
    dhs = _mm(dy0, p["ccd"], name="s5_out_dx", nt=True, out_dtype=BF16)
    dccd = _mm_tn(hs.reshape(t, 2 * S5_NSTATE), dy0, name="s5_out_dw")
    g5, dabar = _s5_scan_bwd(dhs.reshape(nb, l, 2, S5_ROWS, LANES), hs, p["abar"])
    g2 = g5.reshape(t, 2 * S5_NSTATE)
    du = _mm(g2, p["bbd"], name="s5_in_dx", nt=True, res=du_d, out_dtype=BF16)
    dbbd = _mm_tn(u, g2, name="s5_in_dw")
    dproj = jnp.concatenate([dp_hg.reshape(t, 4 * HG_QK), du], axis=1)
    dw_in_t = _mm_tn(dproj, h, name="ev_in_dw", out_dtype=BF16)
    dh = _mm(dproj, p["w_in_t"], name="ev_in_dx")
    dx1, dmix = _rms_bwd(x1, p["mix_norm"], dh, dx2)
    small = dict(mix_norm=dmix, lb=dlb.sum(0).reshape(HG_QK), hg_nw=dhg_nw.reshape(HEAD_DIM),
                 abar=dabar.sum(0), bbd=dbbd, ccd=dccd, s5_d=dd.reshape(S5_WIDTH))
    return dx1, dict(w_in_t=dw_in_t, w_glu=dw_glu, w_out=dw_out), small


def _odd_fwd(x1, p, nb, l):
    t = nb * l
    h = _rms_fwd(x1, p["mix_norm"])
    proj3 = _mm(h, p["w_in"], name="od_in").reshape(nb, l, OD_IN_PAD)
    qkv = _gdn_pre_fwd(proj3, p["conv_w"])
    y, st = _gdn_fwd(qkv, proj3, p["a_log"], p["dt_bias"], p["gdn_nw"])
    y2 = y.reshape(t, D_MODEL)
    x2 = _mm(y2, p["w_out"], name="mix_out", res=x1)
    return x2, (x1, h, proj3, qkv, st, y2)


def _odd_bwd(dx2, res, p, nb, l):
    x1, h, proj3, qkv, st, y2 = res
    t = nb * l
    dy = _mm(dx2, p["w_out"], name="mix_out_dx", nt=True)
    dw_out = _mm_tn(y2, dx2, name="mix_out_dw", out_dtype=BF16)
    dqkv, dgate, dsmall, dalog, ddtb, dnw = _gdn_bwd(
        qkv, proj3, p["a_log"], p["dt_bias"], p["gdn_nw"], st, dy.reshape(nb, l, D_MODEL))
    dx_qkv, dcw = _gdn_pre_bwd(proj3, p["conv_w"], dqkv)
    dproj = jnp.concatenate([dx_qkv, dgate, dsmall], axis=-1).reshape(t, OD_IN_PAD)
    dw_in = _mm_tn(h, dproj, name="od_in_dw", out_dtype=BF16)[:, :OD_IN]
    dh = _mm(dproj, p["w_in"], name="od_in_dx", nt=True)
    dx1, dmix = _rms_bwd(x1, p["mix_norm"], dh, dx2)
    small = dict(mix_norm=dmix, a_log=dalog.sum(0).reshape(GDN_HEADS), dt_bias=ddtb.sum(0).reshape(GDN_HEADS),
                 gdn_nw=dnw.reshape(HEAD_DIM))
    return dx1, dict(w_in=dw_in, conv_w=dcw.sum(0), w_out=dw_out), small


def _local_step(x, target, layer_weights, small, layer_done):
    nb, l, _ = x.shape
    t = nb * l
    lbs, lbs_vjp = jax.vjp(_hg_lower_bounds, small["hg_lb_logits"])
    row = lambda v: v.reshape(1, -1)
    s5_vjps = {}

    def layer_params(layer, big):
        j = layer // 2
        p = dict(mix_norm=row(small["mix_norm"][layer]), w_out=big["w_out"])
        if layer % 2 == 0:
            (abar, bbd, ccd), s5_vjps[j] = jax.vjp(
                _s5_params, small["s5_a_re"][j], small["s5_a_im"][j], small["s5_b_re"][j], small["s5_b_im"][j],
                small["s5_c_re"][j], small["s5_c_im"][j], small["s5_log_dt"][j])
            p.update(w_in_t=big["w_in_t"], w_glu=big["w_glu"],
                     lb=lbs[j].reshape(HG_HEADS, 1, HEAD_DIM), hg_nw=row(small["hg_norm_w"][j]),
                     abar=abar, bbd=bbd.astype(BF16), ccd=ccd.astype(BF16), s5_d=row(small["s5_d"][j]))
        else:
            p.update(w_in=big["w_in"], conv_w=big["conv_w"],
                     a_log=small["gdn_a_log"][j].reshape(GDN_HEADS, 1, 1),
                     dt_bias=small["gdn_dt_bias"][j].reshape(GDN_HEADS, 1, 1), gdn_nw=row(small["gdn_norm_w"][j]))
        return p

    xs = x.reshape(t, D_MODEL)
    saved, layers, ffn_w = [], [], []
    for layer in range(DEPTH):
        big = layer_weights(layer, xs)
        p = layer_params(layer, big)
        xs, r1 = _ffn_fwd(xs, row(small["ffn1_norm"][layer]), big["ffn"], 0)
        xs, r2 = (_even_fwd if layer % 2 == 0 else _odd_fwd)(xs, p, nb, l)
        xs, r3 = _ffn_fwd(xs, row(small["ffn2_norm"][layer]), big["ffn"], 1)
        saved.append((r1, r2, r3))
        layers.append(p)
        ffn_w.append(big["ffn"])
    loss, dx, dfinal = _loss_head(xs, row(small["final_norm"]), target.reshape(t, D_MODEL))

    gs = {k: [None] * DEPTH for k in ("ffn1_norm", "mix_norm", "ffn2_norm")}
    gs.update({k: [None] * 2 for k in ("hg_norm_w", "s5_a_re", "s5_a_im", "s5_b_re", "s5_b_im", "s5_c_re", "s5_c_im",
                                       "s5_d", "s5_log_dt", "gdn_a_log", "gdn_dt_bias", "gdn_norm_w")})
    dlbs = [None] * 2
    token = None
    for layer in reversed(range(DEPTH)):
        j = layer // 2
        p = layers[layer]
        r1, r2, r3 = saved[layer]
        dx, dnw, g_ffn = _ffn_bwd(dx, r3, row(small["ffn2_norm"][layer]), ffn_w[layer], 1, token)
        gs["ffn2_norm"][layer] = dnw[0]
        if layer % 2 == 0:
            dx, gw, sm = _even_bwd(dx, r2, p, nb, l)
            dlbs[j] = sm["lb"]
            gs["hg_norm_w"][j] = sm["hg_nw"]
            gs["s5_d"][j] = sm["s5_d"]
            (gs["s5_a_re"][j], gs["s5_a_im"][j], gs["s5_b_re"][j], gs["s5_b_im"][j], gs["s5_c_re"][j],
             gs["s5_c_im"][j], gs["s5_log_dt"][j]) = s5_vjps[j]((sm["abar"], sm["bbd"], sm["ccd"]))
        else:
            dx, gw, sm = _odd_bwd(dx, r2, p, nb, l)
            gs["gdn_a_log"][j], gs["gdn_dt_bias"][j], gs["gdn_norm_w"][j] = sm["a_log"], sm["dt_bias"], sm["gdn_nw"]
        gs["mix_norm"][layer] = sm["mix_norm"][0]
        token = layer_done(layer, 0, dict(gw, ffn=g_ffn))
        dx, dnw, g_ffn = _ffn_bwd(dx, r1, row(small["ffn1_norm"][layer]), ffn_w[layer], 0, token)
        gs["ffn1_norm"][layer] = dnw[0]
        token = layer_done(layer, 1, dict(ffn=g_ffn))
    gsmall = {k: jnp.stack(v) for k, v in gs.items()}
    gsmall["hg_lb_logits"] = lbs_vjp(jnp.stack(dlbs))[0]
    gsmall["final_norm"] = dfinal[0]
    return loss[0, 0], dx.reshape(nb, l, D_MODEL), gsmall


def _here():
    return lax.axis_index("x"), lax.axis_index("y"), lax.axis_index("c")


def _other_chips(x, y):
    return [(1 - x, y), (x, 1 - y), (1 - x, 1 - y)]


def _all_gather(blocks, name):
    n = len(blocks)

    def body(*refs):
        x_refs, o_refs = refs[:n], refs[n:2 * n]
        send_sems, recv_sems, local_sems = refs[2 * n:]
        x, y, cc = _here()
        me, sibling = (x, y, cc), (x, y, 1 - cc)
        chips = _other_chips(x, y)

        def win(i, p):
            return o_refs[i].at[:, 4 * p[0] + 2 * p[1] + p[2]]

        def copy(i, k, blk, to, src=None):
            return pltpu.make_async_remote_copy(
                src_ref=win(i, blk) if src is None else src, dst_ref=win(i, blk),
                send_sem=send_sems.at[i, k], recv_sem=recv_sems.at[i, k], device_id=to, device_id_type=MESH)

        mine = [pltpu.make_async_copy(x_refs[i], win(i, me), local_sems.at[i]) for i in range(n)]
        for cp in mine:
            cp.start()
        first = []
        for i in range(n):
            first.append(copy(i, 0, me, sibling, src=x_refs[i]))
            first += [copy(i, 1 + j, me, (*chip, cc), src=x_refs[i]) for j, chip in enumerate(chips)]
        for cp in first:
            cp.start()
        passed = []
        for j, chip in enumerate(chips):
            for i in range(n):
                copy(i, 1 + j, (*chip, cc), me).wait_recv()
                fwd = copy(i, 4 + j, (*chip, cc), sibling)
                fwd.start()
                passed.append(fwd)
        for i in range(n):
            copy(i, 0, sibling, me).wait_recv()
            for j, chip in enumerate(chips):
                copy(i, 4 + j, (*chip, 1 - cc), me).wait_recv()
        for cp in first + passed:
            cp.wait_send()
        for cp in mine:
            cp.wait()

    hbm = pl.BlockSpec(memory_space=pl.ANY)
    return pl.pallas_call(
        body, name=name, in_specs=[hbm] * n, out_specs=[hbm] * n,
        out_shape=[jax.ShapeDtypeStruct((b.shape[0], N_DEV) + b.shape[1:], b.dtype) for b in blocks],
        scratch_shapes=[pltpu.SemaphoreType.DMA((n, 7)), pltpu.SemaphoreType.DMA((n, 7)), pltpu.SemaphoreType.DMA((n,))],
    )(*blocks)


_HBM = pl.BlockSpec(memory_space=pltpu.HBM)
_SEM = pl.BlockSpec(memory_space=pltpu.SEMAPHORE)
_DATAFLOW = pltpu.SideEffectType.DATAFLOW_SIDE_EFFECTING


def _peer(k):
    x, y, c = _here()
    return (x ^ ((k >> 2) & 1), y ^ ((k >> 1) & 1), c ^ (k & 1))


def _device_of(p):
    return 4 * p[0] + 2 * p[1] + p[2]


_OTHER_CHIPS = (4, 2, 6)


def _exchange_start(srcs, lands, copies, name, after=None):
    n = len(srcs)
    n_in = 2 * n + (after is not None)

    def body(*refs):
        s_refs, l_refs, sems = refs[:n], refs[n:2 * n], refs[n_in:n_in + 2 * n]
        for i in range(n):
            for slot, k in enumerate(_OTHER_CHIPS):
                src, dst = copies(s_refs[i], l_refs[i], k, slot)
                pltpu.make_async_remote_copy(
                    src_ref=src, dst_ref=dst, send_sem=sems[2 * i], recv_sem=sems[2 * i + 1],
                    device_id=_peer(k), device_id_type=MESH).start()
        refs[-1][...] = jnp.zeros_like(refs[-1])

    both = list(srcs) + list(lands)
    outs = pl.pallas_call(
        body, name=name,
        out_shape=tuple([pltpu.SemaphoreType.DMA(())] * (2 * n)) + tuple(pltpu.HBM(a.shape, a.dtype) for a in both)
        + (jax.ShapeDtypeStruct((8, LANES), F32),),
        in_specs=[_HBM] * (2 * n) + [pl.BlockSpec(memory_space=pl.ANY)] * (after is not None),
        out_specs=tuple([_SEM] * (2 * n)) + tuple([_HBM] * (2 * n)) + (pl.BlockSpec(memory_space=pltpu.VMEM),),
        input_output_aliases={i: 2 * n + i for i in range(2 * n)},
        compiler_params=pltpu.CompilerParams(has_side_effects=_DATAFLOW),
    )(*[pltpu.with_memory_space_constraint(a, pltpu.HBM) for a in both], *([after] if after is not None else []))
    return list(outs[:2 * n]), list(outs[2 * n:3 * n]), list(outs[3 * n:4 * n]), outs[4 * n]


def _exchange_wait(handle, three, after, name):
    sems, srcs, lands, _ = handle
    n = len(srcs)

    def body(*refs):
        l_refs, sem_refs = refs[n:2 * n], refs[2 * n:4 * n]
        for i in range(n):
            all_three = three(l_refs[i])
            cp = pltpu.make_async_remote_copy(
                src_ref=all_three, dst_ref=all_three, send_sem=sem_refs[2 * i], recv_sem=sem_refs[2 * i + 1],
                device_id=_peer(_OTHER_CHIPS[0]), device_id_type=MESH)
            cp.wait_send()
            cp.wait_recv()

    both = list(srcs) + list(lands)
    outs = pl.pallas_call(
        body, name=name, out_shape=tuple(pltpu.HBM(a.shape, a.dtype) for a in both),
        in_specs=[_HBM] * (2 * n) + [_SEM] * (2 * n) + [pl.BlockSpec(memory_space=pl.ANY)],
        out_specs=tuple([_HBM] * (2 * n)), input_output_aliases={i: i for i in range(2 * n)},
        compiler_params=pltpu.CompilerParams(has_side_effects=_DATAFLOW),
    )(*both, *sems, after)
    return list(outs[:n]), list(outs[n:])


def _gather_copies(src, land, k, slot):
    del k, slot
    return src, land.at[:, _device_of(_here())]


def _gather_three(land):
    return land.at[:, pl.ds(0, len(_OTHER_CHIPS))]


def _scatter_copies(src, land, k, slot):
    p = _peer(k)
    return src.at[2 * p[0] + p[1]], land.at[slot]


def _scatter_three(land):
    return land


def _core_forward(lands):
    n = len(lands)

    def body(*refs):
        l_refs = refs[:n]
        send_sems, recv_sems = refs[2 * n:]
        x, y, cc = _here()
        mine = [(x, y)] + [_peer(k)[:2] for k in _OTHER_CHIPS]
        cps = []
        for i in range(n):
            for j, (px, py) in enumerate(mine):
                give = l_refs[i].at[:, _device_of((px, py, cc))]
                take = l_refs[i].at[:, _device_of((px, py, 1 - cc))]
                cps.append((pltpu.make_async_remote_copy(
                    src_ref=give, dst_ref=give, send_sem=send_sems.at[i, j], recv_sem=recv_sems.at[i, j],
                    device_id=(x, y, 1 - cc), device_id_type=MESH), take))
        for cp, _ in cps:
            cp.start()
        for i in range(n):
            for j, (cp, take) in enumerate(cps[4 * i:4 * i + 4]):
                cp.wait_send()
                pltpu.make_async_remote_copy(
                    src_ref=take, dst_ref=take, send_sem=send_sems.at[i, j], recv_sem=recv_sems.at[i, j],
                    device_id=(x, y, 1 - cc), device_id_type=MESH).wait_recv()

    hbm = pl.BlockSpec(memory_space=pl.ANY)
    return pl.pallas_call(
        body, name="gather_core_forward", in_specs=[hbm] * n, out_specs=[hbm] * n,
        out_shape=[jax.ShapeDtypeStruct(a.shape, a.dtype) for a in lands],
        input_output_aliases={i: i for i in range(n)},
        scratch_shapes=[pltpu.SemaphoreType.DMA((n, 4)), pltpu.SemaphoreType.DMA((n, 4))],
    )(*lands)


def _rs_core_exchange(gs):
    n = len(gs)

    def body(*refs):
        g_refs, o_refs = refs[:n], refs[n:2 * n]
        send_sems, recv_sems = refs[2 * n:]
        x, y, cc = _here()
        cps = [pltpu.make_async_remote_copy(
            src_ref=g_refs[i].at[:, 2 * j + (1 - cc)], dst_ref=o_refs[i].at[j],
            send_sem=send_sems.at[i, j], recv_sem=recv_sems.at[i, j], device_id=(x, y, 1 - cc), device_id_type=MESH)
            for i in range(n) for j in range(4)]
        for cp in cps:
            cp.start()
        for cp in cps:
            cp.wait()

    hbm = pl.BlockSpec(memory_space=pl.ANY)
    return pl.pallas_call(
        body, name="rs_core_exchange", in_specs=[hbm] * n, out_specs=[hbm] * n,
        out_shape=[jax.ShapeDtypeStruct((4, g.shape[0]) + g.shape[2:], g.dtype) for g in gs],
        scratch_shapes=[pltpu.SemaphoreType.DMA((n, 4)), pltpu.SemaphoreType.DMA((n, 4))],
    )(*gs)


def _row_tile(w):
    return w if w <= 512 else _tile(w, (512, 256, 128))


def _rs_chip_sum(g, r1):
    a, _, w, c = g.shape
    tr = _row_tile(w)

    def body(g_ref, r_ref, o_ref):
        o_ref[...] = (g_ref[...].astype(F32) + r_ref[...].astype(F32)).astype(o_ref.dtype)

    blk = pl.BlockSpec((None, None, tr, c), lambda j, aa, i: (j, aa, i, 0))
    return pl.pallas_call(
        body, name="rs_chip_sum", grid=(4, a, w // tr),
        in_specs=[pl.BlockSpec((None, None, tr, c), lambda j, aa, i: (aa, 2 * j + lax.axis_index("c"), i, 0)), blk],
        out_specs=blk, out_shape=jax.ShapeDtypeStruct((4, a, w, c), g.dtype),
        compiler_params=_params(("parallel", "parallel", "parallel")),
    )(g, r1)


def _rs_final_sum(g, r1, r2):
    a, _, w, c = g.shape
    tr = _row_tile(w)

    def body(g_ref, r1_ref, r2_ref, o_ref):
        s = g_ref[...].astype(F32) + r1_ref[...].astype(F32)
        for k in range(3):
            s = s + r2_ref[k].astype(F32)
        o_ref[...] = s

    def device():
        return 4 * lax.axis_index("x") + 2 * lax.axis_index("y") + lax.axis_index("c")

    def chip():
        return 2 * lax.axis_index("x") + lax.axis_index("y")

    return pl.pallas_call(
        body, name="rs_final_sum", grid=(a, w // tr),
        in_specs=[pl.BlockSpec((None, None, tr, c), lambda aa, i: (aa, device(), i, 0)),
                  pl.BlockSpec((None, None, tr, c), lambda aa, i: (chip(), aa, i, 0)),
                  pl.BlockSpec((3, None, tr, c), lambda aa, i: (0, aa, i, 0))],
        out_specs=pl.BlockSpec((None, tr, c), lambda aa, i: (aa, i, 0)),
        out_shape=jax.ShapeDtypeStruct((a, w, c), F32),
        compiler_params=_params(("parallel", "parallel")),
    )(g, r1, r2)


def _sum_slots(g):
    _, n, r, c = g.shape

    def body(g_ref, o_ref):
        s = g_ref[0, 0]
        for k in range(1, n):
            s = s + g_ref[0, k]
        o_ref[...] = s

    return pl.pallas_call(
        body, name="sum_slots", grid=(1,),
        in_specs=[pl.BlockSpec((1, n, r, c), lambda i: (0, 0, 0, 0))], out_specs=pl.BlockSpec((r, c), lambda i: (0, 0)),
        out_shape=jax.ShapeDtypeStruct((r, c), F32), compiler_params=_params(("arbitrary",)),
    )(g)


_WEIGHTS = ['ffn1_norm', 'ffn1_w_gate', 'ffn1_w_up', 'ffn1_w_down', 'mix_norm', 'ffn2_norm', 'ffn2_w_gate', 'ffn2_w_up',
            'ffn2_w_down', 'ev_w_in', 'hg_lb_logits', 'hg_norm_w', 's5_a_re', 's5_a_im', 's5_b_re', 's5_b_im', 's5_c_re',
            's5_c_im', 's5_d', 's5_log_dt', 's5_w_glu', 'ev_w_out', 'od_w_in', 'gdn_conv_w', 'gdn_a_log', 'gdn_dt_bias',
            'gdn_norm_w', 'od_w_out', 'final_norm']
_SHARDED = ['ffn1_w_gate', 'ffn1_w_up', 'ffn1_w_down', 'ffn2_w_gate', 'ffn2_w_up', 'ffn2_w_down', 'ev_w_in', 's5_w_glu',
            'ev_w_out', 'od_w_in', 'gdn_conv_w', 'od_w_out']
_REPLICATED = [n for n in _WEIGHTS if n not in _SHARDED]
_SMALL_COLS = 1024


def _pad_rows(a, rows):
    return jnp.pad(a, ((0, rows - a.shape[0]), (0, 0)))


def _layer_blocks(w, layer):
    j = layer // 2
    ffn = []
    for f in ('ffn1', 'ffn2'):
        ffn += [_pad_rows(w[f + '_w_gate'][layer].T, FF_SHARD_PAD), _pad_rows(w[f + '_w_up'][layer].T, FF_SHARD_PAD),
                _pad_rows(w[f + '_w_down'][layer], FF_SHARD_PAD)]
    ffn = jnp.stack(ffn).astype(BF16)
    if layer % 2 == 0:
        return [ffn, w['ev_w_in'][j].T.astype(BF16)[None], w['s5_w_glu'][j].astype(BF16)[None],
                w['ev_w_out'][j].astype(BF16)[None]]
    return [ffn, w['od_w_in'][j].astype(BF16)[None], w['gdn_conv_w'][j][None], w['od_w_out'][j].astype(BF16)[None]]


def _layer_whole(gathered, layer):
    whole = lambda g: g.reshape(g.shape[0], N_DEV * g.shape[2], g.shape[3])
    ffn, a1, a2, w_out = gathered
    if layer % 2 == 0:
        return dict(ffn=whole(ffn), w_in_t=whole(a1)[0], w_glu=whole(a2)[0], w_out=whole(w_out)[0])
    w_in = jnp.moveaxis(a1[0], 0, 1).reshape(D_MODEL, OD_IN)
    conv = jnp.moveaxis(a2[0], 0, 1).reshape(CONV_W, GDN_QKV)
    return dict(ffn=whole(ffn), w_in=jnp.pad(w_in, ((0, 0), (0, OD_IN_PAD - OD_IN))), conv_w=conv, w_out=whole(w_out)[0])


def _layer_grad_windows(g, layer):
    windows = lambda a, w: a.reshape(-1, N_DEV, w, a.shape[-1])
    ffn = windows(g['ffn'], FF_SHARD_PAD)
    if 'w_out' not in g:
        return [ffn]
    w_out = windows(g['w_out'], D_MODEL // N_DEV)
    if layer % 2 == 0:
        return [ffn, windows(g['w_in_t'], EV_IN // N_DEV), windows(g['w_glu'], S5_WIDTH // N_DEV), w_out]
    w_in = jnp.moveaxis(g['w_in'].reshape(D_MODEL, N_DEV, OD_IN // N_DEV), 1, 0)[None]
    conv = jnp.moveaxis(g['conv_w'].reshape(CONV_W, N_DEV, GDN_QKV // N_DEV), 1, 0)[None]
    return [ffn, w_in, conv, w_out]


def _flat_small(src, prefix=''):
    flat = jnp.concatenate([src[prefix + n].reshape(-1) for n in _REPLICATED])
    rows = -(-(flat.shape[0] + 1) // (_SMALL_COLS * 8)) * 8
    return jnp.pad(flat, (0, rows * _SMALL_COLS - flat.shape[0])).reshape(rows, _SMALL_COLS)


def _split_small(flat2d, like):
    flat, out, off = flat2d.reshape(-1), {}, 0
    for n in _REPLICATED:
        size = math.prod(like[n].shape)
        out[n] = flat[off:off + size].reshape(like[n].shape)
        off += size
    return out, flat[off]


def _all_reduce_small(gsmall, loss):
    flat = _flat_small(gsmall)
    n_used = sum(math.prod(gsmall[n].shape) for n in _REPLICATED)
    flat = flat.reshape(-1).at[n_used].set(loss).reshape(flat.shape)
    (gathered,) = _all_gather([flat[None]], "gather_small")
    return _split_small(_sum_slots(gathered), gsmall)


def _as_2d(a):
    return a.reshape(-1, a.shape[-1])


def kernel(x, ffn1_norm, ffn1_w_gate, ffn1_w_up, ffn1_w_down, mix_norm, ffn2_norm, ffn2_w_gate, ffn2_w_up, ffn2_w_down, ev_w_in, hg_lb_logits, hg_norm_w, s5_a_re, s5_a_im, s5_b_re, s5_b_im, s5_c_re, s5_c_im, s5_d, s5_log_dt, s5_w_glu, ev_w_out, od_w_in, gdn_conv_w, gdn_a_log, gdn_dt_bias, gdn_norm_w, od_w_out, final_norm, loss_target, m_ffn1_norm, m_ffn1_w_gate, m_ffn1_w_up, m_ffn1_w_down, m_mix_norm, m_ffn2_norm, m_ffn2_w_gate, m_ffn2_w_up, m_ffn2_w_down, m_ev_w_in, m_hg_lb_logits, m_hg_norm_w, m_s5_a_re, m_s5_a_im, m_s5_b_re, m_s5_b_im, m_s5_c_re, m_s5_c_im, m_s5_d, m_s5_log_dt, m_s5_w_glu, m_ev_w_out, m_od_w_in, m_gdn_conv_w, m_gdn_a_log, m_gdn_dt_bias, m_gdn_norm_w, m_od_w_out, m_final_norm, v_ffn1_norm, v_ffn1_w_gate, v_ffn1_w_up, v_ffn1_w_down, v_mix_norm, v_ffn2_norm, v_ffn2_w_gate, v_ffn2_w_up, v_ffn2_w_down, v_ev_w_in, v_hg_lb_logits, v_hg_norm_w, v_s5_a_re, v_s5_a_im, v_s5_b_re, v_s5_b_im, v_s5_c_re, v_s5_c_im, v_s5_d, v_s5_log_dt, v_s5_w_glu, v_ev_w_out, v_od_w_in, v_gdn_conv_w, v_gdn_a_log, v_gdn_dt_bias, v_gdn_norm_w, v_od_w_out, v_final_norm):
    given = dict(locals())
    w = {n: given[n] for n in _WEIGHTS}
    small = {n: w[n] for n in _REPLICATED}
    me = _device_of(_here())

    blocks = [_layer_blocks(w, layer) for layer in range(DEPTH)]
    first = _all_gather(blocks[0], "gather_weights")
    pending = {}

    def start_gather(layer, after=None):
        lands = [lax.dynamic_update_slice(lax.empty((b.shape[0], N_DEV) + b.shape[1:], b.dtype), b[:, None], (0, me, 0, 0))
                 for b in blocks[layer]]
        pending[layer] = _exchange_start(blocks[layer], lands, _gather_copies, "gather_start_%d" % layer, after)

    start_gather(1)

    def layer_weights(layer, after):
        if layer == 0:
            return _layer_whole(first, 0)
        got = _exchange_wait(pending[layer], _gather_three, after, "gather_wait_%d" % layer)[1]
        if layer + 1 < DEPTH:
            start_gather(layer + 1, after)
        return _layer_whole(_core_forward(got), layer)

    sent = {}

    def layer_done(layer, part, grads):
        gs = _layer_grad_windows(grads, layer)
        from_sibling = _rs_core_exchange(gs)
        chip = [_rs_chip_sum(g, r) for g, r in zip(gs, from_sibling)]
        lands = [lax.empty((len(_OTHER_CHIPS),) + p.shape[1:], p.dtype) for p in chip]
        handle = _exchange_start(chip, lands, _scatter_copies, "scatter_start_%d_%d" % (layer, part))
        sent[layer, part] = (gs, from_sibling, handle)
        return handle[3]

    loss_part, grad_x, gsmall = _local_step(given['x'], given['loss_target'], layer_weights, small, layer_done)
    gsum, loss = _all_reduce_small(gsmall, loss_part)
    summed = {}
    for layer in reversed(range(DEPTH)):
        for part in range(2):
            gs, from_sibling, handle = sent[layer, part]
            _, lands = _exchange_wait(handle, _scatter_three, grad_x, "scatter_wait_%d_%d" % (layer, part))
            summed[layer, part] = [_rs_final_sum(g, r, land) for g, r, land in zip(gs, from_sibling, lands)]

    grads = dict(gsum)
    t_ = lambda a: jnp.swapaxes(a, 1, 2)
    for i, f in enumerate(('ffn1', 'ffn2')):
        ffn = jnp.stack([summed[layer, 1 - i][0][:, :FF_SHARD] for layer in range(DEPTH)], axis=1)
        grads[f + '_w_gate'], grads[f + '_w_up'], grads[f + '_w_down'] = t_(ffn[0]), t_(ffn[1]), ffn[2]
    even, odd = [summed[0, 0], summed[2, 0]], [summed[1, 0], summed[3, 0]]
    grads['ev_w_in'] = jnp.stack([r[1][0].T for r in even])
    grads['s5_w_glu'] = jnp.stack([r[2][0] for r in even])
    grads['ev_w_out'] = jnp.stack([r[3][0] for r in even])
    grads['od_w_in'] = jnp.stack([r[1][0] for r in odd])
    grads['gdn_conv_w'] = jnp.stack([r[2][0] for r in odd])
    grads['od_w_out'] = jnp.stack([r[3][0] for r in odd])

    delta, new_m, new_v = {}, {}, {}
    for n in _SHARDED:
        d, nm, nv = _adamw(_as_2d(w[n]), _as_2d(grads[n]), _as_2d(given['m_' + n]), _as_2d(given['v_' + n]))
        delta[n], new_m[n], new_v[n] = (a.reshape(w[n].shape) for a in (d, nm, nv))
    d, nm, nv = _adamw(_flat_small(w), _flat_small(grads), _flat_small(given, 'm_'), _flat_small(given, 'v_'))
    (delta_s, _), (new_m_s, _), (new_v_s, _) = (_split_small(a, small) for a in (d, nm, nv))
    delta.update(delta_s)
    new_m.update(new_m_s)
    new_v.update(new_v_s)
    return (loss, grad_x, *[grads[n] for n in _WEIGHTS], *[delta[n] for n in _WEIGHTS],
            *[new_m[n] for n in _WEIGHTS], *[new_v[n] for n in _WEIGHTS])
```

```python
import math

import jax
import jax.numpy as jnp
from jax import lax
from jax.experimental import pallas as pl
from jax.experimental.pallas import tpu as pltpu

F32 = jnp.float32
BF16 = jnp.bfloat16
HI = lax.Precision.HIGH

D_MODEL = 1024
DEPTH = 4
D_FF = 2816
NORM_EPS = 1e-6
F_MIN = 1e-6
CHUNK = 64
HG_HEADS = 4
HEAD_DIM = 128
HG_QK = HG_HEADS * HEAD_DIM
S5_WIDTH = 512
S5_GROUP = 16
S5_GROUPS = 32
S5_STATE = 64
S5_NSTATE = S5_GROUPS * S5_STATE
EV_IN = 2560
GDN_HEADS = 8
GDN_QKV = 3 * GDN_HEADS * HEAD_DIM
CONV_W = 4
OD_IN = 4112
OD_IN_PAD = 4224
N_DEV = 8
LANES = 128
ADAM_LR, ADAM_B1, ADAM_B2, ADAM_EPS, ADAM_WD, ADAM_STEP = 0.001, 0.9, 0.999, 1e-08, 0.01, 10
VMEM_LIMIT = 56 * 1024 * 1024

MESH = pl.DeviceIdType.MESH


def _params(sem=None, **kw):
    return pltpu.CompilerParams(dimension_semantics=sem, vmem_limit_bytes=VMEM_LIMIT, **kw)


def _tile(n, cands):
    for c in cands:
        if n % c == 0:
            return c
    return n


def _dot(a, b):
    return jnp.dot(a.astype(BF16), b.astype(BF16), preferred_element_type=F32)


def _dot_nt(a, b):
    return lax.dot_general(a.astype(BF16), b.astype(BF16), (((1,), (1,)), ((), ())), preferred_element_type=F32)


def _dot_tn(a, b):
    return lax.dot_general(a.astype(BF16), b.astype(BF16), (((0,), (0,)), ((), ())), preferred_element_type=F32)


def _dot_hi(a, b):
    return jnp.dot(a, b, precision=HI, preferred_element_type=F32)


def _bmm(a, b):
    return jnp.einsum('gmk,gkn->gmn', a.astype(BF16), b.astype(BF16), preferred_element_type=F32)


def _bmm_nt(a, b):
    return jnp.einsum('gmk,gnk->gmn', a.astype(BF16), b.astype(BF16), preferred_element_type=F32)


def _bmm_tn(a, b):
    return jnp.einsum('gkm,gkn->gmn', a.astype(BF16), b.astype(BF16), preferred_element_type=F32)


def _bmm_hi(a, b):
    return jnp.einsum('gmk,gkn->gmn', a, b, precision=HI, preferred_element_type=F32)


_TN_CANDS = (1408, 1280, 1024, 512, 384, 256, 128)
_B_TILE_BYTES = 6 * 1024 * 1024
_TOKEN_TILES = (1024, 512, 256, 128)


def _mm(a, b, *, name, nt=False, out_dtype=F32, res=None, scale=1.0, token_tiles=None):
    m, k = a.shape
    n = b.shape[0] if nt else b.shape[1]
    tm = _tile(m, token_tiles or _TOKEN_TILES)
    tn = _tile(n, [c for c in _TN_CANDS if c * k * b.dtype.itemsize <= _B_TILE_BYTES])

    def body(*refs):
        if res is None:
            a_ref, b_ref, o_ref = refs
        else:
            a_ref, b_ref, r_ref, o_ref = refs
        acc = (_dot_nt if nt else _dot)(a_ref[...], b_ref[...])
        if scale != 1.0:
            acc = scale * acc
        if res is not None:
            acc = r_ref[...] + acc
        o_ref[...] = acc.astype(out_dtype)

    b_spec = pl.BlockSpec((tn, k), lambda j, i: (j, 0)) if nt else pl.BlockSpec((k, tn), lambda j, i: (0, j))
    in_specs = [pl.BlockSpec((tm, k), lambda j, i: (i, 0)), b_spec]
    args = [a, b]
    if res is not None:
        in_specs.append(pl.BlockSpec((tm, tn), lambda j, i: (i, j)))
        args.append(res)
    return pl.pallas_call(
        body, name=name, grid=(n // tn, m // tm), in_specs=in_specs,
        out_specs=pl.BlockSpec((tm, tn), lambda j, i: (i, j)),
        out_shape=jax.ShapeDtypeStruct((m, n), out_dtype),
        compiler_params=_params(("parallel", "parallel")),
    )(*args)


def _mm_tn(a, b, *, name, scale=1.0, out_dtype=F32, into=None, slot=0):
    t, m = a.shape
    n = b.shape[1]
    tm = _tile(m, (1024, 512, 256, 128))
    tn = _tile(n, _TN_CANDS)
    tk = _tile(t, (2048, 1024, 512, 256, 128))
    nk = t // tk

    def body(*refs):
        a_ref, b_ref = refs[:2]
        o_ref, acc_ref = refs[-2:]
        kk = pl.program_id(2)

        @pl.when(kk == 0)
        def _():
            acc_ref[...] = jnp.zeros_like(acc_ref)

        acc_ref[...] += _dot_tn(a_ref[...], b_ref[...])

        @pl.when(kk == nk - 1)
        def _():
            o_ref[...] = (acc_ref[...] * scale if scale != 1.0 else acc_ref[...]).astype(o_ref.dtype)

    in_specs = [pl.BlockSpec((tk, tm), lambda i, j, kk: (kk, i)), pl.BlockSpec((tk, tn), lambda i, j, kk: (kk, j))]
    args = [a, b]
    if into is None:
        out_spec = pl.BlockSpec((tm, tn), lambda i, j, kk: (i, j))
        out_shape = jax.ShapeDtypeStruct((m, n), out_dtype)
        alias = {}
    else:
        in_specs.append(pl.BlockSpec(memory_space=pl.ANY))
        args.append(into)
        out_spec = pl.BlockSpec((None, tm, tn), lambda i, j, kk: (slot, i, j))
        out_shape = jax.ShapeDtypeStruct(into.shape, into.dtype)
        alias = {2: 0}
    return pl.pallas_call(
        body, name=name, grid=(m // tm, n // tn, nk), in_specs=in_specs, out_specs=out_spec, out_shape=out_shape,
        scratch_shapes=[pltpu.VMEM((tm, tn), F32)], input_output_aliases=alias,
        compiler_params=_params(("parallel", "parallel", "arbitrary")),
    )(*args)


def _sigmoid(x):
    return jax.nn.sigmoid(x)


def _head_norm_gate(o, gate, nw):
    r = lax.rsqrt(jnp.mean(o * o, axis=-1, keepdims=True) + NORM_EPS)
    return o * r * nw * (gate * _sigmoid(gate))


def _hg_chunk(st, ql, fl, iv, gl, lb, nw):
    n_g, c, _ = ql.shape
    q = ql * _sigmoid(ql)
    f = lb + (1.0 - lb) * _sigmoid(fl)
    lf = jnp.log(jnp.maximum(f, F_MIN))
    k = 1.0 - f
    ri = lax.broadcasted_iota(jnp.int32, (c, c), 0)
    ci = lax.broadcasted_iota(jnp.int32, (c, c), 1)
    rl = lax.broadcasted_iota(jnp.int32, (c, LANES), 0)
    each = lambda m2: jnp.broadcast_to(m2[None], (n_g,) + m2.shape)
    b = _bmm_hi(each(jnp.where(ci <= ri, 1.0, 0.0)), lf)
    attn = jnp.where((ri == ci)[None], _bmm_nt(q, k), 0.0)
    sh = 0
    while (1 << sh) < c:
        m = 1 << sh
        ref = ((ri >> (sh + 1)) << (sh + 1)) + (m - 1)
        low_r = ((ri >> sh) & 1) == 1
        low_c = ((ci >> sh) & 1) == 1
        w_low = low_r & (ci > ref) & (ci <= ri)
        w_up = jnp.logical_not(low_r) & (ci > ri) & (ci <= ref)
        w = jnp.where(w_low | w_up, 1.0, 0.0)
        e = jnp.exp(_bmm_hi(each(w), lf))
        low_l = (((rl >> sh) & 1) == 1)[None]
        qs = jnp.where(low_l, q * e, 0.0)
        ks = jnp.where(low_l, 0.0, k * e)
        pair = ((ri >> (sh + 1)) == (ci >> (sh + 1))) & low_r & jnp.logical_not(low_c)
        attn = attn + jnp.where(pair[None], _bmm_nt(qs, ks), 0.0)
        sh += 1
    bl = jnp.sum(lf, axis=1, keepdims=True)
    o = _bmm(attn, iv) + _bmm_nt(q * jnp.exp(b), st)
    st_new = st * jnp.exp(bl) + _bmm_tn(iv, k * jnp.exp(bl - b))
    return _head_norm_gate(o, gl, nw), st_new


def _hg_specs(nb, n_c, rev):
    def cidx(cc):
        return (n_c - 1 - cc) if rev else cc

    def col(width):
        return pl.BlockSpec((nb, CHUNK, width), lambda cc: (0, cidx(cc), 0))

    vec = pl.BlockSpec((HG_HEADS, 1, LANES), lambda cc: (0, 0, 0))
    nw = pl.BlockSpec((1, LANES), lambda cc: (0, 0))
    st = pl.BlockSpec((nb, HG_HEADS, None, LANES, LANES), lambda cc: (0, 0, cidx(cc), 0, 0))
    acc = pl.BlockSpec((nb, HG_HEADS, 1, LANES), lambda cc: (0, 0, 0, 0))
    return col, vec, nw, st, acc


def _head(j):
    return slice(j * LANES, (j + 1) * LANES)


def _seq_heads(ref, nb, n, first=0):
    return jnp.stack([ref[b, :, _head(first + j)] for b in range(nb) for j in range(n)])


def _put_seq_heads(ref, val, nb, n, first=0):
    for b in range(nb):
        for j in range(n):
            ref[b, :, _head(first + j)] = val[b * n + j].astype(ref.dtype)


def _hgrn2_fwd(proj, lb, nw):
    nb, l, _ = proj.shape
    n_c = l // CHUNK
    n_g = nb * HG_HEADS
    col, vec, nws, st, _ = _hg_specs(nb, n_c, False)

    def body(p_ref, lb_ref, nw_ref, y_ref, st_ref, s_scr):
        @pl.when(pl.program_id(0) == 0)
        def _():
            s_scr[...] = jnp.zeros_like(s_scr)

        s_in = s_scr[...]
        st_ref[...] = s_in.reshape(st_ref.shape)
        args = [_seq_heads(p_ref, nb, HG_HEADS, k * HG_HEADS) for k in range(4)]
        y, s_new = _hg_chunk(s_in, *args, jnp.concatenate([lb_ref[...]] * nb), nw_ref[...])
        _put_seq_heads(y_ref, y, nb, HG_HEADS)
        s_scr[...] = s_new

    return pl.pallas_call(
        body, name="hgrn2_fwd", grid=(n_c,),
        in_specs=[col(4 * HG_QK), vec, nws], out_specs=[col(HG_QK), st],
        out_shape=[jax.ShapeDtypeStruct((nb, l, HG_QK), BF16),
                   jax.ShapeDtypeStruct((nb, HG_HEADS, n_c, LANES, LANES), F32)],
        scratch_shapes=[pltpu.VMEM((n_g, LANES, LANES), F32)],
        compiler_params=_params(("arbitrary",)),
    )(proj, lb, nw)


def _hgrn2_bwd(proj, lb, nw, states, dy):
    nb, l, _ = proj.shape
    n_c = l // CHUNK
    n_g = nb * HG_HEADS
    col, vec, nws, st, acc = _hg_specs(nb, n_c, True)

    def body(p_ref, lb_ref, nw_ref, st_ref, dy_ref, dp_ref, dlb_ref, dnw_ref, ds_scr):
        @pl.when(pl.program_id(0) == 0)
        def _():
            ds_scr[...] = jnp.zeros_like(ds_scr)
            dlb_ref[...] = jnp.zeros_like(dlb_ref)
            dnw_ref[...] = jnp.zeros_like(dnw_ref)

        args = [_seq_heads(p_ref, nb, HG_HEADS, k * HG_HEADS) for k in range(4)]
        _, vjp = jax.vjp(_hg_chunk, st_ref[...].reshape(n_g, LANES, LANES), *args,
                         jnp.concatenate([lb_ref[...]] * nb), nw_ref[...])
        ds, dql, dfl, div, dgl, dlb, dnw = vjp((_seq_heads(dy_ref, nb, HG_HEADS).astype(F32), ds_scr[...]))
        ds_scr[...] = ds
        for k, val in enumerate((dql, dfl, div, dgl)):
            _put_seq_heads(dp_ref, val, nb, HG_HEADS, k * HG_HEADS)
        dlb_ref[...] += dlb.reshape(dlb_ref.shape)
        dnw_ref[...] += dnw

    return pl.pallas_call(
        body, name="hgrn2_bwd", grid=(n_c,),
        in_specs=[col(4 * HG_QK), vec, nws, st, col(HG_QK)], out_specs=[col(4 * HG_QK), acc, nws],
        out_shape=[jax.ShapeDtypeStruct((nb, l, 4 * HG_QK), BF16),
                   jax.ShapeDtypeStruct((nb, HG_HEADS, 1, LANES), F32), jax.ShapeDtypeStruct((1, LANES), F32)],
        scratch_shapes=[pltpu.VMEM((n_g, LANES, LANES), F32)],
        compiler_params=_params(("arbitrary",)),
    )(proj, lb, nw, states, dy)


def _shift_down(x, s):
    if s == 0:
        return x
    rows = lax.broadcasted_iota(jnp.int32, x.shape, 0)
    return jnp.where(rows >= s, pltpu.roll(x, s, 0), 0.0)


def _shift_up(x, s):
    if s == 0:
        return x
    n = x.shape[0]
    rows = lax.broadcasted_iota(jnp.int32, x.shape, 0)
    return jnp.where(rows < n - s, pltpu.roll(x, n - s, 0), 0.0)


def _gdn_pre_fwd(proj, conv_w):
    nb, l, _ = proj.shape
    n_blk = GDN_QKV // LANES

    def body(x_ref, w_ref, o_ref):
        j = pl.program_id(1)
        x = x_ref[...]
        w = w_ref[...]
        c = w[3:4] * x
        for t in range(CONV_W - 1):
            c = c + w[t:t + 1] * _shift_down(x, CONV_W - 1 - t)
        s = c * _sigmoid(c)
        r = lax.rsqrt(jnp.sum(s * s, axis=-1, keepdims=True) + NORM_EPS)
        scale = jnp.where(j < GDN_HEADS, HEAD_DIM ** -0.5, 1.0)
        o_ref[...] = jnp.where(j < 2 * GDN_HEADS, s * r * scale, s)

    return pl.pallas_call(
        body, name="gdn_pre_fwd", grid=(nb, n_blk),
        in_specs=[pl.BlockSpec((None, l, LANES), lambda b, j: (b, 0, j)), pl.BlockSpec((CONV_W, LANES), lambda b, j: (0, j))],
        out_specs=pl.BlockSpec((None, l, LANES), lambda b, j: (b, 0, j)),
        out_shape=jax.ShapeDtypeStruct((nb, l, GDN_QKV), F32),
        compiler_params=_params(("parallel", "parallel")),
    )(proj, conv_w)


def _gdn_pre_bwd(proj, conv_w, dout):
    nb, l, _ = proj.shape
    n_blk = GDN_QKV // LANES

    def body(x_ref, w_ref, d_ref, dx_ref, dw_ref):
        j = pl.program_id(1)
        x = x_ref[...]
        w = w_ref[...]
        xs = [_shift_down(x, CONV_W - 1 - t) for t in range(CONV_W)]
        c = w[0:1] * xs[0]
        for t in range(1, CONV_W):
            c = c + w[t:t + 1] * xs[t]
        sg = _sigmoid(c)
        s = c * sg
        d = d_ref[...]
        r = lax.rsqrt(jnp.sum(s * s, axis=-1, keepdims=True) + NORM_EPS)
        scale = jnp.where(j < GDN_HEADS, HEAD_DIM ** -0.5, 1.0)
        ds_norm = scale * r * (d - s * (r * r) * jnp.sum(d * s, axis=-1, keepdims=True))
        ds = jnp.where(j < 2 * GDN_HEADS, ds_norm, d)
        dc = ds * (sg * (1.0 + c * (1.0 - sg)))
        dx = w[3:4] * dc
        for t in range(CONV_W - 1):
            dx = dx + w[t:t + 1] * _shift_up(dc, CONV_W - 1 - t)
        dx_ref[...] = dx.astype(dx_ref.dtype)
        for t in range(CONV_W):
            dw_ref[t:t + 1, :] = jnp.sum(dc * xs[t], axis=0, keepdims=True)

    return pl.pallas_call(
        body, name="gdn_pre_bwd", grid=(nb, n_blk),
        in_specs=[pl.BlockSpec((None, l, LANES), lambda b, j: (b, 0, j)), pl.BlockSpec((CONV_W, LANES), lambda b, j: (0, j)),
                  pl.BlockSpec((None, l, LANES), lambda b, j: (b, 0, j))],
        out_specs=[pl.BlockSpec((None, l, LANES), lambda b, j: (b, 0, j)), pl.BlockSpec((None, CONV_W, LANES), lambda b, j: (b, 0, j))],
        out_shape=[jax.ShapeDtypeStruct((nb, l, GDN_QKV), BF16), jax.ShapeDtypeStruct((nb, CONV_W, GDN_QKV), F32)],
        compiler_params=_params(("parallel", "parallel")),
    )(proj, conv_w, dout)


def _gdn_chunk(st, q, k, v, gate, bl, al, alog, dtb, nw):
    n_g, c, _ = q.shape
    beta = _sigmoid(bl)
    x = al + dtb
    softplus = jnp.maximum(x, 0.0) + jnp.log(1.0 + jnp.exp(-jnp.abs(x)))
    la = -jnp.exp(alog) * softplus
    ri = lax.broadcasted_iota(jnp.int32, (c, c), 0)
    ci = lax.broadcasted_iota(jnp.int32, (c, c), 1)
    lower = (ci <= ri)[None]
    strict = (ci < ri)[None]
    ltri = jnp.broadcast_to(jnp.where(lower, 1.0, 0.0), (n_g, c, c))
    la_l = la + jnp.zeros((1, 1, LANES), F32)
    g = _bmm_hi(ltri, la_l)
    delta = _bmm_hi(ltri, jnp.where(strict, la + jnp.zeros((1, 1, c), F32), 0.0))
    gam = jnp.where(lower, jnp.exp(jnp.where(lower, delta, 0.0)), 0.0)
    kb = k * beta
    vb = v * beta
    m = jnp.where(strict, _bmm_nt(kb, k) * gam, 0.0)
    pw = -m
    t_inv = jnp.where((ri == ci)[None], 1.0, 0.0) + pw
    steps = int(math.log2(c)) - 1
    for _ in range(steps):
        pw = _bmm_hi(pw, pw)
        t_inv = t_inv + _bmm_hi(t_inv, pw)
    eg = jnp.exp(g)
    u = _bmm(t_inv, vb)
    w = _bmm(t_inv, kb * eg)
    attn = jnp.where(lower, _bmm_nt(q, k) * gam, 0.0)
    v_new = u - _bmm_nt(w, st)
    o = _bmm_nt(q * eg, st) + _bmm(attn, v_new)
    g_last = jnp.sum(la_l, axis=1, keepdims=True)
    st_new = st * jnp.exp(g_last) + _bmm_tn(v_new, k * jnp.exp(g_last - g))
    return _head_norm_gate(o, gate, nw), st_new


GDN_WIDE = GDN_HEADS * LANES
GDN_SMALL_BLOCK = (GDN_QKV + GDN_WIDE) // LANES


def _gdn_specs(nb, n_c, rev):
    def cidx(cc):
        return (n_c - 1 - cc) if rev else cc

    def col(width, blk=0):
        return pl.BlockSpec((nb, CHUNK, width), lambda cc: (0, cidx(cc), blk))

    scal = pl.BlockSpec((GDN_HEADS, 1, 1), lambda cc: (0, 0, 0))
    nw = pl.BlockSpec((1, LANES), lambda cc: (0, 0))
    st = pl.BlockSpec((nb, GDN_HEADS, None, LANES, LANES), lambda cc: (0, 0, cidx(cc), 0, 0))
    acc_s = pl.BlockSpec((nb, GDN_HEADS, 1, 1), lambda cc: (0, 0, 0, 0))
    return col, scal, nw, st, acc_s


def _tok_cols(xs, nb, first):
    return jnp.stack([xs[b][:, first + h:first + h + 1] for b in range(nb) for h in range(GDN_HEADS)])


def _gdn_inputs(qkv_ref, p_gate_ref, p_small_ref, alog_ref, dtb_ref, nw_ref, nb):
    q, k, v = (_seq_heads(qkv_ref, nb, GDN_HEADS, i * GDN_HEADS) for i in range(3))
    xs = p_small_ref[...]
    return (q, k, v, _seq_heads(p_gate_ref, nb, GDN_HEADS), _tok_cols(xs, nb, 0), _tok_cols(xs, nb, GDN_HEADS),
            jnp.concatenate([alog_ref[...]] * nb), jnp.concatenate([dtb_ref[...]] * nb), nw_ref[...])


def _gdn_fwd(qkv, proj, alog, dtb, nw):
    nb, l, _ = qkv.shape
    n_c = l // CHUNK
    n_g = nb * GDN_HEADS
    col, scal, nws, st, _ = _gdn_specs(nb, n_c, False)

    def body(qkv_ref, pg_ref, ps_ref, alog_ref, dtb_ref, nw_ref, y_ref, st_ref, s_scr):
        @pl.when(pl.program_id(0) == 0)
        def _():
            s_scr[...] = jnp.zeros_like(s_scr)

        s_in = s_scr[...]
        st_ref[...] = s_in.reshape(st_ref.shape)
        y, s_new = _gdn_chunk(s_in, *_gdn_inputs(qkv_ref, pg_ref, ps_ref, alog_ref, dtb_ref, nw_ref, nb))
        _put_seq_heads(y_ref, y, nb, GDN_HEADS)
        s_scr[...] = s_new

    return pl.pallas_call(
        body, name="gdn_fwd", grid=(n_c,),
        in_specs=[col(GDN_QKV), col(GDN_WIDE, GDN_QKV // GDN_WIDE), col(LANES, GDN_SMALL_BLOCK), scal, scal, nws],
        out_specs=[col(GDN_WIDE), st],
        out_shape=[jax.ShapeDtypeStruct((nb, l, GDN_WIDE), BF16),
                   jax.ShapeDtypeStruct((nb, GDN_HEADS, n_c, LANES, LANES), F32)],
        scratch_shapes=[pltpu.VMEM((n_g, LANES, LANES), F32)],
        compiler_params=_params(("arbitrary",)),
    )(qkv, proj, proj, alog, dtb, nw)


def _gdn_bwd(qkv, proj, alog, dtb, nw, states, dy):
    nb, l, _ = qkv.shape
    n_c = l // CHUNK
    n_g = nb * GDN_HEADS
    col, scal, nws, st, acc_s = _gdn_specs(nb, n_c, True)

    def body(qkv_ref, pg_ref, ps_ref, alog_ref, dtb_ref, nw_ref, st_ref, dy_ref,
             dqkv_ref, dg_ref, dsm_ref, dalog_ref, ddtb_ref, dnw_ref, ds_scr):
        @pl.when(pl.program_id(0) == 0)
        def _():
            ds_scr[...] = jnp.zeros_like(ds_scr)
            dalog_ref[...] = jnp.zeros_like(dalog_ref)
            ddtb_ref[...] = jnp.zeros_like(ddtb_ref)
            dnw_ref[...] = jnp.zeros_like(dnw_ref)

        _, vjp = jax.vjp(_gdn_chunk, st_ref[...].reshape(n_g, LANES, LANES),
                         *_gdn_inputs(qkv_ref, pg_ref, ps_ref, alog_ref, dtb_ref, nw_ref, nb))
        ds, dq, dk, dv, dg, dbl, dal, dalog, ddtb, dnw = vjp((_seq_heads(dy_ref, nb, GDN_HEADS).astype(F32), ds_scr[...]))
        ds_scr[...] = ds
        for i, val in enumerate((dq, dk, dv)):
            _put_seq_heads(dqkv_ref, val, nb, GDN_HEADS, i * GDN_HEADS)
        _put_seq_heads(dg_ref, dg, nb, GDN_HEADS)
        lane = lax.broadcasted_iota(jnp.int32, (CHUNK, LANES), 1)
        for b in range(nb):
            small = jnp.zeros((CHUNK, LANES), F32)
            for h in range(GDN_HEADS):
                small = small + jnp.where(lane == h, dbl[b * GDN_HEADS + h], 0.0)
                small = small + jnp.where(lane == GDN_HEADS + h, dal[b * GDN_HEADS + h], 0.0)
            dsm_ref[b] = small.astype(dsm_ref.dtype)
        dalog_ref[...] += dalog.reshape(dalog_ref.shape)
        ddtb_ref[...] += ddtb.reshape(ddtb_ref.shape)
        dnw_ref[...] += dnw

    h = GDN_HEADS
    return pl.pallas_call(
        body, name="gdn_bwd", grid=(n_c,),
        in_specs=[col(GDN_QKV), col(GDN_WIDE, GDN_QKV // GDN_WIDE), col(LANES, GDN_SMALL_BLOCK), scal, scal, nws, st,
                  col(GDN_WIDE)],
        out_specs=[col(GDN_QKV), col(GDN_WIDE), col(LANES), acc_s, acc_s, nws],
        out_shape=[jax.ShapeDtypeStruct((nb, l, GDN_QKV), F32), jax.ShapeDtypeStruct((nb, l, GDN_WIDE), BF16),
                   jax.ShapeDtypeStruct((nb, l, LANES), BF16),
                   jax.ShapeDtypeStruct((nb, h, 1, 1), F32), jax.ShapeDtypeStruct((nb, h, 1, 1), F32),
                   jax.ShapeDtypeStruct((1, LANES), F32)],
        scratch_shapes=[pltpu.VMEM((n_g, LANES, LANES), F32)],
        compiler_params=_params(("arbitrary",)),
    )(qkv, proj, proj, alog, dtb, nw, states, dy)


S5_ROWS = S5_NSTATE // LANES
S5_TB = 256


def _s5_scan_fwd(bu, abar):
    nb, l = bu.shape[:2]
    tb = min(S5_TB, l)

    def body(bu_ref, a_ref, h_ref, c_scr):
        @pl.when(pl.program_id(1) == 0)
        def _():
            c_scr[...] = jnp.zeros_like(c_scr)

        ar = a_ref[0]
        ai = a_ref[1]

        def step(t, carry):
            hr, hi = carry
            nr = ar * hr - ai * hi + bu_ref[t, 0].astype(F32)
            ni = ar * hi + ai * hr + bu_ref[t, 1].astype(F32)
            h_ref[t, 0] = nr.astype(h_ref.dtype)
            h_ref[t, 1] = ni.astype(h_ref.dtype)
            return nr, ni

        hr, hi = lax.fori_loop(0, tb, step, (c_scr[0], c_scr[1]), unroll=8)
        c_scr[0] = hr
        c_scr[1] = hi

    blk = pl.BlockSpec((None, tb, 2, S5_ROWS, LANES), lambda b, i: (b, i, 0, 0, 0))
    return pl.pallas_call(
        body, name="s5_scan_fwd", grid=(nb, l // tb),
        in_specs=[blk, pl.BlockSpec((2, S5_ROWS, LANES), lambda b, i: (0, 0, 0))],
        out_specs=blk, out_shape=jax.ShapeDtypeStruct(bu.shape, bu.dtype),
        scratch_shapes=[pltpu.VMEM((2, S5_ROWS, LANES), F32)],
        compiler_params=_params(("parallel", "arbitrary")),
    )(bu, abar)


def _s5_scan_bwd(dh, h, abar):
    nb, l = dh.shape[:2]
    tb = min(S5_TB, l)
    n_blk = l // tb

    def body(dh_ref, h_ref, a_ref, g_ref, da_ref, c_scr):
        @pl.when(pl.program_id(1) == 0)
        def _():
            c_scr[...] = jnp.zeros_like(c_scr)
            da_ref[...] = jnp.zeros_like(da_ref)

        ar = a_ref[0]
        ai = a_ref[1]

        def step(s, carry):
            gr, gi, dar, dai = carry
            t = tb - 1 - s
            hr = h_ref[t, 0].astype(F32)
            hi = h_ref[t, 1].astype(F32)
            dar = dar + gr * hr + gi * hi
            dai = dai + gi * hr - gr * hi
            nr = ar * gr + ai * gi + dh_ref[t, 0].astype(F32)
            ni = ar * gi - ai * gr + dh_ref[t, 1].astype(F32)
            g_ref[t, 0] = nr.astype(g_ref.dtype)
            g_ref[t, 1] = ni.astype(g_ref.dtype)
            return nr, ni, dar, dai

        z = jnp.zeros((S5_ROWS, LANES), F32)
        gr, gi, dar, dai = lax.fori_loop(0, tb, step, (c_scr[0], c_scr[1], z, z), unroll=8)
        c_scr[0] = gr
        c_scr[1] = gi
        da_ref[0] += dar
        da_ref[1] += dai

    blk = pl.BlockSpec((None, tb, 2, S5_ROWS, LANES), lambda b, i: (b, n_blk - 1 - i, 0, 0, 0))
    return pl.pallas_call(
        body, name="s5_scan_bwd", grid=(nb, n_blk),
        in_specs=[blk, blk, pl.BlockSpec((2, S5_ROWS, LANES), lambda b, i: (0, 0, 0))],
        out_specs=[blk, pl.BlockSpec((None, 2, S5_ROWS, LANES), lambda b, i: (b, 0, 0, 0))],
        out_shape=[jax.ShapeDtypeStruct(dh.shape, dh.dtype), jax.ShapeDtypeStruct((nb, 2, S5_ROWS, LANES), F32)],
        scratch_shapes=[pltpu.VMEM((2, S5_ROWS, LANES), F32)],
        compiler_params=_params(("parallel", "arbitrary")),
    )(dh, h, abar)


_GELU_C = math.sqrt(2.0 / math.pi)


def _gelu(x):
    return 0.5 * x * (1.0 + jnp.tanh(_GELU_C * (x + 0.044715 * x * x * x)))


def _gelu_grad(x):
    t = jnp.tanh(_GELU_C * (x + 0.044715 * x * x * x))
    return 0.5 * (1.0 + t) + 0.5 * x * (1.0 - t * t) * _GELU_C * (1.0 + 3.0 * 0.044715 * x * x)


def _s5_post_fwd(yc, u, d, w_glu):
    t, w = yc.shape
    tt = _tile(t, (512, 256, 128))

    def body(yc_ref, u_ref, d_ref, w_ref, o_ref, y0_ref):
        y0 = yc_ref[...] + d_ref[...] * u_ref[...]
        y = _gelu(y0)
        z = _dot(y, w_ref[...])
        o_ref[...] = (y * _sigmoid(z)).astype(o_ref.dtype)
        y0_ref[...] = y0

    tok = pl.BlockSpec((tt, w), lambda i: (i, 0))
    return pl.pallas_call(
        body, name="s5_post_fwd", grid=(t // tt,),
        in_specs=[tok, tok, pl.BlockSpec((1, w), lambda i: (0, 0)), pl.BlockSpec((w, w), lambda i: (0, 0))],
        out_specs=[tok, tok],
        out_shape=[jax.ShapeDtypeStruct((t, w), BF16), jax.ShapeDtypeStruct((t, w), F32)],
        compiler_params=_params(("parallel",)),
    )(yc, u, d, w_glu)


def _s5_post_bwd(y0, u, d, w_glu, dout):
    t, w = y0.shape
    tt = _tile(t, (512, 256, 128))

    def body(y0_ref, u_ref, d_ref, w_ref, do_ref, dy0_ref, y_ref, dz_ref, du_ref, dd_ref):
        @pl.when(pl.program_id(0) == 0)
        def _():
            dd_ref[...] = jnp.zeros_like(dd_ref)

        y0 = y0_ref[...]
        y = _gelu(y0)
        s = _sigmoid(_dot(y, w_ref[...]))
        do = do_ref[...]
        dz = do * y * s * (1.0 - s)
        dy = do * s + _dot_nt(dz, w_ref[...])
        dy0 = dy * _gelu_grad(y0)
        dy0_ref[...] = dy0.astype(dy0_ref.dtype)
        y_ref[...] = y.astype(y_ref.dtype)
        dz_ref[...] = dz.astype(dz_ref.dtype)
        du_ref[...] = dy0 * d_ref[...]
        dd_ref[...] += jnp.sum(dy0 * u_ref[...], axis=0, keepdims=True)

    tok = pl.BlockSpec((tt, w), lambda i: (i, 0))
    vec = pl.BlockSpec((1, w), lambda i: (0, 0))
    return pl.pallas_call(
        body, name="s5_post_bwd", grid=(t // tt,),
        in_specs=[tok, tok, vec, pl.BlockSpec((w, w), lambda i: (0, 0)), tok],
        out_specs=[tok, tok, tok, tok, vec],
        out_shape=[jax.ShapeDtypeStruct((t, w), BF16), jax.ShapeDtypeStruct((t, w), BF16),
                   jax.ShapeDtypeStruct((t, w), BF16), jax.ShapeDtypeStruct((t, w), F32),
                   jax.ShapeDtypeStruct((1, w), F32)],
        compiler_params=_params(("arbitrary",)),
    )(y0, u, d, w_glu, dout)


def _s5_params(a_re, a_im, b_re, b_im, c_re, c_im, log_dt):
    dt = jnp.exp(log_dt)[:, None]
    mag = jnp.exp(dt * a_re)
    ang = dt * a_im
    abar_re = mag * jnp.cos(ang)
    abar_im = mag * jnp.sin(ang)
    den = a_re * a_re + a_im * a_im
    zr = abar_re - 1.0
    zi = abar_im
    coef_re = ((zr * a_re + zi * a_im) / den)[..., None]
    coef_im = ((zi * a_re - zr * a_im) / den)[..., None]
    bb_re = coef_re * b_re - coef_im * b_im
    bb_im = coef_re * b_im + coef_im * b_re
    eye = jnp.eye(S5_GROUPS, dtype=F32)

    def dense_in(bb):
        return jnp.einsum('gnp,gh->gphn', bb, eye).reshape(S5_WIDTH, S5_NSTATE)

    def dense_out(cc):
        return jnp.einsum('gpn,gh->gnhp', cc, eye).reshape(S5_NSTATE, S5_WIDTH)

    abar = jnp.stack([abar_re.reshape(S5_ROWS, LANES), abar_im.reshape(S5_ROWS, LANES)])
    bbd = jnp.concatenate([dense_in(bb_re), dense_in(bb_im)], axis=1)
    ccd = jnp.concatenate([dense_out(c_re), dense_out(-c_im)], axis=0)
    return abar, bbd, ccd


def _tok_tile(t):
    return _tile(t, (256, 128))


def _rms_fwd(x, w):
    t, d = x.shape
    tt = _tok_tile(t)

    def body(x_ref, w_ref, o_ref):
        xv = x_ref[...]
        r = lax.rsqrt(jnp.mean(xv * xv, axis=-1, keepdims=True) + NORM_EPS)
        o_ref[...] = (xv * r * w_ref[...]).astype(o_ref.dtype)

    tok = pl.BlockSpec((tt, d), lambda i: (i, 0))
    return pl.pallas_call(
        body, name="rms_fwd", grid=(t // tt,), in_specs=[tok, pl.BlockSpec((1, d), lambda i: (0, 0))],
        out_specs=tok, out_shape=jax.ShapeDtypeStruct((t, d), BF16), compiler_params=_params(("parallel",)),
    )(x, w)


def _rms_bwd_math(xv, wv, dy):
    r = lax.rsqrt(jnp.mean(xv * xv, axis=-1, keepdims=True) + NORM_EPS)
    xh = xv * r
    dxh = dy * wv
    dx = r * (dxh - xh * jnp.mean(dxh * xh, axis=-1, keepdims=True))
    return dx, jnp.sum(dy * xh, axis=0, keepdims=True)


def _rms_bwd(x, w, dy, dres):
    t, d = x.shape
    tt = _tok_tile(t)

    def body(x_ref, w_ref, dy_ref, dr_ref, dx_ref, dw_ref):
        @pl.when(pl.program_id(0) == 0)
        def _():
            dw_ref[...] = jnp.zeros_like(dw_ref)

        dx, dw = _rms_bwd_math(x_ref[...], w_ref[...], dy_ref[...])
        dx_ref[...] = dr_ref[...] + dx
        dw_ref[...] += dw

    tok = pl.BlockSpec((tt, d), lambda i: (i, 0))
    vec = pl.BlockSpec((1, d), lambda i: (0, 0))
    return pl.pallas_call(
        body, name="rms_bwd", grid=(t // tt,), in_specs=[tok, vec, tok, tok], out_specs=[tok, vec],
        out_shape=[jax.ShapeDtypeStruct((t, d), F32), jax.ShapeDtypeStruct((1, d), F32)],
        compiler_params=_params(("arbitrary",)),
    )(x, w, dy, dres)


def _loss_head(x, w, target):
    t, d = x.shape
    tt = _tok_tile(t)

    def body(x_ref, w_ref, t_ref, l_ref, dx_ref, dw_ref):
        @pl.when(pl.program_id(0) == 0)
        def _():
            l_ref[...] = jnp.zeros_like(l_ref)
            dw_ref[...] = jnp.zeros_like(dw_ref)

        xv = x_ref[...]
        wv = w_ref[...]
        r = lax.rsqrt(jnp.mean(xv * xv, axis=-1, keepdims=True) + NORM_EPS)
        err = xv * r * wv - t_ref[...]
        row = jnp.sum(err * err, axis=-1, keepdims=True)
        l_ref[...] += (0.5 / d) * jnp.sum(row, axis=0, keepdims=True)
        dx, dw = _rms_bwd_math(xv, wv, err * (1.0 / d))
        dx_ref[...] = dx
        dw_ref[...] += dw

    tok = pl.BlockSpec((tt, d), lambda i: (i, 0))
    vec = pl.BlockSpec((1, d), lambda i: (0, 0))
    return pl.pallas_call(
        body, name="loss_head", grid=(t // tt,), in_specs=[tok, vec, tok],
        out_specs=[pl.BlockSpec((1, 1), lambda i: (0, 0)), tok, vec],
        out_shape=[jax.ShapeDtypeStruct((1, 1), F32), jax.ShapeDtypeStruct((t, d), F32), jax.ShapeDtypeStruct((1, d), F32)],
        compiler_params=_params(("arbitrary",)),
    )(x, w, target)


FF_PAD = 3072
FF_SHARD = D_FF // N_DEV
FF_SHARD_PAD = FF_PAD // N_DEV


def _ffn_in(x, nw, wg_t, wu_t):
    t, d = x.shape
    tm = _tile(t, _TOKEN_TILES)
    tn = 1024

    def body(x_ref, nw_ref, wg_ref, wu_ref, h_ref, dadg_ref, dadu_ref, a_ref):
        @pl.when(pl.program_id(1) == 0)
        def _():
            xv = x_ref[...]
            r = lax.rsqrt(jnp.mean(xv * xv, axis=-1, keepdims=True) + NORM_EPS)
            h_ref[...] = (xv * r * nw_ref[...]).astype(h_ref.dtype)

        h = h_ref[...]
        g = _dot_nt(h, wg_ref[...])
        u = _dot_nt(h, wu_ref[...])
        s = _sigmoid(g)
        silu = g * s
        dadg_ref[...] = (u * (s + silu * (1.0 - s))).astype(dadg_ref.dtype)
        dadu_ref[...] = silu.astype(dadu_ref.dtype)
        a_ref[...] = (silu * u).astype(a_ref.dtype)

    tok = pl.BlockSpec((tm, d), lambda i, j: (i, 0))
    wsp = pl.BlockSpec((tn, d), lambda i, j: (j, 0))
    wide = pl.BlockSpec((tm, tn), lambda i, j: (i, j))
    return pl.pallas_call(
        body, name="ffn_in", grid=(t // tm, FF_PAD // tn),
        in_specs=[tok, pl.BlockSpec((1, d), lambda i, j: (0, 0)), wsp, wsp], out_specs=[tok, wide, wide, wide],
        out_shape=[jax.ShapeDtypeStruct((t, d), BF16)] + [jax.ShapeDtypeStruct((t, FF_PAD), BF16)] * 3,
        compiler_params=_params(("parallel", "arbitrary")),
    )(x, nw, wg_t, wu_t)


def _ffn_out_dx(dxo, wd, dadg, dadu, after=None):
    t, d = dxo.shape
    tm = _tile(t, _TOKEN_TILES)
    tn = 1024

    def body(dx_ref, wd_ref, dadg_ref, dadu_ref, *rest):
        dg_ref, du_ref = rest[-2:]
        da = 0.5 * _dot_nt(dx_ref[...], wd_ref[...])
        dg_ref[...] = (da * dadg_ref[...].astype(F32)).astype(dg_ref.dtype)
        du_ref[...] = (da * dadu_ref[...].astype(F32)).astype(du_ref.dtype)

    wide = pl.BlockSpec((tm, tn), lambda j, i: (i, j))
    return pl.pallas_call(
        body, name="ffn_out_dx", grid=(FF_PAD // tn, t // tm),
        in_specs=[pl.BlockSpec((tm, d), lambda j, i: (i, 0)), pl.BlockSpec((tn, d), lambda j, i: (j, 0)), wide, wide]
        + [pl.BlockSpec(memory_space=pl.ANY)] * (after is not None),
        out_specs=[wide, wide], out_shape=[jax.ShapeDtypeStruct((t, FF_PAD), BF16)] * 2,
        compiler_params=_params(("parallel", "parallel")),
    )(dxo, wd, dadg, dadu, *([after] if after is not None else []))


def _ffn_in_dx(dg, du, wg_t, wu_t, x, nw, dxo):
    t, d = x.shape
    tm = _tile(t, (256, 128))

    def body(dg_ref, du_ref, wg_ref, wu_ref, x_ref, nw_ref, dr_ref, dx_ref, dw_ref):
        @pl.when(pl.program_id(0) == 0)
        def _():
            dw_ref[...] = jnp.zeros_like(dw_ref)

        dh = _dot(dg_ref[...], wg_ref[...]) + _dot(du_ref[...], wu_ref[...])
        dx, dw = _rms_bwd_math(x_ref[...], nw_ref[...], dh)
        dx_ref[...] = dr_ref[...] + dx
        dw_ref[...] += dw

    wide = pl.BlockSpec((tm, FF_PAD), lambda i: (i, 0))
    wsp = pl.BlockSpec((FF_PAD, d), lambda i: (0, 0))
    tok = pl.BlockSpec((tm, d), lambda i: (i, 0))
    vec = pl.BlockSpec((1, d), lambda i: (0, 0))
    return pl.pallas_call(
        body, name="ffn_in_dx", grid=(t // tm,), in_specs=[wide, wide, wsp, wsp, tok, vec, tok], out_specs=[tok, vec],
        out_shape=[jax.ShapeDtypeStruct((t, d), F32), jax.ShapeDtypeStruct((1, d), F32)],
        compiler_params=_params(("arbitrary",)),
    )(dg, du, wg_t, wu_t, x, nw, dxo)


def _adamw(w, g, m, v):
    r, c = w.shape
    tr = _tile(r, (512, 256, 128, 64, 32, 16, 8))
    c1 = 1.0 - ADAM_B1 ** ADAM_STEP
    c2 = 1.0 - ADAM_B2 ** ADAM_STEP

    def body(w_ref, g_ref, m_ref, v_ref, d_ref, nm_ref, nv_ref):
        gv = g_ref[...]
        nm = ADAM_B1 * m_ref[...] + (1.0 - ADAM_B1) * gv
        nv = ADAM_B2 * v_ref[...] + (1.0 - ADAM_B2) * (gv * gv)
        d_ref[...] = -ADAM_LR * ((nm / c1) / (jnp.sqrt(nv / c2) + ADAM_EPS) + ADAM_WD * w_ref[...])
        nm_ref[...] = nm
        nv_ref[...] = nv

    blk = pl.BlockSpec((tr, c), lambda i: (i, 0))
    return pl.pallas_call(
        body, name="adamw", grid=(r // tr,), in_specs=[blk] * 4, out_specs=[blk] * 3,
        out_shape=[jax.ShapeDtypeStruct((r, c), F32)] * 3, compiler_params=_params(("parallel",)),
    )(w, g, m, v)


N_KINDS = 3


def _ffn_slot(which, kind):
    return which * N_KINDS + kind


def _ffn_fwd(x, nw, w_ffn, which):
    wg, wu, wd = (w_ffn[_ffn_slot(which, kind)] for kind in range(3))
    h, dadg, dadu, a = _ffn_in(x, nw, wg, wu)
    return _mm(a, wd, name="ffn_out", res=x, scale=0.5, token_tiles=_TOKEN_TILES[1:]), (x, h, dadg, dadu, a)


def _ffn_bwd(dxo, res, nw, w_ffn, which, after=None):
    x, h, dadg, dadu, a = res
    wg, wu, wd = (w_ffn[_ffn_slot(which, kind)] for kind in range(3))
    dg, du = _ffn_out_dx(dxo, wd, dadg, dadu, after)
    g_ffn = lax.empty((N_KINDS,) + wd.shape, BF16)
    g_ffn = _mm_tn(a, dxo, name="ffn_out_dw", scale=0.5, into=g_ffn, slot=2)
    g_ffn = _mm_tn(dg, h, name="ffn_in_dw", into=g_ffn, slot=0)
    g_ffn = _mm_tn(du, h, name="ffn_in_dw", into=g_ffn, slot=1)
    dx, dnw = _ffn_in_dx(dg, du, wg, wu, x, nw, dxo)
    return dx, dnw, g_ffn


def _hg_lower_bounds(logits):
    p = jax.nn.softmax(logits, axis=0)
    return jnp.cumsum(p, axis=0) - p[0]


def _even_fwd(x1, p, nb, l):
    t = nb * l
    h = _rms_fwd(x1, p["mix_norm"])
    proj = _mm(h, p["w_in_t"], name="ev_in", nt=True)
    proj3 = proj.reshape(nb, l, EV_IN)
    ya, hst = _hgrn2_fwd(proj3, p["lb"], p["hg_nw"])
    u = proj[:, 4 * HG_QK:]
    bu = _mm(u, p["bbd"], name="s5_in", out_dtype=BF16)
    hs = _s5_scan_fwd(bu.reshape(nb, l, 2, S5_ROWS, LANES), p["abar"])
    yc = _mm(hs.reshape(t, 2 * S5_NSTATE), p["ccd"], name="s5_out")
    yb, y0 = _s5_post_fwd(yc, u, p["s5_d"], p["w_glu"])
    ycat = jnp.concatenate([ya.reshape(t, HG_QK), yb], axis=1)
    x2 = _mm(ycat, p["w_out"], name="mix_out", res=x1)
    return x2, (x1, h, proj3, hst, u, hs, y0, ycat)


def _even_bwd(dx2, res, p, nb, l):
    x1, h, proj3, hst, u, hs, y0, ycat = res
    t = nb * l
    dycat = _mm(dx2, p["w_out"], name="mix_out_dx", nt=True)
    dw_out = _mm_tn(ycat, dx2, name="mix_out_dw", out_dtype=BF16)
    dp_hg, dlb, dhg_nw = _hgrn2_bwd(proj3, p["lb"], p["hg_nw"], hst, dycat.reshape(nb, l, D_MODEL))
    dy0, y, dz, du_d, dd = _s5_post_bwd(y0, u, p["s5_d"], p["w_glu"], dycat[:, HG_QK:])
    dw_glu = _mm_tn(y, dz, name="s5_glu_dw", out_dtype=BF16)
    dhs = _mm(dy0, p["ccd"], name="s5_out_dx", nt=True, out_dtype=BF16)
    dccd = _mm_tn(hs.reshape(t, 2 * S5_NSTATE), dy0, name="s5_out_dw")
    g5, dabar = _s5_scan_bwd(dhs.reshape(nb, l, 2, S5_ROWS, LANES), hs, p["abar"])
    g2 = g5.reshape(t, 2 * S5_NSTATE)
    du = _mm(g2, p["bbd"], name="s5_in_dx", nt=True, res=du_d, out_dtype=BF16)
    dbbd = _mm_tn(u, g2, name="s5_in_dw")
    dproj = jnp.concatenate([dp_hg.reshape(t, 4 * HG_QK), du], axis=1)
    dw_in_t = _mm_tn(dproj, h, name="ev_in_dw", out_dtype=BF16)
    dh = _mm(dproj, p["w_in_t"], name="ev_in_dx")
    dx1, dmix = _rms_bwd(x1, p["mix_norm"], dh, dx2)
    small = dict(mix_norm=dmix, lb=dlb.sum(0).reshape(HG_QK), hg_nw=dhg_nw.reshape(HEAD_DIM),
                 abar=dabar.sum(0), bbd=dbbd, ccd=dccd, s5_d=dd.reshape(S5_WIDTH))
    return dx1, dict(w_in_t=dw_in_t, w_glu=dw_glu, w_out=dw_out), small


def _odd_fwd(x1, p, nb, l):
    t = nb * l
    h = _rms_fwd(x1, p["mix_norm"])
    proj3 = _mm(h, p["w_in"], name="od_in").reshape(nb, l, OD_IN_PAD)
    qkv = _gdn_pre_fwd(proj3, p["conv_w"])
    y, st = _gdn_fwd(qkv, proj3, p["a_log"], p["dt_bias"], p["gdn_nw"])
    y2 = y.reshape(t, D_MODEL)
    x2 = _mm(y2, p["w_out"], name="mix_out", res=x1)
    return x2, (x1, h, proj3, qkv, st, y2)


def _odd_bwd(dx2, res, p, nb, l):
    x1, h, proj3, qkv, st, y2 = res
    t = nb * l
    dy = _mm(dx2, p["w_out"], name="mix_out_dx", nt=True)
    dw_out = _mm_tn(y2, dx2, name="mix_out_dw", out_dtype=BF16)
    dqkv, dgate, dsmall, dalog, ddtb, dnw = _gdn_bwd(
        qkv, proj3, p["a_log"], p["dt_bias"], p["gdn_nw"], st, dy.reshape(nb, l, D_MODEL))
    dx_qkv, dcw = _gdn_pre_bwd(proj3, p["conv_w"], dqkv)
    dproj = jnp.concatenate([dx_qkv, dgate, dsmall], axis=-1).reshape(t, OD_IN_PAD)
    dw_in = _mm_tn(h, dproj, name="od_in_dw", out_dtype=BF16)[:, :OD_IN]
    dh = _mm(dproj, p["w_in"], name="od_in_dx", nt=True)
    dx1, dmix = _rms_bwd(x1, p["mix_norm"], dh, dx2)
    small = dict(mix_norm=dmix, a_log=dalog.sum(0).reshape(GDN_HEADS), dt_bias=ddtb.sum(0).reshape(GDN_HEADS),
                 gdn_nw=dnw.reshape(HEAD_DIM))
    return dx1, dict(w_in=dw_in, conv_w=dcw.sum(0), w_out=dw_out), small


def _local_step(x, target, layer_weights, small, layer_done):
    nb, l, _ = x.shape
    t = nb * l
    lbs, lbs_vjp = jax.vjp(_hg_lower_bounds, small["hg_lb_logits"])
    row = lambda v: v.reshape(1, -1)
    s5_vjps = {}

    def layer_params(layer, big):
        j = layer // 2
        p = dict(mix_norm=row(small["mix_norm"][layer]), w_out=big["w_out"])
        if layer % 2 == 0:
            (abar, bbd, ccd), s5_vjps[j] = jax.vjp(
                _s5_params, small["s5_a_re"][j], small["s5_a_im"][j], small["s5_b_re"][j], small["s5_b_im"][j],
                small["s5_c_re"][j], small["s5_c_im"][j], small["s5_log_dt"][j])
            p.update(w_in_t=big["w_in_t"], w_glu=big["w_glu"],
                     lb=lbs[j].reshape(HG_HEADS, 1, HEAD_DIM), hg_nw=row(small["hg_norm_w"][j]),
                     abar=abar, bbd=bbd.astype(BF16), ccd=ccd.astype(BF16), s5_d=row(small["s5_d"][j]))
        else:
            p.update(w_in=big["w_in"], conv_w=big["conv_w"],
                     a_log=small["gdn_a_log"][j].reshape(GDN_HEADS, 1, 1),
                     dt_bias=small["gdn_dt_bias"][j].reshape(GDN_HEADS, 1, 1), gdn_nw=row(small["gdn_norm_w"][j]))
        return p

    xs = x.reshape(t, D_MODEL)
    saved, layers, ffn_w = [], [], []
    for layer in range(DEPTH):
        big = layer_weights(layer, xs)
        p = layer_params(layer, big)
        xs, r1 = _ffn_fwd(xs, row(small["ffn1_norm"][layer]), big["ffn"], 0)
        xs, r2 = (_even_fwd if layer % 2 == 0 else _odd_fwd)(xs, p, nb, l)
        xs, r3 = _ffn_fwd(xs, row(small["ffn2_norm"][layer]), big["ffn"], 1)
        saved.append((r1, r2, r3))
        layers.append(p)
        ffn_w.append(big["ffn"])
    loss, dx, dfinal = _loss_head(xs, row(small["final_norm"]), target.reshape(t, D_MODEL))

    gs = {k: [None] * DEPTH for k in ("ffn1_norm", "mix_norm", "ffn2_norm")}
    gs.update({k: [None] * 2 for k in ("hg_norm_w", "s5_a_re", "s5_a_im", "s5_b_re", "s5_b_im", "s5_c_re", "s5_c_im",
                                       "s5_d", "s5_log_dt", "gdn_a_log", "gdn_dt_bias", "gdn_norm_w")})
    dlbs = [None] * 2
    token = None
    for layer in reversed(range(DEPTH)):
        j = layer // 2
        p = layers[layer]
        r1, r2, r3 = saved[layer]
        dx, dnw, g_ffn = _ffn_bwd(dx, r3, row(small["ffn2_norm"][layer]), ffn_w[layer], 1, token)
        gs["ffn2_norm"][layer] = dnw[0]
        if layer % 2 == 0:
            dx, gw, sm = _even_bwd(dx, r2, p, nb, l)
            dlbs[j] = sm["lb"]
            gs["hg_norm_w"][j] = sm["hg_nw"]
            gs["s5_d"][j] = sm["s5_d"]
            (gs["s5_a_re"][j], gs["s5_a_im"][j], gs["s5_b_re"][j], gs["s5_b_im"][j], gs["s5_c_re"][j],
             gs["s5_c_im"][j], gs["s5_log_dt"][j]) = s5_vjps[j]((sm["abar"], sm["bbd"], sm["ccd"]))
        else:
            dx, gw, sm = _odd_bwd(dx, r2, p, nb, l)
            gs["gdn_a_log"][j], gs["gdn_dt_bias"][j], gs["gdn_norm_w"][j] = sm["a_log"], sm["dt_bias"], sm["gdn_nw"]
        gs["mix_norm"][layer] = sm["mix_norm"][0]
        token = layer_done(layer, 0, dict(gw, ffn=g_ffn))
        dx, dnw, g_ffn = _ffn_bwd(dx, r1, row(small["ffn1_norm"][layer]), ffn_w[layer], 0, token)
        gs["ffn1_norm"][layer] = dnw[0]
        token = layer_done(layer, 1, dict(ffn=g_ffn))
    gsmall = {k: jnp.stack(v) for k, v in gs.items()}
    gsmall["hg_lb_logits"] = lbs_vjp(jnp.stack(dlbs))[0]
    gsmall["final_norm"] = dfinal[0]
    return loss[0, 0], dx.reshape(nb, l, D_MODEL), gsmall


def _here():
    return lax.axis_index("x"), lax.axis_index("y"), lax.axis_index("c")


def _other_chips(x, y):
    return [(1 - x, y), (x, 1 - y), (1 - x, 1 - y)]


def _all_gather(blocks, name):
    n = len(blocks)

    def body(*refs):
        x_refs, o_refs = refs[:n], refs[n:2 * n]
        send_sems, recv_sems, local_sems = refs[2 * n:]
        x, y, cc = _here()
        me, sibling = (x, y, cc), (x, y, 1 - cc)
        chips = _other_chips(x, y)

        def win(i, p):
            return o_refs[i].at[:, 4 * p[0] + 2 * p[1] + p[2]]

        def copy(i, k, blk, to, src=None):
            return pltpu.make_async_remote_copy(
                src_ref=win(i, blk) if src is None else src, dst_ref=win(i, blk),
                send_sem=send_sems.at[i, k], recv_sem=recv_sems.at[i, k], device_id=to, device_id_type=MESH)

        mine = [pltpu.make_async_copy(x_refs[i], win(i, me), local_sems.at[i]) for i in range(n)]
        for cp in mine:
            cp.start()
        first = []
        for i in range(n):
            first.append(copy(i, 0, me, sibling, src=x_refs[i]))
            first += [copy(i, 1 + j, me, (*chip, cc), src=x_refs[i]) for j, chip in enumerate(chips)]
        for cp in first:
            cp.start()
        passed = []
        for j, chip in enumerate(chips):
            for i in range(n):
                copy(i, 1 + j, (*chip, cc), me).wait_recv()
                fwd = copy(i, 4 + j, (*chip, cc), sibling)
                fwd.start()
                passed.append(fwd)
        for i in range(n):
            copy(i, 0, sibling, me).wait_recv()
            for j, chip in enumerate(chips):
                copy(i, 4 + j, (*chip, 1 - cc), me).wait_recv()
        for cp in first + passed:
            cp.wait_send()
        for cp in mine:
            cp.wait()

    hbm = pl.BlockSpec(memory_space=pl.ANY)
    return pl.pallas_call(
        body, name=name, in_specs=[hbm] * n, out_specs=[hbm] * n,
        out_shape=[jax.ShapeDtypeStruct((b.shape[0], N_DEV) + b.shape[1:], b.dtype) for b in blocks],
        scratch_shapes=[pltpu.SemaphoreType.DMA((n, 7)), pltpu.SemaphoreType.DMA((n, 7)), pltpu.SemaphoreType.DMA((n,))],
    )(*blocks)


_HBM = pl.BlockSpec(memory_space=pltpu.HBM)
_SEM = pl.BlockSpec(memory_space=pltpu.SEMAPHORE)
_DATAFLOW = pltpu.SideEffectType.DATAFLOW_SIDE_EFFECTING


def _peer(k):
    x, y, c = _here()
    return (x ^ ((k >> 2) & 1), y ^ ((k >> 1) & 1), c ^ (k & 1))


def _device_of(p):
    return 4 * p[0] + 2 * p[1] + p[2]


_OTHER_CHIPS = (4, 2, 6)


def _exchange_start(srcs, lands, copies, name, after=None):
    n = len(srcs)
    n_in = 2 * n + (after is not None)

    def body(*refs):
        s_refs, l_refs, sems = refs[:n], refs[n:2 * n], refs[n_in:n_in + 2 * n]
        for i in range(n):
            for slot, k in enumerate(_OTHER_CHIPS):
                src, dst = copies(s_refs[i], l_refs[i], k, slot)
                pltpu.make_async_remote_copy(
                    src_ref=src, dst_ref=dst, send_sem=sems[2 * i], recv_sem=sems[2 * i + 1],
                    device_id=_peer(k), device_id_type=MESH).start()
        refs[-1][...] = jnp.zeros_like(refs[-1])

    both = list(srcs) + list(lands)
    outs = pl.pallas_call(
        body, name=name,
        out_shape=tuple([pltpu.SemaphoreType.DMA(())] * (2 * n)) + tuple(pltpu.HBM(a.shape, a.dtype) for a in both)
        + (jax.ShapeDtypeStruct((8, LANES), F32),),
        in_specs=[_HBM] * (2 * n) + [pl.BlockSpec(memory_space=pl.ANY)] * (after is not None),
        out_specs=tuple([_SEM] * (2 * n)) + tuple([_HBM] * (2 * n)) + (pl.BlockSpec(memory_space=pltpu.VMEM),),
        input_output_aliases={i: 2 * n + i for i in range(2 * n)},
        compiler_params=pltpu.CompilerParams(has_side_effects=_DATAFLOW),
    )(*[pltpu.with_memory_space_constraint(a, pltpu.HBM) for a in both], *([after] if after is not None else []))
    return list(outs[:2 * n]), list(outs[2 * n:3 * n]), list(outs[3 * n:4 * n]), outs[4 * n]


def _exchange_wait(handle, three, after, name):
    sems, srcs, lands, _ = handle
    n = len(srcs)

    def body(*refs):
        l_refs, sem_refs = refs[n:2 * n], refs[2 * n:4 * n]
        for i in range(n):
            all_three = three(l_refs[i])
            cp = pltpu.make_async_remote_copy(
                src_ref=all_three, dst_ref=all_three, send_sem=sem_refs[2 * i], recv_sem=sem_refs[2 * i + 1],
                device_id=_peer(_OTHER_CHIPS[0]), device_id_type=MESH)
            cp.wait_send()
            cp.wait_recv()

    both = list(srcs) + list(lands)
    outs = pl.pallas_call(
        body, name=name, out_shape=tuple(pltpu.HBM(a.shape, a.dtype) for a in both),
        in_specs=[_HBM] * (2 * n) + [_SEM] * (2 * n) + [pl.BlockSpec(memory_space=pl.ANY)],
        out_specs=tuple([_HBM] * (2 * n)), input_output_aliases={i: i for i in range(2 * n)},
        compiler_params=pltpu.CompilerParams(has_side_effects=_DATAFLOW),
    )(*both, *sems, after)
    return list(outs[:n]), list(outs[n:])


def _gather_copies(src, land, k, slot):
    del k, slot
    return src, land.at[:, _device_of(_here())]


def _gather_three(land):
    return land.at[:, pl.ds(0, len(_OTHER_CHIPS))]


def _scatter_copies(src, land, k, slot):
    p = _peer(k)
    return src.at[2 * p[0] + p[1]], land.at[slot]


def _scatter_three(land):
    return land


def _core_forward(lands, blocks):
    n = len(lands)

    def body(*refs):
        l_refs, b_refs = refs[:n], refs[n:2 * n]
        send_sems, recv_sems, local_sems = refs[3 * n:]
        x, y, cc = _here()
        mine = [(x, y)] + [_peer(k)[:2] for k in _OTHER_CHIPS]
        own = [pltpu.make_async_copy(b_refs[i], l_refs[i].at[:, _device_of((x, y, cc))], local_sems.at[i]) for i in range(n)]
        for cp in own:
            cp.start()
        cps = []
        for i in range(n):
            for j, (px, py) in enumerate(mine):
                give = l_refs[i].at[:, _device_of((px, py, cc))]
                take = l_refs[i].at[:, _device_of((px, py, 1 - cc))]
                cps.append((pltpu.make_async_remote_copy(
                    src_ref=b_refs[i] if j == 0 else give, dst_ref=give, send_sem=send_sems.at[i, j],
                    recv_sem=recv_sems.at[i, j], device_id=(x, y, 1 - cc), device_id_type=MESH), take))
        for cp, _ in cps:
            cp.start()
        for i in range(n):
            for j, (cp, take) in enumerate(cps[4 * i:4 * i + 4]):
                cp.wait_send()
                pltpu.make_async_remote_copy(
                    src_ref=take, dst_ref=take, send_sem=send_sems.at[i, j], recv_sem=recv_sems.at[i, j],
                    device_id=(x, y, 1 - cc), device_id_type=MESH).wait_recv()
        for cp in own:
            cp.wait()

    hbm = pl.BlockSpec(memory_space=pl.ANY)
    return pl.pallas_call(
        body, name="gather_core_forward", in_specs=[hbm] * (2 * n), out_specs=[hbm] * n,
        out_shape=[jax.ShapeDtypeStruct(a.shape, a.dtype) for a in lands],
        input_output_aliases={i: i for i in range(n)},
        scratch_shapes=[pltpu.SemaphoreType.DMA((n, 4)), pltpu.SemaphoreType.DMA((n, 4)), pltpu.SemaphoreType.DMA((n,))],
    )(*lands, *blocks)


def _rs_core_exchange(gs):
    n = len(gs)

    def body(*refs):
        g_refs, o_refs = refs[:n], refs[n:2 * n]
        send_sems, recv_sems = refs[2 * n:]
        x, y, cc = _here()
        cps = [pltpu.make_async_remote_copy(
            src_ref=g_refs[i].at[:, 2 * j + (1 - cc)], dst_ref=o_refs[i].at[j],
            send_sem=send_sems.at[i, j], recv_sem=recv_sems.at[i, j], device_id=(x, y, 1 - cc), device_id_type=MESH)
            for i in range(n) for j in range(4)]
        for cp in cps:
            cp.start()
        for cp in cps:
            cp.wait()

    hbm = pl.BlockSpec(memory_space=pl.ANY)
    return pl.pallas_call(
        body, name="rs_core_exchange", in_specs=[hbm] * n, out_specs=[hbm] * n,
        out_shape=[jax.ShapeDtypeStruct((4, g.shape[0]) + g.shape[2:], g.dtype) for g in gs],
        scratch_shapes=[pltpu.SemaphoreType.DMA((n, 4)), pltpu.SemaphoreType.DMA((n, 4))],
    )(*gs)


def _row_tile(w):
    return w if w <= 512 else _tile(w, (512, 256, 128))


def _rs_chip_sum(g, r1):
    a, _, w, c = g.shape
    tr = _row_tile(w)

    def body(g_ref, r_ref, o_ref):
        o_ref[...] = (g_ref[...].astype(F32) + r_ref[...].astype(F32)).astype(o_ref.dtype)

    blk = pl.BlockSpec((None, None, tr, c), lambda j, aa, i: (j, aa, i, 0))
    return pl.pallas_call(
        body, name="rs_chip_sum", grid=(4, a, w // tr),
        in_specs=[pl.BlockSpec((None, None, tr, c), lambda j, aa, i: (aa, 2 * j + lax.axis_index("c"), i, 0)), blk],
        out_specs=blk, out_shape=jax.ShapeDtypeStruct((4, a, w, c), g.dtype),
        compiler_params=_params(("parallel", "parallel", "parallel")),
    )(g, r1)


def _rs_final_sum(g, r1, r2):
    a, _, w, c = g.shape
    tr = _row_tile(w)

    def body(g_ref, r1_ref, r2_ref, o_ref):
        s = g_ref[...].astype(F32) + r1_ref[...].astype(F32)
        for k in range(3):
            s = s + r2_ref[k].astype(F32)
        o_ref[...] = s

    def device():
        return 4 * lax.axis_index("x") + 2 * lax.axis_index("y") + lax.axis_index("c")

    def chip():
        return 2 * lax.axis_index("x") + lax.axis_index("y")

    return pl.pallas_call(
        body, name="rs_final_sum", grid=(a, w // tr),
        in_specs=[pl.BlockSpec((None, None, tr, c), lambda aa, i: (aa, device(), i, 0)),
                  pl.BlockSpec((None, None, tr, c), lambda aa, i: (chip(), aa, i, 0)),
                  pl.BlockSpec((3, None, tr, c), lambda aa, i: (0, aa, i, 0))],
        out_specs=pl.BlockSpec((None, tr, c), lambda aa, i: (aa, i, 0)),
        out_shape=jax.ShapeDtypeStruct((a, w, c), F32),
        compiler_params=_params(("parallel", "parallel")),
    )(g, r1, r2)


def _sum_slots(g):
    _, n, r, c = g.shape

    def body(g_ref, o_ref):
        s = g_ref[0, 0]
        for k in range(1, n):
            s = s + g_ref[0, k]
        o_ref[...] = s

    return pl.pallas_call(
        body, name="sum_slots", grid=(1,),
        in_specs=[pl.BlockSpec((1, n, r, c), lambda i: (0, 0, 0, 0))], out_specs=pl.BlockSpec((r, c), lambda i: (0, 0)),
        out_shape=jax.ShapeDtypeStruct((r, c), F32), compiler_params=_params(("arbitrary",)),
    )(g)


_WEIGHTS = ['ffn1_norm', 'ffn1_w_gate', 'ffn1_w_up', 'ffn1_w_down', 'mix_norm', 'ffn2_norm', 'ffn2_w_gate', 'ffn2_w_up',
            'ffn2_w_down', 'ev_w_in', 'hg_lb_logits', 'hg_norm_w', 's5_a_re', 's5_a_im', 's5_b_re', 's5_b_im', 's5_c_re',
            's5_c_im', 's5_d', 's5_log_dt', 's5_w_glu', 'ev_w_out', 'od_w_in', 'gdn_conv_w', 'gdn_a_log', 'gdn_dt_bias',
            'gdn_norm_w', 'od_w_out', 'final_norm']
_SHARDED = ['ffn1_w_gate', 'ffn1_w_up', 'ffn1_w_down', 'ffn2_w_gate', 'ffn2_w_up', 'ffn2_w_down', 'ev_w_in', 's5_w_glu',
            'ev_w_out', 'od_w_in', 'gdn_conv_w', 'od_w_out']
_REPLICATED = [n for n in _WEIGHTS if n not in _SHARDED]
_SMALL_COLS = 1024


def _pad_rows(a, rows):
    return jnp.pad(a, ((0, rows - a.shape[0]), (0, 0)))


def _layer_blocks(w, layer):
    j = layer // 2
    ffn = []
    for f in ('ffn1', 'ffn2'):
        ffn += [_pad_rows(w[f + '_w_gate'][layer].T, FF_SHARD_PAD), _pad_rows(w[f + '_w_up'][layer].T, FF_SHARD_PAD),
                _pad_rows(w[f + '_w_down'][layer], FF_SHARD_PAD)]
    ffn = jnp.stack(ffn).astype(BF16)
    if layer % 2 == 0:
        return [ffn, w['ev_w_in'][j].T.astype(BF16)[None], w['s5_w_glu'][j].astype(BF16)[None],
                w['ev_w_out'][j].astype(BF16)[None]]
    return [ffn, w['od_w_in'][j].astype(BF16)[None], w['gdn_conv_w'][j][None], w['od_w_out'][j].astype(BF16)[None]]


def _layer_whole(gathered, layer):
    whole = lambda g: g.reshape(g.shape[0], N_DEV * g.shape[2], g.shape[3])
    ffn, a1, a2, w_out = gathered
    if layer % 2 == 0:
        return dict(ffn=whole(ffn), w_in_t=whole(a1)[0], w_glu=whole(a2)[0], w_out=whole(w_out)[0])
    w_in = jnp.moveaxis(a1[0], 0, 1).reshape(D_MODEL, OD_IN)
    conv = jnp.moveaxis(a2[0], 0, 1).reshape(CONV_W, GDN_QKV)
    return dict(ffn=whole(ffn), w_in=jnp.pad(w_in, ((0, 0), (0, OD_IN_PAD - OD_IN))), conv_w=conv, w_out=whole(w_out)[0])


def _layer_grad_windows(g, layer):
    windows = lambda a, w: a.reshape(-1, N_DEV, w, a.shape[-1])
    ffn = windows(g['ffn'], FF_SHARD_PAD)
    if 'w_out' not in g:
        return [ffn]
    w_out = windows(g['w_out'], D_MODEL // N_DEV)
    if layer % 2 == 0:
        return [ffn, windows(g['w_in_t'], EV_IN // N_DEV), windows(g['w_glu'], S5_WIDTH // N_DEV), w_out]
    w_in = jnp.moveaxis(g['w_in'].reshape(D_MODEL, N_DEV, OD_IN // N_DEV), 1, 0)[None]
    conv = jnp.moveaxis(g['conv_w'].reshape(CONV_W, N_DEV, GDN_QKV // N_DEV), 1, 0)[None]
    return [ffn, w_in, conv, w_out]


def _flat_small(src, prefix=''):
    flat = jnp.concatenate([src[prefix + n].reshape(-1) for n in _REPLICATED])
    rows = -(-(flat.shape[0] + 1) // (_SMALL_COLS * 8)) * 8
    return jnp.pad(flat, (0, rows * _SMALL_COLS - flat.shape[0])).reshape(rows, _SMALL_COLS)


def _split_small(flat2d, like):
    flat, out, off = flat2d.reshape(-1), {}, 0
    for n in _REPLICATED:
        size = math.prod(like[n].shape)
        out[n] = flat[off:off + size].reshape(like[n].shape)
        off += size
    return out, flat[off]


def _all_reduce_small(gsmall, loss):
    flat = _flat_small(gsmall)
    n_used = sum(math.prod(gsmall[n].shape) for n in _REPLICATED)
    flat = flat.reshape(-1).at[n_used].set(loss).reshape(flat.shape)
    (gathered,) = _all_gather([flat[None]], "gather_small")
    return _split_small(_sum_slots(gathered), gsmall)


def _as_2d(a):
    return a.reshape(-1, a.shape[-1])


def kernel(x, ffn1_norm, ffn1_w_gate, ffn1_w_up, ffn1_w_down, mix_norm, ffn2_norm, ffn2_w_gate, ffn2_w_up, ffn2_w_down, ev_w_in, hg_lb_logits, hg_norm_w, s5_a_re, s5_a_im, s5_b_re, s5_b_im, s5_c_re, s5_c_im, s5_d, s5_log_dt, s5_w_glu, ev_w_out, od_w_in, gdn_conv_w, gdn_a_log, gdn_dt_bias, gdn_norm_w, od_w_out, final_norm, loss_target, m_ffn1_norm, m_ffn1_w_gate, m_ffn1_w_up, m_ffn1_w_down, m_mix_norm, m_ffn2_norm, m_ffn2_w_gate, m_ffn2_w_up, m_ffn2_w_down, m_ev_w_in, m_hg_lb_logits, m_hg_norm_w, m_s5_a_re, m_s5_a_im, m_s5_b_re, m_s5_b_im, m_s5_c_re, m_s5_c_im, m_s5_d, m_s5_log_dt, m_s5_w_glu, m_ev_w_out, m_od_w_in, m_gdn_conv_w, m_gdn_a_log, m_gdn_dt_bias, m_gdn_norm_w, m_od_w_out, m_final_norm, v_ffn1_norm, v_ffn1_w_gate, v_ffn1_w_up, v_ffn1_w_down, v_mix_norm, v_ffn2_norm, v_ffn2_w_gate, v_ffn2_w_up, v_ffn2_w_down, v_ev_w_in, v_hg_lb_logits, v_hg_norm_w, v_s5_a_re, v_s5_a_im, v_s5_b_re, v_s5_b_im, v_s5_c_re, v_s5_c_im, v_s5_d, v_s5_log_dt, v_s5_w_glu, v_ev_w_out, v_od_w_in, v_gdn_conv_w, v_gdn_a_log, v_gdn_dt_bias, v_gdn_norm_w, v_od_w_out, v_final_norm):
    given = dict(locals())
    w = {n: given[n] for n in _WEIGHTS}
    small = {n: w[n] for n in _REPLICATED}
    blocks = [_layer_blocks(w, layer) for layer in range(DEPTH)]
    first = _all_gather(blocks[0], "gather_weights")
    pending = {}

    def start_gather(layer, after=None):
        lands = [lax.empty((b.shape[0], N_DEV) + b.shape[1:], b.dtype) for b in blocks[layer]]
        pending[layer] = _exchange_start(blocks[layer], lands, _gather_copies, "gather_start_%d" % layer, after)

    start_gather(1)

    def layer_weights(layer, after):
        if layer == 0:
            return _layer_whole(first, 0)
        mine, got = _exchange_wait(pending[layer], _gather_three, after, "gather_wait_%d" % layer)
        if layer + 1 < DEPTH:
            start_gather(layer + 1, after)
        return _layer_whole(_core_forward(got, mine), layer)

    sent = {}

    def layer_done(layer, part, grads):
        gs = _layer_grad_windows(grads, layer)
        from_sibling = _rs_core_exchange(gs)
        chip = [_rs_chip_sum(g, r) for g, r in zip(gs, from_sibling)]
        lands = [lax.empty((len(_OTHER_CHIPS),) + p.shape[1:], p.dtype) for p in chip]
        handle = _exchange_start(chip, lands, _scatter_copies, "scatter_start_%d_%d" % (layer, part))
        sent[layer, part] = (gs, from_sibling, handle)
        return handle[3]

    loss_part, grad_x, gsmall = _local_step(given['x'], given['loss_target'], layer_weights, small, layer_done)
    gsum, loss = _all_reduce_small(gsmall, loss_part)
    summed = {}
    for layer in reversed(range(DEPTH)):
        for part in range(2):
            gs, from_sibling, handle = sent[layer, part]
            _, lands = _exchange_wait(handle, _scatter_three, grad_x, "scatter_wait_%d_%d" % (layer, part))
            summed[layer, part] = [_rs_final_sum(g, r, land) for g, r, land in zip(gs, from_sibling, lands)]

    grads = dict(gsum)
    t_ = lambda a: jnp.swapaxes(a, 1, 2)
    for i, f in enumerate(('ffn1', 'ffn2')):
        ffn = jnp.stack([summed[layer, 1 - i][0][:, :FF_SHARD] for layer in range(DEPTH)], axis=1)
        grads[f + '_w_gate'], grads[f + '_w_up'], grads[f + '_w_down'] = t_(ffn[0]), t_(ffn[1]), ffn[2]
    even, odd = [summed[0, 0], summed[2, 0]], [summed[1, 0], summed[3, 0]]
    grads['ev_w_in'] = jnp.stack([r[1][0].T for r in even])
    grads['s5_w_glu'] = jnp.stack([r[2][0] for r in even])
    grads['ev_w_out'] = jnp.stack([r[3][0] for r in even])
    grads['od_w_in'] = jnp.stack([r[1][0] for r in odd])
    grads['gdn_conv_w'] = jnp.stack([r[2][0] for r in odd])
    grads['od_w_out'] = jnp.stack([r[3][0] for r in odd])

    delta, new_m, new_v = {}, {}, {}
    for n in _SHARDED:
        d, nm, nv = _adamw(_as_2d(w[n]), _as_2d(grads[n]), _as_2d(given['m_' + n]), _as_2d(given['v_' + n]))
        delta[n], new_m[n], new_v[n] = (a.reshape(w[n].shape) for a in (d, nm, nv))
    d, nm, nv = _adamw(_flat_small(w), _flat_small(grads), _flat_small(given, 'm_'), _flat_small(given, 'v_'))
    (delta_s, _), (new_m_s, _), (new_v_s, _) = (_split_small(a, small) for a in (d, nm, nv))
    delta.update(delta_s)
    new_m.update(new_m_s)
    new_v.update(new_v_s)
    return (loss, grad_x, *[grads[n] for n in _WEIGHTS], *[delta[n] for n in _WEIGHTS],
            *[new_m[n] for n in _WEIGHTS], *[new_v[n] for n in _WEIGHTS])
```

```python
import math

import jax
import jax.numpy as jnp
from jax import lax
from jax.experimental import pallas as pl
from jax.experimental.pallas import tpu as pltpu

F32 = jnp.float32
BF16 = jnp.bfloat16
HI = lax.Precision.HIGH

D_MODEL = 1024
DEPTH = 4
D_FF = 2816
NORM_EPS = 1e-6
F_MIN = 1e-6
CHUNK = 64
HG_HEADS = 4
HEAD_DIM = 128
HG_QK = HG_HEADS * HEAD_DIM
S5_WIDTH = 512
S5_GROUP = 16
S5_GROUPS = 32
S5_STATE = 64
S5_NSTATE = S5_GROUPS * S5_STATE
EV_IN = 2560
GDN_HEADS = 8
GDN_QKV = 3 * GDN_HEADS * HEAD_DIM
CONV_W = 4
OD_IN = 4112
OD_IN_PAD = 4224
N_DEV = 8
LANES = 128
ADAM_LR, ADAM_B1, ADAM_B2, ADAM_EPS, ADAM_WD, ADAM_STEP = 0.001, 0.9, 0.999, 1e-08, 0.01, 10
VMEM_LIMIT = 56 * 1024 * 1024

MESH = pl.DeviceIdType.MESH


def _params(sem=None, **kw):
    return pltpu.CompilerParams(dimension_semantics=sem, vmem_limit_bytes=VMEM_LIMIT, **kw)


def _tile(n, cands):
    for c in cands:
        if n % c == 0:
            return c
    return n


def _dot(a, b):
    return jnp.dot(a.astype(BF16), b.astype(BF16), preferred_element_type=F32)


def _dot_nt(a, b):
    return lax.dot_general(a.astype(BF16), b.astype(BF16), (((1,), (1,)), ((), ())), preferred_element_type=F32)


def _dot_tn(a, b):
    return lax.dot_general(a.astype(BF16), b.astype(BF16), (((0,), (0,)), ((), ())), preferred_element_type=F32)


def _dot_hi(a, b):
    return jnp.dot(a, b, precision=HI, preferred_element_type=F32)


def _bmm(a, b):
    return jnp.einsum('gmk,gkn->gmn', a.astype(BF16), b.astype(BF16), preferred_element_type=F32)


def _bmm_nt(a, b):
    return jnp.einsum('gmk,gnk->gmn', a.astype(BF16), b.astype(BF16), preferred_element_type=F32)


def _bmm_tn(a, b):
    return jnp.einsum('gkm,gkn->gmn', a.astype(BF16), b.astype(BF16), preferred_element_type=F32)


def _bmm_hi(a, b):
    return jnp.einsum('gmk,gkn->gmn', a, b, precision=HI, preferred_element_type=F32)


_TN_CANDS = (1408, 1280, 1024, 512, 384, 256, 128)
_B_TILE_BYTES = 6 * 1024 * 1024
_TOKEN_TILES = (1024, 512, 256, 128)


def _mm(a, b, *, name, nt=False, out_dtype=F32, res=None, scale=1.0, token_tiles=None):
    m, k = a.shape
    n = b.shape[0] if nt else b.shape[1]
    tm = _tile(m, token_tiles or _TOKEN_TILES)
    tn = _tile(n, [c for c in _TN_CANDS if c * k * b.dtype.itemsize <= _B_TILE_BYTES])

    def body(*refs):
        if res is None:
            a_ref, b_ref, o_ref = refs
        else:
            a_ref, b_ref, r_ref, o_ref = refs
        acc = (_dot_nt if nt else _dot)(a_ref[...], b_ref[...])
        if scale != 1.0:
            acc = scale * acc
        if res is not None:
            acc = r_ref[...] + acc
        o_ref[...] = acc.astype(out_dtype)

    b_spec = pl.BlockSpec((tn, k), lambda j, i: (j, 0)) if nt else pl.BlockSpec((k, tn), lambda j, i: (0, j))
    in_specs = [pl.BlockSpec((tm, k), lambda j, i: (i, 0)), b_spec]
    args = [a, b]
    if res is not None:
        in_specs.append(pl.BlockSpec((tm, tn), lambda j, i: (i, j)))
        args.append(res)
    return pl.pallas_call(
        body, name=name, grid=(n // tn, m // tm), in_specs=in_specs,
        out_specs=pl.BlockSpec((tm, tn), lambda j, i: (i, j)),
        out_shape=jax.ShapeDtypeStruct((m, n), out_dtype),
        compiler_params=_params(("parallel", "parallel")),
    )(*args)


def _mm_tn(a, b, *, name, scale=1.0, out_dtype=F32, into=None, slot=0):
    t, m = a.shape
    n = b.shape[1]
    tm = _tile(m, (1024, 512, 256, 128))
    tn = _tile(n, _TN_CANDS)
    tk = _tile(t, (2048, 1024, 512, 256, 128))
    nk = t // tk

    def body(*refs):
        a_ref, b_ref = refs[:2]
        o_ref, acc_ref = refs[-2:]
        kk = pl.program_id(2)

        @pl.when(kk == 0)
        def _():
            acc_ref[...] = jnp.zeros_like(acc_ref)

        acc_ref[...] += _dot_tn(a_ref[...], b_ref[...])

        @pl.when(kk == nk - 1)
        def _():
            o_ref[...] = (acc_ref[...] * scale if scale != 1.0 else acc_ref[...]).astype(o_ref.dtype)

    in_specs = [pl.BlockSpec((tk, tm), lambda i, j, kk: (kk, i)), pl.BlockSpec((tk, tn), lambda i, j, kk: (kk, j))]
    args = [a, b]
    if into is None:
        out_spec = pl.BlockSpec((tm, tn), lambda i, j, kk: (i, j))
        out_shape = jax.ShapeDtypeStruct((m, n), out_dtype)
        alias = {}
    else:
        in_specs.append(pl.BlockSpec(memory_space=pl.ANY))
        args.append(into)
        out_spec = pl.BlockSpec((None, tm, tn), lambda i, j, kk: (slot, i, j))
        out_shape = jax.ShapeDtypeStruct(into.shape, into.dtype)
        alias = {2: 0}
    return pl.pallas_call(
        body, name=name, grid=(m // tm, n // tn, nk), in_specs=in_specs, out_specs=out_spec, out_shape=out_shape,
        scratch_shapes=[pltpu.VMEM((tm, tn), F32)], input_output_aliases=alias,
        compiler_params=_params(("parallel", "parallel", "arbitrary")),
    )(*args)


def _sigmoid(x):
    return jax.nn.sigmoid(x)


def _head_norm_gate(o, gate, nw):
    r = lax.rsqrt(jnp.mean(o * o, axis=-1, keepdims=True) + NORM_EPS)
    return o * r * nw * (gate * _sigmoid(gate))


def _hg_chunk(st, ql, fl, iv, gl, lb, nw):
    n_g, c, _ = ql.shape
    q = ql * _sigmoid(ql)
    f = lb + (1.0 - lb) * _sigmoid(fl)
    lf = jnp.log(jnp.maximum(f, F_MIN))
    k = 1.0 - f
    ri = lax.broadcasted_iota(jnp.int32, (c, c), 0)
    ci = lax.broadcasted_iota(jnp.int32, (c, c), 1)
    rl = lax.broadcasted_iota(jnp.int32, (c, LANES), 0)
    each = lambda m2: jnp.broadcast_to(m2[None], (n_g,) + m2.shape)
    b = _bmm_hi(each(jnp.where(ci <= ri, 1.0, 0.0)), lf)
    attn = jnp.where((ri == ci)[None], _bmm_nt(q, k), 0.0)
    sh = 0
    while (1 << sh) < c:
        m = 1 << sh
        ref = ((ri >> (sh + 1)) << (sh + 1)) + (m - 1)
        low_r = ((ri >> sh) & 1) == 1
        low_c = ((ci >> sh) & 1) == 1
        w_low = low_r & (ci > ref) & (ci <= ri)
        w_up = jnp.logical_not(low_r) & (ci > ri) & (ci <= ref)
        w = jnp.where(w_low | w_up, 1.0, 0.0)
        e = jnp.exp(_bmm_hi(each(w), lf))
        low_l = (((rl >> sh) & 1) == 1)[None]
        qs = jnp.where(low_l, q * e, 0.0)
        ks = jnp.where(low_l, 0.0, k * e)
        pair = ((ri >> (sh + 1)) == (ci >> (sh + 1))) & low_r & jnp.logical_not(low_c)
        attn = attn + jnp.where(pair[None], _bmm_nt(qs, ks), 0.0)
        sh += 1
    bl = jnp.sum(lf, axis=1, keepdims=True)
    o = _bmm(attn, iv) + _bmm_nt(q * jnp.exp(b), st)
    st_new = st * jnp.exp(bl) + _bmm_tn(iv, k * jnp.exp(bl - b))
    return _head_norm_gate(o, gl, nw), st_new


def _hg_specs(nb, n_c, rev):
    def cidx(cc):
        return (n_c - 1 - cc) if rev else cc

    def col(width):
        return pl.BlockSpec((nb, CHUNK, width), lambda cc: (0, cidx(cc), 0))

    vec = pl.BlockSpec((HG_HEADS, 1, LANES), lambda cc: (0, 0, 0))
    nw = pl.BlockSpec((1, LANES), lambda cc: (0, 0))
    st = pl.BlockSpec((nb, HG_HEADS, None, LANES, LANES), lambda cc: (0, 0, cidx(cc), 0, 0))
    acc = pl.BlockSpec((nb, HG_HEADS, 1, LANES), lambda cc: (0, 0, 0, 0))
    return col, vec, nw, st, acc


def _head(j):
    return slice(j * LANES, (j + 1) * LANES)


def _seq_heads(ref, nb, n, first=0):
    return jnp.stack([ref[b, :, _head(first + j)] for b in range(nb) for j in range(n)])


def _put_seq_heads(ref, val, nb, n, first=0):
    for b in range(nb):
        for j in range(n):
            ref[b, :, _head(first + j)] = val[b * n + j].astype(ref.dtype)


def _hgrn2_fwd(proj, lb, nw):
    nb, l, _ = proj.shape
    n_c = l // CHUNK
    n_g = nb * HG_HEADS
    col, vec, nws, st, _ = _hg_specs(nb, n_c, False)

    def body(p_ref, lb_ref, nw_ref, y_ref, st_ref, s_scr):
        @pl.when(pl.program_id(0) == 0)
        def _():
            s_scr[...] = jnp.zeros_like(s_scr)

        s_in = s_scr[...]
        st_ref[...] = s_in.reshape(st_ref.shape)
        args = [_seq_heads(p_ref, nb, HG_HEADS, k * HG_HEADS) for k in range(4)]
        y, s_new = _hg_chunk(s_in, *args, jnp.concatenate([lb_ref[...]] * nb), nw_ref[...])
        _put_seq_heads(y_ref, y, nb, HG_HEADS)
        s_scr[...] = s_new

    return pl.pallas_call(
        body, name="hgrn2_fwd", grid=(n_c,),
        in_specs=[col(4 * HG_QK), vec, nws], out_specs=[col(HG_QK), st],
        out_shape=[jax.ShapeDtypeStruct((nb, l, HG_QK), BF16),
                   jax.ShapeDtypeStruct((nb, HG_HEADS, n_c, LANES, LANES), F32)],
        scratch_shapes=[pltpu.VMEM((n_g, LANES, LANES), F32)],
        compiler_params=_params(("arbitrary",)),
    )(proj, lb, nw)


def _hgrn2_bwd(proj, lb, nw, states, dy):
    nb, l, _ = proj.shape
    n_c = l // CHUNK
    n_g = nb * HG_HEADS
    col, vec, nws, st, acc = _hg_specs(nb, n_c, True)

    def body(p_ref, lb_ref, nw_ref, st_ref, dy_ref, dp_ref, dlb_ref, dnw_ref, ds_scr):
        @pl.when(pl.program_id(0) == 0)
        def _():
            ds_scr[...] = jnp.zeros_like(ds_scr)
            dlb_ref[...] = jnp.zeros_like(dlb_ref)
            dnw_ref[...] = jnp.zeros_like(dnw_ref)

        args = [_seq_heads(p_ref, nb, HG_HEADS, k * HG_HEADS) for k in range(4)]
        _, vjp = jax.vjp(_hg_chunk, st_ref[...].reshape(n_g, LANES, LANES), *args,
                         jnp.concatenate([lb_ref[...]] * nb), nw_ref[...])
        ds, dql, dfl, div, dgl, dlb, dnw = vjp((_seq_heads(dy_ref, nb, HG_HEADS).astype(F32), ds_scr[...]))
        ds_scr[...] = ds
        for k, val in enumerate((dql, dfl, div, dgl)):
            _put_seq_heads(dp_ref, val, nb, HG_HEADS, k * HG_HEADS)
        dlb_ref[...] += dlb.reshape(dlb_ref.shape)
        dnw_ref[...] += dnw

    return pl.pallas_call(
        body, name="hgrn2_bwd", grid=(n_c,),
        in_specs=[col(4 * HG_QK), vec, nws, st, col(HG_QK)], out_specs=[col(4 * HG_QK), acc, nws],
        out_shape=[jax.ShapeDtypeStruct((nb, l, 4 * HG_QK), BF16),
                   jax.ShapeDtypeStruct((nb, HG_HEADS, 1, LANES), F32), jax.ShapeDtypeStruct((1, LANES), F32)],
        scratch_shapes=[pltpu.VMEM((n_g, LANES, LANES), F32)],
        compiler_params=_params(("arbitrary",)),
    )(proj, lb, nw, states, dy)


def _shift_down(x, s):
    if s == 0:
        return x
    rows = lax.broadcasted_iota(jnp.int32, x.shape, 0)
    return jnp.where(rows >= s, pltpu.roll(x, s, 0), 0.0)


def _shift_up(x, s):
    if s == 0:
        return x
    n = x.shape[0]
    rows = lax.broadcasted_iota(jnp.int32, x.shape, 0)
    return jnp.where(rows < n - s, pltpu.roll(x, n - s, 0), 0.0)


def _gdn_pre_fwd(proj, conv_w):
    nb, l, _ = proj.shape
    n_blk = GDN_QKV // LANES

    def body(x_ref, w_ref, o_ref):
        j = pl.program_id(1)
        x = x_ref[...]
        w = w_ref[...]
        c = w[3:4] * x
        for t in range(CONV_W - 1):
            c = c + w[t:t + 1] * _shift_down(x, CONV_W - 1 - t)
        s = c * _sigmoid(c)
        r = lax.rsqrt(jnp.sum(s * s, axis=-1, keepdims=True) + NORM_EPS)
        scale = jnp.where(j < GDN_HEADS, HEAD_DIM ** -0.5, 1.0)
        o_ref[...] = jnp.where(j < 2 * GDN_HEADS, s * r * scale, s)

    return pl.pallas_call(
        body, name="gdn_pre_fwd", grid=(nb, n_blk),
        in_specs=[pl.BlockSpec((None, l, LANES), lambda b, j: (b, 0, j)), pl.BlockSpec((CONV_W, LANES), lambda b, j: (0, j))],
        out_specs=pl.BlockSpec((None, l, LANES), lambda b, j: (b, 0, j)),
        out_shape=jax.ShapeDtypeStruct((nb, l, GDN_QKV), F32),
        compiler_params=_params(("parallel", "parallel")),
    )(proj, conv_w)


def _gdn_pre_bwd(proj, conv_w, dout):
    nb, l, _ = proj.shape
    n_blk = GDN_QKV // LANES

    def body(x_ref, w_ref, d_ref, dx_ref, dw_ref):
        j = pl.program_id(1)
        x = x_ref[...]
        w = w_ref[...]
        xs = [_shift_down(x, CONV_W - 1 - t) for t in range(CONV_W)]
        c = w[0:1] * xs[0]
        for t in range(1, CONV_W):
            c = c + w[t:t + 1] * xs[t]
        sg = _sigmoid(c)
        s = c * sg
        d = d_ref[...]
        r = lax.rsqrt(jnp.sum(s * s, axis=-1, keepdims=True) + NORM_EPS)
        scale = jnp.where(j < GDN_HEADS, HEAD_DIM ** -0.5, 1.0)
        ds_norm = scale * r * (d - s * (r * r) * jnp.sum(d * s, axis=-1, keepdims=True))
        ds = jnp.where(j < 2 * GDN_HEADS, ds_norm, d)
        dc = ds * (sg * (1.0 + c * (1.0 - sg)))
        dx = w[3:4] * dc
        for t in range(CONV_W - 1):
            dx = dx + w[t:t + 1] * _shift_up(dc, CONV_W - 1 - t)
        dx_ref[...] = dx.astype(dx_ref.dtype)
        for t in range(CONV_W):
            dw_ref[t:t + 1, :] = jnp.sum(dc * xs[t], axis=0, keepdims=True)

    return pl.pallas_call(
        body, name="gdn_pre_bwd", grid=(nb, n_blk),
        in_specs=[pl.BlockSpec((None, l, LANES), lambda b, j: (b, 0, j)), pl.BlockSpec((CONV_W, LANES), lambda b, j: (0, j)),
                  pl.BlockSpec((None, l, LANES), lambda b, j: (b, 0, j))],
        out_specs=[pl.BlockSpec((None, l, LANES), lambda b, j: (b, 0, j)), pl.BlockSpec((None, CONV_W, LANES), lambda b, j: (b, 0, j))],
        out_shape=[jax.ShapeDtypeStruct((nb, l, GDN_QKV), BF16), jax.ShapeDtypeStruct((nb, CONV_W, GDN_QKV), F32)],
        compiler_params=_params(("parallel", "parallel")),
    )(proj, conv_w, dout)


def _gdn_chunk(st, q, k, v, gate, bl, al, alog, dtb, nw):
    n_g, c, _ = q.shape
    beta = _sigmoid(bl)
    x = al + dtb
    softplus = jnp.maximum(x, 0.0) + jnp.log(1.0 + jnp.exp(-jnp.abs(x)))
    la = -jnp.exp(alog) * softplus
    ri = lax.broadcasted_iota(jnp.int32, (c, c), 0)
    ci = lax.broadcasted_iota(jnp.int32, (c, c), 1)
    lower = (ci <= ri)[None]
    strict = (ci < ri)[None]
    ltri = jnp.broadcast_to(jnp.where(lower, 1.0, 0.0), (n_g, c, c))
    la_l = la + jnp.zeros((1, 1, LANES), F32)
    g = _bmm_hi(ltri, la_l)
    delta = _bmm_hi(ltri, jnp.where(strict, la + jnp.zeros((1, 1, c), F32), 0.0))
    gam = jnp.where(lower, jnp.exp(jnp.where(lower, delta, 0.0)), 0.0)
    kb = k * beta
    vb = v * beta
    m = jnp.where(strict, _bmm_nt(kb, k) * gam, 0.0)
    pw = -m
    t_inv = jnp.where((ri == ci)[None], 1.0, 0.0) + pw
    steps = int(math.log2(c)) - 1
    for _ in range(steps):
        pw = _bmm_hi(pw, pw)
        t_inv = t_inv + _bmm_hi(t_inv, pw)
    eg = jnp.exp(g)
    u = _bmm(t_inv, vb)
    w = _bmm(t_inv, kb * eg)
    attn = jnp.where(lower, _bmm_nt(q, k) * gam, 0.0)
    v_new = u - _bmm_nt(w, st)
    o = _bmm_nt(q * eg, st) + _bmm(attn, v_new)
    g_last = jnp.sum(la_l, axis=1, keepdims=True)
    st_new = st * jnp.exp(g_last) + _bmm_tn(v_new, k * jnp.exp(g_last - g))
    return _head_norm_gate(o, gate, nw), st_new


GDN_WIDE = GDN_HEADS * LANES
GDN_SMALL_BLOCK = (GDN_QKV + GDN_WIDE) // LANES


def _gdn_specs(nb, n_c, rev):
    def cidx(cc):
        return (n_c - 1 - cc) if rev else cc

    def col(width, blk=0):
        return pl.BlockSpec((nb, CHUNK, width), lambda cc: (0, cidx(cc), blk))

    scal = pl.BlockSpec((GDN_HEADS, 1, 1), lambda cc: (0, 0, 0))
    nw = pl.BlockSpec((1, LANES), lambda cc: (0, 0))
    st = pl.BlockSpec((nb, GDN_HEADS, None, LANES, LANES), lambda cc: (0, 0, cidx(cc), 0, 0))
    acc_s = pl.BlockSpec((nb, GDN_HEADS, 1, 1), lambda cc: (0, 0, 0, 0))
    return col, scal, nw, st, acc_s


def _tok_cols(xs, nb, first):
    return jnp.stack([xs[b][:, first + h:first + h + 1] for b in range(nb) for h in range(GDN_HEADS)])


def _gdn_inputs(qkv_ref, p_gate_ref, p_small_ref, alog_ref, dtb_ref, nw_ref, nb):
    q, k, v = (_seq_heads(qkv_ref, nb, GDN_HEADS, i * GDN_HEADS) for i in range(3))
    xs = p_small_ref[...]
    return (q, k, v, _seq_heads(p_gate_ref, nb, GDN_HEADS), _tok_cols(xs, nb, 0), _tok_cols(xs, nb, GDN_HEADS),
            jnp.concatenate([alog_ref[...]] * nb), jnp.concatenate([dtb_ref[...]] * nb), nw_ref[...])


def _gdn_fwd(qkv, proj, alog, dtb, nw):
    nb, l, _ = qkv.shape
    n_c = l // CHUNK
    n_g = nb * GDN_HEADS
    col, scal, nws, st, _ = _gdn_specs(nb, n_c, False)

    def body(qkv_ref, pg_ref, ps_ref, alog_ref, dtb_ref, nw_ref, y_ref, st_ref, s_scr):
        @pl.when(pl.program_id(0) == 0)
        def _():
            s_scr[...] = jnp.zeros_like(s_scr)

        s_in = s_scr[...]
        st_ref[...] = s_in.reshape(st_ref.shape)
        y, s_new = _gdn_chunk(s_in, *_gdn_inputs(qkv_ref, pg_ref, ps_ref, alog_ref, dtb_ref, nw_ref, nb))
        _put_seq_heads(y_ref, y, nb, GDN_HEADS)
        s_scr[...] = s_new

    return pl.pallas_call(
        body, name="gdn_fwd", grid=(n_c,),
        in_specs=[col(GDN_QKV), col(GDN_WIDE, GDN_QKV // GDN_WIDE), col(LANES, GDN_SMALL_BLOCK), scal, scal, nws],
        out_specs=[col(GDN_WIDE), st],
        out_shape=[jax.ShapeDtypeStruct((nb, l, GDN_WIDE), BF16),
                   jax.ShapeDtypeStruct((nb, GDN_HEADS, n_c, LANES, LANES), F32)],
        scratch_shapes=[pltpu.VMEM((n_g, LANES, LANES), F32)],
        compiler_params=_params(("arbitrary",)),
    )(qkv, proj, proj, alog, dtb, nw)


def _gdn_bwd(qkv, proj, alog, dtb, nw, states, dy):
    nb, l, _ = qkv.shape
    n_c = l // CHUNK
    n_g = nb * GDN_HEADS
    col, scal, nws, st, acc_s = _gdn_specs(nb, n_c, True)

    def body(qkv_ref, pg_ref, ps_ref, alog_ref, dtb_ref, nw_ref, st_ref, dy_ref,
             dqkv_ref, dg_ref, dsm_ref, dalog_ref, ddtb_ref, dnw_ref, ds_scr):
        @pl.when(pl.program_id(0) == 0)
        def _():
            ds_scr[...] = jnp.zeros_like(ds_scr)
            dalog_ref[...] = jnp.zeros_like(dalog_ref)
            ddtb_ref[...] = jnp.zeros_like(ddtb_ref)
            dnw_ref[...] = jnp.zeros_like(dnw_ref)

        _, vjp = jax.vjp(_gdn_chunk, st_ref[...].reshape(n_g, LANES, LANES),
                         *_gdn_inputs(qkv_ref, pg_ref, ps_ref, alog_ref, dtb_ref, nw_ref, nb))
        ds, dq, dk, dv, dg, dbl, dal, dalog, ddtb, dnw = vjp((_seq_heads(dy_ref, nb, GDN_HEADS).astype(F32), ds_scr[...]))
        ds_scr[...] = ds
        for i, val in enumerate((dq, dk, dv)):
            _put_seq_heads(dqkv_ref, val, nb, GDN_HEADS, i * GDN_HEADS)
        _put_seq_heads(dg_ref, dg, nb, GDN_HEADS)
        lane = lax.broadcasted_iota(jnp.int32, (CHUNK, LANES), 1)
        for b in range(nb):
            small = jnp.zeros((CHUNK, LANES), F32)
            for h in range(GDN_HEADS):
                small = small + jnp.where(lane == h, dbl[b * GDN_HEADS + h], 0.0)
                small = small + jnp.where(lane == GDN_HEADS + h, dal[b * GDN_HEADS + h], 0.0)
            dsm_ref[b] = small.astype(dsm_ref.dtype)
        dalog_ref[...] += dalog.reshape(dalog_ref.shape)
        ddtb_ref[...] += ddtb.reshape(ddtb_ref.shape)
        dnw_ref[...] += dnw

    h = GDN_HEADS
    return pl.pallas_call(
        body, name="gdn_bwd", grid=(n_c,),
        in_specs=[col(GDN_QKV), col(GDN_WIDE, GDN_QKV // GDN_WIDE), col(LANES, GDN_SMALL_BLOCK), scal, scal, nws, st,
                  col(GDN_WIDE)],
        out_specs=[col(GDN_QKV), col(GDN_WIDE), col(LANES), acc_s, acc_s, nws],
        out_shape=[jax.ShapeDtypeStruct((nb, l, GDN_QKV), F32), jax.ShapeDtypeStruct((nb, l, GDN_WIDE), BF16),
                   jax.ShapeDtypeStruct((nb, l, LANES), BF16),
                   jax.ShapeDtypeStruct((nb, h, 1, 1), F32), jax.ShapeDtypeStruct((nb, h, 1, 1), F32),
                   jax.ShapeDtypeStruct((1, LANES), F32)],
        scratch_shapes=[pltpu.VMEM((n_g, LANES, LANES), F32)],
        compiler_params=_params(("arbitrary",)),
    )(qkv, proj, proj, alog, dtb, nw, states, dy)


S5_ROWS = S5_NSTATE // LANES
S5_TB = 256


def _s5_scan_fwd(bu, abar):
    nb, l = bu.shape[:2]
    tb = min(S5_TB, l)

    def body(bu_ref, a_ref, h_ref, c_scr):
        @pl.when(pl.program_id(1) == 0)
        def _():
            c_scr[...] = jnp.zeros_like(c_scr)

        ar = a_ref[0]
        ai = a_ref[1]

        def step(t, carry):
            hr, hi = carry
            nr = ar * hr - ai * hi + bu_ref[t, 0].astype(F32)
            ni = ar * hi + ai * hr + bu_ref[t, 1].astype(F32)
            h_ref[t, 0] = nr.astype(h_ref.dtype)
            h_ref[t, 1] = ni.astype(h_ref.dtype)
            return nr, ni

        hr, hi = lax.fori_loop(0, tb, step, (c_scr[0], c_scr[1]), unroll=8)
        c_scr[0] = hr
        c_scr[1] = hi

    blk = pl.BlockSpec((None, tb, 2, S5_ROWS, LANES), lambda b, i: (b, i, 0, 0, 0))
    return pl.pallas_call(
        body, name="s5_scan_fwd", grid=(nb, l // tb),
        in_specs=[blk, pl.BlockSpec((2, S5_ROWS, LANES), lambda b, i: (0, 0, 0))],
        out_specs=blk, out_shape=jax.ShapeDtypeStruct(bu.shape, bu.dtype),
        scratch_shapes=[pltpu.VMEM((2, S5_ROWS, LANES), F32)],
        compiler_params=_params(("parallel", "arbitrary")),
    )(bu, abar)


def _s5_scan_bwd(dh, h, abar):
    nb, l = dh.shape[:2]
    tb = min(S5_TB, l)
    n_blk = l // tb

    def body(dh_ref, h_ref, a_ref, g_ref, da_ref, c_scr):
        @pl.when(pl.program_id(1) == 0)
        def _():
            c_scr[...] = jnp.zeros_like(c_scr)
            da_ref[...] = jnp.zeros_like(da_ref)

        ar = a_ref[0]
        ai = a_ref[1]

        def step(s, carry):
            gr, gi, dar, dai = carry
            t = tb - 1 - s
            hr = h_ref[t, 0].astype(F32)
            hi = h_ref[t, 1].astype(F32)
            dar = dar + gr * hr + gi * hi
            dai = dai + gi * hr - gr * hi
            nr = ar * gr + ai * gi + dh_ref[t, 0].astype(F32)
            ni = ar * gi - ai * gr + dh_ref[t, 1].astype(F32)
            g_ref[t, 0] = nr.astype(g_ref.dtype)
            g_ref[t, 1] = ni.astype(g_ref.dtype)
            return nr, ni, dar, dai

        z = jnp.zeros((S5_ROWS, LANES), F32)
        gr, gi, dar, dai = lax.fori_loop(0, tb, step, (c_scr[0], c_scr[1], z, z), unroll=8)
        c_scr[0] = gr
        c_scr[1] = gi
        da_ref[0] += dar
        da_ref[1] += dai

    blk = pl.BlockSpec((None, tb, 2, S5_ROWS, LANES), lambda b, i: (b, n_blk - 1 - i, 0, 0, 0))
    return pl.pallas_call(
        body, name="s5_scan_bwd", grid=(nb, n_blk),
        in_specs=[blk, blk, pl.BlockSpec((2, S5_ROWS, LANES), lambda b, i: (0, 0, 0))],
        out_specs=[blk, pl.BlockSpec((None, 2, S5_ROWS, LANES), lambda b, i: (b, 0, 0, 0))],
        out_shape=[jax.ShapeDtypeStruct(dh.shape, dh.dtype), jax.ShapeDtypeStruct((nb, 2, S5_ROWS, LANES), F32)],
        scratch_shapes=[pltpu.VMEM((2, S5_ROWS, LANES), F32)],
        compiler_params=_params(("parallel", "arbitrary")),
    )(dh, h, abar)


_GELU_C = math.sqrt(2.0 / math.pi)


def _gelu(x):
    return 0.5 * x * (1.0 + jnp.tanh(_GELU_C * (x + 0.044715 * x * x * x)))


def _gelu_grad(x):
    t = jnp.tanh(_GELU_C * (x + 0.044715 * x * x * x))
    return 0.5 * (1.0 + t) + 0.5 * x * (1.0 - t * t) * _GELU_C * (1.0 + 3.0 * 0.044715 * x * x)


def _s5_post_fwd(yc, u, d, w_glu):
    t, w = yc.shape
    tt = _tile(t, _TOKEN_TILES)

    def body(yc_ref, u_ref, d_ref, w_ref, o_ref, y0_ref):
        y0 = yc_ref[...] + d_ref[...] * u_ref[...]
        y = _gelu(y0)
        z = _dot(y, w_ref[...])
        o_ref[...] = (y * _sigmoid(z)).astype(o_ref.dtype)
        y0_ref[...] = y0

    tok = pl.BlockSpec((tt, w), lambda i: (i, 0))
    return pl.pallas_call(
        body, name="s5_post_fwd", grid=(t // tt,),
        in_specs=[tok, tok, pl.BlockSpec((1, w), lambda i: (0, 0)), pl.BlockSpec((w, w), lambda i: (0, 0))],
        out_specs=[tok, tok],
        out_shape=[jax.ShapeDtypeStruct((t, w), BF16), jax.ShapeDtypeStruct((t, w), F32)],
        compiler_params=_params(("parallel",)),
    )(yc, u, d, w_glu)


def _s5_post_bwd(y0, u, d, w_glu, dout):
    t, w = y0.shape
    tt = _tile(t, _TOKEN_TILES)

    def body(y0_ref, u_ref, d_ref, w_ref, do_ref, dy0_ref, y_ref, dz_ref, du_ref, dd_ref):
        @pl.when(pl.program_id(0) == 0)
        def _():
            dd_ref[...] = jnp.zeros_like(dd_ref)

        y0 = y0_ref[...]
        y = _gelu(y0)
        s = _sigmoid(_dot(y, w_ref[...]))
        do = do_ref[...]
        dz = do * y * s * (1.0 - s)
        dy = do * s + _dot_nt(dz, w_ref[...])
        dy0 = dy * _gelu_grad(y0)
        dy0_ref[...] = dy0.astype(dy0_ref.dtype)
        y_ref[...] = y.astype(y_ref.dtype)
        dz_ref[...] = dz.astype(dz_ref.dtype)
        du_ref[...] = dy0 * d_ref[...]
        dd_ref[...] += jnp.sum(dy0 * u_ref[...], axis=0, keepdims=True)

    tok = pl.BlockSpec((tt, w), lambda i: (i, 0))
    vec = pl.BlockSpec((1, w), lambda i: (0, 0))
    return pl.pallas_call(
        body, name="s5_post_bwd", grid=(t // tt,),
        in_specs=[tok, tok, vec, pl.BlockSpec((w, w), lambda i: (0, 0)), tok],
        out_specs=[tok, tok, tok, tok, vec],
        out_shape=[jax.ShapeDtypeStruct((t, w), BF16), jax.ShapeDtypeStruct((t, w), BF16),
                   jax.ShapeDtypeStruct((t, w), BF16), jax.ShapeDtypeStruct((t, w), F32),
                   jax.ShapeDtypeStruct((1, w), F32)],
        compiler_params=_params(("arbitrary",)),
    )(y0, u, d, w_glu, dout)


def _s5_params(a_re, a_im, b_re, b_im, c_re, c_im, log_dt):
    dt = jnp.exp(log_dt)[:, None]
    mag = jnp.exp(dt * a_re)
    ang = dt * a_im
    abar_re = mag * jnp.cos(ang)
    abar_im = mag * jnp.sin(ang)
    den = a_re * a_re + a_im * a_im
    zr = abar_re - 1.0
    zi = abar_im
    coef_re = ((zr * a_re + zi * a_im) / den)[..., None]
    coef_im = ((zi * a_re - zr * a_im) / den)[..., None]
    bb_re = coef_re * b_re - coef_im * b_im
    bb_im = coef_re * b_im + coef_im * b_re
    eye = jnp.eye(S5_GROUPS, dtype=F32)

    def dense_in(bb):
        return jnp.einsum('gnp,gh->gphn', bb, eye).reshape(S5_WIDTH, S5_NSTATE)

    def dense_out(cc):
        return jnp.einsum('gpn,gh->gnhp', cc, eye).reshape(S5_NSTATE, S5_WIDTH)

    abar = jnp.stack([abar_re.reshape(S5_ROWS, LANES), abar_im.reshape(S5_ROWS, LANES)])
    bbd = jnp.concatenate([dense_in(bb_re), dense_in(bb_im)], axis=1)
    ccd = jnp.concatenate([dense_out(c_re), dense_out(-c_im)], axis=0)
    return abar, bbd, ccd


def _tok_tile(t):
    return _tile(t, _TOKEN_TILES[1:])


def _rms_fwd(x, w):
    t, d = x.shape
    tt = _tok_tile(t)

    def body(x_ref, w_ref, o_ref):
        xv = x_ref[...]
        r = lax.rsqrt(jnp.mean(xv * xv, axis=-1, keepdims=True) + NORM_EPS)
        o_ref[...] = (xv * r * w_ref[...]).astype(o_ref.dtype)

    tok = pl.BlockSpec((tt, d), lambda i: (i, 0))
    return pl.pallas_call(
        body, name="rms_fwd", grid=(t // tt,), in_specs=[tok, pl.BlockSpec((1, d), lambda i: (0, 0))],
        out_specs=tok, out_shape=jax.ShapeDtypeStruct((t, d), BF16), compiler_params=_params(("parallel",)),
    )(x, w)


def _rms_bwd_math(xv, wv, dy):
    r = lax.rsqrt(jnp.mean(xv * xv, axis=-1, keepdims=True) + NORM_EPS)
    xh = xv * r
    dxh = dy * wv
    dx = r * (dxh - xh * jnp.mean(dxh * xh, axis=-1, keepdims=True))
    return dx, jnp.sum(dy * xh, axis=0, keepdims=True)


def _rms_bwd(x, w, dy, dres):
    t, d = x.shape
    tt = _tok_tile(t)

    def body(x_ref, w_ref, dy_ref, dr_ref, dx_ref, dw_ref):
        @pl.when(pl.program_id(0) == 0)
        def _():
            dw_ref[...] = jnp.zeros_like(dw_ref)

        dx, dw = _rms_bwd_math(x_ref[...], w_ref[...], dy_ref[...])
        dx_ref[...] = dr_ref[...] + dx
        dw_ref[...] += dw

    tok = pl.BlockSpec((tt, d), lambda i: (i, 0))
    vec = pl.BlockSpec((1, d), lambda i: (0, 0))
    return pl.pallas_call(
        body, name="rms_bwd", grid=(t // tt,), in_specs=[tok, vec, tok, tok], out_specs=[tok, vec],
        out_shape=[jax.ShapeDtypeStruct((t, d), F32), jax.ShapeDtypeStruct((1, d), F32)],
        compiler_params=_params(("arbitrary",)),
    )(x, w, dy, dres)


def _loss_head(x, w, target):
    t, d = x.shape
    tt = _tok_tile(t)

    def body(x_ref, w_ref, t_ref, l_ref, dx_ref, dw_ref):
        @pl.when(pl.program_id(0) == 0)
        def _():
            l_ref[...] = jnp.zeros_like(l_ref)
            dw_ref[...] = jnp.zeros_like(dw_ref)

        xv = x_ref[...]
        wv = w_ref[...]
        r = lax.rsqrt(jnp.mean(xv * xv, axis=-1, keepdims=True) + NORM_EPS)
        err = xv * r * wv - t_ref[...]
        row = jnp.sum(err * err, axis=-1, keepdims=True)
        l_ref[...] += (0.5 / d) * jnp.sum(row, axis=0, keepdims=True)
        dx, dw = _rms_bwd_math(xv, wv, err * (1.0 / d))
        dx_ref[...] = dx
        dw_ref[...] += dw

    tok = pl.BlockSpec((tt, d), lambda i: (i, 0))
    vec = pl.BlockSpec((1, d), lambda i: (0, 0))
    return pl.pallas_call(
        body, name="loss_head", grid=(t // tt,), in_specs=[tok, vec, tok],
        out_specs=[pl.BlockSpec((1, 1), lambda i: (0, 0)), tok, vec],
        out_shape=[jax.ShapeDtypeStruct((1, 1), F32), jax.ShapeDtypeStruct((t, d), F32), jax.ShapeDtypeStruct((1, d), F32)],
        compiler_params=_params(("arbitrary",)),
    )(x, w, target)


FF_PAD = 3072
FF_SHARD = D_FF // N_DEV
FF_SHARD_PAD = FF_PAD // N_DEV


def _ffn_in(x, nw, wg_t, wu_t):
    t, d = x.shape
    tm = _tile(t, _TOKEN_TILES)
    tn = 1024

    def body(x_ref, nw_ref, wg_ref, wu_ref, h_ref, dadg_ref, dadu_ref, a_ref):
        @pl.when(pl.program_id(1) == 0)
        def _():
            xv = x_ref[...]
            r = lax.rsqrt(jnp.mean(xv * xv, axis=-1, keepdims=True) + NORM_EPS)
            h_ref[...] = (xv * r * nw_ref[...]).astype(h_ref.dtype)

        h = h_ref[...]
        g = _dot_nt(h, wg_ref[...])
        u = _dot_nt(h, wu_ref[...])
        s = _sigmoid(g)
        silu = g * s
        dadg_ref[...] = (u * (s + silu * (1.0 - s))).astype(dadg_ref.dtype)
        dadu_ref[...] = silu.astype(dadu_ref.dtype)
        a_ref[...] = (silu * u).astype(a_ref.dtype)

    tok = pl.BlockSpec((tm, d), lambda i, j: (i, 0))
    wsp = pl.BlockSpec((tn, d), lambda i, j: (j, 0))
    wide = pl.BlockSpec((tm, tn), lambda i, j: (i, j))
    return pl.pallas_call(
        body, name="ffn_in", grid=(t // tm, FF_PAD // tn),
        in_specs=[tok, pl.BlockSpec((1, d), lambda i, j: (0, 0)), wsp, wsp], out_specs=[tok, wide, wide, wide],
        out_shape=[jax.ShapeDtypeStruct((t, d), BF16)] + [jax.ShapeDtypeStruct((t, FF_PAD), BF16)] * 3,
        compiler_params=_params(("parallel", "arbitrary")),
    )(x, nw, wg_t, wu_t)


def _ffn_out_dx(dxo, wd, dadg, dadu, after=None):
    t, d = dxo.shape
    tm = _tile(t, _TOKEN_TILES)
    tn = 1024

    def body(dx_ref, wd_ref, dadg_ref, dadu_ref, *rest):
        dg_ref, du_ref = rest[-2:]
        da = 0.5 * _dot_nt(dx_ref[...], wd_ref[...])
        dg_ref[...] = (da * dadg_ref[...].astype(F32)).astype(dg_ref.dtype)
        du_ref[...] = (da * dadu_ref[...].astype(F32)).astype(du_ref.dtype)

    wide = pl.BlockSpec((tm, tn), lambda j, i: (i, j))
    return pl.pallas_call(
        body, name="ffn_out_dx", grid=(FF_PAD // tn, t // tm),
        in_specs=[pl.BlockSpec((tm, d), lambda j, i: (i, 0)), pl.BlockSpec((tn, d), lambda j, i: (j, 0)), wide, wide]
        + [pl.BlockSpec(memory_space=pl.ANY)] * (after is not None),
        out_specs=[wide, wide], out_shape=[jax.ShapeDtypeStruct((t, FF_PAD), BF16)] * 2,
        compiler_params=_params(("parallel", "parallel")),
    )(dxo, wd, dadg, dadu, *([after] if after is not None else []))


def _ffn_in_dx(dg, du, wg_t, wu_t, x, nw, dxo):
    t, d = x.shape
    tm = _tile(t, (256, 128))

    def body(dg_ref, du_ref, wg_ref, wu_ref, x_ref, nw_ref, dr_ref, dx_ref, dw_ref):
        @pl.when(pl.program_id(0) == 0)
        def _():
            dw_ref[...] = jnp.zeros_like(dw_ref)

        dh = _dot(dg_ref[...], wg_ref[...]) + _dot(du_ref[...], wu_ref[...])
        dx, dw = _rms_bwd_math(x_ref[...], nw_ref[...], dh)
        dx_ref[...] = dr_ref[...] + dx
        dw_ref[...] += dw

    wide = pl.BlockSpec((tm, FF_PAD), lambda i: (i, 0))
    wsp = pl.BlockSpec((FF_PAD, d), lambda i: (0, 0))
    tok = pl.BlockSpec((tm, d), lambda i: (i, 0))
    vec = pl.BlockSpec((1, d), lambda i: (0, 0))
    return pl.pallas_call(
        body, name="ffn_in_dx", grid=(t // tm,), in_specs=[wide, wide, wsp, wsp, tok, vec, tok], out_specs=[tok, vec],
        out_shape=[jax.ShapeDtypeStruct((t, d), F32), jax.ShapeDtypeStruct((1, d), F32)],
        compiler_params=_params(("arbitrary",)),
    )(dg, du, wg_t, wu_t, x, nw, dxo)


def _adamw(w, g, m, v):
    r, c = w.shape
    tr = _tile(r, (512, 256, 128, 64, 32, 16, 8))
    c1 = 1.0 - ADAM_B1 ** ADAM_STEP
    c2 = 1.0 - ADAM_B2 ** ADAM_STEP

    def body(w_ref, g_ref, m_ref, v_ref, d_ref, nm_ref, nv_ref):
        gv = g_ref[...]
        nm = ADAM_B1 * m_ref[...] + (1.0 - ADAM_B1) * gv
        nv = ADAM_B2 * v_ref[...] + (1.0 - ADAM_B2) * (gv * gv)
        d_ref[...] = -ADAM_LR * ((nm / c1) / (jnp.sqrt(nv / c2) + ADAM_EPS) + ADAM_WD * w_ref[...])
        nm_ref[...] = nm
        nv_ref[...] = nv

    blk = pl.BlockSpec((tr, c), lambda i: (i, 0))
    return pl.pallas_call(
        body, name="adamw", grid=(r // tr,), in_specs=[blk] * 4, out_specs=[blk] * 3,
        out_shape=[jax.ShapeDtypeStruct((r, c), F32)] * 3, compiler_params=_params(("parallel",)),
    )(w, g, m, v)


N_KINDS = 3


def _ffn_slot(which, kind):
    return which * N_KINDS + kind


def _ffn_fwd(x, nw, w_ffn, which):
    wg, wu, wd = (w_ffn[_ffn_slot(which, kind)] for kind in range(3))
    h, dadg, dadu, a = _ffn_in(x, nw, wg, wu)
    return _mm(a, wd, name="ffn_out", res=x, scale=0.5, token_tiles=_TOKEN_TILES[1:]), (x, h, dadg, dadu, a)


def _ffn_bwd(dxo, res, nw, w_ffn, which, after=None):
    x, h, dadg, dadu, a = res
    wg, wu, wd = (w_ffn[_ffn_slot(which, kind)] for kind in range(3))
    dg, du = _ffn_out_dx(dxo, wd, dadg, dadu, after)
    g_ffn = lax.empty((N_KINDS,) + wd.shape, BF16)
    g_ffn = _mm_tn(a, dxo, name="ffn_out_dw", scale=0.5, into=g_ffn, slot=2)
    g_ffn = _mm_tn(dg, h, name="ffn_in_dw", into=g_ffn, slot=0)
    g_ffn = _mm_tn(du, h, name="ffn_in_dw", into=g_ffn, slot=1)
    dx, dnw = _ffn_in_dx(dg, du, wg, wu, x, nw, dxo)
    return dx, dnw, g_ffn


def _hg_lower_bounds(logits):
    p = jax.nn.softmax(logits, axis=0)
    return jnp.cumsum(p, axis=0) - p[0]


def _even_fwd(x1, p, nb, l):
    t = nb * l
    h = _rms_fwd(x1, p["mix_norm"])
    proj = _mm(h, p["w_in_t"], name="ev_in", nt=True)
    proj3 = proj.reshape(nb, l, EV_IN)
    ya, hst = _hgrn2_fwd(proj3, p["lb"], p["hg_nw"])
    u = proj[:, 4 * HG_QK:]
    bu = _mm(u, p["bbd"], name="s5_in", out_dtype=BF16)
    hs = _s5_scan_fwd(bu.reshape(nb, l, 2, S5_ROWS, LANES), p["abar"])
    yc = _mm(hs.reshape(t, 2 * S5_NSTATE), p["ccd"], name="s5_out")
    yb, y0 = _s5_post_fwd(yc, u, p["s5_d"], p["w_glu"])
    ycat = jnp.concatenate([ya.reshape(t, HG_QK), yb], axis=1)
    x2 = _mm(ycat, p["w_out"], name="mix_out", res=x1)
    return x2, (x1, h, proj3, hst, u, hs, y0, ycat)


def _even_bwd(dx2, res, p, nb, l):
    x1, h, proj3, hst, u, hs, y0, ycat = res
    t = nb * l
    dycat = _mm(dx2, p["w_out"], name="mix_out_dx", nt=True)
    dw_out = _mm_tn(ycat, dx2, name="mix_out_dw", out_dtype=BF16)
    dp_hg, dlb, dhg_nw = _hgrn2_bwd(proj3, p["lb"], p["hg_nw"], hst, dycat.reshape(nb, l, D_MODEL))
    dy0, y, dz, du_d, dd = _s5_post_bwd(y0, u, p["s5_d"], p["w_glu"], dycat[:, HG_QK:])
    dw_glu = _mm_tn(y, dz, name="s5_glu_dw", out_dtype=BF16)
    dhs = _mm(dy0, p["ccd"], name="s5_out_dx", nt=True, out_dtype=BF16)
    dccd = _mm_tn(hs.reshape(t, 2 * S5_NSTATE), dy0, name="s5_out_dw")
    g5, dabar = _s5_scan_bwd(dhs.reshape(nb, l, 2, S5_ROWS, LANES), hs, p["abar"])
    g2 = g5.reshape(t, 2 * S5_NSTATE)
    du = _mm(g2, p["bbd"], name="s5_in_dx", nt=True, res=du_d, out_dtype=BF16)
    dbbd = _mm_tn(u, g2, name="s5_in_dw")
    dproj = jnp.concatenate([dp_hg.reshape(t, 4 * HG_QK), du], axis=1)
    dw_in_t = _mm_tn(dproj, h, name="ev_in_dw", out_dtype=BF16)
    dh = _mm(dproj, p["w_in_t"], name="ev_in_dx")
    dx1, dmix = _rms_bwd(x1, p["mix_norm"], dh, dx2)
    small = dict(mix_norm=dmix, lb=dlb.sum(0).reshape(HG_QK), hg_nw=dhg_nw.reshape(HEAD_DIM),
                 abar=dabar.sum(0), bbd=dbbd, ccd=dccd, s5_d=dd.reshape(S5_WIDTH))
    return dx1, dict(w_in_t=dw_in_t, w_glu=dw_glu, w_out=dw_out), small


def _odd_fwd(x1, p, nb, l):
    t = nb * l
    h = _rms_fwd(x1, p["mix_norm"])
    proj3 = _mm(h, p["w_in"], name="od_in").reshape(nb, l, OD_IN_PAD)
    qkv = _gdn_pre_fwd(proj3, p["conv_w"])
    y, st = _gdn_fwd(qkv, proj3, p["a_log"], p["dt_bias"], p["gdn_nw"])
    y2 = y.reshape(t, D_MODEL)
    x2 = _mm(y2, p["w_out"], name="mix_out", res=x1)
    return x2, (x1, h, proj3, qkv, st, y2)


def _odd_bwd(dx2, res, p, nb, l):
    x1, h, proj3, qkv, st, y2 = res
    t = nb * l
    dy = _mm(dx2, p["w_out"], name="mix_out_dx", nt=True)
    dw_out = _mm_tn(y2, dx2, name="mix_out_dw", out_dtype=BF16)
    dqkv, dgate, dsmall, dalog, ddtb, dnw = _gdn_bwd(
        qkv, proj3, p["a_log"], p["dt_bias"], p["gdn_nw"], st, dy.reshape(nb, l, D_MODEL))
    dx_qkv, dcw = _gdn_pre_bwd(proj3, p["conv_w"], dqkv)
    dproj = jnp.concatenate([dx_qkv, dgate, dsmall], axis=-1).reshape(t, OD_IN_PAD)
    dw_in = _mm_tn(h, dproj, name="od_in_dw", out_dtype=BF16)[:, :OD_IN]
    dh = _mm(dproj, p["w_in"], name="od_in_dx", nt=True)
    dx1, dmix = _rms_bwd(x1, p["mix_norm"], dh, dx2)
    small = dict(mix_norm=dmix, a_log=dalog.sum(0).reshape(GDN_HEADS), dt_bias=ddtb.sum(0).reshape(GDN_HEADS),
                 gdn_nw=dnw.reshape(HEAD_DIM))
    return dx1, dict(w_in=dw_in, conv_w=dcw.sum(0), w_out=dw_out), small


def _local_step(x, target, layer_weights, small, layer_done):
    nb, l, _ = x.shape
    t = nb * l
    lbs, lbs_vjp = jax.vjp(_hg_lower_bounds, small["hg_lb_logits"])
    row = lambda v: v.reshape(1, -1)
    s5_vjps = {}

    def layer_params(layer, big):
        j = layer // 2
        p = dict(mix_norm=row(small["mix_norm"][layer]), w_out=big["w_out"])
        if layer % 2 == 0:
            (abar, bbd, ccd), s5_vjps[j] = jax.vjp(
                _s5_params, small["s5_a_re"][j], small["s5_a_im"][j], small["s5_b_re"][j], small["s5_b_im"][j],
                small["s5_c_re"][j], small["s5_c_im"][j], small["s5_log_dt"][j])
            p.update(w_in_t=big["w_in_t"], w_glu=big["w_glu"],
                     lb=lbs[j].reshape(HG_HEADS, 1, HEAD_DIM), hg_nw=row(small["hg_norm_w"][j]),
                     abar=abar, bbd=bbd.astype(BF16), ccd=ccd.astype(BF16), s5_d=row(small["s5_d"][j]))
        else:
            p.update(w_in=big["w_in"], conv_w=big["conv_w"],
                     a_log=small["gdn_a_log"][j].reshape(GDN_HEADS, 1, 1),
                     dt_bias=small["gdn_dt_bias"][j].reshape(GDN_HEADS, 1, 1), gdn_nw=row(small["gdn_norm_w"][j]))
        return p

    xs = x.reshape(t, D_MODEL)
    saved, layers, ffn_w = [], [], []
    for layer in range(DEPTH):
        big = layer_weights(layer, xs)
        p = layer_params(layer, big)
        xs, r1 = _ffn_fwd(xs, row(small["ffn1_norm"][layer]), big["ffn"], 0)
        xs, r2 = (_even_fwd if layer % 2 == 0 else _odd_fwd)(xs, p, nb, l)
        xs, r3 = _ffn_fwd(xs, row(small["ffn2_norm"][layer]), big["ffn"], 1)
        saved.append((r1, r2, r3))
        layers.append(p)
        ffn_w.append(big["ffn"])
    loss, dx, dfinal = _loss_head(xs, row(small["final_norm"]), target.reshape(t, D_MODEL))

    gs = {k: [None] * DEPTH for k in ("ffn1_norm", "mix_norm", "ffn2_norm")}
    gs.update({k: [None] * 2 for k in ("hg_norm_w", "s5_a_re", "s5_a_im", "s5_b_re", "s5_b_im", "s5_c_re", "s5_c_im",
                                       "s5_d", "s5_log_dt", "gdn_a_log", "gdn_dt_bias", "gdn_norm_w")})
    dlbs = [None] * 2
    token = None
    for layer in reversed(range(DEPTH)):
        j = layer // 2
        p = layers[layer]
        r1, r2, r3 = saved[layer]
        dx, dnw, g_ffn = _ffn_bwd(dx, r3, row(small["ffn2_norm"][layer]), ffn_w[layer], 1, token)
        gs["ffn2_norm"][layer] = dnw[0]
        if layer % 2 == 0:
            dx, gw, sm = _even_bwd(dx, r2, p, nb, l)
            dlbs[j] = sm["lb"]
            gs["hg_norm_w"][j] = sm["hg_nw"]
            gs["s5_d"][j] = sm["s5_d"]
            (gs["s5_a_re"][j], gs["s5_a_im"][j], gs["s5_b_re"][j], gs["s5_b_im"][j], gs["s5_c_re"][j],
             gs["s5_c_im"][j], gs["s5_log_dt"][j]) = s5_vjps[j]((sm["abar"], sm["bbd"], sm["ccd"]))
        else:
            dx, gw, sm = _odd_bwd(dx, r2, p, nb, l)
            gs["gdn_a_log"][j], gs["gdn_dt_bias"][j], gs["gdn_norm_w"][j] = sm["a_log"], sm["dt_bias"], sm["gdn_nw"]
        gs["mix_norm"][layer] = sm["mix_norm"][0]
        token = layer_done(layer, 0, dict(gw, ffn=g_ffn))
        dx, dnw, g_ffn = _ffn_bwd(dx, r1, row(small["ffn1_norm"][layer]), ffn_w[layer], 0, token)
        gs["ffn1_norm"][layer] = dnw[0]
        token = layer_done(layer, 1, dict(ffn=g_ffn))
    gsmall = {k: jnp.stack(v) for k, v in gs.items()}
    gsmall["hg_lb_logits"] = lbs_vjp(jnp.stack(dlbs))[0]
    gsmall["final_norm"] = dfinal[0]
    return loss[0, 0], dx.reshape(nb, l, D_MODEL), gsmall


def _here():
    return lax.axis_index("x"), lax.axis_index("y"), lax.axis_index("c")


def _other_chips(x, y):
    return [(1 - x, y), (x, 1 - y), (1 - x, 1 - y)]


def _all_gather(blocks, name):
    n = len(blocks)

    def body(*refs):
        x_refs, o_refs = refs[:n], refs[n:2 * n]
        send_sems, recv_sems, local_sems = refs[2 * n:]
        x, y, cc = _here()
        me, sibling = (x, y, cc), (x, y, 1 - cc)
        chips = _other_chips(x, y)

        def win(i, p):
            return o_refs[i].at[:, 4 * p[0] + 2 * p[1] + p[2]]

        def copy(i, k, blk, to, src=None):
            return pltpu.make_async_remote_copy(
                src_ref=win(i, blk) if src is None else src, dst_ref=win(i, blk),
                send_sem=send_sems.at[i, k], recv_sem=recv_sems.at[i, k], device_id=to, device_id_type=MESH)

        mine = [pltpu.make_async_copy(x_refs[i], win(i, me), local_sems.at[i]) for i in range(n)]
        for cp in mine:
            cp.start()
        first = []
        for i in range(n):
            first.append(copy(i, 0, me, sibling, src=x_refs[i]))
            first += [copy(i, 1 + j, me, (*chip, cc), src=x_refs[i]) for j, chip in enumerate(chips)]
        for cp in first:
            cp.start()
        passed = []
        for j, chip in enumerate(chips):
            for i in range(n):
                copy(i, 1 + j, (*chip, cc), me).wait_recv()
                fwd = copy(i, 4 + j, (*chip, cc), sibling)
                fwd.start()
                passed.append(fwd)
        for i in range(n):
            copy(i, 0, sibling, me).wait_recv()
            for j, chip in enumerate(chips):
                copy(i, 4 + j, (*chip, 1 - cc), me).wait_recv()
        for cp in first + passed:
            cp.wait_send()
        for cp in mine:
            cp.wait()

    hbm = pl.BlockSpec(memory_space=pl.ANY)
    return pl.pallas_call(
        body, name=name, in_specs=[hbm] * n, out_specs=[hbm] * n,
        out_shape=[jax.ShapeDtypeStruct((b.shape[0], N_DEV) + b.shape[1:], b.dtype) for b in blocks],
        scratch_shapes=[pltpu.SemaphoreType.DMA((n, 7)), pltpu.SemaphoreType.DMA((n, 7)), pltpu.SemaphoreType.DMA((n,))],
    )(*blocks)


_HBM = pl.BlockSpec(memory_space=pltpu.HBM)
_SEM = pl.BlockSpec(memory_space=pltpu.SEMAPHORE)
_DATAFLOW = pltpu.SideEffectType.DATAFLOW_SIDE_EFFECTING


def _peer(k):
    x, y, c = _here()
    return (x ^ ((k >> 2) & 1), y ^ ((k >> 1) & 1), c ^ (k & 1))


def _device_of(p):
    return 4 * p[0] + 2 * p[1] + p[2]


_OTHER_CHIPS = (4, 2, 6)


def _exchange_start(srcs, lands, copies, name, after=None):
    n = len(srcs)
    n_in = 2 * n + (after is not None)

    def body(*refs):
        s_refs, l_refs, sems = refs[:n], refs[n:2 * n], refs[n_in:n_in + 2 * n]
        for i in range(n):
            for slot, k in enumerate(_OTHER_CHIPS):
                src, dst = copies(s_refs[i], l_refs[i], k, slot)
                pltpu.make_async_remote_copy(
                    src_ref=src, dst_ref=dst, send_sem=sems[2 * i], recv_sem=sems[2 * i + 1],
                    device_id=_peer(k), device_id_type=MESH).start()
        refs[-1][...] = jnp.zeros_like(refs[-1])

    both = list(srcs) + list(lands)
    outs = pl.pallas_call(
        body, name=name,
        out_shape=tuple([pltpu.SemaphoreType.DMA(())] * (2 * n)) + tuple(pltpu.HBM(a.shape, a.dtype) for a in both)
        + (jax.ShapeDtypeStruct((8, LANES), F32),),
        in_specs=[_HBM] * (2 * n) + [pl.BlockSpec(memory_space=pl.ANY)] * (after is not None),
        out_specs=tuple([_SEM] * (2 * n)) + tuple([_HBM] * (2 * n)) + (pl.BlockSpec(memory_space=pltpu.VMEM),),
        input_output_aliases={i: 2 * n + i for i in range(2 * n)},
        compiler_params=pltpu.CompilerParams(has_side_effects=_DATAFLOW),
    )(*[pltpu.with_memory_space_constraint(a, pltpu.HBM) for a in both], *([after] if after is not None else []))
    return list(outs[:2 * n]), list(outs[2 * n:3 * n]), list(outs[3 * n:4 * n]), outs[4 * n]


def _exchange_wait(handle, three, after, name):
    sems, srcs, lands, _ = handle
    n = len(srcs)

    def body(*refs):
        l_refs, sem_refs = refs[n:2 * n], refs[2 * n:4 * n]
        for i in range(n):
            all_three = three(l_refs[i])
            cp = pltpu.make_async_remote_copy(
                src_ref=all_three, dst_ref=all_three, send_sem=sem_refs[2 * i], recv_sem=sem_refs[2 * i + 1],
                device_id=_peer(_OTHER_CHIPS[0]), device_id_type=MESH)
            cp.wait_send()
            cp.wait_recv()

    both = list(srcs) + list(lands)
    outs = pl.pallas_call(
        body, name=name, out_shape=tuple(pltpu.HBM(a.shape, a.dtype) for a in both),
        in_specs=[_HBM] * (2 * n) + [_SEM] * (2 * n) + [pl.BlockSpec(memory_space=pl.ANY)],
        out_specs=tuple([_HBM] * (2 * n)), input_output_aliases={i: i for i in range(2 * n)},
        compiler_params=pltpu.CompilerParams(has_side_effects=_DATAFLOW),
    )(*both, *sems, after)
    return list(outs[:n]), list(outs[n:])


def _gather_copies(src, land, k, slot):
    del k, slot
    return src, land.at[:, _device_of(_here())]


def _gather_three(land):
    return land.at[:, pl.ds(0, len(_OTHER_CHIPS))]


def _scatter_copies(src, land, k, slot):
    p = _peer(k)
    return src.at[2 * p[0] + p[1]], land.at[slot]


def _scatter_three(land):
    return land


def _core_forward(lands):
    n = len(lands)

    def body(*refs):
        l_refs = refs[:n]
        send_sems, recv_sems = refs[2 * n:]
        x, y, cc = _here()
        mine = [(x, y)] + [_peer(k)[:2] for k in _OTHER_CHIPS]
        cps = []
        for i in range(n):
            for j, (px, py) in enumerate(mine):
                give = l_refs[i].at[:, _device_of((px, py, cc))]
                take = l_refs[i].at[:, _device_of((px, py, 1 - cc))]
                cps.append((pltpu.make_async_remote_copy(
                    src_ref=give, dst_ref=give, send_sem=send_sems.at[i, j], recv_sem=recv_sems.at[i, j],
                    device_id=(x, y, 1 - cc), device_id_type=MESH), take))
        for cp, _ in cps:
            cp.start()
        for i in range(n):
            for j, (cp, take) in enumerate(cps[4 * i:4 * i + 4]):
                cp.wait_send()
                pltpu.make_async_remote_copy(
                    src_ref=take, dst_ref=take, send_sem=send_sems.at[i, j], recv_sem=recv_sems.at[i, j],
                    device_id=(x, y, 1 - cc), device_id_type=MESH).wait_recv()

    hbm = pl.BlockSpec(memory_space=pl.ANY)
    return pl.pallas_call(
        body, name="gather_core_forward", in_specs=[hbm] * n, out_specs=[hbm] * n,
        out_shape=[jax.ShapeDtypeStruct(a.shape, a.dtype) for a in lands],
        input_output_aliases={i: i for i in range(n)},
        scratch_shapes=[pltpu.SemaphoreType.DMA((n, 4)), pltpu.SemaphoreType.DMA((n, 4))],
    )(*lands)


def _rs_core_exchange(gs):
    n = len(gs)

    def body(*refs):
        g_refs, o_refs = refs[:n], refs[n:2 * n]
        send_sems, recv_sems = refs[2 * n:]
        x, y, cc = _here()
        cps = [pltpu.make_async_remote_copy(
            src_ref=g_refs[i].at[:, 2 * j + (1 - cc)], dst_ref=o_refs[i].at[j],
            send_sem=send_sems.at[i, j], recv_sem=recv_sems.at[i, j], device_id=(x, y, 1 - cc), device_id_type=MESH)
            for i in range(n) for j in range(4)]
        for cp in cps:
            cp.start()
        for cp in cps:
            cp.wait()

    hbm = pl.BlockSpec(memory_space=pl.ANY)
    return pl.pallas_call(
        body, name="rs_core_exchange", in_specs=[hbm] * n, out_specs=[hbm] * n,
        out_shape=[jax.ShapeDtypeStruct((4, g.shape[0]) + g.shape[2:], g.dtype) for g in gs],
        scratch_shapes=[pltpu.SemaphoreType.DMA((n, 4)), pltpu.SemaphoreType.DMA((n, 4))],
    )(*gs)


def _row_tile(w):
    return w if w <= 512 else _tile(w, (512, 256, 128))


def _rs_chip_sum(g, r1):
    a, _, w, c = g.shape
    tr = _row_tile(w)

    def body(g_ref, r_ref, o_ref):
        o_ref[...] = (g_ref[...].astype(F32) + r_ref[...].astype(F32)).astype(o_ref.dtype)

    blk = pl.BlockSpec((None, None, tr, c), lambda j, aa, i: (j, aa, i, 0))
    return pl.pallas_call(
        body, name="rs_chip_sum", grid=(4, a, w // tr),
        in_specs=[pl.BlockSpec((None, None, tr, c), lambda j, aa, i: (aa, 2 * j + lax.axis_index("c"), i, 0)), blk],
        out_specs=blk, out_shape=jax.ShapeDtypeStruct((4, a, w, c), g.dtype),
        compiler_params=_params(("parallel", "parallel", "parallel")),
    )(g, r1)


def _rs_final_sum(g, r1, r2):
    a, _, w, c = g.shape
    tr = _row_tile(w)

    def body(g_ref, r1_ref, r2_ref, o_ref):
        s = g_ref[...].astype(F32) + r1_ref[...].astype(F32)
        for k in range(3):
            s = s + r2_ref[k].astype(F32)
        o_ref[...] = s

    def device():
        return 4 * lax.axis_index("x") + 2 * lax.axis_index("y") + lax.axis_index("c")

    def chip():
        return 2 * lax.axis_index("x") + lax.axis_index("y")

    return pl.pallas_call(
        body, name="rs_final_sum", grid=(a, w // tr),
        in_specs=[pl.BlockSpec((None, None, tr, c), lambda aa, i: (aa, device(), i, 0)),
                  pl.BlockSpec((None, None, tr, c), lambda aa, i: (chip(), aa, i, 0)),
                  pl.BlockSpec((3, None, tr, c), lambda aa, i: (0, aa, i, 0))],
        out_specs=pl.BlockSpec((None, tr, c), lambda aa, i: (aa, i, 0)),
        out_shape=jax.ShapeDtypeStruct((a, w, c), F32),
        compiler_params=_params(("parallel", "parallel")),
    )(g, r1, r2)


def _sum_slots(g):
    _, n, r, c = g.shape

    def body(g_ref, o_ref):
        s = g_ref[0, 0]
        for k in range(1, n):
            s = s + g_ref[0, k]
        o_ref[...] = s

    return pl.pallas_call(
        body, name="sum_slots", grid=(1,),
        in_specs=[pl.BlockSpec((1, n, r, c), lambda i: (0, 0, 0, 0))], out_specs=pl.BlockSpec((r, c), lambda i: (0, 0)),
        out_shape=jax.ShapeDtypeStruct((r, c), F32), compiler_params=_params(("arbitrary",)),
    )(g)


_WEIGHTS = ['ffn1_norm', 'ffn1_w_gate', 'ffn1_w_up', 'ffn1_w_down', 'mix_norm', 'ffn2_norm', 'ffn2_w_gate', 'ffn2_w_up',
            'ffn2_w_down', 'ev_w_in', 'hg_lb_logits', 'hg_norm_w', 's5_a_re', 's5_a_im', 's5_b_re', 's5_b_im', 's5_c_re',
            's5_c_im', 's5_d', 's5_log_dt', 's5_w_glu', 'ev_w_out', 'od_w_in', 'gdn_conv_w', 'gdn_a_log', 'gdn_dt_bias',
            'gdn_norm_w', 'od_w_out', 'final_norm']
_SHARDED = ['ffn1_w_gate', 'ffn1_w_up', 'ffn1_w_down', 'ffn2_w_gate', 'ffn2_w_up', 'ffn2_w_down', 'ev_w_in', 's5_w_glu',
            'ev_w_out', 'od_w_in', 'gdn_conv_w', 'od_w_out']
_REPLICATED = [n for n in _WEIGHTS if n not in _SHARDED]
_SMALL_COLS = 1024


def _pad_rows(a, rows):
    return jnp.pad(a, ((0, rows - a.shape[0]), (0, 0)))


def _layer_blocks(w, layer):
    j = layer // 2
    ffn = []
    for f in ('ffn1', 'ffn2'):
        ffn += [_pad_rows(w[f + '_w_gate'][layer].T, FF_SHARD_PAD), _pad_rows(w[f + '_w_up'][layer].T, FF_SHARD_PAD),
                _pad_rows(w[f + '_w_down'][layer], FF_SHARD_PAD)]
    ffn = jnp.stack(ffn).astype(BF16)
    if layer % 2 == 0:
        return [ffn, w['ev_w_in'][j].T.astype(BF16)[None], w['s5_w_glu'][j].astype(BF16)[None],
                w['ev_w_out'][j].astype(BF16)[None]]
    return [ffn, w['od_w_in'][j].astype(BF16)[None], w['gdn_conv_w'][j][None], w['od_w_out'][j].astype(BF16)[None]]


def _layer_whole(gathered, layer):
    whole = lambda g: g.reshape(g.shape[0], N_DEV * g.shape[2], g.shape[3])
    ffn, a1, a2, w_out = gathered
    if layer % 2 == 0:
        return dict(ffn=whole(ffn), w_in_t=whole(a1)[0], w_glu=whole(a2)[0], w_out=whole(w_out)[0])
    w_in = jnp.moveaxis(a1[0], 0, 1).reshape(D_MODEL, OD_IN)
    conv = jnp.moveaxis(a2[0], 0, 1).reshape(CONV_W, GDN_QKV)
    return dict(ffn=whole(ffn), w_in=jnp.pad(w_in, ((0, 0), (0, OD_IN_PAD - OD_IN))), conv_w=conv, w_out=whole(w_out)[0])


def _layer_grad_windows(g, layer):
    windows = lambda a, w: a.reshape(-1, N_DEV, w, a.shape[-1])
    ffn = windows(g['ffn'], FF_SHARD_PAD)
    if 'w_out' not in g:
        return [ffn]
    w_out = windows(g['w_out'], D_MODEL // N_DEV)
    if layer % 2 == 0:
        return [ffn, windows(g['w_in_t'], EV_IN // N_DEV), windows(g['w_glu'], S5_WIDTH // N_DEV), w_out]
    w_in = jnp.moveaxis(g['w_in'].reshape(D_MODEL, N_DEV, OD_IN // N_DEV), 1, 0)[None]
    conv = jnp.moveaxis(g['conv_w'].reshape(CONV_W, N_DEV, GDN_QKV // N_DEV), 1, 0)[None]
    return [ffn, w_in, conv, w_out]


def _flat_small(src, prefix=''):
    flat = jnp.concatenate([src[prefix + n].reshape(-1) for n in _REPLICATED])
    rows = -(-(flat.shape[0] + 1) // (_SMALL_COLS * 8)) * 8
    return jnp.pad(flat, (0, rows * _SMALL_COLS - flat.shape[0])).reshape(rows, _SMALL_COLS)


def _split_small(flat2d, like):
    flat, out, off = flat2d.reshape(-1), {}, 0
    for n in _REPLICATED:
        size = math.prod(like[n].shape)
        out[n] = flat[off:off + size].reshape(like[n].shape)
        off += size
    return out, flat[off]


def _all_reduce_small(gsmall, loss):
    flat = _flat_small(gsmall)
    n_used = sum(math.prod(gsmall[n].shape) for n in _REPLICATED)
    flat = flat.reshape(-1).at[n_used].set(loss).reshape(flat.shape)
    (gathered,) = _all_gather([flat[None]], "gather_small")
    return _split_small(_sum_slots(gathered), gsmall)


def _as_2d(a):
    return a.reshape(-1, a.shape[-1])


def kernel(x, ffn1_norm, ffn1_w_gate, ffn1_w_up, ffn1_w_down, mix_norm, ffn2_norm, ffn2_w_gate, ffn2_w_up, ffn2_w_down, ev_w_in, hg_lb_logits, hg_norm_w, s5_a_re, s5_a_im, s5_b_re, s5_b_im, s5_c_re, s5_c_im, s5_d, s5_log_dt, s5_w_glu, ev_w_out, od_w_in, gdn_conv_w, gdn_a_log, gdn_dt_bias, gdn_norm_w, od_w_out, final_norm, loss_target, m_ffn1_norm, m_ffn1_w_gate, m_ffn1_w_up, m_ffn1_w_down, m_mix_norm, m_ffn2_norm, m_ffn2_w_gate, m_ffn2_w_up, m_ffn2_w_down, m_ev_w_in, m_hg_lb_logits, m_hg_norm_w, m_s5_a_re, m_s5_a_im, m_s5_b_re, m_s5_b_im, m_s5_c_re, m_s5_c_im, m_s5_d, m_s5_log_dt, m_s5_w_glu, m_ev_w_out, m_od_w_in, m_gdn_conv_w, m_gdn_a_log, m_gdn_dt_bias, m_gdn_norm_w, m_od_w_out, m_final_norm, v_ffn1_norm, v_ffn1_w_gate, v_ffn1_w_up, v_ffn1_w_down, v_mix_norm, v_ffn2_norm, v_ffn2_w_gate, v_ffn2_w_up, v_ffn2_w_down, v_ev_w_in, v_hg_lb_logits, v_hg_norm_w, v_s5_a_re, v_s5_a_im, v_s5_b_re, v_s5_b_im, v_s5_c_re, v_s5_c_im, v_s5_d, v_s5_log_dt, v_s5_w_glu, v_ev_w_out, v_od_w_in, v_gdn_conv_w, v_gdn_a_log, v_gdn_dt_bias, v_gdn_norm_w, v_od_w_out, v_final_norm):
    given = dict(locals())
    w = {n: given[n] for n in _WEIGHTS}
    small = {n: w[n] for n in _REPLICATED}
    me = _device_of(_here())

    blocks = [_layer_blocks(w, layer) for layer in range(DEPTH)]
    first = _all_gather(blocks[0], "gather_weights")
    pending = {}

    def start_gather(layer, after=None):
        lands = [lax.dynamic_update_slice(lax.empty((b.shape[0], N_DEV) + b.shape[1:], b.dtype), b[:, None], (0, me, 0, 0))
                 for b in blocks[layer]]
        pending[layer] = _exchange_start(blocks[layer], lands, _gather_copies, "gather_start_%d" % layer, after)

    start_gather(1)

    def layer_weights(layer, after):
        if layer == 0:
            return _layer_whole(first, 0)
        got = _exchange_wait(pending[layer], _gather_three, after, "gather_wait_%d" % layer)[1]
        if layer + 1 < DEPTH:
            start_gather(layer + 1, after)
        return _layer_whole(_core_forward(got), layer)

    sent = {}

    def layer_done(layer, part, grads):
        gs = _layer_grad_windows(grads, layer)
        from_sibling = _rs_core_exchange(gs)
        chip = [_rs_chip_sum(g, r) for g, r in zip(gs, from_sibling)]
        lands = [lax.empty((len(_OTHER_CHIPS),) + p.shape[1:], p.dtype) for p in chip]
        handle = _exchange_start(chip, lands, _scatter_copies, "scatter_start_%d_%d" % (layer, part))
        sent[layer, part] = (gs, from_sibling, handle)
        return handle[3]

    loss_part, grad_x, gsmall = _local_step(given['x'], given['loss_target'], layer_weights, small, layer_done)
    gsum, loss = _all_reduce_small(gsmall, loss_part)
    summed = {}
    for layer in reversed(range(DEPTH)):
        for part in range(2):
            gs, from_sibling, handle = sent[layer, part]
            _, lands = _exchange_wait(handle, _scatter_three, grad_x, "scatter_wait_%d_%d" % (layer, part))
            summed[layer, part] = [_rs_final_sum(g, r, land) for g, r, land in zip(gs, from_sibling, lands)]

    grads = dict(gsum)
    t_ = lambda a: jnp.swapaxes(a, 1, 2)
    for i, f in enumerate(('ffn1', 'ffn2')):
        ffn = jnp.stack([summed[layer, 1 - i][0][:, :FF_SHARD] for layer in range(DEPTH)], axis=1)
        grads[f + '_w_gate'], grads[f + '_w_up'], grads[f + '_w_down'] = t_(ffn[0]), t_(ffn[1]), ffn[2]
    even, odd = [summed[0, 0], summed[2, 0]], [summed[1, 0], summed[3, 0]]
    grads['ev_w_in'] = jnp.stack([r[1][0].T for r in even])
    grads['s5_w_glu'] = jnp.stack([r[2][0] for r in even])
    grads['ev_w_out'] = jnp.stack([r[3][0] for r in even])
    grads['od_w_in'] = jnp.stack([r[1][0] for r in odd])
    grads['gdn_conv_w'] = jnp.stack([r[2][0] for r in odd])
    grads['od_w_out'] = jnp.stack([r[3][0] for r in odd])

    delta, new_m, new_v = {}, {}, {}
    for n in _SHARDED:
        d, nm, nv = _adamw(_as_2d(w[n]), _as_2d(grads[n]), _as_2d(given['m_' + n]), _as_2d(given['v_' + n]))
        delta[n], new_m[n], new_v[n] = (a.reshape(w[n].shape) for a in (d, nm, nv))
    d, nm, nv = _adamw(_flat_small(w), _flat_small(grads), _flat_small(given, 'm_'), _flat_small(given, 'v_'))
    (delta_s, _), (new_m_s, _), (new_v_s, _) = (_split_small(a, small) for a in (d, nm, nv))
    delta.update(delta_s)
    new_m.update(new_m_s)
    new_v.update(new_v_s)
    return (loss, grad_x, *[grads[n] for n in _WEIGHTS], *[delta[n] for n in _WEIGHTS],
            *[new_m[n] for n in _WEIGHTS], *[new_v[n] for n in _WEIGHTS])
```

```python
import math

import jax
import jax.numpy as jnp
from jax import lax
from jax.experimental import pallas as pl
from jax.experimental.pallas import tpu as pltpu

F32 = jnp.float32
BF16 = jnp.bfloat16
HI = lax.Precision.HIGH

D_MODEL = 1024
DEPTH = 4
D_FF = 2816
NORM_EPS = 1e-6
F_MIN = 1e-6
CHUNK = 64
HG_HEADS = 4
HEAD_DIM = 128
HG_QK = HG_HEADS * HEAD_DIM
S5_WIDTH = 512
S5_GROUP = 16
S5_GROUPS = 32
S5_STATE = 64
S5_NSTATE = S5_GROUPS * S5_STATE
EV_IN = 2560
GDN_HEADS = 8
GDN_QKV = 3 * GDN_HEADS * HEAD_DIM
CONV_W = 4
OD_IN = 4112
OD_IN_PAD = 4224
N_DEV = 8
LANES = 128
ADAM_LR, ADAM_B1, ADAM_B2, ADAM_EPS, ADAM_WD, ADAM_STEP = 0.001, 0.9, 0.999, 1e-08, 0.01, 10
VMEM_LIMIT = 56 * 1024 * 1024

MESH = pl.DeviceIdType.MESH


def _params(sem=None, **kw):
    return pltpu.CompilerParams(dimension_semantics=sem, vmem_limit_bytes=VMEM_LIMIT, **kw)


def _tile(n, cands):
    for c in cands:
        if n % c == 0:
            return c
    return n


def _dot(a, b):
    return jnp.dot(a.astype(BF16), b.astype(BF16), preferred_element_type=F32)


def _dot_nt(a, b):
    return lax.dot_general(a.astype(BF16), b.astype(BF16), (((1,), (1,)), ((), ())), preferred_element_type=F32)


def _dot_tn(a, b):
    return lax.dot_general(a.astype(BF16), b.astype(BF16), (((0,), (0,)), ((), ())), preferred_element_type=F32)


def _dot_hi(a, b):
    return jnp.dot(a, b, precision=HI, preferred_element_type=F32)


def _bmm(a, b):
    return jnp.einsum('gmk,gkn->gmn', a.astype(BF16), b.astype(BF16), preferred_element_type=F32)


def _bmm_nt(a, b):
    return jnp.einsum('gmk,gnk->gmn', a.astype(BF16), b.astype(BF16), preferred_element_type=F32)


def _bmm_tn(a, b):
    return jnp.einsum('gkm,gkn->gmn', a.astype(BF16), b.astype(BF16), preferred_element_type=F32)


def _bmm_hi(a, b):
    return jnp.einsum('gmk,gkn->gmn', a, b, precision=HI, preferred_element_type=F32)


_TN_CANDS = (1408, 1280, 1024, 512, 384, 256, 128)
_B_TILE_BYTES = 6 * 1024 * 1024
_TOKEN_TILES = (1024, 512, 256, 128)


def _mm(a, b, *, name, nt=False, out_dtype=F32, res=None, scale=1.0, token_tiles=None):
    m, k = a.shape
    n = b.shape[0] if nt else b.shape[1]
    tm = _tile(m, token_tiles or _TOKEN_TILES)
    tn = _tile(n, [c for c in _TN_CANDS if c * k * b.dtype.itemsize <= _B_TILE_BYTES])

    def body(*refs):
        if res is None:
            a_ref, b_ref, o_ref = refs
        else:
            a_ref, b_ref, r_ref, o_ref = refs
        acc = (_dot_nt if nt else _dot)(a_ref[...], b_ref[...])
        if scale != 1.0:
            acc = scale * acc
        if res is not None:
            acc = r_ref[...] + acc
        o_ref[...] = acc.astype(out_dtype)

    b_spec = pl.BlockSpec((tn, k), lambda j, i: (j, 0)) if nt else pl.BlockSpec((k, tn), lambda j, i: (0, j))
    in_specs = [pl.BlockSpec((tm, k), lambda j, i: (i, 0)), b_spec]
    args = [a, b]
    if res is not None:
        in_specs.append(pl.BlockSpec((tm, tn), lambda j, i: (i, j)))
        args.append(res)
    return pl.pallas_call(
        body, name=name, grid=(n // tn, m // tm), in_specs=in_specs,
        out_specs=pl.BlockSpec((tm, tn), lambda j, i: (i, j)),
        out_shape=jax.ShapeDtypeStruct((m, n), out_dtype),
        compiler_params=_params(("parallel", "parallel")),
    )(*args)


def _mm_tn(a, b, *, name, scale=1.0, out_dtype=F32, into=None, slot=0):
    t, m = a.shape
    n = b.shape[1]
    tm = _tile(m, (1024, 512, 256, 128))
    tn = _tile(n, _TN_CANDS)
    tk = _tile(t, (2048, 1024, 512, 256, 128))
    nk = t // tk

    def body(*refs):
        a_ref, b_ref = refs[:2]
        o_ref, acc_ref = refs[-2:]
        kk = pl.program_id(2)

        @pl.when(kk == 0)
        def _():
            acc_ref[...] = jnp.zeros_like(acc_ref)

        acc_ref[...] += _dot_tn(a_ref[...], b_ref[...])

        @pl.when(kk == nk - 1)
        def _():
            o_ref[...] = (acc_ref[...] * scale if scale != 1.0 else acc_ref[...]).astype(o_ref.dtype)

    in_specs = [pl.BlockSpec((tk, tm), lambda i, j, kk: (kk, i)), pl.BlockSpec((tk, tn), lambda i, j, kk: (kk, j))]
    args = [a, b]
    if into is None:
        out_spec = pl.BlockSpec((tm, tn), lambda i, j, kk: (i, j))
        out_shape = jax.ShapeDtypeStruct((m, n), out_dtype)
        alias = {}
    else:
        in_specs.append(pl.BlockSpec(memory_space=pl.ANY))
        args.append(into)
        out_spec = pl.BlockSpec((None, tm, tn), lambda i, j, kk: (slot, i, j))
        out_shape = jax.ShapeDtypeStruct(into.shape, into.dtype)
        alias = {2: 0}
    return pl.pallas_call(
        body, name=name, grid=(m // tm, n // tn, nk), in_specs=in_specs, out_specs=out_spec, out_shape=out_shape,
        scratch_shapes=[pltpu.VMEM((tm, tn), F32)], input_output_aliases=alias,
        compiler_params=_params(("parallel", "parallel", "arbitrary")),
    )(*args)


def _sigmoid(x):
    return jax.nn.sigmoid(x)


def _head_norm_gate(o, gate, nw):
    r = lax.rsqrt(jnp.mean(o * o, axis=-1, keepdims=True) + NORM_EPS)
    return o * r * nw * (gate * _sigmoid(gate))


def _hg_chunk(st, ql, fl, iv, gl, lb, nw):
    n_g, c, _ = ql.shape
    q = ql * _sigmoid(ql)
    f = lb + (1.0 - lb) * _sigmoid(fl)
    lf = jnp.log(jnp.maximum(f, F_MIN))
    k = 1.0 - f
    ri = lax.broadcasted_iota(jnp.int32, (c, c), 0)
    ci = lax.broadcasted_iota(jnp.int32, (c, c), 1)
    rl = lax.broadcasted_iota(jnp.int32, (c, LANES), 0)
    each = lambda m2: jnp.broadcast_to(m2[None], (n_g,) + m2.shape)
    b = _bmm_hi(each(jnp.where(ci <= ri, 1.0, 0.0)), lf)
    attn = jnp.where((ri == ci)[None], _bmm_nt(q, k), 0.0)
    sh = 0
    while (1 << sh) < c:
        m = 1 << sh
        ref = ((ri >> (sh + 1)) << (sh + 1)) + (m - 1)
        low_r = ((ri >> sh) & 1) == 1
        low_c = ((ci >> sh) & 1) == 1
        w_low = low_r & (ci > ref) & (ci <= ri)
        w_up = jnp.logical_not(low_r) & (ci > ri) & (ci <= ref)
        w = jnp.where(w_low | w_up, 1.0, 0.0)
        e = jnp.exp(_bmm_hi(each(w), lf))
        low_l = (((rl >> sh) & 1) == 1)[None]
        qs = jnp.where(low_l, q * e, 0.0)
        ks = jnp.where(low_l, 0.0, k * e)
        pair = ((ri >> (sh + 1)) == (ci >> (sh + 1))) & low_r & jnp.logical_not(low_c)
        attn = attn + jnp.where(pair[None], _bmm_nt(qs, ks), 0.0)
        sh += 1
    bl = jnp.sum(lf, axis=1, keepdims=True)
    o = _bmm(attn, iv) + _bmm_nt(q * jnp.exp(b), st)
    st_new = st * jnp.exp(bl) + _bmm_tn(iv, k * jnp.exp(bl - b))
    return _head_norm_gate(o, gl, nw), st_new


def _hg_specs(nb, n_c, rev):
    def cidx(cc):
        return (n_c - 1 - cc) if rev else cc

    def col(width):
        return pl.BlockSpec((nb, CHUNK, width), lambda cc: (0, cidx(cc), 0))

    vec = pl.BlockSpec((HG_HEADS, 1, LANES), lambda cc: (0, 0, 0))
    nw = pl.BlockSpec((1, LANES), lambda cc: (0, 0))
    st = pl.BlockSpec((nb, HG_HEADS, None, LANES, LANES), lambda cc: (0, 0, cidx(cc), 0, 0))
    acc = pl.BlockSpec((nb, HG_HEADS, 1, LANES), lambda cc: (0, 0, 0, 0))
    return col, vec, nw, st, acc


def _head(j):
    return slice(j * LANES, (j + 1) * LANES)


def _seq_heads(ref, nb, n, first=0):
    return jnp.stack([ref[b, :, _head(first + j)] for b in range(nb) for j in range(n)])


def _put_seq_heads(ref, val, nb, n, first=0):
    for b in range(nb):
        for j in range(n):
            ref[b, :, _head(first + j)] = val[b * n + j].astype(ref.dtype)


def _hgrn2_fwd(proj, lb, nw):
    nb, l, _ = proj.shape
    n_c = l // CHUNK
    n_g = nb * HG_HEADS
    col, vec, nws, st, _ = _hg_specs(nb, n_c, False)

    def body(p_ref, lb_ref, nw_ref, y_ref, st_ref, s_scr):
        @pl.when(pl.program_id(0) == 0)
        def _():
            s_scr[...] = jnp.zeros_like(s_scr)

        s_in = s_scr[...]
        st_ref[...] = s_in.reshape(st_ref.shape)
        args = [_seq_heads(p_ref, nb, HG_HEADS, k * HG_HEADS) for k in range(4)]
        y, s_new = _hg_chunk(s_in, *args, jnp.concatenate([lb_ref[...]] * nb), nw_ref[...])
        _put_seq_heads(y_ref, y, nb, HG_HEADS)
        s_scr[...] = s_new

    return pl.pallas_call(
        body, name="hgrn2_fwd", grid=(n_c,),
        in_specs=[col(4 * HG_QK), vec, nws], out_specs=[col(HG_QK), st],
        out_shape=[jax.ShapeDtypeStruct((nb, l, HG_QK), BF16),
                   jax.ShapeDtypeStruct((nb, HG_HEADS, n_c, LANES, LANES), F32)],
        scratch_shapes=[pltpu.VMEM((n_g, LANES, LANES), F32)],
        compiler_params=_params(("arbitrary",)),
    )(proj, lb, nw)


def _hgrn2_bwd(proj, lb, nw, states, dy):
    nb, l, _ = proj.shape
    n_c = l // CHUNK
    n_g = nb * HG_HEADS
    col, vec, nws, st, acc = _hg_specs(nb, n_c, True)

    def body(p_ref, lb_ref, nw_ref, st_ref, dy_ref, dp_ref, dlb_ref, dnw_ref, ds_scr):
        @pl.when(pl.program_id(0) == 0)
        def _():
            ds_scr[...] = jnp.zeros_like(ds_scr)
            dlb_ref[...] = jnp.zeros_like(dlb_ref)
            dnw_ref[...] = jnp.zeros_like(dnw_ref)

        args = [_seq_heads(p_ref, nb, HG_HEADS, k * HG_HEADS) for k in range(4)]
        _, vjp = jax.vjp(_hg_chunk, st_ref[...].reshape(n_g, LANES, LANES), *args,
                         jnp.concatenate([lb_ref[...]] * nb), nw_ref[...])
        ds, dql, dfl, div, dgl, dlb, dnw = vjp((_seq_heads(dy_ref, nb, HG_HEADS).astype(F32), ds_scr[...]))
        ds_scr[...] = ds
        for k, val in enumerate((dql, dfl, div, dgl)):
            _put_seq_heads(dp_ref, val, nb, HG_HEADS, k * HG_HEADS)
        dlb_ref[...] += dlb.reshape(dlb_ref.shape)
        dnw_ref[...] += dnw

    return pl.pallas_call(
        body, name="hgrn2_bwd", grid=(n_c,),
        in_specs=[col(4 * HG_QK), vec, nws, st, col(HG_QK)], out_specs=[col(4 * HG_QK), acc, nws],
        out_shape=[jax.ShapeDtypeStruct((nb, l, 4 * HG_QK), BF16),
                   jax.ShapeDtypeStruct((nb, HG_HEADS, 1, LANES), F32), jax.ShapeDtypeStruct((1, LANES), F32)],
        scratch_shapes=[pltpu.VMEM((n_g, LANES, LANES), F32)],
        compiler_params=_params(("arbitrary",)),
    )(proj, lb, nw, states, dy)


def _shift_down(x, s):
    if s == 0:
        return x
    rows = lax.broadcasted_iota(jnp.int32, x.shape, 0)
    return jnp.where(rows >= s, pltpu.roll(x, s, 0), 0.0)


def _shift_up(x, s):
    if s == 0:
        return x
    n = x.shape[0]
    rows = lax.broadcasted_iota(jnp.int32, x.shape, 0)
    return jnp.where(rows < n - s, pltpu.roll(x, n - s, 0), 0.0)


def _gdn_pre_fwd(proj, conv_w):
    nb, l, _ = proj.shape
    n_blk = GDN_QKV // LANES

    def body(x_ref, w_ref, o_ref):
        j = pl.program_id(1)
        x = x_ref[...]
        w = w_ref[...]
        c = w[3:4] * x
        for t in range(CONV_W - 1):
            c = c + w[t:t + 1] * _shift_down(x, CONV_W - 1 - t)
        s = c * _sigmoid(c)
        r = lax.rsqrt(jnp.sum(s * s, axis=-1, keepdims=True) + NORM_EPS)
        scale = jnp.where(j < GDN_HEADS, HEAD_DIM ** -0.5, 1.0)
        o_ref[...] = jnp.where(j < 2 * GDN_HEADS, s * r * scale, s)

    return pl.pallas_call(
        body, name="gdn_pre_fwd", grid=(nb, n_blk),
        in_specs=[pl.BlockSpec((None, l, LANES), lambda b, j: (b, 0, j)), pl.BlockSpec((CONV_W, LANES), lambda b, j: (0, j))],
        out_specs=pl.BlockSpec((None, l, LANES), lambda b, j: (b, 0, j)),
        out_shape=jax.ShapeDtypeStruct((nb, l, GDN_QKV), F32),
        compiler_params=_params(("parallel", "parallel")),
    )(proj, conv_w)


def _gdn_pre_bwd(proj, conv_w, dout):
    nb, l, _ = proj.shape
    n_blk = GDN_QKV // LANES

    def body(x_ref, w_ref, d_ref, dx_ref, dw_ref):
        j = pl.program_id(1)
        x = x_ref[...]
        w = w_ref[...]
        xs = [_shift_down(x, CONV_W - 1 - t) for t in range(CONV_W)]
        c = w[0:1] * xs[0]
        for t in range(1, CONV_W):
            c = c + w[t:t + 1] * xs[t]
        sg = _sigmoid(c)
        s = c * sg
        d = d_ref[...]
        r = lax.rsqrt(jnp.sum(s * s, axis=-1, keepdims=True) + NORM_EPS)
        scale = jnp.where(j < GDN_HEADS, HEAD_DIM ** -0.5, 1.0)
        ds_norm = scale * r * (d - s * (r * r) * jnp.sum(d * s, axis=-1, keepdims=True))
        ds = jnp.where(j < 2 * GDN_HEADS, ds_norm, d)
        dc = ds * (sg * (1.0 + c * (1.0 - sg)))
        dx = w[3:4] * dc
        for t in range(CONV_W - 1):
            dx = dx + w[t:t + 1] * _shift_up(dc, CONV_W - 1 - t)
        dx_ref[...] = dx.astype(dx_ref.dtype)
        for t in range(CONV_W):
            dw_ref[t:t + 1, :] = jnp.sum(dc * xs[t], axis=0, keepdims=True)

    return pl.pallas_call(
        body, name="gdn_pre_bwd", grid=(nb, n_blk),
        in_specs=[pl.BlockSpec((None, l, LANES), lambda b, j: (b, 0, j)), pl.BlockSpec((CONV_W, LANES), lambda b, j: (0, j)),
                  pl.BlockSpec((None, l, LANES), lambda b, j: (b, 0, j))],
        out_specs=[pl.BlockSpec((None, l, LANES), lambda b, j: (b, 0, j)), pl.BlockSpec((None, CONV_W, LANES), lambda b, j: (b, 0, j))],
        out_shape=[jax.ShapeDtypeStruct((nb, l, GDN_QKV), BF16), jax.ShapeDtypeStruct((nb, CONV_W, GDN_QKV), F32)],
        compiler_params=_params(("parallel", "parallel")),
    )(proj, conv_w, dout)


def _gdn_chunk(st, q, k, v, gate, bl, al, alog, dtb, nw):
    n_g, c, _ = q.shape
    beta = _sigmoid(bl)
    x = al + dtb
    softplus = jnp.maximum(x, 0.0) + jnp.log(1.0 + jnp.exp(-jnp.abs(x)))
    la = -jnp.exp(alog) * softplus
    ri = lax.broadcasted_iota(jnp.int32, (c, c), 0)
    ci = lax.broadcasted_iota(jnp.int32, (c, c), 1)
    lower = (ci <= ri)[None]
    strict = (ci < ri)[None]
    ltri = jnp.broadcast_to(jnp.where(lower, 1.0, 0.0), (n_g, c, c))
    la_l = la + jnp.zeros((1, 1, LANES), F32)
    g = _bmm_hi(ltri, la_l)
    delta = _bmm_hi(ltri, jnp.where(strict, la + jnp.zeros((1, 1, c), F32), 0.0))
    gam = jnp.where(lower, jnp.exp(jnp.where(lower, delta, 0.0)), 0.0)
    kb = k * beta
    vb = v * beta
    m = jnp.where(strict, _bmm_nt(kb, k) * gam, 0.0)
    pw = -m
    t_inv = jnp.where((ri == ci)[None], 1.0, 0.0) + pw
    steps = int(math.log2(c)) - 1
    for _ in range(steps):
        pw = _bmm_hi(pw, pw)
        t_inv = t_inv + _bmm_hi(t_inv, pw)
    eg = jnp.exp(g)
    u = _bmm(t_inv, vb)
    w = _bmm(t_inv, kb * eg)
    attn = jnp.where(lower, _bmm_nt(q, k) * gam, 0.0)
    v_new = u - _bmm_nt(w, st)
    o = _bmm_nt(q * eg, st) + _bmm(attn, v_new)
    g_last = jnp.sum(la_l, axis=1, keepdims=True)
    st_new = st * jnp.exp(g_last) + _bmm_tn(v_new, k * jnp.exp(g_last - g))
    return _head_norm_gate(o, gate, nw), st_new


GDN_WIDE = GDN_HEADS * LANES
GDN_SMALL_BLOCK = (GDN_QKV + GDN_WIDE) // LANES


def _gdn_specs(nb, n_c, rev):
    def cidx(cc):
        return (n_c - 1 - cc) if rev else cc

    def col(width, blk=0):
        return pl.BlockSpec((nb, CHUNK, width), lambda cc: (0, cidx(cc), blk))

    scal = pl.BlockSpec((GDN_HEADS, 1, 1), lambda cc: (0, 0, 0))
    nw = pl.BlockSpec((1, LANES), lambda cc: (0, 0))
    st = pl.BlockSpec((nb, GDN_HEADS, None, LANES, LANES), lambda cc: (0, 0, cidx(cc), 0, 0))
    acc_s = pl.BlockSpec((nb, GDN_HEADS, 1, 1), lambda cc: (0, 0, 0, 0))
    return col, scal, nw, st, acc_s


def _tok_cols(xs, nb, first):
    return jnp.stack([xs[b][:, first + h:first + h + 1] for b in range(nb) for h in range(GDN_HEADS)])


def _gdn_inputs(qkv_ref, p_gate_ref, p_small_ref, alog_ref, dtb_ref, nw_ref, nb):
    q, k, v = (_seq_heads(qkv_ref, nb, GDN_HEADS, i * GDN_HEADS) for i in range(3))
    xs = p_small_ref[...]
    return (q, k, v, _seq_heads(p_gate_ref, nb, GDN_HEADS), _tok_cols(xs, nb, 0), _tok_cols(xs, nb, GDN_HEADS),
            jnp.concatenate([alog_ref[...]] * nb), jnp.concatenate([dtb_ref[...]] * nb), nw_ref[...])


def _gdn_fwd(qkv, proj, alog, dtb, nw):
    nb, l, _ = qkv.shape
    n_c = l // CHUNK
    n_g = nb * GDN_HEADS
    col, scal, nws, st, _ = _gdn_specs(nb, n_c, False)

    def body(qkv_ref, pg_ref, ps_ref, alog_ref, dtb_ref, nw_ref, y_ref, st_ref, s_scr):
        @pl.when(pl.program_id(0) == 0)
        def _():
            s_scr[...] = jnp.zeros_like(s_scr)

        s_in = s_scr[...]
        st_ref[...] = s_in.reshape(st_ref.shape)
        y, s_new = _gdn_chunk(s_in, *_gdn_inputs(qkv_ref, pg_ref, ps_ref, alog_ref, dtb_ref, nw_ref, nb))
        _put_seq_heads(y_ref, y, nb, GDN_HEADS)
        s_scr[...] = s_new

    return pl.pallas_call(
        body, name="gdn_fwd", grid=(n_c,),
        in_specs=[col(GDN_QKV), col(GDN_WIDE, GDN_QKV // GDN_WIDE), col(LANES, GDN_SMALL_BLOCK), scal, scal, nws],
        out_specs=[col(GDN_WIDE), st],
        out_shape=[jax.ShapeDtypeStruct((nb, l, GDN_WIDE), BF16),
                   jax.ShapeDtypeStruct((nb, GDN_HEADS, n_c, LANES, LANES), F32)],
        scratch_shapes=[pltpu.VMEM((n_g, LANES, LANES), F32)],
        compiler_params=_params(("arbitrary",)),
    )(qkv, proj, proj, alog, dtb, nw)


def _gdn_bwd(qkv, proj, alog, dtb, nw, states, dy):
    nb, l, _ = qkv.shape
    n_c = l // CHUNK
    n_g = nb * GDN_HEADS
    col, scal, nws, st, acc_s = _gdn_specs(nb, n_c, True)

    def body(qkv_ref, pg_ref, ps_ref, alog_ref, dtb_ref, nw_ref, st_ref, dy_ref,
             dqkv_ref, dg_ref, dsm_ref, dalog_ref, ddtb_ref, dnw_ref, ds_scr):
        @pl.when(pl.program_id(0) == 0)
        def _():
            ds_scr[...] = jnp.zeros_like(ds_scr)
            dalog_ref[...] = jnp.zeros_like(dalog_ref)
            ddtb_ref[...] = jnp.zeros_like(ddtb_ref)
            dnw_ref[...] = jnp.zeros_like(dnw_ref)

        _, vjp = jax.vjp(_gdn_chunk, st_ref[...].reshape(n_g, LANES, LANES),
                         *_gdn_inputs(qkv_ref, pg_ref, ps_ref, alog_ref, dtb_ref, nw_ref, nb))
        ds, dq, dk, dv, dg, dbl, dal, dalog, ddtb, dnw = vjp((_seq_heads(dy_ref, nb, GDN_HEADS).astype(F32), ds_scr[...]))
        ds_scr[...] = ds
        for i, val in enumerate((dq, dk, dv)):
            _put_seq_heads(dqkv_ref, val, nb, GDN_HEADS, i * GDN_HEADS)
        _put_seq_heads(dg_ref, dg, nb, GDN_HEADS)
        lane = lax.broadcasted_iota(jnp.int32, (CHUNK, LANES), 1)
        for b in range(nb):
            small = jnp.zeros((CHUNK, LANES), F32)
            for h in range(GDN_HEADS):
                small = small + jnp.where(lane == h, dbl[b * GDN_HEADS + h], 0.0)
                small = small + jnp.where(lane == GDN_HEADS + h, dal[b * GDN_HEADS + h], 0.0)
            dsm_ref[b] = small.astype(dsm_ref.dtype)
        dalog_ref[...] += dalog.reshape(dalog_ref.shape)
        ddtb_ref[...] += ddtb.reshape(ddtb_ref.shape)
        dnw_ref[...] += dnw

    h = GDN_HEADS
    return pl.pallas_call(
        body, name="gdn_bwd", grid=(n_c,),
        in_specs=[col(GDN_QKV), col(GDN_WIDE, GDN_QKV // GDN_WIDE), col(LANES, GDN_SMALL_BLOCK), scal, scal, nws, st,
                  col(GDN_WIDE)],
        out_specs=[col(GDN_QKV), col(GDN_WIDE), col(LANES), acc_s, acc_s, nws],
        out_shape=[jax.ShapeDtypeStruct((nb, l, GDN_QKV), F32), jax.ShapeDtypeStruct((nb, l, GDN_WIDE), BF16),
                   jax.ShapeDtypeStruct((nb, l, LANES), BF16),
                   jax.ShapeDtypeStruct((nb, h, 1, 1), F32), jax.ShapeDtypeStruct((nb, h, 1, 1), F32),
                   jax.ShapeDtypeStruct((1, LANES), F32)],
        scratch_shapes=[pltpu.VMEM((n_g, LANES, LANES), F32)],
        compiler_params=_params(("arbitrary",)),
    )(qkv, proj, proj, alog, dtb, nw, states, dy)


S5_ROWS = S5_NSTATE // LANES
S5_TB = 256


def _s5_scan_fwd(bu, abar):
    nb, l = bu.shape[:2]
    tb = min(S5_TB, l)

    def body(bu_ref, a_ref, h_ref, c_scr):
        @pl.when(pl.program_id(1) == 0)
        def _():
            c_scr[...] = jnp.zeros_like(c_scr)

        ar = a_ref[0]
        ai = a_ref[1]

        def step(t, carry):
            hr, hi = carry
            nr = ar * hr - ai * hi + bu_ref[t, 0].astype(F32)
            ni = ar * hi + ai * hr + bu_ref[t, 1].astype(F32)
            h_ref[t, 0] = nr.astype(h_ref.dtype)
            h_ref[t, 1] = ni.astype(h_ref.dtype)
            return nr, ni

        hr, hi = lax.fori_loop(0, tb, step, (c_scr[0], c_scr[1]), unroll=8)
        c_scr[0] = hr
        c_scr[1] = hi

    blk = pl.BlockSpec((None, tb, 2, S5_ROWS, LANES), lambda b, i: (b, i, 0, 0, 0))
    return pl.pallas_call(
        body, name="s5_scan_fwd", grid=(nb, l // tb),
        in_specs=[blk, pl.BlockSpec((2, S5_ROWS, LANES), lambda b, i: (0, 0, 0))],
        out_specs=blk, out_shape=jax.ShapeDtypeStruct(bu.shape, bu.dtype),
        scratch_shapes=[pltpu.VMEM((2, S5_ROWS, LANES), F32)],
        compiler_params=_params(("parallel", "arbitrary")),
    )(bu, abar)


def _s5_scan_bwd(dh, h, abar):
    nb, l = dh.shape[:2]
    tb = min(S5_TB, l)
    n_blk = l // tb

    def body(dh_ref, h_ref, a_ref, g_ref, da_ref, c_scr):
        @pl.when(pl.program_id(1) == 0)
        def _():
            c_scr[...] = jnp.zeros_like(c_scr)
            da_ref[...] = jnp.zeros_like(da_ref)

        ar = a_ref[0]
        ai = a_ref[1]

        def step(s, carry):
            gr, gi, dar, dai = carry
            t = tb - 1 - s
            hr = h_ref[t, 0].astype(F32)
            hi = h_ref[t, 1].astype(F32)
            dar = dar + gr * hr + gi * hi
            dai = dai + gi * hr - gr * hi
            nr = ar * gr + ai * gi + dh_ref[t, 0].astype(F32)
            ni = ar * gi - ai * gr + dh_ref[t, 1].astype(F32)
            g_ref[t, 0] = nr.astype(g_ref.dtype)
            g_ref[t, 1] = ni.astype(g_ref.dtype)
            return nr, ni, dar, dai

        z = jnp.zeros((S5_ROWS, LANES), F32)
        gr, gi, dar, dai = lax.fori_loop(0, tb, step, (c_scr[0], c_scr[1], z, z), unroll=8)
        c_scr[0] = gr
        c_scr[1] = gi
        da_ref[0] += dar
        da_ref[1] += dai

    blk = pl.BlockSpec((None, tb, 2, S5_ROWS, LANES), lambda b, i: (b, n_blk - 1 - i, 0, 0, 0))
    return pl.pallas_call(
        body, name="s5_scan_bwd", grid=(nb, n_blk),
        in_specs=[blk, blk, pl.BlockSpec((2, S5_ROWS, LANES), lambda b, i: (0, 0, 0))],
        out_specs=[blk, pl.BlockSpec((None, 2, S5_ROWS, LANES), lambda b, i: (b, 0, 0, 0))],
        out_shape=[jax.ShapeDtypeStruct(dh.shape, dh.dtype), jax.ShapeDtypeStruct((nb, 2, S5_ROWS, LANES), F32)],
        scratch_shapes=[pltpu.VMEM((2, S5_ROWS, LANES), F32)],
        compiler_params=_params(("parallel", "arbitrary")),
    )(dh, h, abar)


_GELU_C = math.sqrt(2.0 / math.pi)


def _gelu(x):
    return 0.5 * x * (1.0 + jnp.tanh(_GELU_C * (x + 0.044715 * x * x * x)))


def _gelu_grad(x):
    t = jnp.tanh(_GELU_C * (x + 0.044715 * x * x * x))
    return 0.5 * (1.0 + t) + 0.5 * x * (1.0 - t * t) * _GELU_C * (1.0 + 3.0 * 0.044715 * x * x)


def _s5_post_fwd(yc, u, d, w_glu):
    t, w = yc.shape
    tt = _tile(t, _TOKEN_TILES)

    def body(yc_ref, u_ref, d_ref, w_ref, o_ref, y0_ref):
        y0 = yc_ref[...] + d_ref[...] * u_ref[...]
        y = _gelu(y0)
        z = _dot(y, w_ref[...])
        o_ref[...] = (y * _sigmoid(z)).astype(o_ref.dtype)
        y0_ref[...] = y0

    tok = pl.BlockSpec((tt, w), lambda i: (i, 0))
    return pl.pallas_call(
        body, name="s5_post_fwd", grid=(t // tt,),
        in_specs=[tok, tok, pl.BlockSpec((1, w), lambda i: (0, 0)), pl.BlockSpec((w, w), lambda i: (0, 0))],
        out_specs=[tok, tok],
        out_shape=[jax.ShapeDtypeStruct((t, w), BF16), jax.ShapeDtypeStruct((t, w), F32)],
        compiler_params=_params(("parallel",)),
    )(yc, u, d, w_glu)


def _s5_post_bwd(y0, u, d, w_glu, dout):
    t, w = y0.shape
    tt = _tile(t, _TOKEN_TILES)

    def body(y0_ref, u_ref, d_ref, w_ref, do_ref, dy0_ref, y_ref, dz_ref, du_ref, dd_ref):
        @pl.when(pl.program_id(0) == 0)
        def _():
            dd_ref[...] = jnp.zeros_like(dd_ref)

        y0 = y0_ref[...]
        y = _gelu(y0)
        s = _sigmoid(_dot(y, w_ref[...]))
        do = do_ref[...]
        dz = do * y * s * (1.0 - s)
        dy = do * s + _dot_nt(dz, w_ref[...])
        dy0 = dy * _gelu_grad(y0)
        dy0_ref[...] = dy0.astype(dy0_ref.dtype)
        y_ref[...] = y.astype(y_ref.dtype)
        dz_ref[...] = dz.astype(dz_ref.dtype)
        du_ref[...] = dy0 * d_ref[...]
        dd_ref[...] += jnp.sum(dy0 * u_ref[...], axis=0, keepdims=True)

    tok = pl.BlockSpec((tt, w), lambda i: (i, 0))
    vec = pl.BlockSpec((1, w), lambda i: (0, 0))
    return pl.pallas_call(
        body, name="s5_post_bwd", grid=(t // tt,),
        in_specs=[tok, tok, vec, pl.BlockSpec((w, w), lambda i: (0, 0)), tok],
        out_specs=[tok, tok, tok, tok, vec],
        out_shape=[jax.ShapeDtypeStruct((t, w), BF16), jax.ShapeDtypeStruct((t, w), BF16),
                   jax.ShapeDtypeStruct((t, w), BF16), jax.ShapeDtypeStruct((t, w), F32),
                   jax.ShapeDtypeStruct((1, w), F32)],
        compiler_params=_params(("arbitrary",)),
    )(y0, u, d, w_glu, dout)


def _s5_params(a_re, a_im, b_re, b_im, c_re, c_im, log_dt):
    dt = jnp.exp(log_dt)[:, None]
    mag = jnp.exp(dt * a_re)
    ang = dt * a_im
    abar_re = mag * jnp.cos(ang)
    abar_im = mag * jnp.sin(ang)
    den = a_re * a_re + a_im * a_im
    zr = abar_re - 1.0
    zi = abar_im
    coef_re = ((zr * a_re + zi * a_im) / den)[..., None]
    coef_im = ((zi * a_re - zr * a_im) / den)[..., None]
    bb_re = coef_re * b_re - coef_im * b_im
    bb_im = coef_re * b_im + coef_im * b_re
    eye = jnp.eye(S5_GROUPS, dtype=F32)

    def dense_in(bb):
        return jnp.einsum('gnp,gh->gphn', bb, eye).reshape(S5_WIDTH, S5_NSTATE)

    def dense_out(cc):
        return jnp.einsum('gpn,gh->gnhp', cc, eye).reshape(S5_NSTATE, S5_WIDTH)

    abar = jnp.stack([abar_re.reshape(S5_ROWS, LANES), abar_im.reshape(S5_ROWS, LANES)])
    bbd = jnp.concatenate([dense_in(bb_re), dense_in(bb_im)], axis=1)
    ccd = jnp.concatenate([dense_out(c_re), dense_out(-c_im)], axis=0)
    return abar, bbd, ccd


def _tok_tile(t):
    return _tile(t, _TOKEN_TILES[1:])


def _rms_fwd(x, w):
    t, d = x.shape
    tt = _tok_tile(t)

    def body(x_ref, w_ref, o_ref):
        xv = x_ref[...]
        r = lax.rsqrt(jnp.mean(xv * xv, axis=-1, keepdims=True) + NORM_EPS)
        o_ref[...] = (xv * r * w_ref[...]).astype(o_ref.dtype)

    tok = pl.BlockSpec((tt, d), lambda i: (i, 0))
    return pl.pallas_call(
        body, name="rms_fwd", grid=(t // tt,), in_specs=[tok, pl.BlockSpec((1, d), lambda i: (0, 0))],
        out_specs=tok, out_shape=jax.ShapeDtypeStruct((t, d), BF16), compiler_params=_params(("parallel",)),
    )(x, w)


def _rms_bwd_math(xv, wv, dy):
    r = lax.rsqrt(jnp.mean(xv * xv, axis=-1, keepdims=True) + NORM_EPS)
    xh = xv * r
    dxh = dy * wv
    dx = r * (dxh - xh * jnp.mean(dxh * xh, axis=-1, keepdims=True))
    return dx, jnp.sum(dy * xh, axis=0, keepdims=True)


def _rms_bwd(x, w, dy, dres):
    t, d = x.shape
    tt = _tok_tile(t)

    def body(x_ref, w_ref, dy_ref, dr_ref, dx_ref, dw_ref):
        @pl.when(pl.program_id(0) == 0)
        def _():
            dw_ref[...] = jnp.zeros_like(dw_ref)

        dx, dw = _rms_bwd_math(x_ref[...], w_ref[...], dy_ref[...])
        dx_ref[...] = dr_ref[...] + dx
        dw_ref[...] += dw

    tok = pl.BlockSpec((tt, d), lambda i: (i, 0))
    vec = pl.BlockSpec((1, d), lambda i: (0, 0))
    return pl.pallas_call(
        body, name="rms_bwd", grid=(t // tt,), in_specs=[tok, vec, tok, tok], out_specs=[tok, vec],
        out_shape=[jax.ShapeDtypeStruct((t, d), F32), jax.ShapeDtypeStruct((1, d), F32)],
        compiler_params=_params(("arbitrary",)),
    )(x, w, dy, dres)


def _loss_head(x, w, target):
    t, d = x.shape
    tt = _tok_tile(t)

    def body(x_ref, w_ref, t_ref, l_ref, dx_ref, dw_ref):
        @pl.when(pl.program_id(0) == 0)
        def _():
            l_ref[...] = jnp.zeros_like(l_ref)
            dw_ref[...] = jnp.zeros_like(dw_ref)

        xv = x_ref[...]
        wv = w_ref[...]
        r = lax.rsqrt(jnp.mean(xv * xv, axis=-1, keepdims=True) + NORM_EPS)
        err = xv * r * wv - t_ref[...]
        row = jnp.sum(err * err, axis=-1, keepdims=True)
        l_ref[...] += (0.5 / d) * jnp.sum(row, axis=0, keepdims=True)
        dx, dw = _rms_bwd_math(xv, wv, err * (1.0 / d))
        dx_ref[...] = dx
        dw_ref[...] += dw

    tok = pl.BlockSpec((tt, d), lambda i: (i, 0))
    vec = pl.BlockSpec((1, d), lambda i: (0, 0))
    return pl.pallas_call(
        body, name="loss_head", grid=(t // tt,), in_specs=[tok, vec, tok],
        out_specs=[pl.BlockSpec((1, 1), lambda i: (0, 0)), tok, vec],
        out_shape=[jax.ShapeDtypeStruct((1, 1), F32), jax.ShapeDtypeStruct((t, d), F32), jax.ShapeDtypeStruct((1, d), F32)],
        compiler_params=_params(("arbitrary",)),
    )(x, w, target)


FF_PAD = 3072
FF_SHARD = D_FF // N_DEV
FF_SHARD_PAD = FF_PAD // N_DEV


def _ffn_in(x, nw, wg_t, wu_t):
    t, d = x.shape
    tm = _tile(t, _TOKEN_TILES)
    tn = 1024

    def body(x_ref, nw_ref, wg_ref, wu_ref, h_ref, dadg_ref, dadu_ref, a_ref):
        @pl.when(pl.program_id(1) == 0)
        def _():
            xv = x_ref[...]
            r = lax.rsqrt(jnp.mean(xv * xv, axis=-1, keepdims=True) + NORM_EPS)
            h_ref[...] = (xv * r * nw_ref[...]).astype(h_ref.dtype)

        h = h_ref[...]
        g = _dot_nt(h, wg_ref[...])
        u = _dot_nt(h, wu_ref[...])
        s = _sigmoid(g)
        silu = g * s
        dadg_ref[...] = (u * (s + silu * (1.0 - s))).astype(dadg_ref.dtype)
        dadu_ref[...] = silu.astype(dadu_ref.dtype)
        a_ref[...] = (silu * u).astype(a_ref.dtype)

    tok = pl.BlockSpec((tm, d), lambda i, j: (i, 0))
    wsp = pl.BlockSpec((tn, d), lambda i, j: (j, 0))
    wide = pl.BlockSpec((tm, tn), lambda i, j: (i, j))
    return pl.pallas_call(
        body, name="ffn_in", grid=(t // tm, FF_PAD // tn),
        in_specs=[tok, pl.BlockSpec((1, d), lambda i, j: (0, 0)), wsp, wsp], out_specs=[tok, wide, wide, wide],
        out_shape=[jax.ShapeDtypeStruct((t, d), BF16)] + [jax.ShapeDtypeStruct((t, FF_PAD), BF16)] * 3,
        compiler_params=_params(("parallel", "arbitrary")),
    )(x, nw, wg_t, wu_t)


def _ffn_out_dx(dxo, wd, dadg, dadu, after=None):
    t, d = dxo.shape
    tm = _tile(t, _TOKEN_TILES)
    tn = 1024

    def body(dx_ref, wd_ref, dadg_ref, dadu_ref, *rest):
        dg_ref, du_ref = rest[-2:]
        da = 0.5 * _dot_nt(dx_ref[...], wd_ref[...])
        dg_ref[...] = (da * dadg_ref[...].astype(F32)).astype(dg_ref.dtype)
        du_ref[...] = (da * dadu_ref[...].astype(F32)).astype(du_ref.dtype)

    wide = pl.BlockSpec((tm, tn), lambda j, i: (i, j))
    return pl.pallas_call(
        body, name="ffn_out_dx", grid=(FF_PAD // tn, t // tm),
        in_specs=[pl.BlockSpec((tm, d), lambda j, i: (i, 0)), pl.BlockSpec((tn, d), lambda j, i: (j, 0)), wide, wide]
        + [pl.BlockSpec(memory_space=pl.ANY)] * (after is not None),
        out_specs=[wide, wide], out_shape=[jax.ShapeDtypeStruct((t, FF_PAD), BF16)] * 2,
        compiler_params=_params(("parallel", "parallel")),
    )(dxo, wd, dadg, dadu, *([after] if after is not None else []))


def _ffn_in_dx(dg, du, wg_t, wu_t, x, nw, dxo, after=None):
    t, d = x.shape
    tm = _tile(t, (256, 128))

    def body(dg_ref, du_ref, wg_ref, wu_ref, x_ref, nw_ref, dr_ref, *rest):
        dx_ref, dw_ref = rest[-2:]

        @pl.when(pl.program_id(0) == 0)
        def _():
            dw_ref[...] = jnp.zeros_like(dw_ref)

        dh = _dot(dg_ref[...], wg_ref[...]) + _dot(du_ref[...], wu_ref[...])
        dx, dw = _rms_bwd_math(x_ref[...], nw_ref[...], dh)
        dx_ref[...] = dr_ref[...] + dx
        dw_ref[...] += dw

    wide = pl.BlockSpec((tm, FF_PAD), lambda i: (i, 0))
    wsp = pl.BlockSpec((FF_PAD, d), lambda i: (0, 0))
    tok = pl.BlockSpec((tm, d), lambda i: (i, 0))
    vec = pl.BlockSpec((1, d), lambda i: (0, 0))
    return pl.pallas_call(
        body, name="ffn_in_dx", grid=(t // tm,),
        in_specs=[wide, wide, wsp, wsp, tok, vec, tok] + [pl.BlockSpec(memory_space=pl.ANY)] * (after is not None),
        out_specs=[tok, vec], out_shape=[jax.ShapeDtypeStruct((t, d), F32), jax.ShapeDtypeStruct((1, d), F32)],
        compiler_params=_params(("arbitrary",)),
    )(dg, du, wg_t, wu_t, x, nw, dxo, *([after] if after is not None else []))


def _adamw(w, g, m, v):
    r, c = w.shape
    tr = _tile(r, (512, 256, 128, 64, 32, 16, 8))
    c1 = 1.0 - ADAM_B1 ** ADAM_STEP
    c2 = 1.0 - ADAM_B2 ** ADAM_STEP

    def body(w_ref, g_ref, m_ref, v_ref, d_ref, nm_ref, nv_ref):
        gv = g_ref[...]
        nm = ADAM_B1 * m_ref[...] + (1.0 - ADAM_B1) * gv
        nv = ADAM_B2 * v_ref[...] + (1.0 - ADAM_B2) * (gv * gv)
        d_ref[...] = -ADAM_LR * ((nm / c1) / (jnp.sqrt(nv / c2) + ADAM_EPS) + ADAM_WD * w_ref[...])
        nm_ref[...] = nm
        nv_ref[...] = nv

    blk = pl.BlockSpec((tr, c), lambda i: (i, 0))
    return pl.pallas_call(
        body, name="adamw", grid=(r // tr,), in_specs=[blk] * 4, out_specs=[blk] * 3,
        out_shape=[jax.ShapeDtypeStruct((r, c), F32)] * 3, compiler_params=_params(("parallel",)),
    )(w, g, m, v)


N_KINDS = 3


def _ffn_slot(which, kind):
    return which * N_KINDS + kind


def _ffn_fwd(x, nw, w_ffn, which):
    wg, wu, wd = (w_ffn[_ffn_slot(which, kind)] for kind in range(3))
    h, dadg, dadu, a = _ffn_in(x, nw, wg, wu)
    return _mm(a, wd, name="ffn_out", res=x, scale=0.5, token_tiles=_TOKEN_TILES[1:]), (x, h, dadg, dadu, a)


def _ffn_bwd(dxo, res, nw, w_ffn, which, after=None, done=None):
    x, h, dadg, dadu, a = res
    wg, wu, wd = (w_ffn[_ffn_slot(which, kind)] for kind in range(3))
    dg, du = _ffn_out_dx(dxo, wd, dadg, dadu, after)
    g_ffn = lax.empty((N_KINDS,) + wd.shape, BF16)
    g_ffn = _mm_tn(a, dxo, name="ffn_out_dw", scale=0.5, into=g_ffn, slot=2)
    g_ffn = _mm_tn(dg, h, name="ffn_in_dw", into=g_ffn, slot=0)
    g_ffn = _mm_tn(du, h, name="ffn_in_dw", into=g_ffn, slot=1)
    token = done(g_ffn) if done is not None else None
    dx, dnw = _ffn_in_dx(dg, du, wg, wu, x, nw, dxo, token)
    return dx, dnw, g_ffn, token


def _hg_lower_bounds(logits):
    p = jax.nn.softmax(logits, axis=0)
    return jnp.cumsum(p, axis=0) - p[0]


def _even_fwd(x1, p, nb, l):
    t = nb * l
    h = _rms_fwd(x1, p["mix_norm"])
    proj = _mm(h, p["w_in_t"], name="ev_in", nt=True)
    proj3 = proj.reshape(nb, l, EV_IN)
    ya, hst = _hgrn2_fwd(proj3, p["lb"], p["hg_nw"])
    u = proj[:, 4 * HG_QK:]
    bu = _mm(u, p["bbd"], name="s5_in", out_dtype=BF16)
    hs = _s5_scan_fwd(bu.reshape(nb, l, 2, S5_ROWS, LANES), p["abar"])
    yc = _mm(hs.reshape(t, 2 * S5_NSTATE), p["ccd"], name="s5_out")
    yb, y0 = _s5_post_fwd(yc, u, p["s5_d"], p["w_glu"])
    ycat = jnp.concatenate([ya.reshape(t, HG_QK), yb], axis=1)
    x2 = _mm(ycat, p["w_out"], name="mix_out", res=x1)
    return x2, (x1, h, proj3, hst, u, hs, y0, ycat)


def _even_bwd(dx2, res, p, nb, l):
    x1, h, proj3, hst, u, hs, y0, ycat = res
    t = nb * l
    dycat = _mm(dx2, p["w_out"], name="mix_out_dx", nt=True)
    dw_out = _mm_tn(ycat, dx2, name="mix_out_dw", out_dtype=BF16)
    dp_hg, dlb, dhg_nw = _hgrn2_bwd(proj3, p["lb"], p["hg_nw"], hst, dycat.reshape(nb, l, D_MODEL))
    dy0, y, dz, du_d, dd = _s5_post_bwd(y0, u, p["s5_d"], p["w_glu"], dycat[:, HG_QK:])
    dw_glu = _mm_tn(y, dz, name="s5_glu_dw", out_dtype=BF16)
    dhs = _mm(dy0, p["ccd"], name="s5_out_dx", nt=True, out_dtype=BF16)
    dccd = _mm_tn(hs.reshape(t, 2 * S5_NSTATE), dy0, name="s5_out_dw")
    g5, dabar = _s5_scan_bwd(dhs.reshape(nb, l, 2, S5_ROWS, LANES), hs, p["abar"])
    g2 = g5.reshape(t, 2 * S5_NSTATE)
    du = _mm(g2, p["bbd"], name="s5_in_dx", nt=True, res=du_d, out_dtype=BF16)
    dbbd = _mm_tn(u, g2, name="s5_in_dw")
    dproj = jnp.concatenate([dp_hg.reshape(t, 4 * HG_QK), du], axis=1)
    dw_in_t = _mm_tn(dproj, h, name="ev_in_dw", out_dtype=BF16)
    dh = _mm(dproj, p["w_in_t"], name="ev_in_dx")
    dx1, dmix = _rms_bwd(x1, p["mix_norm"], dh, dx2)
    small = dict(mix_norm=dmix, lb=dlb.sum(0).reshape(HG_QK), hg_nw=dhg_nw.reshape(HEAD_DIM),
                 abar=dabar.sum(0), bbd=dbbd, ccd=dccd, s5_d=dd.reshape(S5_WIDTH))
    return dx1, dict(w_in_t=dw_in_t, w_glu=dw_glu, w_out=dw_out), small


def _odd_fwd(x1, p, nb, l):
    t = nb * l
    h = _rms_fwd(x1, p["mix_norm"])
    proj3 = _mm(h, p["w_in"], name="od_in").reshape(nb, l, OD_IN_PAD)
    qkv = _gdn_pre_fwd(proj3, p["conv_w"])
    y, st = _gdn_fwd(qkv, proj3, p["a_log"], p["dt_bias"], p["gdn_nw"])
    y2 = y.reshape(t, D_MODEL)
    x2 = _mm(y2, p["w_out"], name="mix_out", res=x1)
    return x2, (x1, h, proj3, qkv, st, y2)


def _odd_bwd(dx2, res, p, nb, l):
    x1, h, proj3, qkv, st, y2 = res
    t = nb * l
    dy = _mm(dx2, p["w_out"], name="mix_out_dx", nt=True)
    dw_out = _mm_tn(y2, dx2, name="mix_out_dw", out_dtype=BF16)
    dqkv, dgate, dsmall, dalog, ddtb, dnw = _gdn_bwd(
        qkv, proj3, p["a_log"], p["dt_bias"], p["gdn_nw"], st, dy.reshape(nb, l, D_MODEL))
    dx_qkv, dcw = _gdn_pre_bwd(proj3, p["conv_w"], dqkv)
    dproj = jnp.concatenate([dx_qkv, dgate, dsmall], axis=-1).reshape(t, OD_IN_PAD)
    dw_in = _mm_tn(h, dproj, name="od_in_dw", out_dtype=BF16)[:, :OD_IN]
    dh = _mm(dproj, p["w_in"], name="od_in_dx", nt=True)
    dx1, dmix = _rms_bwd(x1, p["mix_norm"], dh, dx2)
    small = dict(mix_norm=dmix, a_log=dalog.sum(0).reshape(GDN_HEADS), dt_bias=ddtb.sum(0).reshape(GDN_HEADS),
                 gdn_nw=dnw.reshape(HEAD_DIM))
    return dx1, dict(w_in=dw_in, conv_w=dcw.sum(0), w_out=dw_out), small


def _local_step(x, target, layer_weights, small, layer_done):
    nb, l, _ = x.shape
    t = nb * l
    lbs, lbs_vjp = jax.vjp(_hg_lower_bounds, small["hg_lb_logits"])
    row = lambda v: v.reshape(1, -1)
    s5_vjps = {}

    def layer_params(layer, big):
        j = layer // 2
        p = dict(mix_norm=row(small["mix_norm"][layer]), w_out=big["w_out"])
        if layer % 2 == 0:
            (abar, bbd, ccd), s5_vjps[j] = jax.vjp(
                _s5_params, small["s5_a_re"][j], small["s5_a_im"][j], small["s5_b_re"][j], small["s5_b_im"][j],
                small["s5_c_re"][j], small["s5_c_im"][j], small["s5_log_dt"][j])
            p.update(w_in_t=big["w_in_t"], w_glu=big["w_glu"],
                     lb=lbs[j].reshape(HG_HEADS, 1, HEAD_DIM), hg_nw=row(small["hg_norm_w"][j]),
                     abar=abar, bbd=bbd.astype(BF16), ccd=ccd.astype(BF16), s5_d=row(small["s5_d"][j]))
        else:
            p.update(w_in=big["w_in"], conv_w=big["conv_w"],
                     a_log=small["gdn_a_log"][j].reshape(GDN_HEADS, 1, 1),
                     dt_bias=small["gdn_dt_bias"][j].reshape(GDN_HEADS, 1, 1), gdn_nw=row(small["gdn_norm_w"][j]))
        return p

    xs = x.reshape(t, D_MODEL)
    saved, layers, ffn_w = [], [], []
    for layer in range(DEPTH):
        big = layer_weights(layer, xs)
        p = layer_params(layer, big)
        xs, r1 = _ffn_fwd(xs, row(small["ffn1_norm"][layer]), big["ffn"], 0)
        xs, r2 = (_even_fwd if layer % 2 == 0 else _odd_fwd)(xs, p, nb, l)
        xs, r3 = _ffn_fwd(xs, row(small["ffn2_norm"][layer]), big["ffn"], 1)
        saved.append((r1, r2, r3))
        layers.append(p)
        ffn_w.append(big["ffn"])
    loss, dx, dfinal = _loss_head(xs, row(small["final_norm"]), target.reshape(t, D_MODEL))

    gs = {k: [None] * DEPTH for k in ("ffn1_norm", "mix_norm", "ffn2_norm")}
    gs.update({k: [None] * 2 for k in ("hg_norm_w", "s5_a_re", "s5_a_im", "s5_b_re", "s5_b_im", "s5_c_re", "s5_c_im",
                                       "s5_d", "s5_log_dt", "gdn_a_log", "gdn_dt_bias", "gdn_norm_w")})
    dlbs = [None] * 2
    token = None
    for layer in reversed(range(DEPTH)):
        j = layer // 2
        p = layers[layer]
        r1, r2, r3 = saved[layer]
        dx, dnw, g_ffn, _ = _ffn_bwd(dx, r3, row(small["ffn2_norm"][layer]), ffn_w[layer], 1, token)
        gs["ffn2_norm"][layer] = dnw[0]
        if layer % 2 == 0:
            dx, gw, sm = _even_bwd(dx, r2, p, nb, l)
            dlbs[j] = sm["lb"]
            gs["hg_norm_w"][j] = sm["hg_nw"]
            gs["s5_d"][j] = sm["s5_d"]
            (gs["s5_a_re"][j], gs["s5_a_im"][j], gs["s5_b_re"][j], gs["s5_b_im"][j], gs["s5_c_re"][j],
             gs["s5_c_im"][j], gs["s5_log_dt"][j]) = s5_vjps[j]((sm["abar"], sm["bbd"], sm["ccd"]))
        else:
            dx, gw, sm = _odd_bwd(dx, r2, p, nb, l)
            gs["gdn_a_log"][j], gs["gdn_dt_bias"][j], gs["gdn_norm_w"][j] = sm["a_log"], sm["dt_bias"], sm["gdn_nw"]
        gs["mix_norm"][layer] = sm["mix_norm"][0]
        token = layer_done(layer, 0, dict(gw, ffn=g_ffn))
        dx, dnw, _, token = _ffn_bwd(dx, r1, row(small["ffn1_norm"][layer]), ffn_w[layer], 0, token,
                                     done=lambda g, layer=layer: layer_done(layer, 1, dict(ffn=g)))
        gs["ffn1_norm"][layer] = dnw[0]
    gsmall = {k: jnp.stack(v) for k, v in gs.items()}
    gsmall["hg_lb_logits"] = lbs_vjp(jnp.stack(dlbs))[0]
    gsmall["final_norm"] = dfinal[0]
    return loss[0, 0], dx.reshape(nb, l, D_MODEL), gsmall


def _here():
    return lax.axis_index("x"), lax.axis_index("y"), lax.axis_index("c")


def _other_chips(x, y):
    return [(1 - x, y), (x, 1 - y), (1 - x, 1 - y)]


def _all_gather(blocks, name):
    n = len(blocks)

    def body(*refs):
        x_refs, o_refs = refs[:n], refs[n:2 * n]
        send_sems, recv_sems, local_sems = refs[2 * n:]
        x, y, cc = _here()
        me, sibling = (x, y, cc), (x, y, 1 - cc)
        chips = _other_chips(x, y)

        def win(i, p):
            return o_refs[i].at[:, 4 * p[0] + 2 * p[1] + p[2]]

        def copy(i, k, blk, to, src=None):
            return pltpu.make_async_remote_copy(
                src_ref=win(i, blk) if src is None else src, dst_ref=win(i, blk),
                send_sem=send_sems.at[i, k], recv_sem=recv_sems.at[i, k], device_id=to, device_id_type=MESH)

        mine = [pltpu.make_async_copy(x_refs[i], win(i, me), local_sems.at[i]) for i in range(n)]
        for cp in mine:
            cp.start()
        first = []
        for i in range(n):
            first.append(copy(i, 0, me, sibling, src=x_refs[i]))
            first += [copy(i, 1 + j, me, (*chip, cc), src=x_refs[i]) for j, chip in enumerate(chips)]
        for cp in first:
            cp.start()
        passed = []
        for j, chip in enumerate(chips):
            for i in range(n):
                copy(i, 1 + j, (*chip, cc), me).wait_recv()
                fwd = copy(i, 4 + j, (*chip, cc), sibling)
                fwd.start()
                passed.append(fwd)
        for i in range(n):
            copy(i, 0, sibling, me).wait_recv()
            for j, chip in enumerate(chips):
                copy(i, 4 + j, (*chip, 1 - cc), me).wait_recv()
        for cp in first + passed:
            cp.wait_send()
        for cp in mine:
            cp.wait()

    hbm = pl.BlockSpec(memory_space=pl.ANY)
    return pl.pallas_call(
        body, name=name, in_specs=[hbm] * n, out_specs=[hbm] * n,
        out_shape=[jax.ShapeDtypeStruct((b.shape[0], N_DEV) + b.shape[1:], b.dtype) for b in blocks],
        scratch_shapes=[pltpu.SemaphoreType.DMA((n, 7)), pltpu.SemaphoreType.DMA((n, 7)), pltpu.SemaphoreType.DMA((n,))],
    )(*blocks)


_HBM = pl.BlockSpec(memory_space=pltpu.HBM)
_SEM = pl.BlockSpec(memory_space=pltpu.SEMAPHORE)
_DATAFLOW = pltpu.SideEffectType.DATAFLOW_SIDE_EFFECTING


def _peer(k):
    x, y, c = _here()
    return (x ^ ((k >> 2) & 1), y ^ ((k >> 1) & 1), c ^ (k & 1))


def _device_of(p):
    return 4 * p[0] + 2 * p[1] + p[2]


_OTHER_CHIPS = (4, 2, 6)


def _exchange_start(srcs, lands, copies, name, after=None):
    n = len(srcs)
    n_in = 2 * n + (after is not None)

    def body(*refs):
        s_refs, l_refs, sems = refs[:n], refs[n:2 * n], refs[n_in:n_in + 2 * n]
        for i in range(n):
            for slot, k in enumerate(_OTHER_CHIPS):
                src, dst = copies(s_refs[i], l_refs[i], k, slot)
                pltpu.make_async_remote_copy(
                    src_ref=src, dst_ref=dst, send_sem=sems[2 * i], recv_sem=sems[2 * i + 1],
                    device_id=_peer(k), device_id_type=MESH).start()
        refs[-1][...] = jnp.zeros_like(refs[-1])

    both = list(srcs) + list(lands)
    outs = pl.pallas_call(
        body, name=name,
        out_shape=tuple([pltpu.SemaphoreType.DMA(())] * (2 * n)) + tuple(pltpu.HBM(a.shape, a.dtype) for a in both)
        + (jax.ShapeDtypeStruct((8, LANES), F32),),
        in_specs=[_HBM] * (2 * n) + [pl.BlockSpec(memory_space=pl.ANY)] * (after is not None),
        out_specs=tuple([_SEM] * (2 * n)) + tuple([_HBM] * (2 * n)) + (pl.BlockSpec(memory_space=pltpu.VMEM),),
        input_output_aliases={i: 2 * n + i for i in range(2 * n)},
        compiler_params=pltpu.CompilerParams(has_side_effects=_DATAFLOW),
    )(*[pltpu.with_memory_space_constraint(a, pltpu.HBM) for a in both], *([after] if after is not None else []))
    return list(outs[:2 * n]), list(outs[2 * n:3 * n]), list(outs[3 * n:4 * n]), outs[4 * n]


def _exchange_wait(handle, three, after, name):
    sems, srcs, lands, _ = handle
    n = len(srcs)

    def body(*refs):
        l_refs, sem_refs = refs[n:2 * n], refs[2 * n:4 * n]
        for i in range(n):
            all_three = three(l_refs[i])
            cp = pltpu.make_async_remote_copy(
                src_ref=all_three, dst_ref=all_three, send_sem=sem_refs[2 * i], recv_sem=sem_refs[2 * i + 1],
                device_id=_peer(_OTHER_CHIPS[0]), device_id_type=MESH)
            cp.wait_send()
            cp.wait_recv()

    both = list(srcs) + list(lands)
    outs = pl.pallas_call(
        body, name=name, out_shape=tuple(pltpu.HBM(a.shape, a.dtype) for a in both),
        in_specs=[_HBM] * (2 * n) + [_SEM] * (2 * n) + [pl.BlockSpec(memory_space=pl.ANY)],
        out_specs=tuple([_HBM] * (2 * n)), input_output_aliases={i: i for i in range(2 * n)},
        compiler_params=pltpu.CompilerParams(has_side_effects=_DATAFLOW),
    )(*both, *sems, after)
    return list(outs[:n]), list(outs[n:])


def _gather_copies(src, land, k, slot):
    del k, slot
    return src, land.at[:, _device_of(_here())]


def _gather_three(land):
    return land.at[:, pl.ds(0, len(_OTHER_CHIPS))]


def _scatter_copies(src, land, k, slot):
    p = _peer(k)
    return src.at[2 * p[0] + p[1]], land.at[slot]


def _scatter_three(land):
    return land


def _core_forward(lands):
    n = len(lands)

    def body(*refs):
        l_refs = refs[:n]
        send_sems, recv_sems = refs[2 * n:]
        x, y, cc = _here()
        mine = [(x, y)] + [_peer(k)[:2] for k in _OTHER_CHIPS]
        cps = []
        for i in range(n):
            for j, (px, py) in enumerate(mine):
                give = l_refs[i].at[:, _device_of((px, py, cc))]
                take = l_refs[i].at[:, _device_of((px, py, 1 - cc))]
                cps.append((pltpu.make_async_remote_copy(
                    src_ref=give, dst_ref=give, send_sem=send_sems.at[i, j], recv_sem=recv_sems.at[i, j],
                    device_id=(x, y, 1 - cc), device_id_type=MESH), take))
        for cp, _ in cps:
            cp.start()
        for i in range(n):
            for j, (cp, take) in enumerate(cps[4 * i:4 * i + 4]):
                cp.wait_send()
                pltpu.make_async_remote_copy(
                    src_ref=take, dst_ref=take, send_sem=send_sems.at[i, j], recv_sem=recv_sems.at[i, j],
                    device_id=(x, y, 1 - cc), device_id_type=MESH).wait_recv()

    hbm = pl.BlockSpec(memory_space=pl.ANY)
    return pl.pallas_call(
        body, name="gather_core_forward", in_specs=[hbm] * n, out_specs=[hbm] * n,
        out_shape=[jax.ShapeDtypeStruct(a.shape, a.dtype) for a in lands],
        input_output_aliases={i: i for i in range(n)},
        scratch_shapes=[pltpu.SemaphoreType.DMA((n, 4)), pltpu.SemaphoreType.DMA((n, 4))],
    )(*lands)


def _rs_core_exchange(gs):
    n = len(gs)

    def body(*refs):
        g_refs, o_refs = refs[:n], refs[n:2 * n]
        send_sems, recv_sems = refs[2 * n:]
        x, y, cc = _here()
        cps = [pltpu.make_async_remote_copy(
            src_ref=g_refs[i].at[:, 2 * j + (1 - cc)], dst_ref=o_refs[i].at[j],
            send_sem=send_sems.at[i, j], recv_sem=recv_sems.at[i, j], device_id=(x, y, 1 - cc), device_id_type=MESH)
            for i in range(n) for j in range(4)]
        for cp in cps:
            cp.start()
        for cp in cps:
            cp.wait()

    hbm = pl.BlockSpec(memory_space=pl.ANY)
    return pl.pallas_call(
        body, name="rs_core_exchange", in_specs=[hbm] * n, out_specs=[hbm] * n,
        out_shape=[jax.ShapeDtypeStruct((4, g.shape[0]) + g.shape[2:], g.dtype) for g in gs],
        scratch_shapes=[pltpu.SemaphoreType.DMA((n, 4)), pltpu.SemaphoreType.DMA((n, 4))],
    )(*gs)


def _row_tile(w):
    return w if w <= 512 else _tile(w, (512, 256, 128))


def _rs_chip_sum(g, r1):
    a, _, w, c = g.shape
    tr = _row_tile(w)

    def body(g_ref, r_ref, o_ref):
        o_ref[...] = (g_ref[...].astype(F32) + r_ref[...].astype(F32)).astype(o_ref.dtype)

    blk = pl.BlockSpec((None, None, tr, c), lambda j, aa, i: (j, aa, i, 0))
    return pl.pallas_call(
        body, name="rs_chip_sum", grid=(4, a, w // tr),
        in_specs=[pl.BlockSpec((None, None, tr, c), lambda j, aa, i: (aa, 2 * j + lax.axis_index("c"), i, 0)), blk],
        out_specs=blk, out_shape=jax.ShapeDtypeStruct((4, a, w, c), g.dtype),
        compiler_params=_params(("parallel", "parallel", "parallel")),
    )(g, r1)


def _rs_final_sum(g, r1, r2):
    a, _, w, c = g.shape
    tr = _row_tile(w)

    def body(g_ref, r1_ref, r2_ref, o_ref):
        s = g_ref[...].astype(F32) + r1_ref[...].astype(F32)
        for k in range(3):
            s = s + r2_ref[k].astype(F32)
        o_ref[...] = s

    def device():
        return 4 * lax.axis_index("x") + 2 * lax.axis_index("y") + lax.axis_index("c")

    def chip():
        return 2 * lax.axis_index("x") + lax.axis_index("y")

    return pl.pallas_call(
        body, name="rs_final_sum", grid=(a, w // tr),
        in_specs=[pl.BlockSpec((None, None, tr, c), lambda aa, i: (aa, device(), i, 0)),
                  pl.BlockSpec((None, None, tr, c), lambda aa, i: (chip(), aa, i, 0)),
                  pl.BlockSpec((3, None, tr, c), lambda aa, i: (0, aa, i, 0))],
        out_specs=pl.BlockSpec((None, tr, c), lambda aa, i: (aa, i, 0)),
        out_shape=jax.ShapeDtypeStruct((a, w, c), F32),
        compiler_params=_params(("parallel", "parallel")),
    )(g, r1, r2)


def _sum_slots(g):
    _, n, r, c = g.shape

    def body(g_ref, o_ref):
        s = g_ref[0, 0]
        for k in range(1, n):
            s = s + g_ref[0, k]
        o_ref[...] = s

    return pl.pallas_call(
        body, name="sum_slots", grid=(1,),
        in_specs=[pl.BlockSpec((1, n, r, c), lambda i: (0, 0, 0, 0))], out_specs=pl.BlockSpec((r, c), lambda i: (0, 0)),
        out_shape=jax.ShapeDtypeStruct((r, c), F32), compiler_params=_params(("arbitrary",)),
    )(g)


_WEIGHTS = ['ffn1_norm', 'ffn1_w_gate', 'ffn1_w_up', 'ffn1_w_down', 'mix_norm', 'ffn2_norm', 'ffn2_w_gate', 'ffn2_w_up',
            'ffn2_w_down', 'ev_w_in', 'hg_lb_logits', 'hg_norm_w', 's5_a_re', 's5_a_im', 's5_b_re', 's5_b_im', 's5_c_re',
            's5_c_im', 's5_d', 's5_log_dt', 's5_w_glu', 'ev_w_out', 'od_w_in', 'gdn_conv_w', 'gdn_a_log', 'gdn_dt_bias',
            'gdn_norm_w', 'od_w_out', 'final_norm']
_SHARDED = ['ffn1_w_gate', 'ffn1_w_up', 'ffn1_w_down', 'ffn2_w_gate', 'ffn2_w_up', 'ffn2_w_down', 'ev_w_in', 's5_w_glu',
            'ev_w_out', 'od_w_in', 'gdn_conv_w', 'od_w_out']
_REPLICATED = [n for n in _WEIGHTS if n not in _SHARDED]
_SMALL_COLS = 1024


def _pad_rows(a, rows):
    return jnp.pad(a, ((0, rows - a.shape[0]), (0, 0)))


def _layer_blocks(w, layer):
    j = layer // 2
    ffn = []
    for f in ('ffn1', 'ffn2'):
        ffn += [_pad_rows(w[f + '_w_gate'][layer].T, FF_SHARD_PAD), _pad_rows(w[f + '_w_up'][layer].T, FF_SHARD_PAD),
                _pad_rows(w[f + '_w_down'][layer], FF_SHARD_PAD)]
    ffn = jnp.stack(ffn).astype(BF16)
    if layer % 2 == 0:
        return [ffn, w['ev_w_in'][j].T.astype(BF16)[None], w['s5_w_glu'][j].astype(BF16)[None],
                w['ev_w_out'][j].astype(BF16)[None]]
    return [ffn, w['od_w_in'][j].astype(BF16)[None], w['gdn_conv_w'][j][None], w['od_w_out'][j].astype(BF16)[None]]


def _layer_whole(gathered, layer):
    whole = lambda g: g.reshape(g.shape[0], N_DEV * g.shape[2], g.shape[3])
    ffn, a1, a2, w_out = gathered
    if layer % 2 == 0:
        return dict(ffn=whole(ffn), w_in_t=whole(a1)[0], w_glu=whole(a2)[0], w_out=whole(w_out)[0])
    w_in = jnp.moveaxis(a1[0], 0, 1).reshape(D_MODEL, OD_IN)
    conv = jnp.moveaxis(a2[0], 0, 1).reshape(CONV_W, GDN_QKV)
    return dict(ffn=whole(ffn), w_in=jnp.pad(w_in, ((0, 0), (0, OD_IN_PAD - OD_IN))), conv_w=conv, w_out=whole(w_out)[0])


def _layer_grad_windows(g, layer):
    windows = lambda a, w: a.reshape(-1, N_DEV, w, a.shape[-1])
    ffn = windows(g['ffn'], FF_SHARD_PAD)
    if 'w_out' not in g:
        return [ffn]
    w_out = windows(g['w_out'], D_MODEL // N_DEV)
    if layer % 2 == 0:
        return [ffn, windows(g['w_in_t'], EV_IN // N_DEV), windows(g['w_glu'], S5_WIDTH // N_DEV), w_out]
    w_in = jnp.moveaxis(g['w_in'].reshape(D_MODEL, N_DEV, OD_IN // N_DEV), 1, 0)[None]
    conv = jnp.moveaxis(g['conv_w'].reshape(CONV_W, N_DEV, GDN_QKV // N_DEV), 1, 0)[None]
    return [ffn, w_in, conv, w_out]


def _flat_small(src, prefix=''):
    flat = jnp.concatenate([src[prefix + n].reshape(-1) for n in _REPLICATED])
    rows = -(-(flat.shape[0] + 1) // (_SMALL_COLS * 8)) * 8
    return jnp.pad(flat, (0, rows * _SMALL_COLS - flat.shape[0])).reshape(rows, _SMALL_COLS)


def _split_small(flat2d, like):
    flat, out, off = flat2d.reshape(-1), {}, 0
    for n in _REPLICATED:
        size = math.prod(like[n].shape)
        out[n] = flat[off:off + size].reshape(like[n].shape)
        off += size
    return out, flat[off]


def _all_reduce_small(gsmall, loss):
    flat = _flat_small(gsmall)
    n_used = sum(math.prod(gsmall[n].shape) for n in _REPLICATED)
    flat = flat.reshape(-1).at[n_used].set(loss).reshape(flat.shape)
    (gathered,) = _all_gather([flat[None]], "gather_small")
    return _split_small(_sum_slots(gathered), gsmall)


def _as_2d(a):
    return a.reshape(-1, a.shape[-1])


def kernel(x, ffn1_norm, ffn1_w_gate, ffn1_w_up, ffn1_w_down, mix_norm, ffn2_norm, ffn2_w_gate, ffn2_w_up, ffn2_w_down, ev_w_in, hg_lb_logits, hg_norm_w, s5_a_re, s5_a_im, s5_b_re, s5_b_im, s5_c_re, s5_c_im, s5_d, s5_log_dt, s5_w_glu, ev_w_out, od_w_in, gdn_conv_w, gdn_a_log, gdn_dt_bias, gdn_norm_w, od_w_out, final_norm, loss_target, m_ffn1_norm, m_ffn1_w_gate, m_ffn1_w_up, m_ffn1_w_down, m_mix_norm, m_ffn2_norm, m_ffn2_w_gate, m_ffn2_w_up, m_ffn2_w_down, m_ev_w_in, m_hg_lb_logits, m_hg_norm_w, m_s5_a_re, m_s5_a_im, m_s5_b_re, m_s5_b_im, m_s5_c_re, m_s5_c_im, m_s5_d, m_s5_log_dt, m_s5_w_glu, m_ev_w_out, m_od_w_in, m_gdn_conv_w, m_gdn_a_log, m_gdn_dt_bias, m_gdn_norm_w, m_od_w_out, m_final_norm, v_ffn1_norm, v_ffn1_w_gate, v_ffn1_w_up, v_ffn1_w_down, v_mix_norm, v_ffn2_norm, v_ffn2_w_gate, v_ffn2_w_up, v_ffn2_w_down, v_ev_w_in, v_hg_lb_logits, v_hg_norm_w, v_s5_a_re, v_s5_a_im, v_s5_b_re, v_s5_b_im, v_s5_c_re, v_s5_c_im, v_s5_d, v_s5_log_dt, v_s5_w_glu, v_ev_w_out, v_od_w_in, v_gdn_conv_w, v_gdn_a_log, v_gdn_dt_bias, v_gdn_norm_w, v_od_w_out, v_final_norm):
    given = dict(locals())
    w = {n: given[n] for n in _WEIGHTS}
    small = {n: w[n] for n in _REPLICATED}
    me = _device_of(_here())

    blocks = [_layer_blocks(w, layer) for layer in range(DEPTH)]
    first = _all_gather(blocks[0], "gather_weights")
    pending = {}

    def start_gather(layer, after=None):
        lands = [lax.dynamic_update_slice(lax.empty((b.shape[0], N_DEV) + b.shape[1:], b.dtype), b[:, None], (0, me, 0, 0))
                 for b in blocks[layer]]
        pending[layer] = _exchange_start(blocks[layer], lands, _gather_copies, "gather_start_%d" % layer, after)

    start_gather(1)

    def layer_weights(layer, after):
        if layer == 0:
            return _layer_whole(first, 0)
        got = _exchange_wait(pending[layer], _gather_three, after, "gather_wait_%d" % layer)[1]
        if layer + 1 < DEPTH:
            start_gather(layer + 1, after)
        return _layer_whole(_core_forward(got), layer)

    sent = {}

    def layer_done(layer, part, grads):
        gs = _layer_grad_windows(grads, layer)
        from_sibling = _rs_core_exchange(gs)
        chip = [_rs_chip_sum(g, r) for g, r in zip(gs, from_sibling)]
        lands = [lax.empty((len(_OTHER_CHIPS),) + p.shape[1:], p.dtype) for p in chip]
        handle = _exchange_start(chip, lands, _scatter_copies, "scatter_start_%d_%d" % (layer, part))
        sent[layer, part] = (gs, from_sibling, handle)
        return handle[3]

    loss_part, grad_x, gsmall = _local_step(given['x'], given['loss_target'], layer_weights, small, layer_done)
    gsum, loss = _all_reduce_small(gsmall, loss_part)
    summed = {}
    for layer in reversed(range(DEPTH)):
        for part in range(2):
            gs, from_sibling, handle = sent[layer, part]
            _, lands = _exchange_wait(handle, _scatter_three, grad_x, "scatter_wait_%d_%d" % (layer, part))
            summed[layer, part] = [_rs_final_sum(g, r, land) for g, r, land in zip(gs, from_sibling, lands)]

    grads = dict(gsum)
    t_ = lambda a: jnp.swapaxes(a, 1, 2)
    for i, f in enumerate(('ffn1', 'ffn2')):
        ffn = jnp.stack([summed[layer, 1 - i][0][:, :FF_SHARD] for layer in range(DEPTH)], axis=1)
        grads[f + '_w_gate'], grads[f + '_w_up'], grads[f + '_w_down'] = t_(ffn[0]), t_(ffn[1]), ffn[2]
    even, odd = [summed[0, 0], summed[2, 0]], [summed[1, 0], summed[3, 0]]
    grads['ev_w_in'] = jnp.stack([r[1][0].T for r in even])
    grads['s5_w_glu'] = jnp.stack([r[2][0] for r in even])
    grads['ev_w_out'] = jnp.stack([r[3][0] for r in even])
    grads['od_w_in'] = jnp.stack([r[1][0] for r in odd])
    grads['gdn_conv_w'] = jnp.stack([r[2][0] for r in odd])
    grads['od_w_out'] = jnp.stack([r[3][0] for r in odd])

    delta, new_m, new_v = {}, {}, {}
    for n in _SHARDED:
        d, nm, nv = _adamw(_as_2d(w[n]), _as_2d(grads[n]), _as_2d(given['m_' + n]), _as_2d(given['v_' + n]))
        delta[n], new_m[n], new_v[n] = (a.reshape(w[n].shape) for a in (d, nm, nv))
    d, nm, nv = _adamw(_flat_small(w), _flat_small(grads), _flat_small(given, 'm_'), _flat_small(given, 'v_'))
    (delta_s, _), (new_m_s, _), (new_v_s, _) = (_split_small(a, small) for a in (d, nm, nv))
    delta.update(delta_s)
    new_m.update(new_m_s)
    new_v.update(new_v_s)
    return (loss, grad_x, *[grads[n] for n in _WEIGHTS], *[delta[n] for n in _WEIGHTS],
            *[new_m[n] for n in _WEIGHTS], *[new_v[n] for n in _WEIGHTS])
```

```python
import math

import jax
import jax.numpy as jnp
from jax import lax
from jax.experimental import pallas as pl
from jax.experimental.pallas import tpu as pltpu

F32 = jnp.float32
BF16 = jnp.bfloat16
HI = lax.Precision.HIGH

D_MODEL = 1024
DEPTH = 4
D_FF = 2816
NORM_EPS = 1e-6
F_MIN = 1e-6
CHUNK = 64
HG_HEADS = 4
HEAD_DIM = 128
HG_QK = HG_HEADS * HEAD_DIM
S5_WIDTH = 512
S5_GROUP = 16
S5_GROUPS = 32
S5_STATE = 64
S5_NSTATE = S5_GROUPS * S5_STATE
EV_IN = 2560
GDN_HEADS = 8
GDN_QKV = 3 * GDN_HEADS * HEAD_DIM
CONV_W = 4
OD_IN = 4112
OD_IN_PAD = 4224
N_DEV = 8
LANES = 128
ADAM_LR, ADAM_B1, ADAM_B2, ADAM_EPS, ADAM_WD, ADAM_STEP = 0.001, 0.9, 0.999, 1e-08, 0.01, 10
VMEM_LIMIT = 56 * 1024 * 1024

MESH = pl.DeviceIdType.MESH


def _params(sem=None, **kw):
    return pltpu.CompilerParams(dimension_semantics=sem, vmem_limit_bytes=VMEM_LIMIT, **kw)


def _tile(n, cands):
    for c in cands:
        if n % c == 0:
            return c
    return n


def _dot(a, b):
    return jnp.dot(a.astype(BF16), b.astype(BF16), preferred_element_type=F32)


def _dot_nt(a, b):
    return lax.dot_general(a.astype(BF16), b.astype(BF16), (((1,), (1,)), ((), ())), preferred_element_type=F32)


def _dot_tn(a, b):
    return lax.dot_general(a.astype(BF16), b.astype(BF16), (((0,), (0,)), ((), ())), preferred_element_type=F32)


def _dot_hi(a, b):
    return jnp.dot(a, b, precision=HI, preferred_element_type=F32)


def _bmm(a, b):
    return jnp.einsum('gmk,gkn->gmn', a.astype(BF16), b.astype(BF16), preferred_element_type=F32)


def _bmm_nt(a, b):
    return jnp.einsum('gmk,gnk->gmn', a.astype(BF16), b.astype(BF16), preferred_element_type=F32)


def _bmm_tn(a, b):
    return jnp.einsum('gkm,gkn->gmn', a.astype(BF16), b.astype(BF16), preferred_element_type=F32)


def _bmm_hi(a, b):
    return jnp.einsum('gmk,gkn->gmn', a, b, precision=HI, preferred_element_type=F32)


_TN_CANDS = (1408, 1280, 1024, 512, 384, 256, 128)
_B_TILE_BYTES = 6 * 1024 * 1024
_TOKEN_TILES = (1024, 512, 256, 128)


def _mm(a, b, *, name, nt=False, out_dtype=F32, res=None, scale=1.0, token_tiles=None):
    m, k = a.shape
    n = b.shape[0] if nt else b.shape[1]
    tm = _tile(m, token_tiles or _TOKEN_TILES)
    tn = _tile(n, [c for c in _TN_CANDS if c * k * b.dtype.itemsize <= _B_TILE_BYTES])

    def body(*refs):
        if res is None:
            a_ref, b_ref, o_ref = refs
        else:
            a_ref, b_ref, r_ref, o_ref = refs
        acc = (_dot_nt if nt else _dot)(a_ref[...], b_ref[...])
        if scale != 1.0:
            acc = scale * acc
        if res is not None:
            acc = r_ref[...] + acc
        o_ref[...] = acc.astype(out_dtype)

    b_spec = pl.BlockSpec((tn, k), lambda j, i: (j, 0)) if nt else pl.BlockSpec((k, tn), lambda j, i: (0, j))
    in_specs = [pl.BlockSpec((tm, k), lambda j, i: (i, 0)), b_spec]
    args = [a, b]
    if res is not None:
        in_specs.append(pl.BlockSpec((tm, tn), lambda j, i: (i, j)))
        args.append(res)
    return pl.pallas_call(
        body, name=name, grid=(n // tn, m // tm), in_specs=in_specs,
        out_specs=pl.BlockSpec((tm, tn), lambda j, i: (i, j)),
        out_shape=jax.ShapeDtypeStruct((m, n), out_dtype),
        compiler_params=_params(("parallel", "parallel")),
    )(*args)


def _mm_tn(a, b, *, name, scale=1.0, out_dtype=F32, into=None, slot=0):
    t, m = a.shape
    n = b.shape[1]
    tm = _tile(m, (1024, 512, 256, 128))
    tn = _tile(n, _TN_CANDS)
    tk = _tile(t, (2048, 1024, 512, 256, 128))
    nk = t // tk

    def body(*refs):
        a_ref, b_ref = refs[:2]
        o_ref, acc_ref = refs[-2:]
        kk = pl.program_id(2)

        @pl.when(kk == 0)
        def _():
            acc_ref[...] = jnp.zeros_like(acc_ref)

        acc_ref[...] += _dot_tn(a_ref[...], b_ref[...])

        @pl.when(kk == nk - 1)
        def _():
            o_ref[...] = (acc_ref[...] * scale if scale != 1.0 else acc_ref[...]).astype(o_ref.dtype)

    in_specs = [pl.BlockSpec((tk, tm), lambda i, j, kk: (kk, i)), pl.BlockSpec((tk, tn), lambda i, j, kk: (kk, j))]
    args = [a, b]
    if into is None:
        out_spec = pl.BlockSpec((tm, tn), lambda i, j, kk: (i, j))
        out_shape = jax.ShapeDtypeStruct((m, n), out_dtype)
        alias = {}
    else:
        in_specs.append(pl.BlockSpec(memory_space=pl.ANY))
        args.append(into)
        out_spec = pl.BlockSpec((None, tm, tn), lambda i, j, kk: (slot, i, j))
        out_shape = jax.ShapeDtypeStruct(into.shape, into.dtype)
        alias = {2: 0}
    return pl.pallas_call(
        body, name=name, grid=(m // tm, n // tn, nk), in_specs=in_specs, out_specs=out_spec, out_shape=out_shape,
        scratch_shapes=[pltpu.VMEM((tm, tn), F32)], input_output_aliases=alias,
        compiler_params=_params(("parallel", "parallel", "arbitrary")),
    )(*args)


def _sigmoid(x):
    return jax.nn.sigmoid(x)


def _head_norm_gate(o, gate, nw):
    r = lax.rsqrt(jnp.mean(o * o, axis=-1, keepdims=True) + NORM_EPS)
    return o * r * nw * (gate * _sigmoid(gate))


def _hg_chunk(st, ql, fl, iv, gl, lb, nw):
    n_g, c, _ = ql.shape
    q = ql * _sigmoid(ql)
    f = lb + (1.0 - lb) * _sigmoid(fl)
    lf = jnp.log(jnp.maximum(f, F_MIN))
    k = 1.0 - f
    ri = lax.broadcasted_iota(jnp.int32, (c, c), 0)
    ci = lax.broadcasted_iota(jnp.int32, (c, c), 1)
    rl = lax.broadcasted_iota(jnp.int32, (c, LANES), 0)
    each = lambda m2: jnp.broadcast_to(m2[None], (n_g,) + m2.shape)
    b = _bmm_hi(each(jnp.where(ci <= ri, 1.0, 0.0)), lf)
    attn = jnp.where((ri == ci)[None], _bmm_nt(q, k), 0.0)
    sh = 0
    while (1 << sh) < c:
        m = 1 << sh
        ref = ((ri >> (sh + 1)) << (sh + 1)) + (m - 1)
        low_r = ((ri >> sh) & 1) == 1
        low_c = ((ci >> sh) & 1) == 1
        w_low = low_r & (ci > ref) & (ci <= ri)
        w_up = jnp.logical_not(low_r) & (ci > ri) & (ci <= ref)
        w = jnp.where(w_low | w_up, 1.0, 0.0)
        e = jnp.exp(_bmm_hi(each(w), lf))
        low_l = (((rl >> sh) & 1) == 1)[None]
        qs = jnp.where(low_l, q * e, 0.0)
        ks = jnp.where(low_l, 0.0, k * e)
        pair = ((ri >> (sh + 1)) == (ci >> (sh + 1))) & low_r & jnp.logical_not(low_c)
        attn = attn + jnp.where(pair[None], _bmm_nt(qs, ks), 0.0)
        sh += 1
    bl = jnp.sum(lf, axis=1, keepdims=True)
    o = _bmm(attn, iv) + _bmm_nt(q * jnp.exp(b), st)
    st_new = st * jnp.exp(bl) + _bmm_tn(iv, k * jnp.exp(bl - b))
    return _head_norm_gate(o, gl, nw), st_new


def _hg_specs(nb, n_c, rev):
    def cidx(cc):
        return (n_c - 1 - cc) if rev else cc

    def col(width):
        return pl.BlockSpec((nb, CHUNK, width), lambda cc: (0, cidx(cc), 0))

    vec = pl.BlockSpec((HG_HEADS, 1, LANES), lambda cc: (0, 0, 0))
    nw = pl.BlockSpec((1, LANES), lambda cc: (0, 0))
    st = pl.BlockSpec((nb, HG_HEADS, None, LANES, LANES), lambda cc: (0, 0, cidx(cc), 0, 0))
    acc = pl.BlockSpec((nb, HG_HEADS, 1, LANES), lambda cc: (0, 0, 0, 0))
    return col, vec, nw, st, acc


def _head(j):
    return slice(j * LANES, (j + 1) * LANES)


def _seq_heads(ref, nb, n, first=0):
    return jnp.stack([ref[b, :, _head(first + j)] for b in range(nb) for j in range(n)])


def _put_seq_heads(ref, val, nb, n, first=0):
    for b in range(nb):
        for j in range(n):
            ref[b, :, _head(first + j)] = val[b * n + j].astype(ref.dtype)


def _hgrn2_fwd(proj, lb, nw):
    nb, l, _ = proj.shape
    n_c = l // CHUNK
    n_g = nb * HG_HEADS
    col, vec, nws, st, _ = _hg_specs(nb, n_c, False)

    def body(p_ref, lb_ref, nw_ref, y_ref, st_ref, s_scr):
        @pl.when(pl.program_id(0) == 0)
        def _():
            s_scr[...] = jnp.zeros_like(s_scr)

        s_in = s_scr[...]
        st_ref[...] = s_in.reshape(st_ref.shape)
        args = [_seq_heads(p_ref, nb, HG_HEADS, k * HG_HEADS) for k in range(4)]
        y, s_new = _hg_chunk(s_in, *args, jnp.concatenate([lb_ref[...]] * nb), nw_ref[...])
        _put_seq_heads(y_ref, y, nb, HG_HEADS)
        s_scr[...] = s_new

    return pl.pallas_call(
        body, name="hgrn2_fwd", grid=(n_c,),
        in_specs=[col(4 * HG_QK), vec, nws], out_specs=[col(HG_QK), st],
        out_shape=[jax.ShapeDtypeStruct((nb, l, HG_QK), BF16),
                   jax.ShapeDtypeStruct((nb, HG_HEADS, n_c, LANES, LANES), F32)],
        scratch_shapes=[pltpu.VMEM((n_g, LANES, LANES), F32)],
        compiler_params=_params(("arbitrary",)),
    )(proj, lb, nw)


def _hgrn2_bwd(proj, lb, nw, states, dy):
    nb, l, _ = proj.shape
    n_c = l // CHUNK
    n_g = nb * HG_HEADS
    col, vec, nws, st, acc = _hg_specs(nb, n_c, True)

    def body(p_ref, lb_ref, nw_ref, st_ref, dy_ref, dp_ref, dlb_ref, dnw_ref, ds_scr):
        @pl.when(pl.program_id(0) == 0)
        def _():
            ds_scr[...] = jnp.zeros_like(ds_scr)
            dlb_ref[...] = jnp.zeros_like(dlb_ref)
            dnw_ref[...] = jnp.zeros_like(dnw_ref)

        args = [_seq_heads(p_ref, nb, HG_HEADS, k * HG_HEADS) for k in range(4)]
        _, vjp = jax.vjp(_hg_chunk, st_ref[...].reshape(n_g, LANES, LANES), *args,
                         jnp.concatenate([lb_ref[...]] * nb), nw_ref[...])
        ds, dql, dfl, div, dgl, dlb, dnw = vjp((_seq_heads(dy_ref, nb, HG_HEADS).astype(F32), ds_scr[...]))
        ds_scr[...] = ds
        for k, val in enumerate((dql, dfl, div, dgl)):
            _put_seq_heads(dp_ref, val, nb, HG_HEADS, k * HG_HEADS)
        dlb_ref[...] += dlb.reshape(dlb_ref.shape)
        dnw_ref[...] += dnw

    return pl.pallas_call(
        body, name="hgrn2_bwd", grid=(n_c,),
        in_specs=[col(4 * HG_QK), vec, nws, st, col(HG_QK)], out_specs=[col(4 * HG_QK), acc, nws],
        out_shape=[jax.ShapeDtypeStruct((nb, l, 4 * HG_QK), BF16),
                   jax.ShapeDtypeStruct((nb, HG_HEADS, 1, LANES), F32), jax.ShapeDtypeStruct((1, LANES), F32)],
        scratch_shapes=[pltpu.VMEM((n_g, LANES, LANES), F32)],
        compiler_params=_params(("arbitrary",)),
    )(proj, lb, nw, states, dy)


def _shift_down(x, s):
    if s == 0:
        return x
    rows = lax.broadcasted_iota(jnp.int32, x.shape, 0)
    return jnp.where(rows >= s, pltpu.roll(x, s, 0), 0.0)


def _shift_up(x, s):
    if s == 0:
        return x
    n = x.shape[0]
    rows = lax.broadcasted_iota(jnp.int32, x.shape, 0)
    return jnp.where(rows < n - s, pltpu.roll(x, n - s, 0), 0.0)


def _gdn_pre_fwd(proj, conv_w):
    nb, l, _ = proj.shape
    n_blk = GDN_QKV // LANES

    def body(x_ref, w_ref, o_ref):
        j = pl.program_id(1)
        x = x_ref[...]
        w = w_ref[...]
        c = w[3:4] * x
        for t in range(CONV_W - 1):
            c = c + w[t:t + 1] * _shift_down(x, CONV_W - 1 - t)
        s = c * _sigmoid(c)
        r = lax.rsqrt(jnp.sum(s * s, axis=-1, keepdims=True) + NORM_EPS)
        scale = jnp.where(j < GDN_HEADS, HEAD_DIM ** -0.5, 1.0)
        o_ref[...] = jnp.where(j < 2 * GDN_HEADS, s * r * scale, s)

    return pl.pallas_call(
        body, name="gdn_pre_fwd", grid=(nb, n_blk),
        in_specs=[pl.BlockSpec((None, l, LANES), lambda b, j: (b, 0, j)), pl.BlockSpec((CONV_W, LANES), lambda b, j: (0, j))],
        out_specs=pl.BlockSpec((None, l, LANES), lambda b, j: (b, 0, j)),
        out_shape=jax.ShapeDtypeStruct((nb, l, GDN_QKV), F32),
        compiler_params=_params(("parallel", "parallel")),
    )(proj, conv_w)


def _gdn_pre_bwd(proj, conv_w, dout):
    nb, l, _ = proj.shape
    n_blk = GDN_QKV // LANES

    def body(x_ref, w_ref, d_ref, dx_ref, dw_ref):
        j = pl.program_id(1)
        x = x_ref[...]
        w = w_ref[...]
        xs = [_shift_down(x, CONV_W - 1 - t) for t in range(CONV_W)]
        c = w[0:1] * xs[0]
        for t in range(1, CONV_W):
            c = c + w[t:t + 1] * xs[t]
        sg = _sigmoid(c)
        s = c * sg
        d = d_ref[...]
        r = lax.rsqrt(jnp.sum(s * s, axis=-1, keepdims=True) + NORM_EPS)
        scale = jnp.where(j < GDN_HEADS, HEAD_DIM ** -0.5, 1.0)
        ds_norm = scale * r * (d - s * (r * r) * jnp.sum(d * s, axis=-1, keepdims=True))
        ds = jnp.where(j < 2 * GDN_HEADS, ds_norm, d)
        dc = ds * (sg * (1.0 + c * (1.0 - sg)))
        dx = w[3:4] * dc
        for t in range(CONV_W - 1):
            dx = dx + w[t:t + 1] * _shift_up(dc, CONV_W - 1 - t)
        dx_ref[...] = dx.astype(dx_ref.dtype)
        for t in range(CONV_W):
            dw_ref[t:t + 1, :] = jnp.sum(dc * xs[t], axis=0, keepdims=True)

    return pl.pallas_call(
        body, name="gdn_pre_bwd", grid=(nb, n_blk),
        in_specs=[pl.BlockSpec((None, l, LANES), lambda b, j: (b, 0, j)), pl.BlockSpec((CONV_W, LANES), lambda b, j: (0, j)),
                  pl.BlockSpec((None, l, LANES), lambda b, j: (b, 0, j))],
        out_specs=[pl.BlockSpec((None, l, LANES), lambda b, j: (b, 0, j)), pl.BlockSpec((None, CONV_W, LANES), lambda b, j: (b, 0, j))],
        out_shape=[jax.ShapeDtypeStruct((nb, l, GDN_QKV), BF16), jax.ShapeDtypeStruct((nb, CONV_W, GDN_QKV), F32)],
        compiler_params=_params(("parallel", "parallel")),
    )(proj, conv_w, dout)


def _gdn_chunk(st, q, k, v, gate, bl, al, alog, dtb, nw):
    n_g, c, _ = q.shape
    beta = _sigmoid(bl)
    x = al + dtb
    softplus = jnp.maximum(x, 0.0) + jnp.log(1.0 + jnp.exp(-jnp.abs(x)))
    la = -jnp.exp(alog) * softplus
    ri = lax.broadcasted_iota(jnp.int32, (c, c), 0)
    ci = lax.broadcasted_iota(jnp.int32, (c, c), 1)
    lower = (ci <= ri)[None]
    strict = (ci < ri)[None]
    ltri = jnp.broadcast_to(jnp.where(lower, 1.0, 0.0), (n_g, c, c))
    la_l = la + jnp.zeros((1, 1, LANES), F32)
    g = _bmm_hi(ltri, la_l)
    delta = _bmm_hi(ltri, jnp.where(strict, la + jnp.zeros((1, 1, c), F32), 0.0))
    gam = jnp.where(lower, jnp.exp(jnp.where(lower, delta, 0.0)), 0.0)
    kb = k * beta
    vb = v * beta
    m = jnp.where(strict, _bmm_nt(kb, k) * gam, 0.0)
    pw = -m
    t_inv = jnp.where((ri == ci)[None], 1.0, 0.0) + pw
    steps = int(math.log2(c)) - 1
    for _ in range(steps):
        pw = _bmm_hi(pw, pw)
        t_inv = t_inv + _bmm_hi(t_inv, pw)
    eg = jnp.exp(g)
    u = _bmm(t_inv, vb)
    w = _bmm(t_inv, kb * eg)
    attn = jnp.where(lower, _bmm_nt(q, k) * gam, 0.0)
    v_new = u - _bmm_nt(w, st)
    o = _bmm_nt(q * eg, st) + _bmm(attn, v_new)
    g_last = jnp.sum(la_l, axis=1, keepdims=True)
    st_new = st * jnp.exp(g_last) + _bmm_tn(v_new, k * jnp.exp(g_last - g))
    return _head_norm_gate(o, gate, nw), st_new


GDN_WIDE = GDN_HEADS * LANES
GDN_SMALL_BLOCK = (GDN_QKV + GDN_WIDE) // LANES


def _gdn_specs(nb, n_c, rev):
    def cidx(cc):
        return (n_c - 1 - cc) if rev else cc

    def col(width, blk=0):
        return pl.BlockSpec((nb, CHUNK, width), lambda cc: (0, cidx(cc), blk))

    scal = pl.BlockSpec((GDN_HEADS, 1, 1), lambda cc: (0, 0, 0))
    nw = pl.BlockSpec((1, LANES), lambda cc: (0, 0))
    st = pl.BlockSpec((nb, GDN_HEADS, None, LANES, LANES), lambda cc: (0, 0, cidx(cc), 0, 0))
    acc_s = pl.BlockSpec((nb, GDN_HEADS, 1, 1), lambda cc: (0, 0, 0, 0))
    return col, scal, nw, st, acc_s


def _tok_cols(xs, nb, first):
    return jnp.stack([xs[b][:, first + h:first + h + 1] for b in range(nb) for h in range(GDN_HEADS)])


def _gdn_inputs(qkv_ref, p_gate_ref, p_small_ref, alog_ref, dtb_ref, nw_ref, nb):
    q, k, v = (_seq_heads(qkv_ref, nb, GDN_HEADS, i * GDN_HEADS) for i in range(3))
    xs = p_small_ref[...]
    return (q, k, v, _seq_heads(p_gate_ref, nb, GDN_HEADS), _tok_cols(xs, nb, 0), _tok_cols(xs, nb, GDN_HEADS),
            jnp.concatenate([alog_ref[...]] * nb), jnp.concatenate([dtb_ref[...]] * nb), nw_ref[...])


def _gdn_fwd(qkv, proj, alog, dtb, nw):
    nb, l, _ = qkv.shape
    n_c = l // CHUNK
    n_g = nb * GDN_HEADS
    col, scal, nws, st, _ = _gdn_specs(nb, n_c, False)

    def body(qkv_ref, pg_ref, ps_ref, alog_ref, dtb_ref, nw_ref, y_ref, st_ref, s_scr):
        @pl.when(pl.program_id(0) == 0)
        def _():
            s_scr[...] = jnp.zeros_like(s_scr)

        s_in = s_scr[...]
        st_ref[...] = s_in.reshape(st_ref.shape)
        y, s_new = _gdn_chunk(s_in, *_gdn_inputs(qkv_ref, pg_ref, ps_ref, alog_ref, dtb_ref, nw_ref, nb))
        _put_seq_heads(y_ref, y, nb, GDN_HEADS)
        s_scr[...] = s_new

    return pl.pallas_call(
        body, name="gdn_fwd", grid=(n_c,),
        in_specs=[col(GDN_QKV), col(GDN_WIDE, GDN_QKV // GDN_WIDE), col(LANES, GDN_SMALL_BLOCK), scal, scal, nws],
        out_specs=[col(GDN_WIDE), st],
        out_shape=[jax.ShapeDtypeStruct((nb, l, GDN_WIDE), BF16),
                   jax.ShapeDtypeStruct((nb, GDN_HEADS, n_c, LANES, LANES), F32)],
        scratch_shapes=[pltpu.VMEM((n_g, LANES, LANES), F32)],
        compiler_params=_params(("arbitrary",)),
    )(qkv, proj, proj, alog, dtb, nw)


def _gdn_bwd(qkv, proj, alog, dtb, nw, states, dy):
    nb, l, _ = qkv.shape
    n_c = l // CHUNK
    n_g = nb * GDN_HEADS
    col, scal, nws, st, acc_s = _gdn_specs(nb, n_c, True)

    def body(qkv_ref, pg_ref, ps_ref, alog_ref, dtb_ref, nw_ref, st_ref, dy_ref,
             dqkv_ref, dg_ref, dsm_ref, dalog_ref, ddtb_ref, dnw_ref, ds_scr):
        @pl.when(pl.program_id(0) == 0)
        def _():
            ds_scr[...] = jnp.zeros_like(ds_scr)
            dalog_ref[...] = jnp.zeros_like(dalog_ref)
            ddtb_ref[...] = jnp.zeros_like(ddtb_ref)
            dnw_ref[...] = jnp.zeros_like(dnw_ref)

        _, vjp = jax.vjp(_gdn_chunk, st_ref[...].reshape(n_g, LANES, LANES),
                         *_gdn_inputs(qkv_ref, pg_ref, ps_ref, alog_ref, dtb_ref, nw_ref, nb))
        ds, dq, dk, dv, dg, dbl, dal, dalog, ddtb, dnw = vjp((_seq_heads(dy_ref, nb, GDN_HEADS).astype(F32), ds_scr[...]))
        ds_scr[...] = ds
        for i, val in enumerate((dq, dk, dv)):
            _put_seq_heads(dqkv_ref, val, nb, GDN_HEADS, i * GDN_HEADS)
        _put_seq_heads(dg_ref, dg, nb, GDN_HEADS)
        lane = lax.broadcasted_iota(jnp.int32, (CHUNK, LANES), 1)
        for b in range(nb):
            small = jnp.zeros((CHUNK, LANES), F32)
            for h in range(GDN_HEADS):
                small = small + jnp.where(lane == h, dbl[b * GDN_HEADS + h], 0.0)
                small = small + jnp.where(lane == GDN_HEADS + h, dal[b * GDN_HEADS + h], 0.0)
            dsm_ref[b] = small.astype(dsm_ref.dtype)
        dalog_ref[...] += dalog.reshape(dalog_ref.shape)
        ddtb_ref[...] += ddtb.reshape(ddtb_ref.shape)
        dnw_ref[...] += dnw

    h = GDN_HEADS
    return pl.pallas_call(
        body, name="gdn_bwd", grid=(n_c,),
        in_specs=[col(GDN_QKV), col(GDN_WIDE, GDN_QKV // GDN_WIDE), col(LANES, GDN_SMALL_BLOCK), scal, scal, nws, st,
                  col(GDN_WIDE)],
        out_specs=[col(GDN_QKV), col(GDN_WIDE), col(LANES), acc_s, acc_s, nws],
        out_shape=[jax.ShapeDtypeStruct((nb, l, GDN_QKV), F32), jax.ShapeDtypeStruct((nb, l, GDN_WIDE), BF16),
                   jax.ShapeDtypeStruct((nb, l, LANES), BF16),
                   jax.ShapeDtypeStruct((nb, h, 1, 1), F32), jax.ShapeDtypeStruct((nb, h, 1, 1), F32),
                   jax.ShapeDtypeStruct((1, LANES), F32)],
        scratch_shapes=[pltpu.VMEM((n_g, LANES, LANES), F32)],
        compiler_params=_params(("arbitrary",)),
    )(qkv, proj, proj, alog, dtb, nw, states, dy)


S5_ROWS = S5_NSTATE // LANES
S5_TB = 256


def _s5_scan_fwd(bu, abar):
    nb, l = bu.shape[:2]
    tb = min(S5_TB, l)

    def body(bu_ref, a_ref, h_ref, c_scr):
        @pl.when(pl.program_id(1) == 0)
        def _():
            c_scr[...] = jnp.zeros_like(c_scr)

        ar = a_ref[0]
        ai = a_ref[1]

        def step(t, carry):
            hr, hi = carry
            nr = ar * hr - ai * hi + bu_ref[t, 0].astype(F32)
            ni = ar * hi + ai * hr + bu_ref[t, 1].astype(F32)
            h_ref[t, 0] = nr.astype(h_ref.dtype)
            h_ref[t, 1] = ni.astype(h_ref.dtype)
            return nr, ni

        hr, hi = lax.fori_loop(0, tb, step, (c_scr[0], c_scr[1]), unroll=8)
        c_scr[0] = hr
        c_scr[1] = hi

    blk = pl.BlockSpec((None, tb, 2, S5_ROWS, LANES), lambda b, i: (b, i, 0, 0, 0))
    return pl.pallas_call(
        body, name="s5_scan_fwd", grid=(nb, l // tb),
        in_specs=[blk, pl.BlockSpec((2, S5_ROWS, LANES), lambda b, i: (0, 0, 0))],
        out_specs=blk, out_shape=jax.ShapeDtypeStruct(bu.shape, bu.dtype),
        scratch_shapes=[pltpu.VMEM((2, S5_ROWS, LANES), F32)],
        compiler_params=_params(("parallel", "arbitrary")),
    )(bu, abar)


def _s5_scan_bwd(dh, h, abar):
    nb, l = dh.shape[:2]
    tb = min(S5_TB, l)
    n_blk = l // tb

    def body(dh_ref, h_ref, a_ref, g_ref, da_ref, c_scr):
        @pl.when(pl.program_id(1) == 0)
        def _():
            c_scr[...] = jnp.zeros_like(c_scr)
            da_ref[...] = jnp.zeros_like(da_ref)

        ar = a_ref[0]
        ai = a_ref[1]

        def step(s, carry):
            gr, gi, dar, dai = carry
            t = tb - 1 - s
            hr = h_ref[t, 0].astype(F32)
            hi = h_ref[t, 1].astype(F32)
            dar = dar + gr * hr + gi * hi
            dai = dai + gi * hr - gr * hi
            nr = ar * gr + ai * gi + dh_ref[t, 0].astype(F32)
            ni = ar * gi - ai * gr + dh_ref[t, 1].astype(F32)
            g_ref[t, 0] = nr.astype(g_ref.dtype)
            g_ref[t, 1] = ni.astype(g_ref.dtype)
            return nr, ni, dar, dai

        z = jnp.zeros((S5_ROWS, LANES), F32)
        gr, gi, dar, dai = lax.fori_loop(0, tb, step, (c_scr[0], c_scr[1], z, z), unroll=8)
        c_scr[0] = gr
        c_scr[1] = gi
        da_ref[0] += dar
        da_ref[1] += dai

    blk = pl.BlockSpec((None, tb, 2, S5_ROWS, LANES), lambda b, i: (b, n_blk - 1 - i, 0, 0, 0))
    return pl.pallas_call(
        body, name="s5_scan_bwd", grid=(nb, n_blk),
        in_specs=[blk, blk, pl.BlockSpec((2, S5_ROWS, LANES), lambda b, i: (0, 0, 0))],
        out_specs=[blk, pl.BlockSpec((None, 2, S5_ROWS, LANES), lambda b, i: (b, 0, 0, 0))],
        out_shape=[jax.ShapeDtypeStruct(dh.shape, dh.dtype), jax.ShapeDtypeStruct((nb, 2, S5_ROWS, LANES), F32)],
        scratch_shapes=[pltpu.VMEM((2, S5_ROWS, LANES), F32)],
        compiler_params=_params(("parallel", "arbitrary")),
    )(dh, h, abar)


_GELU_C = math.sqrt(2.0 / math.pi)


def _gelu(x):
    return 0.5 * x * (1.0 + jnp.tanh(_GELU_C * (x + 0.044715 * x * x * x)))


def _gelu_grad(x):
    t = jnp.tanh(_GELU_C * (x + 0.044715 * x * x * x))
    return 0.5 * (1.0 + t) + 0.5 * x * (1.0 - t * t) * _GELU_C * (1.0 + 3.0 * 0.044715 * x * x)


def _s5_post_fwd(yc, u, d, w_glu):
    t, w = yc.shape
    tt = _tile(t, _TOKEN_TILES)

    def body(yc_ref, u_ref, d_ref, w_ref, o_ref, y0_ref):
        y0 = yc_ref[...] + d_ref[...] * u_ref[...]
        y = _gelu(y0)
        z = _dot(y, w_ref[...])
        o_ref[...] = (y * _sigmoid(z)).astype(o_ref.dtype)
        y0_ref[...] = y0

    tok = pl.BlockSpec((tt, w), lambda i: (i, 0))
    return pl.pallas_call(
        body, name="s5_post_fwd", grid=(t // tt,),
        in_specs=[tok, tok, pl.BlockSpec((1, w), lambda i: (0, 0)), pl.BlockSpec((w, w), lambda i: (0, 0))],
        out_specs=[tok, tok],
        out_shape=[jax.ShapeDtypeStruct((t, w), BF16), jax.ShapeDtypeStruct((t, w), F32)],
        compiler_params=_params(("parallel",)),
    )(yc, u, d, w_glu)


def _s5_post_bwd(y0, u, d, w_glu, dout):
    t, w = y0.shape
    tt = _tile(t, _TOKEN_TILES)

    def body(y0_ref, u_ref, d_ref, w_ref, do_ref, dy0_ref, y_ref, dz_ref, du_ref, dd_ref):
        @pl.when(pl.program_id(0) == 0)
        def _():
            dd_ref[...] = jnp.zeros_like(dd_ref)

        y0 = y0_ref[...]
        y = _gelu(y0)
        s = _sigmoid(_dot(y, w_ref[...]))
        do = do_ref[...]
        dz = do * y * s * (1.0 - s)
        dy = do * s + _dot_nt(dz, w_ref[...])
        dy0 = dy * _gelu_grad(y0)
        dy0_ref[...] = dy0.astype(dy0_ref.dtype)
        y_ref[...] = y.astype(y_ref.dtype)
        dz_ref[...] = dz.astype(dz_ref.dtype)
        du_ref[...] = dy0 * d_ref[...]
        dd_ref[...] += jnp.sum(dy0 * u_ref[...], axis=0, keepdims=True)

    tok = pl.BlockSpec((tt, w), lambda i: (i, 0))
    vec = pl.BlockSpec((1, w), lambda i: (0, 0))
    return pl.pallas_call(
        body, name="s5_post_bwd", grid=(t // tt,),
        in_specs=[tok, tok, vec, pl.BlockSpec((w, w), lambda i: (0, 0)), tok],
        out_specs=[tok, tok, tok, tok, vec],
        out_shape=[jax.ShapeDtypeStruct((t, w), BF16), jax.ShapeDtypeStruct((t, w), BF16),
                   jax.ShapeDtypeStruct((t, w), BF16), jax.ShapeDtypeStruct((t, w), F32),
                   jax.ShapeDtypeStruct((1, w), F32)],
        compiler_params=_params(("arbitrary",)),
    )(y0, u, d, w_glu, dout)


def _s5_params(a_re, a_im, b_re, b_im, c_re, c_im, log_dt):
    dt = jnp.exp(log_dt)[:, None]
    mag = jnp.exp(dt * a_re)
    ang = dt * a_im
    abar_re = mag * jnp.cos(ang)
    abar_im = mag * jnp.sin(ang)
    den = a_re * a_re + a_im * a_im
    zr = abar_re - 1.0
    zi = abar_im
    coef_re = ((zr * a_re + zi * a_im) / den)[..., None]
    coef_im = ((zi * a_re - zr * a_im) / den)[..., None]
    bb_re = coef_re * b_re - coef_im * b_im
    bb_im = coef_re * b_im + coef_im * b_re
    eye = jnp.eye(S5_GROUPS, dtype=F32)

    def dense_in(bb):
        return jnp.einsum('gnp,gh->gphn', bb, eye).reshape(S5_WIDTH, S5_NSTATE)

    def dense_out(cc):
        return jnp.einsum('gpn,gh->gnhp', cc, eye).reshape(S5_NSTATE, S5_WIDTH)

    abar = jnp.stack([abar_re.reshape(S5_ROWS, LANES), abar_im.reshape(S5_ROWS, LANES)])
    bbd = jnp.concatenate([dense_in(bb_re), dense_in(bb_im)], axis=1)
    ccd = jnp.concatenate([dense_out(c_re), dense_out(-c_im)], axis=0)
    return abar, bbd, ccd


def _tok_tile(t):
    return _tile(t, _TOKEN_TILES[1:])


def _rms_fwd(x, w):
    t, d = x.shape
    tt = _tok_tile(t)

    def body(x_ref, w_ref, o_ref):
        xv = x_ref[...]
        r = lax.rsqrt(jnp.mean(xv * xv, axis=-1, keepdims=True) + NORM_EPS)
        o_ref[...] = (xv * r * w_ref[...]).astype(o_ref.dtype)

    tok = pl.BlockSpec((tt, d), lambda i: (i, 0))
    return pl.pallas_call(
        body, name="rms_fwd", grid=(t // tt,), in_specs=[tok, pl.BlockSpec((1, d), lambda i: (0, 0))],
        out_specs=tok, out_shape=jax.ShapeDtypeStruct((t, d), BF16), compiler_params=_params(("parallel",)),
    )(x, w)


def _rms_bwd_math(xv, wv, dy):
    r = lax.rsqrt(jnp.mean(xv * xv, axis=-1, keepdims=True) + NORM_EPS)
    xh = xv * r
    dxh = dy * wv
    dx = r * (dxh - xh * jnp.mean(dxh * xh, axis=-1, keepdims=True))
    return dx, jnp.sum(dy * xh, axis=0, keepdims=True)


def _rms_bwd(x, w, dy, dres):
    t, d = x.shape
    tt = _tok_tile(t)

    def body(x_ref, w_ref, dy_ref, dr_ref, dx_ref, dxb_ref, dw_ref):
        @pl.when(pl.program_id(0) == 0)
        def _():
            dw_ref[...] = jnp.zeros_like(dw_ref)

        dx, dw = _rms_bwd_math(x_ref[...], w_ref[...], dy_ref[...])
        dx = dr_ref[...] + dx
        dx_ref[...] = dx
        dxb_ref[...] = dx.astype(dxb_ref.dtype)
        dw_ref[...] += dw

    tok = pl.BlockSpec((tt, d), lambda i: (i, 0))
    vec = pl.BlockSpec((1, d), lambda i: (0, 0))
    return pl.pallas_call(
        body, name="rms_bwd", grid=(t // tt,), in_specs=[tok, vec, tok, tok], out_specs=[tok, tok, vec],
        out_shape=[jax.ShapeDtypeStruct((t, d), F32), jax.ShapeDtypeStruct((t, d), BF16), jax.ShapeDtypeStruct((1, d), F32)],
        compiler_params=_params(("arbitrary",)),
    )(x, w, dy, dres)


def _loss_head(x, w, target):
    t, d = x.shape
    tt = _tok_tile(t)

    def body(x_ref, w_ref, t_ref, l_ref, dx_ref, dxb_ref, dw_ref):
        @pl.when(pl.program_id(0) == 0)
        def _():
            l_ref[...] = jnp.zeros_like(l_ref)
            dw_ref[...] = jnp.zeros_like(dw_ref)

        xv = x_ref[...]
        wv = w_ref[...]
        r = lax.rsqrt(jnp.mean(xv * xv, axis=-1, keepdims=True) + NORM_EPS)
        err = xv * r * wv - t_ref[...]
        row = jnp.sum(err * err, axis=-1, keepdims=True)
        l_ref[...] += (0.5 / d) * jnp.sum(row, axis=0, keepdims=True)
        dx, dw = _rms_bwd_math(xv, wv, err * (1.0 / d))
        dx_ref[...] = dx
        dxb_ref[...] = dx.astype(dxb_ref.dtype)
        dw_ref[...] += dw

    tok = pl.BlockSpec((tt, d), lambda i: (i, 0))
    vec = pl.BlockSpec((1, d), lambda i: (0, 0))
    return pl.pallas_call(
        body, name="loss_head", grid=(t // tt,), in_specs=[tok, vec, tok],
        out_specs=[pl.BlockSpec((1, 1), lambda i: (0, 0)), tok, tok, vec],
        out_shape=[jax.ShapeDtypeStruct((1, 1), F32), jax.ShapeDtypeStruct((t, d), F32),
                   jax.ShapeDtypeStruct((t, d), BF16), jax.ShapeDtypeStruct((1, d), F32)],
        compiler_params=_params(("arbitrary",)),
    )(x, w, target)


FF_PAD = 3072
FF_SHARD = D_FF // N_DEV
FF_SHARD_PAD = FF_PAD // N_DEV


def _ffn_in(x, nw, wg_t, wu_t):
    t, d = x.shape
    tm = _tile(t, _TOKEN_TILES)
    tn = 1024

    def body(x_ref, nw_ref, wg_ref, wu_ref, h_ref, dadg_ref, dadu_ref, a_ref):
        @pl.when(pl.program_id(1) == 0)
        def _():
            xv = x_ref[...]
            r = lax.rsqrt(jnp.mean(xv * xv, axis=-1, keepdims=True) + NORM_EPS)
            h_ref[...] = (xv * r * nw_ref[...]).astype(h_ref.dtype)

        h = h_ref[...]
        g = _dot_nt(h, wg_ref[...])
        u = _dot_nt(h, wu_ref[...])
        s = _sigmoid(g)
        silu = g * s
        dadg_ref[...] = (u * (s + silu * (1.0 - s))).astype(dadg_ref.dtype)
        dadu_ref[...] = silu.astype(dadu_ref.dtype)
        a_ref[...] = (silu * u).astype(a_ref.dtype)

    tok = pl.BlockSpec((tm, d), lambda i, j: (i, 0))
    wsp = pl.BlockSpec((tn, d), lambda i, j: (j, 0))
    wide = pl.BlockSpec((tm, tn), lambda i, j: (i, j))
    return pl.pallas_call(
        body, name="ffn_in", grid=(t // tm, FF_PAD // tn),
        in_specs=[tok, pl.BlockSpec((1, d), lambda i, j: (0, 0)), wsp, wsp], out_specs=[tok, wide, wide, wide],
        out_shape=[jax.ShapeDtypeStruct((t, d), BF16)] + [jax.ShapeDtypeStruct((t, FF_PAD), BF16)] * 3,
        compiler_params=_params(("parallel", "arbitrary")),
    )(x, nw, wg_t, wu_t)


def _ffn_out_dx(dxo, wd, dadg, dadu, after=None):
    t, d = dxo.shape
    tm = _tile(t, _TOKEN_TILES)
    tn = 1024

    def body(dx_ref, wd_ref, dadg_ref, dadu_ref, *rest):
        dg_ref, du_ref = rest[-2:]
        da = 0.5 * _dot_nt(dx_ref[...], wd_ref[...])
        dg_ref[...] = (da * dadg_ref[...].astype(F32)).astype(dg_ref.dtype)
        du_ref[...] = (da * dadu_ref[...].astype(F32)).astype(du_ref.dtype)

    wide = pl.BlockSpec((tm, tn), lambda j, i: (i, j))
    return pl.pallas_call(
        body, name="ffn_out_dx", grid=(FF_PAD // tn, t // tm),
        in_specs=[pl.BlockSpec((tm, d), lambda j, i: (i, 0)), pl.BlockSpec((tn, d), lambda j, i: (j, 0)), wide, wide]
        + [pl.BlockSpec(memory_space=pl.ANY)] * (after is not None),
        out_specs=[wide, wide], out_shape=[jax.ShapeDtypeStruct((t, FF_PAD), BF16)] * 2,
        compiler_params=_params(("parallel", "parallel")),
    )(dxo, wd, dadg, dadu, *([after] if after is not None else []))


def _ffn_in_dx(dg, du, wg_t, wu_t, x, nw, dxo, after=None):
    t, d = x.shape
    tm = _tile(t, (256, 128))

    def body(dg_ref, du_ref, wg_ref, wu_ref, x_ref, nw_ref, dr_ref, *rest):
        dx_ref, dxb_ref, dw_ref = rest[-3:]

        @pl.when(pl.program_id(0) == 0)
        def _():
            dw_ref[...] = jnp.zeros_like(dw_ref)

        dh = _dot(dg_ref[...], wg_ref[...]) + _dot(du_ref[...], wu_ref[...])
        dx, dw = _rms_bwd_math(x_ref[...], nw_ref[...], dh)
        dx = dr_ref[...] + dx
        dx_ref[...] = dx
        dxb_ref[...] = dx.astype(dxb_ref.dtype)
        dw_ref[...] += dw

    wide = pl.BlockSpec((tm, FF_PAD), lambda i: (i, 0))
    wsp = pl.BlockSpec((FF_PAD, d), lambda i: (0, 0))
    tok = pl.BlockSpec((tm, d), lambda i: (i, 0))
    vec = pl.BlockSpec((1, d), lambda i: (0, 0))
    return pl.pallas_call(
        body, name="ffn_in_dx", grid=(t // tm,),
        in_specs=[wide, wide, wsp, wsp, tok, vec, tok] + [pl.BlockSpec(memory_space=pl.ANY)] * (after is not None),
        out_specs=[tok, tok, vec],
        out_shape=[jax.ShapeDtypeStruct((t, d), F32), jax.ShapeDtypeStruct((t, d), BF16), jax.ShapeDtypeStruct((1, d), F32)],
        compiler_params=_params(("arbitrary",)),
    )(dg, du, wg_t, wu_t, x, nw, dxo, *([after] if after is not None else []))


def _adamw(w, g, m, v):
    r, c = w.shape
    tr = _tile(r, (512, 256, 128, 64, 32, 16, 8))
    c1 = 1.0 - ADAM_B1 ** ADAM_STEP
    c2 = 1.0 - ADAM_B2 ** ADAM_STEP

    def body(w_ref, g_ref, m_ref, v_ref, d_ref, nm_ref, nv_ref):
        gv = g_ref[...]
        nm = ADAM_B1 * m_ref[...] + (1.0 - ADAM_B1) * gv
        nv = ADAM_B2 * v_ref[...] + (1.0 - ADAM_B2) * (gv * gv)
        d_ref[...] = -ADAM_LR * ((nm / c1) / (jnp.sqrt(nv / c2) + ADAM_EPS) + ADAM_WD * w_ref[...])
        nm_ref[...] = nm
        nv_ref[...] = nv

    blk = pl.BlockSpec((tr, c), lambda i: (i, 0))
    return pl.pallas_call(
        body, name="adamw", grid=(r // tr,), in_specs=[blk] * 4, out_specs=[blk] * 3,
        out_shape=[jax.ShapeDtypeStruct((r, c), F32)] * 3, compiler_params=_params(("parallel",)),
    )(w, g, m, v)


N_KINDS = 3


def _ffn_slot(which, kind):
    return which * N_KINDS + kind


def _ffn_fwd(x, nw, w_ffn, which):
    wg, wu, wd = (w_ffn[_ffn_slot(which, kind)] for kind in range(3))
    h, dadg, dadu, a = _ffn_in(x, nw, wg, wu)
    return _mm(a, wd, name="ffn_out", res=x, scale=0.5, token_tiles=_TOKEN_TILES[1:]), (x, h, dadg, dadu, a)


def _ffn_bwd(dxo, res, nw, w_ffn, which, after=None, done=None):
    x, h, dadg, dadu, a = res
    dxo, dxo_b = dxo
    wg, wu, wd = (w_ffn[_ffn_slot(which, kind)] for kind in range(3))
    dg, du = _ffn_out_dx(dxo_b, wd, dadg, dadu, after)
    g_ffn = lax.empty((N_KINDS,) + wd.shape, BF16)
    g_ffn = _mm_tn(a, dxo_b, name="ffn_out_dw", scale=0.5, into=g_ffn, slot=2)
    g_ffn = _mm_tn(dg, h, name="ffn_in_dw", into=g_ffn, slot=0)
    g_ffn = _mm_tn(du, h, name="ffn_in_dw", into=g_ffn, slot=1)
    token = done(g_ffn) if done is not None else None
    dx, dx_b, dnw = _ffn_in_dx(dg, du, wg, wu, x, nw, dxo, token)
    return (dx, dx_b), dnw, g_ffn, token


def _hg_lower_bounds(logits):
    p = jax.nn.softmax(logits, axis=0)
    return jnp.cumsum(p, axis=0) - p[0]


def _even_fwd(x1, p, nb, l):
    t = nb * l
    h = _rms_fwd(x1, p["mix_norm"])
    proj = _mm(h, p["w_in_t"], name="ev_in", nt=True)
    proj3 = proj.reshape(nb, l, EV_IN)
    ya, hst = _hgrn2_fwd(proj3, p["lb"], p["hg_nw"])
    u = proj[:, 4 * HG_QK:]
    bu = _mm(u, p["bbd"], name="s5_in", out_dtype=BF16)
    hs = _s5_scan_fwd(bu.reshape(nb, l, 2, S5_ROWS, LANES), p["abar"])
    yc = _mm(hs.reshape(t, 2 * S5_NSTATE), p["ccd"], name="s5_out")
    yb, y0 = _s5_post_fwd(yc, u, p["s5_d"], p["w_glu"])
    ycat = jnp.concatenate([ya.reshape(t, HG_QK), yb], axis=1)
    x2 = _mm(ycat, p["w_out"], name="mix_out", res=x1)
    return x2, (x1, h, proj3, hst, u, hs, y0, ycat)


def _even_bwd(dx2, res, p, nb, l):
    x1, h, proj3, hst, u, hs, y0, ycat = res
    t = nb * l
    dx2, dx2_b = dx2
    dycat = _mm(dx2_b, p["w_out"], name="mix_out_dx", nt=True)
    dw_out = _mm_tn(ycat, dx2_b, name="mix_out_dw", out_dtype=BF16)
    dp_hg, dlb, dhg_nw = _hgrn2_bwd(proj3, p["lb"], p["hg_nw"], hst, dycat.reshape(nb, l, D_MODEL))
    dy0, y, dz, du_d, dd = _s5_post_bwd(y0, u, p["s5_d"], p["w_glu"], dycat[:, HG_QK:])
    dw_glu = _mm_tn(y, dz, name="s5_glu_dw", out_dtype=BF16)
    dhs = _mm(dy0, p["ccd"], name="s5_out_dx", nt=True, out_dtype=BF16)
    dccd = _mm_tn(hs.reshape(t, 2 * S5_NSTATE), dy0, name="s5_out_dw")
    g5, dabar = _s5_scan_bwd(dhs.reshape(nb, l, 2, S5_ROWS, LANES), hs, p["abar"])
    g2 = g5.reshape(t, 2 * S5_NSTATE)
    du = _mm(g2, p["bbd"], name="s5_in_dx", nt=True, res=du_d, out_dtype=BF16)
    dbbd = _mm_tn(u, g2, name="s5_in_dw")
    dproj = jnp.concatenate([dp_hg.reshape(t, 4 * HG_QK), du], axis=1)
    dw_in_t = _mm_tn(dproj, h, name="ev_in_dw", out_dtype=BF16)
    dh = _mm(dproj, p["w_in_t"], name="ev_in_dx")
    dx1, dx1_b, dmix = _rms_bwd(x1, p["mix_norm"], dh, dx2)
    small = dict(mix_norm=dmix, lb=dlb.sum(0).reshape(HG_QK), hg_nw=dhg_nw.reshape(HEAD_DIM),
                 abar=dabar.sum(0), bbd=dbbd, ccd=dccd, s5_d=dd.reshape(S5_WIDTH))
    return (dx1, dx1_b), dict(w_in_t=dw_in_t, w_glu=dw_glu, w_out=dw_out), small


def _odd_fwd(x1, p, nb, l):
    t = nb * l
    h = _rms_fwd(x1, p["mix_norm"])
    proj3 = _mm(h, p["w_in"], name="od_in").reshape(nb, l, OD_IN_PAD)
    qkv = _gdn_pre_fwd(proj3, p["conv_w"])
    y, st = _gdn_fwd(qkv, proj3, p["a_log"], p["dt_bias"], p["gdn_nw"])
    y2 = y.reshape(t, D_MODEL)
    x2 = _mm(y2, p["w_out"], name="mix_out", res=x1)
    return x2, (x1, h, proj3, qkv, st, y2)


def _odd_bwd(dx2, res, p, nb, l):
    x1, h, proj3, qkv, st, y2 = res
    t = nb * l
    dx2, dx2_b = dx2
    dy = _mm(dx2_b, p["w_out"], name="mix_out_dx", nt=True)
    dw_out = _mm_tn(y2, dx2_b, name="mix_out_dw", out_dtype=BF16)
    dqkv, dgate, dsmall, dalog, ddtb, dnw = _gdn_bwd(
        qkv, proj3, p["a_log"], p["dt_bias"], p["gdn_nw"], st, dy.reshape(nb, l, D_MODEL))
    dx_qkv, dcw = _gdn_pre_bwd(proj3, p["conv_w"], dqkv)
    dproj = jnp.concatenate([dx_qkv, dgate, dsmall], axis=-1).reshape(t, OD_IN_PAD)
    dw_in = _mm_tn(h, dproj, name="od_in_dw", out_dtype=BF16)[:, :OD_IN]
    dh = _mm(dproj, p["w_in"], name="od_in_dx", nt=True)
    dx1, dx1_b, dmix = _rms_bwd(x1, p["mix_norm"], dh, dx2)
    small = dict(mix_norm=dmix, a_log=dalog.sum(0).reshape(GDN_HEADS), dt_bias=ddtb.sum(0).reshape(GDN_HEADS),
                 gdn_nw=dnw.reshape(HEAD_DIM))
    return (dx1, dx1_b), dict(w_in=dw_in, conv_w=dcw.sum(0), w_out=dw_out), small


def _local_step(x, target, layer_weights, small, layer_done):
    nb, l, _ = x.shape
    t = nb * l
    lbs, lbs_vjp = jax.vjp(_hg_lower_bounds, small["hg_lb_logits"])
    row = lambda v: v.reshape(1, -1)
    s5_vjps = {}

    def layer_params(layer, big):
        j = layer // 2
        p = dict(mix_norm=row(small["mix_norm"][layer]), w_out=big["w_out"])
        if layer % 2 == 0:
            (abar, bbd, ccd), s5_vjps[j] = jax.vjp(
                _s5_params, small["s5_a_re"][j], small["s5_a_im"][j], small["s5_b_re"][j], small["s5_b_im"][j],
                small["s5_c_re"][j], small["s5_c_im"][j], small["s5_log_dt"][j])
            p.update(w_in_t=big["w_in_t"], w_glu=big["w_glu"],
                     lb=lbs[j].reshape(HG_HEADS, 1, HEAD_DIM), hg_nw=row(small["hg_norm_w"][j]),
                     abar=abar, bbd=bbd.astype(BF16), ccd=ccd.astype(BF16), s5_d=row(small["s5_d"][j]))
        else:
            p.update(w_in=big["w_in"], conv_w=big["conv_w"],
                     a_log=small["gdn_a_log"][j].reshape(GDN_HEADS, 1, 1),
                     dt_bias=small["gdn_dt_bias"][j].reshape(GDN_HEADS, 1, 1), gdn_nw=row(small["gdn_norm_w"][j]))
        return p

    xs = x.reshape(t, D_MODEL)
    saved, layers, ffn_w = [], [], []
    for layer in range(DEPTH):
        big = layer_weights(layer, xs)
        p = layer_params(layer, big)
        xs, r1 = _ffn_fwd(xs, row(small["ffn1_norm"][layer]), big["ffn"], 0)
        xs, r2 = (_even_fwd if layer % 2 == 0 else _odd_fwd)(xs, p, nb, l)
        xs, r3 = _ffn_fwd(xs, row(small["ffn2_norm"][layer]), big["ffn"], 1)
        saved.append((r1, r2, r3))
        layers.append(p)
        ffn_w.append(big["ffn"])
    loss, dx_f, dx_b, dfinal = _loss_head(xs, row(small["final_norm"]), target.reshape(t, D_MODEL))
    dx = (dx_f, dx_b)

    gs = {k: [None] * DEPTH for k in ("ffn1_norm", "mix_norm", "ffn2_norm")}
    gs.update({k: [None] * 2 for k in ("hg_norm_w", "s5_a_re", "s5_a_im", "s5_b_re", "s5_b_im", "s5_c_re", "s5_c_im",
                                       "s5_d", "s5_log_dt", "gdn_a_log", "gdn_dt_bias", "gdn_norm_w")})
    dlbs = [None] * 2
    token = None
    for layer in reversed(range(DEPTH)):
        j = layer // 2
        p = layers[layer]
        r1, r2, r3 = saved[layer]
        dx, dnw, g_ffn, _ = _ffn_bwd(dx, r3, row(small["ffn2_norm"][layer]), ffn_w[layer], 1, token)
        gs["ffn2_norm"][layer] = dnw[0]
        if layer % 2 == 0:
            dx, gw, sm = _even_bwd(dx, r2, p, nb, l)
            dlbs[j] = sm["lb"]
            gs["hg_norm_w"][j] = sm["hg_nw"]
            gs["s5_d"][j] = sm["s5_d"]
            (gs["s5_a_re"][j], gs["s5_a_im"][j], gs["s5_b_re"][j], gs["s5_b_im"][j], gs["s5_c_re"][j],
             gs["s5_c_im"][j], gs["s5_log_dt"][j]) = s5_vjps[j]((sm["abar"], sm["bbd"], sm["ccd"]))
        else:
            dx, gw, sm = _odd_bwd(dx, r2, p, nb, l)
            gs["gdn_a_log"][j], gs["gdn_dt_bias"][j], gs["gdn_norm_w"][j] = sm["a_log"], sm["dt_bias"], sm["gdn_nw"]
        gs["mix_norm"][layer] = sm["mix_norm"][0]
        token = layer_done(layer, 0, dict(gw, ffn=g_ffn))
        dx, dnw, _, token = _ffn_bwd(dx, r1, row(small["ffn1_norm"][layer]), ffn_w[layer], 0, token,
                                     done=lambda g, layer=layer: layer_done(layer, 1, dict(ffn=g)))
        gs["ffn1_norm"][layer] = dnw[0]
    gsmall = {k: jnp.stack(v) for k, v in gs.items()}
    gsmall["hg_lb_logits"] = lbs_vjp(jnp.stack(dlbs))[0]
    gsmall["final_norm"] = dfinal[0]
    return loss[0, 0], dx[0].reshape(nb, l, D_MODEL), gsmall


def _here():
    return lax.axis_index("x"), lax.axis_index("y"), lax.axis_index("c")


def _other_chips(x, y):
    return [(1 - x, y), (x, 1 - y), (1 - x, 1 - y)]


def _all_gather(blocks, name):
    n = len(blocks)

    def body(*refs):
        x_refs, o_refs = refs[:n], refs[n:2 * n]
        send_sems, recv_sems, local_sems = refs[2 * n:]
        x, y, cc = _here()
        me, sibling = (x, y, cc), (x, y, 1 - cc)
        chips = _other_chips(x, y)

        def win(i, p):
            return o_refs[i].at[:, 4 * p[0] + 2 * p[1] + p[2]]

        def copy(i, k, blk, to, src=None):
            return pltpu.make_async_remote_copy(
                src_ref=win(i, blk) if src is None else src, dst_ref=win(i, blk),
                send_sem=send_sems.at[i, k], recv_sem=recv_sems.at[i, k], device_id=to, device_id_type=MESH)

        mine = [pltpu.make_async_copy(x_refs[i], win(i, me), local_sems.at[i]) for i in range(n)]
        for cp in mine:
            cp.start()
        first = []
        for i in range(n):
            first.append(copy(i, 0, me, sibling, src=x_refs[i]))
            first += [copy(i, 1 + j, me, (*chip, cc), src=x_refs[i]) for j, chip in enumerate(chips)]
        for cp in first:
            cp.start()
        passed = []
        for j, chip in enumerate(chips):
            for i in range(n):
                copy(i, 1 + j, (*chip, cc), me).wait_recv()
                fwd = copy(i, 4 + j, (*chip, cc), sibling)
                fwd.start()
                passed.append(fwd)
        for i in range(n):
            copy(i, 0, sibling, me).wait_recv()
            for j, chip in enumerate(chips):
                copy(i, 4 + j, (*chip, 1 - cc), me).wait_recv()
        for cp in first + passed:
            cp.wait_send()
        for cp in mine:
            cp.wait()

    hbm = pl.BlockSpec(memory_space=pl.ANY)
    return pl.pallas_call(
        body, name=name, in_specs=[hbm] * n, out_specs=[hbm] * n,
        out_shape=[jax.ShapeDtypeStruct((b.shape[0], N_DEV) + b.shape[1:], b.dtype) for b in blocks],
        scratch_shapes=[pltpu.SemaphoreType.DMA((n, 7)), pltpu.SemaphoreType.DMA((n, 7)), pltpu.SemaphoreType.DMA((n,))],
    )(*blocks)


_HBM = pl.BlockSpec(memory_space=pltpu.HBM)
_SEM = pl.BlockSpec(memory_space=pltpu.SEMAPHORE)
_DATAFLOW = pltpu.SideEffectType.DATAFLOW_SIDE_EFFECTING


def _peer(k):
    x, y, c = _here()
    return (x ^ ((k >> 2) & 1), y ^ ((k >> 1) & 1), c ^ (k & 1))


def _device_of(p):
    return 4 * p[0] + 2 * p[1] + p[2]


_OTHER_CHIPS = (4, 2, 6)


def _exchange_start(srcs, lands, copies, name, after=None):
    n = len(srcs)
    n_in = 2 * n + (after is not None)

    def body(*refs):
        s_refs, l_refs, sems = refs[:n], refs[n:2 * n], refs[n_in:n_in + 2 * n]
        for i in range(n):
            for slot, k in enumerate(_OTHER_CHIPS):
                src, dst = copies(s_refs[i], l_refs[i], k, slot)
                pltpu.make_async_remote_copy(
                    src_ref=src, dst_ref=dst, send_sem=sems[2 * i], recv_sem=sems[2 * i + 1],
                    device_id=_peer(k), device_id_type=MESH).start()
        refs[-1][...] = jnp.zeros_like(refs[-1])

    both = list(srcs) + list(lands)
    outs = pl.pallas_call(
        body, name=name,
        out_shape=tuple([pltpu.SemaphoreType.DMA(())] * (2 * n)) + tuple(pltpu.HBM(a.shape, a.dtype) for a in both)
        + (jax.ShapeDtypeStruct((8, LANES), F32),),
        in_specs=[_HBM] * (2 * n) + [pl.BlockSpec(memory_space=pl.ANY)] * (after is not None),
        out_specs=tuple([_SEM] * (2 * n)) + tuple([_HBM] * (2 * n)) + (pl.BlockSpec(memory_space=pltpu.VMEM),),
        input_output_aliases={i: 2 * n + i for i in range(2 * n)},
        compiler_params=pltpu.CompilerParams(has_side_effects=_DATAFLOW),
    )(*[pltpu.with_memory_space_constraint(a, pltpu.HBM) for a in both], *([after] if after is not None else []))
    return list(outs[:2 * n]), list(outs[2 * n:3 * n]), list(outs[3 * n:4 * n]), outs[4 * n]


def _exchange_wait(handle, three, after, name):
    sems, srcs, lands, _ = handle
    n = len(srcs)

    def body(*refs):
        l_refs, sem_refs = refs[n:2 * n], refs[2 * n:4 * n]
        for i in range(n):
            all_three = three(l_refs[i])
            cp = pltpu.make_async_remote_copy(
                src_ref=all_three, dst_ref=all_three, send_sem=sem_refs[2 * i], recv_sem=sem_refs[2 * i + 1],
                device_id=_peer(_OTHER_CHIPS[0]), device_id_type=MESH)
            cp.wait_send()
            cp.wait_recv()

    both = list(srcs) + list(lands)
    outs = pl.pallas_call(
        body, name=name, out_shape=tuple(pltpu.HBM(a.shape, a.dtype) for a in both),
        in_specs=[_HBM] * (2 * n) + [_SEM] * (2 * n) + [pl.BlockSpec(memory_space=pl.ANY)],
        out_specs=tuple([_HBM] * (2 * n)), input_output_aliases={i: i for i in range(2 * n)},
        compiler_params=pltpu.CompilerParams(has_side_effects=_DATAFLOW),
    )(*both, *sems, after)
    return list(outs[:n]), list(outs[n:])


def _gather_copies(src, land, k, slot):
    del k, slot
    return src, land.at[:, _device_of(_here())]


def _gather_three(land):
    return land.at[:, pl.ds(0, len(_OTHER_CHIPS))]


def _scatter_copies(src, land, k, slot):
    p = _peer(k)
    return src.at[2 * p[0] + p[1]], land.at[slot]


def _scatter_three(land):
    return land


def _core_forward(lands):
    n = len(lands)

    def body(*refs):
        l_refs = refs[:n]
        send_sems, recv_sems = refs[2 * n:]
        x, y, cc = _here()
        mine = [(x, y)] + [_peer(k)[:2] for k in _OTHER_CHIPS]
        cps = []
        for i in range(n):
            for j, (px, py) in enumerate(mine):
                give = l_refs[i].at[:, _device_of((px, py, cc))]
                take = l_refs[i].at[:, _device_of((px, py, 1 - cc))]
                cps.append((pltpu.make_async_remote_copy(
                    src_ref=give, dst_ref=give, send_sem=send_sems.at[i, j], recv_sem=recv_sems.at[i, j],
                    device_id=(x, y, 1 - cc), device_id_type=MESH), take))
        for cp, _ in cps:
            cp.start()
        for i in range(n):
            for j, (cp, take) in enumerate(cps[4 * i:4 * i + 4]):
                cp.wait_send()
                pltpu.make_async_remote_copy(
                    src_ref=take, dst_ref=take, send_sem=send_sems.at[i, j], recv_sem=recv_sems.at[i, j],
                    device_id=(x, y, 1 - cc), device_id_type=MESH).wait_recv()

    hbm = pl.BlockSpec(memory_space=pl.ANY)
    return pl.pallas_call(
        body, name="gather_core_forward", in_specs=[hbm] * n, out_specs=[hbm] * n,
        out_shape=[jax.ShapeDtypeStruct(a.shape, a.dtype) for a in lands],
        input_output_aliases={i: i for i in range(n)},
        scratch_shapes=[pltpu.SemaphoreType.DMA((n, 4)), pltpu.SemaphoreType.DMA((n, 4))],
    )(*lands)


def _rs_core_exchange(gs):
    n = len(gs)

    def body(*refs):
        g_refs, o_refs = refs[:n], refs[n:2 * n]
        send_sems, recv_sems = refs[2 * n:]
        x, y, cc = _here()
        cps = [pltpu.make_async_remote_copy(
            src_ref=g_refs[i].at[:, 2 * j + (1 - cc)], dst_ref=o_refs[i].at[j],
            send_sem=send_sems.at[i, j], recv_sem=recv_sems.at[i, j], device_id=(x, y, 1 - cc), device_id_type=MESH)
            for i in range(n) for j in range(4)]
        for cp in cps:
            cp.start()
        for cp in cps:
            cp.wait()

    hbm = pl.BlockSpec(memory_space=pl.ANY)
    return pl.pallas_call(
        body, name="rs_core_exchange", in_specs=[hbm] * n, out_specs=[hbm] * n,
        out_shape=[jax.ShapeDtypeStruct((4, g.shape[0]) + g.shape[2:], g.dtype) for g in gs],
        scratch_shapes=[pltpu.SemaphoreType.DMA((n, 4)), pltpu.SemaphoreType.DMA((n, 4))],
    )(*gs)


def _row_tile(w):
    return w if w <= 512 else _tile(w, (512, 256, 128))


def _rs_chip_sum(g, r1):
    a, _, w, c = g.shape
    tr = _row_tile(w)

    def body(g_ref, r_ref, o_ref):
        o_ref[...] = (g_ref[...].astype(F32) + r_ref[...].astype(F32)).astype(o_ref.dtype)

    blk = pl.BlockSpec((None, None, tr, c), lambda j, aa, i: (j, aa, i, 0))
    return pl.pallas_call(
        body, name="rs_chip_sum", grid=(4, a, w // tr),
        in_specs=[pl.BlockSpec((None, None, tr, c), lambda j, aa, i: (aa, 2 * j + lax.axis_index("c"), i, 0)), blk],
        out_specs=blk, out_shape=jax.ShapeDtypeStruct((4, a, w, c), g.dtype),
        compiler_params=_params(("parallel", "parallel", "parallel")),
    )(g, r1)


def _rs_final_sum(g, r1, r2):
    a, _, w, c = g.shape
    tr = _row_tile(w)

    def body(g_ref, r1_ref, r2_ref, o_ref):
        s = g_ref[...].astype(F32) + r1_ref[...].astype(F32)
        for k in range(3):
            s = s + r2_ref[k].astype(F32)
        o_ref[...] = s

    def device():
        return 4 * lax.axis_index("x") + 2 * lax.axis_index("y") + lax.axis_index("c")

    def chip():
        return 2 * lax.axis_index("x") + lax.axis_index("y")

    return pl.pallas_call(
        body, name="rs_final_sum", grid=(a, w // tr),
        in_specs=[pl.BlockSpec((None, None, tr, c), lambda aa, i: (aa, device(), i, 0)),
                  pl.BlockSpec((None, None, tr, c), lambda aa, i: (chip(), aa, i, 0)),
                  pl.BlockSpec((3, None, tr, c), lambda aa, i: (0, aa, i, 0))],
        out_specs=pl.BlockSpec((None, tr, c), lambda aa, i: (aa, i, 0)),
        out_shape=jax.ShapeDtypeStruct((a, w, c), F32),
        compiler_params=_params(("parallel", "parallel")),
    )(g, r1, r2)


def _sum_slots(g):
    _, n, r, c = g.shape

    def body(g_ref, o_ref):
        s = g_ref[0, 0]
        for k in range(1, n):
            s = s + g_ref[0, k]
        o_ref[...] = s

    return pl.pallas_call(
        body, name="sum_slots", grid=(1,),
        in_specs=[pl.BlockSpec((1, n, r, c), lambda i: (0, 0, 0, 0))], out_specs=pl.BlockSpec((r, c), lambda i: (0, 0)),
        out_shape=jax.ShapeDtypeStruct((r, c), F32), compiler_params=_params(("arbitrary",)),
    )(g)


_WEIGHTS = ['ffn1_norm', 'ffn1_w_gate', 'ffn1_w_up', 'ffn1_w_down', 'mix_norm', 'ffn2_norm', 'ffn2_w_gate', 'ffn2_w_up',
            'ffn2_w_down', 'ev_w_in', 'hg_lb_logits', 'hg_norm_w', 's5_a_re', 's5_a_im', 's5_b_re', 's5_b_im', 's5_c_re',
            's5_c_im', 's5_d', 's5_log_dt', 's5_w_glu', 'ev_w_out', 'od_w_in', 'gdn_conv_w', 'gdn_a_log', 'gdn_dt_bias',
            'gdn_norm_w', 'od_w_out', 'final_norm']
_SHARDED = ['ffn1_w_gate', 'ffn1_w_up', 'ffn1_w_down', 'ffn2_w_gate', 'ffn2_w_up', 'ffn2_w_down', 'ev_w_in', 's5_w_glu',
            'ev_w_out', 'od_w_in', 'gdn_conv_w', 'od_w_out']
_REPLICATED = [n for n in _WEIGHTS if n not in _SHARDED]
_SMALL_COLS = 1024


def _pad_rows(a, rows):
    return jnp.pad(a, ((0, rows - a.shape[0]), (0, 0)))


def _layer_blocks(w, layer):
    j = layer // 2
    ffn = []
    for f in ('ffn1', 'ffn2'):
        ffn += [_pad_rows(w[f + '_w_gate'][layer].T, FF_SHARD_PAD), _pad_rows(w[f + '_w_up'][layer].T, FF_SHARD_PAD),
                _pad_rows(w[f + '_w_down'][layer], FF_SHARD_PAD)]
    ffn = jnp.stack(ffn).astype(BF16)
    if layer % 2 == 0:
        return [ffn, w['ev_w_in'][j].T.astype(BF16)[None], w['s5_w_glu'][j].astype(BF16)[None],
                w['ev_w_out'][j].astype(BF16)[None]]
    return [ffn, w['od_w_in'][j].astype(BF16)[None], w['gdn_conv_w'][j][None], w['od_w_out'][j].astype(BF16)[None]]


def _layer_whole(gathered, layer):
    whole = lambda g: g.reshape(g.shape[0], N_DEV * g.shape[2], g.shape[3])
    ffn, a1, a2, w_out = gathered
    if layer % 2 == 0:
        return dict(ffn=whole(ffn), w_in_t=whole(a1)[0], w_glu=whole(a2)[0], w_out=whole(w_out)[0])
    w_in = jnp.moveaxis(a1[0], 0, 1).reshape(D_MODEL, OD_IN)
    conv = jnp.moveaxis(a2[0], 0, 1).reshape(CONV_W, GDN_QKV)
    return dict(ffn=whole(ffn), w_in=jnp.pad(w_in, ((0, 0), (0, OD_IN_PAD - OD_IN))), conv_w=conv, w_out=whole(w_out)[0])


def _layer_grad_windows(g, layer):
    windows = lambda a, w: a.reshape(-1, N_DEV, w, a.shape[-1])
    ffn = windows(g['ffn'], FF_SHARD_PAD)
    if 'w_out' not in g:
        return [ffn]
    w_out = windows(g['w_out'], D_MODEL // N_DEV)
    if layer % 2 == 0:
        return [ffn, windows(g['w_in_t'], EV_IN // N_DEV), windows(g['w_glu'], S5_WIDTH // N_DEV), w_out]
    w_in = jnp.moveaxis(g['w_in'].reshape(D_MODEL, N_DEV, OD_IN // N_DEV), 1, 0)[None]
    conv = jnp.moveaxis(g['conv_w'].reshape(CONV_W, N_DEV, GDN_QKV // N_DEV), 1, 0)[None]
    return [ffn, w_in, conv, w_out]


def _flat_small(src, prefix=''):
    flat = jnp.concatenate([src[prefix + n].reshape(-1) for n in _REPLICATED])
    rows = -(-(flat.shape[0] + 1) // (_SMALL_COLS * 8)) * 8
    return jnp.pad(flat, (0, rows * _SMALL_COLS - flat.shape[0])).reshape(rows, _SMALL_COLS)


def _split_small(flat2d, like):
    flat, out, off = flat2d.reshape(-1), {}, 0
    for n in _REPLICATED:
        size = math.prod(like[n].shape)
        out[n] = flat[off:off + size].reshape(like[n].shape)
        off += size
    return out, flat[off]


def _all_reduce_small(gsmall, loss):
    flat = _flat_small(gsmall)
    n_used = sum(math.prod(gsmall[n].shape) for n in _REPLICATED)
    flat = flat.reshape(-1).at[n_used].set(loss).reshape(flat.shape)
    (gathered,) = _all_gather([flat[None]], "gather_small")
    return _split_small(_sum_slots(gathered), gsmall)


def _as_2d(a):
    return a.reshape(-1, a.shape[-1])


def kernel(x, ffn1_norm, ffn1_w_gate, ffn1_w_up, ffn1_w_down, mix_norm, ffn2_norm, ffn2_w_gate, ffn2_w_up, ffn2_w_down, ev_w_in, hg_lb_logits, hg_norm_w, s5_a_re, s5_a_im, s5_b_re, s5_b_im, s5_c_re, s5_c_im, s5_d, s5_log_dt, s5_w_glu, ev_w_out, od_w_in, gdn_conv_w, gdn_a_log, gdn_dt_bias, gdn_norm_w, od_w_out, final_norm, loss_target, m_ffn1_norm, m_ffn1_w_gate, m_ffn1_w_up, m_ffn1_w_down, m_mix_norm, m_ffn2_norm, m_ffn2_w_gate, m_ffn2_w_up, m_ffn2_w_down, m_ev_w_in, m_hg_lb_logits, m_hg_norm_w, m_s5_a_re, m_s5_a_im, m_s5_b_re, m_s5_b_im, m_s5_c_re, m_s5_c_im, m_s5_d, m_s5_log_dt, m_s5_w_glu, m_ev_w_out, m_od_w_in, m_gdn_conv_w, m_gdn_a_log, m_gdn_dt_bias, m_gdn_norm_w, m_od_w_out, m_final_norm, v_ffn1_norm, v_ffn1_w_gate, v_ffn1_w_up, v_ffn1_w_down, v_mix_norm, v_ffn2_norm, v_ffn2_w_gate, v_ffn2_w_up, v_ffn2_w_down, v_ev_w_in, v_hg_lb_logits, v_hg_norm_w, v_s5_a_re, v_s5_a_im, v_s5_b_re, v_s5_b_im, v_s5_c_re, v_s5_c_im, v_s5_d, v_s5_log_dt, v_s5_w_glu, v_ev_w_out, v_od_w_in, v_gdn_conv_w, v_gdn_a_log, v_gdn_dt_bias, v_gdn_norm_w, v_od_w_out, v_final_norm):
    given = dict(locals())
    w = {n: given[n] for n in _WEIGHTS}
    small = {n: w[n] for n in _REPLICATED}
    me = _device_of(_here())

    blocks = [_layer_blocks(w, layer) for layer in range(DEPTH)]
    first = _all_gather(blocks[0], "gather_weights")
    pending = {}

    def start_gather(layer, after=None):
        lands = [lax.dynamic_update_slice(lax.empty((b.shape[0], N_DEV) + b.shape[1:], b.dtype), b[:, None], (0, me, 0, 0))
                 for b in blocks[layer]]
        pending[layer] = _exchange_start(blocks[layer], lands, _gather_copies, "gather_start_%d" % layer, after)

    start_gather(1)

    def layer_weights(layer, after):
        if layer == 0:
            return _layer_whole(first, 0)
        got = _exchange_wait(pending[layer], _gather_three, after, "gather_wait_%d" % layer)[1]
        if layer + 1 < DEPTH:
            start_gather(layer + 1, after)
        return _layer_whole(_core_forward(got), layer)

    sent = {}

    def layer_done(layer, part, grads):
        gs = _layer_grad_windows(grads, layer)
        from_sibling = _rs_core_exchange(gs)
        chip = [_rs_chip_sum(g, r) for g, r in zip(gs, from_sibling)]
        lands = [lax.empty((len(_OTHER_CHIPS),) + p.shape[1:], p.dtype) for p in chip]
        handle = _exchange_start(chip, lands, _scatter_copies, "scatter_start_%d_%d" % (layer, part))
        sent[layer, part] = (gs, from_sibling, handle)
        return handle[3]

    loss_part, grad_x, gsmall = _local_step(given['x'], given['loss_target'], layer_weights, small, layer_done)
    gsum, loss = _all_reduce_small(gsmall, loss_part)
    summed = {}
    for layer in reversed(range(DEPTH)):
        for part in range(2):
            gs, from_sibling, handle = sent[layer, part]
            _, lands = _exchange_wait(handle, _scatter_three, grad_x, "scatter_wait_%d_%d" % (layer, part))
            summed[layer, part] = [_rs_final_sum(g, r, land) for g, r, land in zip(gs, from_sibling, lands)]

    grads = dict(gsum)
    t_ = lambda a: jnp.swapaxes(a, 1, 2)
    for i, f in enumerate(('ffn1', 'ffn2')):
        ffn = jnp.stack([summed[layer, 1 - i][0][:, :FF_SHARD] for layer in range(DEPTH)], axis=1)
        grads[f + '_w_gate'], grads[f + '_w_up'], grads[f + '_w_down'] = t_(ffn[0]), t_(ffn[1]), ffn[2]
    even, odd = [summed[0, 0], summed[2, 0]], [summed[1, 0], summed[3, 0]]
    grads['ev_w_in'] = jnp.stack([r[1][0].T for r in even])
    grads['s5_w_glu'] = jnp.stack([r[2][0] for r in even])
    grads['ev_w_out'] = jnp.stack([r[3][0] for r in even])
    grads['od_w_in'] = jnp.stack([r[1][0] for r in odd])
    grads['gdn_conv_w'] = jnp.stack([r[2][0] for r in odd])
    grads['od_w_out'] = jnp.stack([r[3][0] for r in odd])

    delta, new_m, new_v = {}, {}, {}
    for n in _SHARDED:
        d, nm, nv = _adamw(_as_2d(w[n]), _as_2d(grads[n]), _as_2d(given['m_' + n]), _as_2d(given['v_' + n]))
        delta[n], new_m[n], new_v[n] = (a.reshape(w[n].shape) for a in (d, nm, nv))
    d, nm, nv = _adamw(_flat_small(w), _flat_small(grads), _flat_small(given, 'm_'), _flat_small(given, 'v_'))
    (delta_s, _), (new_m_s, _), (new_v_s, _) = (_split_small(a, small) for a in (d, nm, nv))
    delta.update(delta_s)
    new_m.update(new_m_s)
    new_v.update(new_v_s)
    return (loss, grad_x, *[grads[n] for n in _WEIGHTS], *[delta[n] for n in _WEIGHTS],
            *[new_m[n] for n in _WEIGHTS], *[new_v[n] for n in _WEIGHTS])
```
